```python
import functools
import jax, jax.numpy as jnp
from jax import lax
import numpy as np

D_MODEL = 1024
BATCH = 4
SEQ = 8192
DEPTH = 1
DEC_BATCH = 128
DEC_SEQ = 8
PAST_LEN = 16384
PAGE_SIZE = 128

HEAD_DIM = 64
A_HEADS = 8
A_KV_HEADS = 2
A_GROUP = A_HEADS // A_KV_HEADS
A_WIDTH = A_HEADS * HEAD_DIM
KV_WIDTH = A_KV_HEADS * HEAD_DIM
WINDOW = 128
BLOCK = 128

M_HEADS = 4
M_HEAD_DIM = 128
M_WIDTH = M_HEADS * M_HEAD_DIM
CONV_W = 4
CHUNK = 64

MIX_WIDTH = A_WIDTH + M_WIDTH
IN_WIDTH = A_WIDTH + 2 * KV_WIDTH + 4 * M_WIDTH + 2 * M_HEADS
SPLITS = (A_WIDTH, A_WIDTH + KV_WIDTH, A_WIDTH + 2 * KV_WIDTH,
          A_WIDTH + 2 * KV_WIDTH + 2 * M_WIDTH, A_WIDTH + 2 * KV_WIDTH + 3 * M_WIDTH,
          A_WIDTH + 2 * KV_WIDTH + 4 * M_WIDTH, A_WIDTH + 2 * KV_WIDTH + 4 * M_WIDTH + M_HEADS)

N_EXPERTS = 32
TOP_K = 4
D_FF = D_MODEL
SWIGLU_LIMIT = 7.0
SWIGLU_ALPHA = 1.702
MOE_BLOCK = 256

EPS = 1e-6
NEG = -1e30

kernel_name = 'hymba_swa_sink_mlstm_moe_step'


def rms_norm(x, g):
    xf = x.astype(jnp.float32)
    y = xf * lax.rsqrt(jnp.mean(xf * xf, axis=-1, keepdims=True) + EPS)
    return (y * g.astype(jnp.float32)).astype(x.dtype)


def sink_attention(q, k, v, mask, sinks):
    s = jnp.einsum('bnqhgd,bnkhd->bnhgqk', q, k, preferred_element_type=jnp.float32) * (HEAD_DIM ** -0.5)
    s = jnp.where(mask[None, :, None, None], s, NEG)
    sink = sinks.astype(jnp.float32).reshape(1, 1, A_KV_HEADS, A_GROUP, 1, 1)
    m = jnp.maximum(jnp.max(s, axis=-1, keepdims=True), sink)
    p = jnp.exp(s - m)
    p = p / (jnp.sum(p, axis=-1, keepdims=True) + jnp.exp(sink - m))
    return jnp.einsum('bnhgqk,bnkhd->bnqhgd', p.astype(v.dtype), v)


def attn_prompt(q, k, v, sinks):
    B, S = q.shape[:2]
    nb = S // BLOCK
    qb = q.reshape(B, nb, BLOCK, A_KV_HEADS, A_GROUP, HEAD_DIM)

    def band(t):
        tb = t.reshape(B, nb, BLOCK, A_KV_HEADS, HEAD_DIM)
        prev = jnp.pad(tb, ((0, 0), (1, 0), (0, 0), (0, 0), (0, 0)))[:, :-1]
        return jnp.concatenate([prev, tb], axis=2)

    blk = jnp.arange(nb)[:, None, None]
    qpos = blk * BLOCK + jnp.arange(BLOCK)[None, :, None]
    kpos = (blk - 1) * BLOCK + jnp.arange(2 * BLOCK)[None, None, :]
    d = qpos - kpos
    mask = (d >= 0) & (d <= WINDOW) & (kpos >= 0)
    o = sink_attention(qb, band(k), band(v), mask, sinks)
    return o.reshape(B, S, A_WIDTH), k[:, -WINDOW:], v[:, -WINDOW:]


def attn_sample(q, k, v, sinks, k_buf, v_buf):
    B, L = q.shape[:2]
    kk = jnp.concatenate([k_buf.astype(k.dtype), k], axis=1)
    vv = jnp.concatenate([v_buf.astype(v.dtype), v], axis=1)
    d = (jnp.arange(L)[:, None] + WINDOW) - jnp.arange(WINDOW + L)[None, :]
    mask = ((d >= 0) & (d <= WINDOW))[None]
    o = sink_attention(q.reshape(B, 1, L, A_KV_HEADS, A_GROUP, HEAD_DIM), kk[:, None], vv[:, None], mask, sinks)
    return o.reshape(B, L, A_WIDTH), kk[:, -WINDOW:], vv[:, -WINDOW:]


def mlstm_chunk(carry, xs):
    C0, n0, m0 = carry
    q, k, v, ig, lf = xs
    L = q.shape[2]
    b = jnp.cumsum(lf, axis=-1)
    dmat = b[..., :, None] - b[..., None, :] + ig[..., None, :]
    causal = jnp.tril(jnp.ones((L, L), dtype=bool))
    dmat = jnp.where(causal, dmat, NEG)
    a = b + m0[..., None]
    m = jnp.maximum(a, jnp.max(dmat, axis=-1))
    w = jnp.exp(dmat - m[..., None])
    sc = jnp.exp(a - m)
    wqk = w * jnp.einsum('bhtd,bhsd->bhts', q, k)
    num = jnp.einsum('bhts,bhsv->bhtv', wqk, v) + sc[..., None] * jnp.einsum('bhvd,bhtd->bhtv', C0, q)
    den = jnp.sum(wqk, axis=-1) + sc * jnp.einsum('bhd,bhtd->bht', n0, q)
    h = num / jnp.maximum(jnp.abs(den), jnp.exp(-m))[..., None]
    m_end = m[..., -1]
    w_end = jnp.exp(b[..., -1:] - b + ig - m_end[..., None])
    sc_end = jnp.exp(a[..., -1] - m_end)
    C = sc_end[..., None, None] * C0 + jnp.einsum('bhs,bhsv,bhsd->bhvd', w_end, v, k)
    n = sc_end[..., None] * n0 + jnp.einsum('bhs,bhsd->bhd', w_end, k)
    return (C, n, m_end), h


def mlstm_scan(q, k, v, ig, lf, C0, n0, m0, chunk):
    B, S = q.shape[:2]
    nc = S // chunk

    def to_chunks(t):
        t = t.astype(jnp.float32).reshape((B, nc, chunk) + t.shape[2:])
        return jnp.moveaxis(t, (1, 3), (0, 2))

    init = (C0.astype(jnp.float32), n0.astype(jnp.float32), m0.astype(jnp.float32))
    final, hs = lax.scan(mlstm_chunk, init, tuple(to_chunks(t) for t in (q, k, v, ig, lf)))
    h = jnp.moveaxis(hs, (0, 2), (1, 3)).reshape(B, S, M_HEADS, M_HEAD_DIM)
    return h, final


def mlstm_branch(qk_raw, v_m, o_m, i_m, f_m, conv_buf, C0, n0, m0, chunk, conv_w, conv_b, b_i, b_f, m_norm_g):
    B, S = qk_raw.shape[:2]
    xp = jnp.concatenate([conv_buf.astype(qk_raw.dtype), qk_raw], axis=1)
    conv = sum(xp[:, j:j + S] * conv_w[j] for j in range(CONV_W)) + conv_b
    qk = jax.nn.silu(conv)
    q = qk[..., :M_WIDTH].reshape(B, S, M_HEADS, M_HEAD_DIM)
    k = qk[..., M_WIDTH:].reshape(B, S, M_HEADS, M_HEAD_DIM) * (M_HEAD_DIM ** -0.5)
    v = v_m.reshape(B, S, M_HEADS, M_HEAD_DIM)
    ig = (i_m + b_i).astype(jnp.float32)
    lf = jax.nn.log_sigmoid((f_m + b_f).astype(jnp.float32))
    h, (C, n, m) = mlstm_scan(q, k, v, ig, lf, C0, n0, m0, chunk)
    h = rms_norm(h, m_norm_g).reshape(B, S, M_WIDTH)
    out = (jax.nn.sigmoid(o_m.astype(jnp.float32)) * h).astype(qk_raw.dtype)
    return out, xp[:, -(CONV_W - 1):], C, n, m


def moe(x, w_router, b_router, w_gu, b_gu, w_down, b_down):
    B, S, D = x.shape
    T = B * S
    xf = x.reshape(T, D)
    logits = (xf @ w_router + b_router).astype(jnp.float32)
    top_v, top_e = lax.top_k(logits, TOP_K)
    gate = jax.nn.softmax(top_v, axis=-1)
    A = T * TOP_K
    flat_e = top_e.reshape(-1)
    order = jnp.argsort(flat_e)
    se = flat_e[order]
    counts = jnp.bincount(flat_e, length=N_EXPERTS)
    padded = (counts + MOE_BLOCK - 1) // MOE_BLOCK * MOE_BLOCK
    pad_end = jnp.cumsum(padded)
    pad_start = pad_end - padded
    start = jnp.cumsum(counts) - counts
    dest = pad_start[se] + jnp.arange(A) - start[se]
    n_blocks = -(-A // MOE_BLOCK) + N_EXPERTS
    R = n_blocks * MOE_BLOCK
    row_tok = jnp.zeros((R,), jnp.int32).at[dest].set((order // TOP_K).astype(jnp.int32))
    row_gate = jnp.zeros((R,), jnp.float32).at[dest].set(gate.reshape(-1)[order])
    block_e = jnp.minimum(jnp.searchsorted(pad_end, jnp.arange(n_blocks) * MOE_BLOCK, side='right'), N_EXPERTS - 1)
    xs = xf[row_tok].reshape(n_blocks, MOE_BLOCK, D)

    def expert_block(args):
        xb, e = args
        hb = xb @ w_gu[e] + b_gu[e]
        glu, lin = jnp.split(hb, 2, axis=-1)
        glu = jnp.minimum(glu, SWIGLU_LIMIT)
        lin = jnp.clip(lin, -SWIGLU_LIMIT, SWIGLU_LIMIT)
        act = glu * jax.nn.sigmoid(SWIGLU_ALPHA * glu) * (lin + 1.0)
        return act @ w_down[e] + b_down[e]

    ys = lax.map(expert_block, (xs, block_e)).reshape(R, D)
    y = jax.ops.segment_sum(ys.astype(jnp.float32) * row_gate[:, None], row_tok, num_segments=T)
    return y.astype(x.dtype).reshape(B, S, D)


def decoder_layer(x, attn_fn, conv_buf, C0, n0, m0, chunk, g_attn, w_in, b_i, b_f, q_norm_g, k_norm_g, sinks,
                  conv_w, conv_b, m_norm_g, w_out, g_ffn, w_router, b_router, w_gu, b_gu, w_down, b_down):
    B, S, _ = x.shape
    z = rms_norm(x, g_attn) @ w_in
    q_a, k_a, v_a, qk_m, v_m, o_m, i_m, f_m = jnp.split(z, SPLITS, axis=-1)
    q_a = rms_norm(q_a.reshape(B, S, A_HEADS, HEAD_DIM), q_norm_g)
    k_a = rms_norm(k_a.reshape(B, S, A_KV_HEADS, HEAD_DIM), k_norm_g)
    v_a = v_a.reshape(B, S, A_KV_HEADS, HEAD_DIM)
    a_out, k_win, v_win = attn_fn(q_a, k_a, v_a, sinks)
    m_out, conv_new, C, n, m = mlstm_branch(qk_m, v_m, o_m, i_m, f_m, conv_buf, C0, n0, m0, chunk,
                                            conv_w, conv_b, b_i, b_f, m_norm_g)
    h = x + jnp.concatenate([a_out, m_out], axis=-1) @ w_out
    y = h + moe(rms_norm(h, g_ffn), w_router, b_router, w_gu, b_gu, w_down, b_down)
    return y, (k_win, v_win, conv_new, C, n, m)


def setup_inputs(seed: int = 0) -> dict:
    key = jax.random.key(seed)
    ks = iter(jax.random.split(key, 32))
    L = DEPTH

    def nrm(shape, scale):
        return jax.random.normal(next(ks), shape, jnp.float32) * scale

    return {
        'x_prompt': nrm((BATCH, SEQ, D_MODEL), 1.0),
        'x_sample': nrm((DEC_BATCH, DEC_SEQ, D_MODEL), 1.0),
        'cache_k_win': nrm((L, DEC_BATCH, WINDOW, A_KV_HEADS, HEAD_DIM), 1.0),
        'cache_v_win': nrm((L, DEC_BATCH, WINDOW, A_KV_HEADS, HEAD_DIM), 1.0),
        'state_conv': nrm((L, DEC_BATCH, CONV_W - 1, 2 * M_WIDTH), 1.0),
        'state_C': nrm((L, DEC_BATCH, M_HEADS, M_HEAD_DIM, M_HEAD_DIM), 0.1),
        'state_n': nrm((L, DEC_BATCH, M_HEADS, M_HEAD_DIM), 0.1),
        'state_m': nrm((L, DEC_BATCH, M_HEADS), 1.0),
        'g_attn': 1.0 + nrm((L, D_MODEL), 0.01),
        'w_in': nrm((L, D_MODEL, IN_WIDTH), D_MODEL ** -0.5),
        'b_i': nrm((L, M_HEADS), 0.1),
        'b_f': jnp.linspace(3.0, 6.0, M_HEADS, dtype=jnp.float32) + nrm((L, M_HEADS), 0.1),
        'q_norm_g': 1.0 + nrm((L, HEAD_DIM), 0.01),
        'k_norm_g': 1.0 + nrm((L, HEAD_DIM), 0.01),
        'sinks': nrm((L, A_HEADS), 0.5),
        'conv_w': nrm((L, CONV_W, 2 * M_WIDTH), CONV_W ** -0.5),
        'conv_b': nrm((L, 2 * M_WIDTH), 0.01),
        'm_norm_g': 1.0 + nrm((L, M_HEADS, M_HEAD_DIM), 0.01),
        'w_out': nrm((L, MIX_WIDTH, D_MODEL), MIX_WIDTH ** -0.5),
        'g_ffn': 1.0 + nrm((L, D_MODEL), 0.01),
        'w_router': nrm((L, D_MODEL, N_EXPERTS), D_MODEL ** -0.5),
        'b_router': nrm((L, N_EXPERTS), 0.01),
        'w_gate_up': nrm((L, N_EXPERTS, D_MODEL, 2 * D_FF), D_MODEL ** -0.5),
        'b_gate_up': nrm((L, N_EXPERTS, 2 * D_FF), 0.01),
        'w_down': nrm((L, N_EXPERTS, D_FF, D_MODEL), D_FF ** -0.5),
        'b_down': nrm((L, N_EXPERTS, D_MODEL), 0.01),
    }


def reference(x_prompt, x_sample, cache_k_win, cache_v_win, state_conv, state_C, state_n, state_m,
              g_attn, w_in, b_i, b_f, q_norm_g, k_norm_g, sinks, conv_w, conv_b, m_norm_g, w_out, g_ffn,
              w_router, b_router, w_gate_up, b_gate_up, w_down, b_down):
    yp, ys = x_prompt, x_sample
    sp, ss = [], []
    for l in range(DEPTH):
        lw = (g_attn[l], w_in[l], b_i[l], b_f[l], q_norm_g[l], k_norm_g[l], sinks[l], conv_w[l], conv_b[l],
              m_norm_g[l], w_out[l], g_ffn[l], w_router[l], b_router[l], w_gate_up[l], b_gate_up[l],
              w_down[l], b_down[l])
        bp = yp.shape[0]
        conv0 = jnp.zeros((bp, CONV_W - 1, 2 * M_WIDTH), yp.dtype)
        C0 = jnp.zeros((bp, M_HEADS, M_HEAD_DIM, M_HEAD_DIM), jnp.float32)
        n0 = jnp.zeros((bp, M_HEADS, M_HEAD_DIM), jnp.float32)
        m0 = jnp.full((bp, M_HEADS), NEG, jnp.float32)
        yp, st_p = decoder_layer(yp, attn_prompt, conv0, C0, n0, m0, CHUNK, *lw)
        sp.append(st_p)
        attn_s = functools.partial(attn_sample, k_buf=cache_k_win[l], v_buf=cache_v_win[l])
        ys, st_s = decoder_layer(ys, attn_s, state_conv[l], state_C[l], state_n[l], state_m[l], ys.shape[1], *lw)
        ss.append(st_s)

    def stack(states, i):
        return jnp.stack([s[i] for s in states])

    return (yp, ys,
            stack(sp, 0), stack(sp, 1), stack(sp, 2), stack(sp, 3), stack(sp, 4), stack(sp, 5),
            stack(ss, 0), stack(ss, 1), stack(ss, 2), stack(ss, 3), stack(ss, 4), stack(ss, 5))
```

```python
import functools

import jax
import jax.numpy as jnp
from jax import lax
from jax.experimental import pallas as pl
from jax.experimental.pallas import tpu as pltpu

F32 = jnp.float32
BF16 = jnp.bfloat16

D_MODEL = 1024
HEAD_DIM = 64
A_HEADS = 8
A_KV_HEADS = 2
A_GROUP = A_HEADS // A_KV_HEADS
A_WIDTH = A_HEADS * HEAD_DIM
KV_WIDTH = A_KV_HEADS * HEAD_DIM
WINDOW = 128
M_HEADS = 4
M_HEAD_DIM = 128
M_WIDTH = M_HEADS * M_HEAD_DIM
CONV_W = 4
N_EXPERTS = 32
TOP_K = 4
D_FF = D_MODEL
SWIGLU_LIMIT = 7.0
SWIGLU_ALPHA = 1.702
MOE_BLOCK = 256
EPS = 1e-6
NEG = -1e30

LANES = 128
SUBLANES = 8
GATE_COL = A_WIDTH + 2 * KV_WIDTH + 4 * M_WIDTH
IN_PAD = GATE_COL + LANES
TOK_TILE = 512
ATT_QB = 512
ATT_SB = 128
SAMPLE_NB = 16
PROMPT_CHUNK = 256
DUMMY_ROWS = 2 * MOE_BLOCK
VMEM_LIMIT = 48 * 1024 * 1024


def _dot(a, b):
    return jnp.dot(a, b, preferred_element_type=F32)


def _dot_nt(a, b):
    return lax.dot_general(a, b, (((1,), (1,)), ((), ())), preferred_element_type=F32)


def _dot_tn(a, b):
    return lax.dot_general(a, b, (((0,), (0,)), ((), ())), preferred_element_type=F32)


def _split3(x):
    hi = x.astype(BF16)
    r1 = x - hi.astype(F32)
    mid = r1.astype(BF16)
    lo = (r1 - mid.astype(F32)).astype(BF16)
    return hi, mid, lo


def _log_sigmoid(x):
    return jnp.minimum(x, 0.0) - jnp.log1p(jnp.exp(-jnp.abs(x)))


def _inproj_kernel(x_ref, g_ref, w_ref, wgt_ref, gmat_ref, qg_ref, kg_ref,
                   qn_ref, kn_ref, va_ref, qkm_ref, vm_ref, om_ref, gcol_ref, gt_ref):
    x = x_ref[...]
    ms = jnp.mean(x * x, axis=-1, keepdims=True)
    xn = ((x * lax.rsqrt(ms + EPS)) * g_ref[...]).astype(BF16)

    def seg(lo, hi):
        return _dot(xn, w_ref[:, lo:hi])

    def head_norm(z, gmat, g):
        hi, mid, lo = _split3(z * z)
        ss = _dot(hi, gmat) + _dot(mid, gmat) + _dot(lo, gmat)
        return (z * lax.rsqrt(ss * (1.0 / HEAD_DIM) + EPS)) * g

    o0 = A_WIDTH
    o1 = o0 + KV_WIDTH
    o2 = o1 + KV_WIDTH
    o3 = o2 + 2 * M_WIDTH
    o4 = o3 + M_WIDTH
    o5 = o4 + M_WIDTH
    qn_ref[...] = head_norm(seg(0, o0), gmat_ref[...], qg_ref[...])
    kn_ref[...] = head_norm(seg(o0, o1), gmat_ref[:KV_WIDTH, :KV_WIDTH], kg_ref[...])
    va_ref[...] = seg(o1, o2)
    qkm_ref[...] = seg(o2, o3)
    vm_ref[...] = seg(o3, o4)
    om_ref[...] = seg(o4, o5)
    gcol_ref[...] = seg(o5, o5 + LANES)
    gt_ref[0] = _dot_nt(wgt_ref[...], xn)


def _inproj(x2, g_attn, w_pad, wgt, gmat, qg, kg):
    T = x2.shape[0]
    nt = T // TOK_TILE
    row = lambda w: pl.BlockSpec((TOK_TILE, w), lambda i: (i, 0))
    full = lambda a: pl.BlockSpec(a.shape, lambda i: (0,) * a.ndim)
    out_shape = (
        jax.ShapeDtypeStruct((T, A_WIDTH), F32),
        jax.ShapeDtypeStruct((T, KV_WIDTH), F32),
        jax.ShapeDtypeStruct((T, KV_WIDTH), F32),
        jax.ShapeDtypeStruct((T, 2 * M_WIDTH), F32),
        jax.ShapeDtypeStruct((T, M_WIDTH), F32),
        jax.ShapeDtypeStruct((T, M_WIDTH), F32),
        jax.ShapeDtypeStruct((T, LANES), F32),
        jax.ShapeDtypeStruct((nt, SUBLANES, TOK_TILE), F32),
    )
    out_specs = (row(A_WIDTH), row(KV_WIDTH), row(KV_WIDTH), row(2 * M_WIDTH), row(M_WIDTH), row(M_WIDTH),
                 row(LANES), pl.BlockSpec((1, SUBLANES, TOK_TILE), lambda i: (i, 0, 0)))
    return pl.pallas_call(
        _inproj_kernel,
        grid=(nt,),
        in_specs=[row(D_MODEL), full(g_attn), full(w_pad), full(wgt), full(gmat), full(qg), full(kg)],
        out_specs=out_specs,
        out_shape=out_shape,
        compiler_params=pltpu.CompilerParams(dimension_semantics=("parallel",), vmem_limit_bytes=VMEM_LIMIT),
        name="inproj",
    )(x2, g_attn, w_pad, wgt, gmat, qg, kg)


def _softmax_sink(pieces, masks, sink_col):
    masked = [jnp.where(mk, s, NEG) for s, mk in zip(pieces, masks)]
    m = sink_col
    for s in masked:
        m = jnp.maximum(m, jnp.max(s, axis=-1, keepdims=True))
    ps = [jnp.exp(s - m) for s in masked]
    den = jnp.exp(sink_col - m)
    for p in ps:
        den = den + jnp.sum(p, axis=-1, keepdims=True)
    inv = 1.0 / den
    return [p * inv for p in ps]


def _stack_heads(q, g):
    return jnp.concatenate([q[:, (A_GROUP * g + i) * HEAD_DIM:(A_GROUP * g + i + 1) * HEAD_DIM]
                            for i in range(A_GROUP)], axis=0)


def _sink_col(sink_ref, g, rows_per_head):
    r = lax.broadcasted_iota(jnp.int32, (A_GROUP * rows_per_head, 1), 0)
    col = jnp.zeros((A_GROUP * rows_per_head, 1), F32)
    for i in range(A_GROUP):
        col = jnp.where(r // rows_per_head == i, sink_ref[A_GROUP * g + i], col)
    return col


def _attn_prompt_kernel(sink_ref, q_ref, kp_ref, kc_ref, vp_ref, vc_ref, o_ref):
    j = pl.program_id(1)
    scale = HEAD_DIM ** -0.5
    kc = kc_ref[...].astype(BF16)
    vc = vc_ref[...].astype(BF16)
    kp = kp_ref[...].astype(BF16)
    vp = vp_ref[...].astype(BF16)
    nrow = A_GROUP * ATT_SB
    r = lax.broadcasted_iota(jnp.int32, (nrow, ATT_SB), 0) % ATT_SB
    c = lax.broadcasted_iota(jnp.int32, (nrow, ATT_SB), 1)
    band_prev = c >= r
    band_cur = c <= r
    for sb in range(ATT_QB // ATT_SB):
        q = q_ref[sb * ATT_SB:(sb + 1) * ATT_SB, :].astype(BF16)
        if sb == 0:
            kprev, vprev = kp, vp
            mprev = jnp.logical_and(band_prev, j > 0)
        else:
            kprev = kc[(sb - 1) * ATT_SB:sb * ATT_SB]
            vprev = vc[(sb - 1) * ATT_SB:sb * ATT_SB]
            mprev = band_prev
        kcur = kc[sb * ATT_SB:(sb + 1) * ATT_SB]
        vcur = vc[sb * ATT_SB:(sb + 1) * ATT_SB]
        outs = []
        for g in range(A_KV_HEADS):
            lo, hi = g * HEAD_DIM, (g + 1) * HEAD_DIM
            qs = _stack_heads(q, g)
            s_p = _dot_nt(qs, kprev[:, lo:hi]) * scale
            s_c = _dot_nt(qs, kcur[:, lo:hi]) * scale
            p_p, p_c = _softmax_sink([s_p, s_c], [mprev, band_cur], _sink_col(sink_ref, g, ATT_SB))
            o = _dot(p_p.astype(BF16), vprev[:, lo:hi]) + _dot(p_c.astype(BF16), vcur[:, lo:hi])
            outs += [o[i * ATT_SB:(i + 1) * ATT_SB] for i in range(A_GROUP)]
        o_ref[sb * ATT_SB:(sb + 1) * ATT_SB, :] = jnp.concatenate(outs, axis=1)


def _attn_prompt(sinks, qn, kn, va, batch, seq):
    nq = seq // ATT_QB
    ratio = ATT_QB // ATT_SB
    cur = lambda w: pl.BlockSpec((ATT_QB, w), lambda b, j: (b * nq + j, 0))
    prev = lambda w: pl.BlockSpec((ATT_SB, w), lambda b, j: (jnp.maximum((b * nq + j) * ratio - 1, 0), 0))
    return pl.pallas_call(
        _attn_prompt_kernel,
        grid=(batch, nq),
        in_specs=[pl.BlockSpec(memory_space=pltpu.SMEM), cur(A_WIDTH), prev(KV_WIDTH), cur(KV_WIDTH),
                  prev(KV_WIDTH), cur(KV_WIDTH)],
        out_specs=cur(A_WIDTH),
        out_shape=jax.ShapeDtypeStruct((batch * seq, A_WIDTH), F32),
        compiler_params=pltpu.CompilerParams(dimension_semantics=("parallel", "parallel"),
                                             vmem_limit_bytes=VMEM_LIMIT),
        name="attn_prompt",
    )(sinks, qn, kn, kn, va, va)


def _attn_sample_kernel(dec, sink_ref, q_ref, kn_ref, vn_ref, ck_ref, cv_ref, o_ref, kw_ref, vw_ref):
    scale = HEAD_DIM ** -0.5
    rows = SAMPLE_NB * dec
    knew = kn_ref[...]
    vnew = vn_ref[...]
    knew_b = knew.astype(BF16)
    vnew_b = vnew.astype(BF16)
    nrow = A_GROUP * dec
    t = lax.broadcasted_iota(jnp.int32, (nrow, WINDOW), 0) % dec
    c = lax.broadcasted_iota(jnp.int32, (nrow, WINDOW), 1)
    m_cache = c >= t
    cn = lax.broadcasted_iota(jnp.int32, (nrow, rows), 1)
    tn = lax.broadcasted_iota(jnp.int32, (nrow, rows), 0) % dec
    for i in range(SAMPLE_NB):
        q = q_ref[i * dec:(i + 1) * dec, :].astype(BF16)
        ck = ck_ref[i].astype(BF16)
        cv = cv_ref[i].astype(BF16)
        m_new = jnp.logical_and(cn // dec == i, cn % dec <= tn)
        outs = []
        for g in range(A_KV_HEADS):
            lo, hi = g * HEAD_DIM, (g + 1) * HEAD_DIM
            qs = _stack_heads(q, g)
            s_c = _dot_nt(qs, ck[:, lo:hi]) * scale
            s_n = _dot_nt(qs, knew_b[:, lo:hi]) * scale
            p_c, p_n = _softmax_sink([s_c, s_n], [m_cache, m_new], _sink_col(sink_ref, g, dec))
            o = _dot(p_c.astype(BF16), cv[:, lo:hi]) + _dot(p_n.astype(BF16), vnew_b[:, lo:hi])
            outs += [o[h * dec:(h + 1) * dec] for h in range(A_GROUP)]
        o_ref[i * dec:(i + 1) * dec, :] = jnp.concatenate(outs, axis=1)
        kw_ref[i, 0:WINDOW - dec, :] = ck_ref[i, dec:WINDOW, :]
        kw_ref[i, WINDOW - dec:WINDOW, :] = knew[i * dec:(i + 1) * dec]
        vw_ref[i, 0:WINDOW - dec, :] = cv_ref[i, dec:WINDOW, :]
        vw_ref[i, WINDOW - dec:WINDOW, :] = vnew[i * dec:(i + 1) * dec]


def _attn_sample(sinks, qn, kn, va, ck, cv, row0, dbatch, dec):
    rows = SAMPLE_NB * dec
    off = row0 // rows
    tokrow = lambda w: pl.BlockSpec((rows, w), lambda i: (off + i, 0))
    cache = pl.BlockSpec((SAMPLE_NB, WINDOW, KV_WIDTH), lambda i: (i, 0, 0))
    return pl.pallas_call(
        functools.partial(_attn_sample_kernel, dec),
        grid=(dbatch // SAMPLE_NB,),
        in_specs=[pl.BlockSpec(memory_space=pltpu.SMEM), tokrow(A_WIDTH), tokrow(KV_WIDTH), tokrow(KV_WIDTH),
                  cache, cache],
        out_specs=(pl.BlockSpec((rows, A_WIDTH), lambda i: (i, 0)), cache, cache),
        out_shape=(jax.ShapeDtypeStruct((dbatch * dec, A_WIDTH), F32),
                   jax.ShapeDtypeStruct((dbatch, WINDOW, KV_WIDTH), F32),
                   jax.ShapeDtypeStruct((dbatch, WINDOW, KV_WIDTH), F32)),
        compiler_params=pltpu.CompilerParams(dimension_semantics=("parallel",), vmem_limit_bytes=VMEM_LIMIT),
        name="attn_sample",
    )(sinks, qn, kn, va, ck, cv)


def _mlstm_kernel(nseq, L, qk_ref, v_ref, o_ref, gcol_ref, gt_ref, conv0_ref, c0_ref, n0_ref, m0_ref,
                  cw_ref, cb_ref, gbrow_ref, gbcol_ref, mng_ref, mask_ref,
                  out_ref, cst_ref, nst_ref, mst_ref, prev_ref):
    R = nseq * L
    ci = pl.program_id(1)

    @pl.when(ci == 0)
    def _():
        cst_ref[...] = c0_ref[...]
        nst_ref[...] = n0_ref[...]
        mst_ref[...] = m0_ref[...]
        if nseq == 1:
            prev_ref[...] = jnp.zeros_like(prev_ref)
            prev_ref[R - SUBLANES:R, :] = conv0_ref[0]

    raw = qk_ref[...]
    row = lax.broadcasted_iota(jnp.int32, (R, 1), 0)
    tpos = row % L
    rseq = row // L
    if nseq == 1:
        prevsrc = prev_ref[...]
    else:
        prevsrc = conv0_ref[...].reshape(R, 2 * M_WIDTH)
    acc = raw * cw_ref[CONV_W - 1:CONV_W, :] + cb_ref[...]
    for k in range(1, CONV_W):
        pshift = k if nseq == 1 else R - SUBLANES + k
        sh = jnp.where(tpos >= k, pltpu.roll(raw, k, 0), pltpu.roll(prevsrc, pshift, 0))
        acc = acc + sh * cw_ref[CONV_W - 1 - k:CONV_W - k, :]
    if nseq == 1:
        prev_ref[...] = raw
    qkc = acc * jax.nn.sigmoid(acc)

    gc = gcol_ref[...] + gbrow_ref[...]
    gr = gt_ref[0] + gbcol_ref[...]
    lsc = _log_sigmoid(gc)
    lsr = _log_sigmoid(gr)
    mb = mask_ref[...]
    maskb = mb > 0
    bcol = sum(_dot(mb, p) for p in _split3(lsc))
    brow = sum(_dot_nt(p, mb) for p in _split3(lsr))

    lane = lax.broadcasted_iota(jnp.int32, (1, LANES), 1)
    m_new = [jnp.zeros((1, LANES), F32) for _ in range(nseq)]
    for h in range(M_HEADS):
        sl = slice(h * M_HEAD_DIM, (h + 1) * M_HEAD_DIM)
        qh = qkc[:, sl]
        kh = qkc[:, M_WIDTH + h * M_HEAD_DIM:M_WIDTH + (h + 1) * M_HEAD_DIM] * (M_HEAD_DIM ** -0.5)
        vh = v_ref[:, sl]
        qb, kb, vb = qh.astype(BF16), kh.astype(BF16), vh.astype(BF16)
        ig_c = gc[:, h:h + 1]
        b_c = bcol[:, M_HEADS + h:M_HEADS + h + 1]
        ig_r = gr[h:h + 1, :]
        b_r = brow[M_HEADS + h:M_HEADS + h + 1, :]
        if nseq == 1:
            m0c = mst_ref[0][:, h:h + 1]
            n0rows = nst_ref[0, h:h + 1, :]
        else:
            m0c = jnp.zeros((R, 1), F32)
            n0rows = jnp.zeros((R, M_HEAD_DIM), F32)
            for s in range(nseq):
                m0c = jnp.where(rseq == s, mst_ref[s][:, h:h + 1], m0c)
                n0rows = jnp.where(rseq == s, nst_ref[s, h:h + 1, :], n0rows)
        dm = jnp.where(maskb, b_c - b_r + ig_r, NEG)
        a_c = b_c + m0c
        m_c = jnp.maximum(a_c, jnp.max(dm, axis=-1, keepdims=True))
        w = jnp.exp(dm - m_c)
        sc = jnp.exp(a_c - m_c)
        wqk = w * _dot_nt(qb, kb)
        if nseq == 1:
            inter = _dot_nt(qb, cst_ref[0, h].astype(BF16))
        else:
            inter = jnp.zeros((R, M_HEAD_DIM), F32)
            for s in range(nseq):
                qs = jnp.where(rseq == s, qh, 0.0).astype(BF16)
                inter = inter + _dot_nt(qs, cst_ref[s, h].astype(BF16))
        num = _dot(wqk.astype(BF16), vb) + sc * inter
        den = jnp.sum(wqk, axis=-1, keepdims=True) + sc * jnp.sum(qh * n0rows, axis=-1, keepdims=True)
        hh = num / jnp.maximum(jnp.abs(den), jnp.exp(-m_c))

        for s in range(nseq):
            e = s * L + L - 1
            m_end = m_c[e:e + 1, :]
            wend = jnp.exp(b_c[e:e + 1, :] - b_c + ig_c - m_end)
            if nseq > 1:
                wend = jnp.where(rseq == s, wend, 0.0)
            sce = jnp.exp(a_c[e:e + 1, :] - m_end)
            c_new = sce * cst_ref[s, h] + _dot_tn((vh * wend).astype(BF16), kb)
            n_new = sce * nst_ref[s, h:h + 1, :] + jnp.sum(wend * kh, axis=0, keepdims=True)
            cst_ref[s, h] = c_new
            nst_ref[s, h:h + 1, :] = n_new
            m_new[s] = jnp.where(lane == h, m_end, m_new[s])

        hn = (hh * lax.rsqrt(jnp.mean(hh * hh, axis=-1, keepdims=True) + EPS)) * mng_ref[:, sl]
        out_ref[:, sl] = jax.nn.sigmoid(o_ref[:, sl]) * hn
    for s in range(nseq):
        mst_ref[s] = m_new[s]


def _mlstm(qkm, vm, om, gcol, gt, conv0, c0, n0, m0, cw, cb, gbrow, gbcol, mng, row0, nseq, L, ngroups, nchunks):
    R = nseq * L
    off = row0 // R
    per_tile = TOK_TILE // R
    tok = lambda w: pl.BlockSpec((R, w), lambda g, c: (off + g * nchunks + c, 0))
    gt_spec = pl.BlockSpec((1, SUBLANES, R),
                           lambda g, c: ((off + g * nchunks + c) // per_tile, 0, (off + g * nchunks + c) % per_tile))
    full = lambda a: pl.BlockSpec(a.shape, lambda g, c: (0,) * a.ndim)
    st4 = pl.BlockSpec((nseq, M_HEADS, M_HEAD_DIM, M_HEAD_DIM), lambda g, c: (g, 0, 0, 0))
    st3 = pl.BlockSpec((nseq, M_HEADS, M_HEAD_DIM), lambda g, c: (g, 0, 0))
    stm = pl.BlockSpec((nseq, 1, LANES), lambda g, c: (g, 0, 0))
    conv_spec = pl.BlockSpec((nseq, SUBLANES, 2 * M_WIDTH), lambda g, c: (g, 0, 0))
    r = jnp.arange(R)
    mask = ((r[:, None] // L == r[None, :] // L) & (r[None, :] <= r[:, None])).astype(BF16)
    nstate = ngroups * nseq
    return pl.pallas_call(
        functools.partial(_mlstm_kernel, nseq, L),
        grid=(ngroups, nchunks),
        in_specs=[tok(2 * M_WIDTH), tok(M_WIDTH), tok(M_WIDTH), tok(LANES), gt_spec, conv_spec, st4, st3, stm,
                  full(cw), full(cb), full(gbrow), full(gbcol), full(mng), full(mask)],
        out_specs=(pl.BlockSpec((R, M_WIDTH), lambda g, c: (g * nchunks + c, 0)), st4, st3, stm),
        out_shape=(jax.ShapeDtypeStruct((ngroups * nchunks * R, M_WIDTH), F32),
                   jax.ShapeDtypeStruct((nstate, M_HEADS, M_HEAD_DIM, M_HEAD_DIM), F32),
                   jax.ShapeDtypeStruct((nstate, M_HEADS, M_HEAD_DIM), F32),
                   jax.ShapeDtypeStruct((nstate, 1, LANES), F32)),
        scratch_shapes=[pltpu.VMEM((R, 2 * M_WIDTH), F32)],
        compiler_params=pltpu.CompilerParams(dimension_semantics=("parallel", "arbitrary"),
                                             vmem_limit_bytes=VMEM_LIMIT),
        name="mlstm_n%d" % nseq,
    )(qkm, vm, om, gcol, gt, conv0, c0, n0, m0, cw, cb, gbrow, gbcol, mng, mask)


def _outproj_kernel(a_ref, m_ref, x_ref, wo_ref, g_ref, wr_ref, br_ref, h_ref, hn_ref, route_ref):
    a = a_ref[...].astype(BF16)
    m = m_ref[...].astype(BF16)
    h = x_ref[...] + _dot(a, wo_ref[0:A_WIDTH, :]) + _dot(m, wo_ref[A_WIDTH:A_WIDTH + M_WIDTH, :])
    h_ref[...] = h
    hn = (h * lax.rsqrt(jnp.mean(h * h, axis=-1, keepdims=True) + EPS)) * g_ref[...]
    hn_ref[...] = hn
    logits = _dot(hn.astype(BF16), wr_ref[...]) + br_ref[...]
    lane = lax.broadcasted_iota(jnp.int32, logits.shape, 1)
    route = jnp.zeros(logits.shape, F32)
    top0 = None
    den = None
    es = []
    for k in range(TOP_K):
        mx = jnp.max(logits, axis=-1, keepdims=True)
        idx = jnp.min(jnp.where(logits == mx, lane, LANES), axis=-1, keepdims=True)
        if k == 0:
            top0 = mx
        e = jnp.exp(mx - top0)
        den = e if den is None else den + e
        es.append(e)
        route = jnp.where(lane == TOP_K + k, idx.astype(F32), route)
        logits = jnp.where(lane == idx, -jnp.inf, logits)
    for k in range(TOP_K):
        route = jnp.where(lane == k, es[k] / den, route)
    route_ref[...] = route


def _outproj(a_out, m_out, x2, w_out, g_ffn, wr_pad, br_pad):
    T = x2.shape[0]
    row = lambda w: pl.BlockSpec((TOK_TILE, w), lambda i: (i, 0))
    full = lambda a: pl.BlockSpec(a.shape, lambda i: (0,) * a.ndim)
    return pl.pallas_call(
        _outproj_kernel,
        grid=(T // TOK_TILE,),
        in_specs=[row(A_WIDTH), row(M_WIDTH), row(D_MODEL), full(w_out), full(g_ffn), full(wr_pad), full(br_pad)],
        out_specs=(row(D_MODEL), row(D_MODEL), row(LANES)),
        out_shape=(jax.ShapeDtypeStruct((T, D_MODEL), F32), jax.ShapeDtypeStruct((T, D_MODEL), F32),
                   jax.ShapeDtypeStruct((T, LANES), F32)),
        compiler_params=pltpu.CompilerParams(dimension_semantics=("parallel",), vmem_limit_bytes=VMEM_LIMIT),
        name="outproj_router",
    )(a_out, m_out, x2, w_out, g_ffn, wr_pad, br_pad)


def _moe_kernel(be_ref, nu_ref, tokc_ref, tokn_ref, dst_ref, gate_ref, wgu_ref, bgu_ref, wd_ref, bd_ref,
                hn_hbm, yk_hbm, xbuf, obuf, gsem, ssem):
    del be_ref
    i = pl.program_id(0)
    nblk = pl.num_programs(0)
    nu = nu_ref[0]
    slot = i % 2

    def issue_gather(tok_ref, s):
        def body(r8, carry):
            for u in range(SUBLANES):
                r = r8 * SUBLANES + u
                t = tok_ref[0, 0, r]
                pltpu.make_async_copy(hn_hbm.at[pl.ds(t, 1), :], xbuf.at[s, pl.ds(r, 1), :], gsem.at[s]).start()
            return carry
        lax.fori_loop(0, MOE_BLOCK // SUBLANES, body, 0)

    def wait_gather(s):
        pltpu.make_async_copy(hn_hbm.at[pl.ds(0, MOE_BLOCK), :], xbuf.at[s], gsem.at[s]).wait()

    def issue_scatter(s):
        def body(r8, carry):
            for u in range(SUBLANES):
                r = r8 * SUBLANES + u
                d = dst_ref[0, 0, r]
                pltpu.make_async_copy(obuf.at[s, pl.ds(r, 1), :], yk_hbm.at[pl.ds(d, 1), :], ssem.at[s]).start()
            return carry
        lax.fori_loop(0, MOE_BLOCK // SUBLANES, body, 0)

    def wait_scatter(s):
        pltpu.make_async_copy(obuf.at[s], yk_hbm.at[pl.ds(0, MOE_BLOCK), :], ssem.at[s]).wait()

    @pl.when(i == 0)
    def _():
        obuf[0] = jnp.zeros((MOE_BLOCK, D_MODEL), F32)
        n_real = yk_hbm.shape[0] - DUMMY_ROWS
        for s in range(DUMMY_ROWS // MOE_BLOCK):
            cp = pltpu.make_async_copy(obuf.at[0], yk_hbm.at[pl.ds(n_real + s * MOE_BLOCK, MOE_BLOCK), :], ssem.at[0])
            cp.start()
            cp.wait()

    @pl.when(jnp.logical_and(i == 0, nu > 0))
    def _():
        issue_gather(tokc_ref, 0)

    @pl.when(i + 1 < nu)
    def _():
        issue_gather(tokn_ref, 1 - slot)

    @pl.when(i < nu)
    def _():
        wait_gather(slot)

        @pl.when(i >= 2)
        def _():
            wait_scatter(slot)

        x = xbuf[slot].astype(BF16)
        hb = _dot(x, wgu_ref[0]) + bgu_ref[0]
        glu = jnp.minimum(hb[:, :D_FF], SWIGLU_LIMIT)
        lin = jnp.clip(hb[:, D_FF:], -SWIGLU_LIMIT, SWIGLU_LIMIT)
        act = glu * jax.nn.sigmoid(SWIGLU_ALPHA * glu) * (lin + 1.0)
        y = _dot(act.astype(BF16), wd_ref[0]) + bd_ref[0]
        obuf[slot] = y * gate_ref[...]
        issue_scatter(slot)

    @pl.when(i == nblk - 1)
    def _():
        @pl.when(nu >= 1)
        def _():
            wait_scatter((nu - 1) % 2)

        @pl.when(nu >= 2)
        def _():
            wait_scatter(nu % 2)


def _moe_blocks(block_e, nused, row_tok, row_dst, row_gate, wgu, bgu, wd, bd, hn, n_slots):
    nblk = block_e.shape[0]
    smem_blk = lambda imap: pl.BlockSpec((1, 1, MOE_BLOCK), imap, memory_space=pltpu.SMEM)
    grid_spec = pltpu.PrefetchScalarGridSpec(
        num_scalar_prefetch=2,
        grid=(nblk,),
        in_specs=[
            smem_blk(lambda i, be, nu: (i, 0, 0)),
            smem_blk(lambda i, be, nu: (jnp.minimum(i + 1, nblk - 1), 0, 0)),
            smem_blk(lambda i, be, nu: (i, 0, 0)),
            pl.BlockSpec((MOE_BLOCK, 1), lambda i, be, nu: (i, 0)),
            pl.BlockSpec((1, D_MODEL, 2 * D_FF), lambda i, be, nu: (be[i], 0, 0)),
            pl.BlockSpec((1, 1, 2 * D_FF), lambda i, be, nu: (be[i], 0, 0)),
            pl.BlockSpec((1, D_FF, D_MODEL), lambda i, be, nu: (be[i], 0, 0)),
            pl.BlockSpec((1, 1, D_MODEL), lambda i, be, nu: (be[i], 0, 0)),
            pl.BlockSpec(memory_space=pl.ANY),
        ],
        out_specs=pl.BlockSpec(memory_space=pl.ANY),
        scratch_shapes=[pltpu.VMEM((2, MOE_BLOCK, D_MODEL), F32), pltpu.VMEM((2, MOE_BLOCK, D_MODEL), F32),
                        pltpu.SemaphoreType.DMA((2,)), pltpu.SemaphoreType.DMA((2,))],
    )
    tok3 = row_tok.reshape(nblk, 1, MOE_BLOCK)
    dst3 = row_dst.reshape(nblk, 1, MOE_BLOCK)
    return pl.pallas_call(
        _moe_kernel,
        grid_spec=grid_spec,
        out_shape=jax.ShapeDtypeStruct((n_slots, D_MODEL), F32),
        compiler_params=pltpu.CompilerParams(dimension_semantics=("arbitrary",), vmem_limit_bytes=VMEM_LIMIT),
        name="moe_blocks",
    )(block_e, nused, tok3, tok3, dst3, row_gate.reshape(-1, 1), wgu, bgu, wd, bd, hn)


def _combine_kernel(h_ref, y0_ref, y1_ref, y2_ref, y3_ref, o_ref):
    o_ref[...] = h_ref[...] + (((y0_ref[...] + y1_ref[...]) + y2_ref[...]) + y3_ref[...])


def _combine(h, yk):
    T = h.shape[0]
    nt = T // TOK_TILE
    specs = [pl.BlockSpec((TOK_TILE, D_MODEL), (lambda i, k=k: (k * nt + i, 0))) for k in range(TOP_K)]
    return pl.pallas_call(
        _combine_kernel,
        grid=(nt,),
        in_specs=[pl.BlockSpec((TOK_TILE, D_MODEL), lambda i: (i, 0))] + specs,
        out_specs=pl.BlockSpec((TOK_TILE, D_MODEL), lambda i: (i, 0)),
        out_shape=jax.ShapeDtypeStruct((T, D_MODEL), F32),
        compiler_params=pltpu.CompilerParams(dimension_semantics=("parallel",), vmem_limit_bytes=VMEM_LIMIT),
        name="moe_combine",
    )(h, yk, yk, yk, yk)


def _routing(route, T):
    gate = route[:, :TOP_K]
    top_e = route[:, TOP_K:2 * TOP_K].astype(jnp.int32)
    A = T * TOP_K
    flat_e = top_e.reshape(-1)
    order = jnp.argsort(flat_e)
    se = flat_e[order]
    counts = jnp.bincount(flat_e, length=N_EXPERTS)
    padded = (counts + MOE_BLOCK - 1) // MOE_BLOCK * MOE_BLOCK
    pad_end = jnp.cumsum(padded)
    pad_start = pad_end - padded
    start = jnp.cumsum(counts) - counts
    dest = pad_start[se] + jnp.arange(A) - start[se]
    n_blocks = A // MOE_BLOCK + N_EXPERTS
    R = n_blocks * MOE_BLOCK
    tok = (order // TOP_K).astype(jnp.int32)
    slot = ((order % TOP_K) * T + tok).astype(jnp.int32)
    dummy = (A + jnp.arange(R) % DUMMY_ROWS).astype(jnp.int32)
    row_tok = jnp.zeros((R,), jnp.int32).at[dest].set(tok)
    row_dst = dummy.at[dest].set(slot)
    row_gate = jnp.zeros((R,), F32).at[dest].set(gate.reshape(-1)[order])
    block_e = jnp.minimum(jnp.searchsorted(pad_end, jnp.arange(n_blocks) * MOE_BLOCK, side='right'),
                          N_EXPERTS - 1).astype(jnp.int32)
    nused = (pad_end[-1] // MOE_BLOCK).astype(jnp.int32).reshape(1)
    return block_e, nused, row_tok, row_dst, row_gate


def kernel(x_prompt, x_sample, cache_k_win, cache_v_win, state_conv, state_C, state_n, state_m, g_attn, w_in, b_i,
           b_f, q_norm_g, k_norm_g, sinks, conv_w, conv_b, m_norm_g, w_out, g_ffn, w_router, b_router, w_gate_up,
           b_gate_up, w_down, b_down):
    depth = g_attn.shape[0]
    assert depth == 1
    B, S, _ = x_prompt.shape
    DB, DS, _ = x_sample.shape
    TP = B * S
    TS = DB * DS
    T = TP + TS
    assert T % TOK_TILE == 0 and TP % TOK_TILE == 0 and S % ATT_QB == 0 and S % PROMPT_CHUNK == 0
    assert DS == SUBLANES and DB % SAMPLE_NB == 0 and (SAMPLE_NB * DS) == LANES
    l = 0

    x2 = jnp.concatenate([x_prompt.reshape(TP, D_MODEL), x_sample.reshape(TS, D_MODEL)], axis=0)

    w_pad = jnp.pad(w_in[l], ((0, 0), (0, IN_PAD - w_in.shape[2]))).astype(BF16)
    wgt = jnp.transpose(w_in[l][:, GATE_COL:GATE_COL + 2 * M_HEADS]).astype(BF16)
    gi = jnp.arange(A_WIDTH) // HEAD_DIM
    gmat = (gi[:, None] == gi[None, :]).astype(BF16)
    qg = jnp.tile(q_norm_g[l], A_HEADS).reshape(1, A_WIDTH)
    kg = jnp.tile(k_norm_g[l], A_KV_HEADS).reshape(1, KV_WIDTH)
    gbias = jnp.concatenate([b_i[l], b_f[l]])
    gbrow = jnp.pad(gbias, (0, LANES - 2 * M_HEADS)).reshape(1, LANES)
    gbcol = gbias.reshape(2 * M_HEADS, 1)
    mng = m_norm_g[l].reshape(1, M_WIDTH)
    cw = conv_w[l]
    cb = conv_b[l].reshape(1, 2 * M_WIDTH)
    wr_pad = jnp.pad(w_router[l], ((0, 0), (0, LANES - N_EXPERTS))).astype(BF16)
    br_pad = jnp.concatenate([b_router[l], jnp.full((LANES - N_EXPERTS,), NEG, F32)]).reshape(1, LANES)

    qn, kn, va, qkm, vm, om, gcol, gt = _inproj(x2, g_attn[l].reshape(1, D_MODEL), w_pad, wgt, gmat, qg, kg)

    a_p = _attn_prompt(sinks[l], qn, kn, va, B, S)
    ck = cache_k_win[l].reshape(DB, WINDOW, KV_WIDTH)
    cv = cache_v_win[l].reshape(DB, WINDOW, KV_WIDTH)
    a_s, kwin_s, vwin_s = _attn_sample(sinks[l], qn, kn, va, ck, cv, TP, DB, DS)

    zc = jnp.zeros((B, SUBLANES, 2 * M_WIDTH), F32)
    m_p, C_p, n_p, mm_p = _mlstm(
        qkm, vm, om, gcol, gt, zc,
        jnp.zeros((B, M_HEADS, M_HEAD_DIM, M_HEAD_DIM), F32), jnp.zeros((B, M_HEADS, M_HEAD_DIM), F32),
        jnp.full((B, 1, LANES), NEG, F32), cw, cb, gbrow, gbcol, mng,
        row0=0, nseq=1, L=PROMPT_CHUNK, ngroups=B, nchunks=S // PROMPT_CHUNK)
    conv_s0 = jnp.pad(state_conv[l], ((0, 0), (SUBLANES - (CONV_W - 1), 0), (0, 0)))
    m0_s = jnp.pad(state_m[l], ((0, 0), (0, LANES - M_HEADS))).reshape(DB, 1, LANES)
    m_s, C_s, n_s, mm_s = _mlstm(
        qkm, vm, om, gcol, gt, conv_s0, state_C[l], state_n[l], m0_s, cw, cb, gbrow, gbcol, mng,
        row0=TP, nseq=SAMPLE_NB, L=DS, ngroups=DB // SAMPLE_NB, nchunks=1)

    a_out = jnp.concatenate([a_p, a_s], axis=0)
    m_out = jnp.concatenate([m_p, m_s], axis=0)

    h, hn, route = _outproj(a_out, m_out, x2, w_out[l].astype(BF16), g_ffn[l].reshape(1, D_MODEL), wr_pad, br_pad)

    block_e, nused, row_tok, row_dst, row_gate = _routing(route, T)
    yk = _moe_blocks(block_e, nused, row_tok, row_dst, row_gate,
                     w_gate_up[l].astype(BF16), b_gate_up[l].reshape(N_EXPERTS, 1, 2 * D_FF),
                     w_down[l].astype(BF16), b_down[l].reshape(N_EXPERTS, 1, D_MODEL),
                     hn, T * TOP_K + DUMMY_ROWS)
    y = _combine(h, yk)

    y_p = y[:TP].reshape(B, S, D_MODEL)
    y_s = y[TP:].reshape(DB, DS, D_MODEL)
    kn_p = kn[:TP].reshape(B, S, A_KV_HEADS, HEAD_DIM)
    va_p = va[:TP].reshape(B, S, A_KV_HEADS, HEAD_DIM)
    qkm_p = qkm[:TP].reshape(B, S, 2 * M_WIDTH)
    qkm_s = qkm[TP:].reshape(DB, DS, 2 * M_WIDTH)
    return (y_p, y_s,
            kn_p[:, -WINDOW:][None], va_p[:, -WINDOW:][None], qkm_p[:, -(CONV_W - 1):][None],
            C_p[None], n_p[None], mm_p[:, 0, :M_HEADS][None],
            kwin_s.reshape(DB, WINDOW, A_KV_HEADS, HEAD_DIM)[None],
            vwin_s.reshape(DB, WINDOW, A_KV_HEADS, HEAD_DIM)[None],
            qkm_s[:, -(CONV_W - 1):][None],
            C_s[None], n_s[None], mm_s[:, 0, :M_HEADS][None])
```

```python
import functools

import jax
import jax.numpy as jnp
from jax import lax
from jax.experimental import pallas as pl
from jax.experimental.pallas import tpu as pltpu

F32 = jnp.float32
BF16 = jnp.bfloat16

D_MODEL = 1024
HEAD_DIM = 64
A_HEADS = 8
A_KV_HEADS = 2
A_GROUP = A_HEADS // A_KV_HEADS
A_WIDTH = A_HEADS * HEAD_DIM
KV_WIDTH = A_KV_HEADS * HEAD_DIM
WINDOW = 128
M_HEADS = 4
M_HEAD_DIM = 128
M_WIDTH = M_HEADS * M_HEAD_DIM
CONV_W = 4
N_EXPERTS = 32
TOP_K = 4
D_FF = D_MODEL
SWIGLU_LIMIT = 7.0
SWIGLU_ALPHA = 1.702
MOE_BLOCK = 256
EPS = 1e-6
NEG = -1e30

LANES = 128
SUBLANES = 8
GATE_COL = A_WIDTH + 2 * KV_WIDTH + 4 * M_WIDTH
IN_PAD = GATE_COL + LANES
TOK_TILE = 512
ATT_QB = 512
ATT_SB = 128
SAMPLE_NB = 16
PROMPT_CHUNK = 256
CMB_TILE = 256
VMEM_LIMIT = 48 * 1024 * 1024


def _dot(a, b):
    return jnp.dot(a, b, preferred_element_type=F32)


def _dot_nt(a, b):
    return lax.dot_general(a, b, (((1,), (1,)), ((), ())), preferred_element_type=F32)


def _dot_tn(a, b):
    return lax.dot_general(a, b, (((0,), (0,)), ((), ())), preferred_element_type=F32)


def _split3(x):
    hi = x.astype(BF16)
    r1 = x - hi.astype(F32)
    mid = r1.astype(BF16)
    lo = (r1 - mid.astype(F32)).astype(BF16)
    return hi, mid, lo


def _log_sigmoid(x):
    return jnp.minimum(x, 0.0) - jnp.log1p(jnp.exp(-jnp.abs(x)))


def _pick(n_prompt_tiles, p_ref, s_ref):
    return jnp.where(pl.program_id(0) < n_prompt_tiles, p_ref[...], s_ref[...])


def _split_specs(n_prompt_tiles, rows, width):
    return (pl.BlockSpec((rows, width), lambda i, *_: (jnp.minimum(i, n_prompt_tiles - 1), 0)),
            pl.BlockSpec((rows, width), lambda i, *_: (jnp.maximum(i - n_prompt_tiles, 0), 0)))


def _inproj_kernel(npt, xp_ref, xs_ref, g_ref, w_ref, wgt_ref, gmat_ref, qg_ref, kg_ref,
                   qn_ref, kn_ref, va_ref, qkm_ref, vm_ref, om_ref, gcol_ref, gt_ref):
    x = _pick(npt, xp_ref, xs_ref)
    ms = jnp.mean(x * x, axis=-1, keepdims=True)
    xn = ((x * lax.rsqrt(ms + EPS)) * g_ref[...]).astype(BF16)

    def seg(lo, hi):
        return _dot(xn, w_ref[:, lo:hi])

    def head_norm(z, gmat, g):
        hi, mid, lo = _split3(z * z)
        ss = _dot(hi, gmat) + _dot(mid, gmat) + _dot(lo, gmat)
        return (z * lax.rsqrt(ss * (1.0 / HEAD_DIM) + EPS)) * g

    o0 = A_WIDTH
    o1 = o0 + KV_WIDTH
    o2 = o1 + KV_WIDTH
    o3 = o2 + 2 * M_WIDTH
    o4 = o3 + M_WIDTH
    o5 = o4 + M_WIDTH
    qn_ref[...] = head_norm(seg(0, o0), gmat_ref[...], qg_ref[...])
    kn_ref[...] = head_norm(seg(o0, o1), gmat_ref[:KV_WIDTH, :KV_WIDTH], kg_ref[...])
    va_ref[...] = seg(o1, o2)
    qkm_ref[...] = seg(o2, o3)
    vm_ref[...] = seg(o3, o4)
    om_ref[...] = seg(o4, o5)
    gcol_ref[...] = seg(o5, o5 + LANES)
    gt_ref[0] = _dot_nt(wgt_ref[...], xn)


def _inproj(xp, xs, g_attn, w_pad, wgt, gmat, qg, kg):
    T = xp.shape[0] + xs.shape[0]
    nt = T // TOK_TILE
    npt = xp.shape[0] // TOK_TILE
    row = lambda w: pl.BlockSpec((TOK_TILE, w), lambda i: (i, 0))
    full = lambda a: pl.BlockSpec(a.shape, lambda i: (0,) * a.ndim)
    out_shape = (
        jax.ShapeDtypeStruct((T, A_WIDTH), F32),
        jax.ShapeDtypeStruct((T, KV_WIDTH), F32),
        jax.ShapeDtypeStruct((T, KV_WIDTH), F32),
        jax.ShapeDtypeStruct((T, 2 * M_WIDTH), F32),
        jax.ShapeDtypeStruct((T, M_WIDTH), F32),
        jax.ShapeDtypeStruct((T, M_WIDTH), F32),
        jax.ShapeDtypeStruct((T, LANES), F32),
        jax.ShapeDtypeStruct((nt, SUBLANES, TOK_TILE), F32),
    )
    out_specs = (row(A_WIDTH), row(KV_WIDTH), row(KV_WIDTH), row(2 * M_WIDTH), row(M_WIDTH), row(M_WIDTH),
                 row(LANES), pl.BlockSpec((1, SUBLANES, TOK_TILE), lambda i: (i, 0, 0)))
    return pl.pallas_call(
        functools.partial(_inproj_kernel, npt),
        grid=(nt,),
        in_specs=[*_split_specs(npt, TOK_TILE, D_MODEL), full(g_attn), full(w_pad), full(wgt), full(gmat),
                  full(qg), full(kg)],
        out_specs=out_specs,
        out_shape=out_shape,
        compiler_params=pltpu.CompilerParams(dimension_semantics=("parallel",), vmem_limit_bytes=VMEM_LIMIT),
        name="inproj",
    )(xp, xs, g_attn, w_pad, wgt, gmat, qg, kg)


def _softmax_sink(pieces, masks, sink_col):
    masked = [jnp.where(mk, s, NEG) for s, mk in zip(pieces, masks)]
    m = sink_col
    for s in masked:
        m = jnp.maximum(m, jnp.max(s, axis=-1, keepdims=True))
    ps = [jnp.exp(s - m) for s in masked]
    den = jnp.exp(sink_col - m)
    for p in ps:
        den = den + jnp.sum(p, axis=-1, keepdims=True)
    inv = 1.0 / den
    return [p * inv for p in ps]


def _stack_heads(q, g):
    return jnp.concatenate([q[:, (A_GROUP * g + i) * HEAD_DIM:(A_GROUP * g + i + 1) * HEAD_DIM]
                            for i in range(A_GROUP)], axis=0)


def _sink_col(sink_ref, g, rows_per_head):
    r = lax.broadcasted_iota(jnp.int32, (A_GROUP * rows_per_head, 1), 0)
    col = jnp.zeros((A_GROUP * rows_per_head, 1), F32)
    for i in range(A_GROUP):
        col = jnp.where(r // rows_per_head == i, sink_ref[A_GROUP * g + i], col)
    return col


def _attn_prompt_kernel(sink_ref, q_ref, kp_ref, kc_ref, vp_ref, vc_ref, o_ref):
    j = pl.program_id(1)
    scale = HEAD_DIM ** -0.5
    kc = kc_ref[...].astype(BF16)
    vc = vc_ref[...].astype(BF16)
    kp = kp_ref[...].astype(BF16)
    vp = vp_ref[...].astype(BF16)
    nrow = A_GROUP * ATT_SB
    r = lax.broadcasted_iota(jnp.int32, (nrow, ATT_SB), 0) % ATT_SB
    c = lax.broadcasted_iota(jnp.int32, (nrow, ATT_SB), 1)
    band_prev = c >= r
    band_cur = c <= r
    for sb in range(ATT_QB // ATT_SB):
        q = q_ref[sb * ATT_SB:(sb + 1) * ATT_SB, :].astype(BF16)
        if sb == 0:
            kprev, vprev = kp, vp
            mprev = jnp.logical_and(band_prev, j > 0)
        else:
            kprev = kc[(sb - 1) * ATT_SB:sb * ATT_SB]
            vprev = vc[(sb - 1) * ATT_SB:sb * ATT_SB]
            mprev = band_prev
        kcur = kc[sb * ATT_SB:(sb + 1) * ATT_SB]
        vcur = vc[sb * ATT_SB:(sb + 1) * ATT_SB]
        outs = []
        for g in range(A_KV_HEADS):
            lo, hi = g * HEAD_DIM, (g + 1) * HEAD_DIM
            qs = _stack_heads(q, g)
            s_p = _dot_nt(qs, kprev[:, lo:hi]) * scale
            s_c = _dot_nt(qs, kcur[:, lo:hi]) * scale
            p_p, p_c = _softmax_sink([s_p, s_c], [mprev, band_cur], _sink_col(sink_ref, g, ATT_SB))
            o = _dot(p_p.astype(BF16), vprev[:, lo:hi]) + _dot(p_c.astype(BF16), vcur[:, lo:hi])
            outs += [o[i * ATT_SB:(i + 1) * ATT_SB] for i in range(A_GROUP)]
        o_ref[sb * ATT_SB:(sb + 1) * ATT_SB, :] = jnp.concatenate(outs, axis=1)


def _attn_prompt(sinks, qn, kn, va, batch, seq):
    nq = seq // ATT_QB
    ratio = ATT_QB // ATT_SB
    cur = lambda w: pl.BlockSpec((ATT_QB, w), lambda b, j: (b * nq + j, 0))
    prev = lambda w: pl.BlockSpec((ATT_SB, w), lambda b, j: (jnp.maximum((b * nq + j) * ratio - 1, 0), 0))
    return pl.pallas_call(
        _attn_prompt_kernel,
        grid=(batch, nq),
        in_specs=[pl.BlockSpec(memory_space=pltpu.SMEM), cur(A_WIDTH), prev(KV_WIDTH), cur(KV_WIDTH),
                  prev(KV_WIDTH), cur(KV_WIDTH)],
        out_specs=cur(A_WIDTH),
        out_shape=jax.ShapeDtypeStruct((batch * seq, A_WIDTH), F32),
        compiler_params=pltpu.CompilerParams(dimension_semantics=("parallel", "parallel"),
                                             vmem_limit_bytes=VMEM_LIMIT),
        name="attn_prompt",
    )(sinks, qn, kn, kn, va, va)


def _attn_sample_kernel(dec, sink_ref, q_ref, kn_ref, vn_ref, ck_ref, cv_ref, o_ref, kw_ref, vw_ref):
    scale = HEAD_DIM ** -0.5
    rows = SAMPLE_NB * dec
    knew = kn_ref[...]
    vnew = vn_ref[...]
    knew_b = knew.astype(BF16)
    vnew_b = vnew.astype(BF16)
    nrow = A_GROUP * dec
    t = lax.broadcasted_iota(jnp.int32, (nrow, WINDOW), 0) % dec
    c = lax.broadcasted_iota(jnp.int32, (nrow, WINDOW), 1)
    m_cache = c >= t
    cn = lax.broadcasted_iota(jnp.int32, (nrow, rows), 1)
    tn = lax.broadcasted_iota(jnp.int32, (nrow, rows), 0) % dec
    for i in range(SAMPLE_NB):
        q = q_ref[i * dec:(i + 1) * dec, :].astype(BF16)
        ck = ck_ref[i].astype(BF16)
        cv = cv_ref[i].astype(BF16)
        m_new = jnp.logical_and(cn // dec == i, cn % dec <= tn)
        outs = []
        for g in range(A_KV_HEADS):
            lo, hi = g * HEAD_DIM, (g + 1) * HEAD_DIM
            qs = _stack_heads(q, g)
            s_c = _dot_nt(qs, ck[:, lo:hi]) * scale
            s_n = _dot_nt(qs, knew_b[:, lo:hi]) * scale
            p_c, p_n = _softmax_sink([s_c, s_n], [m_cache, m_new], _sink_col(sink_ref, g, dec))
            o = _dot(p_c.astype(BF16), cv[:, lo:hi]) + _dot(p_n.astype(BF16), vnew_b[:, lo:hi])
            outs += [o[h * dec:(h + 1) * dec] for h in range(A_GROUP)]
        o_ref[i * dec:(i + 1) * dec, :] = jnp.concatenate(outs, axis=1)
        kw_ref[i, 0:WINDOW - dec, :] = ck_ref[i, dec:WINDOW, :]
        kw_ref[i, WINDOW - dec:WINDOW, :] = knew[i * dec:(i + 1) * dec]
        vw_ref[i, 0:WINDOW - dec, :] = cv_ref[i, dec:WINDOW, :]
        vw_ref[i, WINDOW - dec:WINDOW, :] = vnew[i * dec:(i + 1) * dec]


def _attn_sample(sinks, qn, kn, va, ck, cv, row0, dbatch, dec):
    rows = SAMPLE_NB * dec
    off = row0 // rows
    tokrow = lambda w: pl.BlockSpec((rows, w), lambda i: (off + i, 0))
    cache = pl.BlockSpec((SAMPLE_NB, WINDOW, KV_WIDTH), lambda i: (i, 0, 0))
    return pl.pallas_call(
        functools.partial(_attn_sample_kernel, dec),
        grid=(dbatch // SAMPLE_NB,),
        in_specs=[pl.BlockSpec(memory_space=pltpu.SMEM), tokrow(A_WIDTH), tokrow(KV_WIDTH), tokrow(KV_WIDTH),
                  cache, cache],
        out_specs=(pl.BlockSpec((rows, A_WIDTH), lambda i: (i, 0)), cache, cache),
        out_shape=(jax.ShapeDtypeStruct((dbatch * dec, A_WIDTH), F32),
                   jax.ShapeDtypeStruct((dbatch, WINDOW, KV_WIDTH), F32),
                   jax.ShapeDtypeStruct((dbatch, WINDOW, KV_WIDTH), F32)),
        compiler_params=pltpu.CompilerParams(dimension_semantics=("parallel",), vmem_limit_bytes=VMEM_LIMIT),
        name="attn_sample",
    )(sinks, qn, kn, va, ck, cv)


def _mlstm_kernel(nseq, L, qk_ref, v_ref, o_ref, gcol_ref, gt_ref, conv0_ref, c0_ref, n0_ref, m0_ref,
                  cw_ref, cb_ref, gbrow_ref, gbcol_ref, mng_ref, mask_ref,
                  out_ref, cst_ref, nst_ref, mst_ref, prev_ref):
    R = nseq * L
    ci = pl.program_id(1)

    @pl.when(ci == 0)
    def _():
        cst_ref[...] = c0_ref[...]
        nst_ref[...] = n0_ref[...]
        mst_ref[...] = m0_ref[...]
        if nseq == 1:
            prev_ref[...] = jnp.zeros_like(prev_ref)
            prev_ref[R - SUBLANES:R, :] = conv0_ref[0]

    raw = qk_ref[...]
    row = lax.broadcasted_iota(jnp.int32, (R, 1), 0)
    tpos = row % L
    rseq = row // L
    if nseq == 1:
        prevsrc = prev_ref[...]
    else:
        prevsrc = conv0_ref[...].reshape(R, 2 * M_WIDTH)
    acc = raw * cw_ref[CONV_W - 1:CONV_W, :] + cb_ref[...]
    for k in range(1, CONV_W):
        pshift = k if nseq == 1 else R - SUBLANES + k
        sh = jnp.where(tpos >= k, pltpu.roll(raw, k, 0), pltpu.roll(prevsrc, pshift, 0))
        acc = acc + sh * cw_ref[CONV_W - 1 - k:CONV_W - k, :]
    if nseq == 1:
        prev_ref[...] = raw
    qkc = acc * jax.nn.sigmoid(acc)

    gc = gcol_ref[...] + gbrow_ref[...]
    gr = gt_ref[0] + gbcol_ref[...]
    lsc = _log_sigmoid(gc)
    lsr = _log_sigmoid(gr)
    mb = mask_ref[...]
    maskb = mb > 0
    bcol = sum(_dot(mb, p) for p in _split3(lsc))
    brow = sum(_dot_nt(p, mb) for p in _split3(lsr))

    lane = lax.broadcasted_iota(jnp.int32, (1, LANES), 1)
    m_new = [jnp.zeros((1, LANES), F32) for _ in range(nseq)]
    for h in range(M_HEADS):
        sl = slice(h * M_HEAD_DIM, (h + 1) * M_HEAD_DIM)
        qh = qkc[:, sl]
        kh = qkc[:, M_WIDTH + h * M_HEAD_DIM:M_WIDTH + (h + 1) * M_HEAD_DIM] * (M_HEAD_DIM ** -0.5)
        vh = v_ref[:, sl]
        qb, kb, vb = qh.astype(BF16), kh.astype(BF16), vh.astype(BF16)
        ig_c = gc[:, h:h + 1]
        b_c = bcol[:, M_HEADS + h:M_HEADS + h + 1]
        ig_r = gr[h:h + 1, :]
        b_r = brow[M_HEADS + h:M_HEADS + h + 1, :]
        if nseq == 1:
            m0c = mst_ref[0][:, h:h + 1]
            n0rows = nst_ref[0, h:h + 1, :]
        else:
            m0c = jnp.zeros((R, 1), F32)
            n0rows = jnp.zeros((R, M_HEAD_DIM), F32)
            for s in range(nseq):
                m0c = jnp.where(rseq == s, mst_ref[s][:, h:h + 1], m0c)
                n0rows = jnp.where(rseq == s, nst_ref[s, h:h + 1, :], n0rows)
        dm = jnp.where(maskb, b_c - b_r + ig_r, NEG)
        a_c = b_c + m0c
        m_c = jnp.maximum(a_c, jnp.max(dm, axis=-1, keepdims=True))
        w = jnp.exp(dm - m_c)
        sc = jnp.exp(a_c - m_c)
        wqk = w * _dot_nt(qb, kb)
        if nseq == 1:
            inter = _dot_nt(qb, cst_ref[0, h].astype(BF16))
        else:
            inter = jnp.zeros((R, M_HEAD_DIM), F32)
            for s in range(nseq):
                qs = jnp.where(rseq == s, qh, 0.0).astype(BF16)
                inter = inter + _dot_nt(qs, cst_ref[s, h].astype(BF16))
        num = _dot(wqk.astype(BF16), vb) + sc * inter
        den = jnp.sum(wqk, axis=-1, keepdims=True) + sc * jnp.sum(qh * n0rows, axis=-1, keepdims=True)
        hh = num / jnp.maximum(jnp.abs(den), jnp.exp(-m_c))

        for s in range(nseq):
            e = s * L + L - 1
            m_end = m_c[e:e + 1, :]
            wend = jnp.exp(b_c[e:e + 1, :] - b_c + ig_c - m_end)
            if nseq > 1:
                wend = jnp.where(rseq == s, wend, 0.0)
            sce = jnp.exp(a_c[e:e + 1, :] - m_end)
            c_new = sce * cst_ref[s, h] + _dot_tn((vh * wend).astype(BF16), kb)
            n_new = sce * nst_ref[s, h:h + 1, :] + jnp.sum(wend * kh, axis=0, keepdims=True)
            cst_ref[s, h] = c_new
            nst_ref[s, h:h + 1, :] = n_new
            m_new[s] = jnp.where(lane == h, m_end, m_new[s])

        hn = (hh * lax.rsqrt(jnp.mean(hh * hh, axis=-1, keepdims=True) + EPS)) * mng_ref[:, sl]
        out_ref[:, sl] = jax.nn.sigmoid(o_ref[:, sl]) * hn
    for s in range(nseq):
        mst_ref[s] = m_new[s]


def _mlstm(qkm, vm, om, gcol, gt, conv0, c0, n0, m0, cw, cb, gbrow, gbcol, mng, row0, nseq, L, ngroups, nchunks):
    R = nseq * L
    off = row0 // R
    per_tile = TOK_TILE // R
    tok = lambda w: pl.BlockSpec((R, w), lambda g, c: (off + g * nchunks + c, 0))
    gt_spec = pl.BlockSpec((1, SUBLANES, R),
                           lambda g, c: ((off + g * nchunks + c) // per_tile, 0, (off + g * nchunks + c) % per_tile))
    full = lambda a: pl.BlockSpec(a.shape, lambda g, c: (0,) * a.ndim)
    st4 = pl.BlockSpec((nseq, M_HEADS, M_HEAD_DIM, M_HEAD_DIM), lambda g, c: (g, 0, 0, 0))
    st3 = pl.BlockSpec((nseq, M_HEADS, M_HEAD_DIM), lambda g, c: (g, 0, 0))
    stm = pl.BlockSpec((nseq, 1, LANES), lambda g, c: (g, 0, 0))
    conv_spec = pl.BlockSpec((nseq, SUBLANES, 2 * M_WIDTH), lambda g, c: (g, 0, 0))
    r = jnp.arange(R)
    mask = ((r[:, None] // L == r[None, :] // L) & (r[None, :] <= r[:, None])).astype(BF16)
    nstate = ngroups * nseq
    return pl.pallas_call(
        functools.partial(_mlstm_kernel, nseq, L),
        grid=(ngroups, nchunks),
        in_specs=[tok(2 * M_WIDTH), tok(M_WIDTH), tok(M_WIDTH), tok(LANES), gt_spec, conv_spec, st4, st3, stm,
                  full(cw), full(cb), full(gbrow), full(gbcol), full(mng), full(mask)],
        out_specs=(pl.BlockSpec((R, M_WIDTH), lambda g, c: (g * nchunks + c, 0)), st4, st3, stm),
        out_shape=(jax.ShapeDtypeStruct((ngroups * nchunks * R, M_WIDTH), F32),
                   jax.ShapeDtypeStruct((nstate, M_HEADS, M_HEAD_DIM, M_HEAD_DIM), F32),
                   jax.ShapeDtypeStruct((nstate, M_HEADS, M_HEAD_DIM), F32),
                   jax.ShapeDtypeStruct((nstate, 1, LANES), F32)),
        scratch_shapes=[pltpu.VMEM((R, 2 * M_WIDTH), F32)],
        compiler_params=pltpu.CompilerParams(dimension_semantics=("parallel", "arbitrary"),
                                             vmem_limit_bytes=VMEM_LIMIT),
        name="mlstm_n%d" % nseq,
    )(qkm, vm, om, gcol, gt, conv0, c0, n0, m0, cw, cb, gbrow, gbcol, mng, mask)


def _outproj_kernel(npt, ap_ref, as_ref, mp_ref, ms_ref, xp_ref, xs_ref, wo_ref, g_ref, wr_ref, br_ref,
                    h_ref, hn_ref, route_ref, cnt_ref):
    a = _pick(npt, ap_ref, as_ref).astype(BF16)
    m = _pick(npt, mp_ref, ms_ref).astype(BF16)
    h = _pick(npt, xp_ref, xs_ref) + _dot(a, wo_ref[0:A_WIDTH, :]) + _dot(m, wo_ref[A_WIDTH:A_WIDTH + M_WIDTH, :])
    h_ref[...] = h
    hn = (h * lax.rsqrt(jnp.mean(h * h, axis=-1, keepdims=True) + EPS)) * g_ref[...]
    hn_ref[...] = hn
    logits = _dot(hn.astype(BF16), wr_ref[...]) + br_ref[...]
    lane = lax.broadcasted_iota(jnp.int32, logits.shape, 1)
    route = jnp.zeros(logits.shape, F32)
    picked = jnp.zeros(logits.shape, F32)
    top0 = None
    den = None
    es = []
    for k in range(TOP_K):
        mx = jnp.max(logits, axis=-1, keepdims=True)
        idx = jnp.min(jnp.where(logits == mx, lane, LANES), axis=-1, keepdims=True)
        if k == 0:
            top0 = mx
        e = jnp.exp(mx - top0)
        den = e if den is None else den + e
        es.append(e)
        route = jnp.where(lane == TOP_K + k, idx.astype(F32), route)
        picked = jnp.where(lane == idx, 1.0, picked)
        logits = jnp.where(lane == idx, -jnp.inf, logits)
    for k in range(TOP_K):
        route = jnp.where(lane == k, es[k] / den, route)
    route_ref[...] = route

    @pl.when(pl.program_id(0) == 0)
    def _():
        cnt_ref[...] = jnp.zeros_like(cnt_ref)

    cnt_ref[0:1, :] += jnp.sum(picked, axis=0, keepdims=True)


def _outproj(a_p, a_s, m_p, m_s, xp, xs, w_out, g_ffn, wr_pad, br_pad):
    T = xp.shape[0] + xs.shape[0]
    npt = xp.shape[0] // TOK_TILE
    row = lambda w: pl.BlockSpec((TOK_TILE, w), lambda i: (i, 0))
    full = lambda a: pl.BlockSpec(a.shape, lambda i: (0,) * a.ndim)
    return pl.pallas_call(
        functools.partial(_outproj_kernel, npt),
        grid=(T // TOK_TILE,),
        in_specs=[*_split_specs(npt, TOK_TILE, A_WIDTH), *_split_specs(npt, TOK_TILE, M_WIDTH),
                  *_split_specs(npt, TOK_TILE, D_MODEL), full(w_out), full(g_ffn), full(wr_pad), full(br_pad)],
        out_specs=(row(D_MODEL), row(D_MODEL), row(LANES), pl.BlockSpec((SUBLANES, LANES), lambda i: (0, 0))),
        out_shape=(jax.ShapeDtypeStruct((T, D_MODEL), F32), jax.ShapeDtypeStruct((T, D_MODEL), F32),
                   jax.ShapeDtypeStruct((T, LANES), F32), jax.ShapeDtypeStruct((SUBLANES, LANES), F32)),
        compiler_params=pltpu.CompilerParams(dimension_semantics=("arbitrary",), vmem_limit_bytes=VMEM_LIMIT),
        name="outproj_router",
    )(a_p, a_s, m_p, m_s, xp, xs, w_out, g_ffn, wr_pad, br_pad)


def _route_kernel(nblk, route_ref, cnt_ref, lstrict_ref, ustrict_ref, dest_ref, blk_ref, info_ref, carry_ref):
    i = pl.program_id(0)
    lane = lax.broadcasted_iota(jnp.int32, (1, LANES), 1)
    cnt = cnt_ref[0:1, :]
    nb_e = jnp.floor((cnt + (MOE_BLOCK - 1.0)) * (1.0 / MOE_BLOCK))
    u = ustrict_ref[...]
    bstart = sum(_dot(p, u) for p in _split3(jnp.broadcast_to(nb_e, (SUBLANES, LANES))))[0:1, :]
    bend = bstart + nb_e
    row_start = bstart * float(MOE_BLOCK)

    @pl.when(i == 0)
    def _():
        carry_ref[...] = jnp.zeros_like(carry_ref)
        bi = lax.broadcasted_iota(jnp.int32, (nblk, LANES), 0).astype(F32)
        done = jnp.logical_and(bend <= bi, lane < N_EXPERTS)
        be = jnp.minimum(jnp.sum(jnp.where(done, 1.0, 0.0), axis=-1, keepdims=True), N_EXPERTS - 1.0)
        blk_ref[...] = jnp.broadcast_to(be, (nblk, LANES)).astype(jnp.int32)
        info = jnp.zeros((SUBLANES, LANES), F32)
        srow = lax.broadcasted_iota(jnp.int32, (SUBLANES, LANES), 0)
        info = jnp.where(srow == 0, row_start + cnt, info)
        info = jnp.where(srow == 1, nb_e * float(MOE_BLOCK) - cnt, info)
        info = jnp.where(srow == 2, bend, info)
        info_ref[...] = info.astype(jnp.int32)

    r = route_ref[...]
    lane_f = lane.astype(F32)
    sel = [lane_f == r[:, TOP_K + k:TOP_K + k + 1] for k in range(TOP_K)]
    oh = jnp.zeros(r.shape, F32)
    for k in range(TOP_K):
        oh = jnp.where(sel[k], 1.0, oh)
    before = _dot(lstrict_ref[...], oh.astype(BF16)) + carry_ref[...] + row_start
    dest = jnp.zeros(r.shape, jnp.int32)
    for k in range(TOP_K):
        d = jnp.sum(jnp.where(sel[k], before, 0.0), axis=-1, keepdims=True)
        dest = jnp.where(lane == k, d.astype(jnp.int32), dest)
    dest_ref[...] = dest
    carry_ref[...] += jnp.sum(oh, axis=0, keepdims=True)


def _route_tables(route, cnt, nblk):
    T = route.shape[0]
    a = jnp.arange(TOK_TILE)
    lstrict = (a[:, None] > a[None, :]).astype(BF16)
    b = jnp.arange(LANES)
    ustrict = (b[:, None] < b[None, :]).astype(BF16)
    full = lambda x: pl.BlockSpec(x.shape, lambda i: (0,) * x.ndim)
    return pl.pallas_call(
        functools.partial(_route_kernel, nblk),
        grid=(T // TOK_TILE,),
        in_specs=[pl.BlockSpec((TOK_TILE, LANES), lambda i: (i, 0)), full(cnt), full(lstrict), full(ustrict)],
        out_specs=(pl.BlockSpec((TOK_TILE, LANES), lambda i: (i, 0)),
                   pl.BlockSpec((nblk, LANES), lambda i: (0, 0)),
                   pl.BlockSpec((SUBLANES, LANES), lambda i: (0, 0))),
        out_shape=(jax.ShapeDtypeStruct((T, LANES), jnp.int32), jax.ShapeDtypeStruct((nblk, LANES), jnp.int32),
                   jax.ShapeDtypeStruct((SUBLANES, LANES), jnp.int32)),
        scratch_shapes=[pltpu.VMEM((1, LANES), F32)],
        compiler_params=pltpu.CompilerParams(dimension_semantics=("arbitrary",), vmem_limit_bytes=VMEM_LIMIT),
        name="route_tables",
    )(route, cnt, lstrict, ustrict)


def _dispatch_kernel(padrow_ref, npad_ref, nu_ref, dest_ref, hn_ref, xs_hbm, zbuf, sem, zsem):
    i = pl.program_id(0)

    @pl.when(i == 0)
    def _():
        zbuf[...] = jnp.zeros_like(zbuf)
        nblk = xs_hbm.shape[0] // MOE_BLOCK

        def tail_start(b, c):
            pltpu.make_async_copy(zbuf, xs_hbm.at[pl.ds(b * MOE_BLOCK, MOE_BLOCK), :], zsem).start()
            return c

        def tail_wait(b, c):
            pltpu.make_async_copy(zbuf, xs_hbm.at[pl.ds(b * MOE_BLOCK, MOE_BLOCK), :], zsem).wait()
            return c

        lax.fori_loop(nu_ref[0], nblk, tail_start, 0)
        lax.fori_loop(nu_ref[0], nblk, tail_wait, 0)

        def per_expert(e, carry):
            base = padrow_ref[e]
            n = npad_ref[e]

            def start(r, c):
                pltpu.make_async_copy(zbuf.at[pl.ds(0, 1), :], xs_hbm.at[pl.ds(base + r, 1), :], zsem).start()
                return c

            def wait(r, c):
                pltpu.make_async_copy(zbuf.at[pl.ds(0, 1), :], xs_hbm.at[pl.ds(base + r, 1), :], zsem).wait()
                return c

            lax.fori_loop(0, n, start, 0)
            lax.fori_loop(0, n, wait, 0)
            return carry

        lax.fori_loop(0, N_EXPERTS, per_expert, 0)

    def body(j, carry):
        for u in range(SUBLANES):
            t = j * SUBLANES + u
            for k in range(TOP_K):
                d = dest_ref[0, 0, t * TOP_K + k]
                pltpu.make_async_copy(hn_ref.at[pl.ds(t, 1), :], xs_hbm.at[pl.ds(d, 1), :], sem).start(
                    priority=(u * TOP_K + k) % 2)
        return carry

    lax.fori_loop(0, TOK_TILE // SUBLANES, body, 0)
    for _ in range(TOP_K):
        pltpu.make_async_copy(hn_ref, xs_hbm.at[pl.ds(0, TOK_TILE), :], sem).wait()


def _dispatch(padrow, npad, nused, dest_tiles, hn, n_rows):
    T = hn.shape[0]
    grid_spec = pltpu.PrefetchScalarGridSpec(
        num_scalar_prefetch=3,
        grid=(T // TOK_TILE,),
        in_specs=[pl.BlockSpec((1, 1, TOK_TILE * TOP_K), lambda i, *_: (i, 0, 0), memory_space=pltpu.SMEM),
                  pl.BlockSpec((TOK_TILE, D_MODEL), lambda i, *_: (i, 0))],
        out_specs=pl.BlockSpec(memory_space=pl.ANY),
        scratch_shapes=[pltpu.VMEM((MOE_BLOCK, D_MODEL), F32), pltpu.SemaphoreType.DMA(()),
                        pltpu.SemaphoreType.DMA(())],
    )
    return pl.pallas_call(
        _dispatch_kernel,
        grid_spec=grid_spec,
        out_shape=jax.ShapeDtypeStruct((n_rows, D_MODEL), F32),
        compiler_params=pltpu.CompilerParams(dimension_semantics=("arbitrary",), vmem_limit_bytes=VMEM_LIMIT),
        name="moe_dispatch",
    )(padrow, npad, nused, dest_tiles, hn)


def _moe_kernel(be_ref, nu_ref, x_ref, wgu_ref, bgu_ref, wd_ref, bd_ref, y_ref):
    del be_ref
    i = pl.program_id(0)

    @pl.when(i < nu_ref[0])
    def _():
        x = x_ref[...].astype(BF16)
        hb = _dot(x, wgu_ref[0]) + bgu_ref[0]
        glu = jnp.minimum(hb[:, :D_FF], SWIGLU_LIMIT)
        lin = jnp.clip(hb[:, D_FF:], -SWIGLU_LIMIT, SWIGLU_LIMIT)
        act = glu * jax.nn.sigmoid(SWIGLU_ALPHA * glu) * (lin + 1.0)
        y_ref[...] = _dot(act.astype(BF16), wd_ref[0]) + bd_ref[0]

    @pl.when(i >= nu_ref[0])
    def _():
        y_ref[...] = jnp.zeros_like(y_ref)


def _moe_blocks(block_e, nused, xs, wgu, bgu, wd, bd):
    nblk = block_e.shape[0]
    grid_spec = pltpu.PrefetchScalarGridSpec(
        num_scalar_prefetch=2,
        grid=(nblk,),
        in_specs=[
            pl.BlockSpec((MOE_BLOCK, D_MODEL), lambda i, be, nu: (i, 0)),
            pl.BlockSpec((1, D_MODEL, 2 * D_FF), lambda i, be, nu: (be[i], 0, 0)),
            pl.BlockSpec((1, 1, 2 * D_FF), lambda i, be, nu: (be[i], 0, 0)),
            pl.BlockSpec((1, D_FF, D_MODEL), lambda i, be, nu: (be[i], 0, 0)),
            pl.BlockSpec((1, 1, D_MODEL), lambda i, be, nu: (be[i], 0, 0)),
        ],
        out_specs=pl.BlockSpec((MOE_BLOCK, D_MODEL), lambda i, be, nu: (i, 0)),
    )
    return pl.pallas_call(
        _moe_kernel,
        grid_spec=grid_spec,
        out_shape=jax.ShapeDtypeStruct(xs.shape, F32),
        compiler_params=pltpu.CompilerParams(dimension_semantics=("arbitrary",), vmem_limit_bytes=VMEM_LIMIT),
        name="moe_blocks",
    )(block_e, nused, xs, wgu, bgu, wd, bd)


def _combine_kernel(npt, destc_ref, destn_ref, h_ref, route_ref, ys_hbm, yp_ref, ysm_ref, gbuf, gsem):
    i = pl.program_id(0)
    nt = pl.num_programs(0)
    slot = i % 2

    def issue(dest_ref, s):
        def body(j, carry):
            for u in range(SUBLANES):
                t = j * SUBLANES + u
                for k in range(TOP_K):
                    d = dest_ref[0, 0, t * TOP_K + k]
                    pltpu.make_async_copy(ys_hbm.at[pl.ds(d, 1), :], gbuf.at[s, k, pl.ds(t, 1), :],
                                          gsem.at[s]).start(priority=(u * TOP_K + k) % 2)
            return carry
        lax.fori_loop(0, CMB_TILE // SUBLANES, body, 0)

    @pl.when(i == 0)
    def _():
        issue(destc_ref, 0)

    @pl.when(i + 1 < nt)
    def _():
        issue(destn_ref, 1 - slot)

    for k in range(TOP_K):
        pltpu.make_async_copy(ys_hbm.at[pl.ds(0, CMB_TILE), :], gbuf.at[slot, k], gsem.at[slot]).wait()
    r = route_ref[...]
    moe = gbuf[slot, 0] * r[:, 0:1]
    for k in range(1, TOP_K):
        moe = moe + gbuf[slot, k] * r[:, k:k + 1]
    y = h_ref[...] + moe

    @pl.when(i < npt)
    def _():
        yp_ref[...] = y

    @pl.when(i >= npt)
    def _():
        ysm_ref[...] = y


def _combine(dest_tiles, h, route, ys, n_prompt_rows):
    T = h.shape[0]
    nt = T // CMB_TILE
    npt = n_prompt_rows // CMB_TILE
    smem_blk = lambda imap: pl.BlockSpec((1, 1, CMB_TILE * TOP_K), imap, memory_space=pltpu.SMEM)
    return pl.pallas_call(
        functools.partial(_combine_kernel, npt),
        grid=(nt,),
        in_specs=[smem_blk(lambda i: (i, 0, 0)), smem_blk(lambda i: (jnp.minimum(i + 1, nt - 1), 0, 0)),
                  pl.BlockSpec((CMB_TILE, D_MODEL), lambda i: (i, 0)),
                  pl.BlockSpec((CMB_TILE, LANES), lambda i: (i, 0)),
                  pl.BlockSpec(memory_space=pl.ANY)],
        out_specs=(pl.BlockSpec((CMB_TILE, D_MODEL), lambda i: (jnp.minimum(i, npt - 1), 0)),
                   pl.BlockSpec((CMB_TILE, D_MODEL), lambda i: (jnp.maximum(i - npt, 0), 0))),
        out_shape=(jax.ShapeDtypeStruct((n_prompt_rows, D_MODEL), F32),
                   jax.ShapeDtypeStruct((T - n_prompt_rows, D_MODEL), F32)),
        scratch_shapes=[pltpu.VMEM((2, TOP_K, CMB_TILE, D_MODEL), F32), pltpu.SemaphoreType.DMA((2,))],
        compiler_params=pltpu.CompilerParams(dimension_semantics=("arbitrary",), vmem_limit_bytes=VMEM_LIMIT),
        name="moe_combine",
    )(dest_tiles, dest_tiles, h, route, ys)


def kernel(x_prompt, x_sample, cache_k_win, cache_v_win, state_conv, state_C, state_n, state_m, g_attn, w_in, b_i,
           b_f, q_norm_g, k_norm_g, sinks, conv_w, conv_b, m_norm_g, w_out, g_ffn, w_router, b_router, w_gate_up,
           b_gate_up, w_down, b_down):
    depth = g_attn.shape[0]
    assert depth == 1
    B, S, _ = x_prompt.shape
    DB, DS, _ = x_sample.shape
    TP = B * S
    TS = DB * DS
    T = TP + TS
    assert T % TOK_TILE == 0 and TP % TOK_TILE == 0 and S % ATT_QB == 0 and S % PROMPT_CHUNK == 0
    assert DS == SUBLANES and DB % SAMPLE_NB == 0 and (SAMPLE_NB * DS) == LANES
    l = 0

    xp = x_prompt.reshape(TP, D_MODEL)
    xs = x_sample.reshape(TS, D_MODEL)

    w_pad = jnp.pad(w_in[l], ((0, 0), (0, IN_PAD - w_in.shape[2]))).astype(BF16)
    wgt = jnp.transpose(w_in[l][:, GATE_COL:GATE_COL + 2 * M_HEADS]).astype(BF16)
    gi = jnp.arange(A_WIDTH) // HEAD_DIM
    gmat = (gi[:, None] == gi[None, :]).astype(BF16)
    qg = jnp.tile(q_norm_g[l], A_HEADS).reshape(1, A_WIDTH)
    kg = jnp.tile(k_norm_g[l], A_KV_HEADS).reshape(1, KV_WIDTH)
    gbias = jnp.concatenate([b_i[l], b_f[l]])
    gbrow = jnp.pad(gbias, (0, LANES - 2 * M_HEADS)).reshape(1, LANES)
    gbcol = gbias.reshape(2 * M_HEADS, 1)
    mng = m_norm_g[l].reshape(1, M_WIDTH)
    cw = conv_w[l]
    cb = conv_b[l].reshape(1, 2 * M_WIDTH)
    wr_pad = jnp.pad(w_router[l], ((0, 0), (0, LANES - N_EXPERTS))).astype(BF16)
    br_pad = jnp.concatenate([b_router[l], jnp.full((LANES - N_EXPERTS,), NEG, F32)]).reshape(1, LANES)

    qn, kn, va, qkm, vm, om, gcol, gt = _inproj(xp, xs, g_attn[l].reshape(1, D_MODEL), w_pad, wgt, gmat, qg, kg)

    a_p = _attn_prompt(sinks[l], qn, kn, va, B, S)
    ck = cache_k_win[l].reshape(DB, WINDOW, KV_WIDTH)
    cv = cache_v_win[l].reshape(DB, WINDOW, KV_WIDTH)
    a_s, kwin_s, vwin_s = _attn_sample(sinks[l], qn, kn, va, ck, cv, TP, DB, DS)

    zc = jnp.zeros((B, SUBLANES, 2 * M_WIDTH), F32)
    m_p, C_p, n_p, mm_p = _mlstm(
        qkm, vm, om, gcol, gt, zc,
        jnp.zeros((B, M_HEADS, M_HEAD_DIM, M_HEAD_DIM), F32), jnp.zeros((B, M_HEADS, M_HEAD_DIM), F32),
        jnp.full((B, 1, LANES), NEG, F32), cw, cb, gbrow, gbcol, mng,
        row0=0, nseq=1, L=PROMPT_CHUNK, ngroups=B, nchunks=S // PROMPT_CHUNK)
    conv_s0 = jnp.pad(state_conv[l], ((0, 0), (SUBLANES - (CONV_W - 1), 0), (0, 0)))
    m0_s = jnp.pad(state_m[l], ((0, 0), (0, LANES - M_HEADS))).reshape(DB, 1, LANES)
    m_s, C_s, n_s, mm_s = _mlstm(
        qkm, vm, om, gcol, gt, conv_s0, state_C[l], state_n[l], m0_s, cw, cb, gbrow, gbcol, mng,
        row0=TP, nseq=SAMPLE_NB, L=DS, ngroups=DB // SAMPLE_NB, nchunks=1)

    h, hn, route, cnt = _outproj(a_p, a_s, m_p, m_s, xp, xs, w_out[l].astype(BF16), g_ffn[l].reshape(1, D_MODEL),
                                 wr_pad, br_pad)

    nblk = T * TOP_K // MOE_BLOCK + N_EXPERTS
    dest, blk, info = _route_tables(route, cnt, nblk)
    block_e = blk[:, 0]
    padrow = info[0, :N_EXPERTS]
    npad = info[1, :N_EXPERTS]
    nused = info[2, N_EXPERTS - 1:N_EXPERTS]
    dest4 = dest[:, :TOP_K]
    xrows = _dispatch(padrow, npad, nused, dest4.reshape(T // TOK_TILE, 1, TOK_TILE * TOP_K), hn, nblk * MOE_BLOCK)
    yrows = _moe_blocks(block_e, nused, xrows,
                        w_gate_up[l].astype(BF16), b_gate_up[l].reshape(N_EXPERTS, 1, 2 * D_FF),
                        w_down[l].astype(BF16), b_down[l].reshape(N_EXPERTS, 1, D_MODEL))
    y_p, y_s = _combine(dest4.reshape(T // CMB_TILE, 1, CMB_TILE * TOP_K), h, route, yrows, TP)

    y_p = y_p.reshape(B, S, D_MODEL)
    y_s = y_s.reshape(DB, DS, D_MODEL)
    kn_p = kn[:TP].reshape(B, S, A_KV_HEADS, HEAD_DIM)
    va_p = va[:TP].reshape(B, S, A_KV_HEADS, HEAD_DIM)
    qkm_p = qkm[:TP].reshape(B, S, 2 * M_WIDTH)
    qkm_s = qkm[TP:].reshape(DB, DS, 2 * M_WIDTH)
    return (y_p, y_s,
            kn_p[:, -WINDOW:][None], va_p[:, -WINDOW:][None], qkm_p[:, -(CONV_W - 1):][None],
            C_p[None], n_p[None], mm_p[:, 0, :M_HEADS][None],
            kwin_s.reshape(DB, WINDOW, A_KV_HEADS, HEAD_DIM)[None],
            vwin_s.reshape(DB, WINDOW, A_KV_HEADS, HEAD_DIM)[None],
            qkm_s[:, -(CONV_W - 1):][None],
            C_s[None], n_s[None], mm_s[:, 0, :M_HEADS][None])
```

```python
import functools

import jax
import jax.numpy as jnp
from jax import lax
from jax.experimental import pallas as pl
from jax.experimental.pallas import tpu as pltpu

F32 = jnp.float32
BF16 = jnp.bfloat16

D_MODEL = 1024
HEAD_DIM = 64
A_HEADS = 8
A_KV_HEADS = 2
A_GROUP = A_HEADS // A_KV_HEADS
A_WIDTH = A_HEADS * HEAD_DIM
KV_WIDTH = A_KV_HEADS * HEAD_DIM
WINDOW = 128
M_HEADS = 4
M_HEAD_DIM = 128
M_WIDTH = M_HEADS * M_HEAD_DIM
CONV_W = 4
N_EXPERTS = 32
TOP_K = 4
D_FF = D_MODEL
SWIGLU_LIMIT = 7.0
SWIGLU_ALPHA = 1.702
MOE_BLOCK = 512
EPS = 1e-6
NEG = -1e30

LANES = 128
SUBLANES = 8
GATE_COL = A_WIDTH + 2 * KV_WIDTH + 4 * M_WIDTH
IN_PAD = GATE_COL + LANES
TOK_TILE = 512
ATT_QB = 512
ATT_SB = 128
SAMPLE_NB = 16
PROMPT_CHUNK = 256
CMB_TILE = 256
VMEM_LIMIT = 48 * 1024 * 1024


def _dot(a, b):
    return jnp.dot(a, b, preferred_element_type=F32)


def _dot_nt(a, b):
    return lax.dot_general(a, b, (((1,), (1,)), ((), ())), preferred_element_type=F32)


def _dot_tn(a, b):
    return lax.dot_general(a, b, (((0,), (0,)), ((), ())), preferred_element_type=F32)


def _split3(x):
    hi = x.astype(BF16)
    r1 = x - hi.astype(F32)
    mid = r1.astype(BF16)
    lo = (r1 - mid.astype(F32)).astype(BF16)
    return hi, mid, lo


def _log_sigmoid(x):
    return jnp.minimum(x, 0.0) - jnp.log1p(jnp.exp(-jnp.abs(x)))


def _load_row_tiles(ref2, rows):
    return jnp.concatenate([ref2[pl.ds(s, rows, stride=SUBLANES), :] for s in range(SUBLANES)], axis=1)


def _store_row_tiles(ref2, rows, val):
    for s in range(SUBLANES):
        ref2[pl.ds(s, rows, stride=SUBLANES), :] = val[:, s * LANES:(s + 1) * LANES]


def _row_tile(ref2, idx):
    return ref2.at[pl.ds(pl.multiple_of(idx * SUBLANES, SUBLANES), SUBLANES), :]


def _row_tiles(ref2, first, n):
    return ref2.at[pl.ds(pl.multiple_of(first * SUBLANES, SUBLANES), n * SUBLANES), :]


def _pick(n_prompt_tiles, p_ref, s_ref):
    return jnp.where(pl.program_id(0) < n_prompt_tiles, p_ref[...], s_ref[...])


def _split_specs(n_prompt_tiles, rows, width):
    return (pl.BlockSpec((rows, width), lambda i, *_: (jnp.minimum(i, n_prompt_tiles - 1), 0)),
            pl.BlockSpec((rows, width), lambda i, *_: (jnp.maximum(i - n_prompt_tiles, 0), 0)))


def _inproj_kernel(npt, xp_ref, xs_ref, g_ref, w_ref, wgt_ref, gmat_ref, qg_ref, kg_ref,
                   qn_ref, kn_ref, va_ref, qkm_ref, vm_ref, om_ref, gcol_ref, gt_ref):
    x = _pick(npt, xp_ref, xs_ref)
    ms = jnp.mean(x * x, axis=-1, keepdims=True)
    xn = ((x * lax.rsqrt(ms + EPS)) * g_ref[...]).astype(BF16)

    def seg(lo, hi):
        return _dot(xn, w_ref[:, lo:hi])

    def head_norm(z, gmat, g):
        hi, mid, lo = _split3(z * z)
        ss = _dot(hi, gmat) + _dot(mid, gmat) + _dot(lo, gmat)
        return (z * lax.rsqrt(ss * (1.0 / HEAD_DIM) + EPS)) * g

    o0 = A_WIDTH
    o1 = o0 + KV_WIDTH
    o2 = o1 + KV_WIDTH
    o3 = o2 + 2 * M_WIDTH
    o4 = o3 + M_WIDTH
    o5 = o4 + M_WIDTH
    qn_ref[...] = head_norm(seg(0, o0), gmat_ref[...], qg_ref[...])
    kn_ref[...] = head_norm(seg(o0, o1), gmat_ref[:KV_WIDTH, :KV_WIDTH], kg_ref[...])
    va_ref[...] = seg(o1, o2)
    qkm_ref[...] = seg(o2, o3)
    vm_ref[...] = seg(o3, o4)
    om_ref[...] = seg(o4, o5)
    gcol_ref[...] = seg(o5, o5 + LANES)
    gt_ref[0] = _dot_nt(wgt_ref[...], xn)


def _inproj(xp, xs, g_attn, w_pad, wgt, gmat, qg, kg):
    T = xp.shape[0] + xs.shape[0]
    nt = T // TOK_TILE
    npt = xp.shape[0] // TOK_TILE
    row = lambda w: pl.BlockSpec((TOK_TILE, w), lambda i: (i, 0))
    full = lambda a: pl.BlockSpec(a.shape, lambda i: (0,) * a.ndim)
    out_shape = (
        jax.ShapeDtypeStruct((T, A_WIDTH), F32),
        jax.ShapeDtypeStruct((T, KV_WIDTH), F32),
        jax.ShapeDtypeStruct((T, KV_WIDTH), F32),
        jax.ShapeDtypeStruct((T, 2 * M_WIDTH), F32),
        jax.ShapeDtypeStruct((T, M_WIDTH), F32),
        jax.ShapeDtypeStruct((T, M_WIDTH), F32),
        jax.ShapeDtypeStruct((T, LANES), F32),
        jax.ShapeDtypeStruct((nt, SUBLANES, TOK_TILE), F32),
    )
    out_specs = (row(A_WIDTH), row(KV_WIDTH), row(KV_WIDTH), row(2 * M_WIDTH), row(M_WIDTH), row(M_WIDTH),
                 row(LANES), pl.BlockSpec((1, SUBLANES, TOK_TILE), lambda i: (i, 0, 0)))
    return pl.pallas_call(
        functools.partial(_inproj_kernel, npt),
        grid=(nt,),
        in_specs=[*_split_specs(npt, TOK_TILE, D_MODEL), full(g_attn), full(w_pad), full(wgt), full(gmat),
                  full(qg), full(kg)],
        out_specs=out_specs,
        out_shape=out_shape,
        compiler_params=pltpu.CompilerParams(dimension_semantics=("parallel",), vmem_limit_bytes=VMEM_LIMIT),
        name="inproj",
    )(xp, xs, g_attn, w_pad, wgt, gmat, qg, kg)


def _softmax_sink(pieces, masks, sink_col):
    masked = [jnp.where(mk, s, NEG) for s, mk in zip(pieces, masks)]
    m = sink_col
    for s in masked:
        m = jnp.maximum(m, jnp.max(s, axis=-1, keepdims=True))
    ps = [jnp.exp(s - m) for s in masked]
    den = jnp.exp(sink_col - m)
    for p in ps:
        den = den + jnp.sum(p, axis=-1, keepdims=True)
    inv = 1.0 / den
    return [p * inv for p in ps]


def _stack_heads(q, g):
    return jnp.concatenate([q[:, (A_GROUP * g + i) * HEAD_DIM:(A_GROUP * g + i + 1) * HEAD_DIM]
                            for i in range(A_GROUP)], axis=0)


def _sink_col(sink_ref, g, rows_per_head):
    r = lax.broadcasted_iota(jnp.int32, (A_GROUP * rows_per_head, 1), 0)
    col = jnp.zeros((A_GROUP * rows_per_head, 1), F32)
    for i in range(A_GROUP):
        col = jnp.where(r // rows_per_head == i, sink_ref[A_GROUP * g + i], col)
    return col


def _attn_prompt_kernel(sink_ref, q_ref, kp_ref, kc_ref, vp_ref, vc_ref, o_ref):
    j = pl.program_id(1)
    scale = HEAD_DIM ** -0.5
    kc = kc_ref[...].astype(BF16)
    vc = vc_ref[...].astype(BF16)
    kp = kp_ref[...].astype(BF16)
    vp = vp_ref[...].astype(BF16)
    nrow = A_GROUP * ATT_SB
    r = lax.broadcasted_iota(jnp.int32, (nrow, ATT_SB), 0) % ATT_SB
    c = lax.broadcasted_iota(jnp.int32, (nrow, ATT_SB), 1)
    band_prev = c >= r
    band_cur = c <= r
    for sb in range(ATT_QB // ATT_SB):
        q = q_ref[sb * ATT_SB:(sb + 1) * ATT_SB, :].astype(BF16)
        if sb == 0:
            kprev, vprev = kp, vp
            mprev = jnp.logical_and(band_prev, j > 0)
        else:
            kprev = kc[(sb - 1) * ATT_SB:sb * ATT_SB]
            vprev = vc[(sb - 1) * ATT_SB:sb * ATT_SB]
            mprev = band_prev
        kcur = kc[sb * ATT_SB:(sb + 1) * ATT_SB]
        vcur = vc[sb * ATT_SB:(sb + 1) * ATT_SB]
        outs = []
        for g in range(A_KV_HEADS):
            lo, hi = g * HEAD_DIM, (g + 1) * HEAD_DIM
            qs = _stack_heads(q, g)
            s_p = _dot_nt(qs, kprev[:, lo:hi]) * scale
            s_c = _dot_nt(qs, kcur[:, lo:hi]) * scale
            p_p, p_c = _softmax_sink([s_p, s_c], [mprev, band_cur], _sink_col(sink_ref, g, ATT_SB))
            o = _dot(p_p.astype(BF16), vprev[:, lo:hi]) + _dot(p_c.astype(BF16), vcur[:, lo:hi])
            outs += [o[i * ATT_SB:(i + 1) * ATT_SB] for i in range(A_GROUP)]
        o_ref[sb * ATT_SB:(sb + 1) * ATT_SB, :] = jnp.concatenate(outs, axis=1)


def _attn_prompt(sinks, qn, kn, va, batch, seq):
    nq = seq // ATT_QB
    ratio = ATT_QB // ATT_SB
    cur = lambda w: pl.BlockSpec((ATT_QB, w), lambda b, j: (b * nq + j, 0))
    prev = lambda w: pl.BlockSpec((ATT_SB, w), lambda b, j: (jnp.maximum((b * nq + j) * ratio - 1, 0), 0))
    return pl.pallas_call(
        _attn_prompt_kernel,
        grid=(batch, nq),
        in_specs=[pl.BlockSpec(memory_space=pltpu.SMEM), cur(A_WIDTH), prev(KV_WIDTH), cur(KV_WIDTH),
                  prev(KV_WIDTH), cur(KV_WIDTH)],
        out_specs=cur(A_WIDTH),
        out_shape=jax.ShapeDtypeStruct((batch * seq, A_WIDTH), F32),
        compiler_params=pltpu.CompilerParams(dimension_semantics=("parallel", "parallel"),
                                             vmem_limit_bytes=VMEM_LIMIT),
        name="attn_prompt",
    )(sinks, qn, kn, kn, va, va)


def _attn_sample_kernel(dec, sink_ref, q_ref, kn_ref, vn_ref, ck_ref, cv_ref, o_ref, kw_ref, vw_ref):
    scale = HEAD_DIM ** -0.5
    rows = SAMPLE_NB * dec
    knew = kn_ref[...]
    vnew = vn_ref[...]
    knew_b = knew.astype(BF16)
    vnew_b = vnew.astype(BF16)
    nrow = A_GROUP * dec
    t = lax.broadcasted_iota(jnp.int32, (nrow, WINDOW), 0) % dec
    c = lax.broadcasted_iota(jnp.int32, (nrow, WINDOW), 1)
    m_cache = c >= t
    cn = lax.broadcasted_iota(jnp.int32, (nrow, rows), 1)
    tn = lax.broadcasted_iota(jnp.int32, (nrow, rows), 0) % dec
    for i in range(SAMPLE_NB):
        q = q_ref[i * dec:(i + 1) * dec, :].astype(BF16)
        ck = ck_ref[i].astype(BF16)
        cv = cv_ref[i].astype(BF16)
        m_new = jnp.logical_and(cn // dec == i, cn % dec <= tn)
        outs = []
        for g in range(A_KV_HEADS):
            lo, hi = g * HEAD_DIM, (g + 1) * HEAD_DIM
            qs = _stack_heads(q, g)
            s_c = _dot_nt(qs, ck[:, lo:hi]) * scale
            s_n = _dot_nt(qs, knew_b[:, lo:hi]) * scale
            p_c, p_n = _softmax_sink([s_c, s_n], [m_cache, m_new], _sink_col(sink_ref, g, dec))
            o = _dot(p_c.astype(BF16), cv[:, lo:hi]) + _dot(p_n.astype(BF16), vnew_b[:, lo:hi])
            outs += [o[h * dec:(h + 1) * dec] for h in range(A_GROUP)]
        o_ref[i * dec:(i + 1) * dec, :] = jnp.concatenate(outs, axis=1)
        kw_ref[i, 0:WINDOW - dec, :] = ck_ref[i, dec:WINDOW, :]
        kw_ref[i, WINDOW - dec:WINDOW, :] = knew[i * dec:(i + 1) * dec]
        vw_ref[i, 0:WINDOW - dec, :] = cv_ref[i, dec:WINDOW, :]
        vw_ref[i, WINDOW - dec:WINDOW, :] = vnew[i * dec:(i + 1) * dec]


def _attn_sample(sinks, qn, kn, va, ck, cv, row0, dbatch, dec):
    rows = SAMPLE_NB * dec
    off = row0 // rows
    tokrow = lambda w: pl.BlockSpec((rows, w), lambda i: (off + i, 0))
    cache = pl.BlockSpec((SAMPLE_NB, WINDOW, KV_WIDTH), lambda i: (i, 0, 0))
    return pl.pallas_call(
        functools.partial(_attn_sample_kernel, dec),
        grid=(dbatch // SAMPLE_NB,),
        in_specs=[pl.BlockSpec(memory_space=pltpu.SMEM), tokrow(A_WIDTH), tokrow(KV_WIDTH), tokrow(KV_WIDTH),
                  cache, cache],
        out_specs=(pl.BlockSpec((rows, A_WIDTH), lambda i: (i, 0)), cache, cache),
        out_shape=(jax.ShapeDtypeStruct((dbatch * dec, A_WIDTH), F32),
                   jax.ShapeDtypeStruct((dbatch, WINDOW, KV_WIDTH), F32),
                   jax.ShapeDtypeStruct((dbatch, WINDOW, KV_WIDTH), F32)),
        compiler_params=pltpu.CompilerParams(dimension_semantics=("parallel",), vmem_limit_bytes=VMEM_LIMIT),
        name="attn_sample",
    )(sinks, qn, kn, va, ck, cv)


def _mlstm_kernel(nseq, L, qk_ref, v_ref, o_ref, gcol_ref, gt_ref, conv0_ref, c0_ref, n0_ref, m0_ref,
                  cw_ref, cb_ref, gbrow_ref, gbcol_ref, mng_ref, mask_ref,
                  out_ref, cst_ref, nst_ref, mst_ref, prev_ref):
    R = nseq * L
    ci = pl.program_id(1)

    @pl.when(ci == 0)
    def _():
        cst_ref[...] = c0_ref[...]
        nst_ref[...] = n0_ref[...]
        mst_ref[...] = m0_ref[...]
        if nseq == 1:
            prev_ref[...] = jnp.zeros_like(prev_ref)
            prev_ref[R - SUBLANES:R, :] = conv0_ref[0]

    raw = qk_ref[...]
    row = lax.broadcasted_iota(jnp.int32, (R, 1), 0)
    tpos = row % L
    rseq = row // L
    if nseq == 1:
        prevsrc = prev_ref[...]
    else:
        prevsrc = conv0_ref[...].reshape(R, 2 * M_WIDTH)
    acc = raw * cw_ref[CONV_W - 1:CONV_W, :] + cb_ref[...]
    for k in range(1, CONV_W):
        pshift = k if nseq == 1 else R - SUBLANES + k
        sh = jnp.where(tpos >= k, pltpu.roll(raw, k, 0), pltpu.roll(prevsrc, pshift, 0))
        acc = acc + sh * cw_ref[CONV_W - 1 - k:CONV_W - k, :]
    if nseq == 1:
        prev_ref[...] = raw
    qkc = acc * jax.nn.sigmoid(acc)

    gc = gcol_ref[...] + gbrow_ref[...]
    gr = gt_ref[0] + gbcol_ref[...]
    lsc = _log_sigmoid(gc)
    lsr = _log_sigmoid(gr)
    mb = mask_ref[...]
    maskb = mb > 0
    bcol = sum(_dot(mb, p) for p in _split3(lsc))
    brow = sum(_dot_nt(p, mb) for p in _split3(lsr))

    lane = lax.broadcasted_iota(jnp.int32, (1, LANES), 1)
    m_new = [jnp.zeros((1, LANES), F32) for _ in range(nseq)]
    for h in range(M_HEADS):
        sl = slice(h * M_HEAD_DIM, (h + 1) * M_HEAD_DIM)
        qh = qkc[:, sl]
        kh = qkc[:, M_WIDTH + h * M_HEAD_DIM:M_WIDTH + (h + 1) * M_HEAD_DIM] * (M_HEAD_DIM ** -0.5)
        vh = v_ref[:, sl]
        qb, kb, vb = qh.astype(BF16), kh.astype(BF16), vh.astype(BF16)
        ig_c = gc[:, h:h + 1]
        b_c = bcol[:, M_HEADS + h:M_HEADS + h + 1]
        ig_r = gr[h:h + 1, :]
        b_r = brow[M_HEADS + h:M_HEADS + h + 1, :]
        if nseq == 1:
            m0c = mst_ref[0][:, h:h + 1]
            n0rows = nst_ref[0, h:h + 1, :]
        else:
            m0c = jnp.zeros((R, 1), F32)
            n0rows = jnp.zeros((R, M_HEAD_DIM), F32)
            for s in range(nseq):
                m0c = jnp.where(rseq == s, mst_ref[s][:, h:h + 1], m0c)
                n0rows = jnp.where(rseq == s, nst_ref[s, h:h + 1, :], n0rows)
        dm = jnp.where(maskb, b_c - b_r + ig_r, NEG)
        a_c = b_c + m0c
        m_c = jnp.maximum(a_c, jnp.max(dm, axis=-1, keepdims=True))
        w = jnp.exp(dm - m_c)
        sc = jnp.exp(a_c - m_c)
        wqk = w * _dot_nt(qb, kb)
        if nseq == 1:
            inter = _dot_nt(qb, cst_ref[0, h].astype(BF16))
        else:
            inter = jnp.zeros((R, M_HEAD_DIM), F32)
            for s in range(nseq):
                qs = jnp.where(rseq == s, qh, 0.0).astype(BF16)
                inter = inter + _dot_nt(qs, cst_ref[s, h].astype(BF16))
        num = _dot(wqk.astype(BF16), vb) + sc * inter
        den = jnp.sum(wqk, axis=-1, keepdims=True) + sc * jnp.sum(qh * n0rows, axis=-1, keepdims=True)
        hh = num / jnp.maximum(jnp.abs(den), jnp.exp(-m_c))

        for s in range(nseq):
            e = s * L + L - 1
            m_end = m_c[e:e + 1, :]
            wend = jnp.exp(b_c[e:e + 1, :] - b_c + ig_c - m_end)
            if nseq > 1:
                wend = jnp.where(rseq == s, wend, 0.0)
            sce = jnp.exp(a_c[e:e + 1, :] - m_end)
            c_new = sce * cst_ref[s, h] + _dot_tn((vh * wend).astype(BF16), kb)
            n_new = sce * nst_ref[s, h:h + 1, :] + jnp.sum(wend * kh, axis=0, keepdims=True)
            cst_ref[s, h] = c_new
            nst_ref[s, h:h + 1, :] = n_new
            m_new[s] = jnp.where(lane == h, m_end, m_new[s])

        hn = (hh * lax.rsqrt(jnp.mean(hh * hh, axis=-1, keepdims=True) + EPS)) * mng_ref[:, sl]
        out_ref[:, sl] = jax.nn.sigmoid(o_ref[:, sl]) * hn
    for s in range(nseq):
        mst_ref[s] = m_new[s]


def _mlstm(qkm, vm, om, gcol, gt, conv0, c0, n0, m0, cw, cb, gbrow, gbcol, mng, row0, nseq, L, ngroups, nchunks):
    R = nseq * L
    off = row0 // R
    per_tile = TOK_TILE // R
    tok = lambda w: pl.BlockSpec((R, w), lambda g, c: (off + g * nchunks + c, 0))
    gt_spec = pl.BlockSpec((1, SUBLANES, R),
                           lambda g, c: ((off + g * nchunks + c) // per_tile, 0, (off + g * nchunks + c) % per_tile))
    full = lambda a: pl.BlockSpec(a.shape, lambda g, c: (0,) * a.ndim)
    st4 = pl.BlockSpec((nseq, M_HEADS, M_HEAD_DIM, M_HEAD_DIM), lambda g, c: (g, 0, 0, 0))
    st3 = pl.BlockSpec((nseq, M_HEADS, M_HEAD_DIM), lambda g, c: (g, 0, 0))
    stm = pl.BlockSpec((nseq, 1, LANES), lambda g, c: (g, 0, 0))
    conv_spec = pl.BlockSpec((nseq, SUBLANES, 2 * M_WIDTH), lambda g, c: (g, 0, 0))
    r = jnp.arange(R)
    mask = ((r[:, None] // L == r[None, :] // L) & (r[None, :] <= r[:, None])).astype(BF16)
    nstate = ngroups * nseq
    return pl.pallas_call(
        functools.partial(_mlstm_kernel, nseq, L),
        grid=(ngroups, nchunks),
        in_specs=[tok(2 * M_WIDTH), tok(M_WIDTH), tok(M_WIDTH), tok(LANES), gt_spec, conv_spec, st4, st3, stm,
                  full(cw), full(cb), full(gbrow), full(gbcol), full(mng), full(mask)],
        out_specs=(pl.BlockSpec((R, M_WIDTH), lambda g, c: (g * nchunks + c, 0)), st4, st3, stm),
        out_shape=(jax.ShapeDtypeStruct((ngroups * nchunks * R, M_WIDTH), F32),
                   jax.ShapeDtypeStruct((nstate, M_HEADS, M_HEAD_DIM, M_HEAD_DIM), F32),
                   jax.ShapeDtypeStruct((nstate, M_HEADS, M_HEAD_DIM), F32),
                   jax.ShapeDtypeStruct((nstate, 1, LANES), F32)),
        scratch_shapes=[pltpu.VMEM((R, 2 * M_WIDTH), F32)],
        compiler_params=pltpu.CompilerParams(dimension_semantics=("parallel", "arbitrary"),
                                             vmem_limit_bytes=VMEM_LIMIT),
        name="mlstm_n%d" % nseq,
    )(qkm, vm, om, gcol, gt, conv0, c0, n0, m0, cw, cb, gbrow, gbcol, mng, mask)


def _outproj_kernel(npt, ap_ref, as_ref, mp_ref, ms_ref, xp_ref, xs_ref, wo_ref, g_ref, wr_ref, br_ref,
                    h_ref, hn_ref, route_ref, cnt_ref):
    a = _pick(npt, ap_ref, as_ref).astype(BF16)
    m = _pick(npt, mp_ref, ms_ref).astype(BF16)
    h = _pick(npt, xp_ref, xs_ref) + _dot(a, wo_ref[0:A_WIDTH, :]) + _dot(m, wo_ref[A_WIDTH:A_WIDTH + M_WIDTH, :])
    h_ref[...] = h
    hn = (h * lax.rsqrt(jnp.mean(h * h, axis=-1, keepdims=True) + EPS)) * g_ref[...]
    _store_row_tiles(hn_ref, TOK_TILE, hn)
    logits = _dot(hn.astype(BF16), wr_ref[...]) + br_ref[...]
    lane = lax.broadcasted_iota(jnp.int32, logits.shape, 1)
    route = jnp.zeros(logits.shape, F32)
    picked = jnp.zeros(logits.shape, F32)
    top0 = None
    den = None
    es = []
    for k in range(TOP_K):
        mx = jnp.max(logits, axis=-1, keepdims=True)
        idx = jnp.min(jnp.where(logits == mx, lane, LANES), axis=-1, keepdims=True)
        if k == 0:
            top0 = mx
        e = jnp.exp(mx - top0)
        den = e if den is None else den + e
        es.append(e)
        route = jnp.where(lane == TOP_K + k, idx.astype(F32), route)
        picked = jnp.where(lane == idx, 1.0, picked)
        logits = jnp.where(lane == idx, -jnp.inf, logits)
    for k in range(TOP_K):
        route = jnp.where(lane == k, es[k] / den, route)
    route_ref[...] = route

    @pl.when(pl.program_id(0) == 0)
    def _():
        cnt_ref[...] = jnp.zeros_like(cnt_ref)

    cnt_ref[0:1, :] += jnp.sum(picked, axis=0, keepdims=True)


def _outproj(a_p, a_s, m_p, m_s, xp, xs, w_out, g_ffn, wr_pad, br_pad):
    T = xp.shape[0] + xs.shape[0]
    npt = xp.shape[0] // TOK_TILE
    row = lambda w: pl.BlockSpec((TOK_TILE, w), lambda i: (i, 0))
    full = lambda a: pl.BlockSpec(a.shape, lambda i: (0,) * a.ndim)
    return pl.pallas_call(
        functools.partial(_outproj_kernel, npt),
        grid=(T // TOK_TILE,),
        in_specs=[*_split_specs(npt, TOK_TILE, A_WIDTH), *_split_specs(npt, TOK_TILE, M_WIDTH),
                  *_split_specs(npt, TOK_TILE, D_MODEL), full(w_out), full(g_ffn), full(wr_pad), full(br_pad)],
        out_specs=(row(D_MODEL), pl.BlockSpec((TOK_TILE * SUBLANES, LANES), lambda i: (i, 0)), row(LANES),
                   pl.BlockSpec((SUBLANES, LANES), lambda i: (0, 0))),
        out_shape=(jax.ShapeDtypeStruct((T, D_MODEL), F32), jax.ShapeDtypeStruct((T * SUBLANES, LANES), F32),
                   jax.ShapeDtypeStruct((T, LANES), F32), jax.ShapeDtypeStruct((SUBLANES, LANES), F32)),
        compiler_params=pltpu.CompilerParams(dimension_semantics=("arbitrary",), vmem_limit_bytes=VMEM_LIMIT),
        name="outproj_router",
    )(a_p, a_s, m_p, m_s, xp, xs, w_out, g_ffn, wr_pad, br_pad)


def _route_kernel(nblk, route_ref, cnt_ref, lstrict_ref, ustrict_ref, dest_ref, blk_ref, info_ref, carry_ref):
    i = pl.program_id(0)
    lane = lax.broadcasted_iota(jnp.int32, (1, LANES), 1)
    cnt = cnt_ref[0:1, :]
    nb_e = jnp.floor((cnt + (MOE_BLOCK - 1.0)) * (1.0 / MOE_BLOCK))
    u = ustrict_ref[...]
    bstart = sum(_dot(p, u) for p in _split3(jnp.broadcast_to(nb_e, (SUBLANES, LANES))))[0:1, :]
    bend = bstart + nb_e
    row_start = bstart * float(MOE_BLOCK)

    @pl.when(i == 0)
    def _():
        carry_ref[...] = jnp.zeros_like(carry_ref)
        bi = lax.broadcasted_iota(jnp.int32, (nblk, LANES), 0).astype(F32)
        done = jnp.logical_and(bend <= bi, lane < N_EXPERTS)
        be = jnp.minimum(jnp.sum(jnp.where(done, 1.0, 0.0), axis=-1, keepdims=True), N_EXPERTS - 1.0)
        blk_ref[...] = jnp.broadcast_to(be, (nblk, LANES)).astype(jnp.int32)
        info = jnp.zeros((SUBLANES, LANES), F32)
        srow = lax.broadcasted_iota(jnp.int32, (SUBLANES, LANES), 0)
        info = jnp.where(srow == 0, row_start + cnt, info)
        info = jnp.where(srow == 1, nb_e * float(MOE_BLOCK) - cnt, info)
        info = jnp.where(srow == 2, bend, info)
        info_ref[...] = info.astype(jnp.int32)

    r = route_ref[...]
    lane_f = lane.astype(F32)
    sel = [lane_f == r[:, TOP_K + k:TOP_K + k + 1] for k in range(TOP_K)]
    oh = jnp.zeros(r.shape, F32)
    for k in range(TOP_K):
        oh = jnp.where(sel[k], 1.0, oh)
    before = _dot(lstrict_ref[...], oh.astype(BF16)) + carry_ref[...] + row_start
    dest = jnp.zeros(r.shape, jnp.int32)
    for k in range(TOP_K):
        d = jnp.sum(jnp.where(sel[k], before, 0.0), axis=-1, keepdims=True)
        dest = jnp.where(lane == k, d.astype(jnp.int32), dest)
    dest_ref[...] = dest
    carry_ref[...] += jnp.sum(oh, axis=0, keepdims=True)


def _route_tables(route, cnt, nblk):
    T = route.shape[0]
    a = jnp.arange(TOK_TILE)
    lstrict = (a[:, None] > a[None, :]).astype(BF16)
    b = jnp.arange(LANES)
    ustrict = (b[:, None] < b[None, :]).astype(BF16)
    full = lambda x: pl.BlockSpec(x.shape, lambda i: (0,) * x.ndim)
    return pl.pallas_call(
        functools.partial(_route_kernel, nblk),
        grid=(T // TOK_TILE,),
        in_specs=[pl.BlockSpec((TOK_TILE, LANES), lambda i: (i, 0)), full(cnt), full(lstrict), full(ustrict)],
        out_specs=(pl.BlockSpec((TOK_TILE, LANES), lambda i: (i, 0)),
                   pl.BlockSpec((nblk, LANES), lambda i: (0, 0)),
                   pl.BlockSpec((SUBLANES, LANES), lambda i: (0, 0))),
        out_shape=(jax.ShapeDtypeStruct((T, LANES), jnp.int32), jax.ShapeDtypeStruct((nblk, LANES), jnp.int32),
                   jax.ShapeDtypeStruct((SUBLANES, LANES), jnp.int32)),
        scratch_shapes=[pltpu.VMEM((1, LANES), F32)],
        compiler_params=pltpu.CompilerParams(dimension_semantics=("arbitrary",), vmem_limit_bytes=VMEM_LIMIT),
        name="route_tables",
    )(route, cnt, lstrict, ustrict)


def _dispatch_kernel(padrow_ref, npad_ref, nu_ref, dest_ref, hn_ref, xs_hbm, zbuf, sem, zsem):
    i = pl.program_id(0)

    @pl.when(i == 0)
    def _():
        zbuf[...] = jnp.zeros_like(zbuf)
        nblk = xs_hbm.shape[0] // (MOE_BLOCK * SUBLANES)

        def tail_start(b, c):
            pltpu.make_async_copy(zbuf, _row_tiles(xs_hbm, b * MOE_BLOCK, MOE_BLOCK), zsem).start()
            return c

        def tail_wait(b, c):
            pltpu.make_async_copy(zbuf, _row_tiles(xs_hbm, b * MOE_BLOCK, MOE_BLOCK), zsem).wait()
            return c

        lax.fori_loop(nu_ref[0], nblk, tail_start, 0)
        lax.fori_loop(nu_ref[0], nblk, tail_wait, 0)

        def per_expert(e, carry):
            base = padrow_ref[e]
            n = npad_ref[e]

            def start(r, c):
                pltpu.make_async_copy(_row_tile(zbuf, 0), _row_tile(xs_hbm, base + r), zsem).start()
                return c

            def wait(r, c):
                pltpu.make_async_copy(_row_tile(zbuf, 0), _row_tile(xs_hbm, base + r), zsem).wait()
                return c

            lax.fori_loop(0, n, start, 0)
            lax.fori_loop(0, n, wait, 0)
            return carry

        lax.fori_loop(0, N_EXPERTS, per_expert, 0)

    def body(j, carry):
        for u in range(SUBLANES):
            t = j * SUBLANES + u
            for k in range(TOP_K):
                d = dest_ref[0, 0, t * TOP_K + k]
                pltpu.make_async_copy(_row_tile(hn_ref, t), _row_tile(xs_hbm, d), sem).start(
                    priority=(u * TOP_K + k) % 2)
        return carry

    lax.fori_loop(0, TOK_TILE // SUBLANES, body, 0)
    for _ in range(TOP_K):
        pltpu.make_async_copy(hn_ref, _row_tiles(xs_hbm, 0, TOK_TILE), sem).wait()


def _dispatch(padrow, npad, nused, dest_tiles, hn, n_rows):
    T = hn.shape[0] // SUBLANES
    grid_spec = pltpu.PrefetchScalarGridSpec(
        num_scalar_prefetch=3,
        grid=(T // TOK_TILE,),
        in_specs=[pl.BlockSpec((1, 1, TOK_TILE * TOP_K), lambda i, *_: (i, 0, 0), memory_space=pltpu.SMEM),
                  pl.BlockSpec((TOK_TILE * SUBLANES, LANES), lambda i, *_: (i, 0))],
        out_specs=pl.BlockSpec(memory_space=pl.ANY),
        scratch_shapes=[pltpu.VMEM((MOE_BLOCK * SUBLANES, LANES), F32), pltpu.SemaphoreType.DMA(()),
                        pltpu.SemaphoreType.DMA(())],
    )
    return pl.pallas_call(
        _dispatch_kernel,
        grid_spec=grid_spec,
        out_shape=jax.ShapeDtypeStruct((n_rows * SUBLANES, LANES), F32),
        compiler_params=pltpu.CompilerParams(dimension_semantics=("arbitrary",), vmem_limit_bytes=VMEM_LIMIT),
        name="moe_dispatch",
    )(padrow, npad, nused, dest_tiles, hn)


def _moe_kernel(be_ref, nu_ref, x_ref, wgu_ref, bgu_ref, wd_ref, bd_ref, y_ref):
    del be_ref
    i = pl.program_id(0)

    @pl.when(i < nu_ref[0])
    def _():
        x = _load_row_tiles(x_ref, MOE_BLOCK).astype(BF16)
        hb = _dot(x, wgu_ref[0]) + bgu_ref[0]
        glu = jnp.minimum(hb[:, :D_FF], SWIGLU_LIMIT)
        lin = jnp.clip(hb[:, D_FF:], -SWIGLU_LIMIT, SWIGLU_LIMIT)
        act = glu * jax.nn.sigmoid(SWIGLU_ALPHA * glu) * (lin + 1.0)
        _store_row_tiles(y_ref, MOE_BLOCK, _dot(act.astype(BF16), wd_ref[0]) + bd_ref[0])

    @pl.when(i >= nu_ref[0])
    def _():
        y_ref[...] = jnp.zeros_like(y_ref)


def _moe_blocks(block_e, nused, xs, wgu, bgu, wd, bd):
    nblk = block_e.shape[0]
    grid_spec = pltpu.PrefetchScalarGridSpec(
        num_scalar_prefetch=2,
        grid=(nblk,),
        in_specs=[
            pl.BlockSpec((MOE_BLOCK * SUBLANES, LANES), lambda i, be, nu: (i, 0)),
            pl.BlockSpec((1, D_MODEL, 2 * D_FF), lambda i, be, nu: (be[i], 0, 0)),
            pl.BlockSpec((1, 1, 2 * D_FF), lambda i, be, nu: (be[i], 0, 0)),
            pl.BlockSpec((1, D_FF, D_MODEL), lambda i, be, nu: (be[i], 0, 0)),
            pl.BlockSpec((1, 1, D_MODEL), lambda i, be, nu: (be[i], 0, 0)),
        ],
        out_specs=pl.BlockSpec((MOE_BLOCK * SUBLANES, LANES), lambda i, be, nu: (i, 0)),
    )
    return pl.pallas_call(
        _moe_kernel,
        grid_spec=grid_spec,
        out_shape=jax.ShapeDtypeStruct(xs.shape, F32),
        compiler_params=pltpu.CompilerParams(dimension_semantics=("arbitrary",), vmem_limit_bytes=VMEM_LIMIT),
        name="moe_blocks",
    )(block_e, nused, xs, wgu, bgu, wd, bd)


def _combine_kernel(npt, destc_ref, destn_ref, h_ref, route_ref, ys_hbm, yp_ref, ysm_ref, gbuf, gsem):
    i = pl.program_id(0)
    nt = pl.num_programs(0)
    slot = i % 2

    def issue(dest_ref, s):
        def body(j, carry):
            for u in range(SUBLANES):
                t = j * SUBLANES + u
                for k in range(TOP_K):
                    d = dest_ref[0, 0, t * TOP_K + k]
                    pltpu.make_async_copy(_row_tile(ys_hbm, d), _row_tile(gbuf.at[s, k], t), gsem.at[s]).start(
                        priority=(u * TOP_K + k) % 2)
            return carry
        lax.fori_loop(0, CMB_TILE // SUBLANES, body, 0)

    @pl.when(i == 0)
    def _():
        issue(destc_ref, 0)

    @pl.when(i + 1 < nt)
    def _():
        issue(destn_ref, 1 - slot)

    for k in range(TOP_K):
        pltpu.make_async_copy(_row_tiles(ys_hbm, 0, CMB_TILE), gbuf.at[slot, k], gsem.at[slot]).wait()
    r = route_ref[...]
    moe = _load_row_tiles(gbuf.at[slot, 0], CMB_TILE) * r[:, 0:1]
    for k in range(1, TOP_K):
        moe = moe + _load_row_tiles(gbuf.at[slot, k], CMB_TILE) * r[:, k:k + 1]
    y = h_ref[...] + moe

    @pl.when(i < npt)
    def _():
        yp_ref[...] = y

    @pl.when(i >= npt)
    def _():
        ysm_ref[...] = y


def _combine(dest_tiles, h, route, ys, n_prompt_rows):
    T = h.shape[0]
    nt = T // CMB_TILE
    npt = n_prompt_rows // CMB_TILE
    smem_blk = lambda imap: pl.BlockSpec((1, 1, CMB_TILE * TOP_K), imap, memory_space=pltpu.SMEM)
    return pl.pallas_call(
        functools.partial(_combine_kernel, npt),
        grid=(nt,),
        in_specs=[smem_blk(lambda i: (i, 0, 0)), smem_blk(lambda i: (jnp.minimum(i + 1, nt - 1), 0, 0)),
                  pl.BlockSpec((CMB_TILE, D_MODEL), lambda i: (i, 0)),
                  pl.BlockSpec((CMB_TILE, LANES), lambda i: (i, 0)),
                  pl.BlockSpec(memory_space=pl.ANY)],
        out_specs=(pl.BlockSpec((CMB_TILE, D_MODEL), lambda i: (jnp.minimum(i, npt - 1), 0)),
                   pl.BlockSpec((CMB_TILE, D_MODEL), lambda i: (jnp.maximum(i - npt, 0), 0))),
        out_shape=(jax.ShapeDtypeStruct((n_prompt_rows, D_MODEL), F32),
                   jax.ShapeDtypeStruct((T - n_prompt_rows, D_MODEL), F32)),
        scratch_shapes=[pltpu.VMEM((2, TOP_K, CMB_TILE * SUBLANES, LANES), F32), pltpu.SemaphoreType.DMA((2,))],
        compiler_params=pltpu.CompilerParams(dimension_semantics=("arbitrary",), vmem_limit_bytes=VMEM_LIMIT),
        name="moe_combine",
    )(dest_tiles, dest_tiles, h, route, ys)


def kernel(x_prompt, x_sample, cache_k_win, cache_v_win, state_conv, state_C, state_n, state_m, g_attn, w_in, b_i,
           b_f, q_norm_g, k_norm_g, sinks, conv_w, conv_b, m_norm_g, w_out, g_ffn, w_router, b_router, w_gate_up,
           b_gate_up, w_down, b_down):
    depth = g_attn.shape[0]
    assert depth == 1
    B, S, _ = x_prompt.shape
    DB, DS, _ = x_sample.shape
    TP = B * S
    TS = DB * DS
    T = TP + TS
    assert T % TOK_TILE == 0 and TP % TOK_TILE == 0 and S % ATT_QB == 0 and S % PROMPT_CHUNK == 0
    assert DS == SUBLANES and DB % SAMPLE_NB == 0 and (SAMPLE_NB * DS) == LANES
    l = 0

    xp = x_prompt.reshape(TP, D_MODEL)
    xs = x_sample.reshape(TS, D_MODEL)

    w_pad = jnp.pad(w_in[l], ((0, 0), (0, IN_PAD - w_in.shape[2]))).astype(BF16)
    wgt = jnp.transpose(w_in[l][:, GATE_COL:GATE_COL + 2 * M_HEADS]).astype(BF16)
    gi = jnp.arange(A_WIDTH) // HEAD_DIM
    gmat = (gi[:, None] == gi[None, :]).astype(BF16)
    qg = jnp.tile(q_norm_g[l], A_HEADS).reshape(1, A_WIDTH)
    kg = jnp.tile(k_norm_g[l], A_KV_HEADS).reshape(1, KV_WIDTH)
    gbias = jnp.concatenate([b_i[l], b_f[l]])
    gbrow = jnp.pad(gbias, (0, LANES - 2 * M_HEADS)).reshape(1, LANES)
    gbcol = gbias.reshape(2 * M_HEADS, 1)
    mng = m_norm_g[l].reshape(1, M_WIDTH)
    cw = conv_w[l]
    cb = conv_b[l].reshape(1, 2 * M_WIDTH)
    wr_pad = jnp.pad(w_router[l], ((0, 0), (0, LANES - N_EXPERTS))).astype(BF16)
    br_pad = jnp.concatenate([b_router[l], jnp.full((LANES - N_EXPERTS,), NEG, F32)]).reshape(1, LANES)

    qn, kn, va, qkm, vm, om, gcol, gt = _inproj(xp, xs, g_attn[l].reshape(1, D_MODEL), w_pad, wgt, gmat, qg, kg)

    a_p = _attn_prompt(sinks[l], qn, kn, va, B, S)
    ck = cache_k_win[l].reshape(DB, WINDOW, KV_WIDTH)
    cv = cache_v_win[l].reshape(DB, WINDOW, KV_WIDTH)
    a_s, kwin_s, vwin_s = _attn_sample(sinks[l], qn, kn, va, ck, cv, TP, DB, DS)

    zc = jnp.zeros((B, SUBLANES, 2 * M_WIDTH), F32)
    m_p, C_p, n_p, mm_p = _mlstm(
        qkm, vm, om, gcol, gt, zc,
        jnp.zeros((B, M_HEADS, M_HEAD_DIM, M_HEAD_DIM), F32), jnp.zeros((B, M_HEADS, M_HEAD_DIM), F32),
        jnp.full((B, 1, LANES), NEG, F32), cw, cb, gbrow, gbcol, mng,
        row0=0, nseq=1, L=PROMPT_CHUNK, ngroups=B, nchunks=S // PROMPT_CHUNK)
    conv_s0 = jnp.pad(state_conv[l], ((0, 0), (SUBLANES - (CONV_W - 1), 0), (0, 0)))
    m0_s = jnp.pad(state_m[l], ((0, 0), (0, LANES - M_HEADS))).reshape(DB, 1, LANES)
    m_s, C_s, n_s, mm_s = _mlstm(
        qkm, vm, om, gcol, gt, conv_s0, state_C[l], state_n[l], m0_s, cw, cb, gbrow, gbcol, mng,
        row0=TP, nseq=SAMPLE_NB, L=DS, ngroups=DB // SAMPLE_NB, nchunks=1)

    h, hn, route, cnt = _outproj(a_p, a_s, m_p, m_s, xp, xs, w_out[l].astype(BF16), g_ffn[l].reshape(1, D_MODEL),
                                 wr_pad, br_pad)

    nblk = T * TOP_K // MOE_BLOCK + N_EXPERTS
    dest, blk, info = _route_tables(route, cnt, nblk)
    block_e = blk[:, 0]
    padrow = info[0, :N_EXPERTS]
    npad = info[1, :N_EXPERTS]
    nused = info[2, N_EXPERTS - 1:N_EXPERTS]
    dest4 = dest[:, :TOP_K]
    xrows = _dispatch(padrow, npad, nused, dest4.reshape(T // TOK_TILE, 1, TOK_TILE * TOP_K), hn, nblk * MOE_BLOCK)
    yrows = _moe_blocks(block_e, nused, xrows,
                        w_gate_up[l].astype(BF16), b_gate_up[l].reshape(N_EXPERTS, 1, 2 * D_FF),
                        w_down[l].astype(BF16), b_down[l].reshape(N_EXPERTS, 1, D_MODEL))
    y_p, y_s = _combine(dest4.reshape(T // CMB_TILE, 1, CMB_TILE * TOP_K), h, route, yrows, TP)

    y_p = y_p.reshape(B, S, D_MODEL)
    y_s = y_s.reshape(DB, DS, D_MODEL)
    def seq_tail(rows, n):
        return jnp.stack([rows[(b + 1) * S - n:(b + 1) * S] for b in range(B)])

    kwin_p = seq_tail(kn, WINDOW).reshape(B, WINDOW, A_KV_HEADS, HEAD_DIM)
    vwin_p = seq_tail(va, WINDOW).reshape(B, WINDOW, A_KV_HEADS, HEAD_DIM)
    qkm_s = qkm[TP:].reshape(DB, DS, 2 * M_WIDTH)
    return (y_p, y_s,
            kwin_p[None], vwin_p[None], seq_tail(qkm, CONV_W - 1)[None],
            C_p[None], n_p[None], mm_p[:, 0, :M_HEADS][None],
            kwin_s.reshape(DB, WINDOW, A_KV_HEADS, HEAD_DIM)[None],
            vwin_s.reshape(DB, WINDOW, A_KV_HEADS, HEAD_DIM)[None],
            qkm_s[:, -(CONV_W - 1):][None],
            C_s[None], n_s[None], mm_s[:, 0, :M_HEADS][None])
```

```python
import functools

import jax
import jax.numpy as jnp
from jax import lax
from jax.experimental import pallas as pl
from jax.experimental.pallas import tpu as pltpu

F32 = jnp.float32
BF16 = jnp.bfloat16

D_MODEL = 1024
HEAD_DIM = 64
A_HEADS = 8
A_KV_HEADS = 2
A_GROUP = A_HEADS // A_KV_HEADS
A_WIDTH = A_HEADS * HEAD_DIM
KV_WIDTH = A_KV_HEADS * HEAD_DIM
WINDOW = 128
M_HEADS = 4
M_HEAD_DIM = 128
M_WIDTH = M_HEADS * M_HEAD_DIM
CONV_W = 4
N_EXPERTS = 32
TOP_K = 4
D_FF = D_MODEL
SWIGLU_LIMIT = 7.0
SWIGLU_ALPHA = 1.702
MOE_BLOCK = 512
EPS = 1e-6
NEG = -1e30

LANES = 128
SUBLANES = 8
GATE_COL = A_WIDTH + 2 * KV_WIDTH + 4 * M_WIDTH
IN_PAD = GATE_COL + LANES
TOK_TILE = 512
ATT_QB = 512
ATT_SB = 128
SAMPLE_NB = 16
PROMPT_CHUNK = 256
CMB_TILE = 256
CAST_ROWS = 128
VMEM_LIMIT = 48 * 1024 * 1024
MOE_VMEM_LIMIT = 58 * 1024 * 1024
assert D_FF == D_MODEL


def _dot(a, b):
    return jnp.dot(a, b, preferred_element_type=F32)


def _dot_nt(a, b):
    return lax.dot_general(a, b, (((1,), (1,)), ((), ())), preferred_element_type=F32)


def _dot_tn(a, b):
    return lax.dot_general(a, b, (((0,), (0,)), ((), ())), preferred_element_type=F32)


def _split3(x):
    hi = x.astype(BF16)
    r1 = x - hi.astype(F32)
    mid = r1.astype(BF16)
    lo = (r1 - mid.astype(F32)).astype(BF16)
    return hi, mid, lo


def _log_sigmoid(x):
    return jnp.minimum(x, 0.0) - jnp.log1p(jnp.exp(-jnp.abs(x)))


def _load_row_tiles(ref2, rows):
    return jnp.concatenate([ref2[pl.ds(s, rows, stride=SUBLANES), :] for s in range(SUBLANES)], axis=1)


def _store_row_tiles(ref2, rows, val):
    for s in range(SUBLANES):
        ref2[pl.ds(s, rows, stride=SUBLANES), :] = val[:, s * LANES:(s + 1) * LANES]


def _row_tile(ref2, idx):
    return ref2.at[pl.ds(pl.multiple_of(idx * SUBLANES, SUBLANES), SUBLANES), :]


def _row_tiles(ref2, first, n):
    return ref2.at[pl.ds(pl.multiple_of(first * SUBLANES, SUBLANES), n * SUBLANES), :]


def _pick(n_prompt_tiles, p_ref, s_ref):
    return jnp.where(pl.program_id(0) < n_prompt_tiles, p_ref[...], s_ref[...])


def _split_specs(n_prompt_tiles, rows, width):
    return (pl.BlockSpec((rows, width), lambda i, *_: (jnp.minimum(i, n_prompt_tiles - 1), 0)),
            pl.BlockSpec((rows, width), lambda i, *_: (jnp.maximum(i - n_prompt_tiles, 0), 0)))


def _inproj_kernel(npt, xp_ref, xs_ref, g_ref, w_ref, wgt_ref, gmat_ref, qg_ref, kg_ref,
                   qn_ref, kn_ref, va_ref, qkm_ref, vm_ref, om_ref, gcol_ref, gt_ref):
    x = _pick(npt, xp_ref, xs_ref)
    ms = jnp.mean(x * x, axis=-1, keepdims=True)
    xn = ((x * lax.rsqrt(ms + EPS)) * g_ref[...]).astype(BF16)

    def seg(lo, hi):
        return _dot(xn, w_ref[:, lo:hi])

    def head_norm(z, gmat, g):
        hi, mid, lo = _split3(z * z)
        ss = _dot(hi, gmat) + _dot(mid, gmat) + _dot(lo, gmat)
        return (z * lax.rsqrt(ss * (1.0 / HEAD_DIM) + EPS)) * g

    o0 = A_WIDTH
    o1 = o0 + KV_WIDTH
    o2 = o1 + KV_WIDTH
    o3 = o2 + 2 * M_WIDTH
    o4 = o3 + M_WIDTH
    o5 = o4 + M_WIDTH
    qn_ref[...] = head_norm(seg(0, o0), gmat_ref[...], qg_ref[...])
    kn_ref[...] = head_norm(seg(o0, o1), gmat_ref[:KV_WIDTH, :KV_WIDTH], kg_ref[...])
    va_ref[...] = seg(o1, o2)
    qkm_ref[...] = seg(o2, o3)
    vm_ref[...] = seg(o3, o4)
    om_ref[...] = seg(o4, o5)
    gcol_ref[...] = seg(o5, o5 + LANES)
    gt_ref[0] = _dot_nt(wgt_ref[...], xn)


def _inproj(xp, xs, g_attn, w_pad, wgt, gmat, qg, kg):
    T = xp.shape[0] + xs.shape[0]
    nt = T // TOK_TILE
    npt = xp.shape[0] // TOK_TILE
    row = lambda w: pl.BlockSpec((TOK_TILE, w), lambda i: (i, 0))
    full = lambda a: pl.BlockSpec(a.shape, lambda i: (0,) * a.ndim)
    out_shape = (
        jax.ShapeDtypeStruct((T, A_WIDTH), F32),
        jax.ShapeDtypeStruct((T, KV_WIDTH), F32),
        jax.ShapeDtypeStruct((T, KV_WIDTH), F32),
        jax.ShapeDtypeStruct((T, 2 * M_WIDTH), F32),
        jax.ShapeDtypeStruct((T, M_WIDTH), F32),
        jax.ShapeDtypeStruct((T, M_WIDTH), F32),
        jax.ShapeDtypeStruct((T, LANES), F32),
        jax.ShapeDtypeStruct((nt, SUBLANES, TOK_TILE), F32),
    )
    out_specs = (row(A_WIDTH), row(KV_WIDTH), row(KV_WIDTH), row(2 * M_WIDTH), row(M_WIDTH), row(M_WIDTH),
                 row(LANES), pl.BlockSpec((1, SUBLANES, TOK_TILE), lambda i: (i, 0, 0)))
    return pl.pallas_call(
        functools.partial(_inproj_kernel, npt),
        grid=(nt,),
        in_specs=[*_split_specs(npt, TOK_TILE, D_MODEL), full(g_attn), full(w_pad), full(wgt), full(gmat),
                  full(qg), full(kg)],
        out_specs=out_specs,
        out_shape=out_shape,
        compiler_params=pltpu.CompilerParams(dimension_semantics=("parallel",), vmem_limit_bytes=VMEM_LIMIT),
        name="inproj",
    )(xp, xs, g_attn, w_pad, wgt, gmat, qg, kg)


def _softmax_sink(pieces, masks, sink_col):
    masked = [jnp.where(mk, s, NEG) for s, mk in zip(pieces, masks)]
    m = sink_col
    for s in masked:
        m = jnp.maximum(m, jnp.max(s, axis=-1, keepdims=True))
    ps = [jnp.exp(s - m) for s in masked]
    den = jnp.exp(sink_col - m)
    for p in ps:
        den = den + jnp.sum(p, axis=-1, keepdims=True)
    return ps, 1.0 / den


def _stack_heads(q, g):
    return jnp.concatenate([q[:, (A_GROUP * g + i) * HEAD_DIM:(A_GROUP * g + i + 1) * HEAD_DIM]
                            for i in range(A_GROUP)], axis=0)


def _sink_col(sink_ref, g, rows_per_head):
    r = lax.broadcasted_iota(jnp.int32, (A_GROUP * rows_per_head, 1), 0)
    col = jnp.zeros((A_GROUP * rows_per_head, 1), F32)
    for i in range(A_GROUP):
        col = jnp.where(r // rows_per_head == i, sink_ref[A_GROUP * g + i], col)
    return col


def _attn_prompt_kernel(sink_ref, q_ref, kp_ref, kc_ref, vp_ref, vc_ref, o_ref):
    j = pl.program_id(1)
    scale = HEAD_DIM ** -0.5
    kall = jnp.concatenate([kp_ref[...], kc_ref[...]], axis=0).astype(BF16)
    vall = jnp.concatenate([vp_ref[...], vc_ref[...]], axis=0).astype(BF16)
    nrow = A_GROUP * ATT_SB
    r = lax.broadcasted_iota(jnp.int32, (nrow, 2 * ATT_SB), 0) % ATT_SB
    c = lax.broadcasted_iota(jnp.int32, (nrow, 2 * ATT_SB), 1)
    band = jnp.logical_and(c >= r, c <= r + WINDOW)
    band0 = jnp.logical_and(band, jnp.logical_or(c >= ATT_SB, j > 0))
    for sb in range(ATT_QB // ATT_SB):
        q = (q_ref[sb * ATT_SB:(sb + 1) * ATT_SB, :] * scale).astype(BF16)
        kwin = kall[sb * ATT_SB:(sb + 2) * ATT_SB]
        vwin = vall[sb * ATT_SB:(sb + 2) * ATT_SB]
        outs = []
        for g in range(A_KV_HEADS):
            lo, hi = g * HEAD_DIM, (g + 1) * HEAD_DIM
            s = _dot_nt(_stack_heads(q, g), kwin[:, lo:hi])
            (p,), inv = _softmax_sink([s], [band0 if sb == 0 else band], _sink_col(sink_ref, g, ATT_SB))
            o = _dot(p.astype(BF16), vwin[:, lo:hi]) * inv
            outs += [o[i * ATT_SB:(i + 1) * ATT_SB] for i in range(A_GROUP)]
        o_ref[sb * ATT_SB:(sb + 1) * ATT_SB, :] = jnp.concatenate(outs, axis=1)


def _attn_prompt(sinks, qn, kn, va, batch, seq):
    nq = seq // ATT_QB
    ratio = ATT_QB // ATT_SB
    cur = lambda w: pl.BlockSpec((ATT_QB, w), lambda b, j: (b * nq + j, 0))
    prev = lambda w: pl.BlockSpec((ATT_SB, w), lambda b, j: (jnp.maximum((b * nq + j) * ratio - 1, 0), 0))
    return pl.pallas_call(
        _attn_prompt_kernel,
        grid=(batch, nq),
        in_specs=[pl.BlockSpec(memory_space=pltpu.SMEM), cur(A_WIDTH), prev(KV_WIDTH), cur(KV_WIDTH),
                  prev(KV_WIDTH), cur(KV_WIDTH)],
        out_specs=cur(A_WIDTH),
        out_shape=jax.ShapeDtypeStruct((batch * seq, A_WIDTH), F32),
        compiler_params=pltpu.CompilerParams(dimension_semantics=("parallel", "parallel"),
                                             vmem_limit_bytes=VMEM_LIMIT),
        name="attn_prompt",
    )(sinks, qn, kn, kn, va, va)


def _attn_sample_kernel(dec, sink_ref, q_ref, kn_ref, vn_ref, ck_ref, cv_ref, o_ref, kw_ref, vw_ref):
    scale = HEAD_DIM ** -0.5
    rows = SAMPLE_NB * dec
    knew = kn_ref[...]
    vnew = vn_ref[...]
    knew_b = knew.astype(BF16)
    vnew_b = vnew.astype(BF16)
    nrow = A_GROUP * dec
    t = lax.broadcasted_iota(jnp.int32, (nrow, WINDOW), 0) % dec
    c = lax.broadcasted_iota(jnp.int32, (nrow, WINDOW), 1)
    m_cache = c >= t
    cn = lax.broadcasted_iota(jnp.int32, (nrow, rows), 1)
    tn = lax.broadcasted_iota(jnp.int32, (nrow, rows), 0) % dec
    for i in range(SAMPLE_NB):
        q = (q_ref[i * dec:(i + 1) * dec, :] * scale).astype(BF16)
        ck = ck_ref[i].astype(BF16)
        cv = cv_ref[i].astype(BF16)
        m_new = jnp.logical_and(cn // dec == i, cn % dec <= tn)
        outs = []
        for g in range(A_KV_HEADS):
            lo, hi = g * HEAD_DIM, (g + 1) * HEAD_DIM
            qs = _stack_heads(q, g)
            s_c = _dot_nt(qs, ck[:, lo:hi])
            s_n = _dot_nt(qs, knew_b[:, lo:hi])
            (p_c, p_n), inv = _softmax_sink([s_c, s_n], [m_cache, m_new], _sink_col(sink_ref, g, dec))
            o = (_dot(p_c.astype(BF16), cv[:, lo:hi]) + _dot(p_n.astype(BF16), vnew_b[:, lo:hi])) * inv
            outs += [o[h * dec:(h + 1) * dec] for h in range(A_GROUP)]
        o_ref[i * dec:(i + 1) * dec, :] = jnp.concatenate(outs, axis=1)
        kw_ref[i, 0:WINDOW - dec, :] = ck_ref[i, dec:WINDOW, :]
        kw_ref[i, WINDOW - dec:WINDOW, :] = knew[i * dec:(i + 1) * dec]
        vw_ref[i, 0:WINDOW - dec, :] = cv_ref[i, dec:WINDOW, :]
        vw_ref[i, WINDOW - dec:WINDOW, :] = vnew[i * dec:(i + 1) * dec]


def _attn_sample(sinks, qn, kn, va, ck, cv, row0, dbatch, dec):
    rows = SAMPLE_NB * dec
    off = row0 // rows
    tokrow = lambda w: pl.BlockSpec((rows, w), lambda i: (off + i, 0))
    cache = pl.BlockSpec((SAMPLE_NB, WINDOW, KV_WIDTH), lambda i: (i, 0, 0))
    return pl.pallas_call(
        functools.partial(_attn_sample_kernel, dec),
        grid=(dbatch // SAMPLE_NB,),
        in_specs=[pl.BlockSpec(memory_space=pltpu.SMEM), tokrow(A_WIDTH), tokrow(KV_WIDTH), tokrow(KV_WIDTH),
                  cache, cache],
        out_specs=(pl.BlockSpec((rows, A_WIDTH), lambda i: (i, 0)), cache, cache),
        out_shape=(jax.ShapeDtypeStruct((dbatch * dec, A_WIDTH), F32),
                   jax.ShapeDtypeStruct((dbatch, WINDOW, KV_WIDTH), F32),
                   jax.ShapeDtypeStruct((dbatch, WINDOW, KV_WIDTH), F32)),
        compiler_params=pltpu.CompilerParams(dimension_semantics=("parallel",), vmem_limit_bytes=VMEM_LIMIT),
        name="attn_sample",
    )(sinks, qn, kn, va, ck, cv)


def _mlstm_kernel(nseq, L, qk_ref, v_ref, o_ref, gcol_ref, gt_ref, conv0_ref, c0_ref, n0_ref, m0_ref,
                  cw_ref, cb_ref, gbrow_ref, gbcol_ref, mng_ref, mask_ref,
                  out_ref, cst_ref, nst_ref, mst_ref, prev_ref):
    R = nseq * L
    ci = pl.program_id(1)

    @pl.when(ci == 0)
    def _():
        cst_ref[...] = c0_ref[...]
        nst_ref[...] = n0_ref[...]
        mst_ref[...] = m0_ref[...]
        if nseq == 1:
            prev_ref[...] = jnp.zeros_like(prev_ref)
            prev_ref[R - SUBLANES:R, :] = conv0_ref[0]

    raw = qk_ref[...]
    row = lax.broadcasted_iota(jnp.int32, (R, 1), 0)
    tpos = row % L
    rseq = row // L
    if nseq == 1:
        prevsrc = prev_ref[...]
    else:
        prevsrc = conv0_ref[...].reshape(R, 2 * M_WIDTH)
    acc = raw * cw_ref[CONV_W - 1:CONV_W, :] + cb_ref[...]
    for k in range(1, CONV_W):
        pshift = k if nseq == 1 else R - SUBLANES + k
        sh = jnp.where(tpos >= k, pltpu.roll(raw, k, 0), pltpu.roll(prevsrc, pshift, 0))
        acc = acc + sh * cw_ref[CONV_W - 1 - k:CONV_W - k, :]
    if nseq == 1:
        prev_ref[...] = raw
    qkc = acc * jax.nn.sigmoid(acc)

    gc = gcol_ref[...] + gbrow_ref[...]
    gr = gt_ref[0] + gbcol_ref[...]
    lsc = _log_sigmoid(gc)
    lsr = _log_sigmoid(gr)
    mb = mask_ref[...]
    maskb = mb > 0
    bcol = sum(_dot(mb, p) for p in _split3(lsc))
    brow = sum(_dot_nt(p, mb) for p in _split3(lsr))

    lane = lax.broadcasted_iota(jnp.int32, (1, LANES), 1)
    m_new = [jnp.zeros((1, LANES), F32) for _ in range(nseq)]
    for h in range(M_HEADS):
        sl = slice(h * M_HEAD_DIM, (h + 1) * M_HEAD_DIM)
        qh = qkc[:, sl]
        kh = qkc[:, M_WIDTH + h * M_HEAD_DIM:M_WIDTH + (h + 1) * M_HEAD_DIM] * (M_HEAD_DIM ** -0.5)
        vh = v_ref[:, sl]
        qb, kb, vb = qh.astype(BF16), kh.astype(BF16), vh.astype(BF16)
        ig_c = gc[:, h:h + 1]
        b_c = bcol[:, M_HEADS + h:M_HEADS + h + 1]
        ig_r = gr[h:h + 1, :]
        b_r = brow[M_HEADS + h:M_HEADS + h + 1, :]
        if nseq == 1:
            m0c = mst_ref[0][:, h:h + 1]
            n0rows = nst_ref[0, h:h + 1, :]
        else:
            m0c = jnp.zeros((R, 1), F32)
            n0rows = jnp.zeros((R, M_HEAD_DIM), F32)
            for s in range(nseq):
                m0c = jnp.where(rseq == s, mst_ref[s][:, h:h + 1], m0c)
                n0rows = jnp.where(rseq == s, nst_ref[s, h:h + 1, :], n0rows)
        dm = jnp.where(maskb, b_c - b_r + ig_r, NEG)
        a_c = b_c + m0c
        m_c = jnp.maximum(a_c, jnp.max(dm, axis=-1, keepdims=True))
        w = jnp.exp(dm - m_c)
        sc = jnp.exp(a_c - m_c)
        wqk = w * _dot_nt(qb, kb)
        if nseq == 1:
            inter = _dot_nt(qb, cst_ref[0, h].astype(BF16))
        else:
            inter = jnp.zeros((R, M_HEAD_DIM), F32)
            for s in range(nseq):
                qs = jnp.where(rseq == s, qh, 0.0).astype(BF16)
                inter = inter + _dot_nt(qs, cst_ref[s, h].astype(BF16))
        num = _dot(wqk.astype(BF16), vb) + sc * inter
        den = jnp.sum(wqk, axis=-1, keepdims=True) + sc * jnp.sum(qh * n0rows, axis=-1, keepdims=True)
        hh = num / jnp.maximum(jnp.abs(den), jnp.exp(-m_c))

        for s in range(nseq):
            e = s * L + L - 1
            m_end = m_c[e:e + 1, :]
            wend = jnp.exp(b_c[e:e + 1, :] - b_c + ig_c - m_end)
            if nseq > 1:
                wend = jnp.where(rseq == s, wend, 0.0)
            sce = jnp.exp(a_c[e:e + 1, :] - m_end)
            c_new = sce * cst_ref[s, h] + _dot_tn((vh * wend).astype(BF16), kb)
            n_new = sce * nst_ref[s, h:h + 1, :] + jnp.sum(wend * kh, axis=0, keepdims=True)
            cst_ref[s, h] = c_new
            nst_ref[s, h:h + 1, :] = n_new
            m_new[s] = jnp.where(lane == h, m_end, m_new[s])

        hn = (hh * lax.rsqrt(jnp.mean(hh * hh, axis=-1, keepdims=True) + EPS)) * mng_ref[:, sl]
        out_ref[:, sl] = jax.nn.sigmoid(o_ref[:, sl]) * hn
    for s in range(nseq):
        mst_ref[s] = m_new[s]


def _mlstm(qkm, vm, om, gcol, gt, conv0, c0, n0, m0, cw, cb, gbrow, gbcol, mng, row0, nseq, L, ngroups, nchunks):
    R = nseq * L
    off = row0 // R
    per_tile = TOK_TILE // R
    tok = lambda w: pl.BlockSpec((R, w), lambda g, c: (off + g * nchunks + c, 0))
    gt_spec = pl.BlockSpec((1, SUBLANES, R),
                           lambda g, c: ((off + g * nchunks + c) // per_tile, 0, (off + g * nchunks + c) % per_tile))
    full = lambda a: pl.BlockSpec(a.shape, lambda g, c: (0,) * a.ndim)
    st4 = pl.BlockSpec((nseq, M_HEADS, M_HEAD_DIM, M_HEAD_DIM), lambda g, c: (g, 0, 0, 0))
    st3 = pl.BlockSpec((nseq, M_HEADS, M_HEAD_DIM), lambda g, c: (g, 0, 0))
    stm = pl.BlockSpec((nseq, 1, LANES), lambda g, c: (g, 0, 0))
    conv_spec = pl.BlockSpec((nseq, SUBLANES, 2 * M_WIDTH), lambda g, c: (g, 0, 0))
    r = jnp.arange(R)
    mask = ((r[:, None] // L == r[None, :] // L) & (r[None, :] <= r[:, None])).astype(BF16)
    nstate = ngroups * nseq
    return pl.pallas_call(
        functools.partial(_mlstm_kernel, nseq, L),
        grid=(ngroups, nchunks),
        in_specs=[tok(2 * M_WIDTH), tok(M_WIDTH), tok(M_WIDTH), tok(LANES), gt_spec, conv_spec, st4, st3, stm,
                  full(cw), full(cb), full(gbrow), full(gbcol), full(mng), full(mask)],
        out_specs=(pl.BlockSpec((R, M_WIDTH), lambda g, c: (g * nchunks + c, 0)), st4, st3, stm),
        out_shape=(jax.ShapeDtypeStruct((ngroups * nchunks * R, M_WIDTH), F32),
                   jax.ShapeDtypeStruct((nstate, M_HEADS, M_HEAD_DIM, M_HEAD_DIM), F32),
                   jax.ShapeDtypeStruct((nstate, M_HEADS, M_HEAD_DIM), F32),
                   jax.ShapeDtypeStruct((nstate, 1, LANES), F32)),
        scratch_shapes=[pltpu.VMEM((R, 2 * M_WIDTH), F32)],
        compiler_params=pltpu.CompilerParams(dimension_semantics=("parallel", "arbitrary"),
                                             vmem_limit_bytes=VMEM_LIMIT),
        name="mlstm_n%d" % nseq,
    )(qkm, vm, om, gcol, gt, conv0, c0, n0, m0, cw, cb, gbrow, gbcol, mng, mask)


def _outproj_kernel(npt, ap_ref, as_ref, mp_ref, ms_ref, xp_ref, xs_ref, wo_ref, g_ref, wr_ref, br_ref,
                    h_ref, hn_ref, route_ref, cnt_ref):
    def project(a_ref, m_ref, x_ref):
        h_ref[...] = (x_ref[...] + _dot(a_ref[...].astype(BF16), wo_ref[0:A_WIDTH, :])
                      + _dot(m_ref[...].astype(BF16), wo_ref[A_WIDTH:A_WIDTH + M_WIDTH, :]))

    @pl.when(pl.program_id(0) < npt)
    def _():
        project(ap_ref, mp_ref, xp_ref)

    @pl.when(pl.program_id(0) >= npt)
    def _():
        project(as_ref, ms_ref, xs_ref)

    h = h_ref[...]
    hn = (h * lax.rsqrt(jnp.mean(h * h, axis=-1, keepdims=True) + EPS)) * g_ref[...]
    _store_row_tiles(hn_ref, TOK_TILE, hn)
    logits = _dot(hn.astype(BF16), wr_ref[...]) + br_ref[...]
    lane = lax.broadcasted_iota(jnp.int32, logits.shape, 1)
    route = jnp.zeros(logits.shape, F32)
    picked = jnp.zeros(logits.shape, F32)
    top0 = None
    den = None
    es = []
    for k in range(TOP_K):
        mx = jnp.max(logits, axis=-1, keepdims=True)
        idx = jnp.min(jnp.where(logits == mx, lane, LANES), axis=-1, keepdims=True)
        if k == 0:
            top0 = mx
        e = jnp.exp(mx - top0)
        den = e if den is None else den + e
        es.append(e)
        route = jnp.where(lane == TOP_K + k, idx.astype(F32), route)
        picked = jnp.where(lane == idx, 1.0, picked)
        logits = jnp.where(lane == idx, -jnp.inf, logits)
    for k in range(TOP_K):
        route = jnp.where(lane == k, es[k] / den, route)
    route_ref[...] = route

    @pl.when(pl.program_id(0) == 0)
    def _():
        cnt_ref[...] = jnp.zeros_like(cnt_ref)

    cnt_ref[0:1, :] += jnp.sum(picked, axis=0, keepdims=True)


def _outproj(a_p, a_s, m_p, m_s, xp, xs, w_out, g_ffn, wr_pad, br_pad):
    T = xp.shape[0] + xs.shape[0]
    npt = xp.shape[0] // TOK_TILE
    row = lambda w: pl.BlockSpec((TOK_TILE, w), lambda i: (i, 0))
    full = lambda a: pl.BlockSpec(a.shape, lambda i: (0,) * a.ndim)
    return pl.pallas_call(
        functools.partial(_outproj_kernel, npt),
        grid=(T // TOK_TILE,),
        in_specs=[*_split_specs(npt, TOK_TILE, A_WIDTH), *_split_specs(npt, TOK_TILE, M_WIDTH),
                  *_split_specs(npt, TOK_TILE, D_MODEL), full(w_out), full(g_ffn), full(wr_pad), full(br_pad)],
        out_specs=(row(D_MODEL), pl.BlockSpec((TOK_TILE * SUBLANES, LANES), lambda i: (i, 0)), row(LANES),
                   pl.BlockSpec((SUBLANES, LANES), lambda i: (0, 0))),
        out_shape=(jax.ShapeDtypeStruct((T, D_MODEL), F32), jax.ShapeDtypeStruct((T * SUBLANES, LANES), F32),
                   jax.ShapeDtypeStruct((T, LANES), F32), jax.ShapeDtypeStruct((SUBLANES, LANES), F32)),
        compiler_params=pltpu.CompilerParams(dimension_semantics=("arbitrary",), vmem_limit_bytes=VMEM_LIMIT),
        name="outproj_router",
    )(a_p, a_s, m_p, m_s, xp, xs, w_out, g_ffn, wr_pad, br_pad)


def _route_kernel(nblk, route_ref, cnt_ref, lstrict_ref, ustrict_ref, dest_ref, blk_ref, info_ref, carry_ref):
    i = pl.program_id(0)
    lane = lax.broadcasted_iota(jnp.int32, (1, LANES), 1)
    cnt = cnt_ref[0:1, :]
    nb_e = jnp.floor((cnt + (MOE_BLOCK - 1.0)) * (1.0 / MOE_BLOCK))
    u = ustrict_ref[...]
    bstart = sum(_dot(p, u) for p in _split3(jnp.broadcast_to(nb_e, (SUBLANES, LANES))))[0:1, :]
    bend = bstart + nb_e
    row_start = bstart * float(MOE_BLOCK)

    @pl.when(i == 0)
    def _():
        carry_ref[...] = jnp.zeros_like(carry_ref)
        bi = lax.broadcasted_iota(jnp.int32, (nblk, LANES), 0).astype(F32)
        done = jnp.logical_and(bend <= bi, lane < N_EXPERTS)
        be = jnp.minimum(jnp.sum(jnp.where(done, 1.0, 0.0), axis=-1, keepdims=True), N_EXPERTS - 1.0)
        blk_ref[...] = jnp.broadcast_to(be, (nblk, LANES)).astype(jnp.int32)
        info = jnp.zeros((SUBLANES, LANES), F32)
        srow = lax.broadcasted_iota(jnp.int32, (SUBLANES, LANES), 0)
        info = jnp.where(srow == 0, row_start + cnt, info)
        info = jnp.where(srow == 1, nb_e * float(MOE_BLOCK) - cnt, info)
        info = jnp.where(srow == 2, bend, info)
        info_ref[...] = info.astype(jnp.int32)

    r = route_ref[...]
    lane_f = lane.astype(F32)
    sel = [lane_f == r[:, TOP_K + k:TOP_K + k + 1] for k in range(TOP_K)]
    oh = jnp.zeros(r.shape, F32)
    for k in range(TOP_K):
        oh = jnp.where(sel[k], 1.0, oh)
    before = _dot(lstrict_ref[...], oh.astype(BF16)) + carry_ref[...] + row_start
    dest = jnp.zeros(r.shape, jnp.int32)
    for k in range(TOP_K):
        d = jnp.sum(jnp.where(sel[k], before, 0.0), axis=-1, keepdims=True)
        dest = jnp.where(lane == k, d.astype(jnp.int32), dest)
    dest_ref[...] = dest
    carry_ref[...] += jnp.sum(oh, axis=0, keepdims=True)


def _route_tables(route, cnt, nblk):
    T = route.shape[0]
    a = jnp.arange(TOK_TILE)
    lstrict = (a[:, None] > a[None, :]).astype(BF16)
    b = jnp.arange(LANES)
    ustrict = (b[:, None] < b[None, :]).astype(BF16)
    full = lambda x: pl.BlockSpec(x.shape, lambda i: (0,) * x.ndim)
    return pl.pallas_call(
        functools.partial(_route_kernel, nblk),
        grid=(T // TOK_TILE,),
        in_specs=[pl.BlockSpec((TOK_TILE, LANES), lambda i: (i, 0)), full(cnt), full(lstrict), full(ustrict)],
        out_specs=(pl.BlockSpec((TOK_TILE, LANES), lambda i: (i, 0)),
                   pl.BlockSpec((nblk, LANES), lambda i: (0, 0)),
                   pl.BlockSpec((SUBLANES, LANES), lambda i: (0, 0))),
        out_shape=(jax.ShapeDtypeStruct((T, LANES), jnp.int32), jax.ShapeDtypeStruct((nblk, LANES), jnp.int32),
                   jax.ShapeDtypeStruct((SUBLANES, LANES), jnp.int32)),
        scratch_shapes=[pltpu.VMEM((1, LANES), F32)],
        compiler_params=pltpu.CompilerParams(dimension_semantics=("arbitrary",), vmem_limit_bytes=VMEM_LIMIT),
        name="route_tables",
    )(route, cnt, lstrict, ustrict)


def _dispatch_kernel(padrow_ref, npad_ref, nu_ref, dest_ref, hn_ref, xs_hbm, zbuf, sem, zsem):
    i = pl.program_id(0)

    @pl.when(i == 0)
    def _():
        zbuf[...] = jnp.zeros_like(zbuf)
        nblk = xs_hbm.shape[0] // (MOE_BLOCK * SUBLANES)

        def tail_start(b, c):
            pltpu.make_async_copy(zbuf, _row_tiles(xs_hbm, b * MOE_BLOCK, MOE_BLOCK), zsem).start()
            return c

        def tail_wait(b, c):
            pltpu.make_async_copy(zbuf, _row_tiles(xs_hbm, b * MOE_BLOCK, MOE_BLOCK), zsem).wait()
            return c

        lax.fori_loop(nu_ref[0], nblk, tail_start, 0)
        lax.fori_loop(nu_ref[0], nblk, tail_wait, 0)

        def per_expert(e, carry):
            base = padrow_ref[e]
            n = npad_ref[e]

            def start(r, c):
                pltpu.make_async_copy(_row_tile(zbuf, 0), _row_tile(xs_hbm, base + r), zsem).start()
                return c

            def wait(r, c):
                pltpu.make_async_copy(_row_tile(zbuf, 0), _row_tile(xs_hbm, base + r), zsem).wait()
                return c

            lax.fori_loop(0, n, start, 0)
            lax.fori_loop(0, n, wait, 0)
            return carry

        lax.fori_loop(0, N_EXPERTS, per_expert, 0)

    def body(j, carry):
        for u in range(SUBLANES):
            t = j * SUBLANES + u
            for k in range(TOP_K):
                d = dest_ref[0, 0, t * TOP_K + k]
                pltpu.make_async_copy(_row_tile(hn_ref, t), _row_tile(xs_hbm, d), sem).start(
                    priority=(u * TOP_K + k) % 2)
        return carry

    lax.fori_loop(0, TOK_TILE // SUBLANES, body, 0)
    for _ in range(TOP_K):
        pltpu.make_async_copy(hn_ref, _row_tiles(xs_hbm, 0, TOK_TILE), sem).wait()


def _dispatch(padrow, npad, nused, dest_tiles, hn, n_rows):
    T = hn.shape[0] // SUBLANES
    grid_spec = pltpu.PrefetchScalarGridSpec(
        num_scalar_prefetch=3,
        grid=(T // TOK_TILE,),
        in_specs=[pl.BlockSpec((1, 1, TOK_TILE * TOP_K), lambda i, *_: (i, 0, 0), memory_space=pltpu.SMEM),
                  pl.BlockSpec((TOK_TILE * SUBLANES, LANES), lambda i, *_: (i, 0))],
        out_specs=pl.BlockSpec(memory_space=pl.ANY),
        scratch_shapes=[pltpu.VMEM((MOE_BLOCK * SUBLANES, LANES), F32), pltpu.SemaphoreType.DMA(()),
                        pltpu.SemaphoreType.DMA(())],
    )
    return pl.pallas_call(
        _dispatch_kernel,
        grid_spec=grid_spec,
        out_shape=jax.ShapeDtypeStruct((n_rows * SUBLANES, LANES), F32),
        compiler_params=pltpu.CompilerParams(dimension_semantics=("arbitrary",), vmem_limit_bytes=VMEM_LIMIT),
        name="moe_dispatch",
    )(padrow, npad, nused, dest_tiles, hn)


def _moe_kernel(be_ref, nu_ref, x_ref, wgu_ref, bgu_ref, wd_ref, bd_ref, y_ref, wgu_b, wd_b):
    i = pl.program_id(0)
    used = i < nu_ref[0]
    new_expert = jnp.logical_or(i == 0, be_ref[i] != be_ref[jnp.maximum(i - 1, 0)])

    @pl.when(jnp.logical_and(used, new_expert))
    def _():
        def cast(c, carry):
            rows = pl.ds(pl.multiple_of(c * CAST_ROWS, CAST_ROWS), CAST_ROWS)
            wgu_b[rows, :] = wgu_ref[0, rows, :].astype(BF16)
            wd_b[rows, :] = wd_ref[0, rows, :].astype(BF16)
            return carry
        lax.fori_loop(0, D_MODEL // CAST_ROWS, cast, 0)

    @pl.when(used)
    def _():
        x = _load_row_tiles(x_ref, MOE_BLOCK).astype(BF16)
        hb = _dot(x, wgu_b[...]) + bgu_ref[0]
        glu = jnp.minimum(hb[:, :D_FF], SWIGLU_LIMIT)
        lin = jnp.clip(hb[:, D_FF:], -SWIGLU_LIMIT, SWIGLU_LIMIT)
        act = glu * jax.nn.sigmoid(SWIGLU_ALPHA * glu) * (lin + 1.0)
        _store_row_tiles(y_ref, MOE_BLOCK, _dot(act.astype(BF16), wd_b[...]) + bd_ref[0])

    @pl.when(i >= nu_ref[0])
    def _():
        y_ref[...] = jnp.zeros_like(y_ref)


def _moe_blocks(block_e, nused, xs, wgu, bgu, wd, bd):
    nblk = block_e.shape[0]
    grid_spec = pltpu.PrefetchScalarGridSpec(
        num_scalar_prefetch=2,
        grid=(nblk,),
        in_specs=[
            pl.BlockSpec((MOE_BLOCK * SUBLANES, LANES), lambda i, be, nu: (i, 0)),
            pl.BlockSpec((1, D_MODEL, 2 * D_FF), lambda i, be, nu: (be[i], 0, 0)),
            pl.BlockSpec((1, 1, 2 * D_FF), lambda i, be, nu: (be[i], 0, 0)),
            pl.BlockSpec((1, D_FF, D_MODEL), lambda i, be, nu: (be[i], 0, 0)),
            pl.BlockSpec((1, 1, D_MODEL), lambda i, be, nu: (be[i], 0, 0)),
        ],
        out_specs=pl.BlockSpec((MOE_BLOCK * SUBLANES, LANES), lambda i, be, nu: (i, 0)),
        scratch_shapes=[pltpu.VMEM((D_MODEL, 2 * D_FF), BF16), pltpu.VMEM((D_FF, D_MODEL), BF16)],
    )
    return pl.pallas_call(
        _moe_kernel,
        grid_spec=grid_spec,
        out_shape=jax.ShapeDtypeStruct(xs.shape, F32),
        compiler_params=pltpu.CompilerParams(dimension_semantics=("arbitrary",), vmem_limit_bytes=MOE_VMEM_LIMIT),
        name="moe_blocks",
    )(block_e, nused, xs, wgu, bgu, wd, bd)


def _combine_kernel(npt, destc_ref, destn_ref, h_ref, route_ref, ys_hbm, yp_ref, ysm_ref, gbuf, gsem):
    i = pl.program_id(0)
    nt = pl.num_programs(0)
    slot = i % 2

    def issue(dest_ref, s):
        def body(j, carry):
            for u in range(SUBLANES):
                t = j * SUBLANES + u
                for k in range(TOP_K):
                    d = dest_ref[0, 0, t * TOP_K + k]
                    pltpu.make_async_copy(_row_tile(ys_hbm, d), _row_tile(gbuf.at[s, k], t), gsem.at[s]).start(
                        priority=(u * TOP_K + k) % 2)
            return carry
        lax.fori_loop(0, CMB_TILE // SUBLANES, body, 0)

    @pl.when(i == 0)
    def _():
        issue(destc_ref, 0)

    @pl.when(i + 1 < nt)
    def _():
        issue(destn_ref, 1 - slot)

    for k in range(TOP_K):
        pltpu.make_async_copy(_row_tiles(ys_hbm, 0, CMB_TILE), gbuf.at[slot, k], gsem.at[slot]).wait()
    r = route_ref[...]
    moe = _load_row_tiles(gbuf.at[slot, 0], CMB_TILE) * r[:, 0:1]
    for k in range(1, TOP_K):
        moe = moe + _load_row_tiles(gbuf.at[slot, k], CMB_TILE) * r[:, k:k + 1]
    y = h_ref[...] + moe

    @pl.when(i < npt)
    def _():
        yp_ref[...] = y

    @pl.when(i >= npt)
    def _():
        ysm_ref[...] = y


def _combine(dest_tiles, h, route, ys, n_prompt_rows):
    T = h.shape[0]
    nt = T // CMB_TILE
    npt = n_prompt_rows // CMB_TILE
    smem_blk = lambda imap: pl.BlockSpec((1, 1, CMB_TILE * TOP_K), imap, memory_space=pltpu.SMEM)
    return pl.pallas_call(
        functools.partial(_combine_kernel, npt),
        grid=(nt,),
        in_specs=[smem_blk(lambda i: (i, 0, 0)), smem_blk(lambda i: (jnp.minimum(i + 1, nt - 1), 0, 0)),
                  pl.BlockSpec((CMB_TILE, D_MODEL), lambda i: (i, 0)),
                  pl.BlockSpec((CMB_TILE, LANES), lambda i: (i, 0)),
                  pl.BlockSpec(memory_space=pl.ANY)],
        out_specs=(pl.BlockSpec((CMB_TILE, D_MODEL), lambda i: (jnp.minimum(i, npt - 1), 0)),
                   pl.BlockSpec((CMB_TILE, D_MODEL), lambda i: (jnp.maximum(i - npt, 0), 0))),
        out_shape=(jax.ShapeDtypeStruct((n_prompt_rows, D_MODEL), F32),
                   jax.ShapeDtypeStruct((T - n_prompt_rows, D_MODEL), F32)),
        scratch_shapes=[pltpu.VMEM((2, TOP_K, CMB_TILE * SUBLANES, LANES), F32), pltpu.SemaphoreType.DMA((2,))],
        compiler_params=pltpu.CompilerParams(dimension_semantics=("arbitrary",), vmem_limit_bytes=VMEM_LIMIT),
        name="moe_combine",
    )(dest_tiles, dest_tiles, h, route, ys)


def kernel(x_prompt, x_sample, cache_k_win, cache_v_win, state_conv, state_C, state_n, state_m, g_attn, w_in, b_i,
           b_f, q_norm_g, k_norm_g, sinks, conv_w, conv_b, m_norm_g, w_out, g_ffn, w_router, b_router, w_gate_up,
           b_gate_up, w_down, b_down):
    depth = g_attn.shape[0]
    assert depth == 1
    B, S, _ = x_prompt.shape
    DB, DS, _ = x_sample.shape
    TP = B * S
    TS = DB * DS
    T = TP + TS
    assert T % TOK_TILE == 0 and TP % TOK_TILE == 0 and S % ATT_QB == 0 and S % PROMPT_CHUNK == 0
    assert DS == SUBLANES and DB % SAMPLE_NB == 0 and (SAMPLE_NB * DS) == LANES
    l = 0

    xp = x_prompt.reshape(TP, D_MODEL)
    xs = x_sample.reshape(TS, D_MODEL)

    w_pad = jnp.pad(w_in[l], ((0, 0), (0, IN_PAD - w_in.shape[2]))).astype(BF16)
    wgt = jnp.transpose(w_in[l][:, GATE_COL:GATE_COL + 2 * M_HEADS]).astype(BF16)
    gi = jnp.arange(A_WIDTH) // HEAD_DIM
    gmat = (gi[:, None] == gi[None, :]).astype(BF16)
    qg = jnp.tile(q_norm_g[l], A_HEADS).reshape(1, A_WIDTH)
    kg = jnp.tile(k_norm_g[l], A_KV_HEADS).reshape(1, KV_WIDTH)
    gbias = jnp.concatenate([b_i[l], b_f[l]])
    gbrow = jnp.pad(gbias, (0, LANES - 2 * M_HEADS)).reshape(1, LANES)
    gbcol = gbias.reshape(2 * M_HEADS, 1)
    mng = m_norm_g[l].reshape(1, M_WIDTH)
    cw = conv_w[l]
    cb = conv_b[l].reshape(1, 2 * M_WIDTH)
    wr_pad = jnp.pad(w_router[l], ((0, 0), (0, LANES - N_EXPERTS))).astype(BF16)
    br_pad = jnp.concatenate([b_router[l], jnp.full((LANES - N_EXPERTS,), NEG, F32)]).reshape(1, LANES)

    qn, kn, va, qkm, vm, om, gcol, gt = _inproj(xp, xs, g_attn[l].reshape(1, D_MODEL), w_pad, wgt, gmat, qg, kg)

    a_p = _attn_prompt(sinks[l], qn, kn, va, B, S)
    ck = cache_k_win[l].reshape(DB, WINDOW, KV_WIDTH)
    cv = cache_v_win[l].reshape(DB, WINDOW, KV_WIDTH)
    a_s, kwin_s, vwin_s = _attn_sample(sinks[l], qn, kn, va, ck, cv, TP, DB, DS)

    zc = jnp.zeros((B, SUBLANES, 2 * M_WIDTH), F32)
    m_p, C_p, n_p, mm_p = _mlstm(
        qkm, vm, om, gcol, gt, zc,
        jnp.zeros((B, M_HEADS, M_HEAD_DIM, M_HEAD_DIM), F32), jnp.zeros((B, M_HEADS, M_HEAD_DIM), F32),
        jnp.full((B, 1, LANES), NEG, F32), cw, cb, gbrow, gbcol, mng,
        row0=0, nseq=1, L=PROMPT_CHUNK, ngroups=B, nchunks=S // PROMPT_CHUNK)
    conv_s0 = jnp.pad(state_conv[l], ((0, 0), (SUBLANES - (CONV_W - 1), 0), (0, 0)))
    m0_s = jnp.pad(state_m[l], ((0, 0), (0, LANES - M_HEADS))).reshape(DB, 1, LANES)
    m_s, C_s, n_s, mm_s = _mlstm(
        qkm, vm, om, gcol, gt, conv_s0, state_C[l], state_n[l], m0_s, cw, cb, gbrow, gbcol, mng,
        row0=TP, nseq=SAMPLE_NB, L=DS, ngroups=DB // SAMPLE_NB, nchunks=1)

    h, hn, route, cnt = _outproj(a_p, a_s, m_p, m_s, xp, xs, w_out[l].astype(BF16), g_ffn[l].reshape(1, D_MODEL),
                                 wr_pad, br_pad)

    nblk = T * TOP_K // MOE_BLOCK + N_EXPERTS
    dest, blk, info = _route_tables(route, cnt, nblk)
    block_e = blk[:, 0]
    padrow = info[0, :N_EXPERTS]
    npad = info[1, :N_EXPERTS]
    nused = info[2, N_EXPERTS - 1:N_EXPERTS]
    dest4 = dest[:, :TOP_K]
    xrows = _dispatch(padrow, npad, nused, dest4.reshape(T // TOK_TILE, 1, TOK_TILE * TOP_K), hn, nblk * MOE_BLOCK)
    yrows = _moe_blocks(block_e, nused, xrows,
                        w_gate_up[l], b_gate_up[l].reshape(N_EXPERTS, 1, 2 * D_FF),
                        w_down[l], b_down[l].reshape(N_EXPERTS, 1, D_MODEL))
    y_p, y_s = _combine(dest4.reshape(T // CMB_TILE, 1, CMB_TILE * TOP_K), h, route, yrows, TP)

    y_p = y_p.reshape(B, S, D_MODEL)
    y_s = y_s.reshape(DB, DS, D_MODEL)
    def seq_tail(rows, n):
        return jnp.stack([rows[(b + 1) * S - n:(b + 1) * S] for b in range(B)])

    kwin_p = seq_tail(kn, WINDOW).reshape(B, WINDOW, A_KV_HEADS, HEAD_DIM)
    vwin_p = seq_tail(va, WINDOW).reshape(B, WINDOW, A_KV_HEADS, HEAD_DIM)
    qkm_s = qkm[TP:].reshape(DB, DS, 2 * M_WIDTH)
    return (y_p, y_s,
            kwin_p[None], vwin_p[None], seq_tail(qkm, CONV_W - 1)[None],
            C_p[None], n_p[None], mm_p[:, 0, :M_HEADS][None],
            kwin_s.reshape(DB, WINDOW, A_KV_HEADS, HEAD_DIM)[None],
            vwin_s.reshape(DB, WINDOW, A_KV_HEADS, HEAD_DIM)[None],
            qkm_s[:, -(CONV_W - 1):][None],
            C_s[None], n_s[None], mm_s[:, 0, :M_HEADS][None])
```

```python
import functools

import jax
import jax.numpy as jnp
from jax import lax
from jax.experimental import pallas as pl
from jax.experimental.pallas import tpu as pltpu

F32 = jnp.float32
BF16 = jnp.bfloat16

D_MODEL = 1024
HEAD_DIM = 64
A_HEADS = 8
A_KV_HEADS = 2
A_GROUP = A_HEADS // A_KV_HEADS
A_WIDTH = A_HEADS * HEAD_DIM
KV_WIDTH = A_KV_HEADS * HEAD_DIM
WINDOW = 128
M_HEADS = 4
M_HEAD_DIM = 128
M_WIDTH = M_HEADS * M_HEAD_DIM
CONV_W = 4
N_EXPERTS = 32
TOP_K = 4
D_FF = D_MODEL
SWIGLU_LIMIT = 7.0
SWIGLU_ALPHA = 1.702
MOE_BLOCK = 512
EPS = 1e-6
NEG = -1e30

LANES = 128
SUBLANES = 8
GATE_COL = A_WIDTH + 2 * KV_WIDTH + 4 * M_WIDTH
IN_PAD = GATE_COL + LANES
TOK_TILE = 512
PROJ_TILE = 1024
PROJ_VMEM_LIMIT = 58 * 1024 * 1024
ATT_QB = 512
ATT_SB = 128
SAMPLE_NB = 16
PROMPT_CHUNK = 256
CMB_TILE = 256
CAST_ROWS = 128
VMEM_LIMIT = 48 * 1024 * 1024
MOE_VMEM_LIMIT = 58 * 1024 * 1024
assert D_FF == D_MODEL


def _dot(a, b):
    return jnp.dot(a, b, preferred_element_type=F32)


def _dot_nt(a, b):
    return lax.dot_general(a, b, (((1,), (1,)), ((), ())), preferred_element_type=F32)


def _dot_tn(a, b):
    return lax.dot_general(a, b, (((0,), (0,)), ((), ())), preferred_element_type=F32)


def _split3(x):
    hi = x.astype(BF16)
    r1 = x - hi.astype(F32)
    mid = r1.astype(BF16)
    lo = (r1 - mid.astype(F32)).astype(BF16)
    return hi, mid, lo


def _log_sigmoid(x):
    return jnp.minimum(x, 0.0) - jnp.log1p(jnp.exp(-jnp.abs(x)))


def _load_row_tiles(ref2, rows):
    return jnp.concatenate([ref2[pl.ds(s, rows, stride=SUBLANES), :] for s in range(SUBLANES)], axis=1)


def _store_row_tiles(ref2, rows, val):
    for s in range(SUBLANES):
        ref2[pl.ds(s, rows, stride=SUBLANES), :] = val[:, s * LANES:(s + 1) * LANES]


def _row_tile(ref2, idx):
    return ref2.at[pl.ds(pl.multiple_of(idx * SUBLANES, SUBLANES), SUBLANES), :]


def _row_tiles(ref2, first, n):
    return ref2.at[pl.ds(pl.multiple_of(first * SUBLANES, SUBLANES), n * SUBLANES), :]


def _split_specs(n_prompt_tiles, rows, width):
    return (pl.BlockSpec((rows, width), lambda i, *_: (jnp.minimum(i, n_prompt_tiles - 1), 0)),
            pl.BlockSpec((rows, width), lambda i, *_: (jnp.maximum(i - n_prompt_tiles, 0), 0)))


def _inproj_kernel(x_ref, g_ref, w_ref, wgt_ref, gmat_ref, qg_ref, kg_ref,
                   qn_ref, kn_ref, va_ref, qkm_ref, vm_ref, om_ref, gcol_ref, gt_ref):
    x = x_ref[...]
    ms = jnp.mean(x * x, axis=-1, keepdims=True)
    xn = ((x * lax.rsqrt(ms + EPS)) * g_ref[...]).astype(BF16)

    def seg(lo, hi):
        return _dot(xn, w_ref[:, lo:hi])

    def head_norm(z, gmat, g):
        hi, mid, lo = _split3(z * z)
        ss = _dot(hi, gmat) + _dot(mid, gmat) + _dot(lo, gmat)
        return (z * lax.rsqrt(ss * (1.0 / HEAD_DIM) + EPS)) * g

    o0 = A_WIDTH
    o1 = o0 + KV_WIDTH
    o2 = o1 + KV_WIDTH
    o3 = o2 + 2 * M_WIDTH
    o4 = o3 + M_WIDTH
    o5 = o4 + M_WIDTH
    qn_ref[...] = head_norm(seg(0, o0), gmat_ref[...], qg_ref[...])
    kn_ref[...] = head_norm(seg(o0, o1), gmat_ref[:KV_WIDTH, :KV_WIDTH], kg_ref[...])
    va_ref[...] = seg(o1, o2)
    qkm_ref[...] = seg(o2, o3)
    vm_ref[...] = seg(o3, o4)
    om_ref[...] = seg(o4, o5)
    gcol_ref[...] = seg(o5, o5 + LANES)
    gt_ref[0] = _dot_nt(wgt_ref[...], xn)


def _inproj(x, g_attn, w_pad, wgt, gmat, qg, kg):
    T = x.shape[0]
    nt = T // PROJ_TILE
    row = lambda w: pl.BlockSpec((PROJ_TILE, w), lambda i: (i, 0))
    full = lambda a: pl.BlockSpec(a.shape, lambda i: (0,) * a.ndim, pipeline_mode=pl.Buffered(1))
    out_shape = (
        jax.ShapeDtypeStruct((T, A_WIDTH), F32),
        jax.ShapeDtypeStruct((T, KV_WIDTH), F32),
        jax.ShapeDtypeStruct((T, KV_WIDTH), F32),
        jax.ShapeDtypeStruct((T, 2 * M_WIDTH), F32),
        jax.ShapeDtypeStruct((T, M_WIDTH), F32),
        jax.ShapeDtypeStruct((T, M_WIDTH), F32),
        jax.ShapeDtypeStruct((T, LANES), F32),
        jax.ShapeDtypeStruct((nt, SUBLANES, PROJ_TILE), F32),
    )
    out_specs = (row(A_WIDTH), row(KV_WIDTH), row(KV_WIDTH), row(2 * M_WIDTH), row(M_WIDTH), row(M_WIDTH),
                 row(LANES), pl.BlockSpec((1, SUBLANES, PROJ_TILE), lambda i: (i, 0, 0)))
    return pl.pallas_call(
        _inproj_kernel,
        grid=(nt,),
        in_specs=[row(D_MODEL), full(g_attn), full(w_pad), full(wgt), full(gmat), full(qg), full(kg)],
        out_specs=out_specs,
        out_shape=out_shape,
        compiler_params=pltpu.CompilerParams(dimension_semantics=("parallel",), vmem_limit_bytes=PROJ_VMEM_LIMIT),
        name="inproj",
    )(x, g_attn, w_pad, wgt, gmat, qg, kg)


def _softmax_sink(pieces, masks, sink_col):
    masked = [jnp.where(mk, s, NEG) for s, mk in zip(pieces, masks)]
    m = sink_col
    for s in masked:
        m = jnp.maximum(m, jnp.max(s, axis=-1, keepdims=True))
    ps = [jnp.exp(s - m) for s in masked]
    den = jnp.exp(sink_col - m)
    for p in ps:
        den = den + jnp.sum(p, axis=-1, keepdims=True)
    return ps, 1.0 / den


def _stack_heads(q, g):
    return jnp.concatenate([q[:, (A_GROUP * g + i) * HEAD_DIM:(A_GROUP * g + i + 1) * HEAD_DIM]
                            for i in range(A_GROUP)], axis=0)


def _sink_col(sink_ref, g, rows_per_head):
    r = lax.broadcasted_iota(jnp.int32, (A_GROUP * rows_per_head, 1), 0)
    col = jnp.zeros((A_GROUP * rows_per_head, 1), F32)
    for i in range(A_GROUP):
        col = jnp.where(r // rows_per_head == i, sink_ref[A_GROUP * g + i], col)
    return col


def _attn_prompt_kernel(sink_ref, q_ref, kp_ref, kc_ref, vp_ref, vc_ref, o_ref):
    j = pl.program_id(1)
    scale = HEAD_DIM ** -0.5
    kall = jnp.concatenate([kp_ref[...], kc_ref[...]], axis=0).astype(BF16)
    vall = jnp.concatenate([vp_ref[...], vc_ref[...]], axis=0).astype(BF16)
    nrow = A_GROUP * ATT_SB
    r = lax.broadcasted_iota(jnp.int32, (nrow, 2 * ATT_SB), 0) % ATT_SB
    c = lax.broadcasted_iota(jnp.int32, (nrow, 2 * ATT_SB), 1)
    band = jnp.logical_and(c >= r, c <= r + WINDOW)
    band0 = jnp.logical_and(band, jnp.logical_or(c >= ATT_SB, j > 0))
    for sb in range(ATT_QB // ATT_SB):
        q = (q_ref[sb * ATT_SB:(sb + 1) * ATT_SB, :] * scale).astype(BF16)
        kwin = kall[sb * ATT_SB:(sb + 2) * ATT_SB]
        vwin = vall[sb * ATT_SB:(sb + 2) * ATT_SB]
        outs = []
        for g in range(A_KV_HEADS):
            lo, hi = g * HEAD_DIM, (g + 1) * HEAD_DIM
            s = _dot_nt(_stack_heads(q, g), kwin[:, lo:hi])
            (p,), inv = _softmax_sink([s], [band0 if sb == 0 else band], _sink_col(sink_ref, g, ATT_SB))
            o = _dot(p.astype(BF16), vwin[:, lo:hi]) * inv
            outs += [o[i * ATT_SB:(i + 1) * ATT_SB] for i in range(A_GROUP)]
        o_ref[sb * ATT_SB:(sb + 1) * ATT_SB, :] = jnp.concatenate(outs, axis=1)


def _attn_prompt(sinks, qn, kn, va, batch, seq):
    nq = seq // ATT_QB
    ratio = ATT_QB // ATT_SB
    cur = lambda w: pl.BlockSpec((ATT_QB, w), lambda b, j: (b * nq + j, 0))
    prev = lambda w: pl.BlockSpec((ATT_SB, w), lambda b, j: (jnp.maximum((b * nq + j) * ratio - 1, 0), 0))
    return pl.pallas_call(
        _attn_prompt_kernel,
        grid=(batch, nq),
        in_specs=[pl.BlockSpec(memory_space=pltpu.SMEM), cur(A_WIDTH), prev(KV_WIDTH), cur(KV_WIDTH),
                  prev(KV_WIDTH), cur(KV_WIDTH)],
        out_specs=cur(A_WIDTH),
        out_shape=jax.ShapeDtypeStruct((batch * seq, A_WIDTH), F32),
        compiler_params=pltpu.CompilerParams(dimension_semantics=("parallel", "parallel"),
                                             vmem_limit_bytes=VMEM_LIMIT),
        name="attn_prompt",
    )(sinks, qn, kn, kn, va, va)


def _attn_sample_kernel(dec, sink_ref, q_ref, kn_ref, vn_ref, ck_ref, cv_ref, o_ref, kw_ref, vw_ref):
    scale = HEAD_DIM ** -0.5
    rows = SAMPLE_NB * dec
    knew = kn_ref[...]
    vnew = vn_ref[...]
    knew_b = knew.astype(BF16)
    vnew_b = vnew.astype(BF16)
    nrow = A_GROUP * dec
    t = lax.broadcasted_iota(jnp.int32, (nrow, WINDOW), 0) % dec
    c = lax.broadcasted_iota(jnp.int32, (nrow, WINDOW), 1)
    m_cache = c >= t
    cn = lax.broadcasted_iota(jnp.int32, (nrow, rows), 1)
    tn = lax.broadcasted_iota(jnp.int32, (nrow, rows), 0) % dec
    for i in range(SAMPLE_NB):
        q = (q_ref[i * dec:(i + 1) * dec, :] * scale).astype(BF16)
        ck = ck_ref[i].astype(BF16)
        cv = cv_ref[i].astype(BF16)
        m_new = jnp.logical_and(cn // dec == i, cn % dec <= tn)
        outs = []
        for g in range(A_KV_HEADS):
            lo, hi = g * HEAD_DIM, (g + 1) * HEAD_DIM
            qs = _stack_heads(q, g)
            s_c = _dot_nt(qs, ck[:, lo:hi])
            s_n = _dot_nt(qs, knew_b[:, lo:hi])
            (p_c, p_n), inv = _softmax_sink([s_c, s_n], [m_cache, m_new], _sink_col(sink_ref, g, dec))
            o = (_dot(p_c.astype(BF16), cv[:, lo:hi]) + _dot(p_n.astype(BF16), vnew_b[:, lo:hi])) * inv
            outs += [o[h * dec:(h + 1) * dec] for h in range(A_GROUP)]
        o_ref[i * dec:(i + 1) * dec, :] = jnp.concatenate(outs, axis=1)
        kw_ref[i, 0:WINDOW - dec, :] = ck_ref[i, dec:WINDOW, :]
        kw_ref[i, WINDOW - dec:WINDOW, :] = knew[i * dec:(i + 1) * dec]
        vw_ref[i, 0:WINDOW - dec, :] = cv_ref[i, dec:WINDOW, :]
        vw_ref[i, WINDOW - dec:WINDOW, :] = vnew[i * dec:(i + 1) * dec]


def _attn_sample(sinks, qn, kn, va, ck, cv, dbatch, dec):
    rows = SAMPLE_NB * dec
    tokrow = lambda w: pl.BlockSpec((rows, w), lambda i: (i, 0))
    cache = pl.BlockSpec((SAMPLE_NB, WINDOW, KV_WIDTH), lambda i: (i, 0, 0))
    return pl.pallas_call(
        functools.partial(_attn_sample_kernel, dec),
        grid=(dbatch // SAMPLE_NB,),
        in_specs=[pl.BlockSpec(memory_space=pltpu.SMEM), tokrow(A_WIDTH), tokrow(KV_WIDTH), tokrow(KV_WIDTH),
                  cache, cache],
        out_specs=(pl.BlockSpec((rows, A_WIDTH), lambda i: (i, 0)), cache, cache),
        out_shape=(jax.ShapeDtypeStruct((dbatch * dec, A_WIDTH), F32),
                   jax.ShapeDtypeStruct((dbatch, WINDOW, KV_WIDTH), F32),
                   jax.ShapeDtypeStruct((dbatch, WINDOW, KV_WIDTH), F32)),
        compiler_params=pltpu.CompilerParams(dimension_semantics=("parallel",), vmem_limit_bytes=VMEM_LIMIT),
        name="attn_sample",
    )(sinks, qn, kn, va, ck, cv)


def _mlstm_kernel(nseq, L, qk_ref, v_ref, o_ref, gcol_ref, gt_ref, conv0_ref, c0_ref, n0_ref, m0_ref,
                  cw_ref, cb_ref, gbrow_ref, gbcol_ref, mng_ref, mask_ref,
                  out_ref, cst_ref, nst_ref, mst_ref, prev_ref):
    R = nseq * L
    ci = pl.program_id(1)

    @pl.when(ci == 0)
    def _():
        cst_ref[...] = c0_ref[...]
        nst_ref[...] = n0_ref[...]
        mst_ref[...] = m0_ref[...]
        prev_ref[...] = conv0_ref[0]

    raw = qk_ref[...]
    row = lax.broadcasted_iota(jnp.int32, (R, 1), 0)
    tpos = row % L
    rseq = row // L
    acc = raw * cw_ref[CONV_W - 1:CONV_W, :] + cb_ref[...]
    if nseq == 1:
        prev8 = prev_ref[...]
        t8 = lax.broadcasted_iota(jnp.int32, (SUBLANES, 1), 0)
    else:
        prevsrc = conv0_ref[...].reshape(R, 2 * M_WIDTH)
    for k in range(1, CONV_W):
        rolled = pltpu.roll(raw, k, 0)
        if nseq == 1:
            head = jnp.where(t8 >= k, rolled[0:SUBLANES], pltpu.roll(prev8, k, 0))
            sh = jnp.concatenate([head, rolled[SUBLANES:]], axis=0)
        else:
            sh = jnp.where(tpos >= k, rolled, pltpu.roll(prevsrc, R - SUBLANES + k, 0))
        acc = acc + sh * cw_ref[CONV_W - 1 - k:CONV_W - k, :]
    if nseq == 1:
        prev_ref[...] = raw[R - SUBLANES:R]
    qkc = acc * jax.nn.sigmoid(acc)

    gc = gcol_ref[...] + gbrow_ref[...]
    gr = gt_ref[0] + gbcol_ref[...]
    lsc = _log_sigmoid(gc)
    lsr = _log_sigmoid(gr)
    mb = mask_ref[...]
    maskb = mb > 0
    bcol = sum(_dot(mb, p) for p in _split3(lsc))
    brow = sum(_dot_nt(p, mb) for p in _split3(lsr))

    lane = lax.broadcasted_iota(jnp.int32, (1, LANES), 1)
    m_new = [jnp.zeros((1, LANES), F32) for _ in range(nseq)]
    for h in range(M_HEADS):
        sl = slice(h * M_HEAD_DIM, (h + 1) * M_HEAD_DIM)
        qh = qkc[:, sl]
        kh = qkc[:, M_WIDTH + h * M_HEAD_DIM:M_WIDTH + (h + 1) * M_HEAD_DIM] * (M_HEAD_DIM ** -0.5)
        vh = v_ref[:, sl]
        qb, kb, vb = qh.astype(BF16), kh.astype(BF16), vh.astype(BF16)
        ig_c = gc[:, h:h + 1]
        b_c = bcol[:, M_HEADS + h:M_HEADS + h + 1]
        ig_r = gr[h:h + 1, :]
        b_r = brow[M_HEADS + h:M_HEADS + h + 1, :]
        if nseq == 1:
            m0c = mst_ref[0][:, h:h + 1]
            n0rows = nst_ref[0, h:h + 1, :]
        else:
            m0c = jnp.zeros((R, 1), F32)
            n0rows = jnp.zeros((R, M_HEAD_DIM), F32)
            for s in range(nseq):
                m0c = jnp.where(rseq == s, mst_ref[s][:, h:h + 1], m0c)
                n0rows = jnp.where(rseq == s, nst_ref[s, h:h + 1, :], n0rows)
        dm = jnp.where(maskb, b_c - b_r + ig_r, NEG)
        a_c = b_c + m0c
        m_c = jnp.maximum(a_c, jnp.max(dm, axis=-1, keepdims=True))
        w = jnp.exp(dm - m_c)
        sc = jnp.exp(a_c - m_c)
        wqk = w * _dot_nt(qb, kb)
        if nseq == 1:
            inter = _dot_nt(qb, cst_ref[0, h].astype(BF16))
        else:
            inter = jnp.zeros((R, M_HEAD_DIM), F32)
            for s in range(nseq):
                qs = jnp.where(rseq == s, qh, 0.0).astype(BF16)
                inter = inter + _dot_nt(qs, cst_ref[s, h].astype(BF16))
        num = _dot(wqk.astype(BF16), vb) + sc * inter
        den = jnp.sum(wqk, axis=-1, keepdims=True) + sc * jnp.sum(qh * n0rows, axis=-1, keepdims=True)
        hh = num / jnp.maximum(jnp.abs(den), jnp.exp(-m_c))

        for s in range(nseq):
            e = s * L + L - 1
            m_end = m_c[e:e + 1, :]
            wend = jnp.exp(b_c[e:e + 1, :] - b_c + ig_c - m_end)
            if nseq > 1:
                wend = jnp.where(rseq == s, wend, 0.0)
            sce = jnp.exp(a_c[e:e + 1, :] - m_end)
            c_new = sce * cst_ref[s, h] + _dot_tn((vh * wend).astype(BF16), kb)
            n_new = sce * nst_ref[s, h:h + 1, :] + jnp.sum(wend * kh, axis=0, keepdims=True)
            cst_ref[s, h] = c_new
            nst_ref[s, h:h + 1, :] = n_new
            m_new[s] = jnp.where(lane == h, m_end, m_new[s])

        hn = (hh * lax.rsqrt(jnp.mean(hh * hh, axis=-1, keepdims=True) + EPS)) * mng_ref[:, sl]
        out_ref[:, sl] = jax.nn.sigmoid(o_ref[:, sl]) * hn
    for s in range(nseq):
        mst_ref[s] = m_new[s]


def _mlstm(qkm, vm, om, gcol, gt, conv0, c0, n0, m0, cw, cb, gbrow, gbcol, mng, nseq, L, ngroups, nchunks):
    R = nseq * L
    per_tile = gt.shape[2] // R
    tok = lambda w: pl.BlockSpec((R, w), lambda g, c: (g * nchunks + c, 0))
    gt_spec = pl.BlockSpec((1, SUBLANES, R),
                           lambda g, c: ((g * nchunks + c) // per_tile, 0, (g * nchunks + c) % per_tile))
    full = lambda a: pl.BlockSpec(a.shape, lambda g, c: (0,) * a.ndim)
    st4 = pl.BlockSpec((nseq, M_HEADS, M_HEAD_DIM, M_HEAD_DIM), lambda g, c: (g, 0, 0, 0))
    st3 = pl.BlockSpec((nseq, M_HEADS, M_HEAD_DIM), lambda g, c: (g, 0, 0))
    stm = pl.BlockSpec((nseq, 1, LANES), lambda g, c: (g, 0, 0))
    conv_spec = pl.BlockSpec((nseq, SUBLANES, 2 * M_WIDTH), lambda g, c: (g, 0, 0))
    r = jnp.arange(R)
    mask = ((r[:, None] // L == r[None, :] // L) & (r[None, :] <= r[:, None])).astype(BF16)
    nstate = ngroups * nseq
    return pl.pallas_call(
        functools.partial(_mlstm_kernel, nseq, L),
        grid=(ngroups, nchunks),
        in_specs=[tok(2 * M_WIDTH), tok(M_WIDTH), tok(M_WIDTH), tok(LANES), gt_spec, conv_spec, st4, st3, stm,
                  full(cw), full(cb), full(gbrow), full(gbcol), full(mng), full(mask)],
        out_specs=(pl.BlockSpec((R, M_WIDTH), lambda g, c: (g * nchunks + c, 0)), st4, st3, stm),
        out_shape=(jax.ShapeDtypeStruct((ngroups * nchunks * R, M_WIDTH), F32),
                   jax.ShapeDtypeStruct((nstate, M_HEADS, M_HEAD_DIM, M_HEAD_DIM), F32),
                   jax.ShapeDtypeStruct((nstate, M_HEADS, M_HEAD_DIM), F32),
                   jax.ShapeDtypeStruct((nstate, 1, LANES), F32)),
        scratch_shapes=[pltpu.VMEM((SUBLANES, 2 * M_WIDTH), F32)],
        compiler_params=pltpu.CompilerParams(dimension_semantics=("parallel", "arbitrary"),
                                             vmem_limit_bytes=VMEM_LIMIT),
        name="mlstm_n%d" % nseq,
    )(qkm, vm, om, gcol, gt, conv0, c0, n0, m0, cw, cb, gbrow, gbcol, mng, mask)


def _outproj_kernel(npt, ap_ref, as_ref, mp_ref, ms_ref, xp_ref, xs_ref, wo_ref, g_ref, wr_ref, br_ref,
                    h_ref, hn_ref, route_ref, cnt_ref):
    def project(a_ref, m_ref, x_ref):
        h_ref[...] = (x_ref[...] + _dot(a_ref[...].astype(BF16), wo_ref[0:A_WIDTH, :])
                      + _dot(m_ref[...].astype(BF16), wo_ref[A_WIDTH:A_WIDTH + M_WIDTH, :]))

    @pl.when(pl.program_id(0) < npt)
    def _():
        project(ap_ref, mp_ref, xp_ref)

    @pl.when(pl.program_id(0) >= npt)
    def _():
        project(as_ref, ms_ref, xs_ref)

    h = h_ref[...]
    hn = (h * lax.rsqrt(jnp.mean(h * h, axis=-1, keepdims=True) + EPS)) * g_ref[...]
    _store_row_tiles(hn_ref, TOK_TILE, hn)
    logits = _dot(hn.astype(BF16), wr_ref[...]) + br_ref[...]
    lane = lax.broadcasted_iota(jnp.int32, logits.shape, 1)
    route = jnp.zeros(logits.shape, F32)
    picked = jnp.zeros(logits.shape, F32)
    top0 = None
    den = None
    es = []
    for k in range(TOP_K):
        mx = jnp.max(logits, axis=-1, keepdims=True)
        idx = jnp.min(jnp.where(logits == mx, lane, LANES), axis=-1, keepdims=True)
        if k == 0:
            top0 = mx
        e = jnp.exp(mx - top0)
        den = e if den is None else den + e
        es.append(e)
        route = jnp.where(lane == TOP_K + k, idx.astype(F32), route)
        picked = jnp.where(lane == idx, 1.0, picked)
        logits = jnp.where(lane == idx, -jnp.inf, logits)
    for k in range(TOP_K):
        route = jnp.where(lane == k, es[k] / den, route)
    route_ref[...] = route

    @pl.when(pl.program_id(0) == 0)
    def _():
        cnt_ref[...] = jnp.zeros_like(cnt_ref)

    cnt_ref[0:1, :] += jnp.sum(picked, axis=0, keepdims=True)


def _outproj(a_p, a_s, m_p, m_s, xp, xs, w_out, g_ffn, wr_pad, br_pad):
    T = xp.shape[0] + xs.shape[0]
    npt = xp.shape[0] // TOK_TILE
    row = lambda w: pl.BlockSpec((TOK_TILE, w), lambda i: (i, 0))
    full = lambda a: pl.BlockSpec(a.shape, lambda i: (0,) * a.ndim)
    return pl.pallas_call(
        functools.partial(_outproj_kernel, npt),
        grid=(T // TOK_TILE,),
        in_specs=[*_split_specs(npt, TOK_TILE, A_WIDTH), *_split_specs(npt, TOK_TILE, M_WIDTH),
                  *_split_specs(npt, TOK_TILE, D_MODEL), full(w_out), full(g_ffn), full(wr_pad), full(br_pad)],
        out_specs=(row(D_MODEL), pl.BlockSpec((TOK_TILE * SUBLANES, LANES), lambda i: (i, 0)), row(LANES),
                   pl.BlockSpec((SUBLANES, LANES), lambda i: (0, 0))),
        out_shape=(jax.ShapeDtypeStruct((T, D_MODEL), F32), jax.ShapeDtypeStruct((T * SUBLANES, LANES), F32),
                   jax.ShapeDtypeStruct((T, LANES), F32), jax.ShapeDtypeStruct((SUBLANES, LANES), F32)),
        compiler_params=pltpu.CompilerParams(dimension_semantics=("arbitrary",), vmem_limit_bytes=VMEM_LIMIT),
        name="outproj_router",
    )(a_p, a_s, m_p, m_s, xp, xs, w_out, g_ffn, wr_pad, br_pad)


def _route_kernel(nblk, route_ref, cnt_ref, lstrict_ref, ustrict_ref, dest_ref, blk_ref, info_ref, carry_ref):
    i = pl.program_id(0)
    lane = lax.broadcasted_iota(jnp.int32, (1, LANES), 1)
    cnt = cnt_ref[0:1, :]
    nb_e = jnp.floor((cnt + (MOE_BLOCK - 1.0)) * (1.0 / MOE_BLOCK))
    u = ustrict_ref[...]
    bstart = sum(_dot(p, u) for p in _split3(jnp.broadcast_to(nb_e, (SUBLANES, LANES))))[0:1, :]
    bend = bstart + nb_e
    row_start = bstart * float(MOE_BLOCK)

    @pl.when(i == 0)
    def _():
        carry_ref[...] = jnp.zeros_like(carry_ref)
        bi = lax.broadcasted_iota(jnp.int32, (nblk, LANES), 0).astype(F32)
        done = jnp.logical_and(bend <= bi, lane < N_EXPERTS)
        be = jnp.minimum(jnp.sum(jnp.where(done, 1.0, 0.0), axis=-1, keepdims=True), N_EXPERTS - 1.0)
        blk_ref[...] = jnp.broadcast_to(be, (nblk, LANES)).astype(jnp.int32)
        info = jnp.zeros((SUBLANES, LANES), F32)
        srow = lax.broadcasted_iota(jnp.int32, (SUBLANES, LANES), 0)
        info = jnp.where(srow == 0, row_start + cnt, info)
        info = jnp.where(srow == 1, nb_e * float(MOE_BLOCK) - cnt, info)
        info = jnp.where(srow == 2, bend, info)
        info_ref[...] = info.astype(jnp.int32)

    r = route_ref[...]
    lane_f = lane.astype(F32)
    sel = [lane_f == r[:, TOP_K + k:TOP_K + k + 1] for k in range(TOP_K)]
    oh = jnp.zeros(r.shape, F32)
    for k in range(TOP_K):
        oh = jnp.where(sel[k], 1.0, oh)
    before = _dot(lstrict_ref[...], oh.astype(BF16)) + carry_ref[...] + row_start
    dest = jnp.zeros(r.shape, jnp.int32)
    for k in range(TOP_K):
        d = jnp.sum(jnp.where(sel[k], before, 0.0), axis=-1, keepdims=True)
        dest = jnp.where(lane == k, d.astype(jnp.int32), dest)
    dest_ref[...] = dest
    carry_ref[...] += jnp.sum(oh, axis=0, keepdims=True)


def _route_tables(route, cnt, nblk):
    T = route.shape[0]
    a = jnp.arange(TOK_TILE)
    lstrict = (a[:, None] > a[None, :]).astype(BF16)
    b = jnp.arange(LANES)
    ustrict = (b[:, None] < b[None, :]).astype(BF16)
    full = lambda x: pl.BlockSpec(x.shape, lambda i: (0,) * x.ndim)
    return pl.pallas_call(
        functools.partial(_route_kernel, nblk),
        grid=(T // TOK_TILE,),
        in_specs=[pl.BlockSpec((TOK_TILE, LANES), lambda i: (i, 0)), full(cnt), full(lstrict), full(ustrict)],
        out_specs=(pl.BlockSpec((TOK_TILE, LANES), lambda i: (i, 0)),
                   pl.BlockSpec((nblk, LANES), lambda i: (0, 0)),
                   pl.BlockSpec((SUBLANES, LANES), lambda i: (0, 0))),
        out_shape=(jax.ShapeDtypeStruct((T, LANES), jnp.int32), jax.ShapeDtypeStruct((nblk, LANES), jnp.int32),
                   jax.ShapeDtypeStruct((SUBLANES, LANES), jnp.int32)),
        scratch_shapes=[pltpu.VMEM((1, LANES), F32)],
        compiler_params=pltpu.CompilerParams(dimension_semantics=("arbitrary",), vmem_limit_bytes=VMEM_LIMIT),
        name="route_tables",
    )(route, cnt, lstrict, ustrict)


def _dispatch_kernel(padrow_ref, npad_ref, nu_ref, dest_ref, hn_ref, xs_hbm, zbuf, sem, zsem):
    i = pl.program_id(0)

    @pl.when(i == 0)
    def _():
        zbuf[...] = jnp.zeros_like(zbuf)
        nblk = xs_hbm.shape[0] // (MOE_BLOCK * SUBLANES)

        def tail_start(b, c):
            pltpu.make_async_copy(zbuf, _row_tiles(xs_hbm, b * MOE_BLOCK, MOE_BLOCK), zsem).start()
            return c

        def tail_wait(b, c):
            pltpu.make_async_copy(zbuf, _row_tiles(xs_hbm, b * MOE_BLOCK, MOE_BLOCK), zsem).wait()
            return c

        lax.fori_loop(nu_ref[0], nblk, tail_start, 0)
        lax.fori_loop(nu_ref[0], nblk, tail_wait, 0)

        def per_expert(e, carry):
            base = padrow_ref[e]
            n = npad_ref[e]

            def start(r, c):
                pltpu.make_async_copy(_row_tile(zbuf, 0), _row_tile(xs_hbm, base + r), zsem).start()
                return c

            def wait(r, c):
                pltpu.make_async_copy(_row_tile(zbuf, 0), _row_tile(xs_hbm, base + r), zsem).wait()
                return c

            lax.fori_loop(0, n, start, 0)
            lax.fori_loop(0, n, wait, 0)
            return carry

        lax.fori_loop(0, N_EXPERTS, per_expert, 0)

    def body(j, carry):
        for u in range(SUBLANES):
            t = j * SUBLANES + u
            for k in range(TOP_K):
                d = dest_ref[0, 0, t * TOP_K + k]
                pltpu.make_async_copy(_row_tile(hn_ref, t), _row_tile(xs_hbm, d), sem).start(
                    priority=(u * TOP_K + k) % 2)
        return carry

    lax.fori_loop(0, TOK_TILE // SUBLANES, body, 0)
    for _ in range(TOP_K):
        pltpu.make_async_copy(hn_ref, _row_tiles(xs_hbm, 0, TOK_TILE), sem).wait()


def _dispatch(padrow, npad, nused, dest_tiles, hn, n_rows):
    T = hn.shape[0] // SUBLANES
    grid_spec = pltpu.PrefetchScalarGridSpec(
        num_scalar_prefetch=3,
        grid=(T // TOK_TILE,),
        in_specs=[pl.BlockSpec((1, 1, TOK_TILE * TOP_K), lambda i, *_: (i, 0, 0), memory_space=pltpu.SMEM),
                  pl.BlockSpec((TOK_TILE * SUBLANES, LANES), lambda i, *_: (i, 0))],
        out_specs=pl.BlockSpec(memory_space=pl.ANY),
        scratch_shapes=[pltpu.VMEM((MOE_BLOCK * SUBLANES, LANES), F32), pltpu.SemaphoreType.DMA(()),
                        pltpu.SemaphoreType.DMA(())],
    )
    return pl.pallas_call(
        _dispatch_kernel,
        grid_spec=grid_spec,
        out_shape=jax.ShapeDtypeStruct((n_rows * SUBLANES, LANES), F32),
        compiler_params=pltpu.CompilerParams(dimension_semantics=("arbitrary",), vmem_limit_bytes=VMEM_LIMIT),
        name="moe_dispatch",
    )(padrow, npad, nused, dest_tiles, hn)


def _moe_kernel(be_ref, nu_ref, x_ref, wgu_ref, bgu_ref, wd_ref, bd_ref, y_ref, wgu_b, wd_b):
    i = pl.program_id(0)
    used = i < nu_ref[0]
    new_expert = jnp.logical_or(i == 0, be_ref[i] != be_ref[jnp.maximum(i - 1, 0)])

    @pl.when(jnp.logical_and(used, new_expert))
    def _():
        def cast(c, carry):
            rows = pl.ds(pl.multiple_of(c * CAST_ROWS, CAST_ROWS), CAST_ROWS)
            wgu_b[rows, :] = wgu_ref[0, rows, :].astype(BF16)
            wd_b[rows, :] = wd_ref[0, rows, :].astype(BF16)
            return carry
        lax.fori_loop(0, D_MODEL // CAST_ROWS, cast, 0)

    @pl.when(used)
    def _():
        x = _load_row_tiles(x_ref, MOE_BLOCK).astype(BF16)
        hb = _dot(x, wgu_b[...]) + bgu_ref[0]
        glu = jnp.minimum(hb[:, :D_FF], SWIGLU_LIMIT)
        lin = jnp.clip(hb[:, D_FF:], -SWIGLU_LIMIT, SWIGLU_LIMIT)
        act = glu * jax.nn.sigmoid(SWIGLU_ALPHA * glu) * (lin + 1.0)
        _store_row_tiles(y_ref, MOE_BLOCK, _dot(act.astype(BF16), wd_b[...]) + bd_ref[0])

    @pl.when(i >= nu_ref[0])
    def _():
        y_ref[...] = jnp.zeros_like(y_ref)


def _moe_blocks(block_e, nused, xs, wgu, bgu, wd, bd):
    nblk = block_e.shape[0]
    grid_spec = pltpu.PrefetchScalarGridSpec(
        num_scalar_prefetch=2,
        grid=(nblk,),
        in_specs=[
            pl.BlockSpec((MOE_BLOCK * SUBLANES, LANES), lambda i, be, nu: (i, 0)),
            pl.BlockSpec((1, D_MODEL, 2 * D_FF), lambda i, be, nu: (be[i], 0, 0)),
            pl.BlockSpec((1, 1, 2 * D_FF), lambda i, be, nu: (be[i], 0, 0)),
            pl.BlockSpec((1, D_FF, D_MODEL), lambda i, be, nu: (be[i], 0, 0)),
            pl.BlockSpec((1, 1, D_MODEL), lambda i, be, nu: (be[i], 0, 0)),
        ],
        out_specs=pl.BlockSpec((MOE_BLOCK * SUBLANES, LANES), lambda i, be, nu: (i, 0)),
        scratch_shapes=[pltpu.VMEM((D_MODEL, 2 * D_FF), BF16), pltpu.VMEM((D_FF, D_MODEL), BF16)],
    )
    return pl.pallas_call(
        _moe_kernel,
        grid_spec=grid_spec,
        out_shape=jax.ShapeDtypeStruct(xs.shape, F32),
        compiler_params=pltpu.CompilerParams(dimension_semantics=("arbitrary",), vmem_limit_bytes=MOE_VMEM_LIMIT),
        name="moe_blocks",
    )(block_e, nused, xs, wgu, bgu, wd, bd)


def _combine_kernel(npt, destc_ref, destn_ref, h_ref, route_ref, ys_hbm, yp_ref, ysm_ref, gbuf, gsem):
    i = pl.program_id(0)
    nt = pl.num_programs(0)
    slot = i % 2

    def issue(dest_ref, s):
        def body(j, carry):
            for u in range(SUBLANES):
                t = j * SUBLANES + u
                for k in range(TOP_K):
                    d = dest_ref[0, 0, t * TOP_K + k]
                    pltpu.make_async_copy(_row_tile(ys_hbm, d), _row_tile(gbuf.at[s, k], t), gsem.at[s]).start(
                        priority=(u * TOP_K + k) % 2)
            return carry
        lax.fori_loop(0, CMB_TILE // SUBLANES, body, 0)

    @pl.when(i == 0)
    def _():
        issue(destc_ref, 0)

    @pl.when(i + 1 < nt)
    def _():
        issue(destn_ref, 1 - slot)

    for k in range(TOP_K):
        pltpu.make_async_copy(_row_tiles(ys_hbm, 0, CMB_TILE), gbuf.at[slot, k], gsem.at[slot]).wait()
    r = route_ref[...]
    moe = _load_row_tiles(gbuf.at[slot, 0], CMB_TILE) * r[:, 0:1]
    for k in range(1, TOP_K):
        moe = moe + _load_row_tiles(gbuf.at[slot, k], CMB_TILE) * r[:, k:k + 1]
    y = h_ref[...] + moe

    @pl.when(i < npt)
    def _():
        yp_ref[...] = y

    @pl.when(i >= npt)
    def _():
        ysm_ref[...] = y


def _combine(dest_tiles, h, route, ys, n_prompt_rows):
    T = h.shape[0]
    nt = T // CMB_TILE
    npt = n_prompt_rows // CMB_TILE
    smem_blk = lambda imap: pl.BlockSpec((1, 1, CMB_TILE * TOP_K), imap, memory_space=pltpu.SMEM)
    return pl.pallas_call(
        functools.partial(_combine_kernel, npt),
        grid=(nt,),
        in_specs=[smem_blk(lambda i: (i, 0, 0)), smem_blk(lambda i: (jnp.minimum(i + 1, nt - 1), 0, 0)),
                  pl.BlockSpec((CMB_TILE, D_MODEL), lambda i: (i, 0)),
                  pl.BlockSpec((CMB_TILE, LANES), lambda i: (i, 0)),
                  pl.BlockSpec(memory_space=pl.ANY)],
        out_specs=(pl.BlockSpec((CMB_TILE, D_MODEL), lambda i: (jnp.minimum(i, npt - 1), 0)),
                   pl.BlockSpec((CMB_TILE, D_MODEL), lambda i: (jnp.maximum(i - npt, 0), 0))),
        out_shape=(jax.ShapeDtypeStruct((n_prompt_rows, D_MODEL), F32),
                   jax.ShapeDtypeStruct((T - n_prompt_rows, D_MODEL), F32)),
        scratch_shapes=[pltpu.VMEM((2, TOP_K, CMB_TILE * SUBLANES, LANES), F32), pltpu.SemaphoreType.DMA((2,))],
        compiler_params=pltpu.CompilerParams(dimension_semantics=("arbitrary",), vmem_limit_bytes=VMEM_LIMIT),
        name="moe_combine",
    )(dest_tiles, dest_tiles, h, route, ys)


def kernel(x_prompt, x_sample, cache_k_win, cache_v_win, state_conv, state_C, state_n, state_m, g_attn, w_in, b_i,
           b_f, q_norm_g, k_norm_g, sinks, conv_w, conv_b, m_norm_g, w_out, g_ffn, w_router, b_router, w_gate_up,
           b_gate_up, w_down, b_down):
    depth = g_attn.shape[0]
    assert depth == 1
    B, S, _ = x_prompt.shape
    DB, DS, _ = x_sample.shape
    TP = B * S
    TS = DB * DS
    T = TP + TS
    assert T % TOK_TILE == 0 and TP % TOK_TILE == 0 and S % ATT_QB == 0 and S % PROMPT_CHUNK == 0
    assert DS == SUBLANES and DB % SAMPLE_NB == 0 and (SAMPLE_NB * DS) == LANES
    l = 0

    xp = x_prompt.reshape(TP, D_MODEL)
    xs = x_sample.reshape(TS, D_MODEL)

    w_pad = jnp.pad(w_in[l], ((0, 0), (0, IN_PAD - w_in.shape[2]))).astype(BF16)
    wgt = jnp.transpose(w_in[l][:, GATE_COL:GATE_COL + 2 * M_HEADS]).astype(BF16)
    gi = jnp.arange(A_WIDTH) // HEAD_DIM
    gmat = (gi[:, None] == gi[None, :]).astype(BF16)
    qg = jnp.tile(q_norm_g[l], A_HEADS).reshape(1, A_WIDTH)
    kg = jnp.tile(k_norm_g[l], A_KV_HEADS).reshape(1, KV_WIDTH)
    gbias = jnp.concatenate([b_i[l], b_f[l]])
    gbrow = jnp.pad(gbias, (0, LANES - 2 * M_HEADS)).reshape(1, LANES)
    gbcol = gbias.reshape(2 * M_HEADS, 1)
    mng = m_norm_g[l].reshape(1, M_WIDTH)
    cw = conv_w[l]
    cb = conv_b[l].reshape(1, 2 * M_WIDTH)
    wr_pad = jnp.pad(w_router[l], ((0, 0), (0, LANES - N_EXPERTS))).astype(BF16)
    br_pad = jnp.concatenate([b_router[l], jnp.full((LANES - N_EXPERTS,), NEG, F32)]).reshape(1, LANES)

    proj_w = (g_attn[l].reshape(1, D_MODEL), w_pad, wgt, gmat, qg, kg)
    qn, kn, va, qkm, vm, om, gcol, gt = _inproj(xp, *proj_w)
    qn_s, kn_s, va_s, qkm_s, vm_s, om_s, gcol_s, gt_s = _inproj(xs, *proj_w)

    a_p = _attn_prompt(sinks[l], qn, kn, va, B, S)
    ck = cache_k_win[l].reshape(DB, WINDOW, KV_WIDTH)
    cv = cache_v_win[l].reshape(DB, WINDOW, KV_WIDTH)
    a_s, kwin_s, vwin_s = _attn_sample(sinks[l], qn_s, kn_s, va_s, ck, cv, DB, DS)

    zc = jnp.zeros((B, SUBLANES, 2 * M_WIDTH), F32)
    m_p, C_p, n_p, mm_p = _mlstm(
        qkm, vm, om, gcol, gt, zc,
        jnp.zeros((B, M_HEADS, M_HEAD_DIM, M_HEAD_DIM), F32), jnp.zeros((B, M_HEADS, M_HEAD_DIM), F32),
        jnp.full((B, 1, LANES), NEG, F32), cw, cb, gbrow, gbcol, mng,
        nseq=1, L=PROMPT_CHUNK, ngroups=B, nchunks=S // PROMPT_CHUNK)
    conv_s0 = jnp.pad(state_conv[l], ((0, 0), (SUBLANES - (CONV_W - 1), 0), (0, 0)))
    m0_s = jnp.pad(state_m[l], ((0, 0), (0, LANES - M_HEADS))).reshape(DB, 1, LANES)
    m_s, C_s, n_s, mm_s = _mlstm(
        qkm_s, vm_s, om_s, gcol_s, gt_s, conv_s0, state_C[l], state_n[l], m0_s, cw, cb, gbrow, gbcol, mng,
        nseq=SAMPLE_NB, L=DS, ngroups=DB // SAMPLE_NB, nchunks=1)

    h, hn, route, cnt = _outproj(a_p, a_s, m_p, m_s, xp, xs, w_out[l].astype(BF16), g_ffn[l].reshape(1, D_MODEL),
                                 wr_pad, br_pad)

    nblk = T * TOP_K // MOE_BLOCK + N_EXPERTS
    dest, blk, info = _route_tables(route, cnt, nblk)
    block_e = blk[:, 0]
    padrow = info[0, :N_EXPERTS]
    npad = info[1, :N_EXPERTS]
    nused = info[2, N_EXPERTS - 1:N_EXPERTS]
    dest4 = dest[:, :TOP_K]
    xrows = _dispatch(padrow, npad, nused, dest4.reshape(T // TOK_TILE, 1, TOK_TILE * TOP_K), hn, nblk * MOE_BLOCK)
    yrows = _moe_blocks(block_e, nused, xrows,
                        w_gate_up[l], b_gate_up[l].reshape(N_EXPERTS, 1, 2 * D_FF),
                        w_down[l], b_down[l].reshape(N_EXPERTS, 1, D_MODEL))
    y_p, y_s = _combine(dest4.reshape(T // CMB_TILE, 1, CMB_TILE * TOP_K), h, route, yrows, TP)

    y_p = y_p.reshape(B, S, D_MODEL)
    y_s = y_s.reshape(DB, DS, D_MODEL)
    def seq_tail(rows, n):
        return jnp.stack([rows[(b + 1) * S - n:(b + 1) * S] for b in range(B)])

    kwin_p = seq_tail(kn, WINDOW).reshape(B, WINDOW, A_KV_HEADS, HEAD_DIM)
    vwin_p = seq_tail(va, WINDOW).reshape(B, WINDOW, A_KV_HEADS, HEAD_DIM)
    qkm_s = qkm_s.reshape(DB, DS, 2 * M_WIDTH)
    return (y_p, y_s,
            kwin_p[None], vwin_p[None], seq_tail(qkm, CONV_W - 1)[None],
            C_p[None], n_p[None], mm_p[:, 0, :M_HEADS][None],
            kwin_s.reshape(DB, WINDOW, A_KV_HEADS, HEAD_DIM)[None],
            vwin_s.reshape(DB, WINDOW, A_KV_HEADS, HEAD_DIM)[None],
            qkm_s[:, -(CONV_W - 1):][None],
            C_s[None], n_s[None], mm_s[:, 0, :M_HEADS][None])
```

```python
import functools

import jax
import jax.numpy as jnp
from jax import lax
from jax.experimental import pallas as pl
from jax.experimental.pallas import tpu as pltpu

F32 = jnp.float32
BF16 = jnp.bfloat16

D_MODEL = 1024
HEAD_DIM = 64
A_HEADS = 8
A_KV_HEADS = 2
A_GROUP = A_HEADS // A_KV_HEADS
A_WIDTH = A_HEADS * HEAD_DIM
KV_WIDTH = A_KV_HEADS * HEAD_DIM
WINDOW = 128
M_HEADS = 4
M_HEAD_DIM = 128
M_WIDTH = M_HEADS * M_HEAD_DIM
CONV_W = 4
N_EXPERTS = 32
TOP_K = 4
D_FF = D_MODEL
SWIGLU_LIMIT = 7.0
SWIGLU_ALPHA = 1.702
MOE_BLOCK = 512
EPS = 1e-6
NEG = -1e30

LANES = 128
SUBLANES = 8
GATE_COL = A_WIDTH + 2 * KV_WIDTH + 4 * M_WIDTH
IN_PAD = GATE_COL + LANES
TOK_TILE = 512
PROJ_TILE = 1024
PROJ_VMEM_LIMIT = 58 * 1024 * 1024
ATT_QB = 512
ATT_SB = 128
SAMPLE_NB = 16
PROMPT_CHUNK = 256
CMB_TILE = 256
VMEM_LIMIT = 48 * 1024 * 1024
MOE_VMEM_LIMIT = 56 * 1024 * 1024


def _dot(a, b):
    return jnp.dot(a, b, preferred_element_type=F32)


def _dot_nt(a, b):
    return lax.dot_general(a, b, (((1,), (1,)), ((), ())), preferred_element_type=F32)


def _dot_tn(a, b):
    return lax.dot_general(a, b, (((0,), (0,)), ((), ())), preferred_element_type=F32)


def _split3(x):
    hi = x.astype(BF16)
    r1 = x - hi.astype(F32)
    mid = r1.astype(BF16)
    lo = (r1 - mid.astype(F32)).astype(BF16)
    return hi, mid, lo


def _log_sigmoid(x):
    return jnp.minimum(x, 0.0) - jnp.log1p(jnp.exp(-jnp.abs(x)))


def _load_row_tiles(ref2, rows):
    return jnp.concatenate([ref2[pl.ds(s, rows, stride=SUBLANES), :] for s in range(SUBLANES)], axis=1)


def _store_row_tiles(ref2, rows, val):
    for s in range(SUBLANES):
        ref2[pl.ds(s, rows, stride=SUBLANES), :] = val[:, s * LANES:(s + 1) * LANES]


def _row_tile(ref2, idx):
    return ref2.at[pl.ds(pl.multiple_of(idx * SUBLANES, SUBLANES), SUBLANES), :]


def _row_tiles(ref2, first, n):
    return ref2.at[pl.ds(pl.multiple_of(first * SUBLANES, SUBLANES), n * SUBLANES), :]


def _split_specs(n_prompt_tiles, rows, width):
    return (pl.BlockSpec((rows, width), lambda i, *_: (jnp.minimum(i, n_prompt_tiles - 1), 0)),
            pl.BlockSpec((rows, width), lambda i, *_: (jnp.maximum(i - n_prompt_tiles, 0), 0)))


def _inproj_kernel(x_ref, g_ref, w_ref, wgt_ref, gmat_ref, qg_ref, kg_ref,
                   qn_ref, kn_ref, va_ref, qkm_ref, vm_ref, om_ref, gcol_ref, gt_ref):
    x = x_ref[...]
    ms = jnp.mean(x * x, axis=-1, keepdims=True)
    xn = ((x * lax.rsqrt(ms + EPS)) * g_ref[...]).astype(BF16)

    def seg(lo, hi):
        return _dot(xn, w_ref[:, lo:hi])

    def head_norm(z, gmat, g):
        hi, mid, lo = _split3(z * z)
        ss = _dot(hi, gmat) + _dot(mid, gmat) + _dot(lo, gmat)
        return (z * lax.rsqrt(ss * (1.0 / HEAD_DIM) + EPS)) * g

    o0 = A_WIDTH
    o1 = o0 + KV_WIDTH
    o2 = o1 + KV_WIDTH
    o3 = o2 + 2 * M_WIDTH
    o4 = o3 + M_WIDTH
    o5 = o4 + M_WIDTH
    qn_ref[...] = head_norm(seg(0, o0), gmat_ref[...], qg_ref[...])
    kn_ref[...] = head_norm(seg(o0, o1), gmat_ref[:KV_WIDTH, :KV_WIDTH], kg_ref[...])
    va_ref[...] = seg(o1, o2)
    qkm_ref[...] = seg(o2, o3)
    vm_ref[...] = seg(o3, o4)
    om_ref[...] = seg(o4, o5)
    gcol_ref[...] = seg(o5, o5 + LANES)
    gt_ref[0] = _dot_nt(wgt_ref[...], xn)


def _inproj(x, g_attn, w_pad, wgt, gmat, qg, kg):
    T = x.shape[0]
    nt = T // PROJ_TILE
    row = lambda w: pl.BlockSpec((PROJ_TILE, w), lambda i: (i, 0))
    full = lambda a: pl.BlockSpec(a.shape, lambda i: (0,) * a.ndim, pipeline_mode=pl.Buffered(1))
    out_shape = (
        jax.ShapeDtypeStruct((T, A_WIDTH), F32),
        jax.ShapeDtypeStruct((T, KV_WIDTH), F32),
        jax.ShapeDtypeStruct((T, KV_WIDTH), F32),
        jax.ShapeDtypeStruct((T, 2 * M_WIDTH), F32),
        jax.ShapeDtypeStruct((T, M_WIDTH), F32),
        jax.ShapeDtypeStruct((T, M_WIDTH), F32),
        jax.ShapeDtypeStruct((T, LANES), F32),
        jax.ShapeDtypeStruct((nt, SUBLANES, PROJ_TILE), F32),
    )
    out_specs = (row(A_WIDTH), row(KV_WIDTH), row(KV_WIDTH), row(2 * M_WIDTH), row(M_WIDTH), row(M_WIDTH),
                 row(LANES), pl.BlockSpec((1, SUBLANES, PROJ_TILE), lambda i: (i, 0, 0)))
    return pl.pallas_call(
        _inproj_kernel,
        grid=(nt,),
        in_specs=[row(D_MODEL), full(g_attn), full(w_pad), full(wgt), full(gmat), full(qg), full(kg)],
        out_specs=out_specs,
        out_shape=out_shape,
        compiler_params=pltpu.CompilerParams(dimension_semantics=("parallel",), vmem_limit_bytes=PROJ_VMEM_LIMIT),
        name="inproj",
    )(x, g_attn, w_pad, wgt, gmat, qg, kg)


def _softmax_sink(pieces, masks, sink_col):
    masked = [jnp.where(mk, s, NEG) for s, mk in zip(pieces, masks)]
    m = sink_col
    for s in masked:
        m = jnp.maximum(m, jnp.max(s, axis=-1, keepdims=True))
    ps = [jnp.exp(s - m) for s in masked]
    den = jnp.exp(sink_col - m)
    for p in ps:
        den = den + jnp.sum(p, axis=-1, keepdims=True)
    return ps, 1.0 / den


def _stack_heads(q, g):
    return jnp.concatenate([q[:, (A_GROUP * g + i) * HEAD_DIM:(A_GROUP * g + i + 1) * HEAD_DIM]
                            for i in range(A_GROUP)], axis=0)


def _sink_col(sink_ref, g, rows_per_head):
    r = lax.broadcasted_iota(jnp.int32, (A_GROUP * rows_per_head, 1), 0)
    col = jnp.zeros((A_GROUP * rows_per_head, 1), F32)
    for i in range(A_GROUP):
        col = jnp.where(r // rows_per_head == i, sink_ref[A_GROUP * g + i], col)
    return col


def _attn_prompt_kernel(sink_ref, q_ref, kp_ref, kc_ref, vp_ref, vc_ref, o_ref):
    j = pl.program_id(1)
    scale = HEAD_DIM ** -0.5
    kall = jnp.concatenate([kp_ref[...], kc_ref[...]], axis=0).astype(BF16)
    vall = jnp.concatenate([vp_ref[...], vc_ref[...]], axis=0).astype(BF16)
    nrow = A_GROUP * ATT_SB
    r = lax.broadcasted_iota(jnp.int32, (nrow, 2 * ATT_SB), 0) % ATT_SB
    c = lax.broadcasted_iota(jnp.int32, (nrow, 2 * ATT_SB), 1)
    band = jnp.logical_and(c >= r, c <= r + WINDOW)
    band0 = jnp.logical_and(band, jnp.logical_or(c >= ATT_SB, j > 0))
    for sb in range(ATT_QB // ATT_SB):
        q = (q_ref[sb * ATT_SB:(sb + 1) * ATT_SB, :] * scale).astype(BF16)
        kwin = kall[sb * ATT_SB:(sb + 2) * ATT_SB]
        vwin = vall[sb * ATT_SB:(sb + 2) * ATT_SB]
        outs = []
        for g in range(A_KV_HEADS):
            lo, hi = g * HEAD_DIM, (g + 1) * HEAD_DIM
            s = _dot_nt(_stack_heads(q, g), kwin[:, lo:hi])
            (p,), inv = _softmax_sink([s], [band0 if sb == 0 else band], _sink_col(sink_ref, g, ATT_SB))
            o = _dot(p.astype(BF16), vwin[:, lo:hi]) * inv
            outs += [o[i * ATT_SB:(i + 1) * ATT_SB] for i in range(A_GROUP)]
        o_ref[sb * ATT_SB:(sb + 1) * ATT_SB, :] = jnp.concatenate(outs, axis=1)


def _attn_prompt(sinks, qn, kn, va, batch, seq):
    nq = seq // ATT_QB
    ratio = ATT_QB // ATT_SB
    cur = lambda w: pl.BlockSpec((ATT_QB, w), lambda b, j: (b * nq + j, 0))
    prev = lambda w: pl.BlockSpec((ATT_SB, w), lambda b, j: (jnp.maximum((b * nq + j) * ratio - 1, 0), 0))
    return pl.pallas_call(
        _attn_prompt_kernel,
        grid=(batch, nq),
        in_specs=[pl.BlockSpec(memory_space=pltpu.SMEM), cur(A_WIDTH), prev(KV_WIDTH), cur(KV_WIDTH),
                  prev(KV_WIDTH), cur(KV_WIDTH)],
        out_specs=cur(A_WIDTH),
        out_shape=jax.ShapeDtypeStruct((batch * seq, A_WIDTH), F32),
        compiler_params=pltpu.CompilerParams(dimension_semantics=("parallel", "parallel"),
                                             vmem_limit_bytes=VMEM_LIMIT),
        name="attn_prompt",
    )(sinks, qn, kn, kn, va, va)


def _attn_sample_kernel(dec, sink_ref, q_ref, kn_ref, vn_ref, ck_ref, cv_ref, o_ref, kw_ref, vw_ref):
    scale = HEAD_DIM ** -0.5
    rows = SAMPLE_NB * dec
    knew = kn_ref[...]
    vnew = vn_ref[...]
    knew_b = knew.astype(BF16)
    vnew_b = vnew.astype(BF16)
    nrow = A_GROUP * dec
    t = lax.broadcasted_iota(jnp.int32, (nrow, WINDOW), 0) % dec
    c = lax.broadcasted_iota(jnp.int32, (nrow, WINDOW), 1)
    m_cache = c >= t
    cn = lax.broadcasted_iota(jnp.int32, (nrow, rows), 1)
    tn = lax.broadcasted_iota(jnp.int32, (nrow, rows), 0) % dec
    for i in range(SAMPLE_NB):
        q = (q_ref[i * dec:(i + 1) * dec, :] * scale).astype(BF16)
        ck = ck_ref[i].astype(BF16)
        cv = cv_ref[i].astype(BF16)
        m_new = jnp.logical_and(cn // dec == i, cn % dec <= tn)
        outs = []
        for g in range(A_KV_HEADS):
            lo, hi = g * HEAD_DIM, (g + 1) * HEAD_DIM
            qs = _stack_heads(q, g)
            s_c = _dot_nt(qs, ck[:, lo:hi])
            s_n = _dot_nt(qs, knew_b[:, lo:hi])
            (p_c, p_n), inv = _softmax_sink([s_c, s_n], [m_cache, m_new], _sink_col(sink_ref, g, dec))
            o = (_dot(p_c.astype(BF16), cv[:, lo:hi]) + _dot(p_n.astype(BF16), vnew_b[:, lo:hi])) * inv
            outs += [o[h * dec:(h + 1) * dec] for h in range(A_GROUP)]
        o_ref[i * dec:(i + 1) * dec, :] = jnp.concatenate(outs, axis=1)
        kw_ref[i, 0:WINDOW - dec, :] = ck_ref[i, dec:WINDOW, :]
        kw_ref[i, WINDOW - dec:WINDOW, :] = knew[i * dec:(i + 1) * dec]
        vw_ref[i, 0:WINDOW - dec, :] = cv_ref[i, dec:WINDOW, :]
        vw_ref[i, WINDOW - dec:WINDOW, :] = vnew[i * dec:(i + 1) * dec]


def _attn_sample(sinks, qn, kn, va, ck, cv, dbatch, dec):
    rows = SAMPLE_NB * dec
    tokrow = lambda w: pl.BlockSpec((rows, w), lambda i: (i, 0))
    cache = pl.BlockSpec((SAMPLE_NB, WINDOW, KV_WIDTH), lambda i: (i, 0, 0))
    return pl.pallas_call(
        functools.partial(_attn_sample_kernel, dec),
        grid=(dbatch // SAMPLE_NB,),
        in_specs=[pl.BlockSpec(memory_space=pltpu.SMEM), tokrow(A_WIDTH), tokrow(KV_WIDTH), tokrow(KV_WIDTH),
                  cache, cache],
        out_specs=(pl.BlockSpec((rows, A_WIDTH), lambda i: (i, 0)), cache, cache),
        out_shape=(jax.ShapeDtypeStruct((dbatch * dec, A_WIDTH), F32),
                   jax.ShapeDtypeStruct((dbatch, WINDOW, KV_WIDTH), F32),
                   jax.ShapeDtypeStruct((dbatch, WINDOW, KV_WIDTH), F32)),
        compiler_params=pltpu.CompilerParams(dimension_semantics=("parallel",), vmem_limit_bytes=VMEM_LIMIT),
        name="attn_sample",
    )(sinks, qn, kn, va, ck, cv)


def _mlstm_kernel(nseq, L, qk_ref, v_ref, o_ref, gcol_ref, gt_ref, conv0_ref, c0_ref, n0_ref, m0_ref,
                  cw_ref, cb_ref, gbrow_ref, gbcol_ref, mng_ref, mask_ref,
                  out_ref, cst_ref, nst_ref, mst_ref, prev_ref):
    R = nseq * L
    ci = pl.program_id(1)

    @pl.when(ci == 0)
    def _():
        cst_ref[...] = c0_ref[...]
        nst_ref[...] = n0_ref[...]
        mst_ref[...] = m0_ref[...]
        prev_ref[...] = conv0_ref[0]

    raw = qk_ref[...]
    row = lax.broadcasted_iota(jnp.int32, (R, 1), 0)
    tpos = row % L
    rseq = row // L
    acc = raw * cw_ref[CONV_W - 1:CONV_W, :] + cb_ref[...]
    if nseq == 1:
        prev8 = prev_ref[...]
        t8 = lax.broadcasted_iota(jnp.int32, (SUBLANES, 1), 0)
    else:
        prevsrc = conv0_ref[...].reshape(R, 2 * M_WIDTH)
    for k in range(1, CONV_W):
        rolled = pltpu.roll(raw, k, 0)
        if nseq == 1:
            head = jnp.where(t8 >= k, rolled[0:SUBLANES], pltpu.roll(prev8, k, 0))
            sh = jnp.concatenate([head, rolled[SUBLANES:]], axis=0)
        else:
            sh = jnp.where(tpos >= k, rolled, pltpu.roll(prevsrc, R - SUBLANES + k, 0))
        acc = acc + sh * cw_ref[CONV_W - 1 - k:CONV_W - k, :]
    if nseq == 1:
        prev_ref[...] = raw[R - SUBLANES:R]
    qkc = acc * jax.nn.sigmoid(acc)

    gc = gcol_ref[...] + gbrow_ref[...]
    gr = gt_ref[0] + gbcol_ref[...]
    lsc = _log_sigmoid(gc)
    lsr = _log_sigmoid(gr)
    mb = mask_ref[...]
    maskb = mb > 0
    bcol = sum(_dot(mb, p) for p in _split3(lsc))
    brow = sum(_dot_nt(p, mb) for p in _split3(lsr))

    lane = lax.broadcasted_iota(jnp.int32, (1, LANES), 1)
    m_new = [jnp.zeros((1, LANES), F32) for _ in range(nseq)]
    for h in range(M_HEADS):
        sl = slice(h * M_HEAD_DIM, (h + 1) * M_HEAD_DIM)
        qh = qkc[:, sl]
        kh = qkc[:, M_WIDTH + h * M_HEAD_DIM:M_WIDTH + (h + 1) * M_HEAD_DIM] * (M_HEAD_DIM ** -0.5)
        vh = v_ref[:, sl]
        qb, kb, vb = qh.astype(BF16), kh.astype(BF16), vh.astype(BF16)
        ig_c = gc[:, h:h + 1]
        b_c = bcol[:, M_HEADS + h:M_HEADS + h + 1]
        ig_r = gr[h:h + 1, :]
        b_r = brow[M_HEADS + h:M_HEADS + h + 1, :]
        if nseq == 1:
            m0c = mst_ref[0][:, h:h + 1]
            n0rows = nst_ref[0, h:h + 1, :]
        else:
            m0c = jnp.zeros((R, 1), F32)
            n0rows = jnp.zeros((R, M_HEAD_DIM), F32)
            for s in range(nseq):
                m0c = jnp.where(rseq == s, mst_ref[s][:, h:h + 1], m0c)
                n0rows = jnp.where(rseq == s, nst_ref[s, h:h + 1, :], n0rows)
        dm = jnp.where(maskb, b_c - b_r + ig_r, NEG)
        a_c = b_c + m0c
        m_c = jnp.maximum(a_c, jnp.max(dm, axis=-1, keepdims=True))
        w = jnp.exp(dm - m_c)
        sc = jnp.exp(a_c - m_c)
        wqk = w * _dot_nt(qb, kb)
        if nseq == 1:
            inter = _dot_nt(qb, cst_ref[0, h].astype(BF16))
        else:
            inter = jnp.zeros((R, M_HEAD_DIM), F32)
            for s in range(nseq):
                qs = jnp.where(rseq == s, qh, 0.0).astype(BF16)
                inter = inter + _dot_nt(qs, cst_ref[s, h].astype(BF16))
        num = _dot(wqk.astype(BF16), vb) + sc * inter
        den = jnp.sum(wqk, axis=-1, keepdims=True) + sc * jnp.sum(qh * n0rows, axis=-1, keepdims=True)
        hh = num / jnp.maximum(jnp.abs(den), jnp.exp(-m_c))

        for s in range(nseq):
            e = s * L + L - 1
            m_end = m_c[e:e + 1, :]
            wend = jnp.exp(b_c[e:e + 1, :] - b_c + ig_c - m_end)
            if nseq > 1:
                wend = jnp.where(rseq == s, wend, 0.0)
            sce = jnp.exp(a_c[e:e + 1, :] - m_end)
            c_new = sce * cst_ref[s, h] + _dot_tn((vh * wend).astype(BF16), kb)
            n_new = sce * nst_ref[s, h:h + 1, :] + jnp.sum(wend * kh, axis=0, keepdims=True)
            cst_ref[s, h] = c_new
            nst_ref[s, h:h + 1, :] = n_new
            m_new[s] = jnp.where(lane == h, m_end, m_new[s])

        hn = (hh * lax.rsqrt(jnp.mean(hh * hh, axis=-1, keepdims=True) + EPS)) * mng_ref[:, sl]
        out_ref[:, sl] = jax.nn.sigmoid(o_ref[:, sl]) * hn
    for s in range(nseq):
        mst_ref[s] = m_new[s]


def _mlstm(qkm, vm, om, gcol, gt, conv0, c0, n0, m0, cw, cb, gbrow, gbcol, mng, nseq, L, ngroups, nchunks):
    R = nseq * L
    per_tile = gt.shape[2] // R
    tok = lambda w: pl.BlockSpec((R, w), lambda g, c: (g * nchunks + c, 0))
    gt_spec = pl.BlockSpec((1, SUBLANES, R),
                           lambda g, c: ((g * nchunks + c) // per_tile, 0, (g * nchunks + c) % per_tile))
    full = lambda a: pl.BlockSpec(a.shape, lambda g, c: (0,) * a.ndim)
    st4 = pl.BlockSpec((nseq, M_HEADS, M_HEAD_DIM, M_HEAD_DIM), lambda g, c: (g, 0, 0, 0))
    st3 = pl.BlockSpec((nseq, M_HEADS, M_HEAD_DIM), lambda g, c: (g, 0, 0))
    stm = pl.BlockSpec((nseq, 1, LANES), lambda g, c: (g, 0, 0))
    conv_spec = pl.BlockSpec((nseq, SUBLANES, 2 * M_WIDTH), lambda g, c: (g, 0, 0))
    r = jnp.arange(R)
    mask = ((r[:, None] // L == r[None, :] // L) & (r[None, :] <= r[:, None])).astype(BF16)
    nstate = ngroups * nseq
    return pl.pallas_call(
        functools.partial(_mlstm_kernel, nseq, L),
        grid=(ngroups, nchunks),
        in_specs=[tok(2 * M_WIDTH), tok(M_WIDTH), tok(M_WIDTH), tok(LANES), gt_spec, conv_spec, st4, st3, stm,
                  full(cw), full(cb), full(gbrow), full(gbcol), full(mng), full(mask)],
        out_specs=(pl.BlockSpec((R, M_WIDTH), lambda g, c: (g * nchunks + c, 0)), st4, st3, stm),
        out_shape=(jax.ShapeDtypeStruct((ngroups * nchunks * R, M_WIDTH), F32),
                   jax.ShapeDtypeStruct((nstate, M_HEADS, M_HEAD_DIM, M_HEAD_DIM), F32),
                   jax.ShapeDtypeStruct((nstate, M_HEADS, M_HEAD_DIM), F32),
                   jax.ShapeDtypeStruct((nstate, 1, LANES), F32)),
        scratch_shapes=[pltpu.VMEM((SUBLANES, 2 * M_WIDTH), F32)],
        compiler_params=pltpu.CompilerParams(dimension_semantics=("parallel", "arbitrary"),
                                             vmem_limit_bytes=VMEM_LIMIT),
        name="mlstm_n%d" % nseq,
    )(qkm, vm, om, gcol, gt, conv0, c0, n0, m0, cw, cb, gbrow, gbcol, mng, mask)


def _outproj_kernel(npt, ap_ref, as_ref, mp_ref, ms_ref, xp_ref, xs_ref, wo_ref, g_ref, wr_ref, br_ref,
                    h_ref, hn_ref, route_ref, cnt_ref):
    def project(a_ref, m_ref, x_ref):
        h_ref[...] = (x_ref[...] + _dot(a_ref[...].astype(BF16), wo_ref[0:A_WIDTH, :])
                      + _dot(m_ref[...].astype(BF16), wo_ref[A_WIDTH:A_WIDTH + M_WIDTH, :]))

    @pl.when(pl.program_id(0) < npt)
    def _():
        project(ap_ref, mp_ref, xp_ref)

    @pl.when(pl.program_id(0) >= npt)
    def _():
        project(as_ref, ms_ref, xs_ref)

    h = h_ref[...]
    hn = (h * lax.rsqrt(jnp.mean(h * h, axis=-1, keepdims=True) + EPS)) * g_ref[...]
    _store_row_tiles(hn_ref, TOK_TILE, hn)
    logits = _dot(hn.astype(BF16), wr_ref[...]) + br_ref[...]
    lane = lax.broadcasted_iota(jnp.int32, logits.shape, 1).astype(F32)
    route = jnp.zeros(logits.shape, F32)
    picked = jnp.zeros(logits.shape, F32)
    top0 = None
    den = None
    es = []
    for k in range(TOP_K):
        mx = jnp.max(logits, axis=-1, keepdims=True)
        idx = jnp.min(jnp.where(logits == mx, lane, float(LANES)), axis=-1, keepdims=True)
        if k == 0:
            top0 = mx
        e = jnp.exp(mx - top0)
        den = e if den is None else den + e
        es.append(e)
        route = jnp.where(lane == float(TOP_K + k), idx, route)
        picked = jnp.where(lane == idx, 1.0, picked)
        logits = jnp.where(lane == idx, -jnp.inf, logits)
    for k in range(TOP_K):
        route = jnp.where(lane == float(k), es[k] / den, route)
    route_ref[...] = route

    @pl.when(pl.program_id(0) == 0)
    def _():
        cnt_ref[...] = jnp.zeros_like(cnt_ref)

    cnt_ref[0:1, :] += jnp.sum(picked, axis=0, keepdims=True)


def _outproj(a_p, a_s, m_p, m_s, xp, xs, w_out, g_ffn, wr_pad, br_pad):
    T = xp.shape[0] + xs.shape[0]
    npt = xp.shape[0] // TOK_TILE
    row = lambda w: pl.BlockSpec((TOK_TILE, w), lambda i: (i, 0))
    full = lambda a: pl.BlockSpec(a.shape, lambda i: (0,) * a.ndim)
    return pl.pallas_call(
        functools.partial(_outproj_kernel, npt),
        grid=(T // TOK_TILE,),
        in_specs=[*_split_specs(npt, TOK_TILE, A_WIDTH), *_split_specs(npt, TOK_TILE, M_WIDTH),
                  *_split_specs(npt, TOK_TILE, D_MODEL), full(w_out), full(g_ffn), full(wr_pad), full(br_pad)],
        out_specs=(row(D_MODEL), pl.BlockSpec((TOK_TILE * SUBLANES, LANES), lambda i: (i, 0)), row(LANES),
                   pl.BlockSpec((SUBLANES, LANES), lambda i: (0, 0))),
        out_shape=(jax.ShapeDtypeStruct((T, D_MODEL), F32), jax.ShapeDtypeStruct((T * SUBLANES, LANES), F32),
                   jax.ShapeDtypeStruct((T, LANES), F32), jax.ShapeDtypeStruct((SUBLANES, LANES), F32)),
        compiler_params=pltpu.CompilerParams(dimension_semantics=("arbitrary",), vmem_limit_bytes=VMEM_LIMIT),
        name="outproj_router",
    )(a_p, a_s, m_p, m_s, xp, xs, w_out, g_ffn, wr_pad, br_pad)


def _route_kernel(nblk, route_ref, cnt_ref, lstrict_ref, ustrict_ref, dest_ref, blk_ref, info_ref, carry_ref):
    i = pl.program_id(0)
    lane = lax.broadcasted_iota(jnp.int32, (1, LANES), 1)
    cnt = cnt_ref[0:1, :]
    nb_e = jnp.floor((cnt + (MOE_BLOCK - 1.0)) * (1.0 / MOE_BLOCK))
    u = ustrict_ref[...]
    bstart = sum(_dot(p, u) for p in _split3(jnp.broadcast_to(nb_e, (SUBLANES, LANES))))[0:1, :]
    bend = bstart + nb_e
    row_start = bstart * float(MOE_BLOCK)

    @pl.when(i == 0)
    def _():
        carry_ref[...] = jnp.zeros_like(carry_ref)
        bi = lax.broadcasted_iota(jnp.int32, (nblk, LANES), 0).astype(F32)
        done = jnp.logical_and(bend <= bi, lane < N_EXPERTS)
        be = jnp.minimum(jnp.sum(jnp.where(done, 1.0, 0.0), axis=-1, keepdims=True), N_EXPERTS - 1.0)
        blk_ref[...] = jnp.broadcast_to(be, (nblk, LANES)).astype(jnp.int32)
        info = jnp.zeros((SUBLANES, LANES), F32)
        srow = lax.broadcasted_iota(jnp.int32, (SUBLANES, LANES), 0)
        info = jnp.where(srow == 0, row_start + cnt, info)
        info = jnp.where(srow == 1, nb_e * float(MOE_BLOCK) - cnt, info)
        info = jnp.where(srow == 2, bend, info)
        info_ref[...] = info.astype(jnp.int32)

    r = route_ref[...]
    lane_f = lane.astype(F32)
    sel = [lane_f == r[:, TOP_K + k:TOP_K + k + 1] for k in range(TOP_K)]
    oh = jnp.zeros(r.shape, F32)
    for k in range(TOP_K):
        oh = jnp.where(sel[k], 1.0, oh)
    before = _dot(lstrict_ref[...], oh.astype(BF16)) + carry_ref[...] + row_start
    dest = jnp.zeros(r.shape, jnp.int32)
    for k in range(TOP_K):
        d = jnp.sum(jnp.where(sel[k], before, 0.0), axis=-1, keepdims=True)
        dest = jnp.where(lane == k, d.astype(jnp.int32), dest)
    dest_ref[...] = dest
    carry_ref[...] += jnp.sum(oh, axis=0, keepdims=True)


def _route_tables(route, cnt, nblk):
    T = route.shape[0]
    a = jnp.arange(TOK_TILE)
    lstrict = (a[:, None] > a[None, :]).astype(BF16)
    b = jnp.arange(LANES)
    ustrict = (b[:, None] < b[None, :]).astype(BF16)
    full = lambda x: pl.BlockSpec(x.shape, lambda i: (0,) * x.ndim)
    return pl.pallas_call(
        functools.partial(_route_kernel, nblk),
        grid=(T // TOK_TILE,),
        in_specs=[pl.BlockSpec((TOK_TILE, LANES), lambda i: (i, 0)), full(cnt), full(lstrict), full(ustrict)],
        out_specs=(pl.BlockSpec((TOK_TILE, LANES), lambda i: (i, 0)),
                   pl.BlockSpec((nblk, LANES), lambda i: (0, 0)),
                   pl.BlockSpec((SUBLANES, LANES), lambda i: (0, 0))),
        out_shape=(jax.ShapeDtypeStruct((T, LANES), jnp.int32), jax.ShapeDtypeStruct((nblk, LANES), jnp.int32),
                   jax.ShapeDtypeStruct((SUBLANES, LANES), jnp.int32)),
        scratch_shapes=[pltpu.VMEM((1, LANES), F32)],
        compiler_params=pltpu.CompilerParams(dimension_semantics=("arbitrary",), vmem_limit_bytes=VMEM_LIMIT),
        name="route_tables",
    )(route, cnt, lstrict, ustrict)


def _dispatch_kernel(padrow_ref, npad_ref, nu_ref, dest_ref, hn_ref, xs_hbm, zbuf, sem, zsem):
    i = pl.program_id(0)

    @pl.when(i == 0)
    def _():
        zbuf[...] = jnp.zeros_like(zbuf)
        nblk = xs_hbm.shape[0] // (MOE_BLOCK * SUBLANES)

        def tail_start(b, c):
            pltpu.make_async_copy(zbuf, _row_tiles(xs_hbm, b * MOE_BLOCK, MOE_BLOCK), zsem).start()
            return c

        def tail_wait(b, c):
            pltpu.make_async_copy(zbuf, _row_tiles(xs_hbm, b * MOE_BLOCK, MOE_BLOCK), zsem).wait()
            return c

        lax.fori_loop(nu_ref[0], nblk, tail_start, 0)
        lax.fori_loop(nu_ref[0], nblk, tail_wait, 0)

        def per_expert(e, carry):
            base = padrow_ref[e]
            n = npad_ref[e]

            def start(r, c):
                pltpu.make_async_copy(_row_tile(zbuf, 0), _row_tile(xs_hbm, base + r), zsem).start()
                return c

            def wait(r, c):
                pltpu.make_async_copy(_row_tile(zbuf, 0), _row_tile(xs_hbm, base + r), zsem).wait()
                return c

            lax.fori_loop(0, n, start, 0)
            lax.fori_loop(0, n, wait, 0)
            return carry

        lax.fori_loop(0, N_EXPERTS, per_expert, 0)

    def body(j, carry):
        for u in range(SUBLANES):
            t = j * SUBLANES + u
            for k in range(TOP_K):
                d = dest_ref[0, 0, t * TOP_K + k]
                pltpu.make_async_copy(_row_tile(hn_ref, t), _row_tile(xs_hbm, d), sem).start(
                    priority=(u * TOP_K + k) % 2)
        return carry

    lax.fori_loop(0, TOK_TILE // SUBLANES, body, 0)
    for _ in range(TOP_K):
        pltpu.make_async_copy(hn_ref, _row_tiles(xs_hbm, 0, TOK_TILE), sem).wait()


def _dispatch(padrow, npad, nused, dest_tiles, hn, n_rows):
    T = hn.shape[0] // SUBLANES
    grid_spec = pltpu.PrefetchScalarGridSpec(
        num_scalar_prefetch=3,
        grid=(T // TOK_TILE,),
        in_specs=[pl.BlockSpec((1, 1, TOK_TILE * TOP_K), lambda i, *_: (i, 0, 0), memory_space=pltpu.SMEM),
                  pl.BlockSpec((TOK_TILE * SUBLANES, LANES), lambda i, *_: (i, 0))],
        out_specs=pl.BlockSpec(memory_space=pl.ANY),
        scratch_shapes=[pltpu.VMEM((MOE_BLOCK * SUBLANES, LANES), F32), pltpu.SemaphoreType.DMA(()),
                        pltpu.SemaphoreType.DMA(())],
    )
    return pl.pallas_call(
        _dispatch_kernel,
        grid_spec=grid_spec,
        out_shape=jax.ShapeDtypeStruct((n_rows * SUBLANES, LANES), F32),
        compiler_params=pltpu.CompilerParams(dimension_semantics=("arbitrary",), vmem_limit_bytes=VMEM_LIMIT),
        name="moe_dispatch",
    )(padrow, npad, nused, dest_tiles, hn)


def _moe_kernel(be_ref, nu_ref, x_ref, wgu_ref, bgu_ref, wd_ref, bd_ref, y_ref):
    del be_ref
    i = pl.program_id(0)

    @pl.when(i < nu_ref[0])
    def _():
        x = _load_row_tiles(x_ref, MOE_BLOCK).astype(BF16)
        hb = _dot(x, wgu_ref[0].astype(BF16)) + bgu_ref[0]
        glu = jnp.minimum(hb[:, :D_FF], SWIGLU_LIMIT)
        lin = jnp.clip(hb[:, D_FF:], -SWIGLU_LIMIT, SWIGLU_LIMIT)
        act = glu * jax.nn.sigmoid(SWIGLU_ALPHA * glu) * (lin + 1.0)
        _store_row_tiles(y_ref, MOE_BLOCK, _dot(act.astype(BF16), wd_ref[0].astype(BF16)) + bd_ref[0])

    @pl.when(i >= nu_ref[0])
    def _():
        y_ref[...] = jnp.zeros_like(y_ref)


def _moe_blocks(block_e, nused, xs, wgu, bgu, wd, bd):
    nblk = block_e.shape[0]
    grid_spec = pltpu.PrefetchScalarGridSpec(
        num_scalar_prefetch=2,
        grid=(nblk,),
        in_specs=[
            pl.BlockSpec((MOE_BLOCK * SUBLANES, LANES), lambda i, be, nu: (i, 0)),
            pl.BlockSpec((1, D_MODEL, 2 * D_FF), lambda i, be, nu: (be[i], 0, 0)),
            pl.BlockSpec((1, 1, 2 * D_FF), lambda i, be, nu: (be[i], 0, 0)),
            pl.BlockSpec((1, D_FF, D_MODEL), lambda i, be, nu: (be[i], 0, 0)),
            pl.BlockSpec((1, 1, D_MODEL), lambda i, be, nu: (be[i], 0, 0)),
        ],
        out_specs=pl.BlockSpec((MOE_BLOCK * SUBLANES, LANES), lambda i, be, nu: (i, 0)),
    )
    return pl.pallas_call(
        _moe_kernel,
        grid_spec=grid_spec,
        out_shape=jax.ShapeDtypeStruct(xs.shape, F32),
        compiler_params=pltpu.CompilerParams(dimension_semantics=("arbitrary",), vmem_limit_bytes=MOE_VMEM_LIMIT),
        name="moe_blocks",
    )(block_e, nused, xs, wgu, bgu, wd, bd)


def _combine_kernel(npt, destc_ref, destn_ref, h_ref, route_ref, ys_hbm, yp_ref, ysm_ref, gbuf, gsem):
    i = pl.program_id(0)
    nt = pl.num_programs(0)
    slot = i % 2

    def issue(dest_ref, s):
        def body(j, carry):
            for u in range(SUBLANES):
                t = j * SUBLANES + u
                for k in range(TOP_K):
                    d = dest_ref[0, 0, t * TOP_K + k]
                    pltpu.make_async_copy(_row_tile(ys_hbm, d), _row_tile(gbuf.at[s, k], t), gsem.at[s]).start(
                        priority=(u * TOP_K + k) % 2)
            return carry
        lax.fori_loop(0, CMB_TILE // SUBLANES, body, 0)

    @pl.when(i == 0)
    def _():
        issue(destc_ref, 0)

    @pl.when(i + 1 < nt)
    def _():
        issue(destn_ref, 1 - slot)

    for k in range(TOP_K):
        pltpu.make_async_copy(_row_tiles(ys_hbm, 0, CMB_TILE), gbuf.at[slot, k], gsem.at[slot]).wait()
    r = route_ref[...]
    moe = _load_row_tiles(gbuf.at[slot, 0], CMB_TILE) * r[:, 0:1]
    for k in range(1, TOP_K):
        moe = moe + _load_row_tiles(gbuf.at[slot, k], CMB_TILE) * r[:, k:k + 1]
    y = h_ref[...] + moe

    @pl.when(i < npt)
    def _():
        yp_ref[...] = y

    @pl.when(i >= npt)
    def _():
        ysm_ref[...] = y


def _combine(dest_tiles, h, route, ys, n_prompt_rows):
    T = h.shape[0]
    nt = T // CMB_TILE
    npt = n_prompt_rows // CMB_TILE
    smem_blk = lambda imap: pl.BlockSpec((1, 1, CMB_TILE * TOP_K), imap, memory_space=pltpu.SMEM)
    return pl.pallas_call(
        functools.partial(_combine_kernel, npt),
        grid=(nt,),
        in_specs=[smem_blk(lambda i: (i, 0, 0)), smem_blk(lambda i: (jnp.minimum(i + 1, nt - 1), 0, 0)),
                  pl.BlockSpec((CMB_TILE, D_MODEL), lambda i: (i, 0)),
                  pl.BlockSpec((CMB_TILE, LANES), lambda i: (i, 0)),
                  pl.BlockSpec(memory_space=pl.ANY)],
        out_specs=(pl.BlockSpec((CMB_TILE, D_MODEL), lambda i: (jnp.minimum(i, npt - 1), 0)),
                   pl.BlockSpec((CMB_TILE, D_MODEL), lambda i: (jnp.maximum(i - npt, 0), 0))),
        out_shape=(jax.ShapeDtypeStruct((n_prompt_rows, D_MODEL), F32),
                   jax.ShapeDtypeStruct((T - n_prompt_rows, D_MODEL), F32)),
        scratch_shapes=[pltpu.VMEM((2, TOP_K, CMB_TILE * SUBLANES, LANES), F32), pltpu.SemaphoreType.DMA((2,))],
        compiler_params=pltpu.CompilerParams(dimension_semantics=("arbitrary",), vmem_limit_bytes=VMEM_LIMIT),
        name="moe_combine",
    )(dest_tiles, dest_tiles, h, route, ys)


def kernel(x_prompt, x_sample, cache_k_win, cache_v_win, state_conv, state_C, state_n, state_m, g_attn, w_in, b_i,
           b_f, q_norm_g, k_norm_g, sinks, conv_w, conv_b, m_norm_g, w_out, g_ffn, w_router, b_router, w_gate_up,
           b_gate_up, w_down, b_down):
    depth = g_attn.shape[0]
    assert depth == 1
    B, S, _ = x_prompt.shape
    DB, DS, _ = x_sample.shape
    TP = B * S
    TS = DB * DS
    T = TP + TS
    assert T % TOK_TILE == 0 and TP % TOK_TILE == 0 and S % ATT_QB == 0 and S % PROMPT_CHUNK == 0
    assert DS == SUBLANES and DB % SAMPLE_NB == 0 and (SAMPLE_NB * DS) == LANES
    l = 0

    xp = x_prompt.reshape(TP, D_MODEL)
    xs = x_sample.reshape(TS, D_MODEL)

    w_pad = jnp.pad(w_in[l], ((0, 0), (0, IN_PAD - w_in.shape[2]))).astype(BF16)
    wgt = jnp.transpose(w_in[l][:, GATE_COL:GATE_COL + 2 * M_HEADS]).astype(BF16)
    gi = jnp.arange(A_WIDTH) // HEAD_DIM
    gmat = (gi[:, None] == gi[None, :]).astype(BF16)
    qg = jnp.tile(q_norm_g[l], A_HEADS).reshape(1, A_WIDTH)
    kg = jnp.tile(k_norm_g[l], A_KV_HEADS).reshape(1, KV_WIDTH)
    gbias = jnp.concatenate([b_i[l], b_f[l]])
    gbrow = jnp.pad(gbias, (0, LANES - 2 * M_HEADS)).reshape(1, LANES)
    gbcol = gbias.reshape(2 * M_HEADS, 1)
    mng = m_norm_g[l].reshape(1, M_WIDTH)
    cw = conv_w[l]
    cb = conv_b[l].reshape(1, 2 * M_WIDTH)
    wr_pad = jnp.pad(w_router[l], ((0, 0), (0, LANES - N_EXPERTS))).astype(BF16)
    br_pad = jnp.concatenate([b_router[l], jnp.full((LANES - N_EXPERTS,), NEG, F32)]).reshape(1, LANES)

    proj_w = (g_attn[l].reshape(1, D_MODEL), w_pad, wgt, gmat, qg, kg)
    qn, kn, va, qkm, vm, om, gcol, gt = _inproj(xp, *proj_w)
    qn_s, kn_s, va_s, qkm_s, vm_s, om_s, gcol_s, gt_s = _inproj(xs, *proj_w)

    a_p = _attn_prompt(sinks[l], qn, kn, va, B, S)
    ck = cache_k_win[l].reshape(DB, WINDOW, KV_WIDTH)
    cv = cache_v_win[l].reshape(DB, WINDOW, KV_WIDTH)
    a_s, kwin_s, vwin_s = _attn_sample(sinks[l], qn_s, kn_s, va_s, ck, cv, DB, DS)

    zc = jnp.zeros((B, SUBLANES, 2 * M_WIDTH), F32)
    m_p, C_p, n_p, mm_p = _mlstm(
        qkm, vm, om, gcol, gt, zc,
        jnp.zeros((B, M_HEADS, M_HEAD_DIM, M_HEAD_DIM), F32), jnp.zeros((B, M_HEADS, M_HEAD_DIM), F32),
        jnp.full((B, 1, LANES), NEG, F32), cw, cb, gbrow, gbcol, mng,
        nseq=1, L=PROMPT_CHUNK, ngroups=B, nchunks=S // PROMPT_CHUNK)
    conv_s0 = jnp.pad(state_conv[l], ((0, 0), (SUBLANES - (CONV_W - 1), 0), (0, 0)))
    m0_s = jnp.pad(state_m[l], ((0, 0), (0, LANES - M_HEADS))).reshape(DB, 1, LANES)
    m_s, C_s, n_s, mm_s = _mlstm(
        qkm_s, vm_s, om_s, gcol_s, gt_s, conv_s0, state_C[l], state_n[l], m0_s, cw, cb, gbrow, gbcol, mng,
        nseq=SAMPLE_NB, L=DS, ngroups=DB // SAMPLE_NB, nchunks=1)

    h, hn, route, cnt = _outproj(a_p, a_s, m_p, m_s, xp, xs, w_out[l].astype(BF16), g_ffn[l].reshape(1, D_MODEL),
                                 wr_pad, br_pad)

    nblk = T * TOP_K // MOE_BLOCK + N_EXPERTS
    dest, blk, info = _route_tables(route, cnt, nblk)
    block_e = blk[:, 0]
    padrow = info[0, :N_EXPERTS]
    npad = info[1, :N_EXPERTS]
    nused = info[2, N_EXPERTS - 1:N_EXPERTS]
    dest4 = dest[:, :TOP_K]
    xrows = _dispatch(padrow, npad, nused, dest4.reshape(T // TOK_TILE, 1, TOK_TILE * TOP_K), hn, nblk * MOE_BLOCK)
    yrows = _moe_blocks(block_e, nused, xrows,
                        w_gate_up[l], b_gate_up[l].reshape(N_EXPERTS, 1, 2 * D_FF),
                        w_down[l], b_down[l].reshape(N_EXPERTS, 1, D_MODEL))
    y_p, y_s = _combine(dest4.reshape(T // CMB_TILE, 1, CMB_TILE * TOP_K), h, route, yrows, TP)

    y_p = y_p.reshape(B, S, D_MODEL)
    y_s = y_s.reshape(DB, DS, D_MODEL)
    def seq_tail(rows, n):
        return jnp.stack([rows[(b + 1) * S - n:(b + 1) * S] for b in range(B)])

    kwin_p = seq_tail(kn, WINDOW).reshape(B, WINDOW, A_KV_HEADS, HEAD_DIM)
    vwin_p = seq_tail(va, WINDOW).reshape(B, WINDOW, A_KV_HEADS, HEAD_DIM)
    qkm_s = qkm_s.reshape(DB, DS, 2 * M_WIDTH)
    return (y_p, y_s,
            kwin_p[None], vwin_p[None], seq_tail(qkm, CONV_W - 1)[None],
            C_p[None], n_p[None], mm_p[:, 0, :M_HEADS][None],
            kwin_s.reshape(DB, WINDOW, A_KV_HEADS, HEAD_DIM)[None],
            vwin_s.reshape(DB, WINDOW, A_KV_HEADS, HEAD_DIM)[None],
            qkm_s[:, -(CONV_W - 1):][None],
            C_s[None], n_s[None], mm_s[:, 0, :M_HEADS][None])
```

```python
import functools

import jax
import jax.numpy as jnp
from jax import lax
from jax.experimental import pallas as pl
from jax.experimental.pallas import tpu as pltpu

F32 = jnp.float32
BF16 = jnp.bfloat16

D_MODEL = 1024
HEAD_DIM = 64
A_HEADS = 8
A_KV_HEADS = 2
A_GROUP = A_HEADS // A_KV_HEADS
A_WIDTH = A_HEADS * HEAD_DIM
KV_WIDTH = A_KV_HEADS * HEAD_DIM
WINDOW = 128
M_HEADS = 4
M_HEAD_DIM = 128
M_WIDTH = M_HEADS * M_HEAD_DIM
CONV_W = 4
N_EXPERTS = 32
TOP_K = 4
D_FF = D_MODEL
SWIGLU_LIMIT = 7.0
SWIGLU_ALPHA = 1.702
MOE_BLOCK = 512
EPS = 1e-6
NEG = -1e30

LANES = 128
SUBLANES = 8
GATE_COL = A_WIDTH + 2 * KV_WIDTH + 4 * M_WIDTH
IN_PAD = GATE_COL + LANES
TOK_TILE = 512
PROJ_TILE = 1024
PROJ_VMEM_LIMIT = 58 * 1024 * 1024
ATT_QB = 512
ATT_SB = 128
SAMPLE_NB = 16
PROMPT_CHUNK = 256
CMB_TILE = 512
DSP_TILE = 1024
VMEM_LIMIT = 48 * 1024 * 1024
MOE_VMEM_LIMIT = 56 * 1024 * 1024


def _dot(a, b):
    return jnp.dot(a, b, preferred_element_type=F32)


def _dot_nt(a, b):
    return lax.dot_general(a, b, (((1,), (1,)), ((), ())), preferred_element_type=F32)


def _dot_tn(a, b):
    return lax.dot_general(a, b, (((0,), (0,)), ((), ())), preferred_element_type=F32)


def _split3(x):
    hi = x.astype(BF16)
    r1 = x - hi.astype(F32)
    mid = r1.astype(BF16)
    lo = (r1 - mid.astype(F32)).astype(BF16)
    return hi, mid, lo


def _log_sigmoid(x):
    return jnp.minimum(x, 0.0) - jnp.log1p(jnp.exp(-jnp.abs(x)))


def _load_row_tiles(ref2, rows):
    return jnp.concatenate([ref2[pl.ds(s, rows, stride=SUBLANES), :] for s in range(SUBLANES)], axis=1)


def _store_row_tiles(ref2, rows, val):
    for s in range(SUBLANES):
        ref2[pl.ds(s, rows, stride=SUBLANES), :] = val[:, s * LANES:(s + 1) * LANES]


def _row_tile(ref2, idx):
    return ref2.at[pl.ds(pl.multiple_of(idx * SUBLANES, SUBLANES), SUBLANES), :]


def _row_tiles(ref2, first, n):
    return ref2.at[pl.ds(pl.multiple_of(first * SUBLANES, SUBLANES), n * SUBLANES), :]


def _split_specs(n_prompt_tiles, rows, width):
    return (pl.BlockSpec((rows, width), lambda i, *_: (jnp.minimum(i, n_prompt_tiles - 1), 0)),
            pl.BlockSpec((rows, width), lambda i, *_: (jnp.maximum(i - n_prompt_tiles, 0), 0)))


def _inproj_kernel(x_ref, g_ref, w_ref, wgt_ref, gmat_ref, qg_ref, kg_ref,
                   qn_ref, kn_ref, va_ref, qkm_ref, vm_ref, om_ref, gcol_ref, gt_ref):
    x = x_ref[...]
    ms = jnp.mean(x * x, axis=-1, keepdims=True)
    xn = ((x * lax.rsqrt(ms + EPS)) * g_ref[...]).astype(BF16)

    def seg(lo, hi):
        return _dot(xn, w_ref[:, lo:hi])

    def head_norm(z, gmat, g):
        hi, mid, lo = _split3(z * z)
        ss = _dot(hi, gmat) + _dot(mid, gmat) + _dot(lo, gmat)
        return (z * lax.rsqrt(ss * (1.0 / HEAD_DIM) + EPS)) * g

    o0 = A_WIDTH
    o1 = o0 + KV_WIDTH
    o2 = o1 + KV_WIDTH
    o3 = o2 + 2 * M_WIDTH
    o4 = o3 + M_WIDTH
    o5 = o4 + M_WIDTH
    qn_ref[...] = head_norm(seg(0, o0), gmat_ref[...], qg_ref[...])
    kn_ref[...] = head_norm(seg(o0, o1), gmat_ref[:KV_WIDTH, :KV_WIDTH], kg_ref[...])
    va_ref[...] = seg(o1, o2)
    qkm_ref[...] = seg(o2, o3)
    vm_ref[...] = seg(o3, o4)
    om_ref[...] = seg(o4, o5)
    gcol_ref[...] = seg(o5, o5 + LANES)
    gt_ref[0] = _dot_nt(wgt_ref[...], xn)


def _inproj(x, g_attn, w_pad, wgt, gmat, qg, kg):
    T = x.shape[0]
    nt = T // PROJ_TILE
    row = lambda w: pl.BlockSpec((PROJ_TILE, w), lambda i: (i, 0))
    full = lambda a: pl.BlockSpec(a.shape, lambda i: (0,) * a.ndim, pipeline_mode=pl.Buffered(1))
    out_shape = (
        jax.ShapeDtypeStruct((T, A_WIDTH), F32),
        jax.ShapeDtypeStruct((T, KV_WIDTH), F32),
        jax.ShapeDtypeStruct((T, KV_WIDTH), F32),
        jax.ShapeDtypeStruct((T, 2 * M_WIDTH), F32),
        jax.ShapeDtypeStruct((T, M_WIDTH), F32),
        jax.ShapeDtypeStruct((T, M_WIDTH), F32),
        jax.ShapeDtypeStruct((T, LANES), F32),
        jax.ShapeDtypeStruct((nt, SUBLANES, PROJ_TILE), F32),
    )
    out_specs = (row(A_WIDTH), row(KV_WIDTH), row(KV_WIDTH), row(2 * M_WIDTH), row(M_WIDTH), row(M_WIDTH),
                 row(LANES), pl.BlockSpec((1, SUBLANES, PROJ_TILE), lambda i: (i, 0, 0)))
    return pl.pallas_call(
        _inproj_kernel,
        grid=(nt,),
        in_specs=[row(D_MODEL), full(g_attn), full(w_pad), full(wgt), full(gmat), full(qg), full(kg)],
        out_specs=out_specs,
        out_shape=out_shape,
        compiler_params=pltpu.CompilerParams(dimension_semantics=("parallel",), vmem_limit_bytes=PROJ_VMEM_LIMIT),
        name="inproj",
    )(x, g_attn, w_pad, wgt, gmat, qg, kg)


def _softmax_sink(pieces, masks, sink_col):
    masked = [jnp.where(mk, s, NEG) for s, mk in zip(pieces, masks)]
    m = sink_col
    for s in masked:
        m = jnp.maximum(m, jnp.max(s, axis=-1, keepdims=True))
    ps = [jnp.exp(s - m) for s in masked]
    den = jnp.exp(sink_col - m)
    for p in ps:
        den = den + jnp.sum(p, axis=-1, keepdims=True)
    return ps, 1.0 / den


def _stack_heads(q, g):
    return jnp.concatenate([q[:, (A_GROUP * g + i) * HEAD_DIM:(A_GROUP * g + i + 1) * HEAD_DIM]
                            for i in range(A_GROUP)], axis=0)


def _sink_col(sink_ref, g, rows_per_head):
    r = lax.broadcasted_iota(jnp.int32, (A_GROUP * rows_per_head, 1), 0)
    col = jnp.zeros((A_GROUP * rows_per_head, 1), F32)
    for i in range(A_GROUP):
        col = jnp.where(r // rows_per_head == i, sink_ref[A_GROUP * g + i], col)
    return col


def _attn_prompt_kernel(sink_ref, q_ref, kp_ref, kc_ref, vp_ref, vc_ref, o_ref):
    j = pl.program_id(1)
    scale = HEAD_DIM ** -0.5
    kall = jnp.concatenate([kp_ref[...], kc_ref[...]], axis=0).astype(BF16)
    vall = jnp.concatenate([vp_ref[...], vc_ref[...]], axis=0).astype(BF16)
    nrow = A_GROUP * ATT_SB
    r = lax.broadcasted_iota(jnp.int32, (nrow, 2 * ATT_SB), 0) % ATT_SB
    c = lax.broadcasted_iota(jnp.int32, (nrow, 2 * ATT_SB), 1)
    band = jnp.logical_and(c >= r, c <= r + WINDOW)
    band0 = jnp.logical_and(band, jnp.logical_or(c >= ATT_SB, j > 0))
    for sb in range(ATT_QB // ATT_SB):
        q = (q_ref[sb * ATT_SB:(sb + 1) * ATT_SB, :] * scale).astype(BF16)
        kwin = kall[sb * ATT_SB:(sb + 2) * ATT_SB]
        vwin = vall[sb * ATT_SB:(sb + 2) * ATT_SB]
        outs = []
        for g in range(A_KV_HEADS):
            lo, hi = g * HEAD_DIM, (g + 1) * HEAD_DIM
            s = _dot_nt(_stack_heads(q, g), kwin[:, lo:hi])
            (p,), inv = _softmax_sink([s], [band0 if sb == 0 else band], _sink_col(sink_ref, g, ATT_SB))
            o = _dot(p.astype(BF16), vwin[:, lo:hi]) * inv
            outs += [o[i * ATT_SB:(i + 1) * ATT_SB] for i in range(A_GROUP)]
        o_ref[sb * ATT_SB:(sb + 1) * ATT_SB, :] = jnp.concatenate(outs, axis=1)


def _attn_prompt(sinks, qn, kn, va, batch, seq):
    nq = seq // ATT_QB
    ratio = ATT_QB // ATT_SB
    cur = lambda w: pl.BlockSpec((ATT_QB, w), lambda b, j: (b * nq + j, 0))
    prev = lambda w: pl.BlockSpec((ATT_SB, w), lambda b, j: (jnp.maximum((b * nq + j) * ratio - 1, 0), 0))
    return pl.pallas_call(
        _attn_prompt_kernel,
        grid=(batch, nq),
        in_specs=[pl.BlockSpec(memory_space=pltpu.SMEM), cur(A_WIDTH), prev(KV_WIDTH), cur(KV_WIDTH),
                  prev(KV_WIDTH), cur(KV_WIDTH)],
        out_specs=cur(A_WIDTH),
        out_shape=jax.ShapeDtypeStruct((batch * seq, A_WIDTH), F32),
        compiler_params=pltpu.CompilerParams(dimension_semantics=("parallel", "parallel"),
                                             vmem_limit_bytes=VMEM_LIMIT),
        name="attn_prompt",
    )(sinks, qn, kn, kn, va, va)


def _attn_sample_kernel(dec, sink_ref, q_ref, kn_ref, vn_ref, ck_ref, cv_ref, o_ref, kw_ref, vw_ref):
    scale = HEAD_DIM ** -0.5
    rows = SAMPLE_NB * dec
    knew = kn_ref[...]
    vnew = vn_ref[...]
    knew_b = knew.astype(BF16)
    vnew_b = vnew.astype(BF16)
    nrow = A_GROUP * dec
    t = lax.broadcasted_iota(jnp.int32, (nrow, WINDOW), 0) % dec
    c = lax.broadcasted_iota(jnp.int32, (nrow, WINDOW), 1)
    m_cache = c >= t
    cn = lax.broadcasted_iota(jnp.int32, (nrow, rows), 1)
    tn = lax.broadcasted_iota(jnp.int32, (nrow, rows), 0) % dec
    for i in range(SAMPLE_NB):
        q = (q_ref[i * dec:(i + 1) * dec, :] * scale).astype(BF16)
        ck = ck_ref[i].astype(BF16)
        cv = cv_ref[i].astype(BF16)
        m_new = jnp.logical_and(cn // dec == i, cn % dec <= tn)
        outs = []
        for g in range(A_KV_HEADS):
            lo, hi = g * HEAD_DIM, (g + 1) * HEAD_DIM
            qs = _stack_heads(q, g)
            s_c = _dot_nt(qs, ck[:, lo:hi])
            s_n = _dot_nt(qs, knew_b[:, lo:hi])
            (p_c, p_n), inv = _softmax_sink([s_c, s_n], [m_cache, m_new], _sink_col(sink_ref, g, dec))
            o = (_dot(p_c.astype(BF16), cv[:, lo:hi]) + _dot(p_n.astype(BF16), vnew_b[:, lo:hi])) * inv
            outs += [o[h * dec:(h + 1) * dec] for h in range(A_GROUP)]
        o_ref[i * dec:(i + 1) * dec, :] = jnp.concatenate(outs, axis=1)
        kw_ref[i, 0:WINDOW - dec, :] = ck_ref[i, dec:WINDOW, :]
        kw_ref[i, WINDOW - dec:WINDOW, :] = knew[i * dec:(i + 1) * dec]
        vw_ref[i, 0:WINDOW - dec, :] = cv_ref[i, dec:WINDOW, :]
        vw_ref[i, WINDOW - dec:WINDOW, :] = vnew[i * dec:(i + 1) * dec]


def _attn_sample(sinks, qn, kn, va, ck, cv, dbatch, dec):
    rows = SAMPLE_NB * dec
    tokrow = lambda w: pl.BlockSpec((rows, w), lambda i: (i, 0))
    cache = pl.BlockSpec((SAMPLE_NB, WINDOW, KV_WIDTH), lambda i: (i, 0, 0))
    return pl.pallas_call(
        functools.partial(_attn_sample_kernel, dec),
        grid=(dbatch // SAMPLE_NB,),
        in_specs=[pl.BlockSpec(memory_space=pltpu.SMEM), tokrow(A_WIDTH), tokrow(KV_WIDTH), tokrow(KV_WIDTH),
                  cache, cache],
        out_specs=(pl.BlockSpec((rows, A_WIDTH), lambda i: (i, 0)), cache, cache),
        out_shape=(jax.ShapeDtypeStruct((dbatch * dec, A_WIDTH), F32),
                   jax.ShapeDtypeStruct((dbatch, WINDOW, KV_WIDTH), F32),
                   jax.ShapeDtypeStruct((dbatch, WINDOW, KV_WIDTH), F32)),
        compiler_params=pltpu.CompilerParams(dimension_semantics=("parallel",), vmem_limit_bytes=VMEM_LIMIT),
        name="attn_sample",
    )(sinks, qn, kn, va, ck, cv)


def _mlstm_kernel(nseq, L, qk_ref, v_ref, o_ref, gcol_ref, gt_ref, conv0_ref, c0_ref, n0_ref, m0_ref,
                  cw_ref, cb_ref, gbrow_ref, gbcol_ref, mng_ref, mask_ref,
                  out_ref, cst_ref, nst_ref, mst_ref, prev_ref):
    R = nseq * L
    ci = pl.program_id(1)

    @pl.when(ci == 0)
    def _():
        cst_ref[...] = c0_ref[...]
        nst_ref[...] = n0_ref[...]
        mst_ref[...] = m0_ref[...]
        prev_ref[...] = conv0_ref[0]

    raw = qk_ref[...]
    row = lax.broadcasted_iota(jnp.int32, (R, 1), 0)
    tpos = row % L
    rseq = row // L
    acc = raw * cw_ref[CONV_W - 1:CONV_W, :] + cb_ref[...]
    if nseq == 1:
        prev8 = prev_ref[...]
        t8 = lax.broadcasted_iota(jnp.int32, (SUBLANES, 1), 0)
    else:
        prevsrc = conv0_ref[...].reshape(R, 2 * M_WIDTH)
    for k in range(1, CONV_W):
        rolled = pltpu.roll(raw, k, 0)
        if nseq == 1:
            head = jnp.where(t8 >= k, rolled[0:SUBLANES], pltpu.roll(prev8, k, 0))
            sh = jnp.concatenate([head, rolled[SUBLANES:]], axis=0)
        else:
            sh = jnp.where(tpos >= k, rolled, pltpu.roll(prevsrc, R - SUBLANES + k, 0))
        acc = acc + sh * cw_ref[CONV_W - 1 - k:CONV_W - k, :]
    if nseq == 1:
        prev_ref[...] = raw[R - SUBLANES:R]
    qkc = acc * jax.nn.sigmoid(acc)

    gc = gcol_ref[...] + gbrow_ref[...]
    gr = gt_ref[0] + gbcol_ref[...]
    lsc = _log_sigmoid(gc)
    lsr = _log_sigmoid(gr)
    mb = mask_ref[...]
    maskb = mb > 0
    bcol = sum(_dot(mb, p) for p in _split3(lsc))
    brow = sum(_dot_nt(p, mb) for p in _split3(lsr))

    lane = lax.broadcasted_iota(jnp.int32, (1, LANES), 1)
    m_new = [jnp.zeros((1, LANES), F32) for _ in range(nseq)]
    for h in range(M_HEADS):
        sl = slice(h * M_HEAD_DIM, (h + 1) * M_HEAD_DIM)
        qh = qkc[:, sl]
        kh = qkc[:, M_WIDTH + h * M_HEAD_DIM:M_WIDTH + (h + 1) * M_HEAD_DIM] * (M_HEAD_DIM ** -0.5)
        vh = v_ref[:, sl]
        qb, kb, vb = qh.astype(BF16), kh.astype(BF16), vh.astype(BF16)
        ig_c = gc[:, h:h + 1]
        b_c = bcol[:, M_HEADS + h:M_HEADS + h + 1]
        ig_r = gr[h:h + 1, :]
        b_r = brow[M_HEADS + h:M_HEADS + h + 1, :]
        if nseq == 1:
            m0c = mst_ref[0][:, h:h + 1]
            n0rows = nst_ref[0, h:h + 1, :]
        else:
            m0c = jnp.zeros((R, 1), F32)
            n0rows = jnp.zeros((R, M_HEAD_DIM), F32)
            for s in range(nseq):
                m0c = jnp.where(rseq == s, mst_ref[s][:, h:h + 1], m0c)
                n0rows = jnp.where(rseq == s, nst_ref[s, h:h + 1, :], n0rows)
        dm = jnp.where(maskb, b_c - b_r + ig_r, NEG)
        a_c = b_c + m0c
        m_c = jnp.maximum(a_c, jnp.max(dm, axis=-1, keepdims=True))
        w = jnp.exp(dm - m_c)
        sc = jnp.exp(a_c - m_c)
        wqk = w * _dot_nt(qb, kb)
        if nseq == 1:
            inter = _dot_nt(qb, cst_ref[0, h].astype(BF16))
        else:
            inter = jnp.zeros((R, M_HEAD_DIM), F32)
            for s in range(nseq):
                qs = jnp.where(rseq == s, qh, 0.0).astype(BF16)
                inter = inter + _dot_nt(qs, cst_ref[s, h].astype(BF16))
        num = _dot(wqk.astype(BF16), vb) + sc * inter
        den = jnp.sum(wqk, axis=-1, keepdims=True) + sc * jnp.sum(qh * n0rows, axis=-1, keepdims=True)
        hh = num / jnp.maximum(jnp.abs(den), jnp.exp(-m_c))

        for s in range(nseq):
            e = s * L + L - 1
            m_end = m_c[e:e + 1, :]
            wend = jnp.exp(b_c[e:e + 1, :] - b_c + ig_c - m_end)
            if nseq > 1:
                wend = jnp.where(rseq == s, wend, 0.0)
            sce = jnp.exp(a_c[e:e + 1, :] - m_end)
            c_new = sce * cst_ref[s, h] + _dot_tn((vh * wend).astype(BF16), kb)
            n_new = sce * nst_ref[s, h:h + 1, :] + jnp.sum(wend * kh, axis=0, keepdims=True)
            cst_ref[s, h] = c_new
            nst_ref[s, h:h + 1, :] = n_new
            m_new[s] = jnp.where(lane == h, m_end, m_new[s])

        hn = (hh * lax.rsqrt(jnp.mean(hh * hh, axis=-1, keepdims=True) + EPS)) * mng_ref[:, sl]
        out_ref[:, sl] = jax.nn.sigmoid(o_ref[:, sl]) * hn
    for s in range(nseq):
        mst_ref[s] = m_new[s]


def _mlstm(qkm, vm, om, gcol, gt, conv0, c0, n0, m0, cw, cb, gbrow, gbcol, mng, nseq, L, ngroups, nchunks):
    R = nseq * L
    per_tile = gt.shape[2] // R
    tok = lambda w: pl.BlockSpec((R, w), lambda g, c: (g * nchunks + c, 0))
    gt_spec = pl.BlockSpec((1, SUBLANES, R),
                           lambda g, c: ((g * nchunks + c) // per_tile, 0, (g * nchunks + c) % per_tile))
    full = lambda a: pl.BlockSpec(a.shape, lambda g, c: (0,) * a.ndim)
    st4 = pl.BlockSpec((nseq, M_HEADS, M_HEAD_DIM, M_HEAD_DIM), lambda g, c: (g, 0, 0, 0))
    st3 = pl.BlockSpec((nseq, M_HEADS, M_HEAD_DIM), lambda g, c: (g, 0, 0))
    stm = pl.BlockSpec((nseq, 1, LANES), lambda g, c: (g, 0, 0))
    conv_spec = pl.BlockSpec((nseq, SUBLANES, 2 * M_WIDTH), lambda g, c: (g, 0, 0))
    r = jnp.arange(R)
    mask = ((r[:, None] // L == r[None, :] // L) & (r[None, :] <= r[:, None])).astype(BF16)
    nstate = ngroups * nseq
    return pl.pallas_call(
        functools.partial(_mlstm_kernel, nseq, L),
        grid=(ngroups, nchunks),
        in_specs=[tok(2 * M_WIDTH), tok(M_WIDTH), tok(M_WIDTH), tok(LANES), gt_spec, conv_spec, st4, st3, stm,
                  full(cw), full(cb), full(gbrow), full(gbcol), full(mng), full(mask)],
        out_specs=(pl.BlockSpec((R, M_WIDTH), lambda g, c: (g * nchunks + c, 0)), st4, st3, stm),
        out_shape=(jax.ShapeDtypeStruct((ngroups * nchunks * R, M_WIDTH), F32),
                   jax.ShapeDtypeStruct((nstate, M_HEADS, M_HEAD_DIM, M_HEAD_DIM), F32),
                   jax.ShapeDtypeStruct((nstate, M_HEADS, M_HEAD_DIM), F32),
                   jax.ShapeDtypeStruct((nstate, 1, LANES), F32)),
        scratch_shapes=[pltpu.VMEM((SUBLANES, 2 * M_WIDTH), F32)],
        compiler_params=pltpu.CompilerParams(dimension_semantics=("parallel", "arbitrary"),
                                             vmem_limit_bytes=VMEM_LIMIT),
        name="mlstm_n%d" % nseq,
    )(qkm, vm, om, gcol, gt, conv0, c0, n0, m0, cw, cb, gbrow, gbcol, mng, mask)


def _outproj_kernel(npt, ap_ref, as_ref, mp_ref, ms_ref, xp_ref, xs_ref, wo_ref, g_ref, wr_ref, br_ref,
                    h_ref, hn_ref, route_ref, cnt_ref):
    def project(a_ref, m_ref, x_ref):
        h_ref[...] = (x_ref[...] + _dot(a_ref[...].astype(BF16), wo_ref[0:A_WIDTH, :])
                      + _dot(m_ref[...].astype(BF16), wo_ref[A_WIDTH:A_WIDTH + M_WIDTH, :]))

    @pl.when(pl.program_id(0) < npt)
    def _():
        project(ap_ref, mp_ref, xp_ref)

    @pl.when(pl.program_id(0) >= npt)
    def _():
        project(as_ref, ms_ref, xs_ref)

    h = h_ref[...]
    hn = (h * lax.rsqrt(jnp.mean(h * h, axis=-1, keepdims=True) + EPS)) * g_ref[...]
    _store_row_tiles(hn_ref, TOK_TILE, hn)
    logits = _dot(hn.astype(BF16), wr_ref[...]) + br_ref[...]
    lane = lax.broadcasted_iota(jnp.int32, logits.shape, 1).astype(F32)
    route = jnp.zeros(logits.shape, F32)
    picked = jnp.zeros(logits.shape, F32)
    top0 = None
    den = None
    es = []
    for k in range(TOP_K):
        mx = jnp.max(logits, axis=-1, keepdims=True)
        idx = jnp.min(jnp.where(logits == mx, lane, float(LANES)), axis=-1, keepdims=True)
        if k == 0:
            top0 = mx
        e = jnp.exp(mx - top0)
        den = e if den is None else den + e
        es.append(e)
        route = jnp.where(lane == float(TOP_K + k), idx, route)
        picked = jnp.where(lane == idx, 1.0, picked)
        logits = jnp.where(lane == idx, -jnp.inf, logits)
    for k in range(TOP_K):
        route = jnp.where(lane == float(k), es[k] / den, route)
    route_ref[...] = route

    @pl.when(pl.program_id(0) == 0)
    def _():
        cnt_ref[...] = jnp.zeros_like(cnt_ref)

    cnt_ref[0:1, :] += jnp.sum(picked, axis=0, keepdims=True)


def _outproj(a_p, a_s, m_p, m_s, xp, xs, w_out, g_ffn, wr_pad, br_pad):
    T = xp.shape[0] + xs.shape[0]
    npt = xp.shape[0] // TOK_TILE
    row = lambda w: pl.BlockSpec((TOK_TILE, w), lambda i: (i, 0))
    full = lambda a: pl.BlockSpec(a.shape, lambda i: (0,) * a.ndim)
    return pl.pallas_call(
        functools.partial(_outproj_kernel, npt),
        grid=(T // TOK_TILE,),
        in_specs=[*_split_specs(npt, TOK_TILE, A_WIDTH), *_split_specs(npt, TOK_TILE, M_WIDTH),
                  *_split_specs(npt, TOK_TILE, D_MODEL), full(w_out), full(g_ffn), full(wr_pad), full(br_pad)],
        out_specs=(row(D_MODEL), pl.BlockSpec((TOK_TILE * SUBLANES, LANES), lambda i: (i, 0)), row(LANES),
                   pl.BlockSpec((SUBLANES, LANES), lambda i: (0, 0))),
        out_shape=(jax.ShapeDtypeStruct((T, D_MODEL), F32), jax.ShapeDtypeStruct((T * SUBLANES, LANES), F32),
                   jax.ShapeDtypeStruct((T, LANES), F32), jax.ShapeDtypeStruct((SUBLANES, LANES), F32)),
        compiler_params=pltpu.CompilerParams(dimension_semantics=("arbitrary",), vmem_limit_bytes=VMEM_LIMIT),
        name="outproj_router",
    )(a_p, a_s, m_p, m_s, xp, xs, w_out, g_ffn, wr_pad, br_pad)


def _route_kernel(nblk, route_ref, cnt_ref, lstrict_ref, ustrict_ref, dest_ref, blk_ref, info_ref, carry_ref):
    i = pl.program_id(0)
    lane = lax.broadcasted_iota(jnp.int32, (1, LANES), 1)
    cnt = cnt_ref[0:1, :]
    nb_e = jnp.floor((cnt + (MOE_BLOCK - 1.0)) * (1.0 / MOE_BLOCK))
    u = ustrict_ref[...]
    bstart = sum(_dot(p, u) for p in _split3(jnp.broadcast_to(nb_e, (SUBLANES, LANES))))[0:1, :]
    bend = bstart + nb_e
    row_start = bstart * float(MOE_BLOCK)

    @pl.when(i == 0)
    def _():
        carry_ref[...] = jnp.zeros_like(carry_ref)
        bi = lax.broadcasted_iota(jnp.int32, (nblk, LANES), 0).astype(F32)
        done = jnp.logical_and(bend <= bi, lane < N_EXPERTS)
        be = jnp.minimum(jnp.sum(jnp.where(done, 1.0, 0.0), axis=-1, keepdims=True), N_EXPERTS - 1.0)
        blk_ref[...] = jnp.broadcast_to(be, (nblk, LANES)).astype(jnp.int32)
        info = jnp.zeros((SUBLANES, LANES), F32)
        srow = lax.broadcasted_iota(jnp.int32, (SUBLANES, LANES), 0)
        info = jnp.where(srow == 0, row_start + cnt, info)
        info = jnp.where(srow == 1, nb_e * float(MOE_BLOCK) - cnt, info)
        info = jnp.where(srow == 2, bend, info)
        info_ref[...] = info.astype(jnp.int32)

    r = route_ref[...]
    lane_f = lane.astype(F32)
    sel = [lane_f == r[:, TOP_K + k:TOP_K + k + 1] for k in range(TOP_K)]
    oh = jnp.zeros(r.shape, F32)
    for k in range(TOP_K):
        oh = jnp.where(sel[k], 1.0, oh)
    before = _dot(lstrict_ref[...], oh.astype(BF16)) + carry_ref[...] + row_start
    dest = jnp.zeros(r.shape, jnp.int32)
    for k in range(TOP_K):
        d = jnp.sum(jnp.where(sel[k], before, 0.0), axis=-1, keepdims=True)
        dest = jnp.where(lane == k, d.astype(jnp.int32), dest)
    dest_ref[...] = dest
    carry_ref[...] += jnp.sum(oh, axis=0, keepdims=True)


def _route_tables(route, cnt, nblk):
    T = route.shape[0]
    a = jnp.arange(TOK_TILE)
    lstrict = (a[:, None] > a[None, :]).astype(BF16)
    b = jnp.arange(LANES)
    ustrict = (b[:, None] < b[None, :]).astype(BF16)
    full = lambda x: pl.BlockSpec(x.shape, lambda i: (0,) * x.ndim)
    return pl.pallas_call(
        functools.partial(_route_kernel, nblk),
        grid=(T // TOK_TILE,),
        in_specs=[pl.BlockSpec((TOK_TILE, LANES), lambda i: (i, 0)), full(cnt), full(lstrict), full(ustrict)],
        out_specs=(pl.BlockSpec((TOK_TILE, LANES), lambda i: (i, 0)),
                   pl.BlockSpec((nblk, LANES), lambda i: (0, 0)),
                   pl.BlockSpec((SUBLANES, LANES), lambda i: (0, 0))),
        out_shape=(jax.ShapeDtypeStruct((T, LANES), jnp.int32), jax.ShapeDtypeStruct((nblk, LANES), jnp.int32),
                   jax.ShapeDtypeStruct((SUBLANES, LANES), jnp.int32)),
        scratch_shapes=[pltpu.VMEM((1, LANES), F32)],
        compiler_params=pltpu.CompilerParams(dimension_semantics=("arbitrary",), vmem_limit_bytes=VMEM_LIMIT),
        name="route_tables",
    )(route, cnt, lstrict, ustrict)


def _dispatch_kernel(padrow_ref, npad_ref, nu_ref, dest_ref, hn_ref, xs_hbm, zbuf, sem, zsem):
    i = pl.program_id(0)

    @pl.when(i == 0)
    def _():
        zbuf[...] = jnp.zeros_like(zbuf)
        nblk = xs_hbm.shape[0] // (MOE_BLOCK * SUBLANES)

        def tail_start(b, c):
            pltpu.make_async_copy(zbuf, _row_tiles(xs_hbm, b * MOE_BLOCK, MOE_BLOCK), zsem).start()
            return c

        def tail_wait(b, c):
            pltpu.make_async_copy(zbuf, _row_tiles(xs_hbm, b * MOE_BLOCK, MOE_BLOCK), zsem).wait()
            return c

        lax.fori_loop(nu_ref[0], nblk, tail_start, 0)
        lax.fori_loop(nu_ref[0], nblk, tail_wait, 0)

        def per_expert(e, carry):
            base = padrow_ref[e]
            n = npad_ref[e]

            def start(r, c):
                pltpu.make_async_copy(_row_tile(zbuf, 0), _row_tile(xs_hbm, base + r), zsem).start()
                return c

            def wait(r, c):
                pltpu.make_async_copy(_row_tile(zbuf, 0), _row_tile(xs_hbm, base + r), zsem).wait()
                return c

            lax.fori_loop(0, n, start, 0)
            lax.fori_loop(0, n, wait, 0)
            return carry

        lax.fori_loop(0, N_EXPERTS, per_expert, 0)

    def body(j, carry):
        for u in range(SUBLANES):
            t = j * SUBLANES + u
            for k in range(TOP_K):
                d = dest_ref[0, 0, t * TOP_K + k]
                pltpu.make_async_copy(_row_tile(hn_ref, t), _row_tile(xs_hbm, d), sem).start(
                    priority=(u * TOP_K + k) % 2)
        return carry

    lax.fori_loop(0, DSP_TILE // SUBLANES, body, 0)
    for _ in range(TOP_K):
        pltpu.make_async_copy(hn_ref, _row_tiles(xs_hbm, 0, DSP_TILE), sem).wait()


def _dispatch(padrow, npad, nused, dest_tiles, hn, n_rows):
    T = hn.shape[0] // SUBLANES
    grid_spec = pltpu.PrefetchScalarGridSpec(
        num_scalar_prefetch=3,
        grid=(T // DSP_TILE,),
        in_specs=[pl.BlockSpec((1, 1, DSP_TILE * TOP_K), lambda i, *_: (i, 0, 0), memory_space=pltpu.SMEM),
                  pl.BlockSpec((DSP_TILE * SUBLANES, LANES), lambda i, *_: (i, 0))],
        out_specs=pl.BlockSpec(memory_space=pl.ANY),
        scratch_shapes=[pltpu.VMEM((MOE_BLOCK * SUBLANES, LANES), F32), pltpu.SemaphoreType.DMA(()),
                        pltpu.SemaphoreType.DMA(())],
    )
    return pl.pallas_call(
        _dispatch_kernel,
        grid_spec=grid_spec,
        out_shape=jax.ShapeDtypeStruct((n_rows * SUBLANES, LANES), F32),
        compiler_params=pltpu.CompilerParams(dimension_semantics=("arbitrary",), vmem_limit_bytes=VMEM_LIMIT),
        name="moe_dispatch",
    )(padrow, npad, nused, dest_tiles, hn)


def _moe_kernel(be_ref, nu_ref, first_ref, slot_ref, nxt_ref, x_ref, bgu_ref, bd_ref, wgu_hbm, wd_hbm, y_ref,
                wgu_buf, wd_buf, wsem):
    i = pl.program_id(0)
    used = i < nu_ref[0]
    s = slot_ref[i]

    def fetch(e, sl):
        return (pltpu.make_async_copy(wgu_hbm.at[e], wgu_buf.at[sl], wsem.at[0, sl]),
                pltpu.make_async_copy(wd_hbm.at[e], wd_buf.at[sl], wsem.at[1, sl]))

    @pl.when(jnp.logical_and(used, first_ref[i] == 1))
    def _():
        @pl.when(i == 0)
        def _():
            for c in fetch(be_ref[0], 0):
                c.start()

        for c in fetch(be_ref[i], s):
            c.wait()

        @pl.when(nxt_ref[i] >= 0)
        def _():
            for c in fetch(nxt_ref[i], 1 - s):
                c.start()

    @pl.when(used)
    def _():
        x = _load_row_tiles(x_ref, MOE_BLOCK).astype(BF16)
        hb = _dot(x, wgu_buf[s].astype(BF16)) + bgu_ref[0]
        glu = jnp.minimum(hb[:, :D_FF], SWIGLU_LIMIT)
        lin = jnp.clip(hb[:, D_FF:], -SWIGLU_LIMIT, SWIGLU_LIMIT)
        act = glu * jax.nn.sigmoid(SWIGLU_ALPHA * glu) * (lin + 1.0)
        _store_row_tiles(y_ref, MOE_BLOCK, _dot(act.astype(BF16), wd_buf[s].astype(BF16)) + bd_ref[0])

    @pl.when(i >= nu_ref[0])
    def _():
        y_ref[...] = jnp.zeros_like(y_ref)


def _moe_blocks(block_e, nused, xs, wgu, bgu, wd, bd):
    nblk = block_e.shape[0]
    idx = jnp.arange(nblk, dtype=jnp.int32)
    first = (idx < nused[0]) & ((idx == 0) | (block_e != jnp.roll(block_e, 1)))
    slot = ((jnp.cumsum(first.astype(jnp.int32)) - 1) % 2).astype(jnp.int32)
    first_pos = jnp.where(first, idx, nblk)
    later = jnp.concatenate([first_pos[1:], jnp.full((1,), nblk, jnp.int32)])
    next_pos = lax.cummin(later, reverse=True)
    nxt = jnp.sum(jnp.where(idx[None, :] == next_pos[:, None], block_e[None, :] + 1, 0), axis=1) - 1
    grid_spec = pltpu.PrefetchScalarGridSpec(
        num_scalar_prefetch=5,
        grid=(nblk,),
        in_specs=[
            pl.BlockSpec((MOE_BLOCK * SUBLANES, LANES), lambda i, be, *_: (i, 0)),
            pl.BlockSpec((1, 1, 2 * D_FF), lambda i, be, *_: (be[i], 0, 0)),
            pl.BlockSpec((1, 1, D_MODEL), lambda i, be, *_: (be[i], 0, 0)),
            pl.BlockSpec(memory_space=pl.ANY),
            pl.BlockSpec(memory_space=pl.ANY),
        ],
        out_specs=pl.BlockSpec((MOE_BLOCK * SUBLANES, LANES), lambda i, be, *_: (i, 0)),
        scratch_shapes=[pltpu.VMEM((2, D_MODEL, 2 * D_FF), F32), pltpu.VMEM((2, D_FF, D_MODEL), F32),
                        pltpu.SemaphoreType.DMA((2, 2))],
    )
    return pl.pallas_call(
        _moe_kernel,
        grid_spec=grid_spec,
        out_shape=jax.ShapeDtypeStruct(xs.shape, F32),
        compiler_params=pltpu.CompilerParams(dimension_semantics=("arbitrary",), vmem_limit_bytes=MOE_VMEM_LIMIT),
        name="moe_blocks",
    )(block_e, nused, first.astype(jnp.int32), slot, nxt.astype(jnp.int32), xs, bgu, bd, wgu, wd)


def _combine_kernel(npt, destc_ref, destn_ref, h_ref, route_ref, ys_hbm, yp_ref, ysm_ref, gbuf, gsem):
    i = pl.program_id(0)
    nt = pl.num_programs(0)
    slot = i % 2

    def issue(dest_ref, s):
        def body(j, carry):
            for u in range(SUBLANES):
                t = j * SUBLANES + u
                for k in range(TOP_K):
                    d = dest_ref[0, 0, t * TOP_K + k]
                    pltpu.make_async_copy(_row_tile(ys_hbm, d), _row_tile(gbuf.at[s, k], t), gsem.at[s]).start(
                        priority=(u * TOP_K + k) % 2)
            return carry
        lax.fori_loop(0, CMB_TILE // SUBLANES, body, 0)

    @pl.when(i == 0)
    def _():
        issue(destc_ref, 0)

    @pl.when(i + 1 < nt)
    def _():
        issue(destn_ref, 1 - slot)

    for k in range(TOP_K):
        pltpu.make_async_copy(_row_tiles(ys_hbm, 0, CMB_TILE), gbuf.at[slot, k], gsem.at[slot]).wait()
    r = route_ref[...]
    moe = _load_row_tiles(gbuf.at[slot, 0], CMB_TILE) * r[:, 0:1]
    for k in range(1, TOP_K):
        moe = moe + _load_row_tiles(gbuf.at[slot, k], CMB_TILE) * r[:, k:k + 1]
    y = h_ref[...] + moe

    @pl.when(i < npt)
    def _():
        yp_ref[...] = y

    @pl.when(i >= npt)
    def _():
        ysm_ref[...] = y


def _combine(dest_tiles, h, route, ys, n_prompt_rows):
    T = h.shape[0]
    nt = T // CMB_TILE
    npt = n_prompt_rows // CMB_TILE
    smem_blk = lambda imap: pl.BlockSpec((1, 1, CMB_TILE * TOP_K), imap, memory_space=pltpu.SMEM)
    return pl.pallas_call(
        functools.partial(_combine_kernel, npt),
        grid=(nt,),
        in_specs=[smem_blk(lambda i: (i, 0, 0)), smem_blk(lambda i: (jnp.minimum(i + 1, nt - 1), 0, 0)),
                  pl.BlockSpec((CMB_TILE, D_MODEL), lambda i: (i, 0)),
                  pl.BlockSpec((CMB_TILE, LANES), lambda i: (i, 0)),
                  pl.BlockSpec(memory_space=pl.ANY)],
        out_specs=(pl.BlockSpec((CMB_TILE, D_MODEL), lambda i: (jnp.minimum(i, npt - 1), 0)),
                   pl.BlockSpec((CMB_TILE, D_MODEL), lambda i: (jnp.maximum(i - npt, 0), 0))),
        out_shape=(jax.ShapeDtypeStruct((n_prompt_rows, D_MODEL), F32),
                   jax.ShapeDtypeStruct((T - n_prompt_rows, D_MODEL), F32)),
        scratch_shapes=[pltpu.VMEM((2, TOP_K, CMB_TILE * SUBLANES, LANES), F32), pltpu.SemaphoreType.DMA((2,))],
        compiler_params=pltpu.CompilerParams(dimension_semantics=("arbitrary",), vmem_limit_bytes=VMEM_LIMIT),
        name="moe_combine",
    )(dest_tiles, dest_tiles, h, route, ys)


def kernel(x_prompt, x_sample, cache_k_win, cache_v_win, state_conv, state_C, state_n, state_m, g_attn, w_in, b_i,
           b_f, q_norm_g, k_norm_g, sinks, conv_w, conv_b, m_norm_g, w_out, g_ffn, w_router, b_router, w_gate_up,
           b_gate_up, w_down, b_down):
    depth = g_attn.shape[0]
    assert depth == 1
    B, S, _ = x_prompt.shape
    DB, DS, _ = x_sample.shape
    TP = B * S
    TS = DB * DS
    T = TP + TS
    assert T % TOK_TILE == 0 and TP % TOK_TILE == 0 and S % ATT_QB == 0 and S % PROMPT_CHUNK == 0
    assert DS == SUBLANES and DB % SAMPLE_NB == 0 and (SAMPLE_NB * DS) == LANES
    l = 0

    xp = x_prompt.reshape(TP, D_MODEL)
    xs = x_sample.reshape(TS, D_MODEL)

    w_pad = jnp.pad(w_in[l], ((0, 0), (0, IN_PAD - w_in.shape[2]))).astype(BF16)
    wgt = jnp.transpose(w_in[l][:, GATE_COL:GATE_COL + 2 * M_HEADS]).astype(BF16)
    gi = jnp.arange(A_WIDTH) // HEAD_DIM
    gmat = (gi[:, None] == gi[None, :]).astype(BF16)
    qg = jnp.tile(q_norm_g[l], A_HEADS).reshape(1, A_WIDTH)
    kg = jnp.tile(k_norm_g[l], A_KV_HEADS).reshape(1, KV_WIDTH)
    gbias = jnp.concatenate([b_i[l], b_f[l]])
    gbrow = jnp.pad(gbias, (0, LANES - 2 * M_HEADS)).reshape(1, LANES)
    gbcol = gbias.reshape(2 * M_HEADS, 1)
    mng = m_norm_g[l].reshape(1, M_WIDTH)
    cw = conv_w[l]
    cb = conv_b[l].reshape(1, 2 * M_WIDTH)
    wr_pad = jnp.pad(w_router[l], ((0, 0), (0, LANES - N_EXPERTS))).astype(BF16)
    br_pad = jnp.concatenate([b_router[l], jnp.full((LANES - N_EXPERTS,), NEG, F32)]).reshape(1, LANES)

    proj_w = (g_attn[l].reshape(1, D_MODEL), w_pad, wgt, gmat, qg, kg)
    qn, kn, va, qkm, vm, om, gcol, gt = _inproj(xp, *proj_w)
    qn_s, kn_s, va_s, qkm_s, vm_s, om_s, gcol_s, gt_s = _inproj(xs, *proj_w)

    a_p = _attn_prompt(sinks[l], qn, kn, va, B, S)
    ck = cache_k_win[l].reshape(DB, WINDOW, KV_WIDTH)
    cv = cache_v_win[l].reshape(DB, WINDOW, KV_WIDTH)
    a_s, kwin_s, vwin_s = _attn_sample(sinks[l], qn_s, kn_s, va_s, ck, cv, DB, DS)

    zc = jnp.zeros((B, SUBLANES, 2 * M_WIDTH), F32)
    m_p, C_p, n_p, mm_p = _mlstm(
        qkm, vm, om, gcol, gt, zc,
        jnp.zeros((B, M_HEADS, M_HEAD_DIM, M_HEAD_DIM), F32), jnp.zeros((B, M_HEADS, M_HEAD_DIM), F32),
        jnp.full((B, 1, LANES), NEG, F32), cw, cb, gbrow, gbcol, mng,
        nseq=1, L=PROMPT_CHUNK, ngroups=B, nchunks=S // PROMPT_CHUNK)
    conv_s0 = jnp.pad(state_conv[l], ((0, 0), (SUBLANES - (CONV_W - 1), 0), (0, 0)))
    m0_s = jnp.pad(state_m[l], ((0, 0), (0, LANES - M_HEADS))).reshape(DB, 1, LANES)
    m_s, C_s, n_s, mm_s = _mlstm(
        qkm_s, vm_s, om_s, gcol_s, gt_s, conv_s0, state_C[l], state_n[l], m0_s, cw, cb, gbrow, gbcol, mng,
        nseq=SAMPLE_NB, L=DS, ngroups=DB // SAMPLE_NB, nchunks=1)

    h, hn, route, cnt = _outproj(a_p, a_s, m_p, m_s, xp, xs, w_out[l].astype(BF16), g_ffn[l].reshape(1, D_MODEL),
                                 wr_pad, br_pad)

    nblk = T * TOP_K // MOE_BLOCK + N_EXPERTS
    dest, blk, info = _route_tables(route, cnt, nblk)
    block_e = blk[:, 0]
    padrow = info[0, :N_EXPERTS]
    npad = info[1, :N_EXPERTS]
    nused = info[2, N_EXPERTS - 1:N_EXPERTS]
    dest4 = dest[:, :TOP_K]
    xrows = _dispatch(padrow, npad, nused, dest4.reshape(T // DSP_TILE, 1, DSP_TILE * TOP_K), hn, nblk * MOE_BLOCK)
    yrows = _moe_blocks(block_e, nused, xrows,
                        w_gate_up[l], b_gate_up[l].reshape(N_EXPERTS, 1, 2 * D_FF),
                        w_down[l], b_down[l].reshape(N_EXPERTS, 1, D_MODEL))
    y_p, y_s = _combine(dest4.reshape(T // CMB_TILE, 1, CMB_TILE * TOP_K), h, route, yrows, TP)

    y_p = y_p.reshape(B, S, D_MODEL)
    y_s = y_s.reshape(DB, DS, D_MODEL)
    def seq_tail(rows, n):
        return jnp.stack([rows[(b + 1) * S - n:(b + 1) * S] for b in range(B)])

    kwin_p = seq_tail(kn, WINDOW).reshape(B, WINDOW, A_KV_HEADS, HEAD_DIM)
    vwin_p = seq_tail(va, WINDOW).reshape(B, WINDOW, A_KV_HEADS, HEAD_DIM)
    qkm_s = qkm_s.reshape(DB, DS, 2 * M_WIDTH)
    return (y_p, y_s,
            kwin_p[None], vwin_p[None], seq_tail(qkm, CONV_W - 1)[None],
            C_p[None], n_p[None], mm_p[:, 0, :M_HEADS][None],
            kwin_s.reshape(DB, WINDOW, A_KV_HEADS, HEAD_DIM)[None],
            vwin_s.reshape(DB, WINDOW, A_KV_HEADS, HEAD_DIM)[None],
            qkm_s[:, -(CONV_W - 1):][None],
            C_s[None], n_s[None], mm_s[:, 0, :M_HEADS][None])
```

```python
import functools

import jax
import jax.numpy as jnp
from jax import lax
from jax.experimental import pallas as pl
from jax.experimental.pallas import tpu as pltpu

F32 = jnp.float32
BF16 = jnp.bfloat16

D_MODEL = 1024
HEAD_DIM = 64
A_HEADS = 8
A_KV_HEADS = 2
A_GROUP = A_HEADS // A_KV_HEADS
A_WIDTH = A_HEADS * HEAD_DIM
KV_WIDTH = A_KV_HEADS * HEAD_DIM
WINDOW = 128
M_HEADS = 4
M_HEAD_DIM = 128
M_WIDTH = M_HEADS * M_HEAD_DIM
CONV_W = 4
N_EXPERTS = 32
TOP_K = 4
D_FF = D_MODEL
SWIGLU_LIMIT = 7.0
SWIGLU_ALPHA = 1.702
MOE_BLOCK = 512
EPS = 1e-6
NEG = -1e30

LANES = 128
SUBLANES = 8
GATE_COL = A_WIDTH + 2 * KV_WIDTH + 4 * M_WIDTH
IN_PAD = GATE_COL + LANES
TOK_TILE = 512
PROJ_TILE = 1024
PROJ_VMEM_LIMIT = 58 * 1024 * 1024
ATT_QB = 512
ATT_SB = 128
SAMPLE_NB = 16
PROMPT_CHUNK = 256
CMB_TILE = 256
DSP_TILE = 1024
VMEM_LIMIT = 48 * 1024 * 1024
MOE_VMEM_LIMIT = 56 * 1024 * 1024


def _dot(a, b):
    return jnp.dot(a, b, preferred_element_type=F32)


def _dot_nt(a, b):
    return lax.dot_general(a, b, (((1,), (1,)), ((), ())), preferred_element_type=F32)


def _dot_tn(a, b):
    return lax.dot_general(a, b, (((0,), (0,)), ((), ())), preferred_element_type=F32)


def _split3(x):
    hi = x.astype(BF16)
    r1 = x - hi.astype(F32)
    mid = r1.astype(BF16)
    lo = (r1 - mid.astype(F32)).astype(BF16)
    return hi, mid, lo


def _log_sigmoid(x):
    return jnp.minimum(x, 0.0) - jnp.log1p(jnp.exp(-jnp.abs(x)))


def _load_row_tiles(ref2, rows):
    return jnp.concatenate([ref2[pl.ds(s, rows, stride=SUBLANES), :] for s in range(SUBLANES)], axis=1)


def _store_row_tiles(ref2, rows, val):
    for s in range(SUBLANES):
        ref2[pl.ds(s, rows, stride=SUBLANES), :] = val[:, s * LANES:(s + 1) * LANES]


def _row_tile(ref2, idx):
    return ref2.at[pl.ds(pl.multiple_of(idx * SUBLANES, SUBLANES), SUBLANES), :]


def _row_tiles(ref2, first, n):
    return ref2.at[pl.ds(pl.multiple_of(first * SUBLANES, SUBLANES), n * SUBLANES), :]


def _split_specs(n_prompt_tiles, rows, width):
    return (pl.BlockSpec((rows, width), lambda i, *_: (jnp.minimum(i, n_prompt_tiles - 1), 0)),
            pl.BlockSpec((rows, width), lambda i, *_: (jnp.maximum(i - n_prompt_tiles, 0), 0)))


def _inproj_kernel(x_ref, g_ref, w_ref, wgt_ref, gmat_ref, qg_ref, kg_ref,
                   qn_ref, kn_ref, va_ref, qkm_ref, vm_ref, om_ref, gcol_ref, gt_ref):
    x = x_ref[...]
    ms = jnp.mean(x * x, axis=-1, keepdims=True)
    xn = ((x * lax.rsqrt(ms + EPS)) * g_ref[...]).astype(BF16)

    def seg(lo, hi):
        return _dot(xn, w_ref[:, lo:hi])

    def head_norm(z, gmat, g):
        hi, mid, lo = _split3(z * z)
        ss = _dot(hi, gmat) + _dot(mid, gmat) + _dot(lo, gmat)
        return (z * lax.rsqrt(ss * (1.0 / HEAD_DIM) + EPS)) * g

    o0 = A_WIDTH
    o1 = o0 + KV_WIDTH
    o2 = o1 + KV_WIDTH
    o3 = o2 + 2 * M_WIDTH
    o4 = o3 + M_WIDTH
    o5 = o4 + M_WIDTH
    qn_ref[...] = head_norm(seg(0, o0), gmat_ref[...], qg_ref[...])
    kn_ref[...] = head_norm(seg(o0, o1), gmat_ref[:KV_WIDTH, :KV_WIDTH], kg_ref[...])
    va_ref[...] = seg(o1, o2)
    qkm_ref[...] = seg(o2, o3)
    vm_ref[...] = seg(o3, o4)
    om_ref[...] = seg(o4, o5)
    gcol_ref[...] = seg(o5, o5 + LANES)
    gt_ref[0] = _dot_nt(wgt_ref[...], xn)


def _inproj(x, g_attn, w_pad, wgt, gmat, qg, kg):
    T = x.shape[0]
    nt = T // PROJ_TILE
    row = lambda w: pl.BlockSpec((PROJ_TILE, w), lambda i: (i, 0))
    full = lambda a: pl.BlockSpec(a.shape, lambda i: (0,) * a.ndim, pipeline_mode=pl.Buffered(1))
    out_shape = (
        jax.ShapeDtypeStruct((T, A_WIDTH), F32),
        jax.ShapeDtypeStruct((T, KV_WIDTH), F32),
        jax.ShapeDtypeStruct((T, KV_WIDTH), F32),
        jax.ShapeDtypeStruct((T, 2 * M_WIDTH), F32),
        jax.ShapeDtypeStruct((T, M_WIDTH), F32),
        jax.ShapeDtypeStruct((T, M_WIDTH), F32),
        jax.ShapeDtypeStruct((T, LANES), F32),
        jax.ShapeDtypeStruct((nt, SUBLANES, PROJ_TILE), F32),
    )
    out_specs = (row(A_WIDTH), row(KV_WIDTH), row(KV_WIDTH), row(2 * M_WIDTH), row(M_WIDTH), row(M_WIDTH),
                 row(LANES), pl.BlockSpec((1, SUBLANES, PROJ_TILE), lambda i: (i, 0, 0)))
    return pl.pallas_call(
        _inproj_kernel,
        grid=(nt,),
        in_specs=[row(D_MODEL), full(g_attn), full(w_pad), full(wgt), full(gmat), full(qg), full(kg)],
        out_specs=out_specs,
        out_shape=out_shape,
        compiler_params=pltpu.CompilerParams(dimension_semantics=("parallel",), vmem_limit_bytes=PROJ_VMEM_LIMIT),
        name="inproj",
    )(x, g_attn, w_pad, wgt, gmat, qg, kg)


def _softmax_sink(pieces, masks, sink_col):
    masked = [jnp.where(mk, s, NEG) for s, mk in zip(pieces, masks)]
    m = sink_col
    for s in masked:
        m = jnp.maximum(m, jnp.max(s, axis=-1, keepdims=True))
    ps = [jnp.exp(s - m) for s in masked]
    den = jnp.exp(sink_col - m)
    for p in ps:
        den = den + jnp.sum(p, axis=-1, keepdims=True)
    return ps, 1.0 / den


def _stack_heads(q, g):
    return jnp.concatenate([q[:, (A_GROUP * g + i) * HEAD_DIM:(A_GROUP * g + i + 1) * HEAD_DIM]
                            for i in range(A_GROUP)], axis=0)


def _sink_col(sink_ref, g, rows_per_head):
    r = lax.broadcasted_iota(jnp.int32, (A_GROUP * rows_per_head, 1), 0)
    col = jnp.zeros((A_GROUP * rows_per_head, 1), F32)
    for i in range(A_GROUP):
        col = jnp.where(r // rows_per_head == i, sink_ref[A_GROUP * g + i], col)
    return col


def _attn_prompt_kernel(sink_ref, q_ref, kp_ref, kc_ref, vp_ref, vc_ref, o_ref):
    j = pl.program_id(1)
    scale = HEAD_DIM ** -0.5
    kall = jnp.concatenate([kp_ref[...], kc_ref[...]], axis=0).astype(BF16)
    vall = jnp.concatenate([vp_ref[...], vc_ref[...]], axis=0).astype(BF16)
    nrow = A_GROUP * ATT_SB
    r = lax.broadcasted_iota(jnp.int32, (nrow, 2 * ATT_SB), 0) % ATT_SB
    c = lax.broadcasted_iota(jnp.int32, (nrow, 2 * ATT_SB), 1)
    band = jnp.logical_and(c >= r, c <= r + WINDOW)
    band0 = jnp.logical_and(band, jnp.logical_or(c >= ATT_SB, j > 0))
    for sb in range(ATT_QB // ATT_SB):
        q = (q_ref[sb * ATT_SB:(sb + 1) * ATT_SB, :] * scale).astype(BF16)
        kwin = kall[sb * ATT_SB:(sb + 2) * ATT_SB]
        vwin = vall[sb * ATT_SB:(sb + 2) * ATT_SB]
        outs = []
        for g in range(A_KV_HEADS):
            lo, hi = g * HEAD_DIM, (g + 1) * HEAD_DIM
            s = _dot_nt(_stack_heads(q, g), kwin[:, lo:hi])
            (p,), inv = _softmax_sink([s], [band0 if sb == 0 else band], _sink_col(sink_ref, g, ATT_SB))
            o = _dot(p.astype(BF16), vwin[:, lo:hi]) * inv
            outs += [o[i * ATT_SB:(i + 1) * ATT_SB] for i in range(A_GROUP)]
        o_ref[sb * ATT_SB:(sb + 1) * ATT_SB, :] = jnp.concatenate(outs, axis=1)


def _attn_prompt(sinks, qn, kn, va, batch, seq):
    nq = seq // ATT_QB
    ratio = ATT_QB // ATT_SB
    cur = lambda w: pl.BlockSpec((ATT_QB, w), lambda b, j: (b * nq + j, 0))
    prev = lambda w: pl.BlockSpec((ATT_SB, w), lambda b, j: (jnp.maximum((b * nq + j) * ratio - 1, 0), 0))
    return pl.pallas_call(
        _attn_prompt_kernel,
        grid=(batch, nq),
        in_specs=[pl.BlockSpec(memory_space=pltpu.SMEM), cur(A_WIDTH), prev(KV_WIDTH), cur(KV_WIDTH),
                  prev(KV_WIDTH), cur(KV_WIDTH)],
        out_specs=cur(A_WIDTH),
        out_shape=jax.ShapeDtypeStruct((batch * seq, A_WIDTH), F32),
        compiler_params=pltpu.CompilerParams(dimension_semantics=("parallel", "parallel"),
                                             vmem_limit_bytes=VMEM_LIMIT),
        name="attn_prompt",
    )(sinks, qn, kn, kn, va, va)


def _attn_sample_kernel(dec, sink_ref, q_ref, kn_ref, vn_ref, ck_ref, cv_ref, o_ref, kw_ref, vw_ref):
    scale = HEAD_DIM ** -0.5
    rows = SAMPLE_NB * dec
    knew = kn_ref[...]
    vnew = vn_ref[...]
    knew_b = knew.astype(BF16)
    vnew_b = vnew.astype(BF16)
    nrow = A_GROUP * dec
    t = lax.broadcasted_iota(jnp.int32, (nrow, WINDOW), 0) % dec
    c = lax.broadcasted_iota(jnp.int32, (nrow, WINDOW), 1)
    m_cache = c >= t
    cn = lax.broadcasted_iota(jnp.int32, (nrow, rows), 1)
    tn = lax.broadcasted_iota(jnp.int32, (nrow, rows), 0) % dec
    for i in range(SAMPLE_NB):
        q = (q_ref[i * dec:(i + 1) * dec, :] * scale).astype(BF16)
        ck = ck_ref[i].astype(BF16)
        cv = cv_ref[i].astype(BF16)
        m_new = jnp.logical_and(cn // dec == i, cn % dec <= tn)
        outs = []
        for g in range(A_KV_HEADS):
            lo, hi = g * HEAD_DIM, (g + 1) * HEAD_DIM
            qs = _stack_heads(q, g)
            s_c = _dot_nt(qs, ck[:, lo:hi])
            s_n = _dot_nt(qs, knew_b[:, lo:hi])
            (p_c, p_n), inv = _softmax_sink([s_c, s_n], [m_cache, m_new], _sink_col(sink_ref, g, dec))
            o = (_dot(p_c.astype(BF16), cv[:, lo:hi]) + _dot(p_n.astype(BF16), vnew_b[:, lo:hi])) * inv
            outs += [o[h * dec:(h + 1) * dec] for h in range(A_GROUP)]
        o_ref[i * dec:(i + 1) * dec, :] = jnp.concatenate(outs, axis=1)
        kw_ref[i, 0:WINDOW - dec, :] = ck_ref[i, dec:WINDOW, :]
        kw_ref[i, WINDOW - dec:WINDOW, :] = knew[i * dec:(i + 1) * dec]
        vw_ref[i, 0:WINDOW - dec, :] = cv_ref[i, dec:WINDOW, :]
        vw_ref[i, WINDOW - dec:WINDOW, :] = vnew[i * dec:(i + 1) * dec]


def _attn_sample(sinks, qn, kn, va, ck, cv, dbatch, dec):
    rows = SAMPLE_NB * dec
    tokrow = lambda w: pl.BlockSpec((rows, w), lambda i: (i, 0))
    cache = pl.BlockSpec((SAMPLE_NB, WINDOW, KV_WIDTH), lambda i: (i, 0, 0))
    return pl.pallas_call(
        functools.partial(_attn_sample_kernel, dec),
        grid=(dbatch // SAMPLE_NB,),
        in_specs=[pl.BlockSpec(memory_space=pltpu.SMEM), tokrow(A_WIDTH), tokrow(KV_WIDTH), tokrow(KV_WIDTH),
                  cache, cache],
        out_specs=(pl.BlockSpec((rows, A_WIDTH), lambda i: (i, 0)), cache, cache),
        out_shape=(jax.ShapeDtypeStruct((dbatch * dec, A_WIDTH), F32),
                   jax.ShapeDtypeStruct((dbatch, WINDOW, KV_WIDTH), F32),
                   jax.ShapeDtypeStruct((dbatch, WINDOW, KV_WIDTH), F32)),
        compiler_params=pltpu.CompilerParams(dimension_semantics=("parallel",), vmem_limit_bytes=VMEM_LIMIT),
        name="attn_sample",
    )(sinks, qn, kn, va, ck, cv)


def _mlstm_kernel(nseq, L, qk_ref, v_ref, o_ref, gcol_ref, gt_ref, conv0_ref, c0_ref, n0_ref, m0_ref,
                  cw_ref, cb_ref, gbrow_ref, gbcol_ref, mng_ref, mask_ref,
                  out_ref, cst_ref, nst_ref, mst_ref, prev_ref):
    R = nseq * L
    ci = pl.program_id(1)

    @pl.when(ci == 0)
    def _():
        cst_ref[...] = c0_ref[...]
        nst_ref[...] = n0_ref[...]
        mst_ref[...] = m0_ref[...]
        prev_ref[...] = conv0_ref[0]

    raw = qk_ref[...]
    row = lax.broadcasted_iota(jnp.int32, (R, 1), 0)
    tpos = row % L
    rseq = row // L
    acc = raw * cw_ref[CONV_W - 1:CONV_W, :] + cb_ref[...]
    if nseq == 1:
        prev8 = prev_ref[...]
        t8 = lax.broadcasted_iota(jnp.int32, (SUBLANES, 1), 0)
    else:
        prevsrc = conv0_ref[...].reshape(R, 2 * M_WIDTH)
    for k in range(1, CONV_W):
        rolled = pltpu.roll(raw, k, 0)
        if nseq == 1:
            head = jnp.where(t8 >= k, rolled[0:SUBLANES], pltpu.roll(prev8, k, 0))
            sh = jnp.concatenate([head, rolled[SUBLANES:]], axis=0)
        else:
            sh = jnp.where(tpos >= k, rolled, pltpu.roll(prevsrc, R - SUBLANES + k, 0))
        acc = acc + sh * cw_ref[CONV_W - 1 - k:CONV_W - k, :]
    if nseq == 1:
        prev_ref[...] = raw[R - SUBLANES:R]
    qkc = acc * jax.nn.sigmoid(acc)

    gc = gcol_ref[...] + gbrow_ref[...]
    gr = gt_ref[0] + gbcol_ref[...]
    lsc = _log_sigmoid(gc)
    lsr = _log_sigmoid(gr)
    mb = mask_ref[...]
    maskb = mb > 0
    bcol = sum(_dot(mb, p) for p in _split3(lsc))
    brow = sum(_dot_nt(p, mb) for p in _split3(lsr))

    lane = lax.broadcasted_iota(jnp.int32, (1, LANES), 1)
    m_new = [jnp.zeros((1, LANES), F32) for _ in range(nseq)]
    for h in range(M_HEADS):
        sl = slice(h * M_HEAD_DIM, (h + 1) * M_HEAD_DIM)
        qh = qkc[:, sl]
        kh = qkc[:, M_WIDTH + h * M_HEAD_DIM:M_WIDTH + (h + 1) * M_HEAD_DIM] * (M_HEAD_DIM ** -0.5)
        vh = v_ref[:, sl]
        qb, kb, vb = qh.astype(BF16), kh.astype(BF16), vh.astype(BF16)
        ig_c = gc[:, h:h + 1]
        b_c = bcol[:, M_HEADS + h:M_HEADS + h + 1]
        ig_r = gr[h:h + 1, :]
        b_r = brow[M_HEADS + h:M_HEADS + h + 1, :]
        if nseq == 1:
            m0c = mst_ref[0][:, h:h + 1]
            n0rows = nst_ref[0, h:h + 1, :]
        else:
            m0c = jnp.zeros((R, 1), F32)
            n0rows = jnp.zeros((R, M_HEAD_DIM), F32)
            for s in range(nseq):
                m0c = jnp.where(rseq == s, mst_ref[s][:, h:h + 1], m0c)
                n0rows = jnp.where(rseq == s, nst_ref[s, h:h + 1, :], n0rows)
        dm = jnp.where(maskb, b_c - b_r + ig_r, NEG)
        a_c = b_c + m0c
        m_c = jnp.maximum(a_c, jnp.max(dm, axis=-1, keepdims=True))
        w = jnp.exp(dm - m_c)
        sc = jnp.exp(a_c - m_c)
        wqk = w * _dot_nt(qb, kb)
        if nseq == 1:
            inter = _dot_nt(qb, cst_ref[0, h].astype(BF16))
        else:
            inter = jnp.zeros((R, M_HEAD_DIM), F32)
            for s in range(nseq):
                qs = jnp.where(rseq == s, qh, 0.0).astype(BF16)
                inter = inter + _dot_nt(qs, cst_ref[s, h].astype(BF16))
        num = _dot(wqk.astype(BF16), vb) + sc * inter
        den = jnp.sum(wqk, axis=-1, keepdims=True) + sc * jnp.sum(qh * n0rows, axis=-1, keepdims=True)
        hh = num / jnp.maximum(jnp.abs(den), jnp.exp(-m_c))

        for s in range(nseq):
            e = s * L + L - 1
            m_end = m_c[e:e + 1, :]
            wend = jnp.exp(b_c[e:e + 1, :] - b_c + ig_c - m_end)
            if nseq > 1:
                wend = jnp.where(rseq == s, wend, 0.0)
            sce = jnp.exp(a_c[e:e + 1, :] - m_end)
            c_new = sce * cst_ref[s, h] + _dot_tn((vh * wend).astype(BF16), kb)
            n_new = sce * nst_ref[s, h:h + 1, :] + jnp.sum(wend * kh, axis=0, keepdims=True)
            cst_ref[s, h] = c_new
            nst_ref[s, h:h + 1, :] = n_new
            m_new[s] = jnp.where(lane == h, m_end, m_new[s])

        hn = (hh * lax.rsqrt(jnp.mean(hh * hh, axis=-1, keepdims=True) + EPS)) * mng_ref[:, sl]
        out_ref[:, sl] = jax.nn.sigmoid(o_ref[:, sl]) * hn
    for s in range(nseq):
        mst_ref[s] = m_new[s]


def _mlstm(qkm, vm, om, gcol, gt, conv0, c0, n0, m0, cw, cb, gbrow, gbcol, mng, nseq, L, ngroups, nchunks):
    R = nseq * L
    per_tile = gt.shape[2] // R
    tok = lambda w: pl.BlockSpec((R, w), lambda g, c: (g * nchunks + c, 0))
    gt_spec = pl.BlockSpec((1, SUBLANES, R),
                           lambda g, c: ((g * nchunks + c) // per_tile, 0, (g * nchunks + c) % per_tile))
    full = lambda a: pl.BlockSpec(a.shape, lambda g, c: (0,) * a.ndim)
    st4 = pl.BlockSpec((nseq, M_HEADS, M_HEAD_DIM, M_HEAD_DIM), lambda g, c: (g, 0, 0, 0))
    st3 = pl.BlockSpec((nseq, M_HEADS, M_HEAD_DIM), lambda g, c: (g, 0, 0))
    stm = pl.BlockSpec((nseq, 1, LANES), lambda g, c: (g, 0, 0))
    conv_spec = pl.BlockSpec((nseq, SUBLANES, 2 * M_WIDTH), lambda g, c: (g, 0, 0))
    r = jnp.arange(R)
    mask = ((r[:, None] // L == r[None, :] // L) & (r[None, :] <= r[:, None])).astype(BF16)
    nstate = ngroups * nseq
    return pl.pallas_call(
        functools.partial(_mlstm_kernel, nseq, L),
        grid=(ngroups, nchunks),
        in_specs=[tok(2 * M_WIDTH), tok(M_WIDTH), tok(M_WIDTH), tok(LANES), gt_spec, conv_spec, st4, st3, stm,
                  full(cw), full(cb), full(gbrow), full(gbcol), full(mng), full(mask)],
        out_specs=(pl.BlockSpec((R, M_WIDTH), lambda g, c: (g * nchunks + c, 0)), st4, st3, stm),
        out_shape=(jax.ShapeDtypeStruct((ngroups * nchunks * R, M_WIDTH), F32),
                   jax.ShapeDtypeStruct((nstate, M_HEADS, M_HEAD_DIM, M_HEAD_DIM), F32),
                   jax.ShapeDtypeStruct((nstate, M_HEADS, M_HEAD_DIM), F32),
                   jax.ShapeDtypeStruct((nstate, 1, LANES), F32)),
        scratch_shapes=[pltpu.VMEM((SUBLANES, 2 * M_WIDTH), F32)],
        compiler_params=pltpu.CompilerParams(dimension_semantics=("parallel", "arbitrary"),
                                             vmem_limit_bytes=VMEM_LIMIT),
        name="mlstm_n%d" % nseq,
    )(qkm, vm, om, gcol, gt, conv0, c0, n0, m0, cw, cb, gbrow, gbcol, mng, mask)


def _outproj_kernel(npt, ap_ref, as_ref, mp_ref, ms_ref, xp_ref, xs_ref, wo_ref, g_ref, wrt_ref, brc_ref,
                    h_ref, hn_ref, route_ref, cnt_ref):
    def project(a_ref, m_ref, x_ref):
        h_ref[...] = (x_ref[...] + _dot(a_ref[...].astype(BF16), wo_ref[0:A_WIDTH, :])
                      + _dot(m_ref[...].astype(BF16), wo_ref[A_WIDTH:A_WIDTH + M_WIDTH, :]))

    @pl.when(pl.program_id(0) < npt)
    def _():
        project(ap_ref, mp_ref, xp_ref)

    @pl.when(pl.program_id(0) >= npt)
    def _():
        project(as_ref, ms_ref, xs_ref)

    h = h_ref[...]
    hn = (h * lax.rsqrt(jnp.mean(h * h, axis=-1, keepdims=True) + EPS)) * g_ref[...]
    _store_row_tiles(hn_ref, TOK_TILE, hn)
    logits = _dot_nt(wrt_ref[...], hn.astype(BF16)) + brc_ref[...]
    eidx = lax.broadcasted_iota(jnp.int32, logits.shape, 0).astype(F32)
    picked = jnp.zeros(logits.shape, F32)
    top0 = None
    den = None
    es = []
    ids = []
    for k in range(TOP_K):
        mx = jnp.max(logits, axis=0, keepdims=True)
        idx = jnp.min(jnp.where(logits == mx, eidx, float(N_EXPERTS)), axis=0, keepdims=True)
        if k == 0:
            top0 = mx
        e = jnp.exp(mx - top0)
        den = e if den is None else den + e
        es.append(e)
        ids.append(idx)
        hit = eidx == idx
        picked = jnp.where(hit, 1.0, picked)
        logits = jnp.where(hit, -jnp.inf, logits)
    route_ref[0] = jnp.concatenate([e / den for e in es] + ids, axis=0)

    @pl.when(pl.program_id(0) == 0)
    def _():
        cnt_ref[...] = jnp.zeros_like(cnt_ref)

    cnt_ref[...] += jnp.broadcast_to(jnp.sum(picked, axis=1, keepdims=True), cnt_ref.shape)


def _outproj(a_p, a_s, m_p, m_s, xp, xs, w_out, g_ffn, wr_t, br_col):
    T = xp.shape[0] + xs.shape[0]
    nt = T // TOK_TILE
    npt = xp.shape[0] // TOK_TILE
    row = lambda w: pl.BlockSpec((TOK_TILE, w), lambda i: (i, 0))
    full = lambda a: pl.BlockSpec(a.shape, lambda i: (0,) * a.ndim)
    return pl.pallas_call(
        functools.partial(_outproj_kernel, npt),
        grid=(nt,),
        in_specs=[*_split_specs(npt, TOK_TILE, A_WIDTH), *_split_specs(npt, TOK_TILE, M_WIDTH),
                  *_split_specs(npt, TOK_TILE, D_MODEL), full(w_out), full(g_ffn), full(wr_t), full(br_col)],
        out_specs=(row(D_MODEL), pl.BlockSpec((TOK_TILE * SUBLANES, LANES), lambda i: (i, 0)),
                   pl.BlockSpec((1, 2 * TOP_K, TOK_TILE), lambda i: (i, 0, 0)),
                   pl.BlockSpec((N_EXPERTS, LANES), lambda i: (0, 0))),
        out_shape=(jax.ShapeDtypeStruct((T, D_MODEL), F32), jax.ShapeDtypeStruct((T * SUBLANES, LANES), F32),
                   jax.ShapeDtypeStruct((nt, 2 * TOP_K, TOK_TILE), F32),
                   jax.ShapeDtypeStruct((N_EXPERTS, LANES), F32)),
        compiler_params=pltpu.CompilerParams(dimension_semantics=("arbitrary",), vmem_limit_bytes=VMEM_LIMIT),
        name="outproj_router",
    )(a_p, a_s, m_p, m_s, xp, xs, w_out, g_ffn, wr_t, br_col)


def _route_kernel(nblk_pad, route_ref, cnt_ref, ustrict_ref, lstrict_ref, dest_ref, blk_ref, info_ref, carry_ref):
    i = pl.program_id(0)
    cnt = cnt_ref[...]
    nb_e = jnp.floor((cnt + (MOE_BLOCK - 1.0)) * (1.0 / MOE_BLOCK))
    bstart = sum(_dot(lstrict_ref[...], p) for p in _split3(nb_e))
    bend = bstart + nb_e
    row_start = bstart * float(MOE_BLOCK)

    @pl.when(i == 0)
    def _():
        carry_ref[...] = jnp.zeros_like(carry_ref)
        bi = lax.broadcasted_iota(jnp.int32, (N_EXPERTS, nblk_pad), 1).astype(F32)
        done = jnp.where(bend[:, 0:1] <= bi, 1.0, 0.0)
        be = jnp.minimum(jnp.sum(done, axis=0, keepdims=True), N_EXPERTS - 1.0)
        blk_ref[...] = jnp.broadcast_to(be, blk_ref.shape).astype(jnp.int32)
        lane = lax.broadcasted_iota(jnp.int32, (N_EXPERTS, LANES), 1)
        info = jnp.where(lane == 0, row_start + cnt, 0.0)
        info = jnp.where(lane == 1, nb_e * float(MOE_BLOCK) - cnt, info)
        info = jnp.where(lane == 2, bend, info)
        info_ref[...] = info.astype(jnp.int32)

    r = route_ref[0]
    eidx = lax.broadcasted_iota(jnp.int32, (N_EXPERTS, TOK_TILE), 0).astype(F32)
    sel = [eidx == r[TOP_K + k:TOP_K + k + 1, :] for k in range(TOP_K)]
    oh = jnp.zeros((N_EXPERTS, TOK_TILE), F32)
    for k in range(TOP_K):
        oh = jnp.where(sel[k], 1.0, oh)
    before = _dot(oh.astype(BF16), ustrict_ref[...]) + carry_ref[:, 0:1] + row_start[:, 0:1]
    rows = [jnp.sum(jnp.where(sel[k], before, 0.0), axis=0, keepdims=True) for k in range(TOP_K)]
    dest_ref[0] = jnp.concatenate(rows + [jnp.zeros((TOP_K, TOK_TILE), F32)], axis=0).astype(jnp.int32)
    carry_ref[...] += jnp.broadcast_to(jnp.sum(oh, axis=1, keepdims=True), carry_ref.shape)


def _route_tables(route, cnt, nblk):
    nt = route.shape[0]
    nblk_pad = -(-nblk // LANES) * LANES
    a = jnp.arange(TOK_TILE)
    ustrict = (a[:, None] < a[None, :]).astype(BF16)
    b = jnp.arange(N_EXPERTS)
    lstrict = (b[:, None] > b[None, :]).astype(BF16)
    full = lambda x: pl.BlockSpec(x.shape, lambda i: (0,) * x.ndim)
    tile = pl.BlockSpec((1, 2 * TOP_K, TOK_TILE), lambda i: (i, 0, 0))
    return pl.pallas_call(
        functools.partial(_route_kernel, nblk_pad),
        grid=(nt,),
        in_specs=[tile, full(cnt), full(ustrict), full(lstrict)],
        out_specs=(tile, pl.BlockSpec((SUBLANES, nblk_pad), lambda i: (0, 0)),
                   pl.BlockSpec((N_EXPERTS, LANES), lambda i: (0, 0))),
        out_shape=(jax.ShapeDtypeStruct((nt, 2 * TOP_K, TOK_TILE), jnp.int32),
                   jax.ShapeDtypeStruct((SUBLANES, nblk_pad), jnp.int32),
                   jax.ShapeDtypeStruct((N_EXPERTS, LANES), jnp.int32)),
        scratch_shapes=[pltpu.VMEM((N_EXPERTS, LANES), F32)],
        compiler_params=pltpu.CompilerParams(dimension_semantics=("arbitrary",), vmem_limit_bytes=VMEM_LIMIT),
        name="route_tables",
    )(route, cnt, ustrict, lstrict)


def _dispatch_kernel(padrow_ref, npad_ref, nu_ref, dest_ref, hn_ref, xs_hbm, zbuf, sem, zsem):
    i = pl.program_id(0)

    @pl.when(i == 0)
    def _():
        zbuf[...] = jnp.zeros_like(zbuf)
        nblk = xs_hbm.shape[0] // (MOE_BLOCK * SUBLANES)

        def tail_start(b, c):
            pltpu.make_async_copy(zbuf, _row_tiles(xs_hbm, b * MOE_BLOCK, MOE_BLOCK), zsem).start()
            return c

        def tail_wait(b, c):
            pltpu.make_async_copy(zbuf, _row_tiles(xs_hbm, b * MOE_BLOCK, MOE_BLOCK), zsem).wait()
            return c

        lax.fori_loop(nu_ref[0], nblk, tail_start, 0)
        lax.fori_loop(nu_ref[0], nblk, tail_wait, 0)

        def per_expert(e, carry):
            base = padrow_ref[e]
            n = npad_ref[e]

            def start(r, c):
                pltpu.make_async_copy(_row_tile(zbuf, 0), _row_tile(xs_hbm, base + r), zsem).start()
                return c

            def wait(r, c):
                pltpu.make_async_copy(_row_tile(zbuf, 0), _row_tile(xs_hbm, base + r), zsem).wait()
                return c

            lax.fori_loop(0, n, start, 0)
            lax.fori_loop(0, n, wait, 0)
            return carry

        lax.fori_loop(0, N_EXPERTS, per_expert, 0)

    def body(j, carry):
        for u in range(SUBLANES):
            t = j * SUBLANES + u
            for k in range(TOP_K):
                d = dest_ref[0, 0, t * TOP_K + k]
                pltpu.make_async_copy(_row_tile(hn_ref, t), _row_tile(xs_hbm, d), sem).start(
                    priority=(u * TOP_K + k) % 2)
        return carry

    lax.fori_loop(0, DSP_TILE // SUBLANES, body, 0)
    for _ in range(TOP_K):
        pltpu.make_async_copy(hn_ref, _row_tiles(xs_hbm, 0, DSP_TILE), sem).wait()


def _dispatch(padrow, npad, nused, dest_tiles, hn, n_rows):
    T = hn.shape[0] // SUBLANES
    grid_spec = pltpu.PrefetchScalarGridSpec(
        num_scalar_prefetch=3,
        grid=(T // DSP_TILE,),
        in_specs=[pl.BlockSpec((1, 1, DSP_TILE * TOP_K), lambda i, *_: (i, 0, 0), memory_space=pltpu.SMEM),
                  pl.BlockSpec((DSP_TILE * SUBLANES, LANES), lambda i, *_: (i, 0))],
        out_specs=pl.BlockSpec(memory_space=pl.ANY),
        scratch_shapes=[pltpu.VMEM((MOE_BLOCK * SUBLANES, LANES), F32), pltpu.SemaphoreType.DMA(()),
                        pltpu.SemaphoreType.DMA(())],
    )
    return pl.pallas_call(
        _dispatch_kernel,
        grid_spec=grid_spec,
        out_shape=jax.ShapeDtypeStruct((n_rows * SUBLANES, LANES), F32),
        compiler_params=pltpu.CompilerParams(dimension_semantics=("arbitrary",), vmem_limit_bytes=VMEM_LIMIT),
        name="moe_dispatch",
    )(padrow, npad, nused, dest_tiles, hn)


def _moe_kernel(be_ref, nu_ref, first_ref, slot_ref, nxt_ref, x_ref, bgu_ref, bd_ref, wgu_hbm, wd_hbm, y_ref,
                wgu_buf, wd_buf, wsem):
    i = pl.program_id(0)
    used = i < nu_ref[0]
    s = slot_ref[i]

    def fetch(e, sl):
        return (pltpu.make_async_copy(wgu_hbm.at[e], wgu_buf.at[sl], wsem.at[0, sl]),
                pltpu.make_async_copy(wd_hbm.at[e], wd_buf.at[sl], wsem.at[1, sl]))

    @pl.when(jnp.logical_and(used, first_ref[i] == 1))
    def _():
        @pl.when(i == 0)
        def _():
            for c in fetch(be_ref[0], 0):
                c.start()

        for c in fetch(be_ref[i], s):
            c.wait()

        @pl.when(nxt_ref[i] >= 0)
        def _():
            for c in fetch(nxt_ref[i], 1 - s):
                c.start()

    @pl.when(used)
    def _():
        x = _load_row_tiles(x_ref, MOE_BLOCK).astype(BF16)
        hb = _dot(x, wgu_buf[s].astype(BF16)) + bgu_ref[0]
        glu = jnp.minimum(hb[:, :D_FF], SWIGLU_LIMIT)
        lin = jnp.clip(hb[:, D_FF:], -SWIGLU_LIMIT, SWIGLU_LIMIT)
        act = glu * jax.nn.sigmoid(SWIGLU_ALPHA * glu) * (lin + 1.0)
        _store_row_tiles(y_ref, MOE_BLOCK, _dot(act.astype(BF16), wd_buf[s].astype(BF16)) + bd_ref[0])

    @pl.when(i >= nu_ref[0])
    def _():
        y_ref[...] = jnp.zeros_like(y_ref)


def _moe_blocks(block_e, nused, xs, wgu, bgu, wd, bd):
    nblk = block_e.shape[0]
    idx = jnp.arange(nblk, dtype=jnp.int32)
    first = (idx < nused[0]) & ((idx == 0) | (block_e != jnp.roll(block_e, 1)))
    slot = ((jnp.cumsum(first.astype(jnp.int32)) - 1) % 2).astype(jnp.int32)
    first_pos = jnp.where(first, idx, nblk)
    later = jnp.concatenate([first_pos[1:], jnp.full((1,), nblk, jnp.int32)])
    next_pos = lax.cummin(later, reverse=True)
    nxt = jnp.sum(jnp.where(idx[None, :] == next_pos[:, None], block_e[None, :] + 1, 0), axis=1) - 1
    grid_spec = pltpu.PrefetchScalarGridSpec(
        num_scalar_prefetch=5,
        grid=(nblk,),
        in_specs=[
            pl.BlockSpec((MOE_BLOCK * SUBLANES, LANES), lambda i, be, *_: (i, 0)),
            pl.BlockSpec((1, 1, 2 * D_FF), lambda i, be, *_: (be[i], 0, 0)),
            pl.BlockSpec((1, 1, D_MODEL), lambda i, be, *_: (be[i], 0, 0)),
            pl.BlockSpec(memory_space=pl.ANY),
            pl.BlockSpec(memory_space=pl.ANY),
        ],
        out_specs=pl.BlockSpec((MOE_BLOCK * SUBLANES, LANES), lambda i, be, *_: (i, 0)),
        scratch_shapes=[pltpu.VMEM((2, D_MODEL, 2 * D_FF), F32), pltpu.VMEM((2, D_FF, D_MODEL), F32),
                        pltpu.SemaphoreType.DMA((2, 2))],
    )
    return pl.pallas_call(
        _moe_kernel,
        grid_spec=grid_spec,
        out_shape=jax.ShapeDtypeStruct(xs.shape, F32),
        compiler_params=pltpu.CompilerParams(dimension_semantics=("arbitrary",), vmem_limit_bytes=MOE_VMEM_LIMIT),
        name="moe_blocks",
    )(block_e, nused, first.astype(jnp.int32), slot, nxt.astype(jnp.int32), xs, bgu, bd, wgu, wd)


def _combine_kernel(npt, destc_ref, destn_ref, h_ref, route_ref, ys_hbm, yp_ref, ysm_ref, gbuf, gsem):
    i = pl.program_id(0)
    nt = pl.num_programs(0)
    slot = i % 2

    def issue(dest_ref, s):
        def body(j, carry):
            for u in range(SUBLANES):
                t = j * SUBLANES + u
                for k in range(TOP_K):
                    d = dest_ref[0, 0, t * TOP_K + k]
                    pltpu.make_async_copy(_row_tile(ys_hbm, d), _row_tile(gbuf.at[s, k], t), gsem.at[s]).start(
                        priority=(u * TOP_K + k) % 2)
            return carry
        lax.fori_loop(0, CMB_TILE // SUBLANES, body, 0)

    @pl.when(i == 0)
    def _():
        issue(destc_ref, 0)

    @pl.when(i + 1 < nt)
    def _():
        issue(destn_ref, 1 - slot)

    for k in range(TOP_K):
        pltpu.make_async_copy(_row_tiles(ys_hbm, 0, CMB_TILE), gbuf.at[slot, k], gsem.at[slot]).wait()
    r = route_ref[...]
    moe = _load_row_tiles(gbuf.at[slot, 0], CMB_TILE) * r[:, 0:1]
    for k in range(1, TOP_K):
        moe = moe + _load_row_tiles(gbuf.at[slot, k], CMB_TILE) * r[:, k:k + 1]
    y = h_ref[...] + moe

    @pl.when(i < npt)
    def _():
        yp_ref[...] = y

    @pl.when(i >= npt)
    def _():
        ysm_ref[...] = y


def _combine(dest_tiles, h, route, ys, n_prompt_rows):
    T = h.shape[0]
    nt = T // CMB_TILE
    npt = n_prompt_rows // CMB_TILE
    smem_blk = lambda imap: pl.BlockSpec((1, 1, CMB_TILE * TOP_K), imap, memory_space=pltpu.SMEM)
    return pl.pallas_call(
        functools.partial(_combine_kernel, npt),
        grid=(nt,),
        in_specs=[smem_blk(lambda i: (i, 0, 0)), smem_blk(lambda i: (jnp.minimum(i + 1, nt - 1), 0, 0)),
                  pl.BlockSpec((CMB_TILE, D_MODEL), lambda i: (i, 0)),
                  pl.BlockSpec((CMB_TILE, TOP_K), lambda i: (i, 0)),
                  pl.BlockSpec(memory_space=pl.ANY)],
        out_specs=(pl.BlockSpec((CMB_TILE, D_MODEL), lambda i: (jnp.minimum(i, npt - 1), 0)),
                   pl.BlockSpec((CMB_TILE, D_MODEL), lambda i: (jnp.maximum(i - npt, 0), 0))),
        out_shape=(jax.ShapeDtypeStruct((n_prompt_rows, D_MODEL), F32),
                   jax.ShapeDtypeStruct((T - n_prompt_rows, D_MODEL), F32)),
        scratch_shapes=[pltpu.VMEM((2, TOP_K, CMB_TILE * SUBLANES, LANES), F32), pltpu.SemaphoreType.DMA((2,))],
        compiler_params=pltpu.CompilerParams(dimension_semantics=("arbitrary",), vmem_limit_bytes=VMEM_LIMIT),
        name="moe_combine",
    )(dest_tiles, dest_tiles, h, route, ys)


def kernel(x_prompt, x_sample, cache_k_win, cache_v_win, state_conv, state_C, state_n, state_m, g_attn, w_in, b_i,
           b_f, q_norm_g, k_norm_g, sinks, conv_w, conv_b, m_norm_g, w_out, g_ffn, w_router, b_router, w_gate_up,
           b_gate_up, w_down, b_down):
    depth = g_attn.shape[0]
    assert depth == 1
    B, S, _ = x_prompt.shape
    DB, DS, _ = x_sample.shape
    TP = B * S
    TS = DB * DS
    T = TP + TS
    assert T % TOK_TILE == 0 and TP % TOK_TILE == 0 and S % ATT_QB == 0 and S % PROMPT_CHUNK == 0
    assert DS == SUBLANES and DB % SAMPLE_NB == 0 and (SAMPLE_NB * DS) == LANES
    l = 0

    xp = x_prompt.reshape(TP, D_MODEL)
    xs = x_sample.reshape(TS, D_MODEL)

    w_pad = jnp.pad(w_in[l], ((0, 0), (0, IN_PAD - w_in.shape[2]))).astype(BF16)
    wgt = jnp.transpose(w_in[l][:, GATE_COL:GATE_COL + 2 * M_HEADS]).astype(BF16)
    gi = jnp.arange(A_WIDTH) // HEAD_DIM
    gmat = (gi[:, None] == gi[None, :]).astype(BF16)
    qg = jnp.tile(q_norm_g[l], A_HEADS).reshape(1, A_WIDTH)
    kg = jnp.tile(k_norm_g[l], A_KV_HEADS).reshape(1, KV_WIDTH)
    gbias = jnp.concatenate([b_i[l], b_f[l]])
    gbrow = jnp.pad(gbias, (0, LANES - 2 * M_HEADS)).reshape(1, LANES)
    gbcol = gbias.reshape(2 * M_HEADS, 1)
    mng = m_norm_g[l].reshape(1, M_WIDTH)
    cw = conv_w[l]
    cb = conv_b[l].reshape(1, 2 * M_WIDTH)

    proj_w = (g_attn[l].reshape(1, D_MODEL), w_pad, wgt, gmat, qg, kg)
    qn, kn, va, qkm, vm, om, gcol, gt = _inproj(xp, *proj_w)
    qn_s, kn_s, va_s, qkm_s, vm_s, om_s, gcol_s, gt_s = _inproj(xs, *proj_w)

    a_p = _attn_prompt(sinks[l], qn, kn, va, B, S)
    ck = cache_k_win[l].reshape(DB, WINDOW, KV_WIDTH)
    cv = cache_v_win[l].reshape(DB, WINDOW, KV_WIDTH)
    a_s, kwin_s, vwin_s = _attn_sample(sinks[l], qn_s, kn_s, va_s, ck, cv, DB, DS)

    zc = jnp.zeros((B, SUBLANES, 2 * M_WIDTH), F32)
    m_p, C_p, n_p, mm_p = _mlstm(
        qkm, vm, om, gcol, gt, zc,
        jnp.zeros((B, M_HEADS, M_HEAD_DIM, M_HEAD_DIM), F32), jnp.zeros((B, M_HEADS, M_HEAD_DIM), F32),
        jnp.full((B, 1, LANES), NEG, F32), cw, cb, gbrow, gbcol, mng,
        nseq=1, L=PROMPT_CHUNK, ngroups=B, nchunks=S // PROMPT_CHUNK)
    conv_s0 = jnp.pad(state_conv[l], ((0, 0), (SUBLANES - (CONV_W - 1), 0), (0, 0)))
    m0_s = jnp.pad(state_m[l], ((0, 0), (0, LANES - M_HEADS))).reshape(DB, 1, LANES)
    m_s, C_s, n_s, mm_s = _mlstm(
        qkm_s, vm_s, om_s, gcol_s, gt_s, conv_s0, state_C[l], state_n[l], m0_s, cw, cb, gbrow, gbcol, mng,
        nseq=SAMPLE_NB, L=DS, ngroups=DB // SAMPLE_NB, nchunks=1)

    h, hn, route, cnt = _outproj(a_p, a_s, m_p, m_s, xp, xs, w_out[l].astype(BF16), g_ffn[l].reshape(1, D_MODEL),
                                 jnp.transpose(w_router[l]).astype(BF16), b_router[l].reshape(N_EXPERTS, 1))

    nblk = T * TOP_K // MOE_BLOCK + N_EXPERTS
    dest, blk, info = _route_tables(route, cnt, nblk)
    block_e = blk[0, :nblk]
    padrow = info[:, 0]
    npad = info[:, 1]
    nused = info[N_EXPERTS - 1:N_EXPERTS, 2]
    per_token = lambda a: jnp.transpose(a[:, :TOP_K, :], (0, 2, 1)).reshape(T, TOP_K)
    dest4 = per_token(dest)
    gate4 = per_token(route)
    xrows = _dispatch(padrow, npad, nused, dest4.reshape(T // DSP_TILE, 1, DSP_TILE * TOP_K), hn, nblk * MOE_BLOCK)
    yrows = _moe_blocks(block_e, nused, xrows,
                        w_gate_up[l], b_gate_up[l].reshape(N_EXPERTS, 1, 2 * D_FF),
                        w_down[l], b_down[l].reshape(N_EXPERTS, 1, D_MODEL))
    y_p, y_s = _combine(dest4.reshape(T // CMB_TILE, 1, CMB_TILE * TOP_K), h, gate4, yrows, TP)

    y_p = y_p.reshape(B, S, D_MODEL)
    y_s = y_s.reshape(DB, DS, D_MODEL)
    def seq_tail(rows, n):
        return jnp.stack([rows[(b + 1) * S - n:(b + 1) * S] for b in range(B)])

    kwin_p = seq_tail(kn, WINDOW).reshape(B, WINDOW, A_KV_HEADS, HEAD_DIM)
    vwin_p = seq_tail(va, WINDOW).reshape(B, WINDOW, A_KV_HEADS, HEAD_DIM)
    qkm_s = qkm_s.reshape(DB, DS, 2 * M_WIDTH)
    return (y_p, y_s,
            kwin_p[None], vwin_p[None], seq_tail(qkm, CONV_W - 1)[None],
            C_p[None], n_p[None], mm_p[:, 0, :M_HEADS][None],
            kwin_s.reshape(DB, WINDOW, A_KV_HEADS, HEAD_DIM)[None],
            vwin_s.reshape(DB, WINDOW, A_KV_HEADS, HEAD_DIM)[None],
            qkm_s[:, -(CONV_W - 1):][None],
            C_s[None], n_s[None], mm_s[:, 0, :M_HEADS][None])
```

```python
import functools

import jax
import jax.numpy as jnp
from jax import lax
from jax.experimental import pallas as pl
from jax.experimental.pallas import tpu as pltpu

F32 = jnp.float32
BF16 = jnp.bfloat16

D_MODEL = 1024
HEAD_DIM = 64
A_HEADS = 8
A_KV_HEADS = 2
A_GROUP = A_HEADS // A_KV_HEADS
A_WIDTH = A_HEADS * HEAD_DIM
KV_WIDTH = A_KV_HEADS * HEAD_DIM
WINDOW = 128
M_HEADS = 4
M_HEAD_DIM = 128
M_WIDTH = M_HEADS * M_HEAD_DIM
CONV_W = 4
N_EXPERTS = 32
TOP_K = 4
D_FF = D_MODEL
SWIGLU_LIMIT = 7.0
SWIGLU_ALPHA = 1.702
MOE_BLOCK = 512
EPS = 1e-6
NEG = -1e30

LANES = 128
SUBLANES = 8
GATE_COL = A_WIDTH + 2 * KV_WIDTH + 4 * M_WIDTH
IN_PAD = GATE_COL + LANES
TOK_TILE = 512
PROJ_TILE = 1024
PROJ_VMEM_LIMIT = 58 * 1024 * 1024
ATT_QB = 512
ATT_SB = 128
SAMPLE_NB = 16
PROMPT_CHUNK = 256
VMEM_LIMIT = 48 * 1024 * 1024
MOE_VMEM_LIMIT = 56 * 1024 * 1024


def _dot(a, b):
    return jnp.dot(a, b, preferred_element_type=F32)


def _dot_nt(a, b):
    return lax.dot_general(a, b, (((1,), (1,)), ((), ())), preferred_element_type=F32)


def _dot_tn(a, b):
    return lax.dot_general(a, b, (((0,), (0,)), ((), ())), preferred_element_type=F32)


def _split3(x):
    hi = x.astype(BF16)
    r1 = x - hi.astype(F32)
    mid = r1.astype(BF16)
    lo = (r1 - mid.astype(F32)).astype(BF16)
    return hi, mid, lo


def _log_sigmoid(x):
    return jnp.minimum(x, 0.0) - jnp.log1p(jnp.exp(-jnp.abs(x)))


def _load_row_tiles(ref2, rows):
    return jnp.concatenate([ref2[pl.ds(s, rows, stride=SUBLANES), :] for s in range(SUBLANES)], axis=1)


def _store_row_tiles(ref2, rows, val):
    for s in range(SUBLANES):
        ref2[pl.ds(s, rows, stride=SUBLANES), :] = val[:, s * LANES:(s + 1) * LANES]


def _row_tiles(ref2, first, n):
    start = first * SUBLANES
    if not isinstance(start, int):
        start = pl.multiple_of(start, SUBLANES)
    return ref2.at[pl.ds(start, n * SUBLANES), :]


def _split_specs(n_prompt_tiles, rows, width):
    return (pl.BlockSpec((rows, width), lambda i, *_: (jnp.minimum(i, n_prompt_tiles - 1), 0)),
            pl.BlockSpec((rows, width), lambda i, *_: (jnp.maximum(i - n_prompt_tiles, 0), 0)))


def _inproj_kernel(x_ref, g_ref, w_ref, wgt_ref, gmat_ref, qg_ref, kg_ref,
                   qn_ref, kn_ref, va_ref, qkm_ref, vm_ref, om_ref, gcol_ref, gt_ref):
    x = x_ref[...]
    ms = jnp.mean(x * x, axis=-1, keepdims=True)
    xn = ((x * lax.rsqrt(ms + EPS)) * g_ref[...]).astype(BF16)

    def seg(lo, hi):
        return _dot(xn, w_ref[:, lo:hi])

    def head_norm(z, gmat, g):
        hi, mid, lo = _split3(z * z)
        ss = _dot(hi, gmat) + _dot(mid, gmat) + _dot(lo, gmat)
        return (z * lax.rsqrt(ss * (1.0 / HEAD_DIM) + EPS)) * g

    o0 = A_WIDTH
    o1 = o0 + KV_WIDTH
    o2 = o1 + KV_WIDTH
    o3 = o2 + 2 * M_WIDTH
    o4 = o3 + M_WIDTH
    o5 = o4 + M_WIDTH
    qn_ref[...] = head_norm(seg(0, o0), gmat_ref[...], qg_ref[...])
    kn_ref[...] = head_norm(seg(o0, o1), gmat_ref[:KV_WIDTH, :KV_WIDTH], kg_ref[...])
    va_ref[...] = seg(o1, o2)
    qkm_ref[...] = seg(o2, o3)
    vm_ref[...] = seg(o3, o4)
    om_ref[...] = seg(o4, o5)
    gcol_ref[...] = seg(o5, o5 + LANES)
    gt_ref[0] = _dot_nt(wgt_ref[...], xn)


def _inproj(x, g_attn, w_pad, wgt, gmat, qg, kg):
    T = x.shape[0]
    nt = T // PROJ_TILE
    row = lambda w: pl.BlockSpec((PROJ_TILE, w), lambda i: (i, 0))
    full = lambda a: pl.BlockSpec(a.shape, lambda i: (0,) * a.ndim, pipeline_mode=pl.Buffered(1))
    out_shape = (
        jax.ShapeDtypeStruct((T, A_WIDTH), F32),
        jax.ShapeDtypeStruct((T, KV_WIDTH), F32),
        jax.ShapeDtypeStruct((T, KV_WIDTH), F32),
        jax.ShapeDtypeStruct((T, 2 * M_WIDTH), F32),
        jax.ShapeDtypeStruct((T, M_WIDTH), F32),
        jax.ShapeDtypeStruct((T, M_WIDTH), F32),
        jax.ShapeDtypeStruct((T, LANES), F32),
        jax.ShapeDtypeStruct((nt, SUBLANES, PROJ_TILE), F32),
    )
    out_specs = (row(A_WIDTH), row(KV_WIDTH), row(KV_WIDTH), row(2 * M_WIDTH), row(M_WIDTH), row(M_WIDTH),
                 row(LANES), pl.BlockSpec((1, SUBLANES, PROJ_TILE), lambda i: (i, 0, 0)))
    return pl.pallas_call(
        _inproj_kernel,
        grid=(nt,),
        in_specs=[row(D_MODEL), full(g_attn), full(w_pad), full(wgt), full(gmat), full(qg), full(kg)],
        out_specs=out_specs,
        out_shape=out_shape,
        compiler_params=pltpu.CompilerParams(dimension_semantics=("parallel",), vmem_limit_bytes=PROJ_VMEM_LIMIT),
        name="inproj",
    )(x, g_attn, w_pad, wgt, gmat, qg, kg)


def _softmax_sink(pieces, masks, sink_col):
    masked = [jnp.where(mk, s, NEG) for s, mk in zip(pieces, masks)]
    m = sink_col
    for s in masked:
        m = jnp.maximum(m, jnp.max(s, axis=-1, keepdims=True))
    ps = [jnp.exp(s - m) for s in masked]
    den = jnp.exp(sink_col - m)
    for p in ps:
        den = den + jnp.sum(p, axis=-1, keepdims=True)
    return ps, 1.0 / den


def _stack_heads(q, g):
    return jnp.concatenate([q[:, (A_GROUP * g + i) * HEAD_DIM:(A_GROUP * g + i + 1) * HEAD_DIM]
                            for i in range(A_GROUP)], axis=0)


def _sink_col(sink_ref, g, rows_per_head):
    r = lax.broadcasted_iota(jnp.int32, (A_GROUP * rows_per_head, 1), 0)
    col = jnp.zeros((A_GROUP * rows_per_head, 1), F32)
    for i in range(A_GROUP):
        col = jnp.where(r // rows_per_head == i, sink_ref[A_GROUP * g + i], col)
    return col


def _attn_prompt_kernel(sink_ref, q_ref, kp_ref, kc_ref, vp_ref, vc_ref, o_ref):
    j = pl.program_id(1)
    scale = HEAD_DIM ** -0.5
    kall = jnp.concatenate([kp_ref[...], kc_ref[...]], axis=0).astype(BF16)
    vall = jnp.concatenate([vp_ref[...], vc_ref[...]], axis=0).astype(BF16)
    nrow = A_GROUP * ATT_SB
    r = lax.broadcasted_iota(jnp.int32, (nrow, 2 * ATT_SB), 0) % ATT_SB
    c = lax.broadcasted_iota(jnp.int32, (nrow, 2 * ATT_SB), 1)
    band = jnp.logical_and(c >= r, c <= r + WINDOW)
    band0 = jnp.logical_and(band, jnp.logical_or(c >= ATT_SB, j > 0))
    for sb in range(ATT_QB // ATT_SB):
        q = (q_ref[sb * ATT_SB:(sb + 1) * ATT_SB, :] * scale).astype(BF16)
        kwin = kall[sb * ATT_SB:(sb + 2) * ATT_SB]
        vwin = vall[sb * ATT_SB:(sb + 2) * ATT_SB]
        outs = []
        for g in range(A_KV_HEADS):
            lo, hi = g * HEAD_DIM, (g + 1) * HEAD_DIM
            s = _dot_nt(_stack_heads(q, g), kwin[:, lo:hi])
            (p,), inv = _softmax_sink([s], [band0 if sb == 0 else band], _sink_col(sink_ref, g, ATT_SB))
            o = _dot(p.astype(BF16), vwin[:, lo:hi]) * inv
            outs += [o[i * ATT_SB:(i + 1) * ATT_SB] for i in range(A_GROUP)]
        o_ref[sb * ATT_SB:(sb + 1) * ATT_SB, :] = jnp.concatenate(outs, axis=1)


def _attn_prompt(sinks, qn, kn, va, batch, seq):
    nq = seq // ATT_QB
    ratio = ATT_QB // ATT_SB
    cur = lambda w: pl.BlockSpec((ATT_QB, w), lambda b, j: (b * nq + j, 0))
    prev = lambda w: pl.BlockSpec((ATT_SB, w), lambda b, j: (jnp.maximum((b * nq + j) * ratio - 1, 0), 0))
    return pl.pallas_call(
        _attn_prompt_kernel,
        grid=(batch, nq),
        in_specs=[pl.BlockSpec(memory_space=pltpu.SMEM), cur(A_WIDTH), prev(KV_WIDTH), cur(KV_WIDTH),
                  prev(KV_WIDTH), cur(KV_WIDTH)],
        out_specs=cur(A_WIDTH),
        out_shape=jax.ShapeDtypeStruct((batch * seq, A_WIDTH), F32),
        compiler_params=pltpu.CompilerParams(dimension_semantics=("parallel", "parallel"),
                                             vmem_limit_bytes=VMEM_LIMIT),
        name="attn_prompt",
    )(sinks, qn, kn, kn, va, va)


def _attn_sample_kernel(dec, sink_ref, q_ref, kn_ref, vn_ref, ck_ref, cv_ref, o_ref, kw_ref, vw_ref):
    scale = HEAD_DIM ** -0.5
    rows = SAMPLE_NB * dec
    knew = kn_ref[...]
    vnew = vn_ref[...]
    knew_b = knew.astype(BF16)
    vnew_b = vnew.astype(BF16)
    nrow = A_GROUP * dec
    t = lax.broadcasted_iota(jnp.int32, (nrow, WINDOW), 0) % dec
    c = lax.broadcasted_iota(jnp.int32, (nrow, WINDOW), 1)
    m_cache = c >= t
    cn = lax.broadcasted_iota(jnp.int32, (nrow, rows), 1)
    tn = lax.broadcasted_iota(jnp.int32, (nrow, rows), 0) % dec
    for i in range(SAMPLE_NB):
        q = (q_ref[i * dec:(i + 1) * dec, :] * scale).astype(BF16)
        ck = ck_ref[i].astype(BF16)
        cv = cv_ref[i].astype(BF16)
        m_new = jnp.logical_and(cn // dec == i, cn % dec <= tn)
        outs = []
        for g in range(A_KV_HEADS):
            lo, hi = g * HEAD_DIM, (g + 1) * HEAD_DIM
            qs = _stack_heads(q, g)
            s_c = _dot_nt(qs, ck[:, lo:hi])
            s_n = _dot_nt(qs, knew_b[:, lo:hi])
            (p_c, p_n), inv = _softmax_sink([s_c, s_n], [m_cache, m_new], _sink_col(sink_ref, g, dec))
            o = (_dot(p_c.astype(BF16), cv[:, lo:hi]) + _dot(p_n.astype(BF16), vnew_b[:, lo:hi])) * inv
            outs += [o[h * dec:(h + 1) * dec] for h in range(A_GROUP)]
        o_ref[i * dec:(i + 1) * dec, :] = jnp.concatenate(outs, axis=1)
        kw_ref[i, 0:WINDOW - dec, :] = ck_ref[i, dec:WINDOW, :]
        kw_ref[i, WINDOW - dec:WINDOW, :] = knew[i * dec:(i + 1) * dec]
        vw_ref[i, 0:WINDOW - dec, :] = cv_ref[i, dec:WINDOW, :]
        vw_ref[i, WINDOW - dec:WINDOW, :] = vnew[i * dec:(i + 1) * dec]


def _attn_sample(sinks, qn, kn, va, ck, cv, dbatch, dec):
    rows = SAMPLE_NB * dec
    tokrow = lambda w: pl.BlockSpec((rows, w), lambda i: (i, 0))
    cache = pl.BlockSpec((SAMPLE_NB, WINDOW, KV_WIDTH), lambda i: (i, 0, 0))
    return pl.pallas_call(
        functools.partial(_attn_sample_kernel, dec),
        grid=(dbatch // SAMPLE_NB,),
        in_specs=[pl.BlockSpec(memory_space=pltpu.SMEM), tokrow(A_WIDTH), tokrow(KV_WIDTH), tokrow(KV_WIDTH),
                  cache, cache],
        out_specs=(pl.BlockSpec((rows, A_WIDTH), lambda i: (i, 0)), cache, cache),
        out_shape=(jax.ShapeDtypeStruct((dbatch * dec, A_WIDTH), F32),
                   jax.ShapeDtypeStruct((dbatch, WINDOW, KV_WIDTH), F32),
                   jax.ShapeDtypeStruct((dbatch, WINDOW, KV_WIDTH), F32)),
        compiler_params=pltpu.CompilerParams(dimension_semantics=("parallel",), vmem_limit_bytes=VMEM_LIMIT),
        name="attn_sample",
    )(sinks, qn, kn, va, ck, cv)


def _mlstm_kernel(nseq, L, qk_ref, v_ref, o_ref, gcol_ref, gt_ref, conv0_ref, c0_ref, n0_ref, m0_ref,
                  cw_ref, cb_ref, gbrow_ref, gbcol_ref, mng_ref, mask_ref,
                  out_ref, cst_ref, nst_ref, mst_ref, prev_ref):
    R = nseq * L
    ci = pl.program_id(1)

    @pl.when(ci == 0)
    def _():
        cst_ref[...] = c0_ref[...]
        nst_ref[...] = n0_ref[...]
        mst_ref[...] = m0_ref[...]
        prev_ref[...] = conv0_ref[0]

    raw = qk_ref[...]
    row = lax.broadcasted_iota(jnp.int32, (R, 1), 0)
    tpos = row % L
    rseq = row // L
    acc = raw * cw_ref[CONV_W - 1:CONV_W, :] + cb_ref[...]
    if nseq == 1:
        prev8 = prev_ref[...]
        t8 = lax.broadcasted_iota(jnp.int32, (SUBLANES, 1), 0)
    else:
        prevsrc = conv0_ref[...].reshape(R, 2 * M_WIDTH)
    for k in range(1, CONV_W):
        rolled = pltpu.roll(raw, k, 0)
        if nseq == 1:
            head = jnp.where(t8 >= k, rolled[0:SUBLANES], pltpu.roll(prev8, k, 0))
            sh = jnp.concatenate([head, rolled[SUBLANES:]], axis=0)
        else:
            sh = jnp.where(tpos >= k, rolled, pltpu.roll(prevsrc, R - SUBLANES + k, 0))
        acc = acc + sh * cw_ref[CONV_W - 1 - k:CONV_W - k, :]
    if nseq == 1:
        prev_ref[...] = raw[R - SUBLANES:R]
    qkc = acc * jax.nn.sigmoid(acc)

    gc = gcol_ref[...] + gbrow_ref[...]
    gr = gt_ref[0] + gbcol_ref[...]
    lsc = _log_sigmoid(gc)
    lsr = _log_sigmoid(gr)
    mb = mask_ref[...]
    maskb = mb > 0
    bcol = sum(_dot(mb, p) for p in _split3(lsc))
    brow = sum(_dot_nt(p, mb) for p in _split3(lsr))

    lane = lax.broadcasted_iota(jnp.int32, (1, LANES), 1)
    m_new = [jnp.zeros((1, LANES), F32) for _ in range(nseq)]
    for h in range(M_HEADS):
        sl = slice(h * M_HEAD_DIM, (h + 1) * M_HEAD_DIM)
        qh = qkc[:, sl]
        kh = qkc[:, M_WIDTH + h * M_HEAD_DIM:M_WIDTH + (h + 1) * M_HEAD_DIM] * (M_HEAD_DIM ** -0.5)
        vh = v_ref[:, sl]
        qb, kb, vb = qh.astype(BF16), kh.astype(BF16), vh.astype(BF16)
        ig_c = gc[:, h:h + 1]
        b_c = bcol[:, M_HEADS + h:M_HEADS + h + 1]
        ig_r = gr[h:h + 1, :]
        b_r = brow[M_HEADS + h:M_HEADS + h + 1, :]
        if nseq == 1:
            m0c = mst_ref[0][:, h:h + 1]
            n0rows = nst_ref[0, h:h + 1, :]
        else:
            m0c = jnp.zeros((R, 1), F32)
            n0rows = jnp.zeros((R, M_HEAD_DIM), F32)
            for s in range(nseq):
                m0c = jnp.where(rseq == s, mst_ref[s][:, h:h + 1], m0c)
                n0rows = jnp.where(rseq == s, nst_ref[s, h:h + 1, :], n0rows)
        dm = jnp.where(maskb, b_c - b_r + ig_r, NEG)
        a_c = b_c + m0c
        m_c = jnp.maximum(a_c, jnp.max(dm, axis=-1, keepdims=True))
        w = jnp.exp(dm - m_c)
        sc = jnp.exp(a_c - m_c)
        wqk = w * _dot_nt(qb, kb)
        if nseq == 1:
            inter = _dot_nt(qb, cst_ref[0, h].astype(BF16))
        else:
            inter = jnp.zeros((R, M_HEAD_DIM), F32)
            for s in range(nseq):
                qs = jnp.where(rseq == s, qh, 0.0).astype(BF16)
                inter = inter + _dot_nt(qs, cst_ref[s, h].astype(BF16))
        num = _dot(wqk.astype(BF16), vb) + sc * inter
        den = jnp.sum(wqk, axis=-1, keepdims=True) + sc * jnp.sum(qh * n0rows, axis=-1, keepdims=True)
        hh = num / jnp.maximum(jnp.abs(den), jnp.exp(-m_c))

        for s in range(nseq):
            e = s * L + L - 1
            m_end = m_c[e:e + 1, :]
            wend = jnp.exp(b_c[e:e + 1, :] - b_c + ig_c - m_end)
            if nseq > 1:
                wend = jnp.where(rseq == s, wend, 0.0)
            sce = jnp.exp(a_c[e:e + 1, :] - m_end)
            c_new = sce * cst_ref[s, h] + _dot_tn((vh * wend).astype(BF16), kb)
            n_new = sce * nst_ref[s, h:h + 1, :] + jnp.sum(wend * kh, axis=0, keepdims=True)
            cst_ref[s, h] = c_new
            nst_ref[s, h:h + 1, :] = n_new
            m_new[s] = jnp.where(lane == h, m_end, m_new[s])

        hn = (hh * lax.rsqrt(jnp.mean(hh * hh, axis=-1, keepdims=True) + EPS)) * mng_ref[:, sl]
        out_ref[:, sl] = jax.nn.sigmoid(o_ref[:, sl]) * hn
    for s in range(nseq):
        mst_ref[s] = m_new[s]


def _mlstm(qkm, vm, om, gcol, gt, conv0, c0, n0, m0, cw, cb, gbrow, gbcol, mng, nseq, L, ngroups, nchunks):
    R = nseq * L
    per_tile = gt.shape[2] // R
    tok = lambda w: pl.BlockSpec((R, w), lambda g, c: (g * nchunks + c, 0))
    gt_spec = pl.BlockSpec((1, SUBLANES, R),
                           lambda g, c: ((g * nchunks + c) // per_tile, 0, (g * nchunks + c) % per_tile))
    full = lambda a: pl.BlockSpec(a.shape, lambda g, c: (0,) * a.ndim)
    st4 = pl.BlockSpec((nseq, M_HEADS, M_HEAD_DIM, M_HEAD_DIM), lambda g, c: (g, 0, 0, 0))
    st3 = pl.BlockSpec((nseq, M_HEADS, M_HEAD_DIM), lambda g, c: (g, 0, 0))
    stm = pl.BlockSpec((nseq, 1, LANES), lambda g, c: (g, 0, 0))
    conv_spec = pl.BlockSpec((nseq, SUBLANES, 2 * M_WIDTH), lambda g, c: (g, 0, 0))
    r = jnp.arange(R)
    mask = ((r[:, None] // L == r[None, :] // L) & (r[None, :] <= r[:, None])).astype(BF16)
    nstate = ngroups * nseq
    return pl.pallas_call(
        functools.partial(_mlstm_kernel, nseq, L),
        grid=(ngroups, nchunks),
        in_specs=[tok(2 * M_WIDTH), tok(M_WIDTH), tok(M_WIDTH), tok(LANES), gt_spec, conv_spec, st4, st3, stm,
                  full(cw), full(cb), full(gbrow), full(gbcol), full(mng), full(mask)],
        out_specs=(pl.BlockSpec((R, M_WIDTH), lambda g, c: (g * nchunks + c, 0)), st4, st3, stm),
        out_shape=(jax.ShapeDtypeStruct((ngroups * nchunks * R, M_WIDTH), F32),
                   jax.ShapeDtypeStruct((nstate, M_HEADS, M_HEAD_DIM, M_HEAD_DIM), F32),
                   jax.ShapeDtypeStruct((nstate, M_HEADS, M_HEAD_DIM), F32),
                   jax.ShapeDtypeStruct((nstate, 1, LANES), F32)),
        scratch_shapes=[pltpu.VMEM((SUBLANES, 2 * M_WIDTH), F32)],
        compiler_params=pltpu.CompilerParams(dimension_semantics=("parallel", "arbitrary"),
                                             vmem_limit_bytes=VMEM_LIMIT),
        name="mlstm_n%d" % nseq,
    )(qkm, vm, om, gcol, gt, conv0, c0, n0, m0, cw, cb, gbrow, gbcol, mng, mask)


def _outproj_kernel(npt, ap_ref, as_ref, mp_ref, ms_ref, xp_ref, xs_ref, wo_ref, g_ref, wrt_ref, brc_ref,
                    h_ref, hn_ref, route_ref, cnt_ref):
    def project(a_ref, m_ref, x_ref):
        h_ref[...] = (x_ref[...] + _dot(a_ref[...].astype(BF16), wo_ref[0:A_WIDTH, :])
                      + _dot(m_ref[...].astype(BF16), wo_ref[A_WIDTH:A_WIDTH + M_WIDTH, :]))

    @pl.when(pl.program_id(0) < npt)
    def _():
        project(ap_ref, mp_ref, xp_ref)

    @pl.when(pl.program_id(0) >= npt)
    def _():
        project(as_ref, ms_ref, xs_ref)

    h = h_ref[...]
    hn = (h * lax.rsqrt(jnp.mean(h * h, axis=-1, keepdims=True) + EPS)) * g_ref[...]
    _store_row_tiles(hn_ref, TOK_TILE, hn)
    logits = _dot_nt(wrt_ref[...], hn.astype(BF16)) + brc_ref[...]
    eidx = lax.broadcasted_iota(jnp.int32, logits.shape, 0).astype(F32)
    picked = jnp.zeros(logits.shape, F32)
    top0 = None
    den = None
    es = []
    ids = []
    for k in range(TOP_K):
        mx = jnp.max(logits, axis=0, keepdims=True)
        idx = jnp.min(jnp.where(logits == mx, eidx, float(N_EXPERTS)), axis=0, keepdims=True)
        if k == 0:
            top0 = mx
        e = jnp.exp(mx - top0)
        den = e if den is None else den + e
        es.append(e)
        ids.append(idx)
        hit = eidx == idx
        picked = jnp.where(hit, 1.0, picked)
        logits = jnp.where(hit, -jnp.inf, logits)
    route_ref[0] = jnp.concatenate([e / den for e in es] + ids, axis=0)

    @pl.when(pl.program_id(0) == 0)
    def _():
        cnt_ref[...] = jnp.zeros_like(cnt_ref)

    cnt_ref[...] += jnp.broadcast_to(jnp.sum(picked, axis=1, keepdims=True), cnt_ref.shape)


def _outproj(a_p, a_s, m_p, m_s, xp, xs, w_out, g_ffn, wr_t, br_col):
    T = xp.shape[0] + xs.shape[0]
    nt = T // TOK_TILE
    npt = xp.shape[0] // TOK_TILE
    row = lambda w: pl.BlockSpec((TOK_TILE, w), lambda i: (i, 0))
    full = lambda a: pl.BlockSpec(a.shape, lambda i: (0,) * a.ndim)
    return pl.pallas_call(
        functools.partial(_outproj_kernel, npt),
        grid=(nt,),
        in_specs=[*_split_specs(npt, TOK_TILE, A_WIDTH), *_split_specs(npt, TOK_TILE, M_WIDTH),
                  *_split_specs(npt, TOK_TILE, D_MODEL), full(w_out), full(g_ffn), full(wr_t), full(br_col)],
        out_specs=(row(D_MODEL), pl.BlockSpec((TOK_TILE * SUBLANES, LANES), lambda i: (i, 0)),
                   pl.BlockSpec((1, 2 * TOP_K, TOK_TILE), lambda i: (i, 0, 0)),
                   pl.BlockSpec((N_EXPERTS, LANES), lambda i: (0, 0))),
        out_shape=(jax.ShapeDtypeStruct((T, D_MODEL), F32), jax.ShapeDtypeStruct((T * SUBLANES, LANES), F32),
                   jax.ShapeDtypeStruct((nt, 2 * TOP_K, TOK_TILE), F32),
                   jax.ShapeDtypeStruct((N_EXPERTS, LANES), F32)),
        compiler_params=pltpu.CompilerParams(dimension_semantics=("arbitrary",), vmem_limit_bytes=VMEM_LIMIT),
        name="outproj_router",
    )(a_p, a_s, m_p, m_s, xp, xs, w_out, g_ffn, wr_t, br_col)


def _route_kernel(nblk_pad, route_ref, cnt_ref, ustrict_ref, lstrict_ref, lp_ref, tab_ref, blk_ref, info_ref,
                  carry_ref):
    i = pl.program_id(0)
    cnt = cnt_ref[...]
    nb_e = jnp.floor((cnt + (MOE_BLOCK - 1.0)) * (1.0 / MOE_BLOCK))
    bstart = sum(_dot(lstrict_ref[...], p) for p in _split3(nb_e))
    bend = bstart + nb_e
    row_start = bstart * float(MOE_BLOCK)

    @pl.when(i == 0)
    def _():
        carry_ref[...] = jnp.zeros_like(carry_ref)
        bi = lax.broadcasted_iota(jnp.int32, (N_EXPERTS, nblk_pad), 1).astype(F32)
        done = jnp.where(bend[:, 0:1] <= bi, 1.0, 0.0)
        be = jnp.minimum(jnp.sum(done, axis=0, keepdims=True), N_EXPERTS - 1.0)
        blk_ref[...] = jnp.broadcast_to(be, blk_ref.shape).astype(jnp.int32)
        lane = lax.broadcasted_iota(jnp.int32, (N_EXPERTS, LANES), 1)
        info = jnp.where(lane == 0, row_start + cnt, 0.0)
        info = jnp.where(lane == 1, nb_e * float(MOE_BLOCK) - cnt, info)
        info = jnp.where(lane == 2, bend, info)
        info_ref[...] = info.astype(jnp.int32)

    r = route_ref[0]
    eidx = lax.broadcasted_iota(jnp.int32, (N_EXPERTS, TOK_TILE), 0).astype(F32)
    sel = [eidx == r[TOP_K + k:TOP_K + k + 1, :] for k in range(TOP_K)]
    oh = jnp.zeros((N_EXPERTS, TOK_TILE), F32)
    for k in range(TOP_K):
        oh = jnp.where(sel[k], 1.0, oh)
    cnt_t = jnp.broadcast_to(jnp.sum(oh, axis=1, keepdims=True), (N_EXPERTS, LANES))
    seg = sum(_dot(lstrict_ref[...], p) for p in _split3(cnt_t))
    local = _dot(oh.astype(BF16), ustrict_ref[...]) + seg[:, 0:1]
    rows = [jnp.sum(jnp.where(sel[k], local, 0.0), axis=0, keepdims=True) for k in range(TOP_K)]
    lp_ref[0] = jnp.concatenate(rows + [jnp.zeros((TOP_K, TOK_TILE), F32)], axis=0).astype(jnp.int32)
    diag = (lax.broadcasted_iota(jnp.int32, (N_EXPERTS, LANES), 0)
            == lax.broadcasted_iota(jnp.int32, (N_EXPERTS, LANES), 1))
    to_lanes = lambda col: jnp.sum(jnp.where(diag, col, 0.0), axis=0, keepdims=True)
    tab = [to_lanes(seg), to_lanes(cnt_t), to_lanes(row_start + carry_ref[...])]
    tab_ref[0] = jnp.concatenate(tab + [jnp.zeros((SUBLANES - len(tab), LANES), F32)], axis=0).astype(jnp.int32)
    carry_ref[...] += cnt_t


def _route_tables(route, cnt, nblk):
    nt = route.shape[0]
    nblk_pad = -(-nblk // LANES) * LANES
    a = jnp.arange(TOK_TILE)
    ustrict = (a[:, None] < a[None, :]).astype(BF16)
    b = jnp.arange(N_EXPERTS)
    lstrict = (b[:, None] > b[None, :]).astype(BF16)
    full = lambda x: pl.BlockSpec(x.shape, lambda i: (0,) * x.ndim)
    tile = pl.BlockSpec((1, 2 * TOP_K, TOK_TILE), lambda i: (i, 0, 0))
    return pl.pallas_call(
        functools.partial(_route_kernel, nblk_pad),
        grid=(nt,),
        in_specs=[tile, full(cnt), full(ustrict), full(lstrict)],
        out_specs=(tile, pl.BlockSpec((1, SUBLANES, LANES), lambda i: (i, 0, 0)),
                   pl.BlockSpec((SUBLANES, nblk_pad), lambda i: (0, 0)),
                   pl.BlockSpec((N_EXPERTS, LANES), lambda i: (0, 0))),
        out_shape=(jax.ShapeDtypeStruct((nt, 2 * TOP_K, TOK_TILE), jnp.int32),
                   jax.ShapeDtypeStruct((nt, SUBLANES, LANES), jnp.int32),
                   jax.ShapeDtypeStruct((SUBLANES, nblk_pad), jnp.int32),
                   jax.ShapeDtypeStruct((N_EXPERTS, LANES), jnp.int32)),
        scratch_shapes=[pltpu.VMEM((N_EXPERTS, LANES), F32)],
        compiler_params=pltpu.CompilerParams(dimension_semantics=("arbitrary",), vmem_limit_bytes=VMEM_LIMIT),
        name="route_tables",
    )(route, cnt, ustrict, lstrict)


RUN_PIECES = tuple(1 << b for b in range(9, -1, -1))


def _for_run_pieces(count, fn):
    for size in RUN_PIECES:
        @pl.when((count & size) != 0)
        def _():
            fn(count & ~(2 * size - 1), size)


def _dispatch_kernel(padrow_ref, npad_ref, nu_ref, lp_ref, tab_ref, hn_ref, xs_hbm, srt, zbuf, sem, zsem):
    i = pl.program_id(0)
    nt = pl.num_programs(0)
    slot = i % 2

    @pl.when(i == 0)
    def _():
        zbuf[...] = jnp.zeros_like(zbuf)
        nblk = xs_hbm.shape[0] // (MOE_BLOCK * SUBLANES)

        def tail_start(b, c):
            pltpu.make_async_copy(zbuf, _row_tiles(xs_hbm, b * MOE_BLOCK, MOE_BLOCK), zsem).start()
            return c

        def tail_wait(b, c):
            pltpu.make_async_copy(zbuf, _row_tiles(xs_hbm, b * MOE_BLOCK, MOE_BLOCK), zsem).wait()
            return c

        lax.fori_loop(nu_ref[0], nblk, tail_start, 0)
        lax.fori_loop(nu_ref[0], nblk, tail_wait, 0)

        def pad_runs(wait):
            def per_expert(e, carry):
                def piece(off, size):
                    cp = pltpu.make_async_copy(_row_tiles(zbuf, 0, size), _row_tiles(xs_hbm, padrow_ref[e] + off, size),
                                               zsem)
                    cp.wait() if wait else cp.start()
                _for_run_pieces(npad_ref[e], piece)
                return carry
            lax.fori_loop(0, N_EXPERTS, per_expert, 0)

        pad_runs(False)
        pad_runs(True)

    def wait_runs(s):
        for _ in range(TOP_K):
            pltpu.make_async_copy(_row_tiles(srt.at[s], 0, TOK_TILE), _row_tiles(xs_hbm, 0, TOK_TILE), sem.at[s]).wait()

    @pl.when(i >= 2)
    def _():
        wait_runs(slot)

    def permute(j, carry):
        for u in range(SUBLANES):
            t = j * SUBLANES + u
            row = hn_ref[pl.ds(pl.multiple_of(t * SUBLANES, SUBLANES), SUBLANES), :]
            for k in range(TOP_K):
                p = lp_ref[0, 0, t * TOP_K + k]
                srt[slot, pl.ds(pl.multiple_of(p * SUBLANES, SUBLANES), SUBLANES), :] = row
        return carry

    lax.fori_loop(0, TOK_TILE // SUBLANES, permute, 0)

    def send_run(e, carry):
        def piece(off, size):
            pltpu.make_async_copy(_row_tiles(srt.at[slot], tab_ref[0, 0, e] + off, size),
                                  _row_tiles(xs_hbm, tab_ref[0, 2, e] + off, size), sem.at[slot]).start()
        _for_run_pieces(tab_ref[0, 1, e], piece)
        return carry

    lax.fori_loop(0, N_EXPERTS, send_run, 0)

    @pl.when(i == nt - 1)
    def _():
        @pl.when(nt >= 2)
        def _():
            wait_runs(1 - slot)

        wait_runs(slot)


def _dispatch(padrow, npad, nused, lp_tiles, tab, hn, n_rows):
    nt = lp_tiles.shape[0]
    grid_spec = pltpu.PrefetchScalarGridSpec(
        num_scalar_prefetch=3,
        grid=(nt,),
        in_specs=[pl.BlockSpec((1, 1, TOK_TILE * TOP_K), lambda i, *_: (i, 0, 0), memory_space=pltpu.SMEM),
                  pl.BlockSpec((1, SUBLANES, LANES), lambda i, *_: (i, 0, 0), memory_space=pltpu.SMEM),
                  pl.BlockSpec((TOK_TILE * SUBLANES, LANES), lambda i, *_: (i, 0))],
        out_specs=pl.BlockSpec(memory_space=pl.ANY),
        scratch_shapes=[pltpu.VMEM((2, TOK_TILE * TOP_K * SUBLANES, LANES), F32),
                        pltpu.VMEM((MOE_BLOCK * SUBLANES, LANES), F32),
                        pltpu.SemaphoreType.DMA((2,)), pltpu.SemaphoreType.DMA(())],
    )
    return pl.pallas_call(
        _dispatch_kernel,
        grid_spec=grid_spec,
        out_shape=jax.ShapeDtypeStruct((n_rows * SUBLANES, LANES), F32),
        compiler_params=pltpu.CompilerParams(dimension_semantics=("arbitrary",), vmem_limit_bytes=VMEM_LIMIT),
        name="moe_dispatch",
    )(padrow, npad, nused, lp_tiles, tab, hn)


def _moe_kernel(be_ref, nu_ref, first_ref, slot_ref, nxt_ref, x_ref, bgu_ref, bd_ref, wgu_hbm, wd_hbm, y_ref,
                wgu_buf, wd_buf, wsem):
    i = pl.program_id(0)
    used = i < nu_ref[0]
    s = slot_ref[i]

    def fetch(e, sl):
        return (pltpu.make_async_copy(wgu_hbm.at[e], wgu_buf.at[sl], wsem.at[0, sl]),
                pltpu.make_async_copy(wd_hbm.at[e], wd_buf.at[sl], wsem.at[1, sl]))

    @pl.when(jnp.logical_and(used, first_ref[i] == 1))
    def _():
        @pl.when(i == 0)
        def _():
            for c in fetch(be_ref[0], 0):
                c.start()

        for c in fetch(be_ref[i], s):
            c.wait()

        @pl.when(nxt_ref[i] >= 0)
        def _():
            for c in fetch(nxt_ref[i], 1 - s):
                c.start()

    @pl.when(used)
    def _():
        x = _load_row_tiles(x_ref, MOE_BLOCK).astype(BF16)
        hb = _dot(x, wgu_buf[s].astype(BF16)) + bgu_ref[0]
        glu = jnp.minimum(hb[:, :D_FF], SWIGLU_LIMIT)
        lin = jnp.clip(hb[:, D_FF:], -SWIGLU_LIMIT, SWIGLU_LIMIT)
        act = glu * jax.nn.sigmoid(SWIGLU_ALPHA * glu) * (lin + 1.0)
        _store_row_tiles(y_ref, MOE_BLOCK, _dot(act.astype(BF16), wd_buf[s].astype(BF16)) + bd_ref[0])

    @pl.when(i >= nu_ref[0])
    def _():
        y_ref[...] = jnp.zeros_like(y_ref)


def _moe_blocks(block_e, nused, xs, wgu, bgu, wd, bd):
    nblk = block_e.shape[0]
    idx = jnp.arange(nblk, dtype=jnp.int32)
    first = (idx < nused[0]) & ((idx == 0) | (block_e != jnp.roll(block_e, 1)))
    slot = ((jnp.cumsum(first.astype(jnp.int32)) - 1) % 2).astype(jnp.int32)
    first_pos = jnp.where(first, idx, nblk)
    later = jnp.concatenate([first_pos[1:], jnp.full((1,), nblk, jnp.int32)])
    next_pos = lax.cummin(later, reverse=True)
    nxt = jnp.sum(jnp.where(idx[None, :] == next_pos[:, None], block_e[None, :] + 1, 0), axis=1) - 1
    grid_spec = pltpu.PrefetchScalarGridSpec(
        num_scalar_prefetch=5,
        grid=(nblk,),
        in_specs=[
            pl.BlockSpec((MOE_BLOCK * SUBLANES, LANES), lambda i, be, *_: (i, 0)),
            pl.BlockSpec((1, 1, 2 * D_FF), lambda i, be, *_: (be[i], 0, 0)),
            pl.BlockSpec((1, 1, D_MODEL), lambda i, be, *_: (be[i], 0, 0)),
            pl.BlockSpec(memory_space=pl.ANY),
            pl.BlockSpec(memory_space=pl.ANY),
        ],
        out_specs=pl.BlockSpec((MOE_BLOCK * SUBLANES, LANES), lambda i, be, *_: (i, 0)),
        scratch_shapes=[pltpu.VMEM((2, D_MODEL, 2 * D_FF), F32), pltpu.VMEM((2, D_FF, D_MODEL), F32),
                        pltpu.SemaphoreType.DMA((2, 2))],
    )
    return pl.pallas_call(
        _moe_kernel,
        grid_spec=grid_spec,
        out_shape=jax.ShapeDtypeStruct(xs.shape, F32),
        compiler_params=pltpu.CompilerParams(dimension_semantics=("arbitrary",), vmem_limit_bytes=MOE_VMEM_LIMIT),
        name="moe_blocks",
    )(block_e, nused, first.astype(jnp.int32), slot, nxt.astype(jnp.int32), xs, bgu, bd, wgu, wd)


def _combine_kernel(npt, lp_ref, gate_ref, tabc_ref, tabn_ref, h_ref, ys_hbm, yp_ref, ysm_ref, srt, acc, sem):
    i = pl.program_id(0)
    nt = pl.num_programs(0)
    slot = i % 2

    def fetch_runs(tab_ref, s):
        def run(e, carry):
            def piece(off, size):
                pltpu.make_async_copy(_row_tiles(ys_hbm, tab_ref[0, 2, e] + off, size),
                                      _row_tiles(srt.at[s], tab_ref[0, 0, e] + off, size), sem.at[s]).start()
            _for_run_pieces(tab_ref[0, 1, e], piece)
            return carry
        lax.fori_loop(0, N_EXPERTS, run, 0)

    @pl.when(i == 0)
    def _():
        fetch_runs(tabc_ref, 0)

    @pl.when(i + 1 < nt)
    def _():
        fetch_runs(tabn_ref, 1 - slot)

    for _ in range(TOP_K):
        pltpu.make_async_copy(_row_tiles(ys_hbm, 0, TOK_TILE), _row_tiles(srt.at[slot], 0, TOK_TILE),
                              sem.at[slot]).wait()

    def gather(j, carry):
        for u in range(SUBLANES):
            t = j * SUBLANES + u
            tot = None
            for k in range(TOP_K):
                p = lp_ref[0, 0, t * TOP_K + k]
                term = srt[slot, pl.ds(pl.multiple_of(p * SUBLANES, SUBLANES), SUBLANES), :] * gate_ref[0, 0, t * TOP_K + k]
                tot = term if tot is None else tot + term
            acc[pl.ds(pl.multiple_of(t * SUBLANES, SUBLANES), SUBLANES), :] = tot
        return carry

    lax.fori_loop(0, TOK_TILE // SUBLANES, gather, 0)
    y = h_ref[...] + _load_row_tiles(acc, TOK_TILE)

    @pl.when(i < npt)
    def _():
        yp_ref[...] = y

    @pl.when(i >= npt)
    def _():
        ysm_ref[...] = y


def _combine(lp_tiles, gate_tiles, tab, h, ys, n_prompt_rows):
    T = h.shape[0]
    nt = T // TOK_TILE
    npt = n_prompt_rows // TOK_TILE
    per_assign = pl.BlockSpec((1, 1, TOK_TILE * TOP_K), lambda i: (i, 0, 0), memory_space=pltpu.SMEM)
    tab_blk = lambda imap: pl.BlockSpec((1, SUBLANES, LANES), imap, memory_space=pltpu.SMEM)
    return pl.pallas_call(
        functools.partial(_combine_kernel, npt),
        grid=(nt,),
        in_specs=[per_assign, per_assign,
                  tab_blk(lambda i: (i, 0, 0)), tab_blk(lambda i: (jnp.minimum(i + 1, nt - 1), 0, 0)),
                  pl.BlockSpec((TOK_TILE, D_MODEL), lambda i: (i, 0)),
                  pl.BlockSpec(memory_space=pl.ANY)],
        out_specs=(pl.BlockSpec((TOK_TILE, D_MODEL), lambda i: (jnp.minimum(i, npt - 1), 0)),
                   pl.BlockSpec((TOK_TILE, D_MODEL), lambda i: (jnp.maximum(i - npt, 0), 0))),
        out_shape=(jax.ShapeDtypeStruct((n_prompt_rows, D_MODEL), F32),
                   jax.ShapeDtypeStruct((T - n_prompt_rows, D_MODEL), F32)),
        scratch_shapes=[pltpu.VMEM((2, TOK_TILE * TOP_K * SUBLANES, LANES), F32),
                        pltpu.VMEM((TOK_TILE * SUBLANES, LANES), F32), pltpu.SemaphoreType.DMA((2,))],
        compiler_params=pltpu.CompilerParams(dimension_semantics=("arbitrary",), vmem_limit_bytes=VMEM_LIMIT),
        name="moe_combine",
    )(lp_tiles, gate_tiles, tab, tab, h, ys)


def kernel(x_prompt, x_sample, cache_k_win, cache_v_win, state_conv, state_C, state_n, state_m, g_attn, w_in, b_i,
           b_f, q_norm_g, k_norm_g, sinks, conv_w, conv_b, m_norm_g, w_out, g_ffn, w_router, b_router, w_gate_up,
           b_gate_up, w_down, b_down):
    depth = g_attn.shape[0]
    assert depth == 1
    B, S, _ = x_prompt.shape
    DB, DS, _ = x_sample.shape
    TP = B * S
    TS = DB * DS
    T = TP + TS
    assert T % TOK_TILE == 0 and TP % TOK_TILE == 0 and S % ATT_QB == 0 and S % PROMPT_CHUNK == 0
    assert DS == SUBLANES and DB % SAMPLE_NB == 0 and (SAMPLE_NB * DS) == LANES
    l = 0

    xp = x_prompt.reshape(TP, D_MODEL)
    xs = x_sample.reshape(TS, D_MODEL)

    w_pad = jnp.pad(w_in[l], ((0, 0), (0, IN_PAD - w_in.shape[2]))).astype(BF16)
    wgt = jnp.transpose(w_in[l][:, GATE_COL:GATE_COL + 2 * M_HEADS]).astype(BF16)
    gi = jnp.arange(A_WIDTH) // HEAD_DIM
    gmat = (gi[:, None] == gi[None, :]).astype(BF16)
    qg = jnp.tile(q_norm_g[l], A_HEADS).reshape(1, A_WIDTH)
    kg = jnp.tile(k_norm_g[l], A_KV_HEADS).reshape(1, KV_WIDTH)
    gbias = jnp.concatenate([b_i[l], b_f[l]])
    gbrow = jnp.pad(gbias, (0, LANES - 2 * M_HEADS)).reshape(1, LANES)
    gbcol = gbias.reshape(2 * M_HEADS, 1)
    mng = m_norm_g[l].reshape(1, M_WIDTH)
    cw = conv_w[l]
    cb = conv_b[l].reshape(1, 2 * M_WIDTH)

    proj_w = (g_attn[l].reshape(1, D_MODEL), w_pad, wgt, gmat, qg, kg)
    qn, kn, va, qkm, vm, om, gcol, gt = _inproj(xp, *proj_w)
    qn_s, kn_s, va_s, qkm_s, vm_s, om_s, gcol_s, gt_s = _inproj(xs, *proj_w)

    a_p = _attn_prompt(sinks[l], qn, kn, va, B, S)
    ck = cache_k_win[l].reshape(DB, WINDOW, KV_WIDTH)
    cv = cache_v_win[l].reshape(DB, WINDOW, KV_WIDTH)
    a_s, kwin_s, vwin_s = _attn_sample(sinks[l], qn_s, kn_s, va_s, ck, cv, DB, DS)

    zc = jnp.zeros((B, SUBLANES, 2 * M_WIDTH), F32)
    m_p, C_p, n_p, mm_p = _mlstm(
        qkm, vm, om, gcol, gt, zc,
        jnp.zeros((B, M_HEADS, M_HEAD_DIM, M_HEAD_DIM), F32), jnp.zeros((B, M_HEADS, M_HEAD_DIM), F32),
        jnp.full((B, 1, LANES), NEG, F32), cw, cb, gbrow, gbcol, mng,
        nseq=1, L=PROMPT_CHUNK, ngroups=B, nchunks=S // PROMPT_CHUNK)
    conv_s0 = jnp.pad(state_conv[l], ((0, 0), (SUBLANES - (CONV_W - 1), 0), (0, 0)))
    m0_s = jnp.pad(state_m[l], ((0, 0), (0, LANES - M_HEADS))).reshape(DB, 1, LANES)
    m_s, C_s, n_s, mm_s = _mlstm(
        qkm_s, vm_s, om_s, gcol_s, gt_s, conv_s0, state_C[l], state_n[l], m0_s, cw, cb, gbrow, gbcol, mng,
        nseq=SAMPLE_NB, L=DS, ngroups=DB // SAMPLE_NB, nchunks=1)

    h, hn, route, cnt = _outproj(a_p, a_s, m_p, m_s, xp, xs, w_out[l].astype(BF16), g_ffn[l].reshape(1, D_MODEL),
                                 jnp.transpose(w_router[l]).astype(BF16), b_router[l].reshape(N_EXPERTS, 1))

    nblk = T * TOP_K // MOE_BLOCK + N_EXPERTS
    lp, tab, blk, info = _route_tables(route, cnt, nblk)
    block_e = blk[0, :nblk]
    padrow = info[:, 0]
    npad = info[:, 1]
    nused = info[N_EXPERTS - 1:N_EXPERTS, 2]
    per_assign = lambda a: jnp.transpose(a[:, :TOP_K, :], (0, 2, 1)).reshape(a.shape[0], 1, TOK_TILE * TOP_K)
    lp_tiles = per_assign(lp)
    gate_tiles = per_assign(route)
    xrows = _dispatch(padrow, npad, nused, lp_tiles, tab, hn, nblk * MOE_BLOCK)
    yrows = _moe_blocks(block_e, nused, xrows,
                        w_gate_up[l], b_gate_up[l].reshape(N_EXPERTS, 1, 2 * D_FF),
                        w_down[l], b_down[l].reshape(N_EXPERTS, 1, D_MODEL))
    y_p, y_s = _combine(lp_tiles, gate_tiles, tab, h, yrows, TP)

    y_p = y_p.reshape(B, S, D_MODEL)
    y_s = y_s.reshape(DB, DS, D_MODEL)
    def seq_tail(rows, n):
        return jnp.stack([rows[(b + 1) * S - n:(b + 1) * S] for b in range(B)])

    kwin_p = seq_tail(kn, WINDOW).reshape(B, WINDOW, A_KV_HEADS, HEAD_DIM)
    vwin_p = seq_tail(va, WINDOW).reshape(B, WINDOW, A_KV_HEADS, HEAD_DIM)
    qkm_s = qkm_s.reshape(DB, DS, 2 * M_WIDTH)
    return (y_p, y_s,
            kwin_p[None], vwin_p[None], seq_tail(qkm, CONV_W - 1)[None],
            C_p[None], n_p[None], mm_p[:, 0, :M_HEADS][None],
            kwin_s.reshape(DB, WINDOW, A_KV_HEADS, HEAD_DIM)[None],
            vwin_s.reshape(DB, WINDOW, A_KV_HEADS, HEAD_DIM)[None],
            qkm_s[:, -(CONV_W - 1):][None],
            C_s[None], n_s[None], mm_s[:, 0, :M_HEADS][None])
```

```python
import functools

import jax
import jax.numpy as jnp
from jax import lax
from jax.experimental import pallas as pl
from jax.experimental.pallas import tpu as pltpu

F32 = jnp.float32
BF16 = jnp.bfloat16

D_MODEL = 1024
HEAD_DIM = 64
A_HEADS = 8
A_KV_HEADS = 2
A_GROUP = A_HEADS // A_KV_HEADS
A_WIDTH = A_HEADS * HEAD_DIM
KV_WIDTH = A_KV_HEADS * HEAD_DIM
WINDOW = 128
M_HEADS = 4
M_HEAD_DIM = 128
M_WIDTH = M_HEADS * M_HEAD_DIM
CONV_W = 4
N_EXPERTS = 32
TOP_K = 4
D_FF = D_MODEL
SWIGLU_LIMIT = 7.0
SWIGLU_ALPHA = 1.702
MOE_BLOCK = 512
EPS = 1e-6
NEG = -1e30

LANES = 128
SUBLANES = 8
GATE_COL = A_WIDTH + 2 * KV_WIDTH + 4 * M_WIDTH
IN_PAD = GATE_COL + LANES
TOK_TILE = 512
PROJ_TILE = 1024
PROJ_VMEM_LIMIT = 58 * 1024 * 1024
ATT_QB = 512
ATT_SB = 128
SAMPLE_NB = 16
PROMPT_CHUNK = 256
VMEM_LIMIT = 48 * 1024 * 1024
MOE_VMEM_LIMIT = 56 * 1024 * 1024


def _dot(a, b):
    return jnp.dot(a, b, preferred_element_type=F32)


def _dot_nt(a, b):
    return lax.dot_general(a, b, (((1,), (1,)), ((), ())), preferred_element_type=F32)


def _dot_tn(a, b):
    return lax.dot_general(a, b, (((0,), (0,)), ((), ())), preferred_element_type=F32)


def _split3(x):
    hi = x.astype(BF16)
    r1 = x - hi.astype(F32)
    mid = r1.astype(BF16)
    lo = (r1 - mid.astype(F32)).astype(BF16)
    return hi, mid, lo


def _log_sigmoid(x):
    return jnp.minimum(x, 0.0) - jnp.log1p(jnp.exp(-jnp.abs(x)))


def _load_row_tiles(ref2, rows):
    return jnp.concatenate([ref2[pl.ds(s, rows, stride=SUBLANES), :] for s in range(SUBLANES)], axis=1)


def _store_row_tiles(ref2, rows, val):
    for s in range(SUBLANES):
        ref2[pl.ds(s, rows, stride=SUBLANES), :] = val[:, s * LANES:(s + 1) * LANES]


def _row_tiles(ref2, first, n):
    start = first * SUBLANES
    if not isinstance(start, int):
        start = pl.multiple_of(start, SUBLANES)
    return ref2.at[pl.ds(start, n * SUBLANES), :]


def _split_specs(n_prompt_tiles, rows, width):
    return (pl.BlockSpec((rows, width), lambda i, *_: (jnp.minimum(i, n_prompt_tiles - 1), 0)),
            pl.BlockSpec((rows, width), lambda i, *_: (jnp.maximum(i - n_prompt_tiles, 0), 0)))


def _inproj_kernel(x_ref, g_ref, w_ref, wgt_ref, gmat_ref, qg_ref, kg_ref,
                   qn_ref, kn_ref, va_ref, qkm_ref, vm_ref, om_ref, gcol_ref, gt_ref):
    x = x_ref[...]
    ms = jnp.mean(x * x, axis=-1, keepdims=True)
    xn = ((x * lax.rsqrt(ms + EPS)) * g_ref[...]).astype(BF16)

    def seg(lo, hi):
        return _dot(xn, w_ref[:, lo:hi])

    def head_norm(z, gmat, g):
        hi, mid, lo = _split3(z * z)
        ss = _dot(hi, gmat) + _dot(mid, gmat) + _dot(lo, gmat)
        return (z * lax.rsqrt(ss * (1.0 / HEAD_DIM) + EPS)) * g

    o0 = A_WIDTH
    o1 = o0 + KV_WIDTH
    o2 = o1 + KV_WIDTH
    o3 = o2 + 2 * M_WIDTH
    o4 = o3 + M_WIDTH
    o5 = o4 + M_WIDTH
    qn_ref[...] = head_norm(seg(0, o0), gmat_ref[...], qg_ref[...])
    kn_ref[...] = head_norm(seg(o0, o1), gmat_ref[:KV_WIDTH, :KV_WIDTH], kg_ref[...])
    va_ref[...] = seg(o1, o2)
    qkm_ref[...] = seg(o2, o3)
    vm_ref[...] = seg(o3, o4)
    om_ref[...] = seg(o4, o5)
    gcol_ref[...] = seg(o5, o5 + LANES)
    gt_ref[0] = _dot_nt(wgt_ref[...], xn)


def _inproj(x, g_attn, w_pad, wgt, gmat, qg, kg):
    T = x.shape[0]
    nt = T // PROJ_TILE
    row = lambda w: pl.BlockSpec((PROJ_TILE, w), lambda i: (i, 0))
    full = lambda a: pl.BlockSpec(a.shape, lambda i: (0,) * a.ndim, pipeline_mode=pl.Buffered(1))
    out_shape = (
        jax.ShapeDtypeStruct((T, A_WIDTH), F32),
        jax.ShapeDtypeStruct((T, KV_WIDTH), F32),
        jax.ShapeDtypeStruct((T, KV_WIDTH), F32),
        jax.ShapeDtypeStruct((T, 2 * M_WIDTH), F32),
        jax.ShapeDtypeStruct((T, M_WIDTH), F32),
        jax.ShapeDtypeStruct((T, M_WIDTH), F32),
        jax.ShapeDtypeStruct((T, LANES), F32),
        jax.ShapeDtypeStruct((nt, SUBLANES, PROJ_TILE), F32),
    )
    out_specs = (row(A_WIDTH), row(KV_WIDTH), row(KV_WIDTH), row(2 * M_WIDTH), row(M_WIDTH), row(M_WIDTH),
                 row(LANES), pl.BlockSpec((1, SUBLANES, PROJ_TILE), lambda i: (i, 0, 0)))
    return pl.pallas_call(
        _inproj_kernel,
        grid=(nt,),
        in_specs=[row(D_MODEL), full(g_attn), full(w_pad), full(wgt), full(gmat), full(qg), full(kg)],
        out_specs=out_specs,
        out_shape=out_shape,
        compiler_params=pltpu.CompilerParams(dimension_semantics=("parallel",), vmem_limit_bytes=PROJ_VMEM_LIMIT),
        name="inproj",
    )(x, g_attn, w_pad, wgt, gmat, qg, kg)


def _softmax_sink(pieces, masks, sink_col):
    masked = [jnp.where(mk, s, NEG) for s, mk in zip(pieces, masks)]
    m = sink_col
    for s in masked:
        m = jnp.maximum(m, jnp.max(s, axis=-1, keepdims=True))
    ps = [jnp.exp(s - m) for s in masked]
    den = jnp.exp(sink_col - m)
    for p in ps:
        den = den + jnp.sum(p, axis=-1, keepdims=True)
    return ps, 1.0 / den


def _stack_heads(q, g):
    return jnp.concatenate([q[:, (A_GROUP * g + i) * HEAD_DIM:(A_GROUP * g + i + 1) * HEAD_DIM]
                            for i in range(A_GROUP)], axis=0)


def _sink_col(sink_ref, g, rows_per_head):
    r = lax.broadcasted_iota(jnp.int32, (A_GROUP * rows_per_head, 1), 0)
    col = jnp.zeros((A_GROUP * rows_per_head, 1), F32)
    for i in range(A_GROUP):
        col = jnp.where(r // rows_per_head == i, sink_ref[A_GROUP * g + i], col)
    return col


def _attn_prompt_kernel(sink_ref, q_ref, kp_ref, kc_ref, vp_ref, vc_ref, o_ref):
    j = pl.program_id(1)
    scale = HEAD_DIM ** -0.5
    kall = jnp.concatenate([kp_ref[...], kc_ref[...]], axis=0).astype(BF16)
    vall = jnp.concatenate([vp_ref[...], vc_ref[...]], axis=0).astype(BF16)
    nrow = A_GROUP * ATT_SB
    r = lax.broadcasted_iota(jnp.int32, (nrow, 2 * ATT_SB), 0) % ATT_SB
    c = lax.broadcasted_iota(jnp.int32, (nrow, 2 * ATT_SB), 1)
    band = jnp.logical_and(c >= r, c <= r + WINDOW)
    band0 = jnp.logical_and(band, jnp.logical_or(c >= ATT_SB, j > 0))
    for sb in range(ATT_QB // ATT_SB):
        q = (q_ref[sb * ATT_SB:(sb + 1) * ATT_SB, :] * scale).astype(BF16)
        kwin = kall[sb * ATT_SB:(sb + 2) * ATT_SB]
        vwin = vall[sb * ATT_SB:(sb + 2) * ATT_SB]
        outs = []
        for g in range(A_KV_HEADS):
            lo, hi = g * HEAD_DIM, (g + 1) * HEAD_DIM
            s = _dot_nt(_stack_heads(q, g), kwin[:, lo:hi])
            (p,), inv = _softmax_sink([s], [band0 if sb == 0 else band], _sink_col(sink_ref, g, ATT_SB))
            o = _dot(p.astype(BF16), vwin[:, lo:hi]) * inv
            outs += [o[i * ATT_SB:(i + 1) * ATT_SB] for i in range(A_GROUP)]
        o_ref[sb * ATT_SB:(sb + 1) * ATT_SB, :] = jnp.concatenate(outs, axis=1)


def _attn_prompt(sinks, qn, kn, va, batch, seq):
    nq = seq // ATT_QB
    ratio = ATT_QB // ATT_SB
    cur = lambda w: pl.BlockSpec((ATT_QB, w), lambda b, j: (b * nq + j, 0))
    prev = lambda w: pl.BlockSpec((ATT_SB, w), lambda b, j: (jnp.maximum((b * nq + j) * ratio - 1, 0), 0))
    return pl.pallas_call(
        _attn_prompt_kernel,
        grid=(batch, nq),
        in_specs=[pl.BlockSpec(memory_space=pltpu.SMEM), cur(A_WIDTH), prev(KV_WIDTH), cur(KV_WIDTH),
                  prev(KV_WIDTH), cur(KV_WIDTH)],
        out_specs=cur(A_WIDTH),
        out_shape=jax.ShapeDtypeStruct((batch * seq, A_WIDTH), F32),
        compiler_params=pltpu.CompilerParams(dimension_semantics=("parallel", "parallel"),
                                             vmem_limit_bytes=VMEM_LIMIT),
        name="attn_prompt",
    )(sinks, qn, kn, kn, va, va)


def _attn_sample_kernel(dec, sink_ref, q_ref, kn_ref, vn_ref, ck_ref, cv_ref, o_ref, kw_ref, vw_ref):
    scale = HEAD_DIM ** -0.5
    rows = SAMPLE_NB * dec
    knew = kn_ref[...]
    vnew = vn_ref[...]
    knew_b = knew.astype(BF16)
    vnew_b = vnew.astype(BF16)
    nrow = A_GROUP * dec
    t = lax.broadcasted_iota(jnp.int32, (nrow, WINDOW), 0) % dec
    c = lax.broadcasted_iota(jnp.int32, (nrow, WINDOW), 1)
    m_cache = c >= t
    cn = lax.broadcasted_iota(jnp.int32, (nrow, rows), 1)
    tn = lax.broadcasted_iota(jnp.int32, (nrow, rows), 0) % dec
    for i in range(SAMPLE_NB):
        q = (q_ref[i * dec:(i + 1) * dec, :] * scale).astype(BF16)
        ck = ck_ref[i].astype(BF16)
        cv = cv_ref[i].astype(BF16)
        m_new = jnp.logical_and(cn // dec == i, cn % dec <= tn)
        outs = []
        for g in range(A_KV_HEADS):
            lo, hi = g * HEAD_DIM, (g + 1) * HEAD_DIM
            qs = _stack_heads(q, g)
            s_c = _dot_nt(qs, ck[:, lo:hi])
            s_n = _dot_nt(qs, knew_b[:, lo:hi])
            (p_c, p_n), inv = _softmax_sink([s_c, s_n], [m_cache, m_new], _sink_col(sink_ref, g, dec))
            o = (_dot(p_c.astype(BF16), cv[:, lo:hi]) + _dot(p_n.astype(BF16), vnew_b[:, lo:hi])) * inv
            outs += [o[h * dec:(h + 1) * dec] for h in range(A_GROUP)]
        o_ref[i * dec:(i + 1) * dec, :] = jnp.concatenate(outs, axis=1)
        kw_ref[i, 0:WINDOW - dec, :] = ck_ref[i, dec:WINDOW, :]
        kw_ref[i, WINDOW - dec:WINDOW, :] = knew[i * dec:(i + 1) * dec]
        vw_ref[i, 0:WINDOW - dec, :] = cv_ref[i, dec:WINDOW, :]
        vw_ref[i, WINDOW - dec:WINDOW, :] = vnew[i * dec:(i + 1) * dec]


def _attn_sample(sinks, qn, kn, va, ck, cv, dbatch, dec):
    rows = SAMPLE_NB * dec
    tokrow = lambda w: pl.BlockSpec((rows, w), lambda i: (i, 0))
    cache = pl.BlockSpec((SAMPLE_NB, WINDOW, KV_WIDTH), lambda i: (i, 0, 0))
    return pl.pallas_call(
        functools.partial(_attn_sample_kernel, dec),
        grid=(dbatch // SAMPLE_NB,),
        in_specs=[pl.BlockSpec(memory_space=pltpu.SMEM), tokrow(A_WIDTH), tokrow(KV_WIDTH), tokrow(KV_WIDTH),
                  cache, cache],
        out_specs=(pl.BlockSpec((rows, A_WIDTH), lambda i: (i, 0)), cache, cache),
        out_shape=(jax.ShapeDtypeStruct((dbatch * dec, A_WIDTH), F32),
                   jax.ShapeDtypeStruct((dbatch, WINDOW, KV_WIDTH), F32),
                   jax.ShapeDtypeStruct((dbatch, WINDOW, KV_WIDTH), F32)),
        compiler_params=pltpu.CompilerParams(dimension_semantics=("parallel",), vmem_limit_bytes=VMEM_LIMIT),
        name="attn_sample",
    )(sinks, qn, kn, va, ck, cv)


def _mlstm_kernel(nseq, L, qk_ref, v_ref, o_ref, gcol_ref, gt_ref, conv0_ref, c0_ref, n0_ref, m0_ref,
                  cw_ref, cb_ref, gbrow_ref, gbcol_ref, mng_ref, mask_ref,
                  out_ref, cst_ref, nst_ref, mst_ref, prev_ref):
    R = nseq * L
    ci = pl.program_id(1)

    @pl.when(ci == 0)
    def _():
        cst_ref[...] = c0_ref[...]
        nst_ref[...] = n0_ref[...]
        mst_ref[...] = m0_ref[...]
        prev_ref[...] = conv0_ref[0]

    raw = qk_ref[...]
    row = lax.broadcasted_iota(jnp.int32, (R, 1), 0)
    tpos = row % L
    rseq = row // L
    acc = raw * cw_ref[CONV_W - 1:CONV_W, :] + cb_ref[...]
    if nseq == 1:
        prev8 = prev_ref[...]
        t8 = lax.broadcasted_iota(jnp.int32, (SUBLANES, 1), 0)
    else:
        prevsrc = conv0_ref[...].reshape(R, 2 * M_WIDTH)
    for k in range(1, CONV_W):
        rolled = pltpu.roll(raw, k, 0)
        if nseq == 1:
            head = jnp.where(t8 >= k, rolled[0:SUBLANES], pltpu.roll(prev8, k, 0))
            sh = jnp.concatenate([head, rolled[SUBLANES:]], axis=0)
        else:
            sh = jnp.where(tpos >= k, rolled, pltpu.roll(prevsrc, R - SUBLANES + k, 0))
        acc = acc + sh * cw_ref[CONV_W - 1 - k:CONV_W - k, :]
    if nseq == 1:
        prev_ref[...] = raw[R - SUBLANES:R]
    qkc = acc * jax.nn.sigmoid(acc)

    gc = gcol_ref[...] + gbrow_ref[...]
    gr = gt_ref[0] + gbcol_ref[...]
    lsc = _log_sigmoid(gc)
    lsr = _log_sigmoid(gr)
    mb = mask_ref[...]
    maskb = mb > 0
    bcol = sum(_dot(mb, p) for p in _split3(lsc))
    brow = sum(_dot_nt(p, mb) for p in _split3(lsr))

    lane = lax.broadcasted_iota(jnp.int32, (1, LANES), 1)
    m_new = [jnp.zeros((1, LANES), F32) for _ in range(nseq)]
    for h in range(M_HEADS):
        sl = slice(h * M_HEAD_DIM, (h + 1) * M_HEAD_DIM)
        qh = qkc[:, sl]
        kh = qkc[:, M_WIDTH + h * M_HEAD_DIM:M_WIDTH + (h + 1) * M_HEAD_DIM] * (M_HEAD_DIM ** -0.5)
        vh = v_ref[:, sl]
        qb, kb, vb = qh.astype(BF16), kh.astype(BF16), vh.astype(BF16)
        ig_c = gc[:, h:h + 1]
        b_c = bcol[:, M_HEADS + h:M_HEADS + h + 1]
        ig_r = gr[h:h + 1, :]
        b_r = brow[M_HEADS + h:M_HEADS + h + 1, :]
        if nseq == 1:
            m0c = mst_ref[0][:, h:h + 1]
            n0rows = nst_ref[0, h:h + 1, :]
        else:
            m0c = jnp.zeros((R, 1), F32)
            n0rows = jnp.zeros((R, M_HEAD_DIM), F32)
            for s in range(nseq):
                m0c = jnp.where(rseq == s, mst_ref[s][:, h:h + 1], m0c)
                n0rows = jnp.where(rseq == s, nst_ref[s, h:h + 1, :], n0rows)
        dm = jnp.where(maskb, b_c - b_r + ig_r, NEG)
        a_c = b_c + m0c
        m_c = jnp.maximum(a_c, jnp.max(dm, axis=-1, keepdims=True))
        w = jnp.exp(dm - m_c)
        sc = jnp.exp(a_c - m_c)
        wqk = w * _dot_nt(qb, kb)
        if nseq == 1:
            inter = _dot_nt(qb, cst_ref[0, h].astype(BF16))
        else:
            inter = jnp.zeros((R, M_HEAD_DIM), F32)
            for s in range(nseq):
                qs = jnp.where(rseq == s, qh, 0.0).astype(BF16)
                inter = inter + _dot_nt(qs, cst_ref[s, h].astype(BF16))
        num = _dot(wqk.astype(BF16), vb) + sc * inter
        den = jnp.sum(wqk, axis=-1, keepdims=True) + sc * jnp.sum(qh * n0rows, axis=-1, keepdims=True)
        hh = num / jnp.maximum(jnp.abs(den), jnp.exp(-m_c))

        for s in range(nseq):
            e = s * L + L - 1
            m_end = m_c[e:e + 1, :]
            wend = jnp.exp(b_c[e:e + 1, :] - b_c + ig_c - m_end)
            if nseq > 1:
                wend = jnp.where(rseq == s, wend, 0.0)
            sce = jnp.exp(a_c[e:e + 1, :] - m_end)
            c_new = sce * cst_ref[s, h] + _dot_tn((vh * wend).astype(BF16), kb)
            n_new = sce * nst_ref[s, h:h + 1, :] + jnp.sum(wend * kh, axis=0, keepdims=True)
            cst_ref[s, h] = c_new
            nst_ref[s, h:h + 1, :] = n_new
            m_new[s] = jnp.where(lane == h, m_end, m_new[s])

        hn = (hh * lax.rsqrt(jnp.mean(hh * hh, axis=-1, keepdims=True) + EPS)) * mng_ref[:, sl]
        out_ref[:, sl] = jax.nn.sigmoid(o_ref[:, sl]) * hn
    for s in range(nseq):
        mst_ref[s] = m_new[s]


def _mlstm(qkm, vm, om, gcol, gt, conv0, c0, n0, m0, cw, cb, gbrow, gbcol, mng, nseq, L, ngroups, nchunks):
    R = nseq * L
    per_tile = gt.shape[2] // R
    tok = lambda w: pl.BlockSpec((R, w), lambda g, c: (g * nchunks + c, 0))
    gt_spec = pl.BlockSpec((1, SUBLANES, R),
                           lambda g, c: ((g * nchunks + c) // per_tile, 0, (g * nchunks + c) % per_tile))
    full = lambda a: pl.BlockSpec(a.shape, lambda g, c: (0,) * a.ndim)
    st4 = pl.BlockSpec((nseq, M_HEADS, M_HEAD_DIM, M_HEAD_DIM), lambda g, c: (g, 0, 0, 0))
    st3 = pl.BlockSpec((nseq, M_HEADS, M_HEAD_DIM), lambda g, c: (g, 0, 0))
    stm = pl.BlockSpec((nseq, 1, LANES), lambda g, c: (g, 0, 0))
    conv_spec = pl.BlockSpec((nseq, SUBLANES, 2 * M_WIDTH), lambda g, c: (g, 0, 0))
    r = jnp.arange(R)
    mask = ((r[:, None] // L == r[None, :] // L) & (r[None, :] <= r[:, None])).astype(BF16)
    nstate = ngroups * nseq
    return pl.pallas_call(
        functools.partial(_mlstm_kernel, nseq, L),
        grid=(ngroups, nchunks),
        in_specs=[tok(2 * M_WIDTH), tok(M_WIDTH), tok(M_WIDTH), tok(LANES), gt_spec, conv_spec, st4, st3, stm,
                  full(cw), full(cb), full(gbrow), full(gbcol), full(mng), full(mask)],
        out_specs=(pl.BlockSpec((R, M_WIDTH), lambda g, c: (g * nchunks + c, 0)), st4, st3, stm),
        out_shape=(jax.ShapeDtypeStruct((ngroups * nchunks * R, M_WIDTH), F32),
                   jax.ShapeDtypeStruct((nstate, M_HEADS, M_HEAD_DIM, M_HEAD_DIM), F32),
                   jax.ShapeDtypeStruct((nstate, M_HEADS, M_HEAD_DIM), F32),
                   jax.ShapeDtypeStruct((nstate, 1, LANES), F32)),
        scratch_shapes=[pltpu.VMEM((SUBLANES, 2 * M_WIDTH), F32)],
        compiler_params=pltpu.CompilerParams(dimension_semantics=("parallel", "arbitrary"),
                                             vmem_limit_bytes=VMEM_LIMIT),
        name="mlstm_n%d" % nseq,
    )(qkm, vm, om, gcol, gt, conv0, c0, n0, m0, cw, cb, gbrow, gbcol, mng, mask)


def _outproj_kernel(npt, ap_ref, as_ref, mp_ref, ms_ref, xp_ref, xs_ref, wo_ref, g_ref, wrt_ref, brc_ref,
                    h_ref, hn_ref, route_ref, cnt_ref):
    def project(a_ref, m_ref, x_ref):
        h_ref[...] = (x_ref[...] + _dot(a_ref[...].astype(BF16), wo_ref[0:A_WIDTH, :])
                      + _dot(m_ref[...].astype(BF16), wo_ref[A_WIDTH:A_WIDTH + M_WIDTH, :]))

    @pl.when(pl.program_id(0) < npt)
    def _():
        project(ap_ref, mp_ref, xp_ref)

    @pl.when(pl.program_id(0) >= npt)
    def _():
        project(as_ref, ms_ref, xs_ref)

    h = h_ref[...]
    hn = (h * lax.rsqrt(jnp.mean(h * h, axis=-1, keepdims=True) + EPS)) * g_ref[...]
    _store_row_tiles(hn_ref, TOK_TILE, hn)
    logits = _dot_nt(wrt_ref[...], hn.astype(BF16)) + brc_ref[...]
    eidx = lax.broadcasted_iota(jnp.int32, logits.shape, 0).astype(F32)
    picked = jnp.zeros(logits.shape, F32)
    top0 = None
    den = None
    es = []
    ids = []
    for k in range(TOP_K):
        mx = jnp.max(logits, axis=0, keepdims=True)
        idx = jnp.min(jnp.where(logits == mx, eidx, float(N_EXPERTS)), axis=0, keepdims=True)
        if k == 0:
            top0 = mx
        e = jnp.exp(mx - top0)
        den = e if den is None else den + e
        es.append(e)
        ids.append(idx)
        hit = eidx == idx
        picked = jnp.where(hit, 1.0, picked)
        logits = jnp.where(hit, -jnp.inf, logits)
    route_ref[0] = jnp.concatenate([e / den for e in es] + ids, axis=0)

    @pl.when(pl.program_id(0) == 0)
    def _():
        cnt_ref[...] = jnp.zeros_like(cnt_ref)

    cnt_ref[...] += jnp.broadcast_to(jnp.sum(picked, axis=1, keepdims=True), cnt_ref.shape)


def _outproj(a_p, a_s, m_p, m_s, xp, xs, w_out, g_ffn, wr_t, br_col):
    T = xp.shape[0] + xs.shape[0]
    nt = T // TOK_TILE
    npt = xp.shape[0] // TOK_TILE
    row = lambda w: pl.BlockSpec((TOK_TILE, w), lambda i: (i, 0))
    full = lambda a: pl.BlockSpec(a.shape, lambda i: (0,) * a.ndim)
    return pl.pallas_call(
        functools.partial(_outproj_kernel, npt),
        grid=(nt,),
        in_specs=[*_split_specs(npt, TOK_TILE, A_WIDTH), *_split_specs(npt, TOK_TILE, M_WIDTH),
                  *_split_specs(npt, TOK_TILE, D_MODEL), full(w_out), full(g_ffn), full(wr_t), full(br_col)],
        out_specs=(row(D_MODEL), pl.BlockSpec((TOK_TILE * SUBLANES, LANES), lambda i: (i, 0)),
                   pl.BlockSpec((1, 2 * TOP_K, TOK_TILE), lambda i: (i, 0, 0)),
                   pl.BlockSpec((N_EXPERTS, LANES), lambda i: (0, 0))),
        out_shape=(jax.ShapeDtypeStruct((T, D_MODEL), F32), jax.ShapeDtypeStruct((T * SUBLANES, LANES), F32),
                   jax.ShapeDtypeStruct((nt, 2 * TOP_K, TOK_TILE), F32),
                   jax.ShapeDtypeStruct((N_EXPERTS, LANES), F32)),
        compiler_params=pltpu.CompilerParams(dimension_semantics=("arbitrary",), vmem_limit_bytes=VMEM_LIMIT),
        name="outproj_router",
    )(a_p, a_s, m_p, m_s, xp, xs, w_out, g_ffn, wr_t, br_col)


def _route_kernel(nblk_pad, route_ref, cnt_ref, ustrict_ref, lstrict_ref, lp_ref, tab_ref, blk_ref, info_ref,
                  carry_ref):
    i = pl.program_id(0)
    cnt = cnt_ref[...]
    nb_e = jnp.floor((cnt + (MOE_BLOCK - 1.0)) * (1.0 / MOE_BLOCK))
    bstart = sum(_dot(lstrict_ref[...], p) for p in _split3(nb_e))
    bend = bstart + nb_e
    row_start = bstart * float(MOE_BLOCK)

    @pl.when(i == 0)
    def _():
        carry_ref[...] = jnp.zeros_like(carry_ref)
        bi = lax.broadcasted_iota(jnp.int32, (N_EXPERTS, nblk_pad), 1).astype(F32)
        done = jnp.where(bend[:, 0:1] <= bi, 1.0, 0.0)
        be = jnp.minimum(jnp.sum(done, axis=0, keepdims=True), N_EXPERTS - 1.0)
        blk_ref[...] = jnp.broadcast_to(be, blk_ref.shape).astype(jnp.int32)
        lane = lax.broadcasted_iota(jnp.int32, (N_EXPERTS, LANES), 1)
        info = jnp.where(lane == 0, row_start + cnt, 0.0)
        info = jnp.where(lane == 1, nb_e * float(MOE_BLOCK) - cnt, info)
        info = jnp.where(lane == 2, bend, info)
        info_ref[...] = info.astype(jnp.int32)

    r = route_ref[0]
    eidx = lax.broadcasted_iota(jnp.int32, (N_EXPERTS, TOK_TILE), 0).astype(F32)
    sel = [eidx == r[TOP_K + k:TOP_K + k + 1, :] for k in range(TOP_K)]
    oh = jnp.zeros((N_EXPERTS, TOK_TILE), F32)
    for k in range(TOP_K):
        oh = jnp.where(sel[k], 1.0, oh)
    cnt_t = jnp.broadcast_to(jnp.sum(oh, axis=1, keepdims=True), (N_EXPERTS, LANES))
    seg = sum(_dot(lstrict_ref[...], p) for p in _split3(cnt_t))
    local = _dot(oh.astype(BF16), ustrict_ref[...]) + seg[:, 0:1]
    rows = [jnp.sum(jnp.where(sel[k], local, 0.0), axis=0, keepdims=True) for k in range(TOP_K)]
    lp_ref[0] = jnp.concatenate(rows + [jnp.zeros((TOP_K, TOK_TILE), F32)], axis=0).astype(jnp.int32)
    diag = (lax.broadcasted_iota(jnp.int32, (N_EXPERTS, LANES), 0)
            == lax.broadcasted_iota(jnp.int32, (N_EXPERTS, LANES), 1))
    to_lanes = lambda col: jnp.sum(jnp.where(diag, col, 0.0), axis=0, keepdims=True)
    tab = [to_lanes(seg), to_lanes(cnt_t), to_lanes(row_start + carry_ref[...])]
    tab_ref[0] = jnp.concatenate(tab + [jnp.zeros((SUBLANES - len(tab), LANES), F32)], axis=0).astype(jnp.int32)
    carry_ref[...] += cnt_t


def _route_tables(route, cnt, nblk):
    nt = route.shape[0]
    nblk_pad = -(-nblk // LANES) * LANES
    a = jnp.arange(TOK_TILE)
    ustrict = (a[:, None] < a[None, :]).astype(BF16)
    b = jnp.arange(N_EXPERTS)
    lstrict = (b[:, None] > b[None, :]).astype(BF16)
    full = lambda x: pl.BlockSpec(x.shape, lambda i: (0,) * x.ndim)
    tile = pl.BlockSpec((1, 2 * TOP_K, TOK_TILE), lambda i: (i, 0, 0))
    return pl.pallas_call(
        functools.partial(_route_kernel, nblk_pad),
        grid=(nt,),
        in_specs=[tile, full(cnt), full(ustrict), full(lstrict)],
        out_specs=(tile, pl.BlockSpec((1, SUBLANES, LANES), lambda i: (i, 0, 0)),
                   pl.BlockSpec((SUBLANES, nblk_pad), lambda i: (0, 0)),
                   pl.BlockSpec((N_EXPERTS, LANES), lambda i: (0, 0))),
        out_shape=(jax.ShapeDtypeStruct((nt, 2 * TOP_K, TOK_TILE), jnp.int32),
                   jax.ShapeDtypeStruct((nt, SUBLANES, LANES), jnp.int32),
                   jax.ShapeDtypeStruct((SUBLANES, nblk_pad), jnp.int32),
                   jax.ShapeDtypeStruct((N_EXPERTS, LANES), jnp.int32)),
        scratch_shapes=[pltpu.VMEM((N_EXPERTS, LANES), F32)],
        compiler_params=pltpu.CompilerParams(dimension_semantics=("arbitrary",), vmem_limit_bytes=VMEM_LIMIT),
        name="route_tables",
    )(route, cnt, ustrict, lstrict)


RUN_UNIT = 16


def _for_run_pieces(count, fn):
    nfull = count // RUN_UNIT

    def full(q, carry):
        fn(q * RUN_UNIT, RUN_UNIT)
        return carry

    lax.fori_loop(0, nfull, full, 0)
    size = RUN_UNIT // 2
    while size >= 1:
        @pl.when((count & size) != 0)
        def _(size=size):
            fn(nfull * RUN_UNIT + (count & (RUN_UNIT - 1) & ~(2 * size - 1)), size)
        size //= 2


def _dispatch_kernel(padrow_ref, npad_ref, nu_ref, lp_ref, tab_ref, hn_ref, xs_hbm, srt, zbuf, sem, zsem):
    i = pl.program_id(0)
    nt = pl.num_programs(0)
    slot = i % 2

    @pl.when(i == 0)
    def _():
        zbuf[...] = jnp.zeros_like(zbuf)
        nblk = xs_hbm.shape[0] // (MOE_BLOCK * SUBLANES)

        def tail_start(b, c):
            pltpu.make_async_copy(zbuf, _row_tiles(xs_hbm, b * MOE_BLOCK, MOE_BLOCK), zsem).start()
            return c

        def tail_wait(b, c):
            pltpu.make_async_copy(zbuf, _row_tiles(xs_hbm, b * MOE_BLOCK, MOE_BLOCK), zsem).wait()
            return c

        lax.fori_loop(nu_ref[0], nblk, tail_start, 0)
        lax.fori_loop(nu_ref[0], nblk, tail_wait, 0)

        def pad_runs(wait):
            def per_expert(e, carry):
                def piece(off, size):
                    cp = pltpu.make_async_copy(_row_tiles(zbuf, 0, size), _row_tiles(xs_hbm, padrow_ref[e] + off, size),
                                               zsem)
                    cp.wait() if wait else cp.start()
                _for_run_pieces(npad_ref[e], piece)
                return carry
            lax.fori_loop(0, N_EXPERTS, per_expert, 0)

        pad_runs(False)
        pad_runs(True)

    def wait_runs(s):
        for _ in range(TOP_K):
            pltpu.make_async_copy(_row_tiles(srt.at[s], 0, TOK_TILE), _row_tiles(xs_hbm, 0, TOK_TILE), sem.at[s]).wait()

    def step(s):
        @pl.when(i >= 2)
        def _():
            wait_runs(s)

        def permute(j, carry):
            for u in range(SUBLANES):
                t = j * SUBLANES + u
                row = hn_ref[pl.ds(pl.multiple_of(t * SUBLANES, SUBLANES), SUBLANES), :]
                for k in range(TOP_K):
                    p = lp_ref[0, 0, t * TOP_K + k]
                    srt[s, pl.ds(pl.multiple_of(p * SUBLANES, SUBLANES), SUBLANES), :] = row
            return carry

        lax.fori_loop(0, TOK_TILE // SUBLANES, permute, 0)

        def send_run(e, carry):
            def piece(off, size):
                pltpu.make_async_copy(_row_tiles(srt.at[s], tab_ref[0, 0, e] + off, size),
                                      _row_tiles(xs_hbm, tab_ref[0, 2, e] + off, size), sem.at[s]).start()
            _for_run_pieces(tab_ref[0, 1, e], piece)
            return carry

        lax.fori_loop(0, N_EXPERTS, send_run, 0)

        @pl.when(i == nt - 1)
        def _():
            @pl.when(nt >= 2)
            def _():
                wait_runs(1 - s)

            wait_runs(s)

    for s in range(2):
        pl.when(slot == s)(functools.partial(step, s))


def _dispatch(padrow, npad, nused, lp_tiles, tab, hn, n_rows):
    nt = lp_tiles.shape[0]
    grid_spec = pltpu.PrefetchScalarGridSpec(
        num_scalar_prefetch=3,
        grid=(nt,),
        in_specs=[pl.BlockSpec((1, 1, TOK_TILE * TOP_K), lambda i, *_: (i, 0, 0), memory_space=pltpu.SMEM),
                  pl.BlockSpec((1, SUBLANES, LANES), lambda i, *_: (i, 0, 0), memory_space=pltpu.SMEM),
                  pl.BlockSpec((TOK_TILE * SUBLANES, LANES), lambda i, *_: (i, 0))],
        out_specs=pl.BlockSpec(memory_space=pl.ANY),
        scratch_shapes=[pltpu.VMEM((2, TOK_TILE * TOP_K * SUBLANES, LANES), F32),
                        pltpu.VMEM((MOE_BLOCK * SUBLANES, LANES), F32),
                        pltpu.SemaphoreType.DMA((2,)), pltpu.SemaphoreType.DMA(())],
    )
    return pl.pallas_call(
        _dispatch_kernel,
        grid_spec=grid_spec,
        out_shape=jax.ShapeDtypeStruct((n_rows * SUBLANES, LANES), F32),
        compiler_params=pltpu.CompilerParams(dimension_semantics=("arbitrary",), vmem_limit_bytes=VMEM_LIMIT),
        name="moe_dispatch",
    )(padrow, npad, nused, lp_tiles, tab, hn)


def _moe_kernel(be_ref, nu_ref, first_ref, slot_ref, nxt_ref, x_ref, bgu_ref, bd_ref, wgu_hbm, wd_hbm, y_ref,
                wgu_buf, wd_buf, wsem):
    i = pl.program_id(0)
    used = i < nu_ref[0]
    s = slot_ref[i]

    def fetch(e, sl):
        return (pltpu.make_async_copy(wgu_hbm.at[e], wgu_buf.at[sl], wsem.at[0, sl]),
                pltpu.make_async_copy(wd_hbm.at[e], wd_buf.at[sl], wsem.at[1, sl]))

    @pl.when(jnp.logical_and(used, first_ref[i] == 1))
    def _():
        @pl.when(i == 0)
        def _():
            for c in fetch(be_ref[0], 0):
                c.start()

        for c in fetch(be_ref[i], s):
            c.wait()

        @pl.when(nxt_ref[i] >= 0)
        def _():
            for c in fetch(nxt_ref[i], 1 - s):
                c.start()

    @pl.when(used)
    def _():
        x = _load_row_tiles(x_ref, MOE_BLOCK).astype(BF16)
        hb = _dot(x, wgu_buf[s].astype(BF16)) + bgu_ref[0]
        glu = jnp.minimum(hb[:, :D_FF], SWIGLU_LIMIT)
        lin = jnp.clip(hb[:, D_FF:], -SWIGLU_LIMIT, SWIGLU_LIMIT)
        act = glu * jax.nn.sigmoid(SWIGLU_ALPHA * glu) * (lin + 1.0)
        _store_row_tiles(y_ref, MOE_BLOCK, _dot(act.astype(BF16), wd_buf[s].astype(BF16)) + bd_ref[0])

    @pl.when(i >= nu_ref[0])
    def _():
        y_ref[...] = jnp.zeros_like(y_ref)


def _moe_blocks(block_e, nused, xs, wgu, bgu, wd, bd):
    nblk = block_e.shape[0]
    idx = jnp.arange(nblk, dtype=jnp.int32)
    first = (idx < nused[0]) & ((idx == 0) | (block_e != jnp.roll(block_e, 1)))
    slot = ((jnp.cumsum(first.astype(jnp.int32)) - 1) % 2).astype(jnp.int32)
    first_pos = jnp.where(first, idx, nblk)
    later = jnp.concatenate([first_pos[1:], jnp.full((1,), nblk, jnp.int32)])
    next_pos = lax.cummin(later, reverse=True)
    nxt = jnp.sum(jnp.where(idx[None, :] == next_pos[:, None], block_e[None, :] + 1, 0), axis=1) - 1
    grid_spec = pltpu.PrefetchScalarGridSpec(
        num_scalar_prefetch=5,
        grid=(nblk,),
        in_specs=[
            pl.BlockSpec((MOE_BLOCK * SUBLANES, LANES), lambda i, be, *_: (i, 0)),
            pl.BlockSpec((1, 1, 2 * D_FF), lambda i, be, *_: (be[i], 0, 0)),
            pl.BlockSpec((1, 1, D_MODEL), lambda i, be, *_: (be[i], 0, 0)),
            pl.BlockSpec(memory_space=pl.ANY),
            pl.BlockSpec(memory_space=pl.ANY),
        ],
        out_specs=pl.BlockSpec((MOE_BLOCK * SUBLANES, LANES), lambda i, be, *_: (i, 0)),
        scratch_shapes=[pltpu.VMEM((2, D_MODEL, 2 * D_FF), F32), pltpu.VMEM((2, D_FF, D_MODEL), F32),
                        pltpu.SemaphoreType.DMA((2, 2))],
    )
    return pl.pallas_call(
        _moe_kernel,
        grid_spec=grid_spec,
        out_shape=jax.ShapeDtypeStruct(xs.shape, F32),
        compiler_params=pltpu.CompilerParams(dimension_semantics=("arbitrary",), vmem_limit_bytes=MOE_VMEM_LIMIT),
        name="moe_blocks",
    )(block_e, nused, first.astype(jnp.int32), slot, nxt.astype(jnp.int32), xs, bgu, bd, wgu, wd)


def _combine_kernel(npt, lp_ref, gate_ref, tabc_ref, tabn_ref, h_ref, ys_hbm, yp_ref, ysm_ref, srt, acc, sem):
    i = pl.program_id(0)
    nt = pl.num_programs(0)
    slot = i % 2

    def fetch_runs(tab_ref, s):
        def run(e, carry):
            def piece(off, size):
                pltpu.make_async_copy(_row_tiles(ys_hbm, tab_ref[0, 2, e] + off, size),
                                      _row_tiles(srt.at[s], tab_ref[0, 0, e] + off, size), sem.at[s]).start()
            _for_run_pieces(tab_ref[0, 1, e], piece)
            return carry
        lax.fori_loop(0, N_EXPERTS, run, 0)

    @pl.when(i == 0)
    def _():
        fetch_runs(tabc_ref, 0)

    def step(s):
        @pl.when(i + 1 < nt)
        def _():
            fetch_runs(tabn_ref, 1 - s)

        for _ in range(TOP_K):
            pltpu.make_async_copy(_row_tiles(ys_hbm, 0, TOK_TILE), _row_tiles(srt.at[s], 0, TOK_TILE),
                                  sem.at[s]).wait()

        def gather(j, carry):
            for u in range(SUBLANES):
                t = j * SUBLANES + u
                tot = None
                for k in range(TOP_K):
                    p = lp_ref[0, 0, t * TOP_K + k]
                    term = (srt[s, pl.ds(pl.multiple_of(p * SUBLANES, SUBLANES), SUBLANES), :]
                            * gate_ref[0, 0, t * TOP_K + k])
                    tot = term if tot is None else tot + term
                acc[pl.ds(pl.multiple_of(t * SUBLANES, SUBLANES), SUBLANES), :] = tot
            return carry

        lax.fori_loop(0, TOK_TILE // SUBLANES, gather, 0)

    for s in range(2):
        pl.when(slot == s)(functools.partial(step, s))
    y = h_ref[...] + _load_row_tiles(acc, TOK_TILE)

    @pl.when(i < npt)
    def _():
        yp_ref[...] = y

    @pl.when(i >= npt)
    def _():
        ysm_ref[...] = y


def _combine(lp_tiles, gate_tiles, tab, h, ys, n_prompt_rows):
    T = h.shape[0]
    nt = T // TOK_TILE
    npt = n_prompt_rows // TOK_TILE
    per_assign = pl.BlockSpec((1, 1, TOK_TILE * TOP_K), lambda i: (i, 0, 0), memory_space=pltpu.SMEM)
    tab_blk = lambda imap: pl.BlockSpec((1, SUBLANES, LANES), imap, memory_space=pltpu.SMEM)
    return pl.pallas_call(
        functools.partial(_combine_kernel, npt),
        grid=(nt,),
        in_specs=[per_assign, per_assign,
                  tab_blk(lambda i: (i, 0, 0)), tab_blk(lambda i: (jnp.minimum(i + 1, nt - 1), 0, 0)),
                  pl.BlockSpec((TOK_TILE, D_MODEL), lambda i: (i, 0)),
                  pl.BlockSpec(memory_space=pl.ANY)],
        out_specs=(pl.BlockSpec((TOK_TILE, D_MODEL), lambda i: (jnp.minimum(i, npt - 1), 0)),
                   pl.BlockSpec((TOK_TILE, D_MODEL), lambda i: (jnp.maximum(i - npt, 0), 0))),
        out_shape=(jax.ShapeDtypeStruct((n_prompt_rows, D_MODEL), F32),
                   jax.ShapeDtypeStruct((T - n_prompt_rows, D_MODEL), F32)),
        scratch_shapes=[pltpu.VMEM((2, TOK_TILE * TOP_K * SUBLANES, LANES), F32),
                        pltpu.VMEM((TOK_TILE * SUBLANES, LANES), F32), pltpu.SemaphoreType.DMA((2,))],
        compiler_params=pltpu.CompilerParams(dimension_semantics=("arbitrary",), vmem_limit_bytes=VMEM_LIMIT),
        name="moe_combine",
    )(lp_tiles, gate_tiles, tab, tab, h, ys)


def kernel(x_prompt, x_sample, cache_k_win, cache_v_win, state_conv, state_C, state_n, state_m, g_attn, w_in, b_i,
           b_f, q_norm_g, k_norm_g, sinks, conv_w, conv_b, m_norm_g, w_out, g_ffn, w_router, b_router, w_gate_up,
           b_gate_up, w_down, b_down):
    depth = g_attn.shape[0]
    assert depth == 1
    B, S, _ = x_prompt.shape
    DB, DS, _ = x_sample.shape
    TP = B * S
    TS = DB * DS
    T = TP + TS
    assert T % TOK_TILE == 0 and TP % TOK_TILE == 0 and S % ATT_QB == 0 and S % PROMPT_CHUNK == 0
    assert DS == SUBLANES and DB % SAMPLE_NB == 0 and (SAMPLE_NB * DS) == LANES
    l = 0

    xp = x_prompt.reshape(TP, D_MODEL)
    xs = x_sample.reshape(TS, D_MODEL)

    w_pad = jnp.pad(w_in[l], ((0, 0), (0, IN_PAD - w_in.shape[2]))).astype(BF16)
    wgt = jnp.transpose(w_in[l][:, GATE_COL:GATE_COL + 2 * M_HEADS]).astype(BF16)
    gi = jnp.arange(A_WIDTH) // HEAD_DIM
    gmat = (gi[:, None] == gi[None, :]).astype(BF16)
    qg = jnp.tile(q_norm_g[l], A_HEADS).reshape(1, A_WIDTH)
    kg = jnp.tile(k_norm_g[l], A_KV_HEADS).reshape(1, KV_WIDTH)
    gbias = jnp.concatenate([b_i[l], b_f[l]])
    gbrow = jnp.pad(gbias, (0, LANES - 2 * M_HEADS)).reshape(1, LANES)
    gbcol = gbias.reshape(2 * M_HEADS, 1)
    mng = m_norm_g[l].reshape(1, M_WIDTH)
    cw = conv_w[l]
    cb = conv_b[l].reshape(1, 2 * M_WIDTH)

    proj_w = (g_attn[l].reshape(1, D_MODEL), w_pad, wgt, gmat, qg, kg)
    qn, kn, va, qkm, vm, om, gcol, gt = _inproj(xp, *proj_w)
    qn_s, kn_s, va_s, qkm_s, vm_s, om_s, gcol_s, gt_s = _inproj(xs, *proj_w)

    a_p = _attn_prompt(sinks[l], qn, kn, va, B, S)
    ck = cache_k_win[l].reshape(DB, WINDOW, KV_WIDTH)
    cv = cache_v_win[l].reshape(DB, WINDOW, KV_WIDTH)
    a_s, kwin_s, vwin_s = _attn_sample(sinks[l], qn_s, kn_s, va_s, ck, cv, DB, DS)

    zc = jnp.zeros((B, SUBLANES, 2 * M_WIDTH), F32)
    m_p, C_p, n_p, mm_p = _mlstm(
        qkm, vm, om, gcol, gt, zc,
        jnp.zeros((B, M_HEADS, M_HEAD_DIM, M_HEAD_DIM), F32), jnp.zeros((B, M_HEADS, M_HEAD_DIM), F32),
        jnp.full((B, 1, LANES), NEG, F32), cw, cb, gbrow, gbcol, mng,
        nseq=1, L=PROMPT_CHUNK, ngroups=B, nchunks=S // PROMPT_CHUNK)
    conv_s0 = jnp.pad(state_conv[l], ((0, 0), (SUBLANES - (CONV_W - 1), 0), (0, 0)))
    m0_s = jnp.pad(state_m[l], ((0, 0), (0, LANES - M_HEADS))).reshape(DB, 1, LANES)
    m_s, C_s, n_s, mm_s = _mlstm(
        qkm_s, vm_s, om_s, gcol_s, gt_s, conv_s0, state_C[l], state_n[l], m0_s, cw, cb, gbrow, gbcol, mng,
        nseq=SAMPLE_NB, L=DS, ngroups=DB // SAMPLE_NB, nchunks=1)

    h, hn, route, cnt = _outproj(a_p, a_s, m_p, m_s, xp, xs, w_out[l].astype(BF16), g_ffn[l].reshape(1, D_MODEL),
                                 jnp.transpose(w_router[l]).astype(BF16), b_router[l].reshape(N_EXPERTS, 1))

    nblk = T * TOP_K // MOE_BLOCK + N_EXPERTS
    lp, tab, blk, info = _route_tables(route, cnt, nblk)
    block_e = blk[0, :nblk]
    padrow = info[:, 0]
    npad = info[:, 1]
    nused = info[N_EXPERTS - 1:N_EXPERTS, 2]
    per_assign = lambda a: jnp.transpose(a[:, :TOP_K, :], (0, 2, 1)).reshape(a.shape[0], 1, TOK_TILE * TOP_K)
    lp_tiles = per_assign(lp)
    gate_tiles = per_assign(route)
    xrows = _dispatch(padrow, npad, nused, lp_tiles, tab, hn, nblk * MOE_BLOCK)
    yrows = _moe_blocks(block_e, nused, xrows,
                        w_gate_up[l], b_gate_up[l].reshape(N_EXPERTS, 1, 2 * D_FF),
                        w_down[l], b_down[l].reshape(N_EXPERTS, 1, D_MODEL))
    y_p, y_s = _combine(lp_tiles, gate_tiles, tab, h, yrows, TP)

    y_p = y_p.reshape(B, S, D_MODEL)
    y_s = y_s.reshape(DB, DS, D_MODEL)
    def seq_tail(rows, n):
        return jnp.stack([rows[(b + 1) * S - n:(b + 1) * S] for b in range(B)])

    kwin_p = seq_tail(kn, WINDOW).reshape(B, WINDOW, A_KV_HEADS, HEAD_DIM)
    vwin_p = seq_tail(va, WINDOW).reshape(B, WINDOW, A_KV_HEADS, HEAD_DIM)
    qkm_s = qkm_s.reshape(DB, DS, 2 * M_WIDTH)
    return (y_p, y_s,
            kwin_p[None], vwin_p[None], seq_tail(qkm, CONV_W - 1)[None],
            C_p[None], n_p[None], mm_p[:, 0, :M_HEADS][None],
            kwin_s.reshape(DB, WINDOW, A_KV_HEADS, HEAD_DIM)[None],
            vwin_s.reshape(DB, WINDOW, A_KV_HEADS, HEAD_DIM)[None],
            qkm_s[:, -(CONV_W - 1):][None],
            C_s[None], n_s[None], mm_s[:, 0, :M_HEADS][None])
```

```python
import functools

import jax
import jax.numpy as jnp
from jax import lax
from jax.experimental import pallas as pl
from jax.experimental.pallas import tpu as pltpu

F32 = jnp.float32
BF16 = jnp.bfloat16

D_MODEL = 1024
HEAD_DIM = 64
A_HEADS = 8
A_KV_HEADS = 2
A_GROUP = A_HEADS // A_KV_HEADS
A_WIDTH = A_HEADS * HEAD_DIM
KV_WIDTH = A_KV_HEADS * HEAD_DIM
WINDOW = 128
M_HEADS = 4
M_HEAD_DIM = 128
M_WIDTH = M_HEADS * M_HEAD_DIM
CONV_W = 4
N_EXPERTS = 32
TOP_K = 4
D_FF = D_MODEL
SWIGLU_LIMIT = 7.0
SWIGLU_ALPHA = 1.702
MOE_BLOCK = 512
EPS = 1e-6
NEG = -1e30

LANES = 128
SUBLANES = 8
GATE_COL = A_WIDTH + 2 * KV_WIDTH + 4 * M_WIDTH
IN_PAD = GATE_COL + LANES
TOK_TILE = 512
PROJ_TILE = 1024
PROJ_VMEM_LIMIT = 58 * 1024 * 1024
ATT_QB = 512
ATT_SB = 128
SAMPLE_NB = 16
PROMPT_CHUNK = 256
VMEM_LIMIT = 48 * 1024 * 1024
MOE_VMEM_LIMIT = 56 * 1024 * 1024


def _dot(a, b):
    return jnp.dot(a, b, preferred_element_type=F32)


def _dot_nt(a, b):
    return lax.dot_general(a, b, (((1,), (1,)), ((), ())), preferred_element_type=F32)


def _dot_tn(a, b):
    return lax.dot_general(a, b, (((0,), (0,)), ((), ())), preferred_element_type=F32)


def _split3(x):
    hi = x.astype(BF16)
    r1 = x - hi.astype(F32)
    mid = r1.astype(BF16)
    lo = (r1 - mid.astype(F32)).astype(BF16)
    return hi, mid, lo


def _log_sigmoid(x):
    return jnp.minimum(x, 0.0) - jnp.log1p(jnp.exp(-jnp.abs(x)))


def _load_row_tiles(ref2, rows):
    return jnp.concatenate([ref2[pl.ds(s, rows, stride=SUBLANES), :] for s in range(SUBLANES)], axis=1)


def _store_row_tiles(ref2, rows, val):
    for s in range(SUBLANES):
        ref2[pl.ds(s, rows, stride=SUBLANES), :] = val[:, s * LANES:(s + 1) * LANES]


def _row_tiles(ref2, first, n):
    start = first * SUBLANES
    if not isinstance(start, int):
        start = pl.multiple_of(start, SUBLANES)
    return ref2.at[pl.ds(start, n * SUBLANES), :]


def _split_specs(n_prompt_tiles, rows, width):
    return (pl.BlockSpec((rows, width), lambda i, *_: (jnp.minimum(i, n_prompt_tiles - 1), 0)),
            pl.BlockSpec((rows, width), lambda i, *_: (jnp.maximum(i - n_prompt_tiles, 0), 0)))


def _inproj_kernel(x_ref, g_ref, w_ref, wgt_ref, gmat_ref, qg_ref, kg_ref,
                   qn_ref, kn_ref, va_ref, qkm_ref, vm_ref, om_ref, gcol_ref, gt_ref):
    x = x_ref[...]
    ms = jnp.mean(x * x, axis=-1, keepdims=True)
    xn = ((x * lax.rsqrt(ms + EPS)) * g_ref[...]).astype(BF16)

    def seg(lo, hi):
        return _dot(xn, w_ref[:, lo:hi])

    def head_norm(z, gmat, g):
        hi, mid, lo = _split3(z * z)
        ss = _dot(hi, gmat) + _dot(mid, gmat) + _dot(lo, gmat)
        return (z * lax.rsqrt(ss * (1.0 / HEAD_DIM) + EPS)) * g

    o0 = A_WIDTH
    o1 = o0 + KV_WIDTH
    o2 = o1 + KV_WIDTH
    o3 = o2 + 2 * M_WIDTH
    o4 = o3 + M_WIDTH
    o5 = o4 + M_WIDTH
    qn_ref[...] = head_norm(seg(0, o0), gmat_ref[...], qg_ref[...])
    kn_ref[...] = head_norm(seg(o0, o1), gmat_ref[:KV_WIDTH, :KV_WIDTH], kg_ref[...])
    va_ref[...] = seg(o1, o2)
    qkm_ref[...] = seg(o2, o3)
    vm_ref[...] = seg(o3, o4)
    om_ref[...] = seg(o4, o5)
    gcol_ref[...] = seg(o5, o5 + LANES)
    gt_ref[0] = _dot_nt(wgt_ref[...], xn)


def _inproj(x, g_attn, w_pad, wgt, gmat, qg, kg):
    T = x.shape[0]
    nt = T // PROJ_TILE
    row = lambda w: pl.BlockSpec((PROJ_TILE, w), lambda i: (i, 0))
    full = lambda a: pl.BlockSpec(a.shape, lambda i: (0,) * a.ndim, pipeline_mode=pl.Buffered(1))
    out_shape = (
        jax.ShapeDtypeStruct((T, A_WIDTH), F32),
        jax.ShapeDtypeStruct((T, KV_WIDTH), F32),
        jax.ShapeDtypeStruct((T, KV_WIDTH), F32),
        jax.ShapeDtypeStruct((T, 2 * M_WIDTH), F32),
        jax.ShapeDtypeStruct((T, M_WIDTH), F32),
        jax.ShapeDtypeStruct((T, M_WIDTH), F32),
        jax.ShapeDtypeStruct((T, LANES), F32),
        jax.ShapeDtypeStruct((nt, SUBLANES, PROJ_TILE), F32),
    )
    out_specs = (row(A_WIDTH), row(KV_WIDTH), row(KV_WIDTH), row(2 * M_WIDTH), row(M_WIDTH), row(M_WIDTH),
                 row(LANES), pl.BlockSpec((1, SUBLANES, PROJ_TILE), lambda i: (i, 0, 0)))
    return pl.pallas_call(
        _inproj_kernel,
        grid=(nt,),
        in_specs=[row(D_MODEL), full(g_attn), full(w_pad), full(wgt), full(gmat), full(qg), full(kg)],
        out_specs=out_specs,
        out_shape=out_shape,
        compiler_params=pltpu.CompilerParams(dimension_semantics=("parallel",), vmem_limit_bytes=PROJ_VMEM_LIMIT),
        name="inproj",
    )(x, g_attn, w_pad, wgt, gmat, qg, kg)


def _softmax_sink(pieces, masks, sink_col):
    masked = [jnp.where(mk, s, NEG) for s, mk in zip(pieces, masks)]
    m = sink_col
    for s in masked:
        m = jnp.maximum(m, jnp.max(s, axis=-1, keepdims=True))
    ps = [jnp.exp(s - m) for s in masked]
    den = jnp.exp(sink_col - m)
    for p in ps:
        den = den + jnp.sum(p, axis=-1, keepdims=True)
    return ps, 1.0 / den


def _stack_heads(q, g):
    return jnp.concatenate([q[:, (A_GROUP * g + i) * HEAD_DIM:(A_GROUP * g + i + 1) * HEAD_DIM]
                            for i in range(A_GROUP)], axis=0)


def _sink_col(sink_ref, g, rows_per_head):
    r = lax.broadcasted_iota(jnp.int32, (A_GROUP * rows_per_head, 1), 0)
    col = jnp.zeros((A_GROUP * rows_per_head, 1), F32)
    for i in range(A_GROUP):
        col = jnp.where(r // rows_per_head == i, sink_ref[A_GROUP * g + i], col)
    return col


def _attn_prompt_kernel(sink_ref, q_ref, kp_ref, kc_ref, vp_ref, vc_ref, o_ref):
    j = pl.program_id(1)
    scale = HEAD_DIM ** -0.5
    kall = jnp.concatenate([kp_ref[...], kc_ref[...]], axis=0).astype(BF16)
    vall = jnp.concatenate([vp_ref[...], vc_ref[...]], axis=0).astype(BF16)
    nrow = A_GROUP * ATT_SB
    r = lax.broadcasted_iota(jnp.int32, (nrow, 2 * ATT_SB), 0) % ATT_SB
    c = lax.broadcasted_iota(jnp.int32, (nrow, 2 * ATT_SB), 1)
    band = jnp.logical_and(c >= r, c <= r + WINDOW)
    band0 = jnp.logical_and(band, jnp.logical_or(c >= ATT_SB, j > 0))
    for sb in range(ATT_QB // ATT_SB):
        q = (q_ref[sb * ATT_SB:(sb + 1) * ATT_SB, :] * scale).astype(BF16)
        kwin = kall[sb * ATT_SB:(sb + 2) * ATT_SB]
        vwin = vall[sb * ATT_SB:(sb + 2) * ATT_SB]
        outs = []
        for g in range(A_KV_HEADS):
            lo, hi = g * HEAD_DIM, (g + 1) * HEAD_DIM
            s = _dot_nt(_stack_heads(q, g), kwin[:, lo:hi])
            (p,), inv = _softmax_sink([s], [band0 if sb == 0 else band], _sink_col(sink_ref, g, ATT_SB))
            o = _dot(p.astype(BF16), vwin[:, lo:hi]) * inv
            outs += [o[i * ATT_SB:(i + 1) * ATT_SB] for i in range(A_GROUP)]
        o_ref[sb * ATT_SB:(sb + 1) * ATT_SB, :] = jnp.concatenate(outs, axis=1)


def _attn_prompt(sinks, qn, kn, va, batch, seq):
    nq = seq // ATT_QB
    ratio = ATT_QB // ATT_SB
    cur = lambda w: pl.BlockSpec((ATT_QB, w), lambda b, j: (b * nq + j, 0))
    prev = lambda w: pl.BlockSpec((ATT_SB, w), lambda b, j: (jnp.maximum((b * nq + j) * ratio - 1, 0), 0))
    return pl.pallas_call(
        _attn_prompt_kernel,
        grid=(batch, nq),
        in_specs=[pl.BlockSpec(memory_space=pltpu.SMEM), cur(A_WIDTH), prev(KV_WIDTH), cur(KV_WIDTH),
                  prev(KV_WIDTH), cur(KV_WIDTH)],
        out_specs=cur(A_WIDTH),
        out_shape=jax.ShapeDtypeStruct((batch * seq, A_WIDTH), F32),
        compiler_params=pltpu.CompilerParams(dimension_semantics=("parallel", "parallel"),
                                             vmem_limit_bytes=VMEM_LIMIT),
        name="attn_prompt",
    )(sinks, qn, kn, kn, va, va)


def _attn_sample_kernel(dec, sink_ref, q_ref, kn_ref, vn_ref, ck_ref, cv_ref, o_ref, kw_ref, vw_ref):
    scale = HEAD_DIM ** -0.5
    rows = SAMPLE_NB * dec
    knew = kn_ref[...]
    vnew = vn_ref[...]
    knew_b = knew.astype(BF16)
    vnew_b = vnew.astype(BF16)
    nrow = A_GROUP * dec
    t = lax.broadcasted_iota(jnp.int32, (nrow, WINDOW), 0) % dec
    c = lax.broadcasted_iota(jnp.int32, (nrow, WINDOW), 1)
    m_cache = c >= t
    cn = lax.broadcasted_iota(jnp.int32, (nrow, rows), 1)
    tn = lax.broadcasted_iota(jnp.int32, (nrow, rows), 0) % dec
    for i in range(SAMPLE_NB):
        q = (q_ref[i * dec:(i + 1) * dec, :] * scale).astype(BF16)
        ck = ck_ref[i].astype(BF16)
        cv = cv_ref[i].astype(BF16)
        m_new = jnp.logical_and(cn // dec == i, cn % dec <= tn)
        outs = []
        for g in range(A_KV_HEADS):
            lo, hi = g * HEAD_DIM, (g + 1) * HEAD_DIM
            qs = _stack_heads(q, g)
            s_c = _dot_nt(qs, ck[:, lo:hi])
            s_n = _dot_nt(qs, knew_b[:, lo:hi])
            (p_c, p_n), inv = _softmax_sink([s_c, s_n], [m_cache, m_new], _sink_col(sink_ref, g, dec))
            o = (_dot(p_c.astype(BF16), cv[:, lo:hi]) + _dot(p_n.astype(BF16), vnew_b[:, lo:hi])) * inv
            outs += [o[h * dec:(h + 1) * dec] for h in range(A_GROUP)]
        o_ref[i * dec:(i + 1) * dec, :] = jnp.concatenate(outs, axis=1)
        kw_ref[i, 0:WINDOW - dec, :] = ck_ref[i, dec:WINDOW, :]
        kw_ref[i, WINDOW - dec:WINDOW, :] = knew[i * dec:(i + 1) * dec]
        vw_ref[i, 0:WINDOW - dec, :] = cv_ref[i, dec:WINDOW, :]
        vw_ref[i, WINDOW - dec:WINDOW, :] = vnew[i * dec:(i + 1) * dec]


def _attn_sample(sinks, qn, kn, va, ck, cv, dbatch, dec):
    rows = SAMPLE_NB * dec
    tokrow = lambda w: pl.BlockSpec((rows, w), lambda i: (i, 0))
    cache = pl.BlockSpec((SAMPLE_NB, WINDOW, KV_WIDTH), lambda i: (i, 0, 0))
    return pl.pallas_call(
        functools.partial(_attn_sample_kernel, dec),
        grid=(dbatch // SAMPLE_NB,),
        in_specs=[pl.BlockSpec(memory_space=pltpu.SMEM), tokrow(A_WIDTH), tokrow(KV_WIDTH), tokrow(KV_WIDTH),
                  cache, cache],
        out_specs=(pl.BlockSpec((rows, A_WIDTH), lambda i: (i, 0)), cache, cache),
        out_shape=(jax.ShapeDtypeStruct((dbatch * dec, A_WIDTH), F32),
                   jax.ShapeDtypeStruct((dbatch, WINDOW, KV_WIDTH), F32),
                   jax.ShapeDtypeStruct((dbatch, WINDOW, KV_WIDTH), F32)),
        compiler_params=pltpu.CompilerParams(dimension_semantics=("parallel",), vmem_limit_bytes=VMEM_LIMIT),
        name="attn_sample",
    )(sinks, qn, kn, va, ck, cv)


def _mlstm_kernel(nseq, L, qk_ref, v_ref, o_ref, gcol_ref, gt_ref, conv0_ref, c0_ref, n0_ref, m0_ref,
                  cw_ref, cb_ref, gbrow_ref, gbcol_ref, mng_ref, mask_ref,
                  out_ref, cst_ref, nst_ref, mst_ref, prev_ref):
    R = nseq * L
    ci = pl.program_id(1)

    @pl.when(ci == 0)
    def _():
        cst_ref[...] = c0_ref[...]
        nst_ref[...] = n0_ref[...]
        mst_ref[...] = m0_ref[...]
        prev_ref[...] = conv0_ref[0]

    raw = qk_ref[...]
    row = lax.broadcasted_iota(jnp.int32, (R, 1), 0)
    tpos = row % L
    rseq = row // L
    acc = raw * cw_ref[CONV_W - 1:CONV_W, :] + cb_ref[...]
    if nseq == 1:
        prev8 = prev_ref[...]
        t8 = lax.broadcasted_iota(jnp.int32, (SUBLANES, 1), 0)
    else:
        prevsrc = conv0_ref[...].reshape(R, 2 * M_WIDTH)
    for k in range(1, CONV_W):
        rolled = pltpu.roll(raw, k, 0)
        if nseq == 1:
            head = jnp.where(t8 >= k, rolled[0:SUBLANES], pltpu.roll(prev8, k, 0))
            sh = jnp.concatenate([head, rolled[SUBLANES:]], axis=0)
        else:
            sh = jnp.where(tpos >= k, rolled, pltpu.roll(prevsrc, R - SUBLANES + k, 0))
        acc = acc + sh * cw_ref[CONV_W - 1 - k:CONV_W - k, :]
    if nseq == 1:
        prev_ref[...] = raw[R - SUBLANES:R]
    qkc = acc * jax.nn.sigmoid(acc)

    gc = gcol_ref[...] + gbrow_ref[...]
    gr = gt_ref[0] + gbcol_ref[...]
    lsc = _log_sigmoid(gc)
    lsr = _log_sigmoid(gr)
    mb = mask_ref[...]
    maskb = mb > 0
    bcol = sum(_dot(mb, p) for p in _split3(lsc))
    brow = sum(_dot_nt(p, mb) for p in _split3(lsr))

    lane = lax.broadcasted_iota(jnp.int32, (1, LANES), 1)
    m_new = [jnp.zeros((1, LANES), F32) for _ in range(nseq)]
    for h in range(M_HEADS):
        sl = slice(h * M_HEAD_DIM, (h + 1) * M_HEAD_DIM)
        qh = qkc[:, sl]
        kh = qkc[:, M_WIDTH + h * M_HEAD_DIM:M_WIDTH + (h + 1) * M_HEAD_DIM] * (M_HEAD_DIM ** -0.5)
        vh = v_ref[:, sl]
        qb, kb, vb = qh.astype(BF16), kh.astype(BF16), vh.astype(BF16)
        ig_c = gc[:, h:h + 1]
        b_c = bcol[:, M_HEADS + h:M_HEADS + h + 1]
        ig_r = gr[h:h + 1, :]
        b_r = brow[M_HEADS + h:M_HEADS + h + 1, :]
        if nseq == 1:
            m0c = mst_ref[0][:, h:h + 1]
            n0rows = nst_ref[0, h:h + 1, :]
        else:
            m0c = jnp.zeros((R, 1), F32)
            n0rows = jnp.zeros((R, M_HEAD_DIM), F32)
            for s in range(nseq):
                m0c = jnp.where(rseq == s, mst_ref[s][:, h:h + 1], m0c)
                n0rows = jnp.where(rseq == s, nst_ref[s, h:h + 1, :], n0rows)
        dm = jnp.where(maskb, b_c - b_r + ig_r, NEG)
        a_c = b_c + m0c
        m_c = jnp.maximum(a_c, jnp.max(dm, axis=-1, keepdims=True))
        w = jnp.exp(dm - m_c)
        sc = jnp.exp(a_c - m_c)
        wqk = w * _dot_nt(qb, kb)
        if nseq == 1:
            inter = _dot_nt(qb, cst_ref[0, h].astype(BF16))
        else:
            inter = jnp.zeros((R, M_HEAD_DIM), F32)
            for s in range(nseq):
                qs = jnp.where(rseq == s, qh, 0.0).astype(BF16)
                inter = inter + _dot_nt(qs, cst_ref[s, h].astype(BF16))
        num = _dot(wqk.astype(BF16), vb) + sc * inter
        den = jnp.sum(wqk, axis=-1, keepdims=True) + sc * jnp.sum(qh * n0rows, axis=-1, keepdims=True)
        hh = num / jnp.maximum(jnp.abs(den), jnp.exp(-m_c))

        for s in range(nseq):
            e = s * L + L - 1
            m_end = m_c[e:e + 1, :]
            wend = jnp.exp(b_c[e:e + 1, :] - b_c + ig_c - m_end)
            if nseq > 1:
                wend = jnp.where(rseq == s, wend, 0.0)
            sce = jnp.exp(a_c[e:e + 1, :] - m_end)
            c_new = sce * cst_ref[s, h] + _dot_tn((vh * wend).astype(BF16), kb)
            n_new = sce * nst_ref[s, h:h + 1, :] + jnp.sum(wend * kh, axis=0, keepdims=True)
            cst_ref[s, h] = c_new
            nst_ref[s, h:h + 1, :] = n_new
            m_new[s] = jnp.where(lane == h, m_end, m_new[s])

        hn = (hh * lax.rsqrt(jnp.mean(hh * hh, axis=-1, keepdims=True) + EPS)) * mng_ref[:, sl]
        out_ref[:, sl] = jax.nn.sigmoid(o_ref[:, sl]) * hn
    for s in range(nseq):
        mst_ref[s] = m_new[s]


def _mlstm(qkm, vm, om, gcol, gt, conv0, c0, n0, m0, cw, cb, gbrow, gbcol, mng, nseq, L, ngroups, nchunks):
    R = nseq * L
    per_tile = gt.shape[2] // R
    tok = lambda w: pl.BlockSpec((R, w), lambda g, c: (g * nchunks + c, 0))
    gt_spec = pl.BlockSpec((1, SUBLANES, R),
                           lambda g, c: ((g * nchunks + c) // per_tile, 0, (g * nchunks + c) % per_tile))
    full = lambda a: pl.BlockSpec(a.shape, lambda g, c: (0,) * a.ndim)
    st4 = pl.BlockSpec((nseq, M_HEADS, M_HEAD_DIM, M_HEAD_DIM), lambda g, c: (g, 0, 0, 0))
    st3 = pl.BlockSpec((nseq, M_HEADS, M_HEAD_DIM), lambda g, c: (g, 0, 0))
    stm = pl.BlockSpec((nseq, 1, LANES), lambda g, c: (g, 0, 0))
    conv_spec = pl.BlockSpec((nseq, SUBLANES, 2 * M_WIDTH), lambda g, c: (g, 0, 0))
    r = jnp.arange(R)
    mask = ((r[:, None] // L == r[None, :] // L) & (r[None, :] <= r[:, None])).astype(BF16)
    nstate = ngroups * nseq
    return pl.pallas_call(
        functools.partial(_mlstm_kernel, nseq, L),
        grid=(ngroups, nchunks),
        in_specs=[tok(2 * M_WIDTH), tok(M_WIDTH), tok(M_WIDTH), tok(LANES), gt_spec, conv_spec, st4, st3, stm,
                  full(cw), full(cb), full(gbrow), full(gbcol), full(mng), full(mask)],
        out_specs=(pl.BlockSpec((R, M_WIDTH), lambda g, c: (g * nchunks + c, 0)), st4, st3, stm),
        out_shape=(jax.ShapeDtypeStruct((ngroups * nchunks * R, M_WIDTH), F32),
                   jax.ShapeDtypeStruct((nstate, M_HEADS, M_HEAD_DIM, M_HEAD_DIM), F32),
                   jax.ShapeDtypeStruct((nstate, M_HEADS, M_HEAD_DIM), F32),
                   jax.ShapeDtypeStruct((nstate, 1, LANES), F32)),
        scratch_shapes=[pltpu.VMEM((SUBLANES, 2 * M_WIDTH), F32)],
        compiler_params=pltpu.CompilerParams(dimension_semantics=("parallel", "arbitrary"),
                                             vmem_limit_bytes=VMEM_LIMIT),
        name="mlstm_n%d" % nseq,
    )(qkm, vm, om, gcol, gt, conv0, c0, n0, m0, cw, cb, gbrow, gbcol, mng, mask)


def _outproj_kernel(npt, ap_ref, as_ref, mp_ref, ms_ref, xp_ref, xs_ref, wo_ref, g_ref, wrt_ref, brc_ref,
                    h_ref, hn_ref, route_ref, cnt_ref):
    def project(a_ref, m_ref, x_ref):
        h_ref[...] = (x_ref[...] + _dot(a_ref[...].astype(BF16), wo_ref[0:A_WIDTH, :])
                      + _dot(m_ref[...].astype(BF16), wo_ref[A_WIDTH:A_WIDTH + M_WIDTH, :]))

    @pl.when(pl.program_id(0) < npt)
    def _():
        project(ap_ref, mp_ref, xp_ref)

    @pl.when(pl.program_id(0) >= npt)
    def _():
        project(as_ref, ms_ref, xs_ref)

    h = h_ref[...]
    hn = (h * lax.rsqrt(jnp.mean(h * h, axis=-1, keepdims=True) + EPS)) * g_ref[...]
    _store_row_tiles(hn_ref, TOK_TILE, hn)
    logits = _dot_nt(wrt_ref[...], hn.astype(BF16)) + brc_ref[...]
    eidx = lax.broadcasted_iota(jnp.int32, logits.shape, 0).astype(F32)
    picked = jnp.zeros(logits.shape, F32)
    top0 = None
    den = None
    es = []
    ids = []
    for k in range(TOP_K):
        mx = jnp.max(logits, axis=0, keepdims=True)
        idx = jnp.min(jnp.where(logits == mx, eidx, float(N_EXPERTS)), axis=0, keepdims=True)
        if k == 0:
            top0 = mx
        e = jnp.exp(mx - top0)
        den = e if den is None else den + e
        es.append(e)
        ids.append(idx)
        hit = eidx == idx
        picked = jnp.where(hit, 1.0, picked)
        logits = jnp.where(hit, -jnp.inf, logits)
    route_ref[0] = jnp.concatenate([e / den for e in es] + ids, axis=0)

    @pl.when(pl.program_id(0) == 0)
    def _():
        cnt_ref[...] = jnp.zeros_like(cnt_ref)

    cnt_ref[...] += jnp.broadcast_to(jnp.sum(picked, axis=1, keepdims=True), cnt_ref.shape)


def _outproj(a_p, a_s, m_p, m_s, xp, xs, w_out, g_ffn, wr_t, br_col):
    T = xp.shape[0] + xs.shape[0]
    nt = T // TOK_TILE
    npt = xp.shape[0] // TOK_TILE
    row = lambda w: pl.BlockSpec((TOK_TILE, w), lambda i: (i, 0))
    full = lambda a: pl.BlockSpec(a.shape, lambda i: (0,) * a.ndim)
    return pl.pallas_call(
        functools.partial(_outproj_kernel, npt),
        grid=(nt,),
        in_specs=[*_split_specs(npt, TOK_TILE, A_WIDTH), *_split_specs(npt, TOK_TILE, M_WIDTH),
                  *_split_specs(npt, TOK_TILE, D_MODEL), full(w_out), full(g_ffn), full(wr_t), full(br_col)],
        out_specs=(row(D_MODEL), pl.BlockSpec((TOK_TILE * SUBLANES, LANES), lambda i: (i, 0)),
                   pl.BlockSpec((1, 2 * TOP_K, TOK_TILE), lambda i: (i, 0, 0)),
                   pl.BlockSpec((N_EXPERTS, LANES), lambda i: (0, 0))),
        out_shape=(jax.ShapeDtypeStruct((T, D_MODEL), F32), jax.ShapeDtypeStruct((T * SUBLANES, LANES), F32),
                   jax.ShapeDtypeStruct((nt, 2 * TOP_K, TOK_TILE), F32),
                   jax.ShapeDtypeStruct((N_EXPERTS, LANES), F32)),
        compiler_params=pltpu.CompilerParams(dimension_semantics=("arbitrary",), vmem_limit_bytes=VMEM_LIMIT),
        name="outproj_router",
    )(a_p, a_s, m_p, m_s, xp, xs, w_out, g_ffn, wr_t, br_col)


def _route_kernel(nblk_pad, route_ref, cnt_ref, ustrict_ref, lstrict_ref, lp_ref, tab_ref, blk_ref, info_ref,
                  carry_ref):
    i = pl.program_id(0)
    cnt = cnt_ref[...]
    nb_e = jnp.floor((cnt + (MOE_BLOCK - 1.0)) * (1.0 / MOE_BLOCK))
    bstart = sum(_dot(lstrict_ref[...], p) for p in _split3(nb_e))
    bend = bstart + nb_e
    row_start = bstart * float(MOE_BLOCK)

    @pl.when(i == 0)
    def _():
        carry_ref[...] = jnp.zeros_like(carry_ref)
        bi = lax.broadcasted_iota(jnp.int32, (N_EXPERTS, nblk_pad), 1).astype(F32)
        done = jnp.where(bend[:, 0:1] <= bi, 1.0, 0.0)
        be = jnp.minimum(jnp.sum(done, axis=0, keepdims=True), N_EXPERTS - 1.0)
        blk_ref[...] = jnp.broadcast_to(be, blk_ref.shape).astype(jnp.int32)
        lane = lax.broadcasted_iota(jnp.int32, (N_EXPERTS, LANES), 1)
        info = jnp.where(lane == 0, row_start + cnt, 0.0)
        info = jnp.where(lane == 1, nb_e * float(MOE_BLOCK) - cnt, info)
        info = jnp.where(lane == 2, bend, info)
        info_ref[...] = info.astype(jnp.int32)

    r = route_ref[0]
    eidx = lax.broadcasted_iota(jnp.int32, (N_EXPERTS, TOK_TILE), 0).astype(F32)
    sel = [eidx == r[TOP_K + k:TOP_K + k + 1, :] for k in range(TOP_K)]
    oh = jnp.zeros((N_EXPERTS, TOK_TILE), F32)
    for k in range(TOP_K):
        oh = jnp.where(sel[k], 1.0, oh)
    cnt_t = jnp.broadcast_to(jnp.sum(oh, axis=1, keepdims=True), (N_EXPERTS, LANES))
    seg = sum(_dot(lstrict_ref[...], p) for p in _split3(cnt_t))
    local = _dot(oh.astype(BF16), ustrict_ref[...]) + seg[:, 0:1]
    rows = [jnp.sum(jnp.where(sel[k], local, 0.0), axis=0, keepdims=True) for k in range(TOP_K)]
    lp_ref[0] = (jnp.concatenate(rows + [jnp.zeros((TOP_K, TOK_TILE), F32)], axis=0)
                 * float(SUBLANES)).astype(jnp.int32)
    diag = (lax.broadcasted_iota(jnp.int32, (N_EXPERTS, LANES), 0)
            == lax.broadcasted_iota(jnp.int32, (N_EXPERTS, LANES), 1))
    to_lanes = lambda col: jnp.sum(jnp.where(diag, col, 0.0), axis=0, keepdims=True)
    tab = [to_lanes(seg), to_lanes(cnt_t), to_lanes(row_start + carry_ref[...])]
    tab_ref[0] = jnp.concatenate(tab + [jnp.zeros((SUBLANES - len(tab), LANES), F32)], axis=0).astype(jnp.int32)
    carry_ref[...] += cnt_t


def _route_tables(route, cnt, nblk):
    nt = route.shape[0]
    nblk_pad = -(-nblk // LANES) * LANES
    a = jnp.arange(TOK_TILE)
    ustrict = (a[:, None] < a[None, :]).astype(BF16)
    b = jnp.arange(N_EXPERTS)
    lstrict = (b[:, None] > b[None, :]).astype(BF16)
    full = lambda x: pl.BlockSpec(x.shape, lambda i: (0,) * x.ndim)
    tile = pl.BlockSpec((1, 2 * TOP_K, TOK_TILE), lambda i: (i, 0, 0))
    return pl.pallas_call(
        functools.partial(_route_kernel, nblk_pad),
        grid=(nt,),
        in_specs=[tile, full(cnt), full(ustrict), full(lstrict)],
        out_specs=(tile, pl.BlockSpec((1, SUBLANES, LANES), lambda i: (i, 0, 0)),
                   pl.BlockSpec((SUBLANES, nblk_pad), lambda i: (0, 0)),
                   pl.BlockSpec((N_EXPERTS, LANES), lambda i: (0, 0))),
        out_shape=(jax.ShapeDtypeStruct((nt, 2 * TOP_K, TOK_TILE), jnp.int32),
                   jax.ShapeDtypeStruct((nt, SUBLANES, LANES), jnp.int32),
                   jax.ShapeDtypeStruct((SUBLANES, nblk_pad), jnp.int32),
                   jax.ShapeDtypeStruct((N_EXPERTS, LANES), jnp.int32)),
        scratch_shapes=[pltpu.VMEM((N_EXPERTS, LANES), F32)],
        compiler_params=pltpu.CompilerParams(dimension_semantics=("arbitrary",), vmem_limit_bytes=VMEM_LIMIT),
        name="route_tables",
    )(route, cnt, ustrict, lstrict)


RUN_PIECES = tuple(1 << b for b in range(9, -1, -1))


def _for_run_pieces(count, fn):
    for size in RUN_PIECES:
        @pl.when((count & size) != 0)
        def _(size=size):
            fn(count & ~(2 * size - 1), size)


def _dispatch_kernel(padrow_ref, npad_ref, nu_ref, lp_ref, tab_ref, hn_ref, xs_hbm, srt, zbuf, sem, zsem):
    i = pl.program_id(0)
    nt = pl.num_programs(0)
    slot = i % 2

    @pl.when(i == 0)
    def _():
        zbuf[...] = jnp.zeros_like(zbuf)
        nblk = xs_hbm.shape[0] // (MOE_BLOCK * SUBLANES)

        def tail_start(b, c):
            pltpu.make_async_copy(zbuf, _row_tiles(xs_hbm, b * MOE_BLOCK, MOE_BLOCK), zsem).start()
            return c

        def tail_wait(b, c):
            pltpu.make_async_copy(zbuf, _row_tiles(xs_hbm, b * MOE_BLOCK, MOE_BLOCK), zsem).wait()
            return c

        lax.fori_loop(nu_ref[0], nblk, tail_start, 0)
        lax.fori_loop(nu_ref[0], nblk, tail_wait, 0)

        def pad_runs(wait):
            def per_expert(e, carry):
                first, count = padrow_ref[e], npad_ref[e]

                def piece(off, size):
                    cp = pltpu.make_async_copy(_row_tiles(zbuf, 0, size), _row_tiles(xs_hbm, first + off, size), zsem)
                    cp.wait() if wait else cp.start()
                _for_run_pieces(count, piece)
                return carry
            lax.fori_loop(0, N_EXPERTS, per_expert, 0)

        pad_runs(False)
        pad_runs(True)

    def wait_runs(s):
        for _ in range(TOP_K):
            pltpu.make_async_copy(_row_tiles(srt.at[s], 0, TOK_TILE), _row_tiles(xs_hbm, 0, TOK_TILE), sem.at[s]).wait()

    def step(s):
        @pl.when(i >= 2)
        def _():
            wait_runs(s)

        def permute(j, carry):
            for u in range(SUBLANES):
                t = j * SUBLANES + u
                row = hn_ref[pl.ds(pl.multiple_of(t * SUBLANES, SUBLANES), SUBLANES), :]
                for k in range(TOP_K):
                    p = lp_ref[0, 0, t * TOP_K + k]
                    srt[s, pl.ds(pl.multiple_of(p, SUBLANES), SUBLANES), :] = row
            return carry

        lax.fori_loop(0, TOK_TILE // SUBLANES, permute, 0)

        def send_run(e, carry):
            local, count, first = tab_ref[0, 0, e], tab_ref[0, 1, e], tab_ref[0, 2, e]

            def piece(off, size):
                pltpu.make_async_copy(_row_tiles(srt.at[s], local + off, size),
                                      _row_tiles(xs_hbm, first + off, size), sem.at[s]).start()
            _for_run_pieces(count, piece)
            return carry

        lax.fori_loop(0, N_EXPERTS, send_run, 0)

        @pl.when(i == nt - 1)
        def _():
            @pl.when(nt >= 2)
            def _():
                wait_runs(1 - s)

            wait_runs(s)

    for s in range(2):
        pl.when(slot == s)(functools.partial(step, s))


def _dispatch(padrow, npad, nused, lp_tiles, tab, hn, n_rows):
    nt = lp_tiles.shape[0]
    grid_spec = pltpu.PrefetchScalarGridSpec(
        num_scalar_prefetch=3,
        grid=(nt,),
        in_specs=[pl.BlockSpec((1, 1, TOK_TILE * TOP_K), lambda i, *_: (i, 0, 0), memory_space=pltpu.SMEM),
                  pl.BlockSpec((1, SUBLANES, LANES), lambda i, *_: (i, 0, 0), memory_space=pltpu.SMEM),
                  pl.BlockSpec((TOK_TILE * SUBLANES, LANES), lambda i, *_: (i, 0))],
        out_specs=pl.BlockSpec(memory_space=pl.ANY),
        scratch_shapes=[pltpu.VMEM((2, TOK_TILE * TOP_K * SUBLANES, LANES), F32),
                        pltpu.VMEM((MOE_BLOCK * SUBLANES, LANES), F32),
                        pltpu.SemaphoreType.DMA((2,)), pltpu.SemaphoreType.DMA(())],
    )
    return pl.pallas_call(
        _dispatch_kernel,
        grid_spec=grid_spec,
        out_shape=jax.ShapeDtypeStruct((n_rows * SUBLANES, LANES), F32),
        compiler_params=pltpu.CompilerParams(dimension_semantics=("arbitrary",), vmem_limit_bytes=VMEM_LIMIT),
        name="moe_dispatch",
    )(padrow, npad, nused, lp_tiles, tab, hn)


def _moe_kernel(be_ref, nu_ref, first_ref, slot_ref, nxt_ref, x_ref, bgu_ref, bd_ref, wgu_hbm, wd_hbm, y_ref,
                wgu_buf, wd_buf, wsem):
    i = pl.program_id(0)
    used = i < nu_ref[0]
    s = slot_ref[i]

    def fetch(e, sl):
        return (pltpu.make_async_copy(wgu_hbm.at[e], wgu_buf.at[sl], wsem.at[0, sl]),
                pltpu.make_async_copy(wd_hbm.at[e], wd_buf.at[sl], wsem.at[1, sl]))

    @pl.when(jnp.logical_and(used, first_ref[i] == 1))
    def _():
        @pl.when(i == 0)
        def _():
            for c in fetch(be_ref[0], 0):
                c.start()

        for c in fetch(be_ref[i], s):
            c.wait()

        @pl.when(nxt_ref[i] >= 0)
        def _():
            for c in fetch(nxt_ref[i], 1 - s):
                c.start()

    @pl.when(used)
    def _():
        x = _load_row_tiles(x_ref, MOE_BLOCK).astype(BF16)
        hb = _dot(x, wgu_buf[s].astype(BF16)) + bgu_ref[0]
        glu = jnp.minimum(hb[:, :D_FF], SWIGLU_LIMIT)
        lin = jnp.clip(hb[:, D_FF:], -SWIGLU_LIMIT, SWIGLU_LIMIT)
        act = glu * jax.nn.sigmoid(SWIGLU_ALPHA * glu) * (lin + 1.0)
        _store_row_tiles(y_ref, MOE_BLOCK, _dot(act.astype(BF16), wd_buf[s].astype(BF16)) + bd_ref[0])

    @pl.when(i >= nu_ref[0])
    def _():
        y_ref[...] = jnp.zeros_like(y_ref)


def _moe_blocks(block_e, nused, xs, wgu, bgu, wd, bd):
    nblk = block_e.shape[0]
    idx = jnp.arange(nblk, dtype=jnp.int32)
    first = (idx < nused[0]) & ((idx == 0) | (block_e != jnp.roll(block_e, 1)))
    slot = ((jnp.cumsum(first.astype(jnp.int32)) - 1) % 2).astype(jnp.int32)
    first_pos = jnp.where(first, idx, nblk)
    later = jnp.concatenate([first_pos[1:], jnp.full((1,), nblk, jnp.int32)])
    next_pos = lax.cummin(later, reverse=True)
    nxt = jnp.sum(jnp.where(idx[None, :] == next_pos[:, None], block_e[None, :] + 1, 0), axis=1) - 1
    grid_spec = pltpu.PrefetchScalarGridSpec(
        num_scalar_prefetch=5,
        grid=(nblk,),
        in_specs=[
            pl.BlockSpec((MOE_BLOCK * SUBLANES, LANES), lambda i, be, *_: (i, 0)),
            pl.BlockSpec((1, 1, 2 * D_FF), lambda i, be, *_: (be[i], 0, 0)),
            pl.BlockSpec((1, 1, D_MODEL), lambda i, be, *_: (be[i], 0, 0)),
            pl.BlockSpec(memory_space=pl.ANY),
            pl.BlockSpec(memory_space=pl.ANY),
        ],
        out_specs=pl.BlockSpec((MOE_BLOCK * SUBLANES, LANES), lambda i, be, *_: (i, 0)),
        scratch_shapes=[pltpu.VMEM((2, D_MODEL, 2 * D_FF), F32), pltpu.VMEM((2, D_FF, D_MODEL), F32),
                        pltpu.SemaphoreType.DMA((2, 2))],
    )
    return pl.pallas_call(
        _moe_kernel,
        grid_spec=grid_spec,
        out_shape=jax.ShapeDtypeStruct(xs.shape, F32),
        compiler_params=pltpu.CompilerParams(dimension_semantics=("arbitrary",), vmem_limit_bytes=MOE_VMEM_LIMIT),
        name="moe_blocks",
    )(block_e, nused, first.astype(jnp.int32), slot, nxt.astype(jnp.int32), xs, bgu, bd, wgu, wd)


def _combine_kernel(npt, lp_ref, gate_ref, tabc_ref, tabn_ref, h_ref, ys_hbm, yp_ref, ysm_ref, srt, acc, sem):
    i = pl.program_id(0)
    nt = pl.num_programs(0)
    slot = i % 2

    def fetch_runs(tab_ref, s):
        def run(e, carry):
            local, count, first = tab_ref[0, 0, e], tab_ref[0, 1, e], tab_ref[0, 2, e]

            def piece(off, size):
                pltpu.make_async_copy(_row_tiles(ys_hbm, first + off, size),
                                      _row_tiles(srt.at[s], local + off, size), sem.at[s]).start()
            _for_run_pieces(count, piece)
            return carry
        lax.fori_loop(0, N_EXPERTS, run, 0)

    @pl.when(i == 0)
    def _():
        fetch_runs(tabc_ref, 0)

    def step(s):
        @pl.when(i + 1 < nt)
        def _():
            fetch_runs(tabn_ref, 1 - s)

        for _ in range(TOP_K):
            pltpu.make_async_copy(_row_tiles(ys_hbm, 0, TOK_TILE), _row_tiles(srt.at[s], 0, TOK_TILE),
                                  sem.at[s]).wait()

        def gather(j, carry):
            for u in range(SUBLANES):
                t = j * SUBLANES + u
                tot = None
                for k in range(TOP_K):
                    p = lp_ref[0, 0, t * TOP_K + k]
                    term = (srt[s, pl.ds(pl.multiple_of(p, SUBLANES), SUBLANES), :]
                            * gate_ref[0, 0, t * TOP_K + k])
                    tot = term if tot is None else tot + term
                acc[pl.ds(pl.multiple_of(t * SUBLANES, SUBLANES), SUBLANES), :] = tot
            return carry

        lax.fori_loop(0, TOK_TILE // SUBLANES, gather, 0)

    for s in range(2):
        pl.when(slot == s)(functools.partial(step, s))
    y = h_ref[...] + _load_row_tiles(acc, TOK_TILE)

    @pl.when(i < npt)
    def _():
        yp_ref[...] = y

    @pl.when(i >= npt)
    def _():
        ysm_ref[...] = y


def _combine(lp_tiles, gate_tiles, tab, h, ys, n_prompt_rows):
    T = h.shape[0]
    nt = T // TOK_TILE
    npt = n_prompt_rows // TOK_TILE
    per_assign = pl.BlockSpec((1, 1, TOK_TILE * TOP_K), lambda i: (i, 0, 0), memory_space=pltpu.SMEM)
    tab_blk = lambda imap: pl.BlockSpec((1, SUBLANES, LANES), imap, memory_space=pltpu.SMEM)
    return pl.pallas_call(
        functools.partial(_combine_kernel, npt),
        grid=(nt,),
        in_specs=[per_assign, per_assign,
                  tab_blk(lambda i: (i, 0, 0)), tab_blk(lambda i: (jnp.minimum(i + 1, nt - 1), 0, 0)),
                  pl.BlockSpec((TOK_TILE, D_MODEL), lambda i: (i, 0)),
                  pl.BlockSpec(memory_space=pl.ANY)],
        out_specs=(pl.BlockSpec((TOK_TILE, D_MODEL), lambda i: (jnp.minimum(i, npt - 1), 0)),
                   pl.BlockSpec((TOK_TILE, D_MODEL), lambda i: (jnp.maximum(i - npt, 0), 0))),
        out_shape=(jax.ShapeDtypeStruct((n_prompt_rows, D_MODEL), F32),
                   jax.ShapeDtypeStruct((T - n_prompt_rows, D_MODEL), F32)),
        scratch_shapes=[pltpu.VMEM((2, TOK_TILE * TOP_K * SUBLANES, LANES), F32),
                        pltpu.VMEM((TOK_TILE * SUBLANES, LANES), F32), pltpu.SemaphoreType.DMA((2,))],
        compiler_params=pltpu.CompilerParams(dimension_semantics=("arbitrary",), vmem_limit_bytes=VMEM_LIMIT),
        name="moe_combine",
    )(lp_tiles, gate_tiles, tab, tab, h, ys)


def kernel(x_prompt, x_sample, cache_k_win, cache_v_win, state_conv, state_C, state_n, state_m, g_attn, w_in, b_i,
           b_f, q_norm_g, k_norm_g, sinks, conv_w, conv_b, m_norm_g, w_out, g_ffn, w_router, b_router, w_gate_up,
           b_gate_up, w_down, b_down):
    depth = g_attn.shape[0]
    assert depth == 1
    B, S, _ = x_prompt.shape
    DB, DS, _ = x_sample.shape
    TP = B * S
    TS = DB * DS
    T = TP + TS
    assert T % TOK_TILE == 0 and TP % TOK_TILE == 0 and S % ATT_QB == 0 and S % PROMPT_CHUNK == 0
    assert DS == SUBLANES and DB % SAMPLE_NB == 0 and (SAMPLE_NB * DS) == LANES
    l = 0

    xp = x_prompt.reshape(TP, D_MODEL)
    xs = x_sample.reshape(TS, D_MODEL)

    w_pad = jnp.pad(w_in[l], ((0, 0), (0, IN_PAD - w_in.shape[2]))).astype(BF16)
    wgt = jnp.transpose(w_in[l][:, GATE_COL:GATE_COL + 2 * M_HEADS]).astype(BF16)
    gi = jnp.arange(A_WIDTH) // HEAD_DIM
    gmat = (gi[:, None] == gi[None, :]).astype(BF16)
    qg = jnp.tile(q_norm_g[l], A_HEADS).reshape(1, A_WIDTH)
    kg = jnp.tile(k_norm_g[l], A_KV_HEADS).reshape(1, KV_WIDTH)
    gbias = jnp.concatenate([b_i[l], b_f[l]])
    gbrow = jnp.pad(gbias, (0, LANES - 2 * M_HEADS)).reshape(1, LANES)
    gbcol = gbias.reshape(2 * M_HEADS, 1)
    mng = m_norm_g[l].reshape(1, M_WIDTH)
    cw = conv_w[l]
    cb = conv_b[l].reshape(1, 2 * M_WIDTH)

    proj_w = (g_attn[l].reshape(1, D_MODEL), w_pad, wgt, gmat, qg, kg)
    qn, kn, va, qkm, vm, om, gcol, gt = _inproj(xp, *proj_w)
    qn_s, kn_s, va_s, qkm_s, vm_s, om_s, gcol_s, gt_s = _inproj(xs, *proj_w)

    a_p = _attn_prompt(sinks[l], qn, kn, va, B, S)
    ck = cache_k_win[l].reshape(DB, WINDOW, KV_WIDTH)
    cv = cache_v_win[l].reshape(DB, WINDOW, KV_WIDTH)
    a_s, kwin_s, vwin_s = _attn_sample(sinks[l], qn_s, kn_s, va_s, ck, cv, DB, DS)

    zc = jnp.zeros((B, SUBLANES, 2 * M_WIDTH), F32)
    m_p, C_p, n_p, mm_p = _mlstm(
        qkm, vm, om, gcol, gt, zc,
        jnp.zeros((B, M_HEADS, M_HEAD_DIM, M_HEAD_DIM), F32), jnp.zeros((B, M_HEADS, M_HEAD_DIM), F32),
        jnp.full((B, 1, LANES), NEG, F32), cw, cb, gbrow, gbcol, mng,
        nseq=1, L=PROMPT_CHUNK, ngroups=B, nchunks=S // PROMPT_CHUNK)
    conv_s0 = jnp.pad(state_conv[l], ((0, 0), (SUBLANES - (CONV_W - 1), 0), (0, 0)))
    m0_s = jnp.pad(state_m[l], ((0, 0), (0, LANES - M_HEADS))).reshape(DB, 1, LANES)
    m_s, C_s, n_s, mm_s = _mlstm(
        qkm_s, vm_s, om_s, gcol_s, gt_s, conv_s0, state_C[l], state_n[l], m0_s, cw, cb, gbrow, gbcol, mng,
        nseq=SAMPLE_NB, L=DS, ngroups=DB // SAMPLE_NB, nchunks=1)

    h, hn, route, cnt = _outproj(a_p, a_s, m_p, m_s, xp, xs, w_out[l].astype(BF16), g_ffn[l].reshape(1, D_MODEL),
                                 jnp.transpose(w_router[l]).astype(BF16), b_router[l].reshape(N_EXPERTS, 1))

    nblk = T * TOP_K // MOE_BLOCK + N_EXPERTS
    lp, tab, blk, info = _route_tables(route, cnt, nblk)
    block_e = blk[0, :nblk]
    padrow = info[:, 0]
    npad = info[:, 1]
    nused = info[N_EXPERTS - 1:N_EXPERTS, 2]
    per_assign = lambda a: jnp.transpose(a[:, :TOP_K, :], (0, 2, 1)).reshape(a.shape[0], 1, TOK_TILE * TOP_K)
    lp_tiles = per_assign(lp)
    gate_tiles = per_assign(route)
    xrows = _dispatch(padrow, npad, nused, lp_tiles, tab, hn, nblk * MOE_BLOCK)
    yrows = _moe_blocks(block_e, nused, xrows,
                        w_gate_up[l], b_gate_up[l].reshape(N_EXPERTS, 1, 2 * D_FF),
                        w_down[l], b_down[l].reshape(N_EXPERTS, 1, D_MODEL))
    y_p, y_s = _combine(lp_tiles, gate_tiles, tab, h, yrows, TP)

    y_p = y_p.reshape(B, S, D_MODEL)
    y_s = y_s.reshape(DB, DS, D_MODEL)
    def seq_tail(rows, n):
        return jnp.stack([rows[(b + 1) * S - n:(b + 1) * S] for b in range(B)])

    kwin_p = seq_tail(kn, WINDOW).reshape(B, WINDOW, A_KV_HEADS, HEAD_DIM)
    vwin_p = seq_tail(va, WINDOW).reshape(B, WINDOW, A_KV_HEADS, HEAD_DIM)
    qkm_s = qkm_s.reshape(DB, DS, 2 * M_WIDTH)
    return (y_p, y_s,
            kwin_p[None], vwin_p[None], seq_tail(qkm, CONV_W - 1)[None],
            C_p[None], n_p[None], mm_p[:, 0, :M_HEADS][None],
            kwin_s.reshape(DB, WINDOW, A_KV_HEADS, HEAD_DIM)[None],
            vwin_s.reshape(DB, WINDOW, A_KV_HEADS, HEAD_DIM)[None],
            qkm_s[:, -(CONV_W - 1):][None],
            C_s[None], n_s[None], mm_s[:, 0, :M_HEADS][None])
```

```python
import functools

import jax
import jax.numpy as jnp
from jax import lax
from jax.experimental import pallas as pl
from jax.experimental.pallas import tpu as pltpu

F32 = jnp.float32
BF16 = jnp.bfloat16

D_MODEL = 1024
HEAD_DIM = 64
A_HEADS = 8
A_KV_HEADS = 2
A_GROUP = A_HEADS // A_KV_HEADS
A_WIDTH = A_HEADS * HEAD_DIM
KV_WIDTH = A_KV_HEADS * HEAD_DIM
WINDOW = 128
M_HEADS = 4
M_HEAD_DIM = 128
M_WIDTH = M_HEADS * M_HEAD_DIM
CONV_W = 4
N_EXPERTS = 32
TOP_K = 4
D_FF = D_MODEL
SWIGLU_LIMIT = 7.0
SWIGLU_ALPHA = 1.702
MOE_BLOCK = 512
EPS = 1e-6
NEG = -1e30

LANES = 128
SUBLANES = 8
GATE_COL = A_WIDTH + 2 * KV_WIDTH + 4 * M_WIDTH
IN_PAD = GATE_COL + LANES
TOK_TILE = 512
PROJ_TILE = 1024
PROJ_VMEM_LIMIT = 58 * 1024 * 1024
GROUP_CHUNK = 256
ATT_QB = 512
ATT_SB = 128
SAMPLE_NB = 16
PROMPT_CHUNK = 256
VMEM_LIMIT = 48 * 1024 * 1024
MOE_VMEM_LIMIT = 56 * 1024 * 1024


def _dot(a, b):
    return jnp.dot(a, b, preferred_element_type=F32)


def _dot_nt(a, b):
    return lax.dot_general(a, b, (((1,), (1,)), ((), ())), preferred_element_type=F32)


def _dot_tn(a, b):
    return lax.dot_general(a, b, (((0,), (0,)), ((), ())), preferred_element_type=F32)


def _split3(x):
    hi = x.astype(BF16)
    r1 = x - hi.astype(F32)
    mid = r1.astype(BF16)
    lo = (r1 - mid.astype(F32)).astype(BF16)
    return hi, mid, lo


def _log_sigmoid(x):
    return jnp.minimum(x, 0.0) - jnp.log1p(jnp.exp(-jnp.abs(x)))


def _sigmoid(x):
    return 0.5 * jnp.tanh(0.5 * x) + 0.5


def _load_row_tiles(ref2, rows):
    return jnp.concatenate([ref2[pl.ds(s, rows, stride=SUBLANES), :] for s in range(SUBLANES)], axis=1)


def _store_row_tiles(ref2, rows, val):
    for s in range(SUBLANES):
        ref2[pl.ds(s, rows, stride=SUBLANES), :] = val[:, s * LANES:(s + 1) * LANES]


def _row_tiles(ref2, first, n):
    start = first * SUBLANES
    if not isinstance(start, int):
        start = pl.multiple_of(start, SUBLANES)
    return ref2.at[pl.ds(start, n * SUBLANES), :]


def _split_specs(n_prompt_tiles, rows, width):
    return (pl.BlockSpec((rows, width), lambda i, *_: (jnp.minimum(i, n_prompt_tiles - 1), 0)),
            pl.BlockSpec((rows, width), lambda i, *_: (jnp.maximum(i - n_prompt_tiles, 0), 0)))


def _inproj_kernel(x_ref, g_ref, w_ref, wgt_ref, gmat_ref, qg_ref, kg_ref,
                   qn_ref, kn_ref, va_ref, qkm_ref, vm_ref, om_ref, gcol_ref, gt_ref):
    x = x_ref[...]
    ms = jnp.mean(x * x, axis=-1, keepdims=True)
    xn = ((x * lax.rsqrt(ms + EPS)) * g_ref[...]).astype(BF16)

    def seg(lo, hi):
        return _dot(xn, w_ref[:, lo:hi])

    def head_norm(z, gmat, g):
        parts = _split3(z * z)
        w = min(GROUP_CHUNK, z.shape[1])
        ss = jnp.concatenate([sum(_dot(p[:, c:c + w], gmat) for p in parts) for c in range(0, z.shape[1], w)], axis=1)
        return (z * lax.rsqrt(ss * (1.0 / HEAD_DIM) + EPS)) * g

    o0 = A_WIDTH
    o1 = o0 + KV_WIDTH
    o2 = o1 + KV_WIDTH
    o3 = o2 + 2 * M_WIDTH
    o4 = o3 + M_WIDTH
    o5 = o4 + M_WIDTH
    qn_ref[...] = head_norm(seg(0, o0), gmat_ref[...], qg_ref[...])
    kn_ref[...] = head_norm(seg(o0, o1), gmat_ref[:KV_WIDTH, :KV_WIDTH], kg_ref[...])
    va_ref[...] = seg(o1, o2)
    qkm_ref[...] = seg(o2, o3)
    vm_ref[...] = seg(o3, o4)
    om_ref[...] = seg(o4, o5)
    gcol_ref[...] = seg(o5, o5 + LANES)
    gt_ref[0] = _dot_nt(wgt_ref[...], xn)


def _inproj(x, g_attn, w_pad, wgt, gmat, qg, kg):
    T = x.shape[0]
    nt = T // PROJ_TILE
    row = lambda w: pl.BlockSpec((PROJ_TILE, w), lambda i: (i, 0))
    full = lambda a: pl.BlockSpec(a.shape, lambda i: (0,) * a.ndim, pipeline_mode=pl.Buffered(1))
    out_shape = (
        jax.ShapeDtypeStruct((T, A_WIDTH), F32),
        jax.ShapeDtypeStruct((T, KV_WIDTH), F32),
        jax.ShapeDtypeStruct((T, KV_WIDTH), F32),
        jax.ShapeDtypeStruct((T, 2 * M_WIDTH), F32),
        jax.ShapeDtypeStruct((T, M_WIDTH), F32),
        jax.ShapeDtypeStruct((T, M_WIDTH), F32),
        jax.ShapeDtypeStruct((T, LANES), F32),
        jax.ShapeDtypeStruct((nt, SUBLANES, PROJ_TILE), F32),
    )
    out_specs = (row(A_WIDTH), row(KV_WIDTH), row(KV_WIDTH), row(2 * M_WIDTH), row(M_WIDTH), row(M_WIDTH),
                 row(LANES), pl.BlockSpec((1, SUBLANES, PROJ_TILE), lambda i: (i, 0, 0)))
    return pl.pallas_call(
        _inproj_kernel,
        grid=(nt,),
        in_specs=[row(D_MODEL), full(g_attn), full(w_pad), full(wgt), full(gmat), full(qg), full(kg)],
        out_specs=out_specs,
        out_shape=out_shape,
        compiler_params=pltpu.CompilerParams(dimension_semantics=("parallel",), vmem_limit_bytes=PROJ_VMEM_LIMIT),
        name="inproj",
    )(x, g_attn, w_pad, wgt, gmat, qg, kg)


def _softmax_sink(pieces, masks, sink_col):
    masked = [jnp.where(mk, s, NEG) for s, mk in zip(pieces, masks)]
    m = sink_col
    for s in masked:
        m = jnp.maximum(m, jnp.max(s, axis=-1, keepdims=True))
    ps = [jnp.exp(s - m) for s in masked]
    den = jnp.exp(sink_col - m)
    for p in ps:
        den = den + jnp.sum(p, axis=-1, keepdims=True)
    return ps, 1.0 / den


def _stack_heads(q, g):
    return jnp.concatenate([q[:, (A_GROUP * g + i) * HEAD_DIM:(A_GROUP * g + i + 1) * HEAD_DIM]
                            for i in range(A_GROUP)], axis=0)


def _sink_col(sink_ref, g, rows_per_head):
    r = lax.broadcasted_iota(jnp.int32, (A_GROUP * rows_per_head, 1), 0)
    col = jnp.zeros((A_GROUP * rows_per_head, 1), F32)
    for i in range(A_GROUP):
        col = jnp.where(r // rows_per_head == i, sink_ref[A_GROUP * g + i], col)
    return col


def _attn_prompt_kernel(sink_ref, q_ref, kp_ref, kc_ref, vp_ref, vc_ref, o_ref):
    j = pl.program_id(1)
    scale = HEAD_DIM ** -0.5
    kall = jnp.concatenate([kp_ref[...], kc_ref[...]], axis=0).astype(BF16)
    vall = jnp.concatenate([vp_ref[...], vc_ref[...]], axis=0).astype(BF16)
    nrow = A_GROUP * ATT_SB
    r = lax.broadcasted_iota(jnp.int32, (nrow, 2 * ATT_SB), 0) % ATT_SB
    c = lax.broadcasted_iota(jnp.int32, (nrow, 2 * ATT_SB), 1)
    band = jnp.logical_and(c >= r, c <= r + WINDOW)
    band0 = jnp.logical_and(band, jnp.logical_or(c >= ATT_SB, j > 0))
    for sb in range(ATT_QB // ATT_SB):
        q = (q_ref[sb * ATT_SB:(sb + 1) * ATT_SB, :] * scale).astype(BF16)
        kwin = kall[sb * ATT_SB:(sb + 2) * ATT_SB]
        vwin = vall[sb * ATT_SB:(sb + 2) * ATT_SB]
        outs = []
        for g in range(A_KV_HEADS):
            lo, hi = g * HEAD_DIM, (g + 1) * HEAD_DIM
            s = _dot_nt(_stack_heads(q, g), kwin[:, lo:hi])
            (p,), inv = _softmax_sink([s], [band0 if sb == 0 else band], _sink_col(sink_ref, g, ATT_SB))
            o = _dot(p.astype(BF16), vwin[:, lo:hi]) * inv
            outs += [o[i * ATT_SB:(i + 1) * ATT_SB] for i in range(A_GROUP)]
        o_ref[sb * ATT_SB:(sb + 1) * ATT_SB, :] = jnp.concatenate(outs, axis=1)


def _attn_prompt(sinks, qn, kn, va, batch, seq):
    nq = seq // ATT_QB
    ratio = ATT_QB // ATT_SB
    cur = lambda w: pl.BlockSpec((ATT_QB, w), lambda b, j: (b * nq + j, 0))
    prev = lambda w: pl.BlockSpec((ATT_SB, w), lambda b, j: (jnp.maximum((b * nq + j) * ratio - 1, 0), 0))
    return pl.pallas_call(
        _attn_prompt_kernel,
        grid=(batch, nq),
        in_specs=[pl.BlockSpec(memory_space=pltpu.SMEM), cur(A_WIDTH), prev(KV_WIDTH), cur(KV_WIDTH),
                  prev(KV_WIDTH), cur(KV_WIDTH)],
        out_specs=cur(A_WIDTH),
        out_shape=jax.ShapeDtypeStruct((batch * seq, A_WIDTH), F32),
        compiler_params=pltpu.CompilerParams(dimension_semantics=("parallel", "parallel"),
                                             vmem_limit_bytes=VMEM_LIMIT),
        name="attn_prompt",
    )(sinks, qn, kn, kn, va, va)


def _attn_sample_kernel(dec, sink_ref, q_ref, kn_ref, vn_ref, ck_ref, cv_ref, o_ref, kw_ref, vw_ref):
    scale = HEAD_DIM ** -0.5
    rows = SAMPLE_NB * dec
    knew = kn_ref[...]
    vnew = vn_ref[...]
    knew_b = knew.astype(BF16)
    vnew_b = vnew.astype(BF16)
    nrow = A_GROUP * dec
    t = lax.broadcasted_iota(jnp.int32, (nrow, WINDOW), 0) % dec
    c = lax.broadcasted_iota(jnp.int32, (nrow, WINDOW), 1)
    m_cache = c >= t
    cn = lax.broadcasted_iota(jnp.int32, (nrow, rows), 1)
    tn = lax.broadcasted_iota(jnp.int32, (nrow, rows), 0) % dec
    for i in range(SAMPLE_NB):
        q = (q_ref[i * dec:(i + 1) * dec, :] * scale).astype(BF16)
        ck = ck_ref[i].astype(BF16)
        cv = cv_ref[i].astype(BF16)
        m_new = jnp.logical_and(cn // dec == i, cn % dec <= tn)
        outs = []
        for g in range(A_KV_HEADS):
            lo, hi = g * HEAD_DIM, (g + 1) * HEAD_DIM
            qs = _stack_heads(q, g)
            s_c = _dot_nt(qs, ck[:, lo:hi])
            s_n = _dot_nt(qs, knew_b[:, lo:hi])
            (p_c, p_n), inv = _softmax_sink([s_c, s_n], [m_cache, m_new], _sink_col(sink_ref, g, dec))
            o = (_dot(p_c.astype(BF16), cv[:, lo:hi]) + _dot(p_n.astype(BF16), vnew_b[:, lo:hi])) * inv
            outs += [o[h * dec:(h + 1) * dec] for h in range(A_GROUP)]
        o_ref[i * dec:(i + 1) * dec, :] = jnp.concatenate(outs, axis=1)
        kw_ref[i, 0:WINDOW - dec, :] = ck_ref[i, dec:WINDOW, :]
        kw_ref[i, WINDOW - dec:WINDOW, :] = knew[i * dec:(i + 1) * dec]
        vw_ref[i, 0:WINDOW - dec, :] = cv_ref[i, dec:WINDOW, :]
        vw_ref[i, WINDOW - dec:WINDOW, :] = vnew[i * dec:(i + 1) * dec]


def _attn_sample(sinks, qn, kn, va, ck, cv, dbatch, dec):
    rows = SAMPLE_NB * dec
    tokrow = lambda w: pl.BlockSpec((rows, w), lambda i: (i, 0))
    cache = pl.BlockSpec((SAMPLE_NB, WINDOW, KV_WIDTH), lambda i: (i, 0, 0))
    return pl.pallas_call(
        functools.partial(_attn_sample_kernel, dec),
        grid=(dbatch // SAMPLE_NB,),
        in_specs=[pl.BlockSpec(memory_space=pltpu.SMEM), tokrow(A_WIDTH), tokrow(KV_WIDTH), tokrow(KV_WIDTH),
                  cache, cache],
        out_specs=(pl.BlockSpec((rows, A_WIDTH), lambda i: (i, 0)), cache, cache),
        out_shape=(jax.ShapeDtypeStruct((dbatch * dec, A_WIDTH), F32),
                   jax.ShapeDtypeStruct((dbatch, WINDOW, KV_WIDTH), F32),
                   jax.ShapeDtypeStruct((dbatch, WINDOW, KV_WIDTH), F32)),
        compiler_params=pltpu.CompilerParams(dimension_semantics=("parallel",), vmem_limit_bytes=VMEM_LIMIT),
        name="attn_sample",
    )(sinks, qn, kn, va, ck, cv)


def _mlstm_kernel(nseq, L, qk_ref, v_ref, o_ref, gcol_ref, gt_ref, conv0_ref, c0_ref, n0_ref, m0_ref,
                  cw_ref, cb_ref, gbrow_ref, gbcol_ref, mng_ref, mask_ref,
                  out_ref, cst_ref, nst_ref, mst_ref, prev_ref):
    R = nseq * L
    ci = pl.program_id(1)

    @pl.when(ci == 0)
    def _():
        cst_ref[...] = c0_ref[...]
        nst_ref[...] = n0_ref[...]
        mst_ref[...] = m0_ref[...]
        prev_ref[...] = conv0_ref[0]

    raw = qk_ref[...]
    row = lax.broadcasted_iota(jnp.int32, (R, 1), 0)
    tpos = row % L
    rseq = row // L
    acc = raw * cw_ref[CONV_W - 1:CONV_W, :] + cb_ref[...]
    if nseq == 1:
        prev8 = prev_ref[...]
        t8 = lax.broadcasted_iota(jnp.int32, (SUBLANES, 1), 0)
    else:
        prevsrc = conv0_ref[...].reshape(R, 2 * M_WIDTH)
    for k in range(1, CONV_W):
        rolled = pltpu.roll(raw, k, 0)
        if nseq == 1:
            head = jnp.where(t8 >= k, rolled[0:SUBLANES], pltpu.roll(prev8, k, 0))
            sh = jnp.concatenate([head, rolled[SUBLANES:]], axis=0)
        else:
            sh = jnp.where(tpos >= k, rolled, pltpu.roll(prevsrc, R - SUBLANES + k, 0))
        acc = acc + sh * cw_ref[CONV_W - 1 - k:CONV_W - k, :]
    if nseq == 1:
        prev_ref[...] = raw[R - SUBLANES:R]
    qkc = acc * _sigmoid(acc)

    gc = gcol_ref[...] + gbrow_ref[...]
    gr = gt_ref[0] + gbcol_ref[...]
    lsc = _log_sigmoid(gc)
    lsr = _log_sigmoid(gr)
    mb = mask_ref[...]
    maskb = mb > 0
    bcol = sum(_dot(mb, p) for p in _split3(lsc))
    brow = sum(_dot_nt(p, mb) for p in _split3(lsr))

    lane = lax.broadcasted_iota(jnp.int32, (1, LANES), 1)
    m_new = [jnp.zeros((1, LANES), F32) for _ in range(nseq)]
    for h in range(M_HEADS):
        sl = slice(h * M_HEAD_DIM, (h + 1) * M_HEAD_DIM)
        qh = qkc[:, sl]
        kh = qkc[:, M_WIDTH + h * M_HEAD_DIM:M_WIDTH + (h + 1) * M_HEAD_DIM] * (M_HEAD_DIM ** -0.5)
        vh = v_ref[:, sl]
        qb, kb, vb = qh.astype(BF16), kh.astype(BF16), vh.astype(BF16)
        ig_c = gc[:, h:h + 1]
        b_c = bcol[:, M_HEADS + h:M_HEADS + h + 1]
        ig_r = gr[h:h + 1, :]
        b_r = brow[M_HEADS + h:M_HEADS + h + 1, :]
        if nseq == 1:
            m0c = mst_ref[0][:, h:h + 1]
            n0rows = nst_ref[0, h:h + 1, :]
        else:
            m0c = jnp.zeros((R, 1), F32)
            n0rows = jnp.zeros((R, M_HEAD_DIM), F32)
            for s in range(nseq):
                m0c = jnp.where(rseq == s, mst_ref[s][:, h:h + 1], m0c)
                n0rows = jnp.where(rseq == s, nst_ref[s, h:h + 1, :], n0rows)
        dm = jnp.where(maskb, b_c - b_r + ig_r, NEG)
        a_c = b_c + m0c
        m_c = jnp.maximum(a_c, jnp.max(dm, axis=-1, keepdims=True))
        w = jnp.exp(dm - m_c)
        sc = jnp.exp(a_c - m_c)
        wqk = w * _dot_nt(qb, kb)
        if nseq == 1:
            inter = _dot_nt(qb, cst_ref[0, h].astype(BF16))
        else:
            inter = jnp.zeros((R, M_HEAD_DIM), F32)
            for s in range(nseq):
                qs = jnp.where(rseq == s, qh, 0.0).astype(BF16)
                inter = inter + _dot_nt(qs, cst_ref[s, h].astype(BF16))
        num = _dot(wqk.astype(BF16), vb) + sc * inter
        den = jnp.sum(wqk, axis=-1, keepdims=True) + sc * jnp.sum(qh * n0rows, axis=-1, keepdims=True)
        hh = num / jnp.maximum(jnp.abs(den), jnp.exp(-m_c))

        for s in range(nseq):
            e = s * L + L - 1
            m_end = m_c[e:e + 1, :]
            wend = jnp.exp(b_c[e:e + 1, :] - b_c + ig_c - m_end)
            if nseq > 1:
                wend = jnp.where(rseq == s, wend, 0.0)
            sce = jnp.exp(a_c[e:e + 1, :] - m_end)
            c_new = sce * cst_ref[s, h] + _dot_tn((vh * wend).astype(BF16), kb)
            n_new = sce * nst_ref[s, h:h + 1, :] + jnp.sum(wend * kh, axis=0, keepdims=True)
            cst_ref[s, h] = c_new
            nst_ref[s, h:h + 1, :] = n_new
            m_new[s] = jnp.where(lane == h, m_end, m_new[s])

        hn = (hh * lax.rsqrt(jnp.mean(hh * hh, axis=-1, keepdims=True) + EPS)) * mng_ref[:, sl]
        out_ref[:, sl] = _sigmoid(o_ref[:, sl]) * hn
    for s in range(nseq):
        mst_ref[s] = m_new[s]


def _mlstm(qkm, vm, om, gcol, gt, conv0, c0, n0, m0, cw, cb, gbrow, gbcol, mng, nseq, L, ngroups, nchunks):
    R = nseq * L
    per_tile = gt.shape[2] // R
    tok = lambda w: pl.BlockSpec((R, w), lambda g, c: (g * nchunks + c, 0))
    gt_spec = pl.BlockSpec((1, SUBLANES, R),
                           lambda g, c: ((g * nchunks + c) // per_tile, 0, (g * nchunks + c) % per_tile))
    full = lambda a: pl.BlockSpec(a.shape, lambda g, c: (0,) * a.ndim)
    st4 = pl.BlockSpec((nseq, M_HEADS, M_HEAD_DIM, M_HEAD_DIM), lambda g, c: (g, 0, 0, 0))
    st3 = pl.BlockSpec((nseq, M_HEADS, M_HEAD_DIM), lambda g, c: (g, 0, 0))
    stm = pl.BlockSpec((nseq, 1, LANES), lambda g, c: (g, 0, 0))
    conv_spec = pl.BlockSpec((nseq, SUBLANES, 2 * M_WIDTH), lambda g, c: (g, 0, 0))
    r = jnp.arange(R)
    mask = ((r[:, None] // L == r[None, :] // L) & (r[None, :] <= r[:, None])).astype(BF16)
    nstate = ngroups * nseq
    return pl.pallas_call(
        functools.partial(_mlstm_kernel, nseq, L),
        grid=(ngroups, nchunks),
        in_specs=[tok(2 * M_WIDTH), tok(M_WIDTH), tok(M_WIDTH), tok(LANES), gt_spec, conv_spec, st4, st3, stm,
                  full(cw), full(cb), full(gbrow), full(gbcol), full(mng), full(mask)],
        out_specs=(pl.BlockSpec((R, M_WIDTH), lambda g, c: (g * nchunks + c, 0)), st4, st3, stm),
        out_shape=(jax.ShapeDtypeStruct((ngroups * nchunks * R, M_WIDTH), F32),
                   jax.ShapeDtypeStruct((nstate, M_HEADS, M_HEAD_DIM, M_HEAD_DIM), F32),
                   jax.ShapeDtypeStruct((nstate, M_HEADS, M_HEAD_DIM), F32),
                   jax.ShapeDtypeStruct((nstate, 1, LANES), F32)),
        scratch_shapes=[pltpu.VMEM((SUBLANES, 2 * M_WIDTH), F32)],
        compiler_params=pltpu.CompilerParams(dimension_semantics=("parallel", "arbitrary"),
                                             vmem_limit_bytes=VMEM_LIMIT),
        name="mlstm_n%d" % nseq,
    )(qkm, vm, om, gcol, gt, conv0, c0, n0, m0, cw, cb, gbrow, gbcol, mng, mask)


def _outproj_kernel(npt, ap_ref, as_ref, mp_ref, ms_ref, xp_ref, xs_ref, wo_ref, g_ref, wrt_ref, brc_ref,
                    h_ref, hn_ref, route_ref, cnt_ref):
    def project(a_ref, m_ref, x_ref):
        h_ref[...] = (x_ref[...] + _dot(a_ref[...].astype(BF16), wo_ref[0:A_WIDTH, :])
                      + _dot(m_ref[...].astype(BF16), wo_ref[A_WIDTH:A_WIDTH + M_WIDTH, :]))

    @pl.when(pl.program_id(0) < npt)
    def _():
        project(ap_ref, mp_ref, xp_ref)

    @pl.when(pl.program_id(0) >= npt)
    def _():
        project(as_ref, ms_ref, xs_ref)

    h = h_ref[...]
    hn = (h * lax.rsqrt(jnp.mean(h * h, axis=-1, keepdims=True) + EPS)) * g_ref[...]
    _store_row_tiles(hn_ref, TOK_TILE, hn)
    logits = _dot_nt(wrt_ref[...], hn.astype(BF16)) + brc_ref[...]
    eidx = lax.broadcasted_iota(jnp.int32, logits.shape, 0).astype(F32)
    picked = jnp.zeros(logits.shape, F32)
    top0 = None
    den = None
    es = []
    ids = []
    for k in range(TOP_K):
        mx = jnp.max(logits, axis=0, keepdims=True)
        idx = jnp.min(jnp.where(logits == mx, eidx, float(N_EXPERTS)), axis=0, keepdims=True)
        if k == 0:
            top0 = mx
        e = jnp.exp(mx - top0)
        den = e if den is None else den + e
        es.append(e)
        ids.append(idx)
        hit = eidx == idx
        picked = jnp.where(hit, 1.0, picked)
        logits = jnp.where(hit, -jnp.inf, logits)
    route_ref[0] = jnp.concatenate([e / den for e in es] + ids, axis=0)

    @pl.when(pl.program_id(0) == 0)
    def _():
        cnt_ref[...] = jnp.zeros_like(cnt_ref)

    cnt_ref[...] += jnp.broadcast_to(jnp.sum(picked, axis=1, keepdims=True), cnt_ref.shape)


def _outproj(a_p, a_s, m_p, m_s, xp, xs, w_out, g_ffn, wr_t, br_col):
    T = xp.shape[0] + xs.shape[0]
    nt = T // TOK_TILE
    npt = xp.shape[0] // TOK_TILE
    row = lambda w: pl.BlockSpec((TOK_TILE, w), lambda i: (i, 0))
    full = lambda a: pl.BlockSpec(a.shape, lambda i: (0,) * a.ndim)
    return pl.pallas_call(
        functools.partial(_outproj_kernel, npt),
        grid=(nt,),
        in_specs=[*_split_specs(npt, TOK_TILE, A_WIDTH), *_split_specs(npt, TOK_TILE, M_WIDTH),
                  *_split_specs(npt, TOK_TILE, D_MODEL), full(w_out), full(g_ffn), full(wr_t), full(br_col)],
        out_specs=(row(D_MODEL), pl.BlockSpec((TOK_TILE * SUBLANES, LANES), lambda i: (i, 0)),
                   pl.BlockSpec((1, 2 * TOP_K, TOK_TILE), lambda i: (i, 0, 0)),
                   pl.BlockSpec((N_EXPERTS, LANES), lambda i: (0, 0))),
        out_shape=(jax.ShapeDtypeStruct((T, D_MODEL), F32), jax.ShapeDtypeStruct((T * SUBLANES, LANES), F32),
                   jax.ShapeDtypeStruct((nt, 2 * TOP_K, TOK_TILE), F32),
                   jax.ShapeDtypeStruct((N_EXPERTS, LANES), F32)),
        compiler_params=pltpu.CompilerParams(dimension_semantics=("arbitrary",), vmem_limit_bytes=VMEM_LIMIT),
        name="outproj_router",
    )(a_p, a_s, m_p, m_s, xp, xs, w_out, g_ffn, wr_t, br_col)


def _route_kernel(nblk_pad, route_ref, cnt_ref, ustrict_ref, lstrict_ref, lp_ref, tab_ref, blk_ref, info_ref,
                  carry_ref):
    i = pl.program_id(0)
    cnt = cnt_ref[...]
    nb_e = jnp.floor((cnt + (MOE_BLOCK - 1.0)) * (1.0 / MOE_BLOCK))
    bstart = sum(_dot(lstrict_ref[...], p) for p in _split3(nb_e))
    bend = bstart + nb_e
    row_start = bstart * float(MOE_BLOCK)

    @pl.when(i == 0)
    def _():
        carry_ref[...] = jnp.zeros_like(carry_ref)
        bi = lax.broadcasted_iota(jnp.int32, (N_EXPERTS, nblk_pad), 1).astype(F32)
        done = jnp.where(bend[:, 0:1] <= bi, 1.0, 0.0)
        be = jnp.minimum(jnp.sum(done, axis=0, keepdims=True), N_EXPERTS - 1.0)
        blk_ref[...] = jnp.broadcast_to(be, blk_ref.shape).astype(jnp.int32)
        lane = lax.broadcasted_iota(jnp.int32, (N_EXPERTS, LANES), 1)
        info = jnp.where(lane == 0, row_start + cnt, 0.0)
        info = jnp.where(lane == 1, nb_e * float(MOE_BLOCK) - cnt, info)
        info = jnp.where(lane == 2, bend, info)
        info_ref[...] = info.astype(jnp.int32)

    r = route_ref[0]
    eidx = lax.broadcasted_iota(jnp.int32, (N_EXPERTS, TOK_TILE), 0).astype(F32)
    sel = [eidx == r[TOP_K + k:TOP_K + k + 1, :] for k in range(TOP_K)]
    oh = jnp.zeros((N_EXPERTS, TOK_TILE), F32)
    for k in range(TOP_K):
        oh = jnp.where(sel[k], 1.0, oh)
    cnt_t = jnp.broadcast_to(jnp.sum(oh, axis=1, keepdims=True), (N_EXPERTS, LANES))
    seg = sum(_dot(lstrict_ref[...], p) for p in _split3(cnt_t))
    local = _dot(oh.astype(BF16), ustrict_ref[...]) + seg[:, 0:1]
    rows = [jnp.sum(jnp.where(sel[k], local, 0.0), axis=0, keepdims=True) for k in range(TOP_K)]
    lp_ref[0] = (jnp.concatenate(rows + [jnp.zeros((TOP_K, TOK_TILE), F32)], axis=0)
                 * float(SUBLANES)).astype(jnp.int32)
    diag = (lax.broadcasted_iota(jnp.int32, (N_EXPERTS, LANES), 0)
            == lax.broadcasted_iota(jnp.int32, (N_EXPERTS, LANES), 1))
    to_lanes = lambda col: jnp.sum(jnp.where(diag, col, 0.0), axis=0, keepdims=True)
    tab = [to_lanes(seg), to_lanes(cnt_t), to_lanes(row_start + carry_ref[...])]
    tab_ref[0] = jnp.concatenate(tab + [jnp.zeros((SUBLANES - len(tab), LANES), F32)], axis=0).astype(jnp.int32)
    carry_ref[...] += cnt_t


def _route_tables(route, cnt, nblk):
    nt = route.shape[0]
    nblk_pad = -(-nblk // LANES) * LANES
    a = jnp.arange(TOK_TILE)
    ustrict = (a[:, None] < a[None, :]).astype(BF16)
    b = jnp.arange(N_EXPERTS)
    lstrict = (b[:, None] > b[None, :]).astype(BF16)
    full = lambda x: pl.BlockSpec(x.shape, lambda i: (0,) * x.ndim)
    tile = pl.BlockSpec((1, 2 * TOP_K, TOK_TILE), lambda i: (i, 0, 0))
    return pl.pallas_call(
        functools.partial(_route_kernel, nblk_pad),
        grid=(nt,),
        in_specs=[tile, full(cnt), full(ustrict), full(lstrict)],
        out_specs=(tile, pl.BlockSpec((1, SUBLANES, LANES), lambda i: (i, 0, 0)),
                   pl.BlockSpec((SUBLANES, nblk_pad), lambda i: (0, 0)),
                   pl.BlockSpec((N_EXPERTS, LANES), lambda i: (0, 0))),
        out_shape=(jax.ShapeDtypeStruct((nt, 2 * TOP_K, TOK_TILE), jnp.int32),
                   jax.ShapeDtypeStruct((nt, SUBLANES, LANES), jnp.int32),
                   jax.ShapeDtypeStruct((SUBLANES, nblk_pad), jnp.int32),
                   jax.ShapeDtypeStruct((N_EXPERTS, LANES), jnp.int32)),
        scratch_shapes=[pltpu.VMEM((N_EXPERTS, LANES), F32)],
        compiler_params=pltpu.CompilerParams(dimension_semantics=("arbitrary",), vmem_limit_bytes=VMEM_LIMIT),
        name="route_tables",
    )(route, cnt, ustrict, lstrict)


RUN_PIECES = tuple(1 << b for b in range(9, -1, -1))


def _for_run_pieces(count, fn):
    for size in RUN_PIECES:
        @pl.when((count & size) != 0)
        def _(size=size):
            fn(count & ~(2 * size - 1), size)


def _dispatch_kernel(padrow_ref, npad_ref, nu_ref, lp_ref, tab_ref, hn_ref, xs_hbm, srt, zbuf, sem, zsem):
    i = pl.program_id(0)
    nt = pl.num_programs(0)
    slot = i % 2

    @pl.when(i == 0)
    def _():
        zbuf[...] = jnp.zeros_like(zbuf)
        nblk = xs_hbm.shape[0] // (MOE_BLOCK * SUBLANES)

        def tail_start(b, c):
            pltpu.make_async_copy(zbuf, _row_tiles(xs_hbm, b * MOE_BLOCK, MOE_BLOCK), zsem).start()
            return c

        def tail_wait(b, c):
            pltpu.make_async_copy(zbuf, _row_tiles(xs_hbm, b * MOE_BLOCK, MOE_BLOCK), zsem).wait()
            return c

        lax.fori_loop(nu_ref[0], nblk, tail_start, 0)
        lax.fori_loop(nu_ref[0], nblk, tail_wait, 0)

        def pad_runs(wait):
            def per_expert(e, carry):
                first, count = padrow_ref[e], npad_ref[e]

                def piece(off, size):
                    cp = pltpu.make_async_copy(_row_tiles(zbuf, 0, size), _row_tiles(xs_hbm, first + off, size), zsem)
                    cp.wait() if wait else cp.start()
                _for_run_pieces(count, piece)
                return carry
            lax.fori_loop(0, N_EXPERTS, per_expert, 0)

        pad_runs(False)
        pad_runs(True)

    def wait_runs(s):
        for _ in range(TOP_K):
            pltpu.make_async_copy(_row_tiles(srt.at[s], 0, TOK_TILE), _row_tiles(xs_hbm, 0, TOK_TILE), sem.at[s]).wait()

    def step(s):
        @pl.when(i >= 2)
        def _():
            wait_runs(s)

        def permute(j, carry):
            for u in range(SUBLANES):
                t = j * SUBLANES + u
                row = hn_ref[pl.ds(pl.multiple_of(t * SUBLANES, SUBLANES), SUBLANES), :]
                for k in range(TOP_K):
                    p = lp_ref[0, 0, t * TOP_K + k]
                    srt[s, pl.ds(pl.multiple_of(p, SUBLANES), SUBLANES), :] = row
            return carry

        lax.fori_loop(0, TOK_TILE // SUBLANES, permute, 0)

        def send_run(e, carry):
            local, count, first = tab_ref[0, 0, e], tab_ref[0, 1, e], tab_ref[0, 2, e]

            def piece(off, size):
                pltpu.make_async_copy(_row_tiles(srt.at[s], local + off, size),
                                      _row_tiles(xs_hbm, first + off, size), sem.at[s]).start()
            _for_run_pieces(count, piece)
            return carry

        lax.fori_loop(0, N_EXPERTS, send_run, 0)

        @pl.when(i == nt - 1)
        def _():
            @pl.when(nt >= 2)
            def _():
                wait_runs(1 - s)

            wait_runs(s)

    for s in range(2):
        pl.when(slot == s)(functools.partial(step, s))


def _dispatch(padrow, npad, nused, lp_tiles, tab, hn, n_rows):
    nt = lp_tiles.shape[0]
    grid_spec = pltpu.PrefetchScalarGridSpec(
        num_scalar_prefetch=3,
        grid=(nt,),
        in_specs=[pl.BlockSpec((1, 1, TOK_TILE * TOP_K), lambda i, *_: (i, 0, 0), memory_space=pltpu.SMEM),
                  pl.BlockSpec((1, SUBLANES, LANES), lambda i, *_: (i, 0, 0), memory_space=pltpu.SMEM),
                  pl.BlockSpec((TOK_TILE * SUBLANES, LANES), lambda i, *_: (i, 0))],
        out_specs=pl.BlockSpec(memory_space=pl.ANY),
        scratch_shapes=[pltpu.VMEM((2, TOK_TILE * TOP_K * SUBLANES, LANES), F32),
                        pltpu.VMEM((MOE_BLOCK * SUBLANES, LANES), F32),
                        pltpu.SemaphoreType.DMA((2,)), pltpu.SemaphoreType.DMA(())],
    )
    return pl.pallas_call(
        _dispatch_kernel,
        grid_spec=grid_spec,
        out_shape=jax.ShapeDtypeStruct((n_rows * SUBLANES, LANES), F32),
        compiler_params=pltpu.CompilerParams(dimension_semantics=("arbitrary",), vmem_limit_bytes=VMEM_LIMIT),
        name="moe_dispatch",
    )(padrow, npad, nused, lp_tiles, tab, hn)


def _moe_kernel(be_ref, nu_ref, first_ref, slot_ref, nxt_ref, x_ref, bgu_ref, bd_ref, wgu_hbm, wd_hbm, y_ref,
                wgu_buf, wd_buf, wsem):
    i = pl.program_id(0)
    used = i < nu_ref[0]
    s = slot_ref[i]

    def fetch(e, sl):
        return (pltpu.make_async_copy(wgu_hbm.at[e], wgu_buf.at[sl], wsem.at[0, sl]),
                pltpu.make_async_copy(wd_hbm.at[e], wd_buf.at[sl], wsem.at[1, sl]))

    @pl.when(jnp.logical_and(used, first_ref[i] == 1))
    def _():
        @pl.when(i == 0)
        def _():
            for c in fetch(be_ref[0], 0):
                c.start()

        for c in fetch(be_ref[i], s):
            c.wait()

        @pl.when(nxt_ref[i] >= 0)
        def _():
            for c in fetch(nxt_ref[i], 1 - s):
                c.start()

    @pl.when(used)
    def _():
        x = _load_row_tiles(x_ref, MOE_BLOCK).astype(BF16)
        hb = _dot(x, wgu_buf[s].astype(BF16)) + bgu_ref[0]
        glu = jnp.minimum(hb[:, :D_FF], SWIGLU_LIMIT)
        lin = jnp.clip(hb[:, D_FF:], -SWIGLU_LIMIT, SWIGLU_LIMIT)
        act = glu * _sigmoid(SWIGLU_ALPHA * glu) * (lin + 1.0)
        _store_row_tiles(y_ref, MOE_BLOCK, _dot(act.astype(BF16), wd_buf[s].astype(BF16)) + bd_ref[0])

    @pl.when(i >= nu_ref[0])
    def _():
        y_ref[...] = jnp.zeros_like(y_ref)


def _moe_blocks(block_e, nused, xs, wgu, bgu, wd, bd):
    nblk = block_e.shape[0]
    idx = jnp.arange(nblk, dtype=jnp.int32)
    first = (idx < nused[0]) & ((idx == 0) | (block_e != jnp.roll(block_e, 1)))
    slot = ((jnp.cumsum(first.astype(jnp.int32)) - 1) % 2).astype(jnp.int32)
    first_pos = jnp.where(first, idx, nblk)
    later = jnp.concatenate([first_pos[1:], jnp.full((1,), nblk, jnp.int32)])
    next_pos = lax.cummin(later, reverse=True)
    nxt = jnp.sum(jnp.where(idx[None, :] == next_pos[:, None], block_e[None, :] + 1, 0), axis=1) - 1
    grid_spec = pltpu.PrefetchScalarGridSpec(
        num_scalar_prefetch=5,
        grid=(nblk,),
        in_specs=[
            pl.BlockSpec((MOE_BLOCK * SUBLANES, LANES), lambda i, be, *_: (i, 0)),
            pl.BlockSpec((1, 1, 2 * D_FF), lambda i, be, *_: (be[i], 0, 0)),
            pl.BlockSpec((1, 1, D_MODEL), lambda i, be, *_: (be[i], 0, 0)),
            pl.BlockSpec(memory_space=pl.ANY),
            pl.BlockSpec(memory_space=pl.ANY),
        ],
        out_specs=pl.BlockSpec((MOE_BLOCK * SUBLANES, LANES), lambda i, be, *_: (i, 0)),
        scratch_shapes=[pltpu.VMEM((2, D_MODEL, 2 * D_FF), F32), pltpu.VMEM((2, D_FF, D_MODEL), F32),
                        pltpu.SemaphoreType.DMA((2, 2))],
    )
    return pl.pallas_call(
        _moe_kernel,
        grid_spec=grid_spec,
        out_shape=jax.ShapeDtypeStruct(xs.shape, F32),
        compiler_params=pltpu.CompilerParams(dimension_semantics=("arbitrary",), vmem_limit_bytes=MOE_VMEM_LIMIT),
        name="moe_blocks",
    )(block_e, nused, first.astype(jnp.int32), slot, nxt.astype(jnp.int32), xs, bgu, bd, wgu, wd)


def _combine_kernel(npt, lp_ref, gate_ref, tabc_ref, tabn_ref, h_ref, ys_hbm, yp_ref, ysm_ref, srt, acc, sem):
    i = pl.program_id(0)
    nt = pl.num_programs(0)
    slot = i % 2

    def fetch_runs(tab_ref, s):
        def run(e, carry):
            local, count, first = tab_ref[0, 0, e], tab_ref[0, 1, e], tab_ref[0, 2, e]

            def piece(off, size):
                pltpu.make_async_copy(_row_tiles(ys_hbm, first + off, size),
                                      _row_tiles(srt.at[s], local + off, size), sem.at[s]).start()
            _for_run_pieces(count, piece)
            return carry
        lax.fori_loop(0, N_EXPERTS, run, 0)

    @pl.when(i == 0)
    def _():
        fetch_runs(tabc_ref, 0)

    def step(s):
        @pl.when(i + 1 < nt)
        def _():
            fetch_runs(tabn_ref, 1 - s)

        for _ in range(TOP_K):
            pltpu.make_async_copy(_row_tiles(ys_hbm, 0, TOK_TILE), _row_tiles(srt.at[s], 0, TOK_TILE),
                                  sem.at[s]).wait()

        def gather(j, carry):
            for u in range(SUBLANES):
                t = j * SUBLANES + u
                tot = None
                for k in range(TOP_K):
                    p = lp_ref[0, 0, t * TOP_K + k]
                    term = (srt[s, pl.ds(pl.multiple_of(p, SUBLANES), SUBLANES), :]
                            * gate_ref[0, 0, t * TOP_K + k])
                    tot = term if tot is None else tot + term
                acc[pl.ds(pl.multiple_of(t * SUBLANES, SUBLANES), SUBLANES), :] = tot
            return carry

        lax.fori_loop(0, TOK_TILE // SUBLANES, gather, 0)

    for s in range(2):
        pl.when(slot == s)(functools.partial(step, s))
    y = h_ref[...] + _load_row_tiles(acc, TOK_TILE)

    @pl.when(i < npt)
    def _():
        yp_ref[...] = y

    @pl.when(i >= npt)
    def _():
        ysm_ref[...] = y


def _combine(lp_tiles, gate_tiles, tab, h, ys, n_prompt_rows):
    T = h.shape[0]
    nt = T // TOK_TILE
    npt = n_prompt_rows // TOK_TILE
    per_assign = pl.BlockSpec((1, 1, TOK_TILE * TOP_K), lambda i: (i, 0, 0), memory_space=pltpu.SMEM)
    tab_blk = lambda imap: pl.BlockSpec((1, SUBLANES, LANES), imap, memory_space=pltpu.SMEM)
    return pl.pallas_call(
        functools.partial(_combine_kernel, npt),
        grid=(nt,),
        in_specs=[per_assign, per_assign,
                  tab_blk(lambda i: (i, 0, 0)), tab_blk(lambda i: (jnp.minimum(i + 1, nt - 1), 0, 0)),
                  pl.BlockSpec((TOK_TILE, D_MODEL), lambda i: (i, 0)),
                  pl.BlockSpec(memory_space=pl.ANY)],
        out_specs=(pl.BlockSpec((TOK_TILE, D_MODEL), lambda i: (jnp.minimum(i, npt - 1), 0)),
                   pl.BlockSpec((TOK_TILE, D_MODEL), lambda i: (jnp.maximum(i - npt, 0), 0))),
        out_shape=(jax.ShapeDtypeStruct((n_prompt_rows, D_MODEL), F32),
                   jax.ShapeDtypeStruct((T - n_prompt_rows, D_MODEL), F32)),
        scratch_shapes=[pltpu.VMEM((2, TOK_TILE * TOP_K * SUBLANES, LANES), F32),
                        pltpu.VMEM((TOK_TILE * SUBLANES, LANES), F32), pltpu.SemaphoreType.DMA((2,))],
        compiler_params=pltpu.CompilerParams(dimension_semantics=("arbitrary",), vmem_limit_bytes=VMEM_LIMIT),
        name="moe_combine",
    )(lp_tiles, gate_tiles, tab, tab, h, ys)


def kernel(x_prompt, x_sample, cache_k_win, cache_v_win, state_conv, state_C, state_n, state_m, g_attn, w_in, b_i,
           b_f, q_norm_g, k_norm_g, sinks, conv_w, conv_b, m_norm_g, w_out, g_ffn, w_router, b_router, w_gate_up,
           b_gate_up, w_down, b_down):
    depth = g_attn.shape[0]
    assert depth == 1
    B, S, _ = x_prompt.shape
    DB, DS, _ = x_sample.shape
    TP = B * S
    TS = DB * DS
    T = TP + TS
    assert T % TOK_TILE == 0 and TP % TOK_TILE == 0 and S % ATT_QB == 0 and S % PROMPT_CHUNK == 0
    assert DS == SUBLANES and DB % SAMPLE_NB == 0 and (SAMPLE_NB * DS) == LANES
    l = 0

    xp = x_prompt.reshape(TP, D_MODEL)
    xs = x_sample.reshape(TS, D_MODEL)

    w_pad = jnp.pad(w_in[l], ((0, 0), (0, IN_PAD - w_in.shape[2]))).astype(BF16)
    wgt = jnp.transpose(w_in[l][:, GATE_COL:GATE_COL + 2 * M_HEADS]).astype(BF16)
    gi = jnp.arange(GROUP_CHUNK) // HEAD_DIM
    gmat = (gi[:, None] == gi[None, :]).astype(BF16)
    qg = jnp.tile(q_norm_g[l], A_HEADS).reshape(1, A_WIDTH)
    kg = jnp.tile(k_norm_g[l], A_KV_HEADS).reshape(1, KV_WIDTH)
    gbias = jnp.concatenate([b_i[l], b_f[l]])
    gbrow = jnp.pad(gbias, (0, LANES - 2 * M_HEADS)).reshape(1, LANES)
    gbcol = gbias.reshape(2 * M_HEADS, 1)
    mng = m_norm_g[l].reshape(1, M_WIDTH)
    cw = conv_w[l]
    cb = conv_b[l].reshape(1, 2 * M_WIDTH)

    proj_w = (g_attn[l].reshape(1, D_MODEL), w_pad, wgt, gmat, qg, kg)
    qn, kn, va, qkm, vm, om, gcol, gt = _inproj(xp, *proj_w)
    qn_s, kn_s, va_s, qkm_s, vm_s, om_s, gcol_s, gt_s = _inproj(xs, *proj_w)

    a_p = _attn_prompt(sinks[l], qn, kn, va, B, S)
    ck = cache_k_win[l].reshape(DB, WINDOW, KV_WIDTH)
    cv = cache_v_win[l].reshape(DB, WINDOW, KV_WIDTH)
    a_s, kwin_s, vwin_s = _attn_sample(sinks[l], qn_s, kn_s, va_s, ck, cv, DB, DS)

    zc = jnp.zeros((B, SUBLANES, 2 * M_WIDTH), F32)
    m_p, C_p, n_p, mm_p = _mlstm(
        qkm, vm, om, gcol, gt, zc,
        jnp.zeros((B, M_HEADS, M_HEAD_DIM, M_HEAD_DIM), F32), jnp.zeros((B, M_HEADS, M_HEAD_DIM), F32),
        jnp.full((B, 1, LANES), NEG, F32), cw, cb, gbrow, gbcol, mng,
        nseq=1, L=PROMPT_CHUNK, ngroups=B, nchunks=S // PROMPT_CHUNK)
    conv_s0 = jnp.pad(state_conv[l], ((0, 0), (SUBLANES - (CONV_W - 1), 0), (0, 0)))
    m0_s = jnp.pad(state_m[l], ((0, 0), (0, LANES - M_HEADS))).reshape(DB, 1, LANES)
    m_s, C_s, n_s, mm_s = _mlstm(
        qkm_s, vm_s, om_s, gcol_s, gt_s, conv_s0, state_C[l], state_n[l], m0_s, cw, cb, gbrow, gbcol, mng,
        nseq=SAMPLE_NB, L=DS, ngroups=DB // SAMPLE_NB, nchunks=1)

    h, hn, route, cnt = _outproj(a_p, a_s, m_p, m_s, xp, xs, w_out[l].astype(BF16), g_ffn[l].reshape(1, D_MODEL),
                                 jnp.transpose(w_router[l]).astype(BF16), b_router[l].reshape(N_EXPERTS, 1))

    nblk = T * TOP_K // MOE_BLOCK + N_EXPERTS
    lp, tab, blk, info = _route_tables(route, cnt, nblk)
    block_e = blk[0, :nblk]
    padrow = info[:, 0]
    npad = info[:, 1]
    nused = info[N_EXPERTS - 1:N_EXPERTS, 2]
    per_assign = lambda a: jnp.transpose(a[:, :TOP_K, :], (0, 2, 1)).reshape(a.shape[0], 1, TOK_TILE * TOP_K)
    lp_tiles = per_assign(lp)
    gate_tiles = per_assign(route)
    xrows = _dispatch(padrow, npad, nused, lp_tiles, tab, hn, nblk * MOE_BLOCK)
    yrows = _moe_blocks(block_e, nused, xrows,
                        w_gate_up[l], b_gate_up[l].reshape(N_EXPERTS, 1, 2 * D_FF),
                        w_down[l], b_down[l].reshape(N_EXPERTS, 1, D_MODEL))
    y_p, y_s = _combine(lp_tiles, gate_tiles, tab, h, yrows, TP)

    y_p = y_p.reshape(B, S, D_MODEL)
    y_s = y_s.reshape(DB, DS, D_MODEL)
    def seq_tail(rows, n):
        return jnp.stack([rows[(b + 1) * S - n:(b + 1) * S] for b in range(B)])

    kwin_p = seq_tail(kn, WINDOW).reshape(B, WINDOW, A_KV_HEADS, HEAD_DIM)
    vwin_p = seq_tail(va, WINDOW).reshape(B, WINDOW, A_KV_HEADS, HEAD_DIM)
    qkm_s = qkm_s.reshape(DB, DS, 2 * M_WIDTH)
    return (y_p, y_s,
            kwin_p[None], vwin_p[None], seq_tail(qkm, CONV_W - 1)[None],
            C_p[None], n_p[None], mm_p[:, 0, :M_HEADS][None],
            kwin_s.reshape(DB, WINDOW, A_KV_HEADS, HEAD_DIM)[None],
            vwin_s.reshape(DB, WINDOW, A_KV_HEADS, HEAD_DIM)[None],
            qkm_s[:, -(CONV_W - 1):][None],
            C_s[None], n_s[None], mm_s[:, 0, :M_HEADS][None])
```

```python
import functools

import jax
import jax.numpy as jnp
from jax import lax
from jax.experimental import pallas as pl
from jax.experimental.pallas import tpu as pltpu

F32 = jnp.float32
BF16 = jnp.bfloat16

D_MODEL = 1024
HEAD_DIM = 64
A_HEADS = 8
A_KV_HEADS = 2
A_GROUP = A_HEADS // A_KV_HEADS
A_WIDTH = A_HEADS * HEAD_DIM
KV_WIDTH = A_KV_HEADS * HEAD_DIM
WINDOW = 128
M_HEADS = 4
M_HEAD_DIM = 128
M_WIDTH = M_HEADS * M_HEAD_DIM
CONV_W = 4
N_EXPERTS = 32
TOP_K = 4
D_FF = D_MODEL
SWIGLU_LIMIT = 7.0
SWIGLU_ALPHA = 1.702
MOE_BLOCK = 512
MOE_QUARTER = 128
EPS = 1e-6
NEG = -1e30

LANES = 128
SUBLANES = 8
GATE_COL = A_WIDTH + 2 * KV_WIDTH + 4 * M_WIDTH
IN_PAD = GATE_COL + LANES
TOK_TILE = 512
PROJ_TILE = 1024
PROJ_VMEM_LIMIT = 58 * 1024 * 1024
GROUP_CHUNK = 256
ATT_QB = 512
ATT_SB = 128
SAMPLE_NB = 16
PROMPT_CHUNK = 256
VMEM_LIMIT = 48 * 1024 * 1024
MOE_VMEM_LIMIT = 56 * 1024 * 1024


def _dot(a, b):
    return jnp.dot(a, b, preferred_element_type=F32)


def _dot_nt(a, b):
    return lax.dot_general(a, b, (((1,), (1,)), ((), ())), preferred_element_type=F32)


def _dot_tn(a, b):
    return lax.dot_general(a, b, (((0,), (0,)), ((), ())), preferred_element_type=F32)


def _split3(x):
    hi = x.astype(BF16)
    r1 = x - hi.astype(F32)
    mid = r1.astype(BF16)
    lo = (r1 - mid.astype(F32)).astype(BF16)
    return hi, mid, lo


def _log_sigmoid(x):
    return jnp.minimum(x, 0.0) - jnp.log1p(jnp.exp(-jnp.abs(x)))


def _sigmoid(x):
    return 0.5 * jnp.tanh(0.5 * x) + 0.5


def _load_row_tiles(ref2, rows):
    return jnp.concatenate([ref2[pl.ds(s, rows, stride=SUBLANES), :] for s in range(SUBLANES)], axis=1)


def _store_row_tiles(ref2, rows, val):
    for s in range(SUBLANES):
        ref2[pl.ds(s, rows, stride=SUBLANES), :] = val[:, s * LANES:(s + 1) * LANES]


def _row_tiles(ref2, first, n):
    start = first * SUBLANES
    if not isinstance(start, int):
        start = pl.multiple_of(start, SUBLANES)
    return ref2.at[pl.ds(start, n * SUBLANES), :]


def _split_specs(n_prompt_tiles, rows, width):
    return (pl.BlockSpec((rows, width), lambda i, *_: (jnp.minimum(i, n_prompt_tiles - 1), 0)),
            pl.BlockSpec((rows, width), lambda i, *_: (jnp.maximum(i - n_prompt_tiles, 0), 0)))


def _inproj_kernel(x_ref, g_ref, w_ref, wgt_ref, gmat_ref, qg_ref, kg_ref,
                   qn_ref, kn_ref, va_ref, qkm_ref, vm_ref, om_ref, gcol_ref, gt_ref):
    x = x_ref[...]
    ms = jnp.mean(x * x, axis=-1, keepdims=True)
    xn = ((x * lax.rsqrt(ms + EPS)) * g_ref[...]).astype(BF16)

    def seg(lo, hi):
        return _dot(xn, w_ref[:, lo:hi])

    def head_norm(z, gmat, g):
        parts = _split3(z * z)
        w = min(GROUP_CHUNK, z.shape[1])
        ss = jnp.concatenate([sum(_dot(p[:, c:c + w], gmat) for p in parts) for c in range(0, z.shape[1], w)], axis=1)
        return (z * lax.rsqrt(ss * (1.0 / HEAD_DIM) + EPS)) * g

    o0 = A_WIDTH
    o1 = o0 + KV_WIDTH
    o2 = o1 + KV_WIDTH
    o3 = o2 + 2 * M_WIDTH
    o4 = o3 + M_WIDTH
    o5 = o4 + M_WIDTH
    qn_ref[...] = head_norm(seg(0, o0), gmat_ref[...], qg_ref[...])
    kn_ref[...] = head_norm(seg(o0, o1), gmat_ref[:KV_WIDTH, :KV_WIDTH], kg_ref[...])
    va_ref[...] = seg(o1, o2)
    qkm_ref[...] = seg(o2, o3)
    vm_ref[...] = seg(o3, o4)
    om_ref[...] = seg(o4, o5)
    gcol_ref[...] = seg(o5, o5 + LANES)
    gt_ref[0] = _dot_nt(wgt_ref[...], xn)


def _inproj(x, g_attn, w_pad, wgt, gmat, qg, kg):
    T = x.shape[0]
    nt = T // PROJ_TILE
    row = lambda w: pl.BlockSpec((PROJ_TILE, w), lambda i: (i, 0))
    full = lambda a: pl.BlockSpec(a.shape, lambda i: (0,) * a.ndim, pipeline_mode=pl.Buffered(1))
    out_shape = (
        jax.ShapeDtypeStruct((T, A_WIDTH), F32),
        jax.ShapeDtypeStruct((T, KV_WIDTH), F32),
        jax.ShapeDtypeStruct((T, KV_WIDTH), F32),
        jax.ShapeDtypeStruct((T, 2 * M_WIDTH), F32),
        jax.ShapeDtypeStruct((T, M_WIDTH), F32),
        jax.ShapeDtypeStruct((T, M_WIDTH), F32),
        jax.ShapeDtypeStruct((T, LANES), F32),
        jax.ShapeDtypeStruct((nt, SUBLANES, PROJ_TILE), F32),
    )
    out_specs = (row(A_WIDTH), row(KV_WIDTH), row(KV_WIDTH), row(2 * M_WIDTH), row(M_WIDTH), row(M_WIDTH),
                 row(LANES), pl.BlockSpec((1, SUBLANES, PROJ_TILE), lambda i: (i, 0, 0)))
    return pl.pallas_call(
        _inproj_kernel,
        grid=(nt,),
        in_specs=[row(D_MODEL), full(g_attn), full(w_pad), full(wgt), full(gmat), full(qg), full(kg)],
        out_specs=out_specs,
        out_shape=out_shape,
        compiler_params=pltpu.CompilerParams(dimension_semantics=("parallel",), vmem_limit_bytes=PROJ_VMEM_LIMIT),
        name="inproj",
    )(x, g_attn, w_pad, wgt, gmat, qg, kg)


def _softmax_sink(pieces, masks, sink_col):
    masked = [jnp.where(mk, s, NEG) for s, mk in zip(pieces, masks)]
    m = sink_col
    for s in masked:
        m = jnp.maximum(m, jnp.max(s, axis=-1, keepdims=True))
    ps = [jnp.exp(s - m) for s in masked]
    den = jnp.exp(sink_col - m)
    for p in ps:
        den = den + jnp.sum(p, axis=-1, keepdims=True)
    return ps, 1.0 / den


def _stack_heads(q, g):
    return jnp.concatenate([q[:, (A_GROUP * g + i) * HEAD_DIM:(A_GROUP * g + i + 1) * HEAD_DIM]
                            for i in range(A_GROUP)], axis=0)


def _sink_col(sink_ref, g, rows_per_head):
    r = lax.broadcasted_iota(jnp.int32, (A_GROUP * rows_per_head, 1), 0)
    col = jnp.zeros((A_GROUP * rows_per_head, 1), F32)
    for i in range(A_GROUP):
        col = jnp.where(r // rows_per_head == i, sink_ref[A_GROUP * g + i], col)
    return col


def _attn_prompt_kernel(sink_ref, q_ref, kp_ref, kc_ref, vp_ref, vc_ref, o_ref):
    j = pl.program_id(1)
    scale = HEAD_DIM ** -0.5
    kall = jnp.concatenate([kp_ref[...], kc_ref[...]], axis=0).astype(BF16)
    vall = jnp.concatenate([vp_ref[...], vc_ref[...]], axis=0).astype(BF16)
    nrow = A_GROUP * ATT_SB
    r = lax.broadcasted_iota(jnp.int32, (nrow, 2 * ATT_SB), 0) % ATT_SB
    c = lax.broadcasted_iota(jnp.int32, (nrow, 2 * ATT_SB), 1)
    band = jnp.logical_and(c >= r, c <= r + WINDOW)
    band0 = jnp.logical_and(band, jnp.logical_or(c >= ATT_SB, j > 0))
    for sb in range(ATT_QB // ATT_SB):
        q = (q_ref[sb * ATT_SB:(sb + 1) * ATT_SB, :] * scale).astype(BF16)
        kwin = kall[sb * ATT_SB:(sb + 2) * ATT_SB]
        vwin = vall[sb * ATT_SB:(sb + 2) * ATT_SB]
        outs = []
        for g in range(A_KV_HEADS):
            lo, hi = g * HEAD_DIM, (g + 1) * HEAD_DIM
            s = _dot_nt(_stack_heads(q, g), kwin[:, lo:hi])
            (p,), inv = _softmax_sink([s], [band0 if sb == 0 else band], _sink_col(sink_ref, g, ATT_SB))
            o = _dot(p.astype(BF16), vwin[:, lo:hi]) * inv
            outs += [o[i * ATT_SB:(i + 1) * ATT_SB] for i in range(A_GROUP)]
        o_ref[sb * ATT_SB:(sb + 1) * ATT_SB, :] = jnp.concatenate(outs, axis=1)


def _attn_prompt(sinks, qn, kn, va, batch, seq):
    nq = seq // ATT_QB
    ratio = ATT_QB // ATT_SB
    cur = lambda w: pl.BlockSpec((ATT_QB, w), lambda b, j: (b * nq + j, 0))
    prev = lambda w: pl.BlockSpec((ATT_SB, w), lambda b, j: (jnp.maximum((b * nq + j) * ratio - 1, 0), 0))
    return pl.pallas_call(
        _attn_prompt_kernel,
        grid=(batch, nq),
        in_specs=[pl.BlockSpec(memory_space=pltpu.SMEM), cur(A_WIDTH), prev(KV_WIDTH), cur(KV_WIDTH),
                  prev(KV_WIDTH), cur(KV_WIDTH)],
        out_specs=cur(A_WIDTH),
        out_shape=jax.ShapeDtypeStruct((batch * seq, A_WIDTH), F32),
        compiler_params=pltpu.CompilerParams(dimension_semantics=("parallel", "parallel"),
                                             vmem_limit_bytes=VMEM_LIMIT),
        name="attn_prompt",
    )(sinks, qn, kn, kn, va, va)


def _attn_sample_kernel(dec, sink_ref, q_ref, kn_ref, vn_ref, ck_ref, cv_ref, o_ref, kw_ref, vw_ref):
    scale = HEAD_DIM ** -0.5
    rows = SAMPLE_NB * dec
    knew = kn_ref[...]
    vnew = vn_ref[...]
    knew_b = knew.astype(BF16)
    vnew_b = vnew.astype(BF16)
    nrow = A_GROUP * dec
    t = lax.broadcasted_iota(jnp.int32, (nrow, WINDOW), 0) % dec
    c = lax.broadcasted_iota(jnp.int32, (nrow, WINDOW), 1)
    m_cache = c >= t
    cn = lax.broadcasted_iota(jnp.int32, (nrow, rows), 1)
    tn = lax.broadcasted_iota(jnp.int32, (nrow, rows), 0) % dec
    for i in range(SAMPLE_NB):
        q = (q_ref[i * dec:(i + 1) * dec, :] * scale).astype(BF16)
        ck = ck_ref[i].astype(BF16)
        cv = cv_ref[i].astype(BF16)
        m_new = jnp.logical_and(cn // dec == i, cn % dec <= tn)
        outs = []
        for g in range(A_KV_HEADS):
            lo, hi = g * HEAD_DIM, (g + 1) * HEAD_DIM
            qs = _stack_heads(q, g)
            s_c = _dot_nt(qs, ck[:, lo:hi])
            s_n = _dot_nt(qs, knew_b[:, lo:hi])
            (p_c, p_n), inv = _softmax_sink([s_c, s_n], [m_cache, m_new], _sink_col(sink_ref, g, dec))
            o = (_dot(p_c.astype(BF16), cv[:, lo:hi]) + _dot(p_n.astype(BF16), vnew_b[:, lo:hi])) * inv
            outs += [o[h * dec:(h + 1) * dec] for h in range(A_GROUP)]
        o_ref[i * dec:(i + 1) * dec, :] = jnp.concatenate(outs, axis=1)
        kw_ref[i, 0:WINDOW - dec, :] = ck_ref[i, dec:WINDOW, :]
        kw_ref[i, WINDOW - dec:WINDOW, :] = knew[i * dec:(i + 1) * dec]
        vw_ref[i, 0:WINDOW - dec, :] = cv_ref[i, dec:WINDOW, :]
        vw_ref[i, WINDOW - dec:WINDOW, :] = vnew[i * dec:(i + 1) * dec]


def _attn_sample(sinks, qn, kn, va, ck, cv, dbatch, dec):
    rows = SAMPLE_NB * dec
    tokrow = lambda w: pl.BlockSpec((rows, w), lambda i: (i, 0))
    cache = pl.BlockSpec((SAMPLE_NB, WINDOW, KV_WIDTH), lambda i: (i, 0, 0))
    return pl.pallas_call(
        functools.partial(_attn_sample_kernel, dec),
        grid=(dbatch // SAMPLE_NB,),
        in_specs=[pl.BlockSpec(memory_space=pltpu.SMEM), tokrow(A_WIDTH), tokrow(KV_WIDTH), tokrow(KV_WIDTH),
                  cache, cache],
        out_specs=(pl.BlockSpec((rows, A_WIDTH), lambda i: (i, 0)), cache, cache),
        out_shape=(jax.ShapeDtypeStruct((dbatch * dec, A_WIDTH), F32),
                   jax.ShapeDtypeStruct((dbatch, WINDOW, KV_WIDTH), F32),
                   jax.ShapeDtypeStruct((dbatch, WINDOW, KV_WIDTH), F32)),
        compiler_params=pltpu.CompilerParams(dimension_semantics=("parallel",), vmem_limit_bytes=VMEM_LIMIT),
        name="attn_sample",
    )(sinks, qn, kn, va, ck, cv)


def _mlstm_kernel(nseq, L, qk_ref, v_ref, o_ref, gcol_ref, gt_ref, conv0_ref, c0_ref, n0_ref, m0_ref,
                  cw_ref, cb_ref, gbrow_ref, gbcol_ref, mng_ref, mask_ref,
                  out_ref, cst_ref, nst_ref, mst_ref, prev_ref):
    R = nseq * L
    ci = pl.program_id(1)

    @pl.when(ci == 0)
    def _():
        cst_ref[...] = c0_ref[...]
        nst_ref[...] = n0_ref[...]
        mst_ref[...] = m0_ref[...]
        prev_ref[...] = conv0_ref[0]

    raw = qk_ref[...]
    row = lax.broadcasted_iota(jnp.int32, (R, 1), 0)
    tpos = row % L
    rseq = row // L
    acc = raw * cw_ref[CONV_W - 1:CONV_W, :] + cb_ref[...]
    if nseq == 1:
        prev8 = prev_ref[...]
        t8 = lax.broadcasted_iota(jnp.int32, (SUBLANES, 1), 0)
    else:
        prevsrc = conv0_ref[...].reshape(R, 2 * M_WIDTH)
    for k in range(1, CONV_W):
        rolled = pltpu.roll(raw, k, 0)
        if nseq == 1:
            head = jnp.where(t8 >= k, rolled[0:SUBLANES], pltpu.roll(prev8, k, 0))
            sh = jnp.concatenate([head, rolled[SUBLANES:]], axis=0)
        else:
            sh = jnp.where(tpos >= k, rolled, pltpu.roll(prevsrc, R - SUBLANES + k, 0))
        acc = acc + sh * cw_ref[CONV_W - 1 - k:CONV_W - k, :]
    if nseq == 1:
        prev_ref[...] = raw[R - SUBLANES:R]
    qkc = acc * _sigmoid(acc)

    gc = gcol_ref[...] + gbrow_ref[...]
    gr = gt_ref[0] + gbcol_ref[...]
    lsc = _log_sigmoid(gc)
    lsr = _log_sigmoid(gr)
    mb = mask_ref[...]
    maskb = mb > 0
    bcol = sum(_dot(mb, p) for p in _split3(lsc))
    brow = sum(_dot_nt(p, mb) for p in _split3(lsr))

    lane = lax.broadcasted_iota(jnp.int32, (1, LANES), 1)
    m_new = [jnp.zeros((1, LANES), F32) for _ in range(nseq)]
    for h in range(M_HEADS):
        sl = slice(h * M_HEAD_DIM, (h + 1) * M_HEAD_DIM)
        qh = qkc[:, sl]
        kh = qkc[:, M_WIDTH + h * M_HEAD_DIM:M_WIDTH + (h + 1) * M_HEAD_DIM] * (M_HEAD_DIM ** -0.5)
        vh = v_ref[:, sl]
        qb, kb, vb = qh.astype(BF16), kh.astype(BF16), vh.astype(BF16)
        ig_c = gc[:, h:h + 1]
        b_c = bcol[:, M_HEADS + h:M_HEADS + h + 1]
        ig_r = gr[h:h + 1, :]
        b_r = brow[M_HEADS + h:M_HEADS + h + 1, :]
        if nseq == 1:
            m0c = mst_ref[0][:, h:h + 1]
            n0rows = nst_ref[0, h:h + 1, :]
        else:
            m0c = jnp.zeros((R, 1), F32)
            n0rows = jnp.zeros((R, M_HEAD_DIM), F32)
            for s in range(nseq):
                m0c = jnp.where(rseq == s, mst_ref[s][:, h:h + 1], m0c)
                n0rows = jnp.where(rseq == s, nst_ref[s, h:h + 1, :], n0rows)
        dm = jnp.where(maskb, b_c - b_r + ig_r, NEG)
        a_c = b_c + m0c
        m_c = jnp.maximum(a_c, jnp.max(dm, axis=-1, keepdims=True))
        w = jnp.exp(dm - m_c)
        sc = jnp.exp(a_c - m_c)
        wqk = w * _dot_nt(qb, kb)
        if nseq == 1:
            inter = _dot_nt(qb, cst_ref[0, h].astype(BF16))
        else:
            inter = jnp.zeros((R, M_HEAD_DIM), F32)
            for s in range(nseq):
                qs = jnp.where(rseq == s, qh, 0.0).astype(BF16)
                inter = inter + _dot_nt(qs, cst_ref[s, h].astype(BF16))
        num = _dot(wqk.astype(BF16), vb) + sc * inter
        den = jnp.sum(wqk, axis=-1, keepdims=True) + sc * jnp.sum(qh * n0rows, axis=-1, keepdims=True)
        hh = num / jnp.maximum(jnp.abs(den), jnp.exp(-m_c))

        for s in range(nseq):
            e = s * L + L - 1
            m_end = m_c[e:e + 1, :]
            wend = jnp.exp(b_c[e:e + 1, :] - b_c + ig_c - m_end)
            if nseq > 1:
                wend = jnp.where(rseq == s, wend, 0.0)
            sce = jnp.exp(a_c[e:e + 1, :] - m_end)
            c_new = sce * cst_ref[s, h] + _dot_tn((vh * wend).astype(BF16), kb)
            n_new = sce * nst_ref[s, h:h + 1, :] + jnp.sum(wend * kh, axis=0, keepdims=True)
            cst_ref[s, h] = c_new
            nst_ref[s, h:h + 1, :] = n_new
            m_new[s] = jnp.where(lane == h, m_end, m_new[s])

        hn = (hh * lax.rsqrt(jnp.mean(hh * hh, axis=-1, keepdims=True) + EPS)) * mng_ref[:, sl]
        out_ref[:, sl] = _sigmoid(o_ref[:, sl]) * hn
    for s in range(nseq):
        mst_ref[s] = m_new[s]


def _mlstm(qkm, vm, om, gcol, gt, conv0, c0, n0, m0, cw, cb, gbrow, gbcol, mng, nseq, L, ngroups, nchunks):
    R = nseq * L
    per_tile = gt.shape[2] // R
    tok = lambda w: pl.BlockSpec((R, w), lambda g, c: (g * nchunks + c, 0))
    gt_spec = pl.BlockSpec((1, SUBLANES, R),
                           lambda g, c: ((g * nchunks + c) // per_tile, 0, (g * nchunks + c) % per_tile))
    full = lambda a: pl.BlockSpec(a.shape, lambda g, c: (0,) * a.ndim)
    st4 = pl.BlockSpec((nseq, M_HEADS, M_HEAD_DIM, M_HEAD_DIM), lambda g, c: (g, 0, 0, 0))
    st3 = pl.BlockSpec((nseq, M_HEADS, M_HEAD_DIM), lambda g, c: (g, 0, 0))
    stm = pl.BlockSpec((nseq, 1, LANES), lambda g, c: (g, 0, 0))
    conv_spec = pl.BlockSpec((nseq, SUBLANES, 2 * M_WIDTH), lambda g, c: (g, 0, 0))
    r = jnp.arange(R)
    mask = ((r[:, None] // L == r[None, :] // L) & (r[None, :] <= r[:, None])).astype(BF16)
    nstate = ngroups * nseq
    return pl.pallas_call(
        functools.partial(_mlstm_kernel, nseq, L),
        grid=(ngroups, nchunks),
        in_specs=[tok(2 * M_WIDTH), tok(M_WIDTH), tok(M_WIDTH), tok(LANES), gt_spec, conv_spec, st4, st3, stm,
                  full(cw), full(cb), full(gbrow), full(gbcol), full(mng), full(mask)],
        out_specs=(pl.BlockSpec((R, M_WIDTH), lambda g, c: (g * nchunks + c, 0)), st4, st3, stm),
        out_shape=(jax.ShapeDtypeStruct((ngroups * nchunks * R, M_WIDTH), F32),
                   jax.ShapeDtypeStruct((nstate, M_HEADS, M_HEAD_DIM, M_HEAD_DIM), F32),
                   jax.ShapeDtypeStruct((nstate, M_HEADS, M_HEAD_DIM), F32),
                   jax.ShapeDtypeStruct((nstate, 1, LANES), F32)),
        scratch_shapes=[pltpu.VMEM((SUBLANES, 2 * M_WIDTH), F32)],
        compiler_params=pltpu.CompilerParams(dimension_semantics=("parallel", "arbitrary"),
                                             vmem_limit_bytes=VMEM_LIMIT),
        name="mlstm_n%d" % nseq,
    )(qkm, vm, om, gcol, gt, conv0, c0, n0, m0, cw, cb, gbrow, gbcol, mng, mask)


def _outproj_kernel(npt, ap_ref, as_ref, mp_ref, ms_ref, xp_ref, xs_ref, wo_ref, g_ref, wrt_ref, brc_ref,
                    h_ref, hn_ref, route_ref, cnt_ref):
    def project(a_ref, m_ref, x_ref):
        h_ref[...] = (x_ref[...] + _dot(a_ref[...].astype(BF16), wo_ref[0:A_WIDTH, :])
                      + _dot(m_ref[...].astype(BF16), wo_ref[A_WIDTH:A_WIDTH + M_WIDTH, :]))

    @pl.when(pl.program_id(0) < npt)
    def _():
        project(ap_ref, mp_ref, xp_ref)

    @pl.when(pl.program_id(0) >= npt)
    def _():
        project(as_ref, ms_ref, xs_ref)

    h = h_ref[...]
    hn = (h * lax.rsqrt(jnp.mean(h * h, axis=-1, keepdims=True) + EPS)) * g_ref[...]
    _store_row_tiles(hn_ref, TOK_TILE, hn)
    logits = _dot_nt(wrt_ref[...], hn.astype(BF16)) + brc_ref[...]
    eidx = lax.broadcasted_iota(jnp.int32, logits.shape, 0).astype(F32)
    picked = jnp.zeros(logits.shape, F32)
    top0 = None
    den = None
    es = []
    ids = []
    for k in range(TOP_K):
        mx = jnp.max(logits, axis=0, keepdims=True)
        idx = jnp.min(jnp.where(logits == mx, eidx, float(N_EXPERTS)), axis=0, keepdims=True)
        if k == 0:
            top0 = mx
        e = jnp.exp(mx - top0)
        den = e if den is None else den + e
        es.append(e)
        ids.append(idx)
        hit = eidx == idx
        picked = jnp.where(hit, 1.0, picked)
        logits = jnp.where(hit, -jnp.inf, logits)
    route_ref[0] = jnp.concatenate([e / den for e in es] + ids, axis=0)

    @pl.when(pl.program_id(0) == 0)
    def _():
        cnt_ref[...] = jnp.zeros_like(cnt_ref)

    cnt_ref[...] += jnp.broadcast_to(jnp.sum(picked, axis=1, keepdims=True), cnt_ref.shape)


def _outproj(a_p, a_s, m_p, m_s, xp, xs, w_out, g_ffn, wr_t, br_col):
    T = xp.shape[0] + xs.shape[0]
    nt = T // TOK_TILE
    npt = xp.shape[0] // TOK_TILE
    row = lambda w: pl.BlockSpec((TOK_TILE, w), lambda i: (i, 0))
    full = lambda a: pl.BlockSpec(a.shape, lambda i: (0,) * a.ndim)
    return pl.pallas_call(
        functools.partial(_outproj_kernel, npt),
        grid=(nt,),
        in_specs=[*_split_specs(npt, TOK_TILE, A_WIDTH), *_split_specs(npt, TOK_TILE, M_WIDTH),
                  *_split_specs(npt, TOK_TILE, D_MODEL), full(w_out), full(g_ffn), full(wr_t), full(br_col)],
        out_specs=(row(D_MODEL), pl.BlockSpec((TOK_TILE * SUBLANES, LANES), lambda i: (i, 0)),
                   pl.BlockSpec((1, 2 * TOP_K, TOK_TILE), lambda i: (i, 0, 0)),
                   pl.BlockSpec((N_EXPERTS, LANES), lambda i: (0, 0))),
        out_shape=(jax.ShapeDtypeStruct((T, D_MODEL), F32), jax.ShapeDtypeStruct((T * SUBLANES, LANES), F32),
                   jax.ShapeDtypeStruct((nt, 2 * TOP_K, TOK_TILE), F32),
                   jax.ShapeDtypeStruct((N_EXPERTS, LANES), F32)),
        compiler_params=pltpu.CompilerParams(dimension_semantics=("arbitrary",), vmem_limit_bytes=VMEM_LIMIT),
        name="outproj_router",
    )(a_p, a_s, m_p, m_s, xp, xs, w_out, g_ffn, wr_t, br_col)


def _route_kernel(nblk_pad, route_ref, cnt_ref, ustrict_ref, lstrict_ref, lp_ref, tab_ref, blk_ref, info_ref,
                  carry_ref):
    i = pl.program_id(0)
    cnt = cnt_ref[...]
    nb_e = jnp.floor((cnt + (MOE_BLOCK - 1.0)) * (1.0 / MOE_BLOCK))
    bstart = sum(_dot(lstrict_ref[...], p) for p in _split3(nb_e))
    bend = bstart + nb_e
    row_start = bstart * float(MOE_BLOCK)

    @pl.when(i == 0)
    def _():
        carry_ref[...] = jnp.zeros_like(carry_ref)
        bi = lax.broadcasted_iota(jnp.int32, (N_EXPERTS, nblk_pad), 1).astype(F32)
        done = jnp.where(bend[:, 0:1] <= bi, 1.0, 0.0)
        be = jnp.minimum(jnp.sum(done, axis=0, keepdims=True), N_EXPERTS - 1.0)
        blk_ref[...] = jnp.broadcast_to(be, blk_ref.shape).astype(jnp.int32)
        lane = lax.broadcasted_iota(jnp.int32, (N_EXPERTS, LANES), 1)
        info = jnp.where(lane == 0, row_start + cnt, 0.0)
        info = jnp.where(lane == 1, nb_e * float(MOE_BLOCK) - cnt, info)
        info = jnp.where(lane == 2, bend, info)
        info_ref[...] = info.astype(jnp.int32)

    r = route_ref[0]
    eidx = lax.broadcasted_iota(jnp.int32, (N_EXPERTS, TOK_TILE), 0).astype(F32)
    sel = [eidx == r[TOP_K + k:TOP_K + k + 1, :] for k in range(TOP_K)]
    oh = jnp.zeros((N_EXPERTS, TOK_TILE), F32)
    for k in range(TOP_K):
        oh = jnp.where(sel[k], 1.0, oh)
    cnt_t = jnp.broadcast_to(jnp.sum(oh, axis=1, keepdims=True), (N_EXPERTS, LANES))
    seg = sum(_dot(lstrict_ref[...], p) for p in _split3(cnt_t))
    local = _dot(oh.astype(BF16), ustrict_ref[...]) + seg[:, 0:1]
    rows = [jnp.sum(jnp.where(sel[k], local, 0.0), axis=0, keepdims=True) for k in range(TOP_K)]
    lp_ref[0] = (jnp.concatenate(rows + [jnp.zeros((TOP_K, TOK_TILE), F32)], axis=0)
                 * float(SUBLANES)).astype(jnp.int32)
    diag = (lax.broadcasted_iota(jnp.int32, (N_EXPERTS, LANES), 0)
            == lax.broadcasted_iota(jnp.int32, (N_EXPERTS, LANES), 1))
    to_lanes = lambda col: jnp.sum(jnp.where(diag, col, 0.0), axis=0, keepdims=True)
    tab = [to_lanes(seg), to_lanes(cnt_t), to_lanes(row_start + carry_ref[...])]
    tab_ref[0] = jnp.concatenate(tab + [jnp.zeros((SUBLANES - len(tab), LANES), F32)], axis=0).astype(jnp.int32)
    carry_ref[...] += cnt_t


def _route_tables(route, cnt, nblk):
    nt = route.shape[0]
    nblk_pad = -(-nblk // LANES) * LANES
    a = jnp.arange(TOK_TILE)
    ustrict = (a[:, None] < a[None, :]).astype(BF16)
    b = jnp.arange(N_EXPERTS)
    lstrict = (b[:, None] > b[None, :]).astype(BF16)
    full = lambda x: pl.BlockSpec(x.shape, lambda i: (0,) * x.ndim)
    tile = pl.BlockSpec((1, 2 * TOP_K, TOK_TILE), lambda i: (i, 0, 0))
    return pl.pallas_call(
        functools.partial(_route_kernel, nblk_pad),
        grid=(nt,),
        in_specs=[tile, full(cnt), full(ustrict), full(lstrict)],
        out_specs=(tile, pl.BlockSpec((1, SUBLANES, LANES), lambda i: (i, 0, 0)),
                   pl.BlockSpec((SUBLANES, nblk_pad), lambda i: (0, 0)),
                   pl.BlockSpec((N_EXPERTS, LANES), lambda i: (0, 0))),
        out_shape=(jax.ShapeDtypeStruct((nt, 2 * TOP_K, TOK_TILE), jnp.int32),
                   jax.ShapeDtypeStruct((nt, SUBLANES, LANES), jnp.int32),
                   jax.ShapeDtypeStruct((SUBLANES, nblk_pad), jnp.int32),
                   jax.ShapeDtypeStruct((N_EXPERTS, LANES), jnp.int32)),
        scratch_shapes=[pltpu.VMEM((N_EXPERTS, LANES), F32)],
        compiler_params=pltpu.CompilerParams(dimension_semantics=("arbitrary",), vmem_limit_bytes=VMEM_LIMIT),
        name="route_tables",
    )(route, cnt, ustrict, lstrict)


RUN_PIECES = tuple(1 << b for b in range(9, -1, -1))


def _for_run_pieces(count, fn):
    for size in RUN_PIECES:
        @pl.when((count & size) != 0)
        def _(size=size):
            fn(count & ~(2 * size - 1), size)


def _dispatch_kernel(padrow_ref, npad_ref, nu_ref, lp_ref, tab_ref, hn_ref, xs_hbm, srt, zbuf, sem, zsem):
    i = pl.program_id(0)
    nt = pl.num_programs(0)
    slot = i % 2

    @pl.when(i == 0)
    def _():
        zbuf[...] = jnp.zeros_like(zbuf)
        nblk = xs_hbm.shape[0] // (MOE_BLOCK * SUBLANES)

        def tail_start(b, c):
            pltpu.make_async_copy(zbuf, _row_tiles(xs_hbm, b * MOE_BLOCK, MOE_BLOCK), zsem).start()
            return c

        def tail_wait(b, c):
            pltpu.make_async_copy(zbuf, _row_tiles(xs_hbm, b * MOE_BLOCK, MOE_BLOCK), zsem).wait()
            return c

        lax.fori_loop(nu_ref[0], nblk, tail_start, 0)
        lax.fori_loop(nu_ref[0], nblk, tail_wait, 0)

        def pad_runs(wait):
            def per_expert(e, carry):
                first, count = padrow_ref[e], npad_ref[e]

                def piece(off, size):
                    cp = pltpu.make_async_copy(_row_tiles(zbuf, 0, size), _row_tiles(xs_hbm, first + off, size), zsem)
                    cp.wait() if wait else cp.start()
                _for_run_pieces(count, piece)
                return carry
            lax.fori_loop(0, N_EXPERTS, per_expert, 0)

        pad_runs(False)
        pad_runs(True)

    def wait_runs(s):
        for _ in range(TOP_K):
            pltpu.make_async_copy(_row_tiles(srt.at[s], 0, TOK_TILE), _row_tiles(xs_hbm, 0, TOK_TILE), sem.at[s]).wait()

    def step(s):
        @pl.when(i >= 2)
        def _():
            wait_runs(s)

        def permute(j, carry):
            for u in range(SUBLANES):
                t = j * SUBLANES + u
                row = hn_ref[pl.ds(pl.multiple_of(t * SUBLANES, SUBLANES), SUBLANES), :]
                for k in range(TOP_K):
                    p = lp_ref[0, 0, t * TOP_K + k]
                    srt[s, pl.ds(pl.multiple_of(p, SUBLANES), SUBLANES), :] = row
            return carry

        lax.fori_loop(0, TOK_TILE // SUBLANES, permute, 0)

        def send_run(e, carry):
            local, count, first = tab_ref[0, 0, e], tab_ref[0, 1, e], tab_ref[0, 2, e]

            def piece(off, size):
                pltpu.make_async_copy(_row_tiles(srt.at[s], local + off, size),
                                      _row_tiles(xs_hbm, first + off, size), sem.at[s]).start()
            _for_run_pieces(count, piece)
            return carry

        lax.fori_loop(0, N_EXPERTS, send_run, 0)

        @pl.when(i == nt - 1)
        def _():
            @pl.when(nt >= 2)
            def _():
                wait_runs(1 - s)

            wait_runs(s)

    for s in range(2):
        pl.when(slot == s)(functools.partial(step, s))


def _dispatch(padrow, npad, nused, lp_tiles, tab, hn, n_rows):
    nt = lp_tiles.shape[0]
    grid_spec = pltpu.PrefetchScalarGridSpec(
        num_scalar_prefetch=3,
        grid=(nt,),
        in_specs=[pl.BlockSpec((1, 1, TOK_TILE * TOP_K), lambda i, *_: (i, 0, 0), memory_space=pltpu.SMEM),
                  pl.BlockSpec((1, SUBLANES, LANES), lambda i, *_: (i, 0, 0), memory_space=pltpu.SMEM),
                  pl.BlockSpec((TOK_TILE * SUBLANES, LANES), lambda i, *_: (i, 0))],
        out_specs=pl.BlockSpec(memory_space=pl.ANY),
        scratch_shapes=[pltpu.VMEM((2, TOK_TILE * TOP_K * SUBLANES, LANES), F32),
                        pltpu.VMEM((MOE_BLOCK * SUBLANES, LANES), F32),
                        pltpu.SemaphoreType.DMA((2,)), pltpu.SemaphoreType.DMA(())],
    )
    return pl.pallas_call(
        _dispatch_kernel,
        grid_spec=grid_spec,
        out_shape=jax.ShapeDtypeStruct((n_rows * SUBLANES, LANES), F32),
        compiler_params=pltpu.CompilerParams(dimension_semantics=("arbitrary",), vmem_limit_bytes=VMEM_LIMIT),
        name="moe_dispatch",
    )(padrow, npad, nused, lp_tiles, tab, hn)


def _moe_kernel(be_ref, nu_ref, first_ref, slot_ref, nxt_ref, nvalid_ref, x_ref, bgu_ref, bd_ref, wgu_hbm, wd_hbm, y_ref,
                wgu_buf, wd_buf, wsem):
    i = pl.program_id(0)
    used = i < nu_ref[0]
    s = slot_ref[i]

    def fetch(e, sl):
        return (pltpu.make_async_copy(wgu_hbm.at[e], wgu_buf.at[sl], wsem.at[0, sl]),
                pltpu.make_async_copy(wd_hbm.at[e], wd_buf.at[sl], wsem.at[1, sl]))

    @pl.when(jnp.logical_and(used, first_ref[i] == 1))
    def _():
        @pl.when(i == 0)
        def _():
            for c in fetch(be_ref[0], 0):
                c.start()

        for c in fetch(be_ref[i], s):
            c.wait()

        @pl.when(nxt_ref[i] >= 0)
        def _():
            for c in fetch(nxt_ref[i], 1 - s):
                c.start()

    def expert(rows):
        x = _load_row_tiles(x_ref, rows).astype(BF16)
        hb = _dot(x, wgu_buf[s].astype(BF16)) + bgu_ref[0]
        glu = jnp.minimum(hb[:, :D_FF], SWIGLU_LIMIT)
        lin = jnp.clip(hb[:, D_FF:], -SWIGLU_LIMIT, SWIGLU_LIMIT)
        act = glu * _sigmoid(SWIGLU_ALPHA * glu) * (lin + 1.0)
        _store_row_tiles(y_ref, rows, _dot(act.astype(BF16), wd_buf[s].astype(BF16)) + bd_ref[0])
        if rows < MOE_BLOCK:
            y_ref[rows * SUBLANES:, :] = jnp.zeros(((MOE_BLOCK - rows) * SUBLANES, LANES), F32)

    nv = nvalid_ref[i]
    for rows in range(MOE_QUARTER, MOE_BLOCK + 1, MOE_QUARTER):
        lo = rows - MOE_QUARTER if rows > MOE_QUARTER else -1
        pl.when(jnp.logical_and(used, jnp.logical_and(nv > lo, nv <= rows)))(functools.partial(expert, rows))

    @pl.when(i >= nu_ref[0])
    def _():
        y_ref[...] = jnp.zeros_like(y_ref)


def _moe_blocks(block_e, nused, padrow, xs, wgu, bgu, wd, bd):
    nblk = block_e.shape[0]
    pad_at = jnp.sum(jnp.where(block_e[:, None] == jnp.arange(N_EXPERTS)[None, :], padrow[None, :], 0), axis=1)
    nvalid = jnp.clip(pad_at - jnp.arange(nblk, dtype=jnp.int32) * MOE_BLOCK, 0, MOE_BLOCK).astype(jnp.int32)
    idx = jnp.arange(nblk, dtype=jnp.int32)
    first = (idx < nused[0]) & ((idx == 0) | (block_e != jnp.roll(block_e, 1)))
    slot = ((jnp.cumsum(first.astype(jnp.int32)) - 1) % 2).astype(jnp.int32)
    first_pos = jnp.where(first, idx, nblk)
    later = jnp.concatenate([first_pos[1:], jnp.full((1,), nblk, jnp.int32)])
    next_pos = lax.cummin(later, reverse=True)
    nxt = jnp.sum(jnp.where(idx[None, :] == next_pos[:, None], block_e[None, :] + 1, 0), axis=1) - 1
    grid_spec = pltpu.PrefetchScalarGridSpec(
        num_scalar_prefetch=6,
        grid=(nblk,),
        in_specs=[
            pl.BlockSpec((MOE_BLOCK * SUBLANES, LANES), lambda i, be, *_: (i, 0)),
            pl.BlockSpec((1, 1, 2 * D_FF), lambda i, be, *_: (be[i], 0, 0)),
            pl.BlockSpec((1, 1, D_MODEL), lambda i, be, *_: (be[i], 0, 0)),
            pl.BlockSpec(memory_space=pl.ANY),
            pl.BlockSpec(memory_space=pl.ANY),
        ],
        out_specs=pl.BlockSpec((MOE_BLOCK * SUBLANES, LANES), lambda i, be, *_: (i, 0)),
        scratch_shapes=[pltpu.VMEM((2, D_MODEL, 2 * D_FF), F32), pltpu.VMEM((2, D_FF, D_MODEL), F32),
                        pltpu.SemaphoreType.DMA((2, 2))],
    )
    return pl.pallas_call(
        _moe_kernel,
        grid_spec=grid_spec,
        out_shape=jax.ShapeDtypeStruct(xs.shape, F32),
        compiler_params=pltpu.CompilerParams(dimension_semantics=("arbitrary",), vmem_limit_bytes=MOE_VMEM_LIMIT),
        name="moe_blocks",
    )(block_e, nused, first.astype(jnp.int32), slot, nxt.astype(jnp.int32), nvalid, xs, bgu, bd, wgu, wd)


def _combine_kernel(npt, lp_ref, gate_ref, tabc_ref, tabn_ref, h_ref, ys_hbm, yp_ref, ysm_ref, srt, acc, sem):
    i = pl.program_id(0)
    nt = pl.num_programs(0)
    slot = i % 2

    def fetch_runs(tab_ref, s):
        def run(e, carry):
            local, count, first = tab_ref[0, 0, e], tab_ref[0, 1, e], tab_ref[0, 2, e]

            def piece(off, size):
                pltpu.make_async_copy(_row_tiles(ys_hbm, first + off, size),
                                      _row_tiles(srt.at[s], local + off, size), sem.at[s]).start()
            _for_run_pieces(count, piece)
            return carry
        lax.fori_loop(0, N_EXPERTS, run, 0)

    @pl.when(i == 0)
    def _():
        fetch_runs(tabc_ref, 0)

    def step(s):
        @pl.when(i + 1 < nt)
        def _():
            fetch_runs(tabn_ref, 1 - s)

        for _ in range(TOP_K):
            pltpu.make_async_copy(_row_tiles(ys_hbm, 0, TOK_TILE), _row_tiles(srt.at[s], 0, TOK_TILE),
                                  sem.at[s]).wait()

        def gather(j, carry):
            for u in range(SUBLANES):
                t = j * SUBLANES + u
                tot = None
                for k in range(TOP_K):
                    p = lp_ref[0, 0, t * TOP_K + k]
                    term = (srt[s, pl.ds(pl.multiple_of(p, SUBLANES), SUBLANES), :]
                            * gate_ref[0, 0, t * TOP_K + k])
                    tot = term if tot is None else tot + term
                acc[pl.ds(pl.multiple_of(t * SUBLANES, SUBLANES), SUBLANES), :] = tot
            return carry

        lax.fori_loop(0, TOK_TILE // SUBLANES, gather, 0)

    for s in range(2):
        pl.when(slot == s)(functools.partial(step, s))
    y = h_ref[...] + _load_row_tiles(acc, TOK_TILE)

    @pl.when(i < npt)
    def _():
        yp_ref[...] = y

    @pl.when(i >= npt)
    def _():
        ysm_ref[...] = y


def _combine(lp_tiles, gate_tiles, tab, h, ys, n_prompt_rows):
    T = h.shape[0]
    nt = T // TOK_TILE
    npt = n_prompt_rows // TOK_TILE
    per_assign = pl.BlockSpec((1, 1, TOK_TILE * TOP_K), lambda i: (i, 0, 0), memory_space=pltpu.SMEM)
    tab_blk = lambda imap: pl.BlockSpec((1, SUBLANES, LANES), imap, memory_space=pltpu.SMEM)
    return pl.pallas_call(
        functools.partial(_combine_kernel, npt),
        grid=(nt,),
        in_specs=[per_assign, per_assign,
                  tab_blk(lambda i: (i, 0, 0)), tab_blk(lambda i: (jnp.minimum(i + 1, nt - 1), 0, 0)),
                  pl.BlockSpec((TOK_TILE, D_MODEL), lambda i: (i, 0)),
                  pl.BlockSpec(memory_space=pl.ANY)],
        out_specs=(pl.BlockSpec((TOK_TILE, D_MODEL), lambda i: (jnp.minimum(i, npt - 1), 0)),
                   pl.BlockSpec((TOK_TILE, D_MODEL), lambda i: (jnp.maximum(i - npt, 0), 0))),
        out_shape=(jax.ShapeDtypeStruct((n_prompt_rows, D_MODEL), F32),
                   jax.ShapeDtypeStruct((T - n_prompt_rows, D_MODEL), F32)),
        scratch_shapes=[pltpu.VMEM((2, TOK_TILE * TOP_K * SUBLANES, LANES), F32),
                        pltpu.VMEM((TOK_TILE * SUBLANES, LANES), F32), pltpu.SemaphoreType.DMA((2,))],
        compiler_params=pltpu.CompilerParams(dimension_semantics=("arbitrary",), vmem_limit_bytes=VMEM_LIMIT),
        name="moe_combine",
    )(lp_tiles, gate_tiles, tab, tab, h, ys)


def kernel(x_prompt, x_sample, cache_k_win, cache_v_win, state_conv, state_C, state_n, state_m, g_attn, w_in, b_i,
           b_f, q_norm_g, k_norm_g, sinks, conv_w, conv_b, m_norm_g, w_out, g_ffn, w_router, b_router, w_gate_up,
           b_gate_up, w_down, b_down):
    depth = g_attn.shape[0]
    assert depth == 1
    B, S, _ = x_prompt.shape
    DB, DS, _ = x_sample.shape
    TP = B * S
    TS = DB * DS
    T = TP + TS
    assert T % TOK_TILE == 0 and TP % TOK_TILE == 0 and S % ATT_QB == 0 and S % PROMPT_CHUNK == 0
    assert DS == SUBLANES and DB % SAMPLE_NB == 0 and (SAMPLE_NB * DS) == LANES
    l = 0

    xp = x_prompt.reshape(TP, D_MODEL)
    xs = x_sample.reshape(TS, D_MODEL)

    w_pad = jnp.pad(w_in[l], ((0, 0), (0, IN_PAD - w_in.shape[2]))).astype(BF16)
    wgt = jnp.transpose(w_in[l][:, GATE_COL:GATE_COL + 2 * M_HEADS]).astype(BF16)
    gi = jnp.arange(GROUP_CHUNK) // HEAD_DIM
    gmat = (gi[:, None] == gi[None, :]).astype(BF16)
    qg = jnp.tile(q_norm_g[l], A_HEADS).reshape(1, A_WIDTH)
    kg = jnp.tile(k_norm_g[l], A_KV_HEADS).reshape(1, KV_WIDTH)
    gbias = jnp.concatenate([b_i[l], b_f[l]])
    gbrow = jnp.pad(gbias, (0, LANES - 2 * M_HEADS)).reshape(1, LANES)
    gbcol = gbias.reshape(2 * M_HEADS, 1)
    mng = m_norm_g[l].reshape(1, M_WIDTH)
    cw = conv_w[l]
    cb = conv_b[l].reshape(1, 2 * M_WIDTH)

    proj_w = (g_attn[l].reshape(1, D_MODEL), w_pad, wgt, gmat, qg, kg)
    qn, kn, va, qkm, vm, om, gcol, gt = _inproj(xp, *proj_w)
    qn_s, kn_s, va_s, qkm_s, vm_s, om_s, gcol_s, gt_s = _inproj(xs, *proj_w)

    a_p = _attn_prompt(sinks[l], qn, kn, va, B, S)
    ck = cache_k_win[l].reshape(DB, WINDOW, KV_WIDTH)
    cv = cache_v_win[l].reshape(DB, WINDOW, KV_WIDTH)
    a_s, kwin_s, vwin_s = _attn_sample(sinks[l], qn_s, kn_s, va_s, ck, cv, DB, DS)

    zc = jnp.zeros((B, SUBLANES, 2 * M_WIDTH), F32)
    m_p, C_p, n_p, mm_p = _mlstm(
        qkm, vm, om, gcol, gt, zc,
        jnp.zeros((B, M_HEADS, M_HEAD_DIM, M_HEAD_DIM), F32), jnp.zeros((B, M_HEADS, M_HEAD_DIM), F32),
        jnp.full((B, 1, LANES), NEG, F32), cw, cb, gbrow, gbcol, mng,
        nseq=1, L=PROMPT_CHUNK, ngroups=B, nchunks=S // PROMPT_CHUNK)
    conv_s0 = jnp.pad(state_conv[l], ((0, 0), (SUBLANES - (CONV_W - 1), 0), (0, 0)))
    m0_s = jnp.pad(state_m[l], ((0, 0), (0, LANES - M_HEADS))).reshape(DB, 1, LANES)
    m_s, C_s, n_s, mm_s = _mlstm(
        qkm_s, vm_s, om_s, gcol_s, gt_s, conv_s0, state_C[l], state_n[l], m0_s, cw, cb, gbrow, gbcol, mng,
        nseq=SAMPLE_NB, L=DS, ngroups=DB // SAMPLE_NB, nchunks=1)

    h, hn, route, cnt = _outproj(a_p, a_s, m_p, m_s, xp, xs, w_out[l].astype(BF16), g_ffn[l].reshape(1, D_MODEL),
                                 jnp.transpose(w_router[l]).astype(BF16), b_router[l].reshape(N_EXPERTS, 1))

    nblk = T * TOP_K // MOE_BLOCK + N_EXPERTS
    lp, tab, blk, info = _route_tables(route, cnt, nblk)
    block_e = blk[0, :nblk]
    padrow = info[:, 0]
    npad = info[:, 1]
    nused = info[N_EXPERTS - 1:N_EXPERTS, 2]
    per_assign = lambda a: jnp.transpose(a[:, :TOP_K, :], (0, 2, 1)).reshape(a.shape[0], 1, TOK_TILE * TOP_K)
    lp_tiles = per_assign(lp)
    gate_tiles = per_assign(route)
    xrows = _dispatch(padrow, npad, nused, lp_tiles, tab, hn, nblk * MOE_BLOCK)
    yrows = _moe_blocks(block_e, nused, padrow, xrows,
                        w_gate_up[l], b_gate_up[l].reshape(N_EXPERTS, 1, 2 * D_FF),
                        w_down[l], b_down[l].reshape(N_EXPERTS, 1, D_MODEL))
    y_p, y_s = _combine(lp_tiles, gate_tiles, tab, h, yrows, TP)

    y_p = y_p.reshape(B, S, D_MODEL)
    y_s = y_s.reshape(DB, DS, D_MODEL)
    def seq_tail(rows, n):
        return jnp.stack([rows[(b + 1) * S - n:(b + 1) * S] for b in range(B)])

    kwin_p = seq_tail(kn, WINDOW).reshape(B, WINDOW, A_KV_HEADS, HEAD_DIM)
    vwin_p = seq_tail(va, WINDOW).reshape(B, WINDOW, A_KV_HEADS, HEAD_DIM)
    qkm_s = qkm_s.reshape(DB, DS, 2 * M_WIDTH)
    return (y_p, y_s,
            kwin_p[None], vwin_p[None], seq_tail(qkm, CONV_W - 1)[None],
            C_p[None], n_p[None], mm_p[:, 0, :M_HEADS][None],
            kwin_s.reshape(DB, WINDOW, A_KV_HEADS, HEAD_DIM)[None],
            vwin_s.reshape(DB, WINDOW, A_KV_HEADS, HEAD_DIM)[None],
            qkm_s[:, -(CONV_W - 1):][None],
            C_s[None], n_s[None], mm_s[:, 0, :M_HEADS][None])
```

```python
import functools

import jax
import jax.numpy as jnp
from jax import lax
from jax.experimental import pallas as pl
from jax.experimental.pallas import tpu as pltpu

F32 = jnp.float32
BF16 = jnp.bfloat16

D_MODEL = 1024
HEAD_DIM = 64
A_HEADS = 8
A_KV_HEADS = 2
A_GROUP = A_HEADS // A_KV_HEADS
A_WIDTH = A_HEADS * HEAD_DIM
KV_WIDTH = A_KV_HEADS * HEAD_DIM
WINDOW = 128
M_HEADS = 4
M_HEAD_DIM = 128
M_WIDTH = M_HEADS * M_HEAD_DIM
CONV_W = 4
N_EXPERTS = 32
TOP_K = 4
D_FF = D_MODEL
SWIGLU_LIMIT = 7.0
SWIGLU_ALPHA = 1.702
MOE_BLOCK = 512
MOE_QUARTER = 128
EPS = 1e-6
NEG = -1e30

LANES = 128
SUBLANES = 8
GATE_COL = A_WIDTH + 2 * KV_WIDTH + 4 * M_WIDTH
IN_PAD = GATE_COL + LANES
TOK_TILE = 512
PROJ_TILE = 1024
PROJ_VMEM_LIMIT = 58 * 1024 * 1024
GROUP_CHUNK = 256
ATT_QB = 1024
ATT_SB = 128
SAMPLE_NB = 16
PROMPT_CHUNK = 256
VMEM_LIMIT = 48 * 1024 * 1024
MOE_VMEM_LIMIT = 56 * 1024 * 1024


def _dot(a, b):
    return jnp.dot(a, b, preferred_element_type=F32)


def _dot_nt(a, b):
    return lax.dot_general(a, b, (((1,), (1,)), ((), ())), preferred_element_type=F32)


def _dot_tn(a, b):
    return lax.dot_general(a, b, (((0,), (0,)), ((), ())), preferred_element_type=F32)


def _split3(x):
    hi = x.astype(BF16)
    r1 = x - hi.astype(F32)
    mid = r1.astype(BF16)
    lo = (r1 - mid.astype(F32)).astype(BF16)
    return hi, mid, lo


def _log_sigmoid(x):
    return jnp.minimum(x, 0.0) - jnp.log1p(jnp.exp(-jnp.abs(x)))


def _sigmoid(x):
    return 0.5 * jnp.tanh(0.5 * x) + 0.5


def _load_row_tiles(ref2, rows):
    return jnp.concatenate([ref2[pl.ds(s, rows, stride=SUBLANES), :] for s in range(SUBLANES)], axis=1)


def _store_row_tiles(ref2, rows, val):
    for s in range(SUBLANES):
        ref2[pl.ds(s, rows, stride=SUBLANES), :] = val[:, s * LANES:(s + 1) * LANES]


def _row_tiles(ref2, first, n):
    start = first * SUBLANES
    if not isinstance(start, int):
        start = pl.multiple_of(start, SUBLANES)
    return ref2.at[pl.ds(start, n * SUBLANES), :]


def _split_specs(n_prompt_tiles, rows, width):
    return (pl.BlockSpec((rows, width), lambda i, *_: (jnp.minimum(i, n_prompt_tiles - 1), 0)),
            pl.BlockSpec((rows, width), lambda i, *_: (jnp.maximum(i - n_prompt_tiles, 0), 0)))


def _inproj_kernel(x_ref, g_ref, w_ref, wgt_ref, gmat_ref, qg_ref, kg_ref,
                   qn_ref, kn_ref, va_ref, qkm_ref, vm_ref, om_ref, gcol_ref, gt_ref):
    x = x_ref[...]
    ms = jnp.mean(x * x, axis=-1, keepdims=True)
    xn = ((x * lax.rsqrt(ms + EPS)) * g_ref[...]).astype(BF16)

    def seg(lo, hi):
        return _dot(xn, w_ref[:, lo:hi])

    def head_norm(z, gmat, g):
        parts = _split3(z * z)
        w = min(GROUP_CHUNK, z.shape[1])
        ss = jnp.concatenate([sum(_dot(p[:, c:c + w], gmat) for p in parts) for c in range(0, z.shape[1], w)], axis=1)
        return (z * lax.rsqrt(ss * (1.0 / HEAD_DIM) + EPS)) * g

    o0 = A_WIDTH
    o1 = o0 + KV_WIDTH
    o2 = o1 + KV_WIDTH
    o3 = o2 + 2 * M_WIDTH
    o4 = o3 + M_WIDTH
    o5 = o4 + M_WIDTH
    qn_ref[...] = head_norm(seg(0, o0), gmat_ref[...], qg_ref[...])
    kn_ref[...] = head_norm(seg(o0, o1), gmat_ref[:KV_WIDTH, :KV_WIDTH], kg_ref[...])
    va_ref[...] = seg(o1, o2)
    qkm_ref[...] = seg(o2, o3)
    vm_ref[...] = seg(o3, o4)
    om_ref[...] = seg(o4, o5)
    gcol_ref[...] = seg(o5, o5 + LANES)
    gt_ref[0] = _dot_nt(wgt_ref[...], xn)


def _inproj(x, g_attn, w_pad, wgt, gmat, qg, kg):
    T = x.shape[0]
    nt = T // PROJ_TILE
    row = lambda w: pl.BlockSpec((PROJ_TILE, w), lambda i: (i, 0))
    full = lambda a: pl.BlockSpec(a.shape, lambda i: (0,) * a.ndim, pipeline_mode=pl.Buffered(1))
    out_shape = (
        jax.ShapeDtypeStruct((T, A_WIDTH), F32),
        jax.ShapeDtypeStruct((T, KV_WIDTH), F32),
        jax.ShapeDtypeStruct((T, KV_WIDTH), F32),
        jax.ShapeDtypeStruct((T, 2 * M_WIDTH), F32),
        jax.ShapeDtypeStruct((T, M_WIDTH), F32),
        jax.ShapeDtypeStruct((T, M_WIDTH), F32),
        jax.ShapeDtypeStruct((T, LANES), F32),
        jax.ShapeDtypeStruct((nt, SUBLANES, PROJ_TILE), F32),
    )
    out_specs = (row(A_WIDTH), row(KV_WIDTH), row(KV_WIDTH), row(2 * M_WIDTH), row(M_WIDTH), row(M_WIDTH),
                 row(LANES), pl.BlockSpec((1, SUBLANES, PROJ_TILE), lambda i: (i, 0, 0)))
    return pl.pallas_call(
        _inproj_kernel,
        grid=(nt,),
        in_specs=[row(D_MODEL), full(g_attn), full(w_pad), full(wgt), full(gmat), full(qg), full(kg)],
        out_specs=out_specs,
        out_shape=out_shape,
        compiler_params=pltpu.CompilerParams(dimension_semantics=("parallel",), vmem_limit_bytes=PROJ_VMEM_LIMIT),
        name="inproj",
    )(x, g_attn, w_pad, wgt, gmat, qg, kg)


def _softmax_sink(pieces, masks, sink_col):
    masked = [jnp.where(mk, s, NEG) for s, mk in zip(pieces, masks)]
    m = sink_col
    for s in masked:
        m = jnp.maximum(m, jnp.max(s, axis=-1, keepdims=True))
    ps = [jnp.exp(s - m) for s in masked]
    den = jnp.exp(sink_col - m)
    for p in ps:
        den = den + jnp.sum(p, axis=-1, keepdims=True)
    return ps, 1.0 / den


def _stack_heads(q, g):
    return jnp.concatenate([q[:, (A_GROUP * g + i) * HEAD_DIM:(A_GROUP * g + i + 1) * HEAD_DIM]
                            for i in range(A_GROUP)], axis=0)


def _sink_col(sink_ref, g, rows_per_head):
    r = lax.broadcasted_iota(jnp.int32, (A_GROUP * rows_per_head, 1), 0)
    col = jnp.zeros((A_GROUP * rows_per_head, 1), F32)
    for i in range(A_GROUP):
        col = jnp.where(r // rows_per_head == i, sink_ref[A_GROUP * g + i], col)
    return col


def _attn_prompt_kernel(sink_ref, q_ref, kp_ref, kc_ref, vp_ref, vc_ref, o_ref):
    j = pl.program_id(1)
    scale = HEAD_DIM ** -0.5
    kall = jnp.concatenate([kp_ref[...], kc_ref[...]], axis=0).astype(BF16)
    vall = jnp.concatenate([vp_ref[...], vc_ref[...]], axis=0).astype(BF16)
    nrow = A_GROUP * ATT_SB
    r = lax.broadcasted_iota(jnp.int32, (nrow, 2 * ATT_SB), 0) % ATT_SB
    c = lax.broadcasted_iota(jnp.int32, (nrow, 2 * ATT_SB), 1)
    band = jnp.logical_and(c >= r, c <= r + WINDOW)
    band0 = jnp.logical_and(band, jnp.logical_or(c >= ATT_SB, j > 0))
    for sb in range(ATT_QB // ATT_SB):
        q = (q_ref[sb * ATT_SB:(sb + 1) * ATT_SB, :] * scale).astype(BF16)
        kwin = kall[sb * ATT_SB:(sb + 2) * ATT_SB]
        vwin = vall[sb * ATT_SB:(sb + 2) * ATT_SB]
        outs = []
        for g in range(A_KV_HEADS):
            lo, hi = g * HEAD_DIM, (g + 1) * HEAD_DIM
            s = _dot_nt(_stack_heads(q, g), kwin[:, lo:hi])
            (p,), inv = _softmax_sink([s], [band0 if sb == 0 else band], _sink_col(sink_ref, g, ATT_SB))
            o = _dot(p.astype(BF16), vwin[:, lo:hi]) * inv
            outs += [o[i * ATT_SB:(i + 1) * ATT_SB] for i in range(A_GROUP)]
        o_ref[sb * ATT_SB:(sb + 1) * ATT_SB, :] = jnp.concatenate(outs, axis=1)


def _attn_prompt(sinks, qn, kn, va, batch, seq):
    nq = seq // ATT_QB
    ratio = ATT_QB // ATT_SB
    cur = lambda w: pl.BlockSpec((ATT_QB, w), lambda b, j: (b * nq + j, 0))
    prev = lambda w: pl.BlockSpec((ATT_SB, w), lambda b, j: (jnp.maximum((b * nq + j) * ratio - 1, 0), 0))
    return pl.pallas_call(
        _attn_prompt_kernel,
        grid=(batch, nq),
        in_specs=[pl.BlockSpec(memory_space=pltpu.SMEM), cur(A_WIDTH), prev(KV_WIDTH), cur(KV_WIDTH),
                  prev(KV_WIDTH), cur(KV_WIDTH)],
        out_specs=cur(A_WIDTH),
        out_shape=jax.ShapeDtypeStruct((batch * seq, A_WIDTH), F32),
        compiler_params=pltpu.CompilerParams(dimension_semantics=("parallel", "parallel"),
                                             vmem_limit_bytes=VMEM_LIMIT),
        name="attn_prompt",
    )(sinks, qn, kn, kn, va, va)


def _attn_sample_kernel(dec, sink_ref, q_ref, kn_ref, vn_ref, ck_ref, cv_ref, o_ref, kw_ref, vw_ref):
    scale = HEAD_DIM ** -0.5
    rows = SAMPLE_NB * dec
    knew = kn_ref[...]
    vnew = vn_ref[...]
    knew_b = knew.astype(BF16)
    vnew_b = vnew.astype(BF16)
    nrow = A_GROUP * dec
    t = lax.broadcasted_iota(jnp.int32, (nrow, WINDOW), 0) % dec
    c = lax.broadcasted_iota(jnp.int32, (nrow, WINDOW), 1)
    m_cache = c >= t
    cn = lax.broadcasted_iota(jnp.int32, (nrow, rows), 1)
    tn = lax.broadcasted_iota(jnp.int32, (nrow, rows), 0) % dec
    for i in range(SAMPLE_NB):
        q = (q_ref[i * dec:(i + 1) * dec, :] * scale).astype(BF16)
        ck = ck_ref[i].astype(BF16)
        cv = cv_ref[i].astype(BF16)
        m_new = jnp.logical_and(cn // dec == i, cn % dec <= tn)
        outs = []
        for g in range(A_KV_HEADS):
            lo, hi = g * HEAD_DIM, (g + 1) * HEAD_DIM
            qs = _stack_heads(q, g)
            s_c = _dot_nt(qs, ck[:, lo:hi])
            s_n = _dot_nt(qs, knew_b[:, lo:hi])
            (p_c, p_n), inv = _softmax_sink([s_c, s_n], [m_cache, m_new], _sink_col(sink_ref, g, dec))
            o = (_dot(p_c.astype(BF16), cv[:, lo:hi]) + _dot(p_n.astype(BF16), vnew_b[:, lo:hi])) * inv
            outs += [o[h * dec:(h + 1) * dec] for h in range(A_GROUP)]
        o_ref[i * dec:(i + 1) * dec, :] = jnp.concatenate(outs, axis=1)
        kw_ref[i, 0:WINDOW - dec, :] = ck_ref[i, dec:WINDOW, :]
        kw_ref[i, WINDOW - dec:WINDOW, :] = knew[i * dec:(i + 1) * dec]
        vw_ref[i, 0:WINDOW - dec, :] = cv_ref[i, dec:WINDOW, :]
        vw_ref[i, WINDOW - dec:WINDOW, :] = vnew[i * dec:(i + 1) * dec]


def _attn_sample(sinks, qn, kn, va, ck, cv, dbatch, dec):
    rows = SAMPLE_NB * dec
    tokrow = lambda w: pl.BlockSpec((rows, w), lambda i: (i, 0))
    cache = pl.BlockSpec((SAMPLE_NB, WINDOW, KV_WIDTH), lambda i: (i, 0, 0))
    return pl.pallas_call(
        functools.partial(_attn_sample_kernel, dec),
        grid=(dbatch // SAMPLE_NB,),
        in_specs=[pl.BlockSpec(memory_space=pltpu.SMEM), tokrow(A_WIDTH), tokrow(KV_WIDTH), tokrow(KV_WIDTH),
                  cache, cache],
        out_specs=(pl.BlockSpec((rows, A_WIDTH), lambda i: (i, 0)), cache, cache),
        out_shape=(jax.ShapeDtypeStruct((dbatch * dec, A_WIDTH), F32),
                   jax.ShapeDtypeStruct((dbatch, WINDOW, KV_WIDTH), F32),
                   jax.ShapeDtypeStruct((dbatch, WINDOW, KV_WIDTH), F32)),
        compiler_params=pltpu.CompilerParams(dimension_semantics=("parallel",), vmem_limit_bytes=VMEM_LIMIT),
        name="attn_sample",
    )(sinks, qn, kn, va, ck, cv)


def _mlstm_kernel(nseq, L, qk_ref, v_ref, o_ref, gcol_ref, gt_ref, conv0_ref, c0_ref, n0_ref, m0_ref,
                  cw_ref, cb_ref, gbrow_ref, gbcol_ref, mng_ref, mask_ref,
                  out_ref, cst_ref, nst_ref, mst_ref, prev_ref):
    R = nseq * L
    ci = pl.program_id(1)

    @pl.when(ci == 0)
    def _():
        cst_ref[...] = c0_ref[...]
        nst_ref[...] = n0_ref[...]
        mst_ref[...] = m0_ref[...]
        prev_ref[...] = conv0_ref[0]

    raw = qk_ref[...]
    row = lax.broadcasted_iota(jnp.int32, (R, 1), 0)
    tpos = row % L
    rseq = row // L
    acc = raw * cw_ref[CONV_W - 1:CONV_W, :] + cb_ref[...]
    if nseq == 1:
        prev8 = prev_ref[...]
        t8 = lax.broadcasted_iota(jnp.int32, (SUBLANES, 1), 0)
    else:
        prevsrc = conv0_ref[...].reshape(R, 2 * M_WIDTH)
    for k in range(1, CONV_W):
        rolled = pltpu.roll(raw, k, 0)
        if nseq == 1:
            head = jnp.where(t8 >= k, rolled[0:SUBLANES], pltpu.roll(prev8, k, 0))
            sh = jnp.concatenate([head, rolled[SUBLANES:]], axis=0)
        else:
            sh = jnp.where(tpos >= k, rolled, pltpu.roll(prevsrc, R - SUBLANES + k, 0))
        acc = acc + sh * cw_ref[CONV_W - 1 - k:CONV_W - k, :]
    if nseq == 1:
        prev_ref[...] = raw[R - SUBLANES:R]
    qkc = acc * _sigmoid(acc)

    gc = gcol_ref[...] + gbrow_ref[...]
    gr = gt_ref[0] + gbcol_ref[...]
    lsc = _log_sigmoid(gc)
    lsr = _log_sigmoid(gr)
    mb = mask_ref[...]
    maskb = mb > 0
    bcol = sum(_dot(mb, p) for p in _split3(lsc))
    brow = sum(_dot_nt(p, mb) for p in _split3(lsr))

    lane = lax.broadcasted_iota(jnp.int32, (1, LANES), 1)
    m_new = [jnp.zeros((1, LANES), F32) for _ in range(nseq)]
    for h in range(M_HEADS):
        sl = slice(h * M_HEAD_DIM, (h + 1) * M_HEAD_DIM)
        qh = qkc[:, sl]
        kh = qkc[:, M_WIDTH + h * M_HEAD_DIM:M_WIDTH + (h + 1) * M_HEAD_DIM] * (M_HEAD_DIM ** -0.5)
        vh = v_ref[:, sl]
        qb, kb, vb = qh.astype(BF16), kh.astype(BF16), vh.astype(BF16)
        ig_c = gc[:, h:h + 1]
        b_c = bcol[:, M_HEADS + h:M_HEADS + h + 1]
        ig_r = gr[h:h + 1, :]
        b_r = brow[M_HEADS + h:M_HEADS + h + 1, :]
        if nseq == 1:
            m0c = mst_ref[0][:, h:h + 1]
            n0rows = nst_ref[0, h:h + 1, :]
        else:
            m0c = jnp.zeros((R, 1), F32)
            n0rows = jnp.zeros((R, M_HEAD_DIM), F32)
            for s in range(nseq):
                m0c = jnp.where(rseq == s, mst_ref[s][:, h:h + 1], m0c)
                n0rows = jnp.where(rseq == s, nst_ref[s, h:h + 1, :], n0rows)
        dm = jnp.where(maskb, b_c - b_r + ig_r, NEG)
        a_c = b_c + m0c
        m_c = jnp.maximum(a_c, jnp.max(dm, axis=-1, keepdims=True))
        w = jnp.exp(dm - m_c)
        sc = jnp.exp(a_c - m_c)
        wqk = w * _dot_nt(qb, kb)
        if nseq == 1:
            inter = _dot_nt(qb, cst_ref[0, h].astype(BF16))
        else:
            inter = jnp.zeros((R, M_HEAD_DIM), F32)
            for s in range(nseq):
                qs = jnp.where(rseq == s, qh, 0.0).astype(BF16)
                inter = inter + _dot_nt(qs, cst_ref[s, h].astype(BF16))
        num = _dot(wqk.astype(BF16), vb) + sc * inter
        den = jnp.sum(wqk, axis=-1, keepdims=True) + sc * jnp.sum(qh * n0rows, axis=-1, keepdims=True)
        hh = num / jnp.maximum(jnp.abs(den), jnp.exp(-m_c))

        for s in range(nseq):
            e = s * L + L - 1
            m_end = m_c[e:e + 1, :]
            wend = jnp.exp(b_c[e:e + 1, :] - b_c + ig_c - m_end)
            if nseq > 1:
                wend = jnp.where(rseq == s, wend, 0.0)
            sce = jnp.exp(a_c[e:e + 1, :] - m_end)
            c_new = sce * cst_ref[s, h] + _dot_tn((vh * wend).astype(BF16), kb)
            n_new = sce * nst_ref[s, h:h + 1, :] + jnp.sum(wend * kh, axis=0, keepdims=True)
            cst_ref[s, h] = c_new
            nst_ref[s, h:h + 1, :] = n_new
            m_new[s] = jnp.where(lane == h, m_end, m_new[s])

        hn = (hh * lax.rsqrt(jnp.mean(hh * hh, axis=-1, keepdims=True) + EPS)) * mng_ref[:, sl]
        out_ref[:, sl] = _sigmoid(o_ref[:, sl]) * hn
    for s in range(nseq):
        mst_ref[s] = m_new[s]


def _mlstm(qkm, vm, om, gcol, gt, conv0, c0, n0, m0, cw, cb, gbrow, gbcol, mng, nseq, L, ngroups, nchunks):
    R = nseq * L
    per_tile = gt.shape[2] // R
    tok = lambda w: pl.BlockSpec((R, w), lambda g, c: (g * nchunks + c, 0))
    gt_spec = pl.BlockSpec((1, SUBLANES, R),
                           lambda g, c: ((g * nchunks + c) // per_tile, 0, (g * nchunks + c) % per_tile))
    full = lambda a: pl.BlockSpec(a.shape, lambda g, c: (0,) * a.ndim)
    st4 = pl.BlockSpec((nseq, M_HEADS, M_HEAD_DIM, M_HEAD_DIM), lambda g, c: (g, 0, 0, 0))
    st3 = pl.BlockSpec((nseq, M_HEADS, M_HEAD_DIM), lambda g, c: (g, 0, 0))
    stm = pl.BlockSpec((nseq, 1, LANES), lambda g, c: (g, 0, 0))
    conv_spec = pl.BlockSpec((nseq, SUBLANES, 2 * M_WIDTH), lambda g, c: (g, 0, 0))
    r = jnp.arange(R)
    mask = ((r[:, None] // L == r[None, :] // L) & (r[None, :] <= r[:, None])).astype(BF16)
    nstate = ngroups * nseq
    return pl.pallas_call(
        functools.partial(_mlstm_kernel, nseq, L),
        grid=(ngroups, nchunks),
        in_specs=[tok(2 * M_WIDTH), tok(M_WIDTH), tok(M_WIDTH), tok(LANES), gt_spec, conv_spec, st4, st3, stm,
                  full(cw), full(cb), full(gbrow), full(gbcol), full(mng), full(mask)],
        out_specs=(pl.BlockSpec((R, M_WIDTH), lambda g, c: (g * nchunks + c, 0)), st4, st3, stm),
        out_shape=(jax.ShapeDtypeStruct((ngroups * nchunks * R, M_WIDTH), F32),
                   jax.ShapeDtypeStruct((nstate, M_HEADS, M_HEAD_DIM, M_HEAD_DIM), F32),
                   jax.ShapeDtypeStruct((nstate, M_HEADS, M_HEAD_DIM), F32),
                   jax.ShapeDtypeStruct((nstate, 1, LANES), F32)),
        scratch_shapes=[pltpu.VMEM((SUBLANES, 2 * M_WIDTH), F32)],
        compiler_params=pltpu.CompilerParams(dimension_semantics=("parallel", "arbitrary"),
                                             vmem_limit_bytes=VMEM_LIMIT),
        name="mlstm_n%d" % nseq,
    )(qkm, vm, om, gcol, gt, conv0, c0, n0, m0, cw, cb, gbrow, gbcol, mng, mask)


def _outproj_kernel(npt, ap_ref, as_ref, mp_ref, ms_ref, xp_ref, xs_ref, wo_ref, g_ref, wrt_ref, brc_ref,
                    h_ref, hn_ref, route_ref, cnt_ref):
    def project(a_ref, m_ref, x_ref):
        h_ref[...] = (x_ref[...] + _dot(a_ref[...].astype(BF16), wo_ref[0:A_WIDTH, :])
                      + _dot(m_ref[...].astype(BF16), wo_ref[A_WIDTH:A_WIDTH + M_WIDTH, :]))

    @pl.when(pl.program_id(0) < npt)
    def _():
        project(ap_ref, mp_ref, xp_ref)

    @pl.when(pl.program_id(0) >= npt)
    def _():
        project(as_ref, ms_ref, xs_ref)

    h = h_ref[...]
    hn = (h * lax.rsqrt(jnp.mean(h * h, axis=-1, keepdims=True) + EPS)) * g_ref[...]
    _store_row_tiles(hn_ref, TOK_TILE, hn)
    logits = _dot_nt(wrt_ref[...], hn.astype(BF16)) + brc_ref[...]
    eidx = lax.broadcasted_iota(jnp.int32, logits.shape, 0).astype(F32)
    picked = jnp.zeros(logits.shape, F32)
    top0 = None
    den = None
    es = []
    ids = []
    for k in range(TOP_K):
        mx = jnp.max(logits, axis=0, keepdims=True)
        idx = jnp.min(jnp.where(logits == mx, eidx, float(N_EXPERTS)), axis=0, keepdims=True)
        if k == 0:
            top0 = mx
        e = jnp.exp(mx - top0)
        den = e if den is None else den + e
        es.append(e)
        ids.append(idx)
        hit = eidx == idx
        picked = jnp.where(hit, 1.0, picked)
        logits = jnp.where(hit, -jnp.inf, logits)
    route_ref[0] = jnp.concatenate([e / den for e in es] + ids, axis=0)

    @pl.when(pl.program_id(0) == 0)
    def _():
        cnt_ref[...] = jnp.zeros_like(cnt_ref)

    cnt_ref[...] += jnp.broadcast_to(jnp.sum(picked, axis=1, keepdims=True), cnt_ref.shape)


def _outproj(a_p, a_s, m_p, m_s, xp, xs, w_out, g_ffn, wr_t, br_col):
    T = xp.shape[0] + xs.shape[0]
    nt = T // TOK_TILE
    npt = xp.shape[0] // TOK_TILE
    row = lambda w: pl.BlockSpec((TOK_TILE, w), lambda i: (i, 0))
    full = lambda a: pl.BlockSpec(a.shape, lambda i: (0,) * a.ndim)
    return pl.pallas_call(
        functools.partial(_outproj_kernel, npt),
        grid=(nt,),
        in_specs=[*_split_specs(npt, TOK_TILE, A_WIDTH), *_split_specs(npt, TOK_TILE, M_WIDTH),
                  *_split_specs(npt, TOK_TILE, D_MODEL), full(w_out), full(g_ffn), full(wr_t), full(br_col)],
        out_specs=(row(D_MODEL), pl.BlockSpec((TOK_TILE * SUBLANES, LANES), lambda i: (i, 0)),
                   pl.BlockSpec((1, 2 * TOP_K, TOK_TILE), lambda i: (i, 0, 0)),
                   pl.BlockSpec((N_EXPERTS, LANES), lambda i: (0, 0))),
        out_shape=(jax.ShapeDtypeStruct((T, D_MODEL), F32), jax.ShapeDtypeStruct((T * SUBLANES, LANES), F32),
                   jax.ShapeDtypeStruct((nt, 2 * TOP_K, TOK_TILE), F32),
                   jax.ShapeDtypeStruct((N_EXPERTS, LANES), F32)),
        compiler_params=pltpu.CompilerParams(dimension_semantics=("arbitrary",), vmem_limit_bytes=VMEM_LIMIT),
        name="outproj_router",
    )(a_p, a_s, m_p, m_s, xp, xs, w_out, g_ffn, wr_t, br_col)


def _route_kernel(nblk_pad, route_ref, cnt_ref, ustrict_ref, lstrict_ref, lp_ref, tab_ref, blk_ref, info_ref,
                  carry_ref):
    i = pl.program_id(0)
    cnt = cnt_ref[...]
    nb_e = jnp.floor((cnt + (MOE_BLOCK - 1.0)) * (1.0 / MOE_BLOCK))
    bstart = sum(_dot(lstrict_ref[...], p) for p in _split3(nb_e))
    bend = bstart + nb_e
    row_start = bstart * float(MOE_BLOCK)

    @pl.when(i == 0)
    def _():
        carry_ref[...] = jnp.zeros_like(carry_ref)
        bi = lax.broadcasted_iota(jnp.int32, (N_EXPERTS, nblk_pad), 1).astype(F32)
        done = jnp.where(bend[:, 0:1] <= bi, 1.0, 0.0)
        be = jnp.minimum(jnp.sum(done, axis=0, keepdims=True), N_EXPERTS - 1.0)
        blk_ref[...] = jnp.broadcast_to(be, blk_ref.shape).astype(jnp.int32)
        lane = lax.broadcasted_iota(jnp.int32, (N_EXPERTS, LANES), 1)
        info = jnp.where(lane == 0, row_start + cnt, 0.0)
        info = jnp.where(lane == 1, nb_e * float(MOE_BLOCK) - cnt, info)
        info = jnp.where(lane == 2, bend, info)
        info_ref[...] = info.astype(jnp.int32)

    r = route_ref[0]
    eidx = lax.broadcasted_iota(jnp.int32, (N_EXPERTS, TOK_TILE), 0).astype(F32)
    sel = [eidx == r[TOP_K + k:TOP_K + k + 1, :] for k in range(TOP_K)]
    oh = jnp.zeros((N_EXPERTS, TOK_TILE), F32)
    for k in range(TOP_K):
        oh = jnp.where(sel[k], 1.0, oh)
    cnt_t = jnp.broadcast_to(jnp.sum(oh, axis=1, keepdims=True), (N_EXPERTS, LANES))
    seg = sum(_dot(lstrict_ref[...], p) for p in _split3(cnt_t))
    local = _dot(oh.astype(BF16), ustrict_ref[...]) + seg[:, 0:1]
    rows = [jnp.sum(jnp.where(sel[k], local, 0.0), axis=0, keepdims=True) for k in range(TOP_K)]
    lp_ref[0] = (jnp.concatenate(rows + [jnp.zeros((TOP_K, TOK_TILE), F32)], axis=0)
                 * float(SUBLANES)).astype(jnp.int32)
    diag = (lax.broadcasted_iota(jnp.int32, (N_EXPERTS, LANES), 0)
            == lax.broadcasted_iota(jnp.int32, (N_EXPERTS, LANES), 1))
    to_lanes = lambda col: jnp.sum(jnp.where(diag, col, 0.0), axis=0, keepdims=True)
    tab = [to_lanes(seg), to_lanes(cnt_t), to_lanes(row_start + carry_ref[...])]
    tab_ref[0] = jnp.concatenate(tab + [jnp.zeros((SUBLANES - len(tab), LANES), F32)], axis=0).astype(jnp.int32)
    carry_ref[...] += cnt_t


def _route_tables(route, cnt, nblk):
    nt = route.shape[0]
    nblk_pad = -(-nblk // LANES) * LANES
    a = jnp.arange(TOK_TILE)
    ustrict = (a[:, None] < a[None, :]).astype(BF16)
    b = jnp.arange(N_EXPERTS)
    lstrict = (b[:, None] > b[None, :]).astype(BF16)
    full = lambda x: pl.BlockSpec(x.shape, lambda i: (0,) * x.ndim)
    tile = pl.BlockSpec((1, 2 * TOP_K, TOK_TILE), lambda i: (i, 0, 0))
    return pl.pallas_call(
        functools.partial(_route_kernel, nblk_pad),
        grid=(nt,),
        in_specs=[tile, full(cnt), full(ustrict), full(lstrict)],
        out_specs=(tile, pl.BlockSpec((1, SUBLANES, LANES), lambda i: (i, 0, 0)),
                   pl.BlockSpec((SUBLANES, nblk_pad), lambda i: (0, 0)),
                   pl.BlockSpec((N_EXPERTS, LANES), lambda i: (0, 0))),
        out_shape=(jax.ShapeDtypeStruct((nt, 2 * TOP_K, TOK_TILE), jnp.int32),
                   jax.ShapeDtypeStruct((nt, SUBLANES, LANES), jnp.int32),
                   jax.ShapeDtypeStruct((SUBLANES, nblk_pad), jnp.int32),
                   jax.ShapeDtypeStruct((N_EXPERTS, LANES), jnp.int32)),
        scratch_shapes=[pltpu.VMEM((N_EXPERTS, LANES), F32)],
        compiler_params=pltpu.CompilerParams(dimension_semantics=("arbitrary",), vmem_limit_bytes=VMEM_LIMIT),
        name="route_tables",
    )(route, cnt, ustrict, lstrict)


RUN_PIECES = tuple(1 << b for b in range(9, -1, -1))


def _for_run_pieces(count, fn):
    for size in RUN_PIECES:
        @pl.when((count & size) != 0)
        def _(size=size):
            fn(count & ~(2 * size - 1), size)


def _dispatch_kernel(padrow_ref, npad_ref, nu_ref, lp_ref, tab_ref, hn_ref, xs_hbm, srt, zbuf, sem, zsem):
    i = pl.program_id(0)
    nt = pl.num_programs(0)
    slot = i % 2

    @pl.when(i == 0)
    def _():
        zbuf[...] = jnp.zeros_like(zbuf)
        nblk = xs_hbm.shape[0] // (MOE_BLOCK * SUBLANES)

        def tail_start(b, c):
            pltpu.make_async_copy(zbuf, _row_tiles(xs_hbm, b * MOE_BLOCK, MOE_BLOCK), zsem).start()
            return c

        def tail_wait(b, c):
            pltpu.make_async_copy(zbuf, _row_tiles(xs_hbm, b * MOE_BLOCK, MOE_BLOCK), zsem).wait()
            return c

        lax.fori_loop(nu_ref[0], nblk, tail_start, 0)
        lax.fori_loop(nu_ref[0], nblk, tail_wait, 0)

        def pad_runs(wait):
            def per_expert(e, carry):
                first, count = padrow_ref[e], npad_ref[e]

                def piece(off, size):
                    cp = pltpu.make_async_copy(_row_tiles(zbuf, 0, size), _row_tiles(xs_hbm, first + off, size), zsem)
                    cp.wait() if wait else cp.start()
                _for_run_pieces(count, piece)
                return carry
            lax.fori_loop(0, N_EXPERTS, per_expert, 0)

        pad_runs(False)
        pad_runs(True)

    def wait_runs(s):
        for _ in range(TOP_K):
            pltpu.make_async_copy(_row_tiles(srt.at[s], 0, TOK_TILE), _row_tiles(xs_hbm, 0, TOK_TILE), sem.at[s]).wait()

    def step(s):
        @pl.when(i >= 2)
        def _():
            wait_runs(s)

        def permute(j, carry):
            for u in range(SUBLANES):
                t = j * SUBLANES + u
                row = hn_ref[pl.ds(pl.multiple_of(t * SUBLANES, SUBLANES), SUBLANES), :]
                for k in range(TOP_K):
                    p = lp_ref[0, 0, t * TOP_K + k]
                    srt[s, pl.ds(pl.multiple_of(p, SUBLANES), SUBLANES), :] = row
            return carry

        lax.fori_loop(0, TOK_TILE // SUBLANES, permute, 0)

        def send_run(e, carry):
            local, count, first = tab_ref[0, 0, e], tab_ref[0, 1, e], tab_ref[0, 2, e]

            def piece(off, size):
                pltpu.make_async_copy(_row_tiles(srt.at[s], local + off, size),
                                      _row_tiles(xs_hbm, first + off, size), sem.at[s]).start()
            _for_run_pieces(count, piece)
            return carry

        lax.fori_loop(0, N_EXPERTS, send_run, 0)

        @pl.when(i == nt - 1)
        def _():
            @pl.when(nt >= 2)
            def _():
                wait_runs(1 - s)

            wait_runs(s)

    for s in range(2):
        pl.when(slot == s)(functools.partial(step, s))


def _dispatch(padrow, npad, nused, lp_tiles, tab, hn, n_rows):
    nt = lp_tiles.shape[0]
    grid_spec = pltpu.PrefetchScalarGridSpec(
        num_scalar_prefetch=3,
        grid=(nt,),
        in_specs=[pl.BlockSpec((1, 1, TOK_TILE * TOP_K), lambda i, *_: (i, 0, 0), memory_space=pltpu.SMEM),
                  pl.BlockSpec((1, SUBLANES, LANES), lambda i, *_: (i, 0, 0), memory_space=pltpu.SMEM),
                  pl.BlockSpec((TOK_TILE * SUBLANES, LANES), lambda i, *_: (i, 0))],
        out_specs=pl.BlockSpec(memory_space=pl.ANY),
        scratch_shapes=[pltpu.VMEM((2, TOK_TILE * TOP_K * SUBLANES, LANES), F32),
                        pltpu.VMEM((MOE_BLOCK * SUBLANES, LANES), F32),
                        pltpu.SemaphoreType.DMA((2,)), pltpu.SemaphoreType.DMA(())],
    )
    return pl.pallas_call(
        _dispatch_kernel,
        grid_spec=grid_spec,
        out_shape=jax.ShapeDtypeStruct((n_rows * SUBLANES, LANES), F32),
        compiler_params=pltpu.CompilerParams(dimension_semantics=("arbitrary",), vmem_limit_bytes=VMEM_LIMIT),
        name="moe_dispatch",
    )(padrow, npad, nused, lp_tiles, tab, hn)


def _moe_kernel(be_ref, nu_ref, first_ref, slot_ref, nxt_ref, nvalid_ref, x_ref, bgu_ref, bd_ref, wgu_hbm, wd_hbm, y_ref,
                wgu_buf, wd_buf, wsem):
    i = pl.program_id(0)
    used = i < nu_ref[0]
    s = slot_ref[i]

    def fetch(e, sl):
        return (pltpu.make_async_copy(wgu_hbm.at[e], wgu_buf.at[sl], wsem.at[0, sl]),
                pltpu.make_async_copy(wd_hbm.at[e], wd_buf.at[sl], wsem.at[1, sl]))

    @pl.when(jnp.logical_and(used, first_ref[i] == 1))
    def _():
        @pl.when(i == 0)
        def _():
            for c in fetch(be_ref[0], 0):
                c.start()

        for c in fetch(be_ref[i], s):
            c.wait()

        @pl.when(nxt_ref[i] >= 0)
        def _():
            for c in fetch(nxt_ref[i], 1 - s):
                c.start()

    def expert(rows):
        x = _load_row_tiles(x_ref, rows).astype(BF16)
        hb = _dot(x, wgu_buf[s].astype(BF16)) + bgu_ref[0]
        glu = jnp.minimum(hb[:, :D_FF], SWIGLU_LIMIT)
        lin = jnp.clip(hb[:, D_FF:], -SWIGLU_LIMIT, SWIGLU_LIMIT)
        act = glu * _sigmoid(SWIGLU_ALPHA * glu) * (lin + 1.0)
        _store_row_tiles(y_ref, rows, _dot(act.astype(BF16), wd_buf[s].astype(BF16)) + bd_ref[0])
        if rows < MOE_BLOCK:
            y_ref[rows * SUBLANES:, :] = jnp.zeros(((MOE_BLOCK - rows) * SUBLANES, LANES), F32)

    nv = nvalid_ref[i]
    for rows in range(MOE_QUARTER, MOE_BLOCK + 1, MOE_QUARTER):
        lo = rows - MOE_QUARTER if rows > MOE_QUARTER else -1
        pl.when(jnp.logical_and(used, jnp.logical_and(nv > lo, nv <= rows)))(functools.partial(expert, rows))

    @pl.when(i >= nu_ref[0])
    def _():
        y_ref[...] = jnp.zeros_like(y_ref)


def _moe_blocks(block_e, nused, padrow, xs, wgu, bgu, wd, bd):
    nblk = block_e.shape[0]
    pad_at = jnp.sum(jnp.where(block_e[:, None] == jnp.arange(N_EXPERTS)[None, :], padrow[None, :], 0), axis=1)
    nvalid = jnp.clip(pad_at - jnp.arange(nblk, dtype=jnp.int32) * MOE_BLOCK, 0, MOE_BLOCK).astype(jnp.int32)
    idx = jnp.arange(nblk, dtype=jnp.int32)
    first = (idx < nused[0]) & ((idx == 0) | (block_e != jnp.roll(block_e, 1)))
    slot = ((jnp.cumsum(first.astype(jnp.int32)) - 1) % 2).astype(jnp.int32)
    first_pos = jnp.where(first, idx, nblk)
    later = jnp.concatenate([first_pos[1:], jnp.full((1,), nblk, jnp.int32)])
    next_pos = lax.cummin(later, reverse=True)
    nxt = jnp.sum(jnp.where(idx[None, :] == next_pos[:, None], block_e[None, :] + 1, 0), axis=1) - 1
    grid_spec = pltpu.PrefetchScalarGridSpec(
        num_scalar_prefetch=6,
        grid=(nblk,),
        in_specs=[
            pl.BlockSpec((MOE_BLOCK * SUBLANES, LANES), lambda i, be, nu, *_: (jnp.minimum(i, jnp.maximum(nu[0] - 1, 0)), 0)),
            pl.BlockSpec((1, 1, 2 * D_FF), lambda i, be, *_: (be[i], 0, 0)),
            pl.BlockSpec((1, 1, D_MODEL), lambda i, be, *_: (be[i], 0, 0)),
            pl.BlockSpec(memory_space=pl.ANY),
            pl.BlockSpec(memory_space=pl.ANY),
        ],
        out_specs=pl.BlockSpec((MOE_BLOCK * SUBLANES, LANES), lambda i, be, *_: (i, 0)),
        scratch_shapes=[pltpu.VMEM((2, D_MODEL, 2 * D_FF), F32), pltpu.VMEM((2, D_FF, D_MODEL), F32),
                        pltpu.SemaphoreType.DMA((2, 2))],
    )
    return pl.pallas_call(
        _moe_kernel,
        grid_spec=grid_spec,
        out_shape=jax.ShapeDtypeStruct(xs.shape, F32),
        compiler_params=pltpu.CompilerParams(dimension_semantics=("arbitrary",), vmem_limit_bytes=MOE_VMEM_LIMIT),
        name="moe_blocks",
    )(block_e, nused, first.astype(jnp.int32), slot, nxt.astype(jnp.int32), nvalid, xs, bgu, bd, wgu, wd)


def _combine_kernel(npt, lp_ref, gate_ref, tabc_ref, tabn_ref, h_ref, ys_hbm, yp_ref, ysm_ref, srt, acc, sem):
    i = pl.program_id(0)
    nt = pl.num_programs(0)
    slot = i % 2

    def fetch_runs(tab_ref, s):
        def run(e, carry):
            local, count, first = tab_ref[0, 0, e], tab_ref[0, 1, e], tab_ref[0, 2, e]

            def piece(off, size):
                pltpu.make_async_copy(_row_tiles(ys_hbm, first + off, size),
                                      _row_tiles(srt.at[s], local + off, size), sem.at[s]).start()
            _for_run_pieces(count, piece)
            return carry
        lax.fori_loop(0, N_EXPERTS, run, 0)

    @pl.when(i == 0)
    def _():
        fetch_runs(tabc_ref, 0)

    def step(s):
        @pl.when(i + 1 < nt)
        def _():
            fetch_runs(tabn_ref, 1 - s)

        for _ in range(TOP_K):
            pltpu.make_async_copy(_row_tiles(ys_hbm, 0, TOK_TILE), _row_tiles(srt.at[s], 0, TOK_TILE),
                                  sem.at[s]).wait()

        def gather(j, carry):
            for u in range(SUBLANES):
                t = j * SUBLANES + u
                tot = None
                for k in range(TOP_K):
                    p = lp_ref[0, 0, t * TOP_K + k]
                    term = (srt[s, pl.ds(pl.multiple_of(p, SUBLANES), SUBLANES), :]
                            * gate_ref[0, 0, t * TOP_K + k])
                    tot = term if tot is None else tot + term
                acc[pl.ds(pl.multiple_of(t * SUBLANES, SUBLANES), SUBLANES), :] = tot
            return carry

        lax.fori_loop(0, TOK_TILE // SUBLANES, gather, 0)

    for s in range(2):
        pl.when(slot == s)(functools.partial(step, s))
    y = h_ref[...] + _load_row_tiles(acc, TOK_TILE)

    @pl.when(i < npt)
    def _():
        yp_ref[...] = y

    @pl.when(i >= npt)
    def _():
        ysm_ref[...] = y


def _combine(lp_tiles, gate_tiles, tab, h, ys, n_prompt_rows):
    T = h.shape[0]
    nt = T // TOK_TILE
    npt = n_prompt_rows // TOK_TILE
    per_assign = pl.BlockSpec((1, 1, TOK_TILE * TOP_K), lambda i: (i, 0, 0), memory_space=pltpu.SMEM)
    tab_blk = lambda imap: pl.BlockSpec((1, SUBLANES, LANES), imap, memory_space=pltpu.SMEM)
    return pl.pallas_call(
        functools.partial(_combine_kernel, npt),
        grid=(nt,),
        in_specs=[per_assign, per_assign,
                  tab_blk(lambda i: (i, 0, 0)), tab_blk(lambda i: (jnp.minimum(i + 1, nt - 1), 0, 0)),
                  pl.BlockSpec((TOK_TILE, D_MODEL), lambda i: (i, 0)),
                  pl.BlockSpec(memory_space=pl.ANY)],
        out_specs=(pl.BlockSpec((TOK_TILE, D_MODEL), lambda i: (jnp.minimum(i, npt - 1), 0)),
                   pl.BlockSpec((TOK_TILE, D_MODEL), lambda i: (jnp.maximum(i - npt, 0), 0))),
        out_shape=(jax.ShapeDtypeStruct((n_prompt_rows, D_MODEL), F32),
                   jax.ShapeDtypeStruct((T - n_prompt_rows, D_MODEL), F32)),
        scratch_shapes=[pltpu.VMEM((2, TOK_TILE * TOP_K * SUBLANES, LANES), F32),
                        pltpu.VMEM((TOK_TILE * SUBLANES, LANES), F32), pltpu.SemaphoreType.DMA((2,))],
        compiler_params=pltpu.CompilerParams(dimension_semantics=("arbitrary",), vmem_limit_bytes=VMEM_LIMIT),
        name="moe_combine",
    )(lp_tiles, gate_tiles, tab, tab, h, ys)


def kernel(x_prompt, x_sample, cache_k_win, cache_v_win, state_conv, state_C, state_n, state_m, g_attn, w_in, b_i,
           b_f, q_norm_g, k_norm_g, sinks, conv_w, conv_b, m_norm_g, w_out, g_ffn, w_router, b_router, w_gate_up,
           b_gate_up, w_down, b_down):
    depth = g_attn.shape[0]
    assert depth == 1
    B, S, _ = x_prompt.shape
    DB, DS, _ = x_sample.shape
    TP = B * S
    TS = DB * DS
    T = TP + TS
    assert T % TOK_TILE == 0 and TP % TOK_TILE == 0 and S % ATT_QB == 0 and S % PROMPT_CHUNK == 0
    assert DS == SUBLANES and DB % SAMPLE_NB == 0 and (SAMPLE_NB * DS) == LANES
    l = 0

    xp = x_prompt.reshape(TP, D_MODEL)
    xs = x_sample.reshape(TS, D_MODEL)

    w_pad = jnp.pad(w_in[l], ((0, 0), (0, IN_PAD - w_in.shape[2]))).astype(BF16)
    wgt = jnp.transpose(w_in[l][:, GATE_COL:GATE_COL + 2 * M_HEADS]).astype(BF16)
    gi = jnp.arange(GROUP_CHUNK) // HEAD_DIM
    gmat = (gi[:, None] == gi[None, :]).astype(BF16)
    qg = jnp.tile(q_norm_g[l], A_HEADS).reshape(1, A_WIDTH)
    kg = jnp.tile(k_norm_g[l], A_KV_HEADS).reshape(1, KV_WIDTH)
    gbias = jnp.concatenate([b_i[l], b_f[l]])
    gbrow = jnp.pad(gbias, (0, LANES - 2 * M_HEADS)).reshape(1, LANES)
    gbcol = gbias.reshape(2 * M_HEADS, 1)
    mng = m_norm_g[l].reshape(1, M_WIDTH)
    cw = conv_w[l]
    cb = conv_b[l].reshape(1, 2 * M_WIDTH)

    proj_w = (g_attn[l].reshape(1, D_MODEL), w_pad, wgt, gmat, qg, kg)
    qn, kn, va, qkm, vm, om, gcol, gt = _inproj(xp, *proj_w)
    qn_s, kn_s, va_s, qkm_s, vm_s, om_s, gcol_s, gt_s = _inproj(xs, *proj_w)

    a_p = _attn_prompt(sinks[l], qn, kn, va, B, S)
    ck = cache_k_win[l].reshape(DB, WINDOW, KV_WIDTH)
    cv = cache_v_win[l].reshape(DB, WINDOW, KV_WIDTH)
    a_s, kwin_s, vwin_s = _attn_sample(sinks[l], qn_s, kn_s, va_s, ck, cv, DB, DS)

    zc = jnp.zeros((B, SUBLANES, 2 * M_WIDTH), F32)
    m_p, C_p, n_p, mm_p = _mlstm(
        qkm, vm, om, gcol, gt, zc,
        jnp.zeros((B, M_HEADS, M_HEAD_DIM, M_HEAD_DIM), F32), jnp.zeros((B, M_HEADS, M_HEAD_DIM), F32),
        jnp.full((B, 1, LANES), NEG, F32), cw, cb, gbrow, gbcol, mng,
        nseq=1, L=PROMPT_CHUNK, ngroups=B, nchunks=S // PROMPT_CHUNK)
    conv_s0 = jnp.pad(state_conv[l], ((0, 0), (SUBLANES - (CONV_W - 1), 0), (0, 0)))
    m0_s = jnp.pad(state_m[l], ((0, 0), (0, LANES - M_HEADS))).reshape(DB, 1, LANES)
    m_s, C_s, n_s, mm_s = _mlstm(
        qkm_s, vm_s, om_s, gcol_s, gt_s, conv_s0, state_C[l], state_n[l], m0_s, cw, cb, gbrow, gbcol, mng,
        nseq=SAMPLE_NB, L=DS, ngroups=DB // SAMPLE_NB, nchunks=1)

    h, hn, route, cnt = _outproj(a_p, a_s, m_p, m_s, xp, xs, w_out[l].astype(BF16), g_ffn[l].reshape(1, D_MODEL),
                                 jnp.transpose(w_router[l]).astype(BF16), b_router[l].reshape(N_EXPERTS, 1))

    nblk = T * TOP_K // MOE_BLOCK + N_EXPERTS
    lp, tab, blk, info = _route_tables(route, cnt, nblk)
    block_e = blk[0, :nblk]
    padrow = info[:, 0]
    npad = info[:, 1]
    nused = info[N_EXPERTS - 1:N_EXPERTS, 2]
    per_assign = lambda a: jnp.transpose(a[:, :TOP_K, :], (0, 2, 1)).reshape(a.shape[0], 1, TOK_TILE * TOP_K)
    lp_tiles = per_assign(lp)
    gate_tiles = per_assign(route)
    xrows = _dispatch(padrow, npad, nused, lp_tiles, tab, hn, nblk * MOE_BLOCK)
    yrows = _moe_blocks(block_e, nused, padrow, xrows,
                        w_gate_up[l], b_gate_up[l].reshape(N_EXPERTS, 1, 2 * D_FF),
                        w_down[l], b_down[l].reshape(N_EXPERTS, 1, D_MODEL))
    y_p, y_s = _combine(lp_tiles, gate_tiles, tab, h, yrows, TP)

    y_p = y_p.reshape(B, S, D_MODEL)
    y_s = y_s.reshape(DB, DS, D_MODEL)
    def seq_tail(rows, n):
        return jnp.stack([rows[(b + 1) * S - n:(b + 1) * S] for b in range(B)])

    kwin_p = seq_tail(kn, WINDOW).reshape(B, WINDOW, A_KV_HEADS, HEAD_DIM)
    vwin_p = seq_tail(va, WINDOW).reshape(B, WINDOW, A_KV_HEADS, HEAD_DIM)
    qkm_s = qkm_s.reshape(DB, DS, 2 * M_WIDTH)
    return (y_p, y_s,
            kwin_p[None], vwin_p[None], seq_tail(qkm, CONV_W - 1)[None],
            C_p[None], n_p[None], mm_p[:, 0, :M_HEADS][None],
            kwin_s.reshape(DB, WINDOW, A_KV_HEADS, HEAD_DIM)[None],
            vwin_s.reshape(DB, WINDOW, A_KV_HEADS, HEAD_DIM)[None],
            qkm_s[:, -(CONV_W - 1):][None],
            C_s[None], n_s[None], mm_s[:, 0, :M_HEADS][None])
```

```python
import functools

import jax
import jax.numpy as jnp
from jax import lax
from jax.experimental import pallas as pl
from jax.experimental.pallas import tpu as pltpu

F32 = jnp.float32
BF16 = jnp.bfloat16

D_MODEL = 1024
HEAD_DIM = 64
A_HEADS = 8
A_KV_HEADS = 2
A_GROUP = A_HEADS // A_KV_HEADS
A_WIDTH = A_HEADS * HEAD_DIM
KV_WIDTH = A_KV_HEADS * HEAD_DIM
WINDOW = 128
M_HEADS = 4
M_HEAD_DIM = 128
M_WIDTH = M_HEADS * M_HEAD_DIM
CONV_W = 4
N_EXPERTS = 32
TOP_K = 4
D_FF = D_MODEL
SWIGLU_LIMIT = 7.0
SWIGLU_ALPHA = 1.702
MOE_BLOCK = 512
MOE_QUARTER = 128
EPS = 1e-6
NEG = -1e30

LANES = 128
SUBLANES = 8
GATE_COL = A_WIDTH + 2 * KV_WIDTH + 4 * M_WIDTH
IN_PAD = GATE_COL + LANES
TOK_TILE = 512
PROJ_TILE = 1024
PROJ_VMEM_LIMIT = 58 * 1024 * 1024
GROUP_CHUNK = 256
ATT_QB = 256
ATT_SB = 128
SAMPLE_NB = 16
PROMPT_CHUNK = 256
VMEM_LIMIT = 48 * 1024 * 1024
MOE_VMEM_LIMIT = 56 * 1024 * 1024


def _dot(a, b):
    return jnp.dot(a, b, preferred_element_type=F32)


def _dot_nt(a, b):
    return lax.dot_general(a, b, (((1,), (1,)), ((), ())), preferred_element_type=F32)


def _dot_tn(a, b):
    return lax.dot_general(a, b, (((0,), (0,)), ((), ())), preferred_element_type=F32)


def _split3(x):
    hi = x.astype(BF16)
    r1 = x - hi.astype(F32)
    mid = r1.astype(BF16)
    lo = (r1 - mid.astype(F32)).astype(BF16)
    return hi, mid, lo


def _log_sigmoid(x):
    return jnp.minimum(x, 0.0) - jnp.log1p(jnp.exp(-jnp.abs(x)))


def _sigmoid(x):
    return 0.5 * jnp.tanh(0.5 * x) + 0.5


def _load_row_tiles(ref2, rows):
    return jnp.concatenate([ref2[pl.ds(s, rows, stride=SUBLANES), :] for s in range(SUBLANES)], axis=1)


def _store_row_tiles(ref2, rows, val):
    for s in range(SUBLANES):
        ref2[pl.ds(s, rows, stride=SUBLANES), :] = val[:, s * LANES:(s + 1) * LANES]


def _row_tiles(ref2, first, n):
    start = first * SUBLANES
    if not isinstance(start, int):
        start = pl.multiple_of(start, SUBLANES)
    return ref2.at[pl.ds(start, n * SUBLANES), :]


def _split_specs(n_prompt_tiles, rows, width):
    return (pl.BlockSpec((rows, width), lambda i, *_: (jnp.minimum(i, n_prompt_tiles - 1), 0)),
            pl.BlockSpec((rows, width), lambda i, *_: (jnp.maximum(i - n_prompt_tiles, 0), 0)))


def _inproj_kernel(x_ref, g_ref, w_ref, wgt_ref, gmat_ref, qg_ref, kg_ref,
                   qn_ref, kn_ref, va_ref, qkm_ref, vm_ref, om_ref, gcol_ref, gt_ref):
    x = x_ref[...]
    ms = jnp.mean(x * x, axis=-1, keepdims=True)
    xn = ((x * lax.rsqrt(ms + EPS)) * g_ref[...]).astype(BF16)

    def seg(lo, hi):
        return _dot(xn, w_ref[:, lo:hi])

    def head_norm(z, gmat, g):
        parts = _split3(z * z)
        w = min(GROUP_CHUNK, z.shape[1])
        ss = jnp.concatenate([sum(_dot(p[:, c:c + w], gmat) for p in parts) for c in range(0, z.shape[1], w)], axis=1)
        return (z * lax.rsqrt(ss * (1.0 / HEAD_DIM) + EPS)) * g

    o0 = A_WIDTH
    o1 = o0 + KV_WIDTH
    o2 = o1 + KV_WIDTH
    o3 = o2 + 2 * M_WIDTH
    o4 = o3 + M_WIDTH
    o5 = o4 + M_WIDTH
    qn_ref[...] = head_norm(seg(0, o0), gmat_ref[...], qg_ref[...])
    kn_ref[...] = head_norm(seg(o0, o1), gmat_ref[:KV_WIDTH, :KV_WIDTH], kg_ref[...])
    va_ref[...] = seg(o1, o2)
    qkm_ref[...] = seg(o2, o3)
    vm_ref[...] = seg(o3, o4)
    om_ref[...] = seg(o4, o5)
    gcol_ref[...] = seg(o5, o5 + LANES)
    gt_ref[0] = _dot_nt(wgt_ref[...], xn)


def _inproj(x, g_attn, w_pad, wgt, gmat, qg, kg):
    T = x.shape[0]
    nt = T // PROJ_TILE
    row = lambda w: pl.BlockSpec((PROJ_TILE, w), lambda i: (i, 0))
    full = lambda a: pl.BlockSpec(a.shape, lambda i: (0,) * a.ndim, pipeline_mode=pl.Buffered(1))
    out_shape = (
        jax.ShapeDtypeStruct((T, A_WIDTH), F32),
        jax.ShapeDtypeStruct((T, KV_WIDTH), F32),
        jax.ShapeDtypeStruct((T, KV_WIDTH), F32),
        jax.ShapeDtypeStruct((T, 2 * M_WIDTH), F32),
        jax.ShapeDtypeStruct((T, M_WIDTH), F32),
        jax.ShapeDtypeStruct((T, M_WIDTH), F32),
        jax.ShapeDtypeStruct((T, LANES), F32),
        jax.ShapeDtypeStruct((nt, SUBLANES, PROJ_TILE), F32),
    )
    out_specs = (row(A_WIDTH), row(KV_WIDTH), row(KV_WIDTH), row(2 * M_WIDTH), row(M_WIDTH), row(M_WIDTH),
                 row(LANES), pl.BlockSpec((1, SUBLANES, PROJ_TILE), lambda i: (i, 0, 0)))
    return pl.pallas_call(
        _inproj_kernel,
        grid=(nt,),
        in_specs=[row(D_MODEL), full(g_attn), full(w_pad), full(wgt), full(gmat), full(qg), full(kg)],
        out_specs=out_specs,
        out_shape=out_shape,
        compiler_params=pltpu.CompilerParams(dimension_semantics=("parallel",), vmem_limit_bytes=PROJ_VMEM_LIMIT),
        name="inproj",
    )(x, g_attn, w_pad, wgt, gmat, qg, kg)


def _softmax_sink(pieces, masks, sink_col):
    masked = [jnp.where(mk, s, NEG) for s, mk in zip(pieces, masks)]
    m = sink_col
    for s in masked:
        m = jnp.maximum(m, jnp.max(s, axis=-1, keepdims=True))
    ps = [jnp.exp(s - m) for s in masked]
    den = jnp.exp(sink_col - m)
    for p in ps:
        den = den + jnp.sum(p, axis=-1, keepdims=True)
    return ps, 1.0 / den


def _stack_heads(q, g):
    return jnp.concatenate([q[:, (A_GROUP * g + i) * HEAD_DIM:(A_GROUP * g + i + 1) * HEAD_DIM]
                            for i in range(A_GROUP)], axis=0)


def _sink_col(sink_ref, g, rows_per_head):
    r = lax.broadcasted_iota(jnp.int32, (A_GROUP * rows_per_head, 1), 0)
    col = jnp.zeros((A_GROUP * rows_per_head, 1), F32)
    for i in range(A_GROUP):
        col = jnp.where(r // rows_per_head == i, sink_ref[A_GROUP * g + i], col)
    return col


def _attn_prompt_kernel(sink_ref, q_ref, kp_ref, kc_ref, vp_ref, vc_ref, o_ref):
    j = pl.program_id(1)
    scale = HEAD_DIM ** -0.5
    kall = jnp.concatenate([kp_ref[...], kc_ref[...]], axis=0).astype(BF16)
    vall = jnp.concatenate([vp_ref[...], vc_ref[...]], axis=0).astype(BF16)
    nrow = A_GROUP * ATT_SB
    r = lax.broadcasted_iota(jnp.int32, (nrow, 2 * ATT_SB), 0) % ATT_SB
    c = lax.broadcasted_iota(jnp.int32, (nrow, 2 * ATT_SB), 1)
    band = jnp.logical_and(c >= r, c <= r + WINDOW)
    band0 = jnp.logical_and(band, jnp.logical_or(c >= ATT_SB, j > 0))
    for sb in range(ATT_QB // ATT_SB):
        q = (q_ref[sb * ATT_SB:(sb + 1) * ATT_SB, :] * scale).astype(BF16)
        kwin = kall[sb * ATT_SB:(sb + 2) * ATT_SB]
        vwin = vall[sb * ATT_SB:(sb + 2) * ATT_SB]
        outs = []
        for g in range(A_KV_HEADS):
            lo, hi = g * HEAD_DIM, (g + 1) * HEAD_DIM
            s = _dot_nt(_stack_heads(q, g), kwin[:, lo:hi])
            (p,), inv = _softmax_sink([s], [band0 if sb == 0 else band], _sink_col(sink_ref, g, ATT_SB))
            o = _dot(p.astype(BF16), vwin[:, lo:hi]) * inv
            outs += [o[i * ATT_SB:(i + 1) * ATT_SB] for i in range(A_GROUP)]
        o_ref[sb * ATT_SB:(sb + 1) * ATT_SB, :] = jnp.concatenate(outs, axis=1)


def _attn_prompt(sinks, qn, kn, va, batch, seq):
    nq = seq // ATT_QB
    ratio = ATT_QB // ATT_SB
    cur = lambda w: pl.BlockSpec((ATT_QB, w), lambda b, j: (b * nq + j, 0))
    prev = lambda w: pl.BlockSpec((ATT_SB, w), lambda b, j: (jnp.maximum((b * nq + j) * ratio - 1, 0), 0))
    return pl.pallas_call(
        _attn_prompt_kernel,
        grid=(batch, nq),
        in_specs=[pl.BlockSpec(memory_space=pltpu.SMEM), cur(A_WIDTH), prev(KV_WIDTH), cur(KV_WIDTH),
                  prev(KV_WIDTH), cur(KV_WIDTH)],
        out_specs=cur(A_WIDTH),
        out_shape=jax.ShapeDtypeStruct((batch * seq, A_WIDTH), F32),
        compiler_params=pltpu.CompilerParams(dimension_semantics=("parallel", "parallel"),
                                             vmem_limit_bytes=VMEM_LIMIT),
        name="attn_prompt",
    )(sinks, qn, kn, kn, va, va)


def _attn_sample_kernel(dec, sink_ref, q_ref, kn_ref, vn_ref, ck_ref, cv_ref, o_ref, kw_ref, vw_ref):
    scale = HEAD_DIM ** -0.5
    rows = SAMPLE_NB * dec
    knew = kn_ref[...]
    vnew = vn_ref[...]
    knew_b = knew.astype(BF16)
    vnew_b = vnew.astype(BF16)
    nrow = A_GROUP * dec
    t = lax.broadcasted_iota(jnp.int32, (nrow, WINDOW), 0) % dec
    c = lax.broadcasted_iota(jnp.int32, (nrow, WINDOW), 1)
    m_cache = c >= t
    cn = lax.broadcasted_iota(jnp.int32, (nrow, rows), 1)
    tn = lax.broadcasted_iota(jnp.int32, (nrow, rows), 0) % dec
    for i in range(SAMPLE_NB):
        q = (q_ref[i * dec:(i + 1) * dec, :] * scale).astype(BF16)
        ck = ck_ref[i].astype(BF16)
        cv = cv_ref[i].astype(BF16)
        m_new = jnp.logical_and(cn // dec == i, cn % dec <= tn)
        outs = []
        for g in range(A_KV_HEADS):
            lo, hi = g * HEAD_DIM, (g + 1) * HEAD_DIM
            qs = _stack_heads(q, g)
            s_c = _dot_nt(qs, ck[:, lo:hi])
            s_n = _dot_nt(qs, knew_b[:, lo:hi])
            (p_c, p_n), inv = _softmax_sink([s_c, s_n], [m_cache, m_new], _sink_col(sink_ref, g, dec))
            o = (_dot(p_c.astype(BF16), cv[:, lo:hi]) + _dot(p_n.astype(BF16), vnew_b[:, lo:hi])) * inv
            outs += [o[h * dec:(h + 1) * dec] for h in range(A_GROUP)]
        o_ref[i * dec:(i + 1) * dec, :] = jnp.concatenate(outs, axis=1)
        kw_ref[i, 0:WINDOW - dec, :] = ck_ref[i, dec:WINDOW, :]
        kw_ref[i, WINDOW - dec:WINDOW, :] = knew[i * dec:(i + 1) * dec]
        vw_ref[i, 0:WINDOW - dec, :] = cv_ref[i, dec:WINDOW, :]
        vw_ref[i, WINDOW - dec:WINDOW, :] = vnew[i * dec:(i + 1) * dec]


def _attn_sample(sinks, qn, kn, va, ck, cv, dbatch, dec):
    rows = SAMPLE_NB * dec
    tokrow = lambda w: pl.BlockSpec((rows, w), lambda i: (i, 0))
    cache = pl.BlockSpec((SAMPLE_NB, WINDOW, KV_WIDTH), lambda i: (i, 0, 0))
    return pl.pallas_call(
        functools.partial(_attn_sample_kernel, dec),
        grid=(dbatch // SAMPLE_NB,),
        in_specs=[pl.BlockSpec(memory_space=pltpu.SMEM), tokrow(A_WIDTH), tokrow(KV_WIDTH), tokrow(KV_WIDTH),
                  cache, cache],
        out_specs=(pl.BlockSpec((rows, A_WIDTH), lambda i: (i, 0)), cache, cache),
        out_shape=(jax.ShapeDtypeStruct((dbatch * dec, A_WIDTH), F32),
                   jax.ShapeDtypeStruct((dbatch, WINDOW, KV_WIDTH), F32),
                   jax.ShapeDtypeStruct((dbatch, WINDOW, KV_WIDTH), F32)),
        compiler_params=pltpu.CompilerParams(dimension_semantics=("parallel",), vmem_limit_bytes=VMEM_LIMIT),
        name="attn_sample",
    )(sinks, qn, kn, va, ck, cv)


def _mlstm_kernel(nseq, L, qk_ref, v_ref, o_ref, gcol_ref, gt_ref, conv0_ref, c0_ref, n0_ref, m0_ref,
                  cw_ref, cb_ref, gbrow_ref, gbcol_ref, mng_ref, mask_ref,
                  out_ref, cst_ref, nst_ref, mst_ref, prev_ref):
    R = nseq * L
    ci = pl.program_id(1)

    @pl.when(ci == 0)
    def _():
        cst_ref[...] = c0_ref[...]
        nst_ref[...] = n0_ref[...]
        mst_ref[...] = m0_ref[...]
        prev_ref[...] = conv0_ref[0]

    raw = qk_ref[...]
    row = lax.broadcasted_iota(jnp.int32, (R, 1), 0)
    tpos = row % L
    rseq = row // L
    acc = raw * cw_ref[CONV_W - 1:CONV_W, :] + cb_ref[...]
    if nseq == 1:
        prev8 = prev_ref[...]
        t8 = lax.broadcasted_iota(jnp.int32, (SUBLANES, 1), 0)
    else:
        prevsrc = conv0_ref[...].reshape(R, 2 * M_WIDTH)
    for k in range(1, CONV_W):
        rolled = pltpu.roll(raw, k, 0)
        if nseq == 1:
            head = jnp.where(t8 >= k, rolled[0:SUBLANES], pltpu.roll(prev8, k, 0))
            sh = jnp.concatenate([head, rolled[SUBLANES:]], axis=0)
        else:
            sh = jnp.where(tpos >= k, rolled, pltpu.roll(prevsrc, R - SUBLANES + k, 0))
        acc = acc + sh * cw_ref[CONV_W - 1 - k:CONV_W - k, :]
    if nseq == 1:
        prev_ref[...] = raw[R - SUBLANES:R]
    qkc = acc * _sigmoid(acc)

    gc = gcol_ref[...] + gbrow_ref[...]
    gr = gt_ref[0] + gbcol_ref[...]
    lsc = _log_sigmoid(gc)
    lsr = _log_sigmoid(gr)
    mb = mask_ref[...]
    maskb = mb > 0
    bcol = sum(_dot(mb, p) for p in _split3(lsc))
    brow = sum(_dot_nt(p, mb) for p in _split3(lsr))

    lane = lax.broadcasted_iota(jnp.int32, (1, LANES), 1)
    m_new = [jnp.zeros((1, LANES), F32) for _ in range(nseq)]
    for h in range(M_HEADS):
        sl = slice(h * M_HEAD_DIM, (h + 1) * M_HEAD_DIM)
        qh = qkc[:, sl]
        kh = qkc[:, M_WIDTH + h * M_HEAD_DIM:M_WIDTH + (h + 1) * M_HEAD_DIM] * (M_HEAD_DIM ** -0.5)
        vh = v_ref[:, sl]
        qb, kb, vb = qh.astype(BF16), kh.astype(BF16), vh.astype(BF16)
        ig_c = gc[:, h:h + 1]
        b_c = bcol[:, M_HEADS + h:M_HEADS + h + 1]
        ig_r = gr[h:h + 1, :]
        b_r = brow[M_HEADS + h:M_HEADS + h + 1, :]
        if nseq == 1:
            m0c = mst_ref[0][:, h:h + 1]
            n0rows = nst_ref[0, h:h + 1, :]
        else:
            m0c = jnp.zeros((R, 1), F32)
            n0rows = jnp.zeros((R, M_HEAD_DIM), F32)
            for s in range(nseq):
                m0c = jnp.where(rseq == s, mst_ref[s][:, h:h + 1], m0c)
                n0rows = jnp.where(rseq == s, nst_ref[s, h:h + 1, :], n0rows)
        dm = jnp.where(maskb, b_c - b_r + ig_r, NEG)
        a_c = b_c + m0c
        m_c = jnp.maximum(a_c, jnp.max(dm, axis=-1, keepdims=True))
        w = jnp.exp(dm - m_c)
        sc = jnp.exp(a_c - m_c)
        wqk = w * _dot_nt(qb, kb)
        if nseq == 1:
            inter = _dot_nt(qb, cst_ref[0, h].astype(BF16))
        else:
            inter = jnp.zeros((R, M_HEAD_DIM), F32)
            for s in range(nseq):
                qs = jnp.where(rseq == s, qh, 0.0).astype(BF16)
                inter = inter + _dot_nt(qs, cst_ref[s, h].astype(BF16))
        num = _dot(wqk.astype(BF16), vb) + sc * inter
        den = jnp.sum(wqk, axis=-1, keepdims=True) + sc * jnp.sum(qh * n0rows, axis=-1, keepdims=True)
        hh = num / jnp.maximum(jnp.abs(den), jnp.exp(-m_c))

        for s in range(nseq):
            e = s * L + L - 1
            m_end = m_c[e:e + 1, :]
            wend = jnp.exp(b_c[e:e + 1, :] - b_c + ig_c - m_end)
            if nseq > 1:
                wend = jnp.where(rseq == s, wend, 0.0)
            sce = jnp.exp(a_c[e:e + 1, :] - m_end)
            c_new = sce * cst_ref[s, h] + _dot_tn((vh * wend).astype(BF16), kb)
            n_new = sce * nst_ref[s, h:h + 1, :] + jnp.sum(wend * kh, axis=0, keepdims=True)
            cst_ref[s, h] = c_new
            nst_ref[s, h:h + 1, :] = n_new
            m_new[s] = jnp.where(lane == h, m_end, m_new[s])

        hn = (hh * lax.rsqrt(jnp.mean(hh * hh, axis=-1, keepdims=True) + EPS)) * mng_ref[:, sl]
        out_ref[:, sl] = _sigmoid(o_ref[:, sl]) * hn
    for s in range(nseq):
        mst_ref[s] = m_new[s]


def _mlstm(qkm, vm, om, gcol, gt, conv0, c0, n0, m0, cw, cb, gbrow, gbcol, mng, nseq, L, ngroups, nchunks):
    R = nseq * L
    per_tile = gt.shape[2] // R
    tok = lambda w: pl.BlockSpec((R, w), lambda g, c: (g * nchunks + c, 0))
    gt_spec = pl.BlockSpec((1, SUBLANES, R),
                           lambda g, c: ((g * nchunks + c) // per_tile, 0, (g * nchunks + c) % per_tile))
    full = lambda a: pl.BlockSpec(a.shape, lambda g, c: (0,) * a.ndim)
    st4 = pl.BlockSpec((nseq, M_HEADS, M_HEAD_DIM, M_HEAD_DIM), lambda g, c: (g, 0, 0, 0))
    st3 = pl.BlockSpec((nseq, M_HEADS, M_HEAD_DIM), lambda g, c: (g, 0, 0))
    stm = pl.BlockSpec((nseq, 1, LANES), lambda g, c: (g, 0, 0))
    conv_spec = pl.BlockSpec((nseq, SUBLANES, 2 * M_WIDTH), lambda g, c: (g, 0, 0))
    r = jnp.arange(R)
    mask = ((r[:, None] // L == r[None, :] // L) & (r[None, :] <= r[:, None])).astype(BF16)
    nstate = ngroups * nseq
    return pl.pallas_call(
        functools.partial(_mlstm_kernel, nseq, L),
        grid=(ngroups, nchunks),
        in_specs=[tok(2 * M_WIDTH), tok(M_WIDTH), tok(M_WIDTH), tok(LANES), gt_spec, conv_spec, st4, st3, stm,
                  full(cw), full(cb), full(gbrow), full(gbcol), full(mng), full(mask)],
        out_specs=(pl.BlockSpec((R, M_WIDTH), lambda g, c: (g * nchunks + c, 0)), st4, st3, stm),
        out_shape=(jax.ShapeDtypeStruct((ngroups * nchunks * R, M_WIDTH), F32),
                   jax.ShapeDtypeStruct((nstate, M_HEADS, M_HEAD_DIM, M_HEAD_DIM), F32),
                   jax.ShapeDtypeStruct((nstate, M_HEADS, M_HEAD_DIM), F32),
                   jax.ShapeDtypeStruct((nstate, 1, LANES), F32)),
        scratch_shapes=[pltpu.VMEM((SUBLANES, 2 * M_WIDTH), F32)],
        compiler_params=pltpu.CompilerParams(dimension_semantics=("parallel", "arbitrary"),
                                             vmem_limit_bytes=VMEM_LIMIT),
        name="mlstm_n%d" % nseq,
    )(qkm, vm, om, gcol, gt, conv0, c0, n0, m0, cw, cb, gbrow, gbcol, mng, mask)


def _outproj_kernel(npt, ap_ref, as_ref, mp_ref, ms_ref, xp_ref, xs_ref, wo_ref, g_ref, wrt_ref, brc_ref,
                    h_ref, hn_ref, route_ref, cnt_ref):
    def project(a_ref, m_ref, x_ref):
        h_ref[...] = (x_ref[...] + _dot(a_ref[...].astype(BF16), wo_ref[0:A_WIDTH, :])
                      + _dot(m_ref[...].astype(BF16), wo_ref[A_WIDTH:A_WIDTH + M_WIDTH, :]))

    @pl.when(pl.program_id(0) < npt)
    def _():
        project(ap_ref, mp_ref, xp_ref)

    @pl.when(pl.program_id(0) >= npt)
    def _():
        project(as_ref, ms_ref, xs_ref)

    h = h_ref[...]
    hn = (h * lax.rsqrt(jnp.mean(h * h, axis=-1, keepdims=True) + EPS)) * g_ref[...]
    _store_row_tiles(hn_ref, TOK_TILE, hn)
    logits = _dot_nt(wrt_ref[...], hn.astype(BF16)) + brc_ref[...]
    eidx = lax.broadcasted_iota(jnp.int32, logits.shape, 0).astype(F32)
    picked = jnp.zeros(logits.shape, F32)
    top0 = None
    den = None
    es = []
    ids = []
    for k in range(TOP_K):
        mx = jnp.max(logits, axis=0, keepdims=True)
        idx = jnp.min(jnp.where(logits == mx, eidx, float(N_EXPERTS)), axis=0, keepdims=True)
        if k == 0:
            top0 = mx
        e = jnp.exp(mx - top0)
        den = e if den is None else den + e
        es.append(e)
        ids.append(idx)
        hit = eidx == idx
        picked = jnp.where(hit, 1.0, picked)
        logits = jnp.where(hit, -jnp.inf, logits)
    route_ref[0] = jnp.concatenate([e / den for e in es] + ids, axis=0)

    @pl.when(pl.program_id(0) == 0)
    def _():
        cnt_ref[...] = jnp.zeros_like(cnt_ref)

    cnt_ref[...] += jnp.broadcast_to(jnp.sum(picked, axis=1, keepdims=True), cnt_ref.shape)


def _outproj(a_p, a_s, m_p, m_s, xp, xs, w_out, g_ffn, wr_t, br_col):
    T = xp.shape[0] + xs.shape[0]
    nt = T // TOK_TILE
    npt = xp.shape[0] // TOK_TILE
    row = lambda w: pl.BlockSpec((TOK_TILE, w), lambda i: (i, 0))
    full = lambda a: pl.BlockSpec(a.shape, lambda i: (0,) * a.ndim)
    return pl.pallas_call(
        functools.partial(_outproj_kernel, npt),
        grid=(nt,),
        in_specs=[*_split_specs(npt, TOK_TILE, A_WIDTH), *_split_specs(npt, TOK_TILE, M_WIDTH),
                  *_split_specs(npt, TOK_TILE, D_MODEL), full(w_out), full(g_ffn), full(wr_t), full(br_col)],
        out_specs=(row(D_MODEL), pl.BlockSpec((TOK_TILE * SUBLANES, LANES), lambda i: (i, 0)),
                   pl.BlockSpec((1, 2 * TOP_K, TOK_TILE), lambda i: (i, 0, 0)),
                   pl.BlockSpec((N_EXPERTS, LANES), lambda i: (0, 0))),
        out_shape=(jax.ShapeDtypeStruct((T, D_MODEL), F32), jax.ShapeDtypeStruct((T * SUBLANES, LANES), F32),
                   jax.ShapeDtypeStruct((nt, 2 * TOP_K, TOK_TILE), F32),
                   jax.ShapeDtypeStruct((N_EXPERTS, LANES), F32)),
        compiler_params=pltpu.CompilerParams(dimension_semantics=("arbitrary",), vmem_limit_bytes=VMEM_LIMIT),
        name="outproj_router",
    )(a_p, a_s, m_p, m_s, xp, xs, w_out, g_ffn, wr_t, br_col)


def _route_kernel(nblk_pad, route_ref, cnt_ref, ustrict_ref, lstrict_ref, lp_ref, tab_ref, blk_ref, info_ref,
                  carry_ref):
    i = pl.program_id(0)
    cnt = cnt_ref[...]
    nb_e = jnp.floor((cnt + (MOE_BLOCK - 1.0)) * (1.0 / MOE_BLOCK))
    bstart = sum(_dot(lstrict_ref[...], p) for p in _split3(nb_e))
    bend = bstart + nb_e
    row_start = bstart * float(MOE_BLOCK)

    @pl.when(i == 0)
    def _():
        carry_ref[...] = jnp.zeros_like(carry_ref)
        bi = lax.broadcasted_iota(jnp.int32, (N_EXPERTS, nblk_pad), 1).astype(F32)
        done = jnp.where(bend[:, 0:1] <= bi, 1.0, 0.0)
        be = jnp.minimum(jnp.sum(done, axis=0, keepdims=True), N_EXPERTS - 1.0)
        blk_ref[...] = jnp.broadcast_to(be, blk_ref.shape).astype(jnp.int32)
        lane = lax.broadcasted_iota(jnp.int32, (N_EXPERTS, LANES), 1)
        info = jnp.where(lane == 0, row_start + cnt, 0.0)
        info = jnp.where(lane == 1, nb_e * float(MOE_BLOCK) - cnt, info)
        info = jnp.where(lane == 2, bend, info)
        info_ref[...] = info.astype(jnp.int32)

    r = route_ref[0]
    eidx = lax.broadcasted_iota(jnp.int32, (N_EXPERTS, TOK_TILE), 0).astype(F32)
    sel = [eidx == r[TOP_K + k:TOP_K + k + 1, :] for k in range(TOP_K)]
    oh = jnp.zeros((N_EXPERTS, TOK_TILE), F32)
    for k in range(TOP_K):
        oh = jnp.where(sel[k], 1.0, oh)
    cnt_t = jnp.broadcast_to(jnp.sum(oh, axis=1, keepdims=True), (N_EXPERTS, LANES))
    seg = sum(_dot(lstrict_ref[...], p) for p in _split3(cnt_t))
    local = _dot(oh.astype(BF16), ustrict_ref[...]) + seg[:, 0:1]
    rows = [jnp.sum(jnp.where(sel[k], local, 0.0), axis=0, keepdims=True) for k in range(TOP_K)]
    lp_ref[0] = (jnp.concatenate(rows + [jnp.zeros((TOP_K, TOK_TILE), F32)], axis=0)
                 * float(SUBLANES)).astype(jnp.int32)
    diag = (lax.broadcasted_iota(jnp.int32, (N_EXPERTS, LANES), 0)
            == lax.broadcasted_iota(jnp.int32, (N_EXPERTS, LANES), 1))
    to_lanes = lambda col: jnp.sum(jnp.where(diag, col, 0.0), axis=0, keepdims=True)
    tab = [to_lanes(seg), to_lanes(cnt_t), to_lanes(row_start + carry_ref[...])]
    tab_ref[0] = jnp.concatenate(tab + [jnp.zeros((SUBLANES - len(tab), LANES), F32)], axis=0).astype(jnp.int32)
    carry_ref[...] += cnt_t


def _route_tables(route, cnt, nblk):
    nt = route.shape[0]
    nblk_pad = -(-nblk // LANES) * LANES
    a = jnp.arange(TOK_TILE)
    ustrict = (a[:, None] < a[None, :]).astype(BF16)
    b = jnp.arange(N_EXPERTS)
    lstrict = (b[:, None] > b[None, :]).astype(BF16)
    full = lambda x: pl.BlockSpec(x.shape, lambda i: (0,) * x.ndim)
    tile = pl.BlockSpec((1, 2 * TOP_K, TOK_TILE), lambda i: (i, 0, 0))
    return pl.pallas_call(
        functools.partial(_route_kernel, nblk_pad),
        grid=(nt,),
        in_specs=[tile, full(cnt), full(ustrict), full(lstrict)],
        out_specs=(tile, pl.BlockSpec((1, SUBLANES, LANES), lambda i: (i, 0, 0)),
                   pl.BlockSpec((SUBLANES, nblk_pad), lambda i: (0, 0)),
                   pl.BlockSpec((N_EXPERTS, LANES), lambda i: (0, 0))),
        out_shape=(jax.ShapeDtypeStruct((nt, 2 * TOP_K, TOK_TILE), jnp.int32),
                   jax.ShapeDtypeStruct((nt, SUBLANES, LANES), jnp.int32),
                   jax.ShapeDtypeStruct((SUBLANES, nblk_pad), jnp.int32),
                   jax.ShapeDtypeStruct((N_EXPERTS, LANES), jnp.int32)),
        scratch_shapes=[pltpu.VMEM((N_EXPERTS, LANES), F32)],
        compiler_params=pltpu.CompilerParams(dimension_semantics=("arbitrary",), vmem_limit_bytes=VMEM_LIMIT),
        name="route_tables",
    )(route, cnt, ustrict, lstrict)


RUN_PIECES = tuple(1 << b for b in range(9, -1, -1))


def _for_run_pieces(count, fn):
    for size in RUN_PIECES:
        @pl.when((count & size) != 0)
        def _(size=size):
            fn(count & ~(2 * size - 1), size)


def _dispatch_kernel(padrow_ref, npad_ref, nu_ref, lp_ref, tab_ref, hn_ref, xs_hbm, srt, zbuf, sem, zsem):
    i = pl.program_id(0)
    nt = pl.num_programs(0)
    slot = i % 2

    @pl.when(i == 0)
    def _():
        zbuf[...] = jnp.zeros_like(zbuf)
        nblk = xs_hbm.shape[0] // (MOE_BLOCK * SUBLANES)

        def tail_start(b, c):
            pltpu.make_async_copy(zbuf, _row_tiles(xs_hbm, b * MOE_BLOCK, MOE_BLOCK), zsem).start()
            return c

        def tail_wait(b, c):
            pltpu.make_async_copy(zbuf, _row_tiles(xs_hbm, b * MOE_BLOCK, MOE_BLOCK), zsem).wait()
            return c

        lax.fori_loop(nu_ref[0], nblk, tail_start, 0)
        lax.fori_loop(nu_ref[0], nblk, tail_wait, 0)

        def pad_runs(wait):
            def per_expert(e, carry):
                first, count = padrow_ref[e], npad_ref[e]

                def piece(off, size):
                    cp = pltpu.make_async_copy(_row_tiles(zbuf, 0, size), _row_tiles(xs_hbm, first + off, size), zsem)
                    cp.wait() if wait else cp.start()
                _for_run_pieces(count, piece)
                return carry
            lax.fori_loop(0, N_EXPERTS, per_expert, 0)

        pad_runs(False)
        pad_runs(True)

    def wait_runs(s):
        for _ in range(TOP_K):
            pltpu.make_async_copy(_row_tiles(srt.at[s], 0, TOK_TILE), _row_tiles(xs_hbm, 0, TOK_TILE), sem.at[s]).wait()

    def step(s):
        @pl.when(i >= 2)
        def _():
            wait_runs(s)

        def permute(j, carry):
            for u in range(SUBLANES):
                t = j * SUBLANES + u
                row = hn_ref[pl.ds(pl.multiple_of(t * SUBLANES, SUBLANES), SUBLANES), :]
                for k in range(TOP_K):
                    p = lp_ref[0, 0, t * TOP_K + k]
                    srt[s, pl.ds(pl.multiple_of(p, SUBLANES), SUBLANES), :] = row
            return carry

        lax.fori_loop(0, TOK_TILE // SUBLANES, permute, 0)

        def send_run(e, carry):
            local, count, first = tab_ref[0, 0, e], tab_ref[0, 1, e], tab_ref[0, 2, e]

            def piece(off, size):
                pltpu.make_async_copy(_row_tiles(srt.at[s], local + off, size),
                                      _row_tiles(xs_hbm, first + off, size), sem.at[s]).start()
            _for_run_pieces(count, piece)
            return carry

        lax.fori_loop(0, N_EXPERTS, send_run, 0)

        @pl.when(i == nt - 1)
        def _():
            @pl.when(nt >= 2)
            def _():
                wait_runs(1 - s)

            wait_runs(s)

    for s in range(2):
        pl.when(slot == s)(functools.partial(step, s))


def _dispatch(padrow, npad, nused, lp_tiles, tab, hn, n_rows):
    nt = lp_tiles.shape[0]
    grid_spec = pltpu.PrefetchScalarGridSpec(
        num_scalar_prefetch=3,
        grid=(nt,),
        in_specs=[pl.BlockSpec((1, 1, TOK_TILE * TOP_K), lambda i, *_: (i, 0, 0), memory_space=pltpu.SMEM),
                  pl.BlockSpec((1, SUBLANES, LANES), lambda i, *_: (i, 0, 0), memory_space=pltpu.SMEM),
                  pl.BlockSpec((TOK_TILE * SUBLANES, LANES), lambda i, *_: (i, 0))],
        out_specs=pl.BlockSpec(memory_space=pl.ANY),
        scratch_shapes=[pltpu.VMEM((2, TOK_TILE * TOP_K * SUBLANES, LANES), F32),
                        pltpu.VMEM((MOE_BLOCK * SUBLANES, LANES), F32),
                        pltpu.SemaphoreType.DMA((2,)), pltpu.SemaphoreType.DMA(())],
    )
    return pl.pallas_call(
        _dispatch_kernel,
        grid_spec=grid_spec,
        out_shape=jax.ShapeDtypeStruct((n_rows * SUBLANES, LANES), F32),
        compiler_params=pltpu.CompilerParams(dimension_semantics=("arbitrary",), vmem_limit_bytes=VMEM_LIMIT),
        name="moe_dispatch",
    )(padrow, npad, nused, lp_tiles, tab, hn)


def _moe_kernel(be_ref, nu_ref, first_ref, slot_ref, nxt_ref, nvalid_ref, x_ref, bgu_ref, bd_ref, wgu_hbm, wd_hbm, y_ref,
                wgu_buf, wd_buf, wsem):
    i = pl.program_id(0)
    used = i < nu_ref[0]
    s = slot_ref[i]

    def fetch(e, sl):
        return (pltpu.make_async_copy(wgu_hbm.at[e], wgu_buf.at[sl], wsem.at[0, sl]),
                pltpu.make_async_copy(wd_hbm.at[e], wd_buf.at[sl], wsem.at[1, sl]))

    @pl.when(jnp.logical_and(used, first_ref[i] == 1))
    def _():
        @pl.when(i == 0)
        def _():
            for c in fetch(be_ref[0], 0):
                c.start()

        for c in fetch(be_ref[i], s):
            c.wait()

        @pl.when(nxt_ref[i] >= 0)
        def _():
            for c in fetch(nxt_ref[i], 1 - s):
                c.start()

    def expert(rows):
        x = _load_row_tiles(x_ref, rows).astype(BF16)
        hb = _dot(x, wgu_buf[s].astype(BF16)) + bgu_ref[0]
        glu = jnp.minimum(hb[:, :D_FF], SWIGLU_LIMIT)
        lin = jnp.clip(hb[:, D_FF:], -SWIGLU_LIMIT, SWIGLU_LIMIT)
        act = glu * _sigmoid(SWIGLU_ALPHA * glu) * (lin + 1.0)
        _store_row_tiles(y_ref, rows, _dot(act.astype(BF16), wd_buf[s].astype(BF16)) + bd_ref[0])
        if rows < MOE_BLOCK:
            y_ref[rows * SUBLANES:, :] = jnp.zeros(((MOE_BLOCK - rows) * SUBLANES, LANES), F32)

    nv = nvalid_ref[i]
    for rows in range(MOE_QUARTER, MOE_BLOCK + 1, MOE_QUARTER):
        lo = rows - MOE_QUARTER if rows > MOE_QUARTER else -1
        pl.when(jnp.logical_and(used, jnp.logical_and(nv > lo, nv <= rows)))(functools.partial(expert, rows))

    @pl.when(i >= nu_ref[0])
    def _():
        y_ref[...] = jnp.zeros_like(y_ref)


def _moe_blocks(block_e, nused, padrow, xs, wgu, bgu, wd, bd):
    nblk = block_e.shape[0]
    pad_at = jnp.sum(jnp.where(block_e[:, None] == jnp.arange(N_EXPERTS)[None, :], padrow[None, :], 0), axis=1)
    nvalid = jnp.clip(pad_at - jnp.arange(nblk, dtype=jnp.int32) * MOE_BLOCK, 0, MOE_BLOCK).astype(jnp.int32)
    idx = jnp.arange(nblk, dtype=jnp.int32)
    first = (idx < nused[0]) & ((idx == 0) | (block_e != jnp.roll(block_e, 1)))
    slot = ((jnp.cumsum(first.astype(jnp.int32)) - 1) % 2).astype(jnp.int32)
    first_pos = jnp.where(first, idx, nblk)
    later = jnp.concatenate([first_pos[1:], jnp.full((1,), nblk, jnp.int32)])
    next_pos = lax.cummin(later, reverse=True)
    nxt = jnp.sum(jnp.where(idx[None, :] == next_pos[:, None], block_e[None, :] + 1, 0), axis=1) - 1
    grid_spec = pltpu.PrefetchScalarGridSpec(
        num_scalar_prefetch=6,
        grid=(nblk,),
        in_specs=[
            pl.BlockSpec((MOE_BLOCK * SUBLANES, LANES), lambda i, be, nu, *_: (jnp.minimum(i, jnp.maximum(nu[0] - 1, 0)), 0)),
            pl.BlockSpec((1, 1, 2 * D_FF), lambda i, be, *_: (be[i], 0, 0)),
            pl.BlockSpec((1, 1, D_MODEL), lambda i, be, *_: (be[i], 0, 0)),
            pl.BlockSpec(memory_space=pl.ANY),
            pl.BlockSpec(memory_space=pl.ANY),
        ],
        out_specs=pl.BlockSpec((MOE_BLOCK * SUBLANES, LANES), lambda i, be, *_: (i, 0)),
        scratch_shapes=[pltpu.VMEM((2, D_MODEL, 2 * D_FF), F32), pltpu.VMEM((2, D_FF, D_MODEL), F32),
                        pltpu.SemaphoreType.DMA((2, 2))],
    )
    return pl.pallas_call(
        _moe_kernel,
        grid_spec=grid_spec,
        out_shape=jax.ShapeDtypeStruct(xs.shape, F32),
        compiler_params=pltpu.CompilerParams(dimension_semantics=("arbitrary",), vmem_limit_bytes=MOE_VMEM_LIMIT),
        name="moe_blocks",
    )(block_e, nused, first.astype(jnp.int32), slot, nxt.astype(jnp.int32), nvalid, xs, bgu, bd, wgu, wd)


def _combine_kernel(npt, lp_ref, gate_ref, tabc_ref, tabn_ref, h_ref, ys_hbm, yp_ref, ysm_ref, srt, acc, sem):
    i = pl.program_id(0)
    nt = pl.num_programs(0)
    slot = i % 2

    def fetch_runs(tab_ref, s):
        def run(e, carry):
            local, count, first = tab_ref[0, 0, e], tab_ref[0, 1, e], tab_ref[0, 2, e]

            def piece(off, size):
                pltpu.make_async_copy(_row_tiles(ys_hbm, first + off, size),
                                      _row_tiles(srt.at[s], local + off, size), sem.at[s]).start()
            _for_run_pieces(count, piece)
            return carry
        lax.fori_loop(0, N_EXPERTS, run, 0)

    @pl.when(i == 0)
    def _():
        fetch_runs(tabc_ref, 0)

    def step(s):
        @pl.when(i + 1 < nt)
        def _():
            fetch_runs(tabn_ref, 1 - s)

        for _ in range(TOP_K):
            pltpu.make_async_copy(_row_tiles(ys_hbm, 0, TOK_TILE), _row_tiles(srt.at[s], 0, TOK_TILE),
                                  sem.at[s]).wait()

        def gather(j, carry):
            for u in range(SUBLANES):
                t = j * SUBLANES + u
                tot = None
                for k in range(TOP_K):
                    p = lp_ref[0, 0, t * TOP_K + k]
                    term = (srt[s, pl.ds(pl.multiple_of(p, SUBLANES), SUBLANES), :]
                            * gate_ref[0, 0, t * TOP_K + k])
                    tot = term if tot is None else tot + term
                acc[pl.ds(pl.multiple_of(t * SUBLANES, SUBLANES), SUBLANES), :] = tot
            return carry

        lax.fori_loop(0, TOK_TILE // SUBLANES, gather, 0)

    for s in range(2):
        pl.when(slot == s)(functools.partial(step, s))
    y = h_ref[...] + _load_row_tiles(acc, TOK_TILE)

    @pl.when(i < npt)
    def _():
        yp_ref[...] = y

    @pl.when(i >= npt)
    def _():
        ysm_ref[...] = y


def _combine(lp_tiles, gate_tiles, tab, h, ys, n_prompt_rows):
    T = h.shape[0]
    nt = T // TOK_TILE
    npt = n_prompt_rows // TOK_TILE
    per_assign = pl.BlockSpec((1, 1, TOK_TILE * TOP_K), lambda i: (i, 0, 0), memory_space=pltpu.SMEM)
    tab_blk = lambda imap: pl.BlockSpec((1, SUBLANES, LANES), imap, memory_space=pltpu.SMEM)
    return pl.pallas_call(
        functools.partial(_combine_kernel, npt),
        grid=(nt,),
        in_specs=[per_assign, per_assign,
                  tab_blk(lambda i: (i, 0, 0)), tab_blk(lambda i: (jnp.minimum(i + 1, nt - 1), 0, 0)),
                  pl.BlockSpec((TOK_TILE, D_MODEL), lambda i: (i, 0)),
                  pl.BlockSpec(memory_space=pl.ANY)],
        out_specs=(pl.BlockSpec((TOK_TILE, D_MODEL), lambda i: (jnp.minimum(i, npt - 1), 0)),
                   pl.BlockSpec((TOK_TILE, D_MODEL), lambda i: (jnp.maximum(i - npt, 0), 0))),
        out_shape=(jax.ShapeDtypeStruct((n_prompt_rows, D_MODEL), F32),
                   jax.ShapeDtypeStruct((T - n_prompt_rows, D_MODEL), F32)),
        scratch_shapes=[pltpu.VMEM((2, TOK_TILE * TOP_K * SUBLANES, LANES), F32),
                        pltpu.VMEM((TOK_TILE * SUBLANES, LANES), F32), pltpu.SemaphoreType.DMA((2,))],
        compiler_params=pltpu.CompilerParams(dimension_semantics=("arbitrary",), vmem_limit_bytes=VMEM_LIMIT),
        name="moe_combine",
    )(lp_tiles, gate_tiles, tab, tab, h, ys)


def kernel(x_prompt, x_sample, cache_k_win, cache_v_win, state_conv, state_C, state_n, state_m, g_attn, w_in, b_i,
           b_f, q_norm_g, k_norm_g, sinks, conv_w, conv_b, m_norm_g, w_out, g_ffn, w_router, b_router, w_gate_up,
           b_gate_up, w_down, b_down):
    depth = g_attn.shape[0]
    assert depth == 1
    B, S, _ = x_prompt.shape
    DB, DS, _ = x_sample.shape
    TP = B * S
    TS = DB * DS
    T = TP + TS
    assert T % TOK_TILE == 0 and TP % TOK_TILE == 0 and S % ATT_QB == 0 and S % PROMPT_CHUNK == 0
    assert DS == SUBLANES and DB % SAMPLE_NB == 0 and (SAMPLE_NB * DS) == LANES
    l = 0

    xp = x_prompt.reshape(TP, D_MODEL)
    xs = x_sample.reshape(TS, D_MODEL)

    w_pad = jnp.pad(w_in[l], ((0, 0), (0, IN_PAD - w_in.shape[2]))).astype(BF16)
    wgt = jnp.transpose(w_in[l][:, GATE_COL:GATE_COL + 2 * M_HEADS]).astype(BF16)
    gi = jnp.arange(GROUP_CHUNK) // HEAD_DIM
    gmat = (gi[:, None] == gi[None, :]).astype(BF16)
    qg = jnp.tile(q_norm_g[l], A_HEADS).reshape(1, A_WIDTH)
    kg = jnp.tile(k_norm_g[l], A_KV_HEADS).reshape(1, KV_WIDTH)
    gbias = jnp.concatenate([b_i[l], b_f[l]])
    gbrow = jnp.pad(gbias, (0, LANES - 2 * M_HEADS)).reshape(1, LANES)
    gbcol = gbias.reshape(2 * M_HEADS, 1)
    mng = m_norm_g[l].reshape(1, M_WIDTH)
    cw = conv_w[l]
    cb = conv_b[l].reshape(1, 2 * M_WIDTH)

    proj_w = (g_attn[l].reshape(1, D_MODEL), w_pad, wgt, gmat, qg, kg)
    qn, kn, va, qkm, vm, om, gcol, gt = _inproj(xp, *proj_w)
    qn_s, kn_s, va_s, qkm_s, vm_s, om_s, gcol_s, gt_s = _inproj(xs, *proj_w)

    a_p = _attn_prompt(sinks[l], qn, kn, va, B, S)
    ck = cache_k_win[l].reshape(DB, WINDOW, KV_WIDTH)
    cv = cache_v_win[l].reshape(DB, WINDOW, KV_WIDTH)
    a_s, kwin_s, vwin_s = _attn_sample(sinks[l], qn_s, kn_s, va_s, ck, cv, DB, DS)

    zc = jnp.zeros((B, SUBLANES, 2 * M_WIDTH), F32)
    m_p, C_p, n_p, mm_p = _mlstm(
        qkm, vm, om, gcol, gt, zc,
        jnp.zeros((B, M_HEADS, M_HEAD_DIM, M_HEAD_DIM), F32), jnp.zeros((B, M_HEADS, M_HEAD_DIM), F32),
        jnp.full((B, 1, LANES), NEG, F32), cw, cb, gbrow, gbcol, mng,
        nseq=1, L=PROMPT_CHUNK, ngroups=B, nchunks=S // PROMPT_CHUNK)
    conv_s0 = jnp.pad(state_conv[l], ((0, 0), (SUBLANES - (CONV_W - 1), 0), (0, 0)))
    m0_s = jnp.pad(state_m[l], ((0, 0), (0, LANES - M_HEADS))).reshape(DB, 1, LANES)
    m_s, C_s, n_s, mm_s = _mlstm(
        qkm_s, vm_s, om_s, gcol_s, gt_s, conv_s0, state_C[l], state_n[l], m0_s, cw, cb, gbrow, gbcol, mng,
        nseq=SAMPLE_NB, L=DS, ngroups=DB // SAMPLE_NB, nchunks=1)

    h, hn, route, cnt = _outproj(a_p, a_s, m_p, m_s, xp, xs, w_out[l].astype(BF16), g_ffn[l].reshape(1, D_MODEL),
                                 jnp.transpose(w_router[l]).astype(BF16), b_router[l].reshape(N_EXPERTS, 1))

    nblk = T * TOP_K // MOE_BLOCK + N_EXPERTS
    lp, tab, blk, info = _route_tables(route, cnt, nblk)
    block_e = blk[0, :nblk]
    padrow = info[:, 0]
    npad = info[:, 1]
    nused = info[N_EXPERTS - 1:N_EXPERTS, 2]
    per_assign = lambda a: jnp.transpose(a[:, :TOP_K, :], (0, 2, 1)).reshape(a.shape[0], 1, TOK_TILE * TOP_K)
    lp_tiles = per_assign(lp)
    gate_tiles = per_assign(route)
    xrows = _dispatch(padrow, npad, nused, lp_tiles, tab, hn, nblk * MOE_BLOCK)
    yrows = _moe_blocks(block_e, nused, padrow, xrows,
                        w_gate_up[l], b_gate_up[l].reshape(N_EXPERTS, 1, 2 * D_FF),
                        w_down[l], b_down[l].reshape(N_EXPERTS, 1, D_MODEL))
    y_p, y_s = _combine(lp_tiles, gate_tiles, tab, h, yrows, TP)

    y_p = y_p.reshape(B, S, D_MODEL)
    y_s = y_s.reshape(DB, DS, D_MODEL)
    def seq_tail(rows, n):
        return jnp.stack([rows[(b + 1) * S - n:(b + 1) * S] for b in range(B)])

    kwin_p = seq_tail(kn, WINDOW).reshape(B, WINDOW, A_KV_HEADS, HEAD_DIM)
    vwin_p = seq_tail(va, WINDOW).reshape(B, WINDOW, A_KV_HEADS, HEAD_DIM)
    qkm_s = qkm_s.reshape(DB, DS, 2 * M_WIDTH)
    return (y_p, y_s,
            kwin_p[None], vwin_p[None], seq_tail(qkm, CONV_W - 1)[None],
            C_p[None], n_p[None], mm_p[:, 0, :M_HEADS][None],
            kwin_s.reshape(DB, WINDOW, A_KV_HEADS, HEAD_DIM)[None],
            vwin_s.reshape(DB, WINDOW, A_KV_HEADS, HEAD_DIM)[None],
            qkm_s[:, -(CONV_W - 1):][None],
            C_s[None], n_s[None], mm_s[:, 0, :M_HEADS][None])
```

```python
import functools

import jax
import jax.numpy as jnp
from jax import lax
from jax.experimental import pallas as pl
from jax.experimental.pallas import tpu as pltpu

F32 = jnp.float32
BF16 = jnp.bfloat16

D_MODEL = 1024
HEAD_DIM = 64
A_HEADS = 8
A_KV_HEADS = 2
A_GROUP = A_HEADS // A_KV_HEADS
A_WIDTH = A_HEADS * HEAD_DIM
KV_WIDTH = A_KV_HEADS * HEAD_DIM
WINDOW = 128
M_HEADS = 4
M_HEAD_DIM = 128
M_WIDTH = M_HEADS * M_HEAD_DIM
CONV_W = 4
N_EXPERTS = 32
TOP_K = 4
D_FF = D_MODEL
SWIGLU_LIMIT = 7.0
SWIGLU_ALPHA = 1.702
MOE_BLOCK = 512
MOE_QUARTER = 128
EPS = 1e-6
NEG = -1e30

LANES = 128
SUBLANES = 8
GATE_COL = A_WIDTH + 2 * KV_WIDTH + 4 * M_WIDTH
IN_PAD = GATE_COL + LANES
TOK_TILE = 512
PROJ_TILE = 1024
PROJ_VMEM_LIMIT = 58 * 1024 * 1024
GROUP_CHUNK = 256
ATT_QB = 128
ATT_SB = 128
SAMPLE_NB = 16
PROMPT_CHUNK = 256
VMEM_LIMIT = 48 * 1024 * 1024
MOE_VMEM_LIMIT = 56 * 1024 * 1024


def _dot(a, b):
    return jnp.dot(a, b, preferred_element_type=F32)


def _dot_nt(a, b):
    return lax.dot_general(a, b, (((1,), (1,)), ((), ())), preferred_element_type=F32)


def _dot_tn(a, b):
    return lax.dot_general(a, b, (((0,), (0,)), ((), ())), preferred_element_type=F32)


def _split3(x):
    hi = x.astype(BF16)
    r1 = x - hi.astype(F32)
    mid = r1.astype(BF16)
    lo = (r1 - mid.astype(F32)).astype(BF16)
    return hi, mid, lo


def _log_sigmoid(x):
    return jnp.minimum(x, 0.0) - jnp.log1p(jnp.exp(-jnp.abs(x)))


def _sigmoid(x):
    return 0.5 * jnp.tanh(0.5 * x) + 0.5


def _load_row_tiles(ref2, rows):
    return jnp.concatenate([ref2[pl.ds(s, rows, stride=SUBLANES), :] for s in range(SUBLANES)], axis=1)


def _store_row_tiles(ref2, rows, val):
    for s in range(SUBLANES):
        ref2[pl.ds(s, rows, stride=SUBLANES), :] = val[:, s * LANES:(s + 1) * LANES]


def _row_tiles(ref2, first, n):
    start = first * SUBLANES
    if not isinstance(start, int):
        start = pl.multiple_of(start, SUBLANES)
    return ref2.at[pl.ds(start, n * SUBLANES), :]


def _split_specs(n_prompt_tiles, rows, width):
    return (pl.BlockSpec((rows, width), lambda i, *_: (jnp.minimum(i, n_prompt_tiles - 1), 0)),
            pl.BlockSpec((rows, width), lambda i, *_: (jnp.maximum(i - n_prompt_tiles, 0), 0)))


def _inproj_kernel(x_ref, g_ref, w_ref, wgt_ref, gmat_ref, qg_ref, kg_ref,
                   qn_ref, kn_ref, va_ref, qkm_ref, vm_ref, om_ref, gcol_ref, gt_ref):
    x = x_ref[...]
    ms = jnp.mean(x * x, axis=-1, keepdims=True)
    xn = ((x * lax.rsqrt(ms + EPS)) * g_ref[...]).astype(BF16)

    def seg(lo, hi):
        return _dot(xn, w_ref[:, lo:hi])

    def head_norm(z, gmat, g):
        parts = _split3(z * z)
        w = min(GROUP_CHUNK, z.shape[1])
        ss = jnp.concatenate([sum(_dot(p[:, c:c + w], gmat) for p in parts) for c in range(0, z.shape[1], w)], axis=1)
        return (z * lax.rsqrt(ss * (1.0 / HEAD_DIM) + EPS)) * g

    o0 = A_WIDTH
    o1 = o0 + KV_WIDTH
    o2 = o1 + KV_WIDTH
    o3 = o2 + 2 * M_WIDTH
    o4 = o3 + M_WIDTH
    o5 = o4 + M_WIDTH
    qn_ref[...] = head_norm(seg(0, o0), gmat_ref[...], qg_ref[...])
    kn_ref[...] = head_norm(seg(o0, o1), gmat_ref[:KV_WIDTH, :KV_WIDTH], kg_ref[...])
    va_ref[...] = seg(o1, o2)
    qkm_ref[...] = seg(o2, o3)
    vm_ref[...] = seg(o3, o4)
    om_ref[...] = seg(o4, o5)
    gcol_ref[...] = seg(o5, o5 + LANES)
    gt_ref[0] = _dot_nt(wgt_ref[...], xn)


def _inproj(x, g_attn, w_pad, wgt, gmat, qg, kg):
    T = x.shape[0]
    nt = T // PROJ_TILE
    row = lambda w: pl.BlockSpec((PROJ_TILE, w), lambda i: (i, 0))
    full = lambda a: pl.BlockSpec(a.shape, lambda i: (0,) * a.ndim, pipeline_mode=pl.Buffered(1))
    out_shape = (
        jax.ShapeDtypeStruct((T, A_WIDTH), F32),
        jax.ShapeDtypeStruct((T, KV_WIDTH), F32),
        jax.ShapeDtypeStruct((T, KV_WIDTH), F32),
        jax.ShapeDtypeStruct((T, 2 * M_WIDTH), F32),
        jax.ShapeDtypeStruct((T, M_WIDTH), F32),
        jax.ShapeDtypeStruct((T, M_WIDTH), F32),
        jax.ShapeDtypeStruct((T, LANES), F32),
        jax.ShapeDtypeStruct((nt, SUBLANES, PROJ_TILE), F32),
    )
    out_specs = (row(A_WIDTH), row(KV_WIDTH), row(KV_WIDTH), row(2 * M_WIDTH), row(M_WIDTH), row(M_WIDTH),
                 row(LANES), pl.BlockSpec((1, SUBLANES, PROJ_TILE), lambda i: (i, 0, 0)))
    return pl.pallas_call(
        _inproj_kernel,
        grid=(nt,),
        in_specs=[row(D_MODEL), full(g_attn), full(w_pad), full(wgt), full(gmat), full(qg), full(kg)],
        out_specs=out_specs,
        out_shape=out_shape,
        compiler_params=pltpu.CompilerParams(dimension_semantics=("parallel",), vmem_limit_bytes=PROJ_VMEM_LIMIT),
        name="inproj",
    )(x, g_attn, w_pad, wgt, gmat, qg, kg)


def _softmax_sink(pieces, masks, sink_col):
    masked = [jnp.where(mk, s, NEG) for s, mk in zip(pieces, masks)]
    m = sink_col
    for s in masked:
        m = jnp.maximum(m, jnp.max(s, axis=-1, keepdims=True))
    ps = [jnp.exp(s - m) for s in masked]
    den = jnp.exp(sink_col - m)
    for p in ps:
        den = den + jnp.sum(p, axis=-1, keepdims=True)
    return ps, 1.0 / den


def _stack_heads(q, g):
    return jnp.concatenate([q[:, (A_GROUP * g + i) * HEAD_DIM:(A_GROUP * g + i + 1) * HEAD_DIM]
                            for i in range(A_GROUP)], axis=0)


def _sink_col(sink_ref, g, rows_per_head):
    r = lax.broadcasted_iota(jnp.int32, (A_GROUP * rows_per_head, 1), 0)
    col = jnp.zeros((A_GROUP * rows_per_head, 1), F32)
    for i in range(A_GROUP):
        col = jnp.where(r // rows_per_head == i, sink_ref[A_GROUP * g + i], col)
    return col


def _attn_prompt_kernel(sink_ref, q_ref, kp_ref, kc_ref, vp_ref, vc_ref, o_ref):
    j = pl.program_id(1)
    scale = HEAD_DIM ** -0.5
    kall = jnp.concatenate([kp_ref[...], kc_ref[...]], axis=0).astype(BF16)
    vall = jnp.concatenate([vp_ref[...], vc_ref[...]], axis=0).astype(BF16)
    nrow = A_GROUP * ATT_SB
    r = lax.broadcasted_iota(jnp.int32, (nrow, 2 * ATT_SB), 0) % ATT_SB
    c = lax.broadcasted_iota(jnp.int32, (nrow, 2 * ATT_SB), 1)
    band = jnp.logical_and(c >= r, c <= r + WINDOW)
    band0 = jnp.logical_and(band, jnp.logical_or(c >= ATT_SB, j > 0))
    for sb in range(ATT_QB // ATT_SB):
        q = (q_ref[sb * ATT_SB:(sb + 1) * ATT_SB, :] * scale).astype(BF16)
        kwin = kall[sb * ATT_SB:(sb + 2) * ATT_SB]
        vwin = vall[sb * ATT_SB:(sb + 2) * ATT_SB]
        outs = []
        for g in range(A_KV_HEADS):
            lo, hi = g * HEAD_DIM, (g + 1) * HEAD_DIM
            s = _dot_nt(_stack_heads(q, g), kwin[:, lo:hi])
            (p,), inv = _softmax_sink([s], [band0 if sb == 0 else band], _sink_col(sink_ref, g, ATT_SB))
            o = _dot(p.astype(BF16), vwin[:, lo:hi]) * inv
            outs += [o[i * ATT_SB:(i + 1) * ATT_SB] for i in range(A_GROUP)]
        o_ref[sb * ATT_SB:(sb + 1) * ATT_SB, :] = jnp.concatenate(outs, axis=1)


def _attn_prompt(sinks, qn, kn, va, batch, seq):
    nq = seq // ATT_QB
    ratio = ATT_QB // ATT_SB
    cur = lambda w: pl.BlockSpec((ATT_QB, w), lambda b, j: (b * nq + j, 0))
    prev = lambda w: pl.BlockSpec((ATT_SB, w), lambda b, j: (jnp.maximum((b * nq + j) * ratio - 1, 0), 0))
    return pl.pallas_call(
        _attn_prompt_kernel,
        grid=(batch, nq),
        in_specs=[pl.BlockSpec(memory_space=pltpu.SMEM), cur(A_WIDTH), prev(KV_WIDTH), cur(KV_WIDTH),
                  prev(KV_WIDTH), cur(KV_WIDTH)],
        out_specs=cur(A_WIDTH),
        out_shape=jax.ShapeDtypeStruct((batch * seq, A_WIDTH), F32),
        compiler_params=pltpu.CompilerParams(dimension_semantics=("parallel", "parallel"),
                                             vmem_limit_bytes=VMEM_LIMIT),
        name="attn_prompt",
    )(sinks, qn, kn, kn, va, va)


def _attn_sample_kernel(dec, sink_ref, q_ref, kn_ref, vn_ref, ck_ref, cv_ref, o_ref, kw_ref, vw_ref):
    scale = HEAD_DIM ** -0.5
    rows = SAMPLE_NB * dec
    knew = kn_ref[...]
    vnew = vn_ref[...]
    knew_b = knew.astype(BF16)
    vnew_b = vnew.astype(BF16)
    nrow = A_GROUP * dec
    t = lax.broadcasted_iota(jnp.int32, (nrow, WINDOW), 0) % dec
    c = lax.broadcasted_iota(jnp.int32, (nrow, WINDOW), 1)
    m_cache = c >= t
    cn = lax.broadcasted_iota(jnp.int32, (nrow, rows), 1)
    tn = lax.broadcasted_iota(jnp.int32, (nrow, rows), 0) % dec
    for i in range(SAMPLE_NB):
        q = (q_ref[i * dec:(i + 1) * dec, :] * scale).astype(BF16)
        ck = ck_ref[i].astype(BF16)
        cv = cv_ref[i].astype(BF16)
        m_new = jnp.logical_and(cn // dec == i, cn % dec <= tn)
        outs = []
        for g in range(A_KV_HEADS):
            lo, hi = g * HEAD_DIM, (g + 1) * HEAD_DIM
            qs = _stack_heads(q, g)
            s_c = _dot_nt(qs, ck[:, lo:hi])
            s_n = _dot_nt(qs, knew_b[:, lo:hi])
            (p_c, p_n), inv = _softmax_sink([s_c, s_n], [m_cache, m_new], _sink_col(sink_ref, g, dec))
            o = (_dot(p_c.astype(BF16), cv[:, lo:hi]) + _dot(p_n.astype(BF16), vnew_b[:, lo:hi])) * inv
            outs += [o[h * dec:(h + 1) * dec] for h in range(A_GROUP)]
        o_ref[i * dec:(i + 1) * dec, :] = jnp.concatenate(outs, axis=1)
        kw_ref[i, 0:WINDOW - dec, :] = ck_ref[i, dec:WINDOW, :]
        kw_ref[i, WINDOW - dec:WINDOW, :] = knew[i * dec:(i + 1) * dec]
        vw_ref[i, 0:WINDOW - dec, :] = cv_ref[i, dec:WINDOW, :]
        vw_ref[i, WINDOW - dec:WINDOW, :] = vnew[i * dec:(i + 1) * dec]


def _attn_sample(sinks, qn, kn, va, ck, cv, dbatch, dec):
    rows = SAMPLE_NB * dec
    tokrow = lambda w: pl.BlockSpec((rows, w), lambda i: (i, 0))
    cache = pl.BlockSpec((SAMPLE_NB, WINDOW, KV_WIDTH), lambda i: (i, 0, 0))
    return pl.pallas_call(
        functools.partial(_attn_sample_kernel, dec),
        grid=(dbatch // SAMPLE_NB,),
        in_specs=[pl.BlockSpec(memory_space=pltpu.SMEM), tokrow(A_WIDTH), tokrow(KV_WIDTH), tokrow(KV_WIDTH),
                  cache, cache],
        out_specs=(pl.BlockSpec((rows, A_WIDTH), lambda i: (i, 0)), cache, cache),
        out_shape=(jax.ShapeDtypeStruct((dbatch * dec, A_WIDTH), F32),
                   jax.ShapeDtypeStruct((dbatch, WINDOW, KV_WIDTH), F32),
                   jax.ShapeDtypeStruct((dbatch, WINDOW, KV_WIDTH), F32)),
        compiler_params=pltpu.CompilerParams(dimension_semantics=("parallel",), vmem_limit_bytes=VMEM_LIMIT),
        name="attn_sample",
    )(sinks, qn, kn, va, ck, cv)


def _mlstm_kernel(nseq, L, qk_ref, v_ref, o_ref, gcol_ref, gt_ref, conv0_ref, c0_ref, n0_ref, m0_ref,
                  cw_ref, cb_ref, gbrow_ref, gbcol_ref, mng_ref, mask_ref,
                  out_ref, cst_ref, nst_ref, mst_ref, prev_ref):
    R = nseq * L
    ci = pl.program_id(1)

    @pl.when(ci == 0)
    def _():
        cst_ref[...] = c0_ref[...]
        nst_ref[...] = n0_ref[...]
        mst_ref[...] = m0_ref[...]
        prev_ref[...] = conv0_ref[0]

    raw = qk_ref[...]
    row = lax.broadcasted_iota(jnp.int32, (R, 1), 0)
    tpos = row % L
    rseq = row // L
    acc = raw * cw_ref[CONV_W - 1:CONV_W, :] + cb_ref[...]
    if nseq == 1:
        prev8 = prev_ref[...]
        t8 = lax.broadcasted_iota(jnp.int32, (SUBLANES, 1), 0)
    else:
        prevsrc = conv0_ref[...].reshape(R, 2 * M_WIDTH)
    for k in range(1, CONV_W):
        rolled = pltpu.roll(raw, k, 0)
        if nseq == 1:
            head = jnp.where(t8 >= k, rolled[0:SUBLANES], pltpu.roll(prev8, k, 0))
            sh = jnp.concatenate([head, rolled[SUBLANES:]], axis=0)
        else:
            sh = jnp.where(tpos >= k, rolled, pltpu.roll(prevsrc, R - SUBLANES + k, 0))
        acc = acc + sh * cw_ref[CONV_W - 1 - k:CONV_W - k, :]
    if nseq == 1:
        prev_ref[...] = raw[R - SUBLANES:R]
    qkc = acc * _sigmoid(acc)

    gc = gcol_ref[...] + gbrow_ref[...]
    gr = gt_ref[0] + gbcol_ref[...]
    lsc = _log_sigmoid(gc)
    lsr = _log_sigmoid(gr)
    mb = mask_ref[...]
    maskb = mb > 0
    bcol = sum(_dot(mb, p) for p in _split3(lsc))
    brow = sum(_dot_nt(p, mb) for p in _split3(lsr))

    lane = lax.broadcasted_iota(jnp.int32, (1, LANES), 1)
    m_new = [jnp.zeros((1, LANES), F32) for _ in range(nseq)]
    for h in range(M_HEADS):
        sl = slice(h * M_HEAD_DIM, (h + 1) * M_HEAD_DIM)
        qh = qkc[:, sl]
        kh = qkc[:, M_WIDTH + h * M_HEAD_DIM:M_WIDTH + (h + 1) * M_HEAD_DIM] * (M_HEAD_DIM ** -0.5)
        vh = v_ref[:, sl]
        qb, kb, vb = qh.astype(BF16), kh.astype(BF16), vh.astype(BF16)
        ig_c = gc[:, h:h + 1]
        b_c = bcol[:, M_HEADS + h:M_HEADS + h + 1]
        ig_r = gr[h:h + 1, :]
        b_r = brow[M_HEADS + h:M_HEADS + h + 1, :]
        if nseq == 1:
            m0c = mst_ref[0][:, h:h + 1]
            n0rows = nst_ref[0, h:h + 1, :]
        else:
            m0c = jnp.zeros((R, 1), F32)
            n0rows = jnp.zeros((R, M_HEAD_DIM), F32)
            for s in range(nseq):
                m0c = jnp.where(rseq == s, mst_ref[s][:, h:h + 1], m0c)
                n0rows = jnp.where(rseq == s, nst_ref[s, h:h + 1, :], n0rows)
        dm = jnp.where(maskb, b_c - b_r + ig_r, NEG)
        a_c = b_c + m0c
        m_c = jnp.maximum(a_c, jnp.max(dm, axis=-1, keepdims=True))
        w = jnp.exp(dm - m_c)
        sc = jnp.exp(a_c - m_c)
        wqk = w * _dot_nt(qb, kb)
        if nseq == 1:
            inter = _dot_nt(qb, cst_ref[0, h].astype(BF16))
        else:
            inter = jnp.zeros((R, M_HEAD_DIM), F32)
            for s in range(nseq):
                qs = jnp.where(rseq == s, qh, 0.0).astype(BF16)
                inter = inter + _dot_nt(qs, cst_ref[s, h].astype(BF16))
        num = _dot(wqk.astype(BF16), vb) + sc * inter
        den = jnp.sum(wqk, axis=-1, keepdims=True) + sc * jnp.sum(qh * n0rows, axis=-1, keepdims=True)
        hh = num / jnp.maximum(jnp.abs(den), jnp.exp(-m_c))

        for s in range(nseq):
            e = s * L + L - 1
            m_end = m_c[e:e + 1, :]
            wend = jnp.exp(b_c[e:e + 1, :] - b_c + ig_c - m_end)
            if nseq > 1:
                wend = jnp.where(rseq == s, wend, 0.0)
            sce = jnp.exp(a_c[e:e + 1, :] - m_end)
            c_new = sce * cst_ref[s, h] + _dot_tn((vh * wend).astype(BF16), kb)
            n_new = sce * nst_ref[s, h:h + 1, :] + jnp.sum(wend * kh, axis=0, keepdims=True)
            cst_ref[s, h] = c_new
            nst_ref[s, h:h + 1, :] = n_new
            m_new[s] = jnp.where(lane == h, m_end, m_new[s])

        hn = (hh * lax.rsqrt(jnp.mean(hh * hh, axis=-1, keepdims=True) + EPS)) * mng_ref[:, sl]
        out_ref[:, sl] = _sigmoid(o_ref[:, sl]) * hn
    for s in range(nseq):
        mst_ref[s] = m_new[s]


def _mlstm(qkm, vm, om, gcol, gt, conv0, c0, n0, m0, cw, cb, gbrow, gbcol, mng, nseq, L, ngroups, nchunks):
    R = nseq * L
    per_tile = gt.shape[2] // R
    tok = lambda w: pl.BlockSpec((R, w), lambda g, c: (g * nchunks + c, 0))
    gt_spec = pl.BlockSpec((1, SUBLANES, R),
                           lambda g, c: ((g * nchunks + c) // per_tile, 0, (g * nchunks + c) % per_tile))
    full = lambda a: pl.BlockSpec(a.shape, lambda g, c: (0,) * a.ndim)
    st4 = pl.BlockSpec((nseq, M_HEADS, M_HEAD_DIM, M_HEAD_DIM), lambda g, c: (g, 0, 0, 0))
    st3 = pl.BlockSpec((nseq, M_HEADS, M_HEAD_DIM), lambda g, c: (g, 0, 0))
    stm = pl.BlockSpec((nseq, 1, LANES), lambda g, c: (g, 0, 0))
    conv_spec = pl.BlockSpec((nseq, SUBLANES, 2 * M_WIDTH), lambda g, c: (g, 0, 0))
    r = jnp.arange(R)
    mask = ((r[:, None] // L == r[None, :] // L) & (r[None, :] <= r[:, None])).astype(BF16)
    nstate = ngroups * nseq
    return pl.pallas_call(
        functools.partial(_mlstm_kernel, nseq, L),
        grid=(ngroups, nchunks),
        in_specs=[tok(2 * M_WIDTH), tok(M_WIDTH), tok(M_WIDTH), tok(LANES), gt_spec, conv_spec, st4, st3, stm,
                  full(cw), full(cb), full(gbrow), full(gbcol), full(mng), full(mask)],
        out_specs=(pl.BlockSpec((R, M_WIDTH), lambda g, c: (g * nchunks + c, 0)), st4, st3, stm),
        out_shape=(jax.ShapeDtypeStruct((ngroups * nchunks * R, M_WIDTH), F32),
                   jax.ShapeDtypeStruct((nstate, M_HEADS, M_HEAD_DIM, M_HEAD_DIM), F32),
                   jax.ShapeDtypeStruct((nstate, M_HEADS, M_HEAD_DIM), F32),
                   jax.ShapeDtypeStruct((nstate, 1, LANES), F32)),
        scratch_shapes=[pltpu.VMEM((SUBLANES, 2 * M_WIDTH), F32)],
        compiler_params=pltpu.CompilerParams(dimension_semantics=("parallel", "arbitrary"),
                                             vmem_limit_bytes=VMEM_LIMIT),
        name="mlstm_n%d" % nseq,
    )(qkm, vm, om, gcol, gt, conv0, c0, n0, m0, cw, cb, gbrow, gbcol, mng, mask)


def _outproj_kernel(npt, ap_ref, as_ref, mp_ref, ms_ref, xp_ref, xs_ref, wo_ref, g_ref, wrt_ref, brc_ref,
                    h_ref, hn_ref, route_ref, cnt_ref):
    def project(a_ref, m_ref, x_ref):
        h_ref[...] = (x_ref[...] + _dot(a_ref[...].astype(BF16), wo_ref[0:A_WIDTH, :])
                      + _dot(m_ref[...].astype(BF16), wo_ref[A_WIDTH:A_WIDTH + M_WIDTH, :]))

    @pl.when(pl.program_id(0) < npt)
    def _():
        project(ap_ref, mp_ref, xp_ref)

    @pl.when(pl.program_id(0) >= npt)
    def _():
        project(as_ref, ms_ref, xs_ref)

    h = h_ref[...]
    hn = (h * lax.rsqrt(jnp.mean(h * h, axis=-1, keepdims=True) + EPS)) * g_ref[...]
    _store_row_tiles(hn_ref, TOK_TILE, hn)
    logits = _dot_nt(wrt_ref[...], hn.astype(BF16)) + brc_ref[...]
    eidx = lax.broadcasted_iota(jnp.int32, logits.shape, 0).astype(F32)
    picked = jnp.zeros(logits.shape, F32)
    top0 = None
    den = None
    es = []
    ids = []
    for k in range(TOP_K):
        mx = jnp.max(logits, axis=0, keepdims=True)
        idx = jnp.min(jnp.where(logits == mx, eidx, float(N_EXPERTS)), axis=0, keepdims=True)
        if k == 0:
            top0 = mx
        e = jnp.exp(mx - top0)
        den = e if den is None else den + e
        es.append(e)
        ids.append(idx)
        hit = eidx == idx
        picked = jnp.where(hit, 1.0, picked)
        logits = jnp.where(hit, -jnp.inf, logits)
    route_ref[0] = jnp.concatenate([e / den for e in es] + ids, axis=0)

    @pl.when(pl.program_id(0) == 0)
    def _():
        cnt_ref[...] = jnp.zeros_like(cnt_ref)

    cnt_ref[...] += jnp.broadcast_to(jnp.sum(picked, axis=1, keepdims=True), cnt_ref.shape)


def _outproj(a_p, a_s, m_p, m_s, xp, xs, w_out, g_ffn, wr_t, br_col):
    T = xp.shape[0] + xs.shape[0]
    nt = T // TOK_TILE
    npt = xp.shape[0] // TOK_TILE
    row = lambda w: pl.BlockSpec((TOK_TILE, w), lambda i: (i, 0))
    full = lambda a: pl.BlockSpec(a.shape, lambda i: (0,) * a.ndim)
    return pl.pallas_call(
        functools.partial(_outproj_kernel, npt),
        grid=(nt,),
        in_specs=[*_split_specs(npt, TOK_TILE, A_WIDTH), *_split_specs(npt, TOK_TILE, M_WIDTH),
                  *_split_specs(npt, TOK_TILE, D_MODEL), full(w_out), full(g_ffn), full(wr_t), full(br_col)],
        out_specs=(row(D_MODEL), pl.BlockSpec((TOK_TILE * SUBLANES, LANES), lambda i: (i, 0)),
                   pl.BlockSpec((1, 2 * TOP_K, TOK_TILE), lambda i: (i, 0, 0)),
                   pl.BlockSpec((N_EXPERTS, LANES), lambda i: (0, 0))),
        out_shape=(jax.ShapeDtypeStruct((T, D_MODEL), F32), jax.ShapeDtypeStruct((T * SUBLANES, LANES), F32),
                   jax.ShapeDtypeStruct((nt, 2 * TOP_K, TOK_TILE), F32),
                   jax.ShapeDtypeStruct((N_EXPERTS, LANES), F32)),
        compiler_params=pltpu.CompilerParams(dimension_semantics=("arbitrary",), vmem_limit_bytes=VMEM_LIMIT),
        name="outproj_router",
    )(a_p, a_s, m_p, m_s, xp, xs, w_out, g_ffn, wr_t, br_col)


def _route_kernel(nblk_pad, route_ref, cnt_ref, ustrict_ref, lstrict_ref, lp_ref, tab_ref, blk_ref, info_ref,
                  carry_ref):
    i = pl.program_id(0)
    cnt = cnt_ref[...]
    nb_e = jnp.floor((cnt + (MOE_BLOCK - 1.0)) * (1.0 / MOE_BLOCK))
    bstart = sum(_dot(lstrict_ref[...], p) for p in _split3(nb_e))
    bend = bstart + nb_e
    row_start = bstart * float(MOE_BLOCK)

    @pl.when(i == 0)
    def _():
        carry_ref[...] = jnp.zeros_like(carry_ref)
        bi = lax.broadcasted_iota(jnp.int32, (N_EXPERTS, nblk_pad), 1).astype(F32)
        done = jnp.where(bend[:, 0:1] <= bi, 1.0, 0.0)
        be = jnp.minimum(jnp.sum(done, axis=0, keepdims=True), N_EXPERTS - 1.0)
        blk_ref[...] = jnp.broadcast_to(be, blk_ref.shape).astype(jnp.int32)
        lane = lax.broadcasted_iota(jnp.int32, (N_EXPERTS, LANES), 1)
        info = jnp.where(lane == 0, row_start + cnt, 0.0)
        info = jnp.where(lane == 1, nb_e * float(MOE_BLOCK) - cnt, info)
        info = jnp.where(lane == 2, bend, info)
        info_ref[...] = info.astype(jnp.int32)

    r = route_ref[0]
    eidx = lax.broadcasted_iota(jnp.int32, (N_EXPERTS, TOK_TILE), 0).astype(F32)
    sel = [eidx == r[TOP_K + k:TOP_K + k + 1, :] for k in range(TOP_K)]
    oh = jnp.zeros((N_EXPERTS, TOK_TILE), F32)
    for k in range(TOP_K):
        oh = jnp.where(sel[k], 1.0, oh)
    cnt_t = jnp.broadcast_to(jnp.sum(oh, axis=1, keepdims=True), (N_EXPERTS, LANES))
    seg = sum(_dot(lstrict_ref[...], p) for p in _split3(cnt_t))
    local = _dot(oh.astype(BF16), ustrict_ref[...]) + seg[:, 0:1]
    rows = [jnp.sum(jnp.where(sel[k], local, 0.0), axis=0, keepdims=True) for k in range(TOP_K)]
    lp_ref[0] = (jnp.concatenate(rows + [jnp.zeros((TOP_K, TOK_TILE), F32)], axis=0)
                 * float(SUBLANES)).astype(jnp.int32)
    diag = (lax.broadcasted_iota(jnp.int32, (N_EXPERTS, LANES), 0)
            == lax.broadcasted_iota(jnp.int32, (N_EXPERTS, LANES), 1))
    to_lanes = lambda col: jnp.sum(jnp.where(diag, col, 0.0), axis=0, keepdims=True)
    tab = [to_lanes(seg), to_lanes(cnt_t), to_lanes(row_start + carry_ref[...])]
    tab_ref[0] = jnp.concatenate(tab + [jnp.zeros((SUBLANES - len(tab), LANES), F32)], axis=0).astype(jnp.int32)
    carry_ref[...] += cnt_t


def _route_tables(route, cnt, nblk):
    nt = route.shape[0]
    nblk_pad = -(-nblk // LANES) * LANES
    a = jnp.arange(TOK_TILE)
    ustrict = (a[:, None] < a[None, :]).astype(BF16)
    b = jnp.arange(N_EXPERTS)
    lstrict = (b[:, None] > b[None, :]).astype(BF16)
    full = lambda x: pl.BlockSpec(x.shape, lambda i: (0,) * x.ndim)
    tile = pl.BlockSpec((1, 2 * TOP_K, TOK_TILE), lambda i: (i, 0, 0))
    return pl.pallas_call(
        functools.partial(_route_kernel, nblk_pad),
        grid=(nt,),
        in_specs=[tile, full(cnt), full(ustrict), full(lstrict)],
        out_specs=(tile, pl.BlockSpec((1, SUBLANES, LANES), lambda i: (i, 0, 0)),
                   pl.BlockSpec((SUBLANES, nblk_pad), lambda i: (0, 0)),
                   pl.BlockSpec((N_EXPERTS, LANES), lambda i: (0, 0))),
        out_shape=(jax.ShapeDtypeStruct((nt, 2 * TOP_K, TOK_TILE), jnp.int32),
                   jax.ShapeDtypeStruct((nt, SUBLANES, LANES), jnp.int32),
                   jax.ShapeDtypeStruct((SUBLANES, nblk_pad), jnp.int32),
                   jax.ShapeDtypeStruct((N_EXPERTS, LANES), jnp.int32)),
        scratch_shapes=[pltpu.VMEM((N_EXPERTS, LANES), F32)],
        compiler_params=pltpu.CompilerParams(dimension_semantics=("arbitrary",), vmem_limit_bytes=VMEM_LIMIT),
        name="route_tables",
    )(route, cnt, ustrict, lstrict)


RUN_PIECES = tuple(1 << b for b in range(9, -1, -1))


def _for_run_pieces(count, fn):
    for size in RUN_PIECES:
        @pl.when((count & size) != 0)
        def _(size=size):
            fn(count & ~(2 * size - 1), size)


def _dispatch_kernel(padrow_ref, npad_ref, nu_ref, lp_ref, tab_ref, hn_ref, xs_hbm, srt, zbuf, sem, zsem):
    i = pl.program_id(0)
    nt = pl.num_programs(0)
    slot = i % 2

    @pl.when(i == 0)
    def _():
        zbuf[...] = jnp.zeros_like(zbuf)
        nblk = xs_hbm.shape[0] // (MOE_BLOCK * SUBLANES)

        def tail_start(b, c):
            pltpu.make_async_copy(zbuf, _row_tiles(xs_hbm, b * MOE_BLOCK, MOE_BLOCK), zsem).start()
            return c

        def tail_wait(b, c):
            pltpu.make_async_copy(zbuf, _row_tiles(xs_hbm, b * MOE_BLOCK, MOE_BLOCK), zsem).wait()
            return c

        lax.fori_loop(nu_ref[0], nblk, tail_start, 0)
        lax.fori_loop(nu_ref[0], nblk, tail_wait, 0)

        def pad_runs(wait):
            def per_expert(e, carry):
                first, count = padrow_ref[e], npad_ref[e]

                def piece(off, size):
                    cp = pltpu.make_async_copy(_row_tiles(zbuf, 0, size), _row_tiles(xs_hbm, first + off, size), zsem)
                    cp.wait() if wait else cp.start()
                _for_run_pieces(count, piece)
                return carry
            lax.fori_loop(0, N_EXPERTS, per_expert, 0)

        pad_runs(False)
        pad_runs(True)

    def wait_runs(s):
        for _ in range(TOP_K):
            pltpu.make_async_copy(_row_tiles(srt.at[s], 0, TOK_TILE), _row_tiles(xs_hbm, 0, TOK_TILE), sem.at[s]).wait()

    def step(s):
        @pl.when(i >= 2)
        def _():
            wait_runs(s)

        def permute(j, carry):
            for u in range(SUBLANES):
                t = j * SUBLANES + u
                row = hn_ref[pl.ds(pl.multiple_of(t * SUBLANES, SUBLANES), SUBLANES), :]
                for k in range(TOP_K):
                    p = lp_ref[0, 0, t * TOP_K + k]
                    srt[s, pl.ds(pl.multiple_of(p, SUBLANES), SUBLANES), :] = row
            return carry

        lax.fori_loop(0, TOK_TILE // SUBLANES, permute, 0)

        def send_run(e, carry):
            local, count, first = tab_ref[0, 0, e], tab_ref[0, 1, e], tab_ref[0, 2, e]

            def piece(off, size):
                pltpu.make_async_copy(_row_tiles(srt.at[s], local + off, size),
                                      _row_tiles(xs_hbm, first + off, size), sem.at[s]).start()
            _for_run_pieces(count, piece)
            return carry

        lax.fori_loop(0, N_EXPERTS, send_run, 0)

        @pl.when(i == nt - 1)
        def _():
            @pl.when(nt >= 2)
            def _():
                wait_runs(1 - s)

            wait_runs(s)

    for s in range(2):
        pl.when(slot == s)(functools.partial(step, s))


def _dispatch(padrow, npad, nused, lp_tiles, tab, hn, n_rows):
    nt = lp_tiles.shape[0]
    grid_spec = pltpu.PrefetchScalarGridSpec(
        num_scalar_prefetch=3,
        grid=(nt,),
        in_specs=[pl.BlockSpec((1, 1, TOK_TILE * TOP_K), lambda i, *_: (i, 0, 0), memory_space=pltpu.SMEM),
                  pl.BlockSpec((1, SUBLANES, LANES), lambda i, *_: (i, 0, 0), memory_space=pltpu.SMEM),
                  pl.BlockSpec((TOK_TILE * SUBLANES, LANES), lambda i, *_: (i, 0))],
        out_specs=pl.BlockSpec(memory_space=pl.ANY),
        scratch_shapes=[pltpu.VMEM((2, TOK_TILE * TOP_K * SUBLANES, LANES), F32),
                        pltpu.VMEM((MOE_BLOCK * SUBLANES, LANES), F32),
                        pltpu.SemaphoreType.DMA((2,)), pltpu.SemaphoreType.DMA(())],
    )
    return pl.pallas_call(
        _dispatch_kernel,
        grid_spec=grid_spec,
        out_shape=jax.ShapeDtypeStruct((n_rows * SUBLANES, LANES), F32),
        compiler_params=pltpu.CompilerParams(dimension_semantics=("arbitrary",), vmem_limit_bytes=VMEM_LIMIT),
        name="moe_dispatch",
    )(padrow, npad, nused, lp_tiles, tab, hn)


def _moe_kernel(be_ref, nu_ref, first_ref, slot_ref, nxt_ref, nvalid_ref, x_ref, bgu_ref, bd_ref, wgu_hbm, wd_hbm, y_ref,
                wgu_buf, wd_buf, wsem):
    i = pl.program_id(0)
    used = i < nu_ref[0]
    s = slot_ref[i]

    def fetch(e, sl):
        return (pltpu.make_async_copy(wgu_hbm.at[e], wgu_buf.at[sl], wsem.at[0, sl]),
                pltpu.make_async_copy(wd_hbm.at[e], wd_buf.at[sl], wsem.at[1, sl]))

    @pl.when(jnp.logical_and(used, first_ref[i] == 1))
    def _():
        @pl.when(i == 0)
        def _():
            for c in fetch(be_ref[0], 0):
                c.start()

        for c in fetch(be_ref[i], s):
            c.wait()

        @pl.when(nxt_ref[i] >= 0)
        def _():
            for c in fetch(nxt_ref[i], 1 - s):
                c.start()

    def expert(rows):
        x = _load_row_tiles(x_ref, rows).astype(BF16)
        hb = _dot(x, wgu_buf[s].astype(BF16)) + bgu_ref[0]
        glu = jnp.minimum(hb[:, :D_FF], SWIGLU_LIMIT)
        lin = jnp.clip(hb[:, D_FF:], -SWIGLU_LIMIT, SWIGLU_LIMIT)
        act = glu * _sigmoid(SWIGLU_ALPHA * glu) * (lin + 1.0)
        _store_row_tiles(y_ref, rows, _dot(act.astype(BF16), wd_buf[s].astype(BF16)) + bd_ref[0])
        if rows < MOE_BLOCK:
            y_ref[rows * SUBLANES:, :] = jnp.zeros(((MOE_BLOCK - rows) * SUBLANES, LANES), F32)

    nv = nvalid_ref[i]
    for rows in range(MOE_QUARTER, MOE_BLOCK + 1, MOE_QUARTER):
        lo = rows - MOE_QUARTER if rows > MOE_QUARTER else -1
        pl.when(jnp.logical_and(used, jnp.logical_and(nv > lo, nv <= rows)))(functools.partial(expert, rows))

    @pl.when(i >= nu_ref[0])
    def _():
        y_ref[...] = jnp.zeros_like(y_ref)


def _moe_blocks(block_e, nused, padrow, xs, wgu, bgu, wd, bd):
    nblk = block_e.shape[0]
    pad_at = jnp.sum(jnp.where(block_e[:, None] == jnp.arange(N_EXPERTS)[None, :], padrow[None, :], 0), axis=1)
    nvalid = jnp.clip(pad_at - jnp.arange(nblk, dtype=jnp.int32) * MOE_BLOCK, 0, MOE_BLOCK).astype(jnp.int32)
    idx = jnp.arange(nblk, dtype=jnp.int32)
    first = (idx < nused[0]) & ((idx == 0) | (block_e != jnp.roll(block_e, 1)))
    slot = ((jnp.cumsum(first.astype(jnp.int32)) - 1) % 2).astype(jnp.int32)
    first_pos = jnp.where(first, idx, nblk)
    later = jnp.concatenate([first_pos[1:], jnp.full((1,), nblk, jnp.int32)])
    next_pos = lax.cummin(later, reverse=True)
    nxt = jnp.sum(jnp.where(idx[None, :] == next_pos[:, None], block_e[None, :] + 1, 0), axis=1) - 1
    grid_spec = pltpu.PrefetchScalarGridSpec(
        num_scalar_prefetch=6,
        grid=(nblk,),
        in_specs=[
            pl.BlockSpec((MOE_BLOCK * SUBLANES, LANES), lambda i, be, nu, *_: (jnp.minimum(i, jnp.maximum(nu[0] - 1, 0)), 0)),
            pl.BlockSpec((1, 1, 2 * D_FF), lambda i, be, *_: (be[i], 0, 0)),
            pl.BlockSpec((1, 1, D_MODEL), lambda i, be, *_: (be[i], 0, 0)),
            pl.BlockSpec(memory_space=pl.ANY),
            pl.BlockSpec(memory_space=pl.ANY),
        ],
        out_specs=pl.BlockSpec((MOE_BLOCK * SUBLANES, LANES), lambda i, be, *_: (i, 0)),
        scratch_shapes=[pltpu.VMEM((2, D_MODEL, 2 * D_FF), F32), pltpu.VMEM((2, D_FF, D_MODEL), F32),
                        pltpu.SemaphoreType.DMA((2, 2))],
    )
    return pl.pallas_call(
        _moe_kernel,
        grid_spec=grid_spec,
        out_shape=jax.ShapeDtypeStruct(xs.shape, F32),
        compiler_params=pltpu.CompilerParams(dimension_semantics=("arbitrary",), vmem_limit_bytes=MOE_VMEM_LIMIT),
        name="moe_blocks",
    )(block_e, nused, first.astype(jnp.int32), slot, nxt.astype(jnp.int32), nvalid, xs, bgu, bd, wgu, wd)


def _combine_kernel(npt, lp_ref, gate_ref, tabc_ref, tabn_ref, h_ref, ys_hbm, yp_ref, ysm_ref, srt, acc, sem):
    i = pl.program_id(0)
    nt = pl.num_programs(0)
    slot = i % 2

    def fetch_runs(tab_ref, s):
        def run(e, carry):
            local, count, first = tab_ref[0, 0, e], tab_ref[0, 1, e], tab_ref[0, 2, e]

            def piece(off, size):
                pltpu.make_async_copy(_row_tiles(ys_hbm, first + off, size),
                                      _row_tiles(srt.at[s], local + off, size), sem.at[s]).start()
            _for_run_pieces(count, piece)
            return carry
        lax.fori_loop(0, N_EXPERTS, run, 0)

    @pl.when(i == 0)
    def _():
        fetch_runs(tabc_ref, 0)

    def step(s):
        @pl.when(i + 1 < nt)
        def _():
            fetch_runs(tabn_ref, 1 - s)

        for _ in range(TOP_K):
            pltpu.make_async_copy(_row_tiles(ys_hbm, 0, TOK_TILE), _row_tiles(srt.at[s], 0, TOK_TILE),
                                  sem.at[s]).wait()

        def gather(j, carry):
            for u in range(SUBLANES):
                t = j * SUBLANES + u
                tot = None
                for k in range(TOP_K):
                    p = lp_ref[0, 0, t * TOP_K + k]
                    term = (srt[s, pl.ds(pl.multiple_of(p, SUBLANES), SUBLANES), :]
                            * gate_ref[0, 0, t * TOP_K + k])
                    tot = term if tot is None else tot + term
                acc[pl.ds(pl.multiple_of(t * SUBLANES, SUBLANES), SUBLANES), :] = tot
            return carry

        lax.fori_loop(0, TOK_TILE // SUBLANES, gather, 0)

    for s in range(2):
        pl.when(slot == s)(functools.partial(step, s))
    y = h_ref[...] + _load_row_tiles(acc, TOK_TILE)

    @pl.when(i < npt)
    def _():
        yp_ref[...] = y

    @pl.when(i >= npt)
    def _():
        ysm_ref[...] = y


def _combine(lp_tiles, gate_tiles, tab, h, ys, n_prompt_rows):
    T = h.shape[0]
    nt = T // TOK_TILE
    npt = n_prompt_rows // TOK_TILE
    per_assign = pl.BlockSpec((1, 1, TOK_TILE * TOP_K), lambda i: (i, 0, 0), memory_space=pltpu.SMEM)
    tab_blk = lambda imap: pl.BlockSpec((1, SUBLANES, LANES), imap, memory_space=pltpu.SMEM)
    return pl.pallas_call(
        functools.partial(_combine_kernel, npt),
        grid=(nt,),
        in_specs=[per_assign, per_assign,
                  tab_blk(lambda i: (i, 0, 0)), tab_blk(lambda i: (jnp.minimum(i + 1, nt - 1), 0, 0)),
                  pl.BlockSpec((TOK_TILE, D_MODEL), lambda i: (i, 0)),
                  pl.BlockSpec(memory_space=pl.ANY)],
        out_specs=(pl.BlockSpec((TOK_TILE, D_MODEL), lambda i: (jnp.minimum(i, npt - 1), 0)),
                   pl.BlockSpec((TOK_TILE, D_MODEL), lambda i: (jnp.maximum(i - npt, 0), 0))),
        out_shape=(jax.ShapeDtypeStruct((n_prompt_rows, D_MODEL), F32),
                   jax.ShapeDtypeStruct((T - n_prompt_rows, D_MODEL), F32)),
        scratch_shapes=[pltpu.VMEM((2, TOK_TILE * TOP_K * SUBLANES, LANES), F32),
                        pltpu.VMEM((TOK_TILE * SUBLANES, LANES), F32), pltpu.SemaphoreType.DMA((2,))],
        compiler_params=pltpu.CompilerParams(dimension_semantics=("arbitrary",), vmem_limit_bytes=VMEM_LIMIT),
        name="moe_combine",
    )(lp_tiles, gate_tiles, tab, tab, h, ys)


def kernel(x_prompt, x_sample, cache_k_win, cache_v_win, state_conv, state_C, state_n, state_m, g_attn, w_in, b_i,
           b_f, q_norm_g, k_norm_g, sinks, conv_w, conv_b, m_norm_g, w_out, g_ffn, w_router, b_router, w_gate_up,
           b_gate_up, w_down, b_down):
    depth = g_attn.shape[0]
    assert depth == 1
    B, S, _ = x_prompt.shape
    DB, DS, _ = x_sample.shape
    TP = B * S
    TS = DB * DS
    T = TP + TS
    assert T % TOK_TILE == 0 and TP % TOK_TILE == 0 and S % ATT_QB == 0 and S % PROMPT_CHUNK == 0
    assert DS == SUBLANES and DB % SAMPLE_NB == 0 and (SAMPLE_NB * DS) == LANES
    l = 0

    xp = x_prompt.reshape(TP, D_MODEL)
    xs = x_sample.reshape(TS, D_MODEL)

    w_pad = jnp.pad(w_in[l], ((0, 0), (0, IN_PAD - w_in.shape[2]))).astype(BF16)
    wgt = jnp.transpose(w_in[l][:, GATE_COL:GATE_COL + 2 * M_HEADS]).astype(BF16)
    gi = jnp.arange(GROUP_CHUNK) // HEAD_DIM
    gmat = (gi[:, None] == gi[None, :]).astype(BF16)
    qg = jnp.tile(q_norm_g[l], A_HEADS).reshape(1, A_WIDTH)
    kg = jnp.tile(k_norm_g[l], A_KV_HEADS).reshape(1, KV_WIDTH)
    gbias = jnp.concatenate([b_i[l], b_f[l]])
    gbrow = jnp.pad(gbias, (0, LANES - 2 * M_HEADS)).reshape(1, LANES)
    gbcol = gbias.reshape(2 * M_HEADS, 1)
    mng = m_norm_g[l].reshape(1, M_WIDTH)
    cw = conv_w[l]
    cb = conv_b[l].reshape(1, 2 * M_WIDTH)

    proj_w = (g_attn[l].reshape(1, D_MODEL), w_pad, wgt, gmat, qg, kg)
    qn, kn, va, qkm, vm, om, gcol, gt = _inproj(xp, *proj_w)
    qn_s, kn_s, va_s, qkm_s, vm_s, om_s, gcol_s, gt_s = _inproj(xs, *proj_w)

    a_p = _attn_prompt(sinks[l], qn, kn, va, B, S)
    ck = cache_k_win[l].reshape(DB, WINDOW, KV_WIDTH)
    cv = cache_v_win[l].reshape(DB, WINDOW, KV_WIDTH)
    a_s, kwin_s, vwin_s = _attn_sample(sinks[l], qn_s, kn_s, va_s, ck, cv, DB, DS)

    zc = jnp.zeros((B, SUBLANES, 2 * M_WIDTH), F32)
    m_p, C_p, n_p, mm_p = _mlstm(
        qkm, vm, om, gcol, gt, zc,
        jnp.zeros((B, M_HEADS, M_HEAD_DIM, M_HEAD_DIM), F32), jnp.zeros((B, M_HEADS, M_HEAD_DIM), F32),
        jnp.full((B, 1, LANES), NEG, F32), cw, cb, gbrow, gbcol, mng,
        nseq=1, L=PROMPT_CHUNK, ngroups=B, nchunks=S // PROMPT_CHUNK)
    conv_s0 = jnp.pad(state_conv[l], ((0, 0), (SUBLANES - (CONV_W - 1), 0), (0, 0)))
    m0_s = jnp.pad(state_m[l], ((0, 0), (0, LANES - M_HEADS))).reshape(DB, 1, LANES)
    m_s, C_s, n_s, mm_s = _mlstm(
        qkm_s, vm_s, om_s, gcol_s, gt_s, conv_s0, state_C[l], state_n[l], m0_s, cw, cb, gbrow, gbcol, mng,
        nseq=SAMPLE_NB, L=DS, ngroups=DB // SAMPLE_NB, nchunks=1)

    h, hn, route, cnt = _outproj(a_p, a_s, m_p, m_s, xp, xs, w_out[l].astype(BF16), g_ffn[l].reshape(1, D_MODEL),
                                 jnp.transpose(w_router[l]).astype(BF16), b_router[l].reshape(N_EXPERTS, 1))

    nblk = T * TOP_K // MOE_BLOCK + N_EXPERTS
    lp, tab, blk, info = _route_tables(route, cnt, nblk)
    block_e = blk[0, :nblk]
    padrow = info[:, 0]
    npad = info[:, 1]
    nused = info[N_EXPERTS - 1:N_EXPERTS, 2]
    per_assign = lambda a: jnp.transpose(a[:, :TOP_K, :], (0, 2, 1)).reshape(a.shape[0], 1, TOK_TILE * TOP_K)
    lp_tiles = per_assign(lp)
    gate_tiles = per_assign(route)
    xrows = _dispatch(padrow, npad, nused, lp_tiles, tab, hn, nblk * MOE_BLOCK)
    yrows = _moe_blocks(block_e, nused, padrow, xrows,
                        w_gate_up[l], b_gate_up[l].reshape(N_EXPERTS, 1, 2 * D_FF),
                        w_down[l], b_down[l].reshape(N_EXPERTS, 1, D_MODEL))
    y_p, y_s = _combine(lp_tiles, gate_tiles, tab, h, yrows, TP)

    y_p = y_p.reshape(B, S, D_MODEL)
    y_s = y_s.reshape(DB, DS, D_MODEL)
    def seq_tail(rows, n):
        return jnp.stack([rows[(b + 1) * S - n:(b + 1) * S] for b in range(B)])

    kwin_p = seq_tail(kn, WINDOW).reshape(B, WINDOW, A_KV_HEADS, HEAD_DIM)
    vwin_p = seq_tail(va, WINDOW).reshape(B, WINDOW, A_KV_HEADS, HEAD_DIM)
    qkm_s = qkm_s.reshape(DB, DS, 2 * M_WIDTH)
    return (y_p, y_s,
            kwin_p[None], vwin_p[None], seq_tail(qkm, CONV_W - 1)[None],
            C_p[None], n_p[None], mm_p[:, 0, :M_HEADS][None],
            kwin_s.reshape(DB, WINDOW, A_KV_HEADS, HEAD_DIM)[None],
            vwin_s.reshape(DB, WINDOW, A_KV_HEADS, HEAD_DIM)[None],
            qkm_s[:, -(CONV_W - 1):][None],
            C_s[None], n_s[None], mm_s[:, 0, :M_HEADS][None])
```

```python
import functools

import jax
import jax.numpy as jnp
from jax import lax
from jax.experimental import pallas as pl
from jax.experimental.pallas import tpu as pltpu

F32 = jnp.float32
BF16 = jnp.bfloat16

D_MODEL = 1024
HEAD_DIM = 64
A_HEADS = 8
A_KV_HEADS = 2
A_GROUP = A_HEADS // A_KV_HEADS
A_WIDTH = A_HEADS * HEAD_DIM
KV_WIDTH = A_KV_HEADS * HEAD_DIM
WINDOW = 128
M_HEADS = 4
M_HEAD_DIM = 128
M_WIDTH = M_HEADS * M_HEAD_DIM
CONV_W = 4
N_EXPERTS = 32
TOP_K = 4
D_FF = D_MODEL
SWIGLU_LIMIT = 7.0
SWIGLU_ALPHA = 1.702
MOE_BLOCK = 512
MOE_QUARTER = 128
EPS = 1e-6
NEG = -1e30

LANES = 128
SUBLANES = 8
GATE_COL = A_WIDTH + 2 * KV_WIDTH + 4 * M_WIDTH
IN_PAD = GATE_COL + LANES
TOK_TILE = 512
PROJ_TILE = 1024
PROJ_VMEM_LIMIT = 58 * 1024 * 1024
GROUP_CHUNK = 256
ATT_QB = 256
ATT_SB = 128
SAMPLE_NB = 16
PROMPT_CHUNK = 128
VMEM_LIMIT = 48 * 1024 * 1024
MOE_VMEM_LIMIT = 56 * 1024 * 1024


def _dot(a, b):
    return jnp.dot(a, b, preferred_element_type=F32)


def _dot_nt(a, b):
    return lax.dot_general(a, b, (((1,), (1,)), ((), ())), preferred_element_type=F32)


def _dot_tn(a, b):
    return lax.dot_general(a, b, (((0,), (0,)), ((), ())), preferred_element_type=F32)


def _split3(x):
    hi = x.astype(BF16)
    r1 = x - hi.astype(F32)
    mid = r1.astype(BF16)
    lo = (r1 - mid.astype(F32)).astype(BF16)
    return hi, mid, lo


def _log_sigmoid(x):
    return jnp.minimum(x, 0.0) - jnp.log1p(jnp.exp(-jnp.abs(x)))


def _sigmoid(x):
    return 0.5 * jnp.tanh(0.5 * x) + 0.5


def _load_row_tiles(ref2, rows):
    return jnp.concatenate([ref2[pl.ds(s, rows, stride=SUBLANES), :] for s in range(SUBLANES)], axis=1)


def _store_row_tiles(ref2, rows, val):
    for s in range(SUBLANES):
        ref2[pl.ds(s, rows, stride=SUBLANES), :] = val[:, s * LANES:(s + 1) * LANES]


def _row_tiles(ref2, first, n):
    start = first * SUBLANES
    if not isinstance(start, int):
        start = pl.multiple_of(start, SUBLANES)
    return ref2.at[pl.ds(start, n * SUBLANES), :]


def _split_specs(n_prompt_tiles, rows, width):
    return (pl.BlockSpec((rows, width), lambda i, *_: (jnp.minimum(i, n_prompt_tiles - 1), 0)),
            pl.BlockSpec((rows, width), lambda i, *_: (jnp.maximum(i - n_prompt_tiles, 0), 0)))


def _inproj_kernel(x_ref, g_ref, w_ref, wgt_ref, gmat_ref, qg_ref, kg_ref,
                   qn_ref, kn_ref, va_ref, qkm_ref, vm_ref, om_ref, gcol_ref, gt_ref):
    x = x_ref[...]
    ms = jnp.mean(x * x, axis=-1, keepdims=True)
    xn = ((x * lax.rsqrt(ms + EPS)) * g_ref[...]).astype(BF16)

    def seg(lo, hi):
        return _dot(xn, w_ref[:, lo:hi])

    def head_norm(z, gmat, g):
        parts = _split3(z * z)
        w = min(GROUP_CHUNK, z.shape[1])
        ss = jnp.concatenate([sum(_dot(p[:, c:c + w], gmat) for p in parts) for c in range(0, z.shape[1], w)], axis=1)
        return (z * lax.rsqrt(ss * (1.0 / HEAD_DIM) + EPS)) * g

    o0 = A_WIDTH
    o1 = o0 + KV_WIDTH
    o2 = o1 + KV_WIDTH
    o3 = o2 + 2 * M_WIDTH
    o4 = o3 + M_WIDTH
    o5 = o4 + M_WIDTH
    qn_ref[...] = head_norm(seg(0, o0), gmat_ref[...], qg_ref[...])
    kn_ref[...] = head_norm(seg(o0, o1), gmat_ref[:KV_WIDTH, :KV_WIDTH], kg_ref[...])
    va_ref[...] = seg(o1, o2)
    qkm_ref[...] = seg(o2, o3)
    vm_ref[...] = seg(o3, o4)
    om_ref[...] = seg(o4, o5)
    gcol_ref[...] = seg(o5, o5 + LANES)
    gt_ref[0] = _dot_nt(wgt_ref[...], xn)


def _inproj(x, g_attn, w_pad, wgt, gmat, qg, kg):
    T = x.shape[0]
    nt = T // PROJ_TILE
    row = lambda w: pl.BlockSpec((PROJ_TILE, w), lambda i: (i, 0))
    full = lambda a: pl.BlockSpec(a.shape, lambda i: (0,) * a.ndim, pipeline_mode=pl.Buffered(1))
    out_shape = (
        jax.ShapeDtypeStruct((T, A_WIDTH), F32),
        jax.ShapeDtypeStruct((T, KV_WIDTH), F32),
        jax.ShapeDtypeStruct((T, KV_WIDTH), F32),
        jax.ShapeDtypeStruct((T, 2 * M_WIDTH), F32),
        jax.ShapeDtypeStruct((T, M_WIDTH), F32),
        jax.ShapeDtypeStruct((T, M_WIDTH), F32),
        jax.ShapeDtypeStruct((T, LANES), F32),
        jax.ShapeDtypeStruct((nt, SUBLANES, PROJ_TILE), F32),
    )
    out_specs = (row(A_WIDTH), row(KV_WIDTH), row(KV_WIDTH), row(2 * M_WIDTH), row(M_WIDTH), row(M_WIDTH),
                 row(LANES), pl.BlockSpec((1, SUBLANES, PROJ_TILE), lambda i: (i, 0, 0)))
    return pl.pallas_call(
        _inproj_kernel,
        grid=(nt,),
        in_specs=[row(D_MODEL), full(g_attn), full(w_pad), full(wgt), full(gmat), full(qg), full(kg)],
        out_specs=out_specs,
        out_shape=out_shape,
        compiler_params=pltpu.CompilerParams(dimension_semantics=("parallel",), vmem_limit_bytes=PROJ_VMEM_LIMIT),
        name="inproj",
    )(x, g_attn, w_pad, wgt, gmat, qg, kg)


def _softmax_sink(pieces, masks, sink_col):
    masked = [jnp.where(mk, s, NEG) for s, mk in zip(pieces, masks)]
    m = sink_col
    for s in masked:
        m = jnp.maximum(m, jnp.max(s, axis=-1, keepdims=True))
    ps = [jnp.exp(s - m) for s in masked]
    den = jnp.exp(sink_col - m)
    for p in ps:
        den = den + jnp.sum(p, axis=-1, keepdims=True)
    return ps, 1.0 / den


def _stack_heads(q, g):
    return jnp.concatenate([q[:, (A_GROUP * g + i) * HEAD_DIM:(A_GROUP * g + i + 1) * HEAD_DIM]
                            for i in range(A_GROUP)], axis=0)


def _sink_col(sink_ref, g, rows_per_head):
    r = lax.broadcasted_iota(jnp.int32, (A_GROUP * rows_per_head, 1), 0)
    col = jnp.zeros((A_GROUP * rows_per_head, 1), F32)
    for i in range(A_GROUP):
        col = jnp.where(r // rows_per_head == i, sink_ref[A_GROUP * g + i], col)
    return col


def _attn_prompt_kernel(sink_ref, q_ref, kp_ref, kc_ref, vp_ref, vc_ref, o_ref):
    j = pl.program_id(1)
    scale = HEAD_DIM ** -0.5
    kall = jnp.concatenate([kp_ref[...], kc_ref[...]], axis=0).astype(BF16)
    vall = jnp.concatenate([vp_ref[...], vc_ref[...]], axis=0).astype(BF16)
    nrow = A_GROUP * ATT_SB
    r = lax.broadcasted_iota(jnp.int32, (nrow, 2 * ATT_SB), 0) % ATT_SB
    c = lax.broadcasted_iota(jnp.int32, (nrow, 2 * ATT_SB), 1)
    band = jnp.logical_and(c >= r, c <= r + WINDOW)
    band0 = jnp.logical_and(band, jnp.logical_or(c >= ATT_SB, j > 0))
    for sb in range(ATT_QB // ATT_SB):
        q = (q_ref[sb * ATT_SB:(sb + 1) * ATT_SB, :] * scale).astype(BF16)
        kwin = kall[sb * ATT_SB:(sb + 2) * ATT_SB]
        vwin = vall[sb * ATT_SB:(sb + 2) * ATT_SB]
        outs = []
        for g in range(A_KV_HEADS):
            lo, hi = g * HEAD_DIM, (g + 1) * HEAD_DIM
            s = _dot_nt(_stack_heads(q, g), kwin[:, lo:hi])
            (p,), inv = _softmax_sink([s], [band0 if sb == 0 else band], _sink_col(sink_ref, g, ATT_SB))
            o = _dot(p.astype(BF16), vwin[:, lo:hi]) * inv
            outs += [o[i * ATT_SB:(i + 1) * ATT_SB] for i in range(A_GROUP)]
        o_ref[sb * ATT_SB:(sb + 1) * ATT_SB, :] = jnp.concatenate(outs, axis=1)


def _attn_prompt(sinks, qn, kn, va, batch, seq):
    nq = seq // ATT_QB
    ratio = ATT_QB // ATT_SB
    cur = lambda w: pl.BlockSpec((ATT_QB, w), lambda b, j: (b * nq + j, 0))
    prev = lambda w: pl.BlockSpec((ATT_SB, w), lambda b, j: (jnp.maximum((b * nq + j) * ratio - 1, 0), 0))
    return pl.pallas_call(
        _attn_prompt_kernel,
        grid=(batch, nq),
        in_specs=[pl.BlockSpec(memory_space=pltpu.SMEM), cur(A_WIDTH), prev(KV_WIDTH), cur(KV_WIDTH),
                  prev(KV_WIDTH), cur(KV_WIDTH)],
        out_specs=cur(A_WIDTH),
        out_shape=jax.ShapeDtypeStruct((batch * seq, A_WIDTH), F32),
        compiler_params=pltpu.CompilerParams(dimension_semantics=("parallel", "parallel"),
                                             vmem_limit_bytes=VMEM_LIMIT),
        name="attn_prompt",
    )(sinks, qn, kn, kn, va, va)


def _attn_sample_kernel(dec, sink_ref, q_ref, kn_ref, vn_ref, ck_ref, cv_ref, o_ref, kw_ref, vw_ref):
    scale = HEAD_DIM ** -0.5
    rows = SAMPLE_NB * dec
    knew = kn_ref[...]
    vnew = vn_ref[...]
    knew_b = knew.astype(BF16)
    vnew_b = vnew.astype(BF16)
    nrow = A_GROUP * dec
    t = lax.broadcasted_iota(jnp.int32, (nrow, WINDOW), 0) % dec
    c = lax.broadcasted_iota(jnp.int32, (nrow, WINDOW), 1)
    m_cache = c >= t
    cn = lax.broadcasted_iota(jnp.int32, (nrow, rows), 1)
    tn = lax.broadcasted_iota(jnp.int32, (nrow, rows), 0) % dec
    for i in range(SAMPLE_NB):
        q = (q_ref[i * dec:(i + 1) * dec, :] * scale).astype(BF16)
        ck = ck_ref[i].astype(BF16)
        cv = cv_ref[i].astype(BF16)
        m_new = jnp.logical_and(cn // dec == i, cn % dec <= tn)
        outs = []
        for g in range(A_KV_HEADS):
            lo, hi = g * HEAD_DIM, (g + 1) * HEAD_DIM
            qs = _stack_heads(q, g)
            s_c = _dot_nt(qs, ck[:, lo:hi])
            s_n = _dot_nt(qs, knew_b[:, lo:hi])
            (p_c, p_n), inv = _softmax_sink([s_c, s_n], [m_cache, m_new], _sink_col(sink_ref, g, dec))
            o = (_dot(p_c.astype(BF16), cv[:, lo:hi]) + _dot(p_n.astype(BF16), vnew_b[:, lo:hi])) * inv
            outs += [o[h * dec:(h + 1) * dec] for h in range(A_GROUP)]
        o_ref[i * dec:(i + 1) * dec, :] = jnp.concatenate(outs, axis=1)
        kw_ref[i, 0:WINDOW - dec, :] = ck_ref[i, dec:WINDOW, :]
        kw_ref[i, WINDOW - dec:WINDOW, :] = knew[i * dec:(i + 1) * dec]
        vw_ref[i, 0:WINDOW - dec, :] = cv_ref[i, dec:WINDOW, :]
        vw_ref[i, WINDOW - dec:WINDOW, :] = vnew[i * dec:(i + 1) * dec]


def _attn_sample(sinks, qn, kn, va, ck, cv, dbatch, dec):
    rows = SAMPLE_NB * dec
    tokrow = lambda w: pl.BlockSpec((rows, w), lambda i: (i, 0))
    cache = pl.BlockSpec((SAMPLE_NB, WINDOW, KV_WIDTH), lambda i: (i, 0, 0))
    return pl.pallas_call(
        functools.partial(_attn_sample_kernel, dec),
        grid=(dbatch // SAMPLE_NB,),
        in_specs=[pl.BlockSpec(memory_space=pltpu.SMEM), tokrow(A_WIDTH), tokrow(KV_WIDTH), tokrow(KV_WIDTH),
                  cache, cache],
        out_specs=(pl.BlockSpec((rows, A_WIDTH), lambda i: (i, 0)), cache, cache),
        out_shape=(jax.ShapeDtypeStruct((dbatch * dec, A_WIDTH), F32),
                   jax.ShapeDtypeStruct((dbatch, WINDOW, KV_WIDTH), F32),
                   jax.ShapeDtypeStruct((dbatch, WINDOW, KV_WIDTH), F32)),
        compiler_params=pltpu.CompilerParams(dimension_semantics=("parallel",), vmem_limit_bytes=VMEM_LIMIT),
        name="attn_sample",
    )(sinks, qn, kn, va, ck, cv)


def _mlstm_kernel(nseq, L, qk_ref, v_ref, o_ref, gcol_ref, gt_ref, conv0_ref, c0_ref, n0_ref, m0_ref,
                  cw_ref, cb_ref, gbrow_ref, gbcol_ref, mng_ref, mask_ref,
                  out_ref, cst_ref, nst_ref, mst_ref, prev_ref):
    R = nseq * L
    ci = pl.program_id(1)

    @pl.when(ci == 0)
    def _():
        cst_ref[...] = c0_ref[...]
        nst_ref[...] = n0_ref[...]
        mst_ref[...] = m0_ref[...]
        prev_ref[...] = conv0_ref[0]

    raw = qk_ref[...]
    row = lax.broadcasted_iota(jnp.int32, (R, 1), 0)
    tpos = row % L
    rseq = row // L
    acc = raw * cw_ref[CONV_W - 1:CONV_W, :] + cb_ref[...]
    if nseq == 1:
        prev8 = prev_ref[...]
        t8 = lax.broadcasted_iota(jnp.int32, (SUBLANES, 1), 0)
    else:
        prevsrc = conv0_ref[...].reshape(R, 2 * M_WIDTH)
    for k in range(1, CONV_W):
        rolled = pltpu.roll(raw, k, 0)
        if nseq == 1:
            head = jnp.where(t8 >= k, rolled[0:SUBLANES], pltpu.roll(prev8, k, 0))
            sh = jnp.concatenate([head, rolled[SUBLANES:]], axis=0)
        else:
            sh = jnp.where(tpos >= k, rolled, pltpu.roll(prevsrc, R - SUBLANES + k, 0))
        acc = acc + sh * cw_ref[CONV_W - 1 - k:CONV_W - k, :]
    if nseq == 1:
        prev_ref[...] = raw[R - SUBLANES:R]
    qkc = acc * _sigmoid(acc)

    gc = gcol_ref[...] + gbrow_ref[...]
    gr = gt_ref[0] + gbcol_ref[...]
    lsc = _log_sigmoid(gc)
    lsr = _log_sigmoid(gr)
    mb = mask_ref[...]
    maskb = mb > 0
    bcol = sum(_dot(mb, p) for p in _split3(lsc))
    brow = sum(_dot_nt(p, mb) for p in _split3(lsr))

    lane = lax.broadcasted_iota(jnp.int32, (1, LANES), 1)
    m_new = [jnp.zeros((1, LANES), F32) for _ in range(nseq)]
    for h in range(M_HEADS):
        sl = slice(h * M_HEAD_DIM, (h + 1) * M_HEAD_DIM)
        qh = qkc[:, sl]
        kh = qkc[:, M_WIDTH + h * M_HEAD_DIM:M_WIDTH + (h + 1) * M_HEAD_DIM] * (M_HEAD_DIM ** -0.5)
        vh = v_ref[:, sl]
        qb, kb, vb = qh.astype(BF16), kh.astype(BF16), vh.astype(BF16)
        ig_c = gc[:, h:h + 1]
        b_c = bcol[:, M_HEADS + h:M_HEADS + h + 1]
        ig_r = gr[h:h + 1, :]
        b_r = brow[M_HEADS + h:M_HEADS + h + 1, :]
        if nseq == 1:
            m0c = mst_ref[0][:, h:h + 1]
            n0rows = nst_ref[0, h:h + 1, :]
        else:
            m0c = jnp.zeros((R, 1), F32)
            n0rows = jnp.zeros((R, M_HEAD_DIM), F32)
            for s in range(nseq):
                m0c = jnp.where(rseq == s, mst_ref[s][:, h:h + 1], m0c)
                n0rows = jnp.where(rseq == s, nst_ref[s, h:h + 1, :], n0rows)
        dm = jnp.where(maskb, b_c - b_r + ig_r, NEG)
        a_c = b_c + m0c
        m_c = jnp.maximum(a_c, jnp.max(dm, axis=-1, keepdims=True))
        w = jnp.exp(dm - m_c)
        sc = jnp.exp(a_c - m_c)
        wqk = w * _dot_nt(qb, kb)
        if nseq == 1:
            inter = _dot_nt(qb, cst_ref[0, h].astype(BF16))
        else:
            inter = jnp.zeros((R, M_HEAD_DIM), F32)
            for s in range(nseq):
                qs = jnp.where(rseq == s, qh, 0.0).astype(BF16)
                inter = inter + _dot_nt(qs, cst_ref[s, h].astype(BF16))
        num = _dot(wqk.astype(BF16), vb) + sc * inter
        den = jnp.sum(wqk, axis=-1, keepdims=True) + sc * jnp.sum(qh * n0rows, axis=-1, keepdims=True)
        hh = num / jnp.maximum(jnp.abs(den), jnp.exp(-m_c))

        for s in range(nseq):
            e = s * L + L - 1
            m_end = m_c[e:e + 1, :]
            wend = jnp.exp(b_c[e:e + 1, :] - b_c + ig_c - m_end)
            if nseq > 1:
                wend = jnp.where(rseq == s, wend, 0.0)
            sce = jnp.exp(a_c[e:e + 1, :] - m_end)
            c_new = sce * cst_ref[s, h] + _dot_tn((vh * wend).astype(BF16), kb)
            n_new = sce * nst_ref[s, h:h + 1, :] + jnp.sum(wend * kh, axis=0, keepdims=True)
            cst_ref[s, h] = c_new
            nst_ref[s, h:h + 1, :] = n_new
            m_new[s] = jnp.where(lane == h, m_end, m_new[s])

        hn = (hh * lax.rsqrt(jnp.mean(hh * hh, axis=-1, keepdims=True) + EPS)) * mng_ref[:, sl]
        out_ref[:, sl] = _sigmoid(o_ref[:, sl]) * hn
    for s in range(nseq):
        mst_ref[s] = m_new[s]


def _mlstm(qkm, vm, om, gcol, gt, conv0, c0, n0, m0, cw, cb, gbrow, gbcol, mng, nseq, L, ngroups, nchunks):
    R = nseq * L
    per_tile = gt.shape[2] // R
    tok = lambda w: pl.BlockSpec((R, w), lambda g, c: (g * nchunks + c, 0))
    gt_spec = pl.BlockSpec((1, SUBLANES, R),
                           lambda g, c: ((g * nchunks + c) // per_tile, 0, (g * nchunks + c) % per_tile))
    full = lambda a: pl.BlockSpec(a.shape, lambda g, c: (0,) * a.ndim)
    st4 = pl.BlockSpec((nseq, M_HEADS, M_HEAD_DIM, M_HEAD_DIM), lambda g, c: (g, 0, 0, 0))
    st3 = pl.BlockSpec((nseq, M_HEADS, M_HEAD_DIM), lambda g, c: (g, 0, 0))
    stm = pl.BlockSpec((nseq, 1, LANES), lambda g, c: (g, 0, 0))
    conv_spec = pl.BlockSpec((nseq, SUBLANES, 2 * M_WIDTH), lambda g, c: (g, 0, 0))
    r = jnp.arange(R)
    mask = ((r[:, None] // L == r[None, :] // L) & (r[None, :] <= r[:, None])).astype(BF16)
    nstate = ngroups * nseq
    return pl.pallas_call(
        functools.partial(_mlstm_kernel, nseq, L),
        grid=(ngroups, nchunks),
        in_specs=[tok(2 * M_WIDTH), tok(M_WIDTH), tok(M_WIDTH), tok(LANES), gt_spec, conv_spec, st4, st3, stm,
                  full(cw), full(cb), full(gbrow), full(gbcol), full(mng), full(mask)],
        out_specs=(pl.BlockSpec((R, M_WIDTH), lambda g, c: (g * nchunks + c, 0)), st4, st3, stm),
        out_shape=(jax.ShapeDtypeStruct((ngroups * nchunks * R, M_WIDTH), F32),
                   jax.ShapeDtypeStruct((nstate, M_HEADS, M_HEAD_DIM, M_HEAD_DIM), F32),
                   jax.ShapeDtypeStruct((nstate, M_HEADS, M_HEAD_DIM), F32),
                   jax.ShapeDtypeStruct((nstate, 1, LANES), F32)),
        scratch_shapes=[pltpu.VMEM((SUBLANES, 2 * M_WIDTH), F32)],
        compiler_params=pltpu.CompilerParams(dimension_semantics=("parallel", "arbitrary"),
                                             vmem_limit_bytes=VMEM_LIMIT),
        name="mlstm_n%d" % nseq,
    )(qkm, vm, om, gcol, gt, conv0, c0, n0, m0, cw, cb, gbrow, gbcol, mng, mask)


def _outproj_kernel(npt, ap_ref, as_ref, mp_ref, ms_ref, xp_ref, xs_ref, wo_ref, g_ref, wrt_ref, brc_ref,
                    h_ref, hn_ref, route_ref, cnt_ref):
    def project(a_ref, m_ref, x_ref):
        h_ref[...] = (x_ref[...] + _dot(a_ref[...].astype(BF16), wo_ref[0:A_WIDTH, :])
                      + _dot(m_ref[...].astype(BF16), wo_ref[A_WIDTH:A_WIDTH + M_WIDTH, :]))

    @pl.when(pl.program_id(0) < npt)
    def _():
        project(ap_ref, mp_ref, xp_ref)

    @pl.when(pl.program_id(0) >= npt)
    def _():
        project(as_ref, ms_ref, xs_ref)

    h = h_ref[...]
    hn = (h * lax.rsqrt(jnp.mean(h * h, axis=-1, keepdims=True) + EPS)) * g_ref[...]
    _store_row_tiles(hn_ref, TOK_TILE, hn)
    logits = _dot_nt(wrt_ref[...], hn.astype(BF16)) + brc_ref[...]
    eidx = lax.broadcasted_iota(jnp.int32, logits.shape, 0).astype(F32)
    picked = jnp.zeros(logits.shape, F32)
    top0 = None
    den = None
    es = []
    ids = []
    for k in range(TOP_K):
        mx = jnp.max(logits, axis=0, keepdims=True)
        idx = jnp.min(jnp.where(logits == mx, eidx, float(N_EXPERTS)), axis=0, keepdims=True)
        if k == 0:
            top0 = mx
        e = jnp.exp(mx - top0)
        den = e if den is None else den + e
        es.append(e)
        ids.append(idx)
        hit = eidx == idx
        picked = jnp.where(hit, 1.0, picked)
        logits = jnp.where(hit, -jnp.inf, logits)
    route_ref[0] = jnp.concatenate([e / den for e in es] + ids, axis=0)

    @pl.when(pl.program_id(0) == 0)
    def _():
        cnt_ref[...] = jnp.zeros_like(cnt_ref)

    cnt_ref[...] += jnp.broadcast_to(jnp.sum(picked, axis=1, keepdims=True), cnt_ref.shape)


def _outproj(a_p, a_s, m_p, m_s, xp, xs, w_out, g_ffn, wr_t, br_col):
    T = xp.shape[0] + xs.shape[0]
    nt = T // TOK_TILE
    npt = xp.shape[0] // TOK_TILE
    row = lambda w: pl.BlockSpec((TOK_TILE, w), lambda i: (i, 0))
    full = lambda a: pl.BlockSpec(a.shape, lambda i: (0,) * a.ndim)
    return pl.pallas_call(
        functools.partial(_outproj_kernel, npt),
        grid=(nt,),
        in_specs=[*_split_specs(npt, TOK_TILE, A_WIDTH), *_split_specs(npt, TOK_TILE, M_WIDTH),
                  *_split_specs(npt, TOK_TILE, D_MODEL), full(w_out), full(g_ffn), full(wr_t), full(br_col)],
        out_specs=(row(D_MODEL), pl.BlockSpec((TOK_TILE * SUBLANES, LANES), lambda i: (i, 0)),
                   pl.BlockSpec((1, 2 * TOP_K, TOK_TILE), lambda i: (i, 0, 0)),
                   pl.BlockSpec((N_EXPERTS, LANES), lambda i: (0, 0))),
        out_shape=(jax.ShapeDtypeStruct((T, D_MODEL), F32), jax.ShapeDtypeStruct((T * SUBLANES, LANES), F32),
                   jax.ShapeDtypeStruct((nt, 2 * TOP_K, TOK_TILE), F32),
                   jax.ShapeDtypeStruct((N_EXPERTS, LANES), F32)),
        compiler_params=pltpu.CompilerParams(dimension_semantics=("arbitrary",), vmem_limit_bytes=VMEM_LIMIT),
        name="outproj_router",
    )(a_p, a_s, m_p, m_s, xp, xs, w_out, g_ffn, wr_t, br_col)


def _route_kernel(nblk_pad, route_ref, cnt_ref, ustrict_ref, lstrict_ref, lp_ref, tab_ref, blk_ref, info_ref,
                  carry_ref):
    i = pl.program_id(0)
    cnt = cnt_ref[...]
    nb_e = jnp.floor((cnt + (MOE_BLOCK - 1.0)) * (1.0 / MOE_BLOCK))
    bstart = sum(_dot(lstrict_ref[...], p) for p in _split3(nb_e))
    bend = bstart + nb_e
    row_start = bstart * float(MOE_BLOCK)

    @pl.when(i == 0)
    def _():
        carry_ref[...] = jnp.zeros_like(carry_ref)
        bi = lax.broadcasted_iota(jnp.int32, (N_EXPERTS, nblk_pad), 1).astype(F32)
        done = jnp.where(bend[:, 0:1] <= bi, 1.0, 0.0)
        be = jnp.minimum(jnp.sum(done, axis=0, keepdims=True), N_EXPERTS - 1.0)
        blk_ref[...] = jnp.broadcast_to(be, blk_ref.shape).astype(jnp.int32)
        lane = lax.broadcasted_iota(jnp.int32, (N_EXPERTS, LANES), 1)
        info = jnp.where(lane == 0, row_start + cnt, 0.0)
        info = jnp.where(lane == 1, nb_e * float(MOE_BLOCK) - cnt, info)
        info = jnp.where(lane == 2, bend, info)
        info_ref[...] = info.astype(jnp.int32)

    r = route_ref[0]
    eidx = lax.broadcasted_iota(jnp.int32, (N_EXPERTS, TOK_TILE), 0).astype(F32)
    sel = [eidx == r[TOP_K + k:TOP_K + k + 1, :] for k in range(TOP_K)]
    oh = jnp.zeros((N_EXPERTS, TOK_TILE), F32)
    for k in range(TOP_K):
        oh = jnp.where(sel[k], 1.0, oh)
    cnt_t = jnp.broadcast_to(jnp.sum(oh, axis=1, keepdims=True), (N_EXPERTS, LANES))
    seg = sum(_dot(lstrict_ref[...], p) for p in _split3(cnt_t))
    local = _dot(oh.astype(BF16), ustrict_ref[...]) + seg[:, 0:1]
    rows = [jnp.sum(jnp.where(sel[k], local, 0.0), axis=0, keepdims=True) for k in range(TOP_K)]
    lp_ref[0] = (jnp.concatenate(rows + [jnp.zeros((TOP_K, TOK_TILE), F32)], axis=0)
                 * float(SUBLANES)).astype(jnp.int32)
    diag = (lax.broadcasted_iota(jnp.int32, (N_EXPERTS, LANES), 0)
            == lax.broadcasted_iota(jnp.int32, (N_EXPERTS, LANES), 1))
    to_lanes = lambda col: jnp.sum(jnp.where(diag, col, 0.0), axis=0, keepdims=True)
    tab = [to_lanes(seg), to_lanes(cnt_t), to_lanes(row_start + carry_ref[...])]
    tab_ref[0] = jnp.concatenate(tab + [jnp.zeros((SUBLANES - len(tab), LANES), F32)], axis=0).astype(jnp.int32)
    carry_ref[...] += cnt_t


def _route_tables(route, cnt, nblk):
    nt = route.shape[0]
    nblk_pad = -(-nblk // LANES) * LANES
    a = jnp.arange(TOK_TILE)
    ustrict = (a[:, None] < a[None, :]).astype(BF16)
    b = jnp.arange(N_EXPERTS)
    lstrict = (b[:, None] > b[None, :]).astype(BF16)
    full = lambda x: pl.BlockSpec(x.shape, lambda i: (0,) * x.ndim)
    tile = pl.BlockSpec((1, 2 * TOP_K, TOK_TILE), lambda i: (i, 0, 0))
    return pl.pallas_call(
        functools.partial(_route_kernel, nblk_pad),
        grid=(nt,),
        in_specs=[tile, full(cnt), full(ustrict), full(lstrict)],
        out_specs=(tile, pl.BlockSpec((1, SUBLANES, LANES), lambda i: (i, 0, 0)),
                   pl.BlockSpec((SUBLANES, nblk_pad), lambda i: (0, 0)),
                   pl.BlockSpec((N_EXPERTS, LANES), lambda i: (0, 0))),
        out_shape=(jax.ShapeDtypeStruct((nt, 2 * TOP_K, TOK_TILE), jnp.int32),
                   jax.ShapeDtypeStruct((nt, SUBLANES, LANES), jnp.int32),
                   jax.ShapeDtypeStruct((SUBLANES, nblk_pad), jnp.int32),
                   jax.ShapeDtypeStruct((N_EXPERTS, LANES), jnp.int32)),
        scratch_shapes=[pltpu.VMEM((N_EXPERTS, LANES), F32)],
        compiler_params=pltpu.CompilerParams(dimension_semantics=("arbitrary",), vmem_limit_bytes=VMEM_LIMIT),
        name="route_tables",
    )(route, cnt, ustrict, lstrict)


RUN_PIECES = tuple(1 << b for b in range(9, -1, -1))


def _for_run_pieces(count, fn):
    for size in RUN_PIECES:
        @pl.when((count & size) != 0)
        def _(size=size):
            fn(count & ~(2 * size - 1), size)


def _dispatch_kernel(padrow_ref, npad_ref, nu_ref, lp_ref, tab_ref, hn_ref, xs_hbm, srt, zbuf, sem, zsem):
    i = pl.program_id(0)
    nt = pl.num_programs(0)
    slot = i % 2

    @pl.when(i == 0)
    def _():
        zbuf[...] = jnp.zeros_like(zbuf)
        nblk = xs_hbm.shape[0] // (MOE_BLOCK * SUBLANES)

        def tail_start(b, c):
            pltpu.make_async_copy(zbuf, _row_tiles(xs_hbm, b * MOE_BLOCK, MOE_BLOCK), zsem).start()
            return c

        def tail_wait(b, c):
            pltpu.make_async_copy(zbuf, _row_tiles(xs_hbm, b * MOE_BLOCK, MOE_BLOCK), zsem).wait()
            return c

        lax.fori_loop(nu_ref[0], nblk, tail_start, 0)
        lax.fori_loop(nu_ref[0], nblk, tail_wait, 0)

        def pad_runs(wait):
            def per_expert(e, carry):
                first, count = padrow_ref[e], npad_ref[e]

                def piece(off, size):
                    cp = pltpu.make_async_copy(_row_tiles(zbuf, 0, size), _row_tiles(xs_hbm, first + off, size), zsem)
                    cp.wait() if wait else cp.start()
                _for_run_pieces(count, piece)
                return carry
            lax.fori_loop(0, N_EXPERTS, per_expert, 0)

        pad_runs(False)
        pad_runs(True)

    def wait_runs(s):
        for _ in range(TOP_K):
            pltpu.make_async_copy(_row_tiles(srt.at[s], 0, TOK_TILE), _row_tiles(xs_hbm, 0, TOK_TILE), sem.at[s]).wait()

    def step(s):
        @pl.when(i >= 2)
        def _():
            wait_runs(s)

        def permute(j, carry):
            for u in range(SUBLANES):
                t = j * SUBLANES + u
                row = hn_ref[pl.ds(pl.multiple_of(t * SUBLANES, SUBLANES), SUBLANES), :]
                for k in range(TOP_K):
                    p = lp_ref[0, 0, t * TOP_K + k]
                    srt[s, pl.ds(pl.multiple_of(p, SUBLANES), SUBLANES), :] = row
            return carry

        lax.fori_loop(0, TOK_TILE // SUBLANES, permute, 0)

        def send_run(e, carry):
            local, count, first = tab_ref[0, 0, e], tab_ref[0, 1, e], tab_ref[0, 2, e]

            def piece(off, size):
                pltpu.make_async_copy(_row_tiles(srt.at[s], local + off, size),
                                      _row_tiles(xs_hbm, first + off, size), sem.at[s]).start()
            _for_run_pieces(count, piece)
            return carry

        lax.fori_loop(0, N_EXPERTS, send_run, 0)

        @pl.when(i == nt - 1)
        def _():
            @pl.when(nt >= 2)
            def _():
                wait_runs(1 - s)

            wait_runs(s)

    for s in range(2):
        pl.when(slot == s)(functools.partial(step, s))


def _dispatch(padrow, npad, nused, lp_tiles, tab, hn, n_rows):
    nt = lp_tiles.shape[0]
    grid_spec = pltpu.PrefetchScalarGridSpec(
        num_scalar_prefetch=3,
        grid=(nt,),
        in_specs=[pl.BlockSpec((1, 1, TOK_TILE * TOP_K), lambda i, *_: (i, 0, 0), memory_space=pltpu.SMEM),
                  pl.BlockSpec((1, SUBLANES, LANES), lambda i, *_: (i, 0, 0), memory_space=pltpu.SMEM),
                  pl.BlockSpec((TOK_TILE * SUBLANES, LANES), lambda i, *_: (i, 0))],
        out_specs=pl.BlockSpec(memory_space=pl.ANY),
        scratch_shapes=[pltpu.VMEM((2, TOK_TILE * TOP_K * SUBLANES, LANES), F32),
                        pltpu.VMEM((MOE_BLOCK * SUBLANES, LANES), F32),
                        pltpu.SemaphoreType.DMA((2,)), pltpu.SemaphoreType.DMA(())],
    )
    return pl.pallas_call(
        _dispatch_kernel,
        grid_spec=grid_spec,
        out_shape=jax.ShapeDtypeStruct((n_rows * SUBLANES, LANES), F32),
        compiler_params=pltpu.CompilerParams(dimension_semantics=("arbitrary",), vmem_limit_bytes=VMEM_LIMIT),
        name="moe_dispatch",
    )(padrow, npad, nused, lp_tiles, tab, hn)


def _moe_kernel(be_ref, nu_ref, first_ref, slot_ref, nxt_ref, nvalid_ref, x_ref, bgu_ref, bd_ref, wgu_hbm, wd_hbm, y_ref,
                wgu_buf, wd_buf, wsem):
    i = pl.program_id(0)
    used = i < nu_ref[0]
    s = slot_ref[i]

    def fetch(e, sl):
        return (pltpu.make_async_copy(wgu_hbm.at[e], wgu_buf.at[sl], wsem.at[0, sl]),
                pltpu.make_async_copy(wd_hbm.at[e], wd_buf.at[sl], wsem.at[1, sl]))

    @pl.when(jnp.logical_and(used, first_ref[i] == 1))
    def _():
        @pl.when(i == 0)
        def _():
            for c in fetch(be_ref[0], 0):
                c.start()

        for c in fetch(be_ref[i], s):
            c.wait()

        @pl.when(nxt_ref[i] >= 0)
        def _():
            for c in fetch(nxt_ref[i], 1 - s):
                c.start()

    def expert(rows):
        x = _load_row_tiles(x_ref, rows).astype(BF16)
        hb = _dot(x, wgu_buf[s].astype(BF16)) + bgu_ref[0]
        glu = jnp.minimum(hb[:, :D_FF], SWIGLU_LIMIT)
        lin = jnp.clip(hb[:, D_FF:], -SWIGLU_LIMIT, SWIGLU_LIMIT)
        act = glu * _sigmoid(SWIGLU_ALPHA * glu) * (lin + 1.0)
        _store_row_tiles(y_ref, rows, _dot(act.astype(BF16), wd_buf[s].astype(BF16)) + bd_ref[0])
        if rows < MOE_BLOCK:
            y_ref[rows * SUBLANES:, :] = jnp.zeros(((MOE_BLOCK - rows) * SUBLANES, LANES), F32)

    nv = nvalid_ref[i]
    for rows in range(MOE_QUARTER, MOE_BLOCK + 1, MOE_QUARTER):
        lo = rows - MOE_QUARTER if rows > MOE_QUARTER else -1
        pl.when(jnp.logical_and(used, jnp.logical_and(nv > lo, nv <= rows)))(functools.partial(expert, rows))

    @pl.when(i >= nu_ref[0])
    def _():
        y_ref[...] = jnp.zeros_like(y_ref)


def _moe_blocks(block_e, nused, padrow, xs, wgu, bgu, wd, bd):
    nblk = block_e.shape[0]
    pad_at = jnp.sum(jnp.where(block_e[:, None] == jnp.arange(N_EXPERTS)[None, :], padrow[None, :], 0), axis=1)
    nvalid = jnp.clip(pad_at - jnp.arange(nblk, dtype=jnp.int32) * MOE_BLOCK, 0, MOE_BLOCK).astype(jnp.int32)
    idx = jnp.arange(nblk, dtype=jnp.int32)
    first = (idx < nused[0]) & ((idx == 0) | (block_e != jnp.roll(block_e, 1)))
    slot = ((jnp.cumsum(first.astype(jnp.int32)) - 1) % 2).astype(jnp.int32)
    first_pos = jnp.where(first, idx, nblk)
    later = jnp.concatenate([first_pos[1:], jnp.full((1,), nblk, jnp.int32)])
    next_pos = lax.cummin(later, reverse=True)
    nxt = jnp.sum(jnp.where(idx[None, :] == next_pos[:, None], block_e[None, :] + 1, 0), axis=1) - 1
    grid_spec = pltpu.PrefetchScalarGridSpec(
        num_scalar_prefetch=6,
        grid=(nblk,),
        in_specs=[
            pl.BlockSpec((MOE_BLOCK * SUBLANES, LANES), lambda i, be, nu, *_: (jnp.minimum(i, jnp.maximum(nu[0] - 1, 0)), 0)),
            pl.BlockSpec((1, 1, 2 * D_FF), lambda i, be, *_: (be[i], 0, 0)),
            pl.BlockSpec((1, 1, D_MODEL), lambda i, be, *_: (be[i], 0, 0)),
            pl.BlockSpec(memory_space=pl.ANY),
            pl.BlockSpec(memory_space=pl.ANY),
        ],
        out_specs=pl.BlockSpec((MOE_BLOCK * SUBLANES, LANES), lambda i, be, *_: (i, 0)),
        scratch_shapes=[pltpu.VMEM((2, D_MODEL, 2 * D_FF), F32), pltpu.VMEM((2, D_FF, D_MODEL), F32),
                        pltpu.SemaphoreType.DMA((2, 2))],
    )
    return pl.pallas_call(
        _moe_kernel,
        grid_spec=grid_spec,
        out_shape=jax.ShapeDtypeStruct(xs.shape, F32),
        compiler_params=pltpu.CompilerParams(dimension_semantics=("arbitrary",), vmem_limit_bytes=MOE_VMEM_LIMIT),
        name="moe_blocks",
    )(block_e, nused, first.astype(jnp.int32), slot, nxt.astype(jnp.int32), nvalid, xs, bgu, bd, wgu, wd)


def _combine_kernel(npt, lp_ref, gate_ref, tabc_ref, tabn_ref, h_ref, ys_hbm, yp_ref, ysm_ref, srt, acc, sem):
    i = pl.program_id(0)
    nt = pl.num_programs(0)
    slot = i % 2

    def fetch_runs(tab_ref, s):
        def run(e, carry):
            local, count, first = tab_ref[0, 0, e], tab_ref[0, 1, e], tab_ref[0, 2, e]

            def piece(off, size):
                pltpu.make_async_copy(_row_tiles(ys_hbm, first + off, size),
                                      _row_tiles(srt.at[s], local + off, size), sem.at[s]).start()
            _for_run_pieces(count, piece)
            return carry
        lax.fori_loop(0, N_EXPERTS, run, 0)

    @pl.when(i == 0)
    def _():
        fetch_runs(tabc_ref, 0)

    def step(s):
        @pl.when(i + 1 < nt)
        def _():
            fetch_runs(tabn_ref, 1 - s)

        for _ in range(TOP_K):
            pltpu.make_async_copy(_row_tiles(ys_hbm, 0, TOK_TILE), _row_tiles(srt.at[s], 0, TOK_TILE),
                                  sem.at[s]).wait()

        def gather(j, carry):
            for u in range(SUBLANES):
                t = j * SUBLANES + u
                tot = None
                for k in range(TOP_K):
                    p = lp_ref[0, 0, t * TOP_K + k]
                    term = (srt[s, pl.ds(pl.multiple_of(p, SUBLANES), SUBLANES), :]
                            * gate_ref[0, 0, t * TOP_K + k])
                    tot = term if tot is None else tot + term
                acc[pl.ds(pl.multiple_of(t * SUBLANES, SUBLANES), SUBLANES), :] = tot
            return carry

        lax.fori_loop(0, TOK_TILE // SUBLANES, gather, 0)

    for s in range(2):
        pl.when(slot == s)(functools.partial(step, s))
    y = h_ref[...] + _load_row_tiles(acc, TOK_TILE)

    @pl.when(i < npt)
    def _():
        yp_ref[...] = y

    @pl.when(i >= npt)
    def _():
        ysm_ref[...] = y


def _combine(lp_tiles, gate_tiles, tab, h, ys, n_prompt_rows):
    T = h.shape[0]
    nt = T // TOK_TILE
    npt = n_prompt_rows // TOK_TILE
    per_assign = pl.BlockSpec((1, 1, TOK_TILE * TOP_K), lambda i: (i, 0, 0), memory_space=pltpu.SMEM)
    tab_blk = lambda imap: pl.BlockSpec((1, SUBLANES, LANES), imap, memory_space=pltpu.SMEM)
    return pl.pallas_call(
        functools.partial(_combine_kernel, npt),
        grid=(nt,),
        in_specs=[per_assign, per_assign,
                  tab_blk(lambda i: (i, 0, 0)), tab_blk(lambda i: (jnp.minimum(i + 1, nt - 1), 0, 0)),
                  pl.BlockSpec((TOK_TILE, D_MODEL), lambda i: (i, 0)),
                  pl.BlockSpec(memory_space=pl.ANY)],
        out_specs=(pl.BlockSpec((TOK_TILE, D_MODEL), lambda i: (jnp.minimum(i, npt - 1), 0)),
                   pl.BlockSpec((TOK_TILE, D_MODEL), lambda i: (jnp.maximum(i - npt, 0), 0))),
        out_shape=(jax.ShapeDtypeStruct((n_prompt_rows, D_MODEL), F32),
                   jax.ShapeDtypeStruct((T - n_prompt_rows, D_MODEL), F32)),
        scratch_shapes=[pltpu.VMEM((2, TOK_TILE * TOP_K * SUBLANES, LANES), F32),
                        pltpu.VMEM((TOK_TILE * SUBLANES, LANES), F32), pltpu.SemaphoreType.DMA((2,))],
        compiler_params=pltpu.CompilerParams(dimension_semantics=("arbitrary",), vmem_limit_bytes=VMEM_LIMIT),
        name="moe_combine",
    )(lp_tiles, gate_tiles, tab, tab, h, ys)


def kernel(x_prompt, x_sample, cache_k_win, cache_v_win, state_conv, state_C, state_n, state_m, g_attn, w_in, b_i,
           b_f, q_norm_g, k_norm_g, sinks, conv_w, conv_b, m_norm_g, w_out, g_ffn, w_router, b_router, w_gate_up,
           b_gate_up, w_down, b_down):
    depth = g_attn.shape[0]
    assert depth == 1
    B, S, _ = x_prompt.shape
    DB, DS, _ = x_sample.shape
    TP = B * S
    TS = DB * DS
    T = TP + TS
    assert T % TOK_TILE == 0 and TP % TOK_TILE == 0 and S % ATT_QB == 0 and S % PROMPT_CHUNK == 0
    assert DS == SUBLANES and DB % SAMPLE_NB == 0 and (SAMPLE_NB * DS) == LANES
    l = 0

    xp = x_prompt.reshape(TP, D_MODEL)
    xs = x_sample.reshape(TS, D_MODEL)

    w_pad = jnp.pad(w_in[l], ((0, 0), (0, IN_PAD - w_in.shape[2]))).astype(BF16)
    wgt = jnp.transpose(w_in[l][:, GATE_COL:GATE_COL + 2 * M_HEADS]).astype(BF16)
    gi = jnp.arange(GROUP_CHUNK) // HEAD_DIM
    gmat = (gi[:, None] == gi[None, :]).astype(BF16)
    qg = jnp.tile(q_norm_g[l], A_HEADS).reshape(1, A_WIDTH)
    kg = jnp.tile(k_norm_g[l], A_KV_HEADS).reshape(1, KV_WIDTH)
    gbias = jnp.concatenate([b_i[l], b_f[l]])
    gbrow = jnp.pad(gbias, (0, LANES - 2 * M_HEADS)).reshape(1, LANES)
    gbcol = gbias.reshape(2 * M_HEADS, 1)
    mng = m_norm_g[l].reshape(1, M_WIDTH)
    cw = conv_w[l]
    cb = conv_b[l].reshape(1, 2 * M_WIDTH)

    proj_w = (g_attn[l].reshape(1, D_MODEL), w_pad, wgt, gmat, qg, kg)
    qn, kn, va, qkm, vm, om, gcol, gt = _inproj(xp, *proj_w)
    qn_s, kn_s, va_s, qkm_s, vm_s, om_s, gcol_s, gt_s = _inproj(xs, *proj_w)

    a_p = _attn_prompt(sinks[l], qn, kn, va, B, S)
    ck = cache_k_win[l].reshape(DB, WINDOW, KV_WIDTH)
    cv = cache_v_win[l].reshape(DB, WINDOW, KV_WIDTH)
    a_s, kwin_s, vwin_s = _attn_sample(sinks[l], qn_s, kn_s, va_s, ck, cv, DB, DS)

    zc = jnp.zeros((B, SUBLANES, 2 * M_WIDTH), F32)
    m_p, C_p, n_p, mm_p = _mlstm(
        qkm, vm, om, gcol, gt, zc,
        jnp.zeros((B, M_HEADS, M_HEAD_DIM, M_HEAD_DIM), F32), jnp.zeros((B, M_HEADS, M_HEAD_DIM), F32),
        jnp.full((B, 1, LANES), NEG, F32), cw, cb, gbrow, gbcol, mng,
        nseq=1, L=PROMPT_CHUNK, ngroups=B, nchunks=S // PROMPT_CHUNK)
    conv_s0 = jnp.pad(state_conv[l], ((0, 0), (SUBLANES - (CONV_W - 1), 0), (0, 0)))
    m0_s = jnp.pad(state_m[l], ((0, 0), (0, LANES - M_HEADS))).reshape(DB, 1, LANES)
    m_s, C_s, n_s, mm_s = _mlstm(
        qkm_s, vm_s, om_s, gcol_s, gt_s, conv_s0, state_C[l], state_n[l], m0_s, cw, cb, gbrow, gbcol, mng,
        nseq=SAMPLE_NB, L=DS, ngroups=DB // SAMPLE_NB, nchunks=1)

    h, hn, route, cnt = _outproj(a_p, a_s, m_p, m_s, xp, xs, w_out[l].astype(BF16), g_ffn[l].reshape(1, D_MODEL),
                                 jnp.transpose(w_router[l]).astype(BF16), b_router[l].reshape(N_EXPERTS, 1))

    nblk = T * TOP_K // MOE_BLOCK + N_EXPERTS
    lp, tab, blk, info = _route_tables(route, cnt, nblk)
    block_e = blk[0, :nblk]
    padrow = info[:, 0]
    npad = info[:, 1]
    nused = info[N_EXPERTS - 1:N_EXPERTS, 2]
    per_assign = lambda a: jnp.transpose(a[:, :TOP_K, :], (0, 2, 1)).reshape(a.shape[0], 1, TOK_TILE * TOP_K)
    lp_tiles = per_assign(lp)
    gate_tiles = per_assign(route)
    xrows = _dispatch(padrow, npad, nused, lp_tiles, tab, hn, nblk * MOE_BLOCK)
    yrows = _moe_blocks(block_e, nused, padrow, xrows,
                        w_gate_up[l], b_gate_up[l].reshape(N_EXPERTS, 1, 2 * D_FF),
                        w_down[l], b_down[l].reshape(N_EXPERTS, 1, D_MODEL))
    y_p, y_s = _combine(lp_tiles, gate_tiles, tab, h, yrows, TP)

    y_p = y_p.reshape(B, S, D_MODEL)
    y_s = y_s.reshape(DB, DS, D_MODEL)
    def seq_tail(rows, n):
        return jnp.stack([rows[(b + 1) * S - n:(b + 1) * S] for b in range(B)])

    kwin_p = seq_tail(kn, WINDOW).reshape(B, WINDOW, A_KV_HEADS, HEAD_DIM)
    vwin_p = seq_tail(va, WINDOW).reshape(B, WINDOW, A_KV_HEADS, HEAD_DIM)
    qkm_s = qkm_s.reshape(DB, DS, 2 * M_WIDTH)
    return (y_p, y_s,
            kwin_p[None], vwin_p[None], seq_tail(qkm, CONV_W - 1)[None],
            C_p[None], n_p[None], mm_p[:, 0, :M_HEADS][None],
            kwin_s.reshape(DB, WINDOW, A_KV_HEADS, HEAD_DIM)[None],
            vwin_s.reshape(DB, WINDOW, A_KV_HEADS, HEAD_DIM)[None],
            qkm_s[:, -(CONV_W - 1):][None],
            C_s[None], n_s[None], mm_s[:, 0, :M_HEADS][None])
```

```python
import functools

import jax
import jax.numpy as jnp
from jax import lax
from jax.experimental import pallas as pl
from jax.experimental.pallas import tpu as pltpu

F32 = jnp.float32
BF16 = jnp.bfloat16

D_MODEL = 1024
HEAD_DIM = 64
A_HEADS = 8
A_KV_HEADS = 2
A_GROUP = A_HEADS // A_KV_HEADS
A_WIDTH = A_HEADS * HEAD_DIM
KV_WIDTH = A_KV_HEADS * HEAD_DIM
WINDOW = 128
M_HEADS = 4
M_HEAD_DIM = 128
M_WIDTH = M_HEADS * M_HEAD_DIM
CONV_W = 4
N_EXPERTS = 32
TOP_K = 4
D_FF = D_MODEL
SWIGLU_LIMIT = 7.0
SWIGLU_ALPHA = 1.702
MOE_BLOCK = 512
MOE_QUARTER = 128
EPS = 1e-6
NEG = -1e30

LANES = 128
SUBLANES = 8
GATE_COL = A_WIDTH + 2 * KV_WIDTH + 4 * M_WIDTH
IN_PAD = GATE_COL + LANES
TOK_TILE = 512
PROJ_TILE = 1024
PROJ_VMEM_LIMIT = 58 * 1024 * 1024
GROUP_CHUNK = 256
ATT_QB = 1024
ATT_UNROLL = 2
ATT_SB = 128
SAMPLE_NB = 16
PROMPT_CHUNK = 256
VMEM_LIMIT = 48 * 1024 * 1024
MOE_VMEM_LIMIT = 56 * 1024 * 1024


def _dot(a, b):
    return jnp.dot(a, b, preferred_element_type=F32)


def _dot_nt(a, b):
    return lax.dot_general(a, b, (((1,), (1,)), ((), ())), preferred_element_type=F32)


def _dot_tn(a, b):
    return lax.dot_general(a, b, (((0,), (0,)), ((), ())), preferred_element_type=F32)


def _split3(x):
    hi = x.astype(BF16)
    r1 = x - hi.astype(F32)
    mid = r1.astype(BF16)
    lo = (r1 - mid.astype(F32)).astype(BF16)
    return hi, mid, lo


def _log_sigmoid(x):
    return jnp.minimum(x, 0.0) - jnp.log1p(jnp.exp(-jnp.abs(x)))


def _sigmoid(x):
    return 0.5 * jnp.tanh(0.5 * x) + 0.5


def _load_row_tiles(ref2, rows):
    return jnp.concatenate([ref2[pl.ds(s, rows, stride=SUBLANES), :] for s in range(SUBLANES)], axis=1)


def _store_row_tiles(ref2, rows, val):
    for s in range(SUBLANES):
        ref2[pl.ds(s, rows, stride=SUBLANES), :] = val[:, s * LANES:(s + 1) * LANES]


def _row_tiles(ref2, first, n):
    start = first * SUBLANES
    if not isinstance(start, int):
        start = pl.multiple_of(start, SUBLANES)
    return ref2.at[pl.ds(start, n * SUBLANES), :]


def _split_specs(n_prompt_tiles, rows, width):
    return (pl.BlockSpec((rows, width), lambda i, *_: (jnp.minimum(i, n_prompt_tiles - 1), 0)),
            pl.BlockSpec((rows, width), lambda i, *_: (jnp.maximum(i - n_prompt_tiles, 0), 0)))


def _inproj_kernel(x_ref, g_ref, w_ref, wgt_ref, gmat_ref, qg_ref, kg_ref,
                   qn_ref, kn_ref, va_ref, qkm_ref, vm_ref, om_ref, gcol_ref, gt_ref):
    x = x_ref[...]
    ms = jnp.mean(x * x, axis=-1, keepdims=True)
    xn = ((x * lax.rsqrt(ms + EPS)) * g_ref[...]).astype(BF16)

    def seg(lo, hi):
        return _dot(xn, w_ref[:, lo:hi])

    def head_norm(z, gmat, g):
        parts = _split3(z * z)
        w = min(GROUP_CHUNK, z.shape[1])
        ss = jnp.concatenate([sum(_dot(p[:, c:c + w], gmat) for p in parts) for c in range(0, z.shape[1], w)], axis=1)
        return (z * lax.rsqrt(ss * (1.0 / HEAD_DIM) + EPS)) * g

    o0 = A_WIDTH
    o1 = o0 + KV_WIDTH
    o2 = o1 + KV_WIDTH
    o3 = o2 + 2 * M_WIDTH
    o4 = o3 + M_WIDTH
    o5 = o4 + M_WIDTH
    qn_ref[...] = head_norm(seg(0, o0), gmat_ref[...], qg_ref[...])
    kn_ref[...] = head_norm(seg(o0, o1), gmat_ref[:KV_WIDTH, :KV_WIDTH], kg_ref[...])
    va_ref[...] = seg(o1, o2)
    qkm_ref[...] = seg(o2, o3)
    vm_ref[...] = seg(o3, o4)
    om_ref[...] = seg(o4, o5)
    gcol_ref[...] = seg(o5, o5 + LANES)
    gt_ref[0] = _dot_nt(wgt_ref[...], xn)


def _inproj(x, g_attn, w_pad, wgt, gmat, qg, kg):
    T = x.shape[0]
    nt = T // PROJ_TILE
    row = lambda w: pl.BlockSpec((PROJ_TILE, w), lambda i: (i, 0))
    full = lambda a: pl.BlockSpec(a.shape, lambda i: (0,) * a.ndim, pipeline_mode=pl.Buffered(1))
    out_shape = (
        jax.ShapeDtypeStruct((T, A_WIDTH), F32),
        jax.ShapeDtypeStruct((T, KV_WIDTH), F32),
        jax.ShapeDtypeStruct((T, KV_WIDTH), F32),
        jax.ShapeDtypeStruct((T, 2 * M_WIDTH), F32),
        jax.ShapeDtypeStruct((T, M_WIDTH), F32),
        jax.ShapeDtypeStruct((T, M_WIDTH), F32),
        jax.ShapeDtypeStruct((T, LANES), F32),
        jax.ShapeDtypeStruct((nt, SUBLANES, PROJ_TILE), F32),
    )
    out_specs = (row(A_WIDTH), row(KV_WIDTH), row(KV_WIDTH), row(2 * M_WIDTH), row(M_WIDTH), row(M_WIDTH),
                 row(LANES), pl.BlockSpec((1, SUBLANES, PROJ_TILE), lambda i: (i, 0, 0)))
    return pl.pallas_call(
        _inproj_kernel,
        grid=(nt,),
        in_specs=[row(D_MODEL), full(g_attn), full(w_pad), full(wgt), full(gmat), full(qg), full(kg)],
        out_specs=out_specs,
        out_shape=out_shape,
        compiler_params=pltpu.CompilerParams(dimension_semantics=("parallel",), vmem_limit_bytes=PROJ_VMEM_LIMIT),
        name="inproj",
    )(x, g_attn, w_pad, wgt, gmat, qg, kg)


def _softmax_sink(pieces, masks, sink_col):
    masked = [jnp.where(mk, s, NEG) for s, mk in zip(pieces, masks)]
    m = sink_col
    for s in masked:
        m = jnp.maximum(m, jnp.max(s, axis=-1, keepdims=True))
    ps = [jnp.exp(s - m) for s in masked]
    den = jnp.exp(sink_col - m)
    for p in ps:
        den = den + jnp.sum(p, axis=-1, keepdims=True)
    return ps, 1.0 / den


def _stack_heads(q, g):
    return jnp.concatenate([q[:, (A_GROUP * g + i) * HEAD_DIM:(A_GROUP * g + i + 1) * HEAD_DIM]
                            for i in range(A_GROUP)], axis=0)


def _sink_col(sink_ref, g, rows_per_head):
    r = lax.broadcasted_iota(jnp.int32, (A_GROUP * rows_per_head, 1), 0)
    col = jnp.zeros((A_GROUP * rows_per_head, 1), F32)
    for i in range(A_GROUP):
        col = jnp.where(r // rows_per_head == i, sink_ref[A_GROUP * g + i], col)
    return col


def _attn_prompt_kernel(sink_ref, q_ref, kp_ref, kc_ref, vp_ref, vc_ref, o_ref, kall, vall):
    j = pl.program_id(1)
    scale = HEAD_DIM ** -0.5
    kall[0:ATT_SB, :] = kp_ref[...].astype(BF16)
    kall[ATT_SB:, :] = kc_ref[...].astype(BF16)
    vall[0:ATT_SB, :] = vp_ref[...].astype(BF16)
    vall[ATT_SB:, :] = vc_ref[...].astype(BF16)
    nrow = A_GROUP * ATT_SB
    r = lax.broadcasted_iota(jnp.int32, (nrow, 2 * ATT_SB), 0) % ATT_SB
    c = lax.broadcasted_iota(jnp.int32, (nrow, 2 * ATT_SB), 1)
    band = jnp.logical_and(c >= r, c <= r + WINDOW)

    def pair(it, carry):
        for half in range(ATT_UNROLL):
            sb = it * ATT_UNROLL + half
            row0 = pl.multiple_of(sb * ATT_SB, ATT_SB)
            q = (q_ref[pl.ds(row0, ATT_SB), :] * scale).astype(BF16)
            kwin = kall[pl.ds(row0, 2 * ATT_SB), :]
            vwin = vall[pl.ds(row0, 2 * ATT_SB), :]
            mask = jnp.logical_and(band, jnp.logical_or(c >= ATT_SB, jnp.logical_or(j > 0, sb > 0)))
            outs = []
            for g in range(A_KV_HEADS):
                lo, hi = g * HEAD_DIM, (g + 1) * HEAD_DIM
                s = _dot_nt(_stack_heads(q, g), kwin[:, lo:hi])
                (p,), inv = _softmax_sink([s], [mask], _sink_col(sink_ref, g, ATT_SB))
                o = _dot(p.astype(BF16), vwin[:, lo:hi]) * inv
                outs += [o[i * ATT_SB:(i + 1) * ATT_SB] for i in range(A_GROUP)]
            o_ref[pl.ds(row0, ATT_SB), :] = jnp.concatenate(outs, axis=1)
        return carry

    lax.fori_loop(0, ATT_QB // (ATT_SB * ATT_UNROLL), pair, 0)


def _attn_prompt(sinks, qn, kn, va, batch, seq):
    nq = seq // ATT_QB
    ratio = ATT_QB // ATT_SB
    cur = lambda w: pl.BlockSpec((ATT_QB, w), lambda b, j: (b * nq + j, 0))
    prev = lambda w: pl.BlockSpec((ATT_SB, w), lambda b, j: (jnp.maximum((b * nq + j) * ratio - 1, 0), 0))
    return pl.pallas_call(
        _attn_prompt_kernel,
        grid=(batch, nq),
        in_specs=[pl.BlockSpec(memory_space=pltpu.SMEM), cur(A_WIDTH), prev(KV_WIDTH), cur(KV_WIDTH),
                  prev(KV_WIDTH), cur(KV_WIDTH)],
        out_specs=cur(A_WIDTH),
        out_shape=jax.ShapeDtypeStruct((batch * seq, A_WIDTH), F32),
        scratch_shapes=[pltpu.VMEM((ATT_QB + ATT_SB, KV_WIDTH), BF16), pltpu.VMEM((ATT_QB + ATT_SB, KV_WIDTH), BF16)],
        compiler_params=pltpu.CompilerParams(dimension_semantics=("parallel", "parallel"),
                                             vmem_limit_bytes=VMEM_LIMIT),
        name="attn_prompt",
    )(sinks, qn, kn, kn, va, va)


def _attn_sample_kernel(dec, sink_ref, q_ref, kn_ref, vn_ref, ck_ref, cv_ref, o_ref, kw_ref, vw_ref):
    scale = HEAD_DIM ** -0.5
    rows = SAMPLE_NB * dec
    knew = kn_ref[...]
    vnew = vn_ref[...]
    knew_b = knew.astype(BF16)
    vnew_b = vnew.astype(BF16)
    nrow = A_GROUP * dec
    t = lax.broadcasted_iota(jnp.int32, (nrow, WINDOW), 0) % dec
    c = lax.broadcasted_iota(jnp.int32, (nrow, WINDOW), 1)
    m_cache = c >= t
    cn = lax.broadcasted_iota(jnp.int32, (nrow, rows), 1)
    tn = lax.broadcasted_iota(jnp.int32, (nrow, rows), 0) % dec
    for i in range(SAMPLE_NB):
        q = (q_ref[i * dec:(i + 1) * dec, :] * scale).astype(BF16)
        ck = ck_ref[i].astype(BF16)
        cv = cv_ref[i].astype(BF16)
        m_new = jnp.logical_and(cn // dec == i, cn % dec <= tn)
        outs = []
        for g in range(A_KV_HEADS):
            lo, hi = g * HEAD_DIM, (g + 1) * HEAD_DIM
            qs = _stack_heads(q, g)
            s_c = _dot_nt(qs, ck[:, lo:hi])
            s_n = _dot_nt(qs, knew_b[:, lo:hi])
            (p_c, p_n), inv = _softmax_sink([s_c, s_n], [m_cache, m_new], _sink_col(sink_ref, g, dec))
            o = (_dot(p_c.astype(BF16), cv[:, lo:hi]) + _dot(p_n.astype(BF16), vnew_b[:, lo:hi])) * inv
            outs += [o[h * dec:(h + 1) * dec] for h in range(A_GROUP)]
        o_ref[i * dec:(i + 1) * dec, :] = jnp.concatenate(outs, axis=1)
        kw_ref[i, 0:WINDOW - dec, :] = ck_ref[i, dec:WINDOW, :]
        kw_ref[i, WINDOW - dec:WINDOW, :] = knew[i * dec:(i + 1) * dec]
        vw_ref[i, 0:WINDOW - dec, :] = cv_ref[i, dec:WINDOW, :]
        vw_ref[i, WINDOW - dec:WINDOW, :] = vnew[i * dec:(i + 1) * dec]


def _attn_sample(sinks, qn, kn, va, ck, cv, dbatch, dec):
    rows = SAMPLE_NB * dec
    tokrow = lambda w: pl.BlockSpec((rows, w), lambda i: (i, 0))
    cache = pl.BlockSpec((SAMPLE_NB, WINDOW, KV_WIDTH), lambda i: (i, 0, 0))
    return pl.pallas_call(
        functools.partial(_attn_sample_kernel, dec),
        grid=(dbatch // SAMPLE_NB,),
        in_specs=[pl.BlockSpec(memory_space=pltpu.SMEM), tokrow(A_WIDTH), tokrow(KV_WIDTH), tokrow(KV_WIDTH),
                  cache, cache],
        out_specs=(pl.BlockSpec((rows, A_WIDTH), lambda i: (i, 0)), cache, cache),
        out_shape=(jax.ShapeDtypeStruct((dbatch * dec, A_WIDTH), F32),
                   jax.ShapeDtypeStruct((dbatch, WINDOW, KV_WIDTH), F32),
                   jax.ShapeDtypeStruct((dbatch, WINDOW, KV_WIDTH), F32)),
        compiler_params=pltpu.CompilerParams(dimension_semantics=("parallel",), vmem_limit_bytes=VMEM_LIMIT),
        name="attn_sample",
    )(sinks, qn, kn, va, ck, cv)


def _mlstm_kernel(nseq, L, qk_ref, v_ref, o_ref, gcol_ref, gt_ref, conv0_ref, c0_ref, n0_ref, m0_ref,
                  cw_ref, cb_ref, gbrow_ref, gbcol_ref, mng_ref, mask_ref,
                  out_ref, cst_ref, nst_ref, mst_ref, prev_ref):
    R = nseq * L
    ci = pl.program_id(1)

    @pl.when(ci == 0)
    def _():
        cst_ref[...] = c0_ref[...]
        nst_ref[...] = n0_ref[...]
        mst_ref[...] = m0_ref[...]
        prev_ref[...] = conv0_ref[0]

    raw = qk_ref[...]
    row = lax.broadcasted_iota(jnp.int32, (R, 1), 0)
    tpos = row % L
    rseq = row // L
    acc = raw * cw_ref[CONV_W - 1:CONV_W, :] + cb_ref[...]
    if nseq == 1:
        prev8 = prev_ref[...]
        t8 = lax.broadcasted_iota(jnp.int32, (SUBLANES, 1), 0)
    else:
        prevsrc = conv0_ref[...].reshape(R, 2 * M_WIDTH)
    for k in range(1, CONV_W):
        rolled = pltpu.roll(raw, k, 0)
        if nseq == 1:
            head = jnp.where(t8 >= k, rolled[0:SUBLANES], pltpu.roll(prev8, k, 0))
            sh = jnp.concatenate([head, rolled[SUBLANES:]], axis=0)
        else:
            sh = jnp.where(tpos >= k, rolled, pltpu.roll(prevsrc, R - SUBLANES + k, 0))
        acc = acc + sh * cw_ref[CONV_W - 1 - k:CONV_W - k, :]
    if nseq == 1:
        prev_ref[...] = raw[R - SUBLANES:R]
    qkc = acc * _sigmoid(acc)

    gc = gcol_ref[...] + gbrow_ref[...]
    gr = gt_ref[0] + gbcol_ref[...]
    lsc = _log_sigmoid(gc)
    lsr = _log_sigmoid(gr)
    mb = mask_ref[...]
    maskb = mb > 0
    bcol = sum(_dot(mb, p) for p in _split3(lsc))
    brow = sum(_dot_nt(p, mb) for p in _split3(lsr))

    lane = lax.broadcasted_iota(jnp.int32, (1, LANES), 1)
    m_new = [jnp.zeros((1, LANES), F32) for _ in range(nseq)]
    for h in range(M_HEADS):
        sl = slice(h * M_HEAD_DIM, (h + 1) * M_HEAD_DIM)
        qh = qkc[:, sl]
        kh = qkc[:, M_WIDTH + h * M_HEAD_DIM:M_WIDTH + (h + 1) * M_HEAD_DIM] * (M_HEAD_DIM ** -0.5)
        vh = v_ref[:, sl]
        qb, kb, vb = qh.astype(BF16), kh.astype(BF16), vh.astype(BF16)
        ig_c = gc[:, h:h + 1]
        b_c = bcol[:, M_HEADS + h:M_HEADS + h + 1]
        ig_r = gr[h:h + 1, :]
        b_r = brow[M_HEADS + h:M_HEADS + h + 1, :]
        if nseq == 1:
            m0c = mst_ref[0][:, h:h + 1]
            n0rows = nst_ref[0, h:h + 1, :]
        else:
            m0c = jnp.zeros((R, 1), F32)
            n0rows = jnp.zeros((R, M_HEAD_DIM), F32)
            for s in range(nseq):
                m0c = jnp.where(rseq == s, mst_ref[s][:, h:h + 1], m0c)
                n0rows = jnp.where(rseq == s, nst_ref[s, h:h + 1, :], n0rows)
        dm = jnp.where(maskb, b_c - b_r + ig_r, NEG)
        a_c = b_c + m0c
        m_c = jnp.maximum(a_c, jnp.max(dm, axis=-1, keepdims=True))
        w = jnp.exp(dm - m_c)
        sc = jnp.exp(a_c - m_c)
        wqk = w * _dot_nt(qb, kb)
        if nseq == 1:
            inter = _dot_nt(qb, cst_ref[0, h].astype(BF16))
        else:
            inter = jnp.zeros((R, M_HEAD_DIM), F32)
            for s in range(nseq):
                qs = jnp.where(rseq == s, qh, 0.0).astype(BF16)
                inter = inter + _dot_nt(qs, cst_ref[s, h].astype(BF16))
        num = _dot(wqk.astype(BF16), vb) + sc * inter
        den = jnp.sum(wqk, axis=-1, keepdims=True) + sc * jnp.sum(qh * n0rows, axis=-1, keepdims=True)
        hh = num / jnp.maximum(jnp.abs(den), jnp.exp(-m_c))

        for s in range(nseq):
            e = s * L + L - 1
            m_end = m_c[e:e + 1, :]
            wend = jnp.exp(b_c[e:e + 1, :] - b_c + ig_c - m_end)
            if nseq > 1:
                wend = jnp.where(rseq == s, wend, 0.0)
            sce = jnp.exp(a_c[e:e + 1, :] - m_end)
            c_new = sce * cst_ref[s, h] + _dot_tn((vh * wend).astype(BF16), kb)
            n_new = sce * nst_ref[s, h:h + 1, :] + jnp.sum(wend * kh, axis=0, keepdims=True)
            cst_ref[s, h] = c_new
            nst_ref[s, h:h + 1, :] = n_new
            m_new[s] = jnp.where(lane == h, m_end, m_new[s])

        hn = (hh * lax.rsqrt(jnp.mean(hh * hh, axis=-1, keepdims=True) + EPS)) * mng_ref[:, sl]
        out_ref[:, sl] = _sigmoid(o_ref[:, sl]) * hn
    for s in range(nseq):
        mst_ref[s] = m_new[s]


def _mlstm(qkm, vm, om, gcol, gt, conv0, c0, n0, m0, cw, cb, gbrow, gbcol, mng, nseq, L, ngroups, nchunks):
    R = nseq * L
    per_tile = gt.shape[2] // R
    tok = lambda w: pl.BlockSpec((R, w), lambda g, c: (g * nchunks + c, 0))
    gt_spec = pl.BlockSpec((1, SUBLANES, R),
                           lambda g, c: ((g * nchunks + c) // per_tile, 0, (g * nchunks + c) % per_tile))
    full = lambda a: pl.BlockSpec(a.shape, lambda g, c: (0,) * a.ndim)
    st4 = pl.BlockSpec((nseq, M_HEADS, M_HEAD_DIM, M_HEAD_DIM), lambda g, c: (g, 0, 0, 0))
    st3 = pl.BlockSpec((nseq, M_HEADS, M_HEAD_DIM), lambda g, c: (g, 0, 0))
    stm = pl.BlockSpec((nseq, 1, LANES), lambda g, c: (g, 0, 0))
    conv_spec = pl.BlockSpec((nseq, SUBLANES, 2 * M_WIDTH), lambda g, c: (g, 0, 0))
    r = jnp.arange(R)
    mask = ((r[:, None] // L == r[None, :] // L) & (r[None, :] <= r[:, None])).astype(BF16)
    nstate = ngroups * nseq
    return pl.pallas_call(
        functools.partial(_mlstm_kernel, nseq, L),
        grid=(ngroups, nchunks),
        in_specs=[tok(2 * M_WIDTH), tok(M_WIDTH), tok(M_WIDTH), tok(LANES), gt_spec, conv_spec, st4, st3, stm,
                  full(cw), full(cb), full(gbrow), full(gbcol), full(mng), full(mask)],
        out_specs=(pl.BlockSpec((R, M_WIDTH), lambda g, c: (g * nchunks + c, 0)), st4, st3, stm),
        out_shape=(jax.ShapeDtypeStruct((ngroups * nchunks * R, M_WIDTH), F32),
                   jax.ShapeDtypeStruct((nstate, M_HEADS, M_HEAD_DIM, M_HEAD_DIM), F32),
                   jax.ShapeDtypeStruct((nstate, M_HEADS, M_HEAD_DIM), F32),
                   jax.ShapeDtypeStruct((nstate, 1, LANES), F32)),
        scratch_shapes=[pltpu.VMEM((SUBLANES, 2 * M_WIDTH), F32)],
        compiler_params=pltpu.CompilerParams(dimension_semantics=("parallel", "arbitrary"),
                                             vmem_limit_bytes=VMEM_LIMIT),
        name="mlstm_n%d" % nseq,
    )(qkm, vm, om, gcol, gt, conv0, c0, n0, m0, cw, cb, gbrow, gbcol, mng, mask)


def _outproj_kernel(npt, ap_ref, as_ref, mp_ref, ms_ref, xp_ref, xs_ref, wo_ref, g_ref, wrt_ref, brc_ref,
                    h_ref, hn_ref, route_ref, cnt_ref):
    def project(a_ref, m_ref, x_ref):
        h_ref[...] = (x_ref[...] + _dot(a_ref[...].astype(BF16), wo_ref[0:A_WIDTH, :])
                      + _dot(m_ref[...].astype(BF16), wo_ref[A_WIDTH:A_WIDTH + M_WIDTH, :]))

    @pl.when(pl.program_id(0) < npt)
    def _():
        project(ap_ref, mp_ref, xp_ref)

    @pl.when(pl.program_id(0) >= npt)
    def _():
        project(as_ref, ms_ref, xs_ref)

    h = h_ref[...]
    hn = (h * lax.rsqrt(jnp.mean(h * h, axis=-1, keepdims=True) + EPS)) * g_ref[...]
    _store_row_tiles(hn_ref, TOK_TILE, hn)
    logits = _dot_nt(wrt_ref[...], hn.astype(BF16)) + brc_ref[...]
    eidx = lax.broadcasted_iota(jnp.int32, logits.shape, 0).astype(F32)
    picked = jnp.zeros(logits.shape, F32)
    top0 = None
    den = None
    es = []
    ids = []
    for k in range(TOP_K):
        mx = jnp.max(logits, axis=0, keepdims=True)
        idx = jnp.min(jnp.where(logits == mx, eidx, float(N_EXPERTS)), axis=0, keepdims=True)
        if k == 0:
            top0 = mx
        e = jnp.exp(mx - top0)
        den = e if den is None else den + e
        es.append(e)
        ids.append(idx)
        hit = eidx == idx
        picked = jnp.where(hit, 1.0, picked)
        logits = jnp.where(hit, -jnp.inf, logits)
    route_ref[0] = jnp.concatenate([e / den for e in es] + ids, axis=0)

    @pl.when(pl.program_id(0) == 0)
    def _():
        cnt_ref[...] = jnp.zeros_like(cnt_ref)

    cnt_ref[...] += jnp.broadcast_to(jnp.sum(picked, axis=1, keepdims=True), cnt_ref.shape)


def _outproj(a_p, a_s, m_p, m_s, xp, xs, w_out, g_ffn, wr_t, br_col):
    T = xp.shape[0] + xs.shape[0]
    nt = T // TOK_TILE
    npt = xp.shape[0] // TOK_TILE
    row = lambda w: pl.BlockSpec((TOK_TILE, w), lambda i: (i, 0))
    full = lambda a: pl.BlockSpec(a.shape, lambda i: (0,) * a.ndim)
    return pl.pallas_call(
        functools.partial(_outproj_kernel, npt),
        grid=(nt,),
        in_specs=[*_split_specs(npt, TOK_TILE, A_WIDTH), *_split_specs(npt, TOK_TILE, M_WIDTH),
                  *_split_specs(npt, TOK_TILE, D_MODEL), full(w_out), full(g_ffn), full(wr_t), full(br_col)],
        out_specs=(row(D_MODEL), pl.BlockSpec((TOK_TILE * SUBLANES, LANES), lambda i: (i, 0)),
                   pl.BlockSpec((1, 2 * TOP_K, TOK_TILE), lambda i: (i, 0, 0)),
                   pl.BlockSpec((N_EXPERTS, LANES), lambda i: (0, 0))),
        out_shape=(jax.ShapeDtypeStruct((T, D_MODEL), F32), jax.ShapeDtypeStruct((T * SUBLANES, LANES), F32),
                   jax.ShapeDtypeStruct((nt, 2 * TOP_K, TOK_TILE), F32),
                   jax.ShapeDtypeStruct((N_EXPERTS, LANES), F32)),
        compiler_params=pltpu.CompilerParams(dimension_semantics=("arbitrary",), vmem_limit_bytes=VMEM_LIMIT),
        name="outproj_router",
    )(a_p, a_s, m_p, m_s, xp, xs, w_out, g_ffn, wr_t, br_col)


def _route_kernel(nblk_pad, route_ref, cnt_ref, ustrict_ref, lstrict_ref, lp_ref, tab_ref, blk_ref, info_ref,
                  carry_ref):
    i = pl.program_id(0)
    cnt = cnt_ref[...]
    nb_e = jnp.floor((cnt + (MOE_BLOCK - 1.0)) * (1.0 / MOE_BLOCK))
    bstart = sum(_dot(lstrict_ref[...], p) for p in _split3(nb_e))
    bend = bstart + nb_e
    row_start = bstart * float(MOE_BLOCK)

    @pl.when(i == 0)
    def _():
        carry_ref[...] = jnp.zeros_like(carry_ref)
        bi = lax.broadcasted_iota(jnp.int32, (N_EXPERTS, nblk_pad), 1).astype(F32)
        done = jnp.where(bend[:, 0:1] <= bi, 1.0, 0.0)
        be = jnp.minimum(jnp.sum(done, axis=0, keepdims=True), N_EXPERTS - 1.0)
        blk_ref[...] = jnp.broadcast_to(be, blk_ref.shape).astype(jnp.int32)
        lane = lax.broadcasted_iota(jnp.int32, (N_EXPERTS, LANES), 1)
        info = jnp.where(lane == 0, row_start + cnt, 0.0)
        info = jnp.where(lane == 1, nb_e * float(MOE_BLOCK) - cnt, info)
        info = jnp.where(lane == 2, bend, info)
        info_ref[...] = info.astype(jnp.int32)

    r = route_ref[0]
    eidx = lax.broadcasted_iota(jnp.int32, (N_EXPERTS, TOK_TILE), 0).astype(F32)
    sel = [eidx == r[TOP_K + k:TOP_K + k + 1, :] for k in range(TOP_K)]
    oh = jnp.zeros((N_EXPERTS, TOK_TILE), F32)
    for k in range(TOP_K):
        oh = jnp.where(sel[k], 1.0, oh)
    cnt_t = jnp.broadcast_to(jnp.sum(oh, axis=1, keepdims=True), (N_EXPERTS, LANES))
    seg = sum(_dot(lstrict_ref[...], p) for p in _split3(cnt_t))
    local = _dot(oh.astype(BF16), ustrict_ref[...]) + seg[:, 0:1]
    rows = [jnp.sum(jnp.where(sel[k], local, 0.0), axis=0, keepdims=True) for k in range(TOP_K)]
    lp_ref[0] = (jnp.concatenate(rows + [jnp.zeros((TOP_K, TOK_TILE), F32)], axis=0)
                 * float(SUBLANES)).astype(jnp.int32)
    diag = (lax.broadcasted_iota(jnp.int32, (N_EXPERTS, LANES), 0)
            == lax.broadcasted_iota(jnp.int32, (N_EXPERTS, LANES), 1))
    to_lanes = lambda col: jnp.sum(jnp.where(diag, col, 0.0), axis=0, keepdims=True)
    tab = [to_lanes(seg), to_lanes(cnt_t), to_lanes(row_start + carry_ref[...])]
    tab_ref[0] = jnp.concatenate(tab + [jnp.zeros((SUBLANES - len(tab), LANES), F32)], axis=0).astype(jnp.int32)
    carry_ref[...] += cnt_t


def _route_tables(route, cnt, nblk):
    nt = route.shape[0]
    nblk_pad = -(-nblk // LANES) * LANES
    a = jnp.arange(TOK_TILE)
    ustrict = (a[:, None] < a[None, :]).astype(BF16)
    b = jnp.arange(N_EXPERTS)
    lstrict = (b[:, None] > b[None, :]).astype(BF16)
    full = lambda x: pl.BlockSpec(x.shape, lambda i: (0,) * x.ndim)
    tile = pl.BlockSpec((1, 2 * TOP_K, TOK_TILE), lambda i: (i, 0, 0))
    return pl.pallas_call(
        functools.partial(_route_kernel, nblk_pad),
        grid=(nt,),
        in_specs=[tile, full(cnt), full(ustrict), full(lstrict)],
        out_specs=(tile, pl.BlockSpec((1, SUBLANES, LANES), lambda i: (i, 0, 0)),
                   pl.BlockSpec((SUBLANES, nblk_pad), lambda i: (0, 0)),
                   pl.BlockSpec((N_EXPERTS, LANES), lambda i: (0, 0))),
        out_shape=(jax.ShapeDtypeStruct((nt, 2 * TOP_K, TOK_TILE), jnp.int32),
                   jax.ShapeDtypeStruct((nt, SUBLANES, LANES), jnp.int32),
                   jax.ShapeDtypeStruct((SUBLANES, nblk_pad), jnp.int32),
                   jax.ShapeDtypeStruct((N_EXPERTS, LANES), jnp.int32)),
        scratch_shapes=[pltpu.VMEM((N_EXPERTS, LANES), F32)],
        compiler_params=pltpu.CompilerParams(dimension_semantics=("arbitrary",), vmem_limit_bytes=VMEM_LIMIT),
        name="route_tables",
    )(route, cnt, ustrict, lstrict)


RUN_PIECES = tuple(1 << b for b in range(9, -1, -1))


def _for_run_pieces(count, fn):
    for size in RUN_PIECES:
        @pl.when((count & size) != 0)
        def _(size=size):
            fn(count & ~(2 * size - 1), size)


def _dispatch_kernel(padrow_ref, npad_ref, nu_ref, lp_ref, tab_ref, hn_ref, xs_hbm, srt, zbuf, sem, zsem):
    i = pl.program_id(0)
    nt = pl.num_programs(0)
    slot = i % 2

    @pl.when(i == 0)
    def _():
        zbuf[...] = jnp.zeros_like(zbuf)
        nblk = xs_hbm.shape[0] // (MOE_BLOCK * SUBLANES)

        def tail_start(b, c):
            pltpu.make_async_copy(zbuf, _row_tiles(xs_hbm, b * MOE_BLOCK, MOE_BLOCK), zsem).start()
            return c

        def tail_wait(b, c):
            pltpu.make_async_copy(zbuf, _row_tiles(xs_hbm, b * MOE_BLOCK, MOE_BLOCK), zsem).wait()
            return c

        lax.fori_loop(nu_ref[0], nblk, tail_start, 0)
        lax.fori_loop(nu_ref[0], nblk, tail_wait, 0)

        def pad_runs(wait):
            def per_expert(e, carry):
                first, count = padrow_ref[e], npad_ref[e]

                def piece(off, size):
                    cp = pltpu.make_async_copy(_row_tiles(zbuf, 0, size), _row_tiles(xs_hbm, first + off, size), zsem)
                    cp.wait() if wait else cp.start()
                _for_run_pieces(count, piece)
                return carry
            lax.fori_loop(0, N_EXPERTS, per_expert, 0)

        pad_runs(False)
        pad_runs(True)

    def wait_runs(s):
        for _ in range(TOP_K):
            pltpu.make_async_copy(_row_tiles(srt.at[s], 0, TOK_TILE), _row_tiles(xs_hbm, 0, TOK_TILE), sem.at[s]).wait()

    def step(s):
        @pl.when(i >= 2)
        def _():
            wait_runs(s)

        def permute(j, carry):
            for u in range(SUBLANES):
                t = j * SUBLANES + u
                row = hn_ref[pl.ds(pl.multiple_of(t * SUBLANES, SUBLANES), SUBLANES), :]
                for k in range(TOP_K):
                    p = lp_ref[0, 0, t * TOP_K + k]
                    srt[s, pl.ds(pl.multiple_of(p, SUBLANES), SUBLANES), :] = row
            return carry

        lax.fori_loop(0, TOK_TILE // SUBLANES, permute, 0)

        def send_run(e, carry):
            local, count, first = tab_ref[0, 0, e], tab_ref[0, 1, e], tab_ref[0, 2, e]

            def piece(off, size):
                pltpu.make_async_copy(_row_tiles(srt.at[s], local + off, size),
                                      _row_tiles(xs_hbm, first + off, size), sem.at[s]).start()
            _for_run_pieces(count, piece)
            return carry

        lax.fori_loop(0, N_EXPERTS, send_run, 0)

        @pl.when(i == nt - 1)
        def _():
            @pl.when(nt >= 2)
            def _():
                wait_runs(1 - s)

            wait_runs(s)

    for s in range(2):
        pl.when(slot == s)(functools.partial(step, s))


def _dispatch(padrow, npad, nused, lp_tiles, tab, hn, n_rows):
    nt = lp_tiles.shape[0]
    grid_spec = pltpu.PrefetchScalarGridSpec(
        num_scalar_prefetch=3,
        grid=(nt,),
        in_specs=[pl.BlockSpec((1, 1, TOK_TILE * TOP_K), lambda i, *_: (i, 0, 0), memory_space=pltpu.SMEM),
                  pl.BlockSpec((1, SUBLANES, LANES), lambda i, *_: (i, 0, 0), memory_space=pltpu.SMEM),
                  pl.BlockSpec((TOK_TILE * SUBLANES, LANES), lambda i, *_: (i, 0))],
        out_specs=pl.BlockSpec(memory_space=pl.ANY),
        scratch_shapes=[pltpu.VMEM((2, TOK_TILE * TOP_K * SUBLANES, LANES), F32),
                        pltpu.VMEM((MOE_BLOCK * SUBLANES, LANES), F32),
                        pltpu.SemaphoreType.DMA((2,)), pltpu.SemaphoreType.DMA(())],
    )
    return pl.pallas_call(
        _dispatch_kernel,
        grid_spec=grid_spec,
        out_shape=jax.ShapeDtypeStruct((n_rows * SUBLANES, LANES), F32),
        compiler_params=pltpu.CompilerParams(dimension_semantics=("arbitrary",), vmem_limit_bytes=VMEM_LIMIT),
        name="moe_dispatch",
    )(padrow, npad, nused, lp_tiles, tab, hn)


def _moe_kernel(be_ref, nu_ref, first_ref, slot_ref, nxt_ref, nvalid_ref, x_ref, bgu_ref, bd_ref, wgu_hbm, wd_hbm, y_ref,
                wgu_buf, wd_buf, wsem):
    i = pl.program_id(0)
    used = i < nu_ref[0]
    s = slot_ref[i]

    def fetch(e, sl):
        return (pltpu.make_async_copy(wgu_hbm.at[e], wgu_buf.at[sl], wsem.at[0, sl]),
                pltpu.make_async_copy(wd_hbm.at[e], wd_buf.at[sl], wsem.at[1, sl]))

    @pl.when(jnp.logical_and(used, first_ref[i] == 1))
    def _():
        @pl.when(i == 0)
        def _():
            for c in fetch(be_ref[0], 0):
                c.start()

        for c in fetch(be_ref[i], s):
            c.wait()

        @pl.when(nxt_ref[i] >= 0)
        def _():
            for c in fetch(nxt_ref[i], 1 - s):
                c.start()

    def expert(rows):
        x = _load_row_tiles(x_ref, rows).astype(BF16)
        hb = _dot(x, wgu_buf[s].astype(BF16)) + bgu_ref[0]
        glu = jnp.minimum(hb[:, :D_FF], SWIGLU_LIMIT)
        lin = jnp.clip(hb[:, D_FF:], -SWIGLU_LIMIT, SWIGLU_LIMIT)
        act = glu * _sigmoid(SWIGLU_ALPHA * glu) * (lin + 1.0)
        _store_row_tiles(y_ref, rows, _dot(act.astype(BF16), wd_buf[s].astype(BF16)) + bd_ref[0])
        if rows < MOE_BLOCK:
            y_ref[rows * SUBLANES:, :] = jnp.zeros(((MOE_BLOCK - rows) * SUBLANES, LANES), F32)

    nv = nvalid_ref[i]
    for rows in range(MOE_QUARTER, MOE_BLOCK + 1, MOE_QUARTER):
        lo = rows - MOE_QUARTER if rows > MOE_QUARTER else -1
        pl.when(jnp.logical_and(used, jnp.logical_and(nv > lo, nv <= rows)))(functools.partial(expert, rows))

    @pl.when(i >= nu_ref[0])
    def _():
        y_ref[...] = jnp.zeros_like(y_ref)


def _moe_blocks(block_e, nused, padrow, xs, wgu, bgu, wd, bd):
    nblk = block_e.shape[0]
    pad_at = jnp.sum(jnp.where(block_e[:, None] == jnp.arange(N_EXPERTS)[None, :], padrow[None, :], 0), axis=1)
    nvalid = jnp.clip(pad_at - jnp.arange(nblk, dtype=jnp.int32) * MOE_BLOCK, 0, MOE_BLOCK).astype(jnp.int32)
    idx = jnp.arange(nblk, dtype=jnp.int32)
    first = (idx < nused[0]) & ((idx == 0) | (block_e != jnp.roll(block_e, 1)))
    slot = ((jnp.cumsum(first.astype(jnp.int32)) - 1) % 2).astype(jnp.int32)
    first_pos = jnp.where(first, idx, nblk)
    later = jnp.concatenate([first_pos[1:], jnp.full((1,), nblk, jnp.int32)])
    next_pos = lax.cummin(later, reverse=True)
    nxt = jnp.sum(jnp.where(idx[None, :] == next_pos[:, None], block_e[None, :] + 1, 0), axis=1) - 1
    grid_spec = pltpu.PrefetchScalarGridSpec(
        num_scalar_prefetch=6,
        grid=(nblk,),
        in_specs=[
            pl.BlockSpec((MOE_BLOCK * SUBLANES, LANES), lambda i, be, nu, *_: (jnp.minimum(i, jnp.maximum(nu[0] - 1, 0)), 0)),
            pl.BlockSpec((1, 1, 2 * D_FF), lambda i, be, *_: (be[i], 0, 0)),
            pl.BlockSpec((1, 1, D_MODEL), lambda i, be, *_: (be[i], 0, 0)),
            pl.BlockSpec(memory_space=pl.ANY),
            pl.BlockSpec(memory_space=pl.ANY),
        ],
        out_specs=pl.BlockSpec((MOE_BLOCK * SUBLANES, LANES), lambda i, be, *_: (i, 0)),
        scratch_shapes=[pltpu.VMEM((2, D_MODEL, 2 * D_FF), F32), pltpu.VMEM((2, D_FF, D_MODEL), F32),
                        pltpu.SemaphoreType.DMA((2, 2))],
    )
    return pl.pallas_call(
        _moe_kernel,
        grid_spec=grid_spec,
        out_shape=jax.ShapeDtypeStruct(xs.shape, F32),
        compiler_params=pltpu.CompilerParams(dimension_semantics=("arbitrary",), vmem_limit_bytes=MOE_VMEM_LIMIT),
        name="moe_blocks",
    )(block_e, nused, first.astype(jnp.int32), slot, nxt.astype(jnp.int32), nvalid, xs, bgu, bd, wgu, wd)


def _combine_kernel(npt, lp_ref, gate_ref, tabc_ref, tabn_ref, h_ref, ys_hbm, yp_ref, ysm_ref, srt, acc, sem):
    i = pl.program_id(0)
    nt = pl.num_programs(0)
    slot = i % 2

    def fetch_runs(tab_ref, s):
        def run(e, carry):
            local, count, first = tab_ref[0, 0, e], tab_ref[0, 1, e], tab_ref[0, 2, e]

            def piece(off, size):
                pltpu.make_async_copy(_row_tiles(ys_hbm, first + off, size),
                                      _row_tiles(srt.at[s], local + off, size), sem.at[s]).start()
            _for_run_pieces(count, piece)
            return carry
        lax.fori_loop(0, N_EXPERTS, run, 0)

    @pl.when(i == 0)
    def _():
        fetch_runs(tabc_ref, 0)

    def step(s):
        @pl.when(i + 1 < nt)
        def _():
            fetch_runs(tabn_ref, 1 - s)

        for _ in range(TOP_K):
            pltpu.make_async_copy(_row_tiles(ys_hbm, 0, TOK_TILE), _row_tiles(srt.at[s], 0, TOK_TILE),
                                  sem.at[s]).wait()

        def gather(j, carry):
            for u in range(SUBLANES):
                t = j * SUBLANES + u
                tot = None
                for k in range(TOP_K):
                    p = lp_ref[0, 0, t * TOP_K + k]
                    term = (srt[s, pl.ds(pl.multiple_of(p, SUBLANES), SUBLANES), :]
                            * gate_ref[0, 0, t * TOP_K + k])
                    tot = term if tot is None else tot + term
                acc[pl.ds(pl.multiple_of(t * SUBLANES, SUBLANES), SUBLANES), :] = tot
            return carry

        lax.fori_loop(0, TOK_TILE // SUBLANES, gather, 0)

    for s in range(2):
        pl.when(slot == s)(functools.partial(step, s))
    y = h_ref[...] + _load_row_tiles(acc, TOK_TILE)

    @pl.when(i < npt)
    def _():
        yp_ref[...] = y

    @pl.when(i >= npt)
    def _():
        ysm_ref[...] = y


def _combine(lp_tiles, gate_tiles, tab, h, ys, n_prompt_rows):
    T = h.shape[0]
    nt = T // TOK_TILE
    npt = n_prompt_rows // TOK_TILE
    per_assign = pl.BlockSpec((1, 1, TOK_TILE * TOP_K), lambda i: (i, 0, 0), memory_space=pltpu.SMEM)
    tab_blk = lambda imap: pl.BlockSpec((1, SUBLANES, LANES), imap, memory_space=pltpu.SMEM)
    return pl.pallas_call(
        functools.partial(_combine_kernel, npt),
        grid=(nt,),
        in_specs=[per_assign, per_assign,
                  tab_blk(lambda i: (i, 0, 0)), tab_blk(lambda i: (jnp.minimum(i + 1, nt - 1), 0, 0)),
                  pl.BlockSpec((TOK_TILE, D_MODEL), lambda i: (i, 0)),
                  pl.BlockSpec(memory_space=pl.ANY)],
        out_specs=(pl.BlockSpec((TOK_TILE, D_MODEL), lambda i: (jnp.minimum(i, npt - 1), 0)),
                   pl.BlockSpec((TOK_TILE, D_MODEL), lambda i: (jnp.maximum(i - npt, 0), 0))),
        out_shape=(jax.ShapeDtypeStruct((n_prompt_rows, D_MODEL), F32),
                   jax.ShapeDtypeStruct((T - n_prompt_rows, D_MODEL), F32)),
        scratch_shapes=[pltpu.VMEM((2, TOK_TILE * TOP_K * SUBLANES, LANES), F32),
                        pltpu.VMEM((TOK_TILE * SUBLANES, LANES), F32), pltpu.SemaphoreType.DMA((2,))],
        compiler_params=pltpu.CompilerParams(dimension_semantics=("arbitrary",), vmem_limit_bytes=VMEM_LIMIT),
        name="moe_combine",
    )(lp_tiles, gate_tiles, tab, tab, h, ys)


def kernel(x_prompt, x_sample, cache_k_win, cache_v_win, state_conv, state_C, state_n, state_m, g_attn, w_in, b_i,
           b_f, q_norm_g, k_norm_g, sinks, conv_w, conv_b, m_norm_g, w_out, g_ffn, w_router, b_router, w_gate_up,
           b_gate_up, w_down, b_down):
    depth = g_attn.shape[0]
    assert depth == 1
    B, S, _ = x_prompt.shape
    DB, DS, _ = x_sample.shape
    TP = B * S
    TS = DB * DS
    T = TP + TS
    assert T % TOK_TILE == 0 and TP % TOK_TILE == 0 and S % ATT_QB == 0 and S % PROMPT_CHUNK == 0
    assert DS == SUBLANES and DB % SAMPLE_NB == 0 and (SAMPLE_NB * DS) == LANES
    l = 0

    xp = x_prompt.reshape(TP, D_MODEL)
    xs = x_sample.reshape(TS, D_MODEL)

    w_pad = jnp.pad(w_in[l], ((0, 0), (0, IN_PAD - w_in.shape[2]))).astype(BF16)
    wgt = jnp.transpose(w_in[l][:, GATE_COL:GATE_COL + 2 * M_HEADS]).astype(BF16)
    gi = jnp.arange(GROUP_CHUNK) // HEAD_DIM
    gmat = (gi[:, None] == gi[None, :]).astype(BF16)
    qg = jnp.tile(q_norm_g[l], A_HEADS).reshape(1, A_WIDTH)
    kg = jnp.tile(k_norm_g[l], A_KV_HEADS).reshape(1, KV_WIDTH)
    gbias = jnp.concatenate([b_i[l], b_f[l]])
    gbrow = jnp.pad(gbias, (0, LANES - 2 * M_HEADS)).reshape(1, LANES)
    gbcol = gbias.reshape(2 * M_HEADS, 1)
    mng = m_norm_g[l].reshape(1, M_WIDTH)
    cw = conv_w[l]
    cb = conv_b[l].reshape(1, 2 * M_WIDTH)

    proj_w = (g_attn[l].reshape(1, D_MODEL), w_pad, wgt, gmat, qg, kg)
    qn, kn, va, qkm, vm, om, gcol, gt = _inproj(xp, *proj_w)
    qn_s, kn_s, va_s, qkm_s, vm_s, om_s, gcol_s, gt_s = _inproj(xs, *proj_w)

    a_p = _attn_prompt(sinks[l], qn, kn, va, B, S)
    ck = cache_k_win[l].reshape(DB, WINDOW, KV_WIDTH)
    cv = cache_v_win[l].reshape(DB, WINDOW, KV_WIDTH)
    a_s, kwin_s, vwin_s = _attn_sample(sinks[l], qn_s, kn_s, va_s, ck, cv, DB, DS)

    zc = jnp.zeros((B, SUBLANES, 2 * M_WIDTH), F32)
    m_p, C_p, n_p, mm_p = _mlstm(
        qkm, vm, om, gcol, gt, zc,
        jnp.zeros((B, M_HEADS, M_HEAD_DIM, M_HEAD_DIM), F32), jnp.zeros((B, M_HEADS, M_HEAD_DIM), F32),
        jnp.full((B, 1, LANES), NEG, F32), cw, cb, gbrow, gbcol, mng,
        nseq=1, L=PROMPT_CHUNK, ngroups=B, nchunks=S // PROMPT_CHUNK)
    conv_s0 = jnp.pad(state_conv[l], ((0, 0), (SUBLANES - (CONV_W - 1), 0), (0, 0)))
    m0_s = jnp.pad(state_m[l], ((0, 0), (0, LANES - M_HEADS))).reshape(DB, 1, LANES)
    m_s, C_s, n_s, mm_s = _mlstm(
        qkm_s, vm_s, om_s, gcol_s, gt_s, conv_s0, state_C[l], state_n[l], m0_s, cw, cb, gbrow, gbcol, mng,
        nseq=SAMPLE_NB, L=DS, ngroups=DB // SAMPLE_NB, nchunks=1)

    h, hn, route, cnt = _outproj(a_p, a_s, m_p, m_s, xp, xs, w_out[l].astype(BF16), g_ffn[l].reshape(1, D_MODEL),
                                 jnp.transpose(w_router[l]).astype(BF16), b_router[l].reshape(N_EXPERTS, 1))

    nblk = T * TOP_K // MOE_BLOCK + N_EXPERTS
    lp, tab, blk, info = _route_tables(route, cnt, nblk)
    block_e = blk[0, :nblk]
    padrow = info[:, 0]
    npad = info[:, 1]
    nused = info[N_EXPERTS - 1:N_EXPERTS, 2]
    per_assign = lambda a: jnp.transpose(a[:, :TOP_K, :], (0, 2, 1)).reshape(a.shape[0], 1, TOK_TILE * TOP_K)
    lp_tiles = per_assign(lp)
    gate_tiles = per_assign(route)
    xrows = _dispatch(padrow, npad, nused, lp_tiles, tab, hn, nblk * MOE_BLOCK)
    yrows = _moe_blocks(block_e, nused, padrow, xrows,
                        w_gate_up[l], b_gate_up[l].reshape(N_EXPERTS, 1, 2 * D_FF),
                        w_down[l], b_down[l].reshape(N_EXPERTS, 1, D_MODEL))
    y_p, y_s = _combine(lp_tiles, gate_tiles, tab, h, yrows, TP)

    y_p = y_p.reshape(B, S, D_MODEL)
    y_s = y_s.reshape(DB, DS, D_MODEL)
    def seq_tail(rows, n):
        return jnp.stack([rows[(b + 1) * S - n:(b + 1) * S] for b in range(B)])

    kwin_p = seq_tail(kn, WINDOW).reshape(B, WINDOW, A_KV_HEADS, HEAD_DIM)
    vwin_p = seq_tail(va, WINDOW).reshape(B, WINDOW, A_KV_HEADS, HEAD_DIM)
    qkm_s = qkm_s.reshape(DB, DS, 2 * M_WIDTH)
    return (y_p, y_s,
            kwin_p[None], vwin_p[None], seq_tail(qkm, CONV_W - 1)[None],
            C_p[None], n_p[None], mm_p[:, 0, :M_HEADS][None],
            kwin_s.reshape(DB, WINDOW, A_KV_HEADS, HEAD_DIM)[None],
            vwin_s.reshape(DB, WINDOW, A_KV_HEADS, HEAD_DIM)[None],
            qkm_s[:, -(CONV_W - 1):][None],
            C_s[None], n_s[None], mm_s[:, 0, :M_HEADS][None])
```
